```python
import math
import jax, jax.numpy as jnp
from jax import lax
import numpy as np

D_MODEL = 1024
BATCH = 16
SEQ = 2048
DEPTH = 2

GRID_W = 64
N_BRANCH = 4
BRANCH_W = 512
EPS = 1e-6

DN_HEADS = 4
DN_DK = 128
DN_DV = 128
DN_CONV = 5
DN_CHUNK = 64
NA_HEADS = 8
NA_DH = 64
NA_ROWS = 8
NA_COLS = 16
GA_HEADS = 4
GA_KV_HEADS = 2
GA_DH = 128
GA_BLOCK = 128
ROPE_THETA = 10000.0
ML_HEADS = 4
ML_DK = 64
ML_DV = 128
ML_CHUNK = 64

PROJ_SIZES = (
    DN_HEADS * (2 * DN_DK + DN_DV),
    2 * DN_HEADS,
    2 * DN_HEADS,
    BRANCH_W,
    3 * NA_HEADS * NA_DH,
    BRANCH_W,
    GA_HEADS * GA_DH,
    GA_KV_HEADS * GA_DH,
    GA_KV_HEADS * GA_DH,
    BRANCH_W,
    ML_HEADS * ML_DK,
    ML_HEADS * ML_DK,
    ML_HEADS * ML_DV,
    2 * ML_HEADS,
    2 * ML_HEADS,
    ML_HEADS * ML_DV,
    BRANCH_W,
    N_BRANCH * D_MODEL,
)

kernel_name = "hybrid_gated_parallel_encoder"

F32 = jnp.float32


def _rms(x, g):
    xf = x.astype(F32)
    y = xf * lax.rsqrt(jnp.mean(xf * xf, axis=-1, keepdims=True) + EPS)
    return (y * g.astype(F32)).astype(x.dtype)


def _l2n(x):
    return x * lax.rsqrt(jnp.sum(x * x, axis=-1, keepdims=True) + EPS)


def _heads(x, h):
    B, S, _ = x.shape
    return x.reshape(B, S, h, -1).transpose(0, 2, 1, 3)


def _merge_heads(x):
    B, H, S, d = x.shape
    return x.transpose(0, 2, 1, 3).reshape(B, S, H * d)


def _split(p, sizes):
    out, o = [], 0
    for s in sizes:
        out.append(p[..., o:o + s])
        o += s
    return out


def _centred_conv(x, w):
    K = w.shape[0]
    S = x.shape[1]
    pad = K // 2
    xp = jnp.pad(x, ((0, 0), (pad, pad), (0, 0)))
    out = xp[:, 0:S] * w[0]
    for j in range(1, K):
        out = out + xp[:, j:j + S] * w[j]
    return out


def _gated_delta(q, k, v, g, beta):
    B, H, S, dk = q.shape
    dv = v.shape[-1]
    C = DN_CHUNK
    N = S // C
    q = q.reshape(B, H, N, C, dk)
    k = k.reshape(B, H, N, C, dk)
    v = v.reshape(B, H, N, C, dv)
    g = g.reshape(B, H, N, C)
    beta = beta.reshape(B, H, N, C)
    gc = jnp.cumsum(g, axis=-1)
    causal = jnp.tril(jnp.ones((C, C), bool))
    strict = jnp.tril(jnp.ones((C, C), bool), -1)
    diff = gc[..., :, None] - gc[..., None, :]
    decay = jnp.where(causal, jnp.exp(jnp.where(causal, diff, 0.0)), 0.0)
    kk = jnp.einsum('bhnid,bhnjd->bhnij', k, k)
    a_mat = jnp.eye(C, dtype=q.dtype) + jnp.where(strict, beta[..., :, None] * kk * decay, 0.0)
    rhs = jnp.concatenate([v * beta[..., None], k * (beta * jnp.exp(gc))[..., None]], axis=-1)
    sol = lax.linalg.triangular_solve(a_mat, rhs, left_side=True, lower=True, unit_diagonal=True)
    u, w = sol[..., :dv], sol[..., dv:]
    qk = jnp.einsum('bhnid,bhnjd->bhnij', q, k) * decay
    q_dec = q * jnp.exp(gc)[..., None]
    k_dec = k * jnp.exp(gc[..., -1:] - gc)[..., None]
    g_tot = jnp.exp(gc[..., -1])
    xs = tuple(jnp.moveaxis(t, 2, 0) for t in (qk, q_dec, k_dec, u, w, g_tot))

    def step(state, inp):
        qk_n, qd_n, kd_n, u_n, w_n, gt_n = inp
        v_new = u_n - jnp.einsum('bhcd,bhde->bhce', w_n, state)
        o = jnp.einsum('bhcd,bhde->bhce', qd_n, state) + jnp.einsum('bhij,bhje->bhie', qk_n, v_new)
        state = state * gt_n[..., None, None] + jnp.einsum('bhcd,bhce->bhde', kd_n, v_new)
        return state, o

    s0 = jnp.zeros((B, H, dk, dv), q.dtype)
    _, o = lax.scan(step, s0, xs)
    return jnp.moveaxis(o, 0, 2).reshape(B, H, S, dv)


def _mixer_deltanet(qkv, a_pre, b_pre, z, conv_w, a_log, dt_bias, norm_g):
    B, S, _ = qkv.shape
    qkv = jax.nn.silu(_centred_conv(qkv, conv_w))
    q, k, v = _split(qkv, (DN_HEADS * DN_DK, DN_HEADS * DN_DK, DN_HEADS * DN_DV))
    q = _l2n(_heads(q, DN_HEADS).astype(F32)) * (DN_DK ** -0.5)
    k = _l2n(_heads(k, DN_HEADS).astype(F32))
    v = _heads(v, DN_HEADS).astype(F32)
    a_pre = a_pre.reshape(B, S, 2, DN_HEADS).transpose(2, 0, 3, 1).astype(F32)
    b_pre = b_pre.reshape(B, S, 2, DN_HEADS).transpose(2, 0, 3, 1).astype(F32)
    g = -jnp.exp(a_log.astype(F32))[:, None, :, None] * jax.nn.softplus(a_pre + dt_bias.astype(F32)[:, None, :, None])
    beta = jax.nn.sigmoid(b_pre)
    o_f = _gated_delta(q, k, v, g[0], beta[0])
    fl = lambda t: jnp.flip(t, axis=2)
    o_b = fl(_gated_delta(fl(q), fl(k), fl(v), fl(g[1]), fl(beta[1])))
    o = _rms(o_f + o_b, norm_g)
    return _merge_heads(o).astype(z.dtype) * jax.nn.silu(z)


def _neighbourhood_attention(q, k, v, rpb):
    B, H, S, dh = q.shape
    rows = S // GRID_W
    kr = min(NA_ROWS, rows)
    qg = q.reshape(B, H, rows, GRID_W, dh)
    kg = k.reshape(B, H, rows, GRID_W, dh)
    vg = v.reshape(B, H, rows, GRID_W, dh)
    r = jnp.arange(rows)
    r0 = jnp.clip(r - kr // 2, 0, rows - kr)
    key_rows = r0[:, None] + jnp.arange(kr)[None, :]
    kw = kg[:, :, key_rows]
    vw = vg[:, :, key_rows]
    s = jnp.einsum('bhrqd,bhrkwd->bhrqkw', qg, kw).astype(F32) * (dh ** -0.5)
    c = jnp.arange(GRID_W)
    c0 = jnp.clip(c - NA_COLS // 2, 0, GRID_W - NA_COLS)
    in_win = (c[None, :] >= c0[:, None]) & (c[None, :] < c0[:, None] + NA_COLS)
    row_off = key_rows - r[:, None] + NA_ROWS - 1
    col_off = jnp.clip(c[None, :] - c[:, None], -(NA_COLS - 1), NA_COLS - 1) + NA_COLS - 1
    bias = rpb[:, row_off[:, None, :, None], col_off[None, :, None, :]]
    s = jnp.where(in_win[:, None, :], s + bias[None].astype(F32), -jnp.inf)
    p = jax.nn.softmax(s.reshape(B, H, rows, GRID_W, kr * GRID_W), axis=-1)
    p = p.reshape(B, H, rows, GRID_W, kr, GRID_W).astype(v.dtype)
    o = jnp.einsum('bhrqkw,bhrkwd->bhrqd', p, vw)
    return o.reshape(B, H, S, dh)


def _mixer_natten(qkv, z, q_g, k_g, rpb):
    q, k, v = _split(qkv, (NA_HEADS * NA_DH,) * 3)
    q = _rms(_heads(q, NA_HEADS), q_g)
    k = _rms(_heads(k, NA_HEADS), k_g)
    v = _heads(v, NA_HEADS)
    o = _neighbourhood_attention(q, k, v, rpb)
    return _merge_heads(o) * jax.nn.silu(z)


def _rope_tables(S, dtype):
    t = jnp.arange(S)
    row = (t // GRID_W).astype(F32)
    col = (t % GRID_W).astype(F32)
    m = GA_DH // 4
    inv = ROPE_THETA ** (-jnp.arange(m, dtype=F32) / m)
    ar = row[:, None] * inv
    ac = col[:, None] * inv
    return (jnp.cos(ar).astype(dtype), jnp.sin(ar).astype(dtype),
            jnp.cos(ac).astype(dtype), jnp.sin(ac).astype(dtype))


def _rope_axis(x, cos, sin):
    m = x.shape[-1] // 2
    x1, x2 = x[..., :m], x[..., m:]
    return jnp.concatenate([x1 * cos - x2 * sin, x1 * sin + x2 * cos], axis=-1)


def _axial_rope(x, cr, sr, cc, sc):
    half = x.shape[-1] // 2
    return jnp.concatenate([_rope_axis(x[..., :half], cr, sr), _rope_axis(x[..., half:], cc, sc)], axis=-1)


def _gqa_blocked(q, k, v):
    B, Hk, G, S, dh = q.shape
    nb = S // GA_BLOCK
    qb = q.reshape(B, Hk, G, nb, GA_BLOCK, dh).transpose(3, 0, 1, 2, 4, 5)

    def one(qi):
        s = jnp.einsum('bkgqd,bksd->bkgqs', qi, k).astype(F32) * (dh ** -0.5)
        p = jax.nn.softmax(s, axis=-1).astype(v.dtype)
        return jnp.einsum('bkgqs,bksd->bkgqd', p, v)

    o = lax.map(one, qb)
    return o.transpose(1, 2, 3, 0, 4, 5).reshape(B, Hk * G, S, dh)


def _mixer_gqa(q, k, v, z, q_g, k_g, rope):
    B, S, _ = q.shape
    q = _axial_rope(_rms(_heads(q, GA_HEADS), q_g), *rope)
    k = _axial_rope(_rms(_heads(k, GA_KV_HEADS), k_g), *rope)
    v = _heads(v, GA_KV_HEADS)
    q = q.reshape(B, GA_KV_HEADS, GA_HEADS // GA_KV_HEADS, S, GA_DH)
    o = _gqa_blocked(q, k, v)
    return _merge_heads(o) * jax.nn.silu(z)


def _mlstm_chunked(q, k, v, ig, lf):
    B, H, S, dk = q.shape
    dv = v.shape[-1]
    C = ML_CHUNK
    N = S // C
    q = q.reshape(B, H, N, C, dk)
    k = k.reshape(B, H, N, C, dk)
    v = v.reshape(B, H, N, C, dv)
    ig = ig.reshape(B, H, N, C)
    bcum = jnp.cumsum(lf.reshape(B, H, N, C), axis=-1)
    xs = tuple(jnp.moveaxis(t, 2, 0) for t in (q, k, v, ig, bcum))
    causal = jnp.tril(jnp.ones((C, C), bool))

    def step(carry, inp):
        c_st, n_st, m_st = carry
        q_n, k_n, v_n, ig_n, b_n = inp
        d_log = jnp.where(causal, b_n[..., :, None] - b_n[..., None, :] + ig_n[..., None, :], -jnp.inf)
        m_inter = b_n + m_st[..., None]
        m = jnp.maximum(jnp.max(d_log, axis=-1), m_inter)
        s = jnp.einsum('bhid,bhjd->bhij', q_n, k_n) * jnp.exp(d_log - m[..., None])
        inter = jnp.exp(m_inter - m)
        numer = inter[..., None] * jnp.einsum('bhid,bhde->bhie', q_n, c_st) + jnp.einsum('bhij,bhje->bhie', s, v_n)
        denom = inter * jnp.einsum('bhid,bhd->bhi', q_n, n_st) + jnp.sum(s, axis=-1)
        h = numer / jnp.maximum(jnp.abs(denom), jnp.exp(-m))[..., None]
        b_last = b_n[..., -1]
        w_log = b_last[..., None] - b_n + ig_n
        m_new = jnp.maximum(b_last + m_st, jnp.max(w_log, axis=-1))
        dec = jnp.exp(b_last + m_st - m_new)
        wk = k_n * jnp.exp(w_log - m_new[..., None])[..., None]
        c_st = dec[..., None, None] * c_st + jnp.einsum('bhcd,bhce->bhde', wk, v_n)
        n_st = dec[..., None] * n_st + jnp.sum(wk, axis=-2)
        return (c_st, n_st, m_new), h

    init = (jnp.zeros((B, H, dk, dv), q.dtype), jnp.zeros((B, H, dk), q.dtype), jnp.zeros((B, H), q.dtype))
    _, h = lax.scan(step, init, xs)
    return jnp.moveaxis(h, 0, 2).reshape(B, H, S, dv)


def _mixer_mlstm(q, k, v, i_pre, f_pre, o_pre, z, i_bias, f_bias, norm_g):
    B, S, _ = q.shape
    q = _heads(q, ML_HEADS).astype(F32)
    k = _heads(k, ML_HEADS).astype(F32) * (ML_DK ** -0.5)
    v = _heads(v, ML_HEADS).astype(F32)
    ig = i_pre.reshape(B, S, 2, ML_HEADS).transpose(2, 0, 3, 1).astype(F32) + i_bias.astype(F32)[:, None, :, None]
    lf = jax.nn.log_sigmoid(f_pre.reshape(B, S, 2, ML_HEADS).transpose(2, 0, 3, 1).astype(F32)
                            + f_bias.astype(F32)[:, None, :, None])
    h_f = _mlstm_chunked(q, k, v, ig[0], lf[0])
    fl = lambda t: jnp.flip(t, axis=2)
    h_b = fl(_mlstm_chunked(fl(q), fl(k), fl(v), fl(ig[1]), fl(lf[1])))
    h = _merge_heads(_rms(h_f + h_b, norm_g)).astype(z.dtype)
    return jax.nn.sigmoid(o_pre) * h * jax.nn.silu(z)


def setup_inputs(seed: int = 0) -> dict:
    key = jax.random.key(seed)
    ks = jax.random.split(key, 18)

    def nrm(k, shape, s):
        return s * jax.random.normal(k, shape, F32)

    proj_w = sum(PROJ_SIZES)
    dt = jnp.exp(jax.random.uniform(ks[5], (DEPTH, 2, DN_HEADS), F32, math.log(1e-3), math.log(1e-1)))
    return {
        "x": jax.random.normal(ks[0], (BATCH, SEQ, D_MODEL), F32),
        "norm_g": 1.0 + nrm(ks[1], (DEPTH, D_MODEL), 0.02),
        "w_in": nrm(ks[2], (DEPTH, D_MODEL, proj_w), D_MODEL ** -0.5),
        "conv_a": nrm(ks[3], (DEPTH, DN_CONV, DN_HEADS * (2 * DN_DK + DN_DV)), DN_CONV ** -0.5),
        "dn_a_log": jnp.log(jax.random.uniform(ks[4], (DEPTH, 2, DN_HEADS), F32, 1.0, 16.0)),
        "dn_dt_bias": dt + jnp.log(-jnp.expm1(-dt)),
        "dn_norm_g": 1.0 + nrm(ks[6], (DEPTH, DN_DV), 0.02),
        "na_q_norm": 1.0 + nrm(ks[7], (DEPTH, NA_DH), 0.02),
        "na_k_norm": 1.0 + nrm(ks[8], (DEPTH, NA_DH), 0.02),
        "na_rpb": nrm(ks[9], (DEPTH, NA_HEADS, 2 * NA_ROWS - 1, 2 * NA_COLS - 1), 0.02),
        "ga_q_norm": 1.0 + nrm(ks[10], (DEPTH, GA_DH), 0.02),
        "ga_k_norm": 1.0 + nrm(ks[11], (DEPTH, GA_DH), 0.02),
        "ml_i_bias": nrm(ks[12], (DEPTH, 2, ML_HEADS), 0.1),
        "ml_f_bias": jax.random.uniform(ks[13], (DEPTH, 2, ML_HEADS), F32, 3.0, 6.0),
        "ml_norm_g": 1.0 + nrm(ks[14], (DEPTH, ML_DV), 0.02),
        "w_branch": nrm(ks[15], (DEPTH, N_BRANCH, BRANCH_W, D_MODEL), BRANCH_W ** -0.5),
        "w_out": nrm(ks[16], (DEPTH, D_MODEL, D_MODEL), D_MODEL ** -0.5),
    }


def reference(x, norm_g, w_in, conv_a, dn_a_log, dn_dt_bias, dn_norm_g, na_q_norm, na_k_norm, na_rpb,
              ga_q_norm, ga_k_norm, ml_i_bias, ml_f_bias, ml_norm_g, w_branch, w_out):
    B, S, _ = x.shape
    rope = _rope_tables(S, x.dtype)
    for l in range(DEPTH):
        h = _rms(x, norm_g[l])
        p = jnp.einsum('bsd,de->bse', h, w_in[l])
        (a_qkv, a_a, a_b, a_z, b_qkv, b_z, c_q, c_k, c_v, c_z,
         d_q, d_k, d_v, d_i, d_f, d_o, d_z, gate_logits) = _split(p, PROJ_SIZES)
        y_a = _mixer_deltanet(a_qkv, a_a, a_b, a_z, conv_a[l], dn_a_log[l], dn_dt_bias[l], dn_norm_g[l])
        y_b = _mixer_natten(b_qkv, b_z, na_q_norm[l], na_k_norm[l], na_rpb[l])
        y_c = _mixer_gqa(c_q, c_k, c_v, c_z, ga_q_norm[l], ga_k_norm[l], rope)
        y_d = _mixer_mlstm(d_q, d_k, d_v, d_i, d_f, d_o, d_z, ml_i_bias[l], ml_f_bias[l], ml_norm_g[l])
        ys = jnp.stack([y_a, y_b, y_c, y_d], axis=2)
        proj = jnp.einsum('bsnw,nwd->bsnd', ys, w_branch[l])
        gates = jax.nn.sigmoid(gate_logits.reshape(B, S, N_BRANCH, D_MODEL))
        merged = jnp.sum(gates * proj, axis=2)
        x = x + jnp.einsum('bsd,de->bse', merged, w_out[l])
    return x
```

```python
import functools
import math

import jax
import jax.numpy as jnp
from jax import lax
from jax.experimental import pallas as pl
from jax.experimental.pallas import tpu as pltpu

F32 = jnp.float32
BF16 = jnp.bfloat16

D_MODEL = 1024
GRID_W = 64
N_BRANCH = 4
BRANCH_W = 512
EPS = 1e-6
DN_HEADS, DN_DK, DN_DV, DN_CONV, DN_CHUNK = 4, 128, 128, 5, 64
NA_HEADS, NA_DH, NA_ROWS, NA_COLS = 8, 64, 8, 16
GA_HEADS, GA_KV_HEADS, GA_DH = 4, 2, 128
ROPE_THETA = 10000.0
ML_HEADS, ML_DK, ML_DV, ML_CHUNK = 4, 64, 128, 64

LANES = 128
SUBLANES = 8
VMEM_LIMIT_BYTES = 56 * 1024 * 1024

_O_A_QKV, _O_A_A, _O_A_B, _O_A_Z = 0, 1536, 1544, 1552
_O_B_QKV, _O_B_Z = 2064, 3600
_O_C_Q, _O_C_K, _O_C_V, _O_C_Z = 4112, 4624, 4880, 5136
_O_D_Q, _O_D_K, _O_D_V, _O_D_I, _O_D_F, _O_D_O, _O_D_Z = 5648, 5904, 6160, 6672, 6680, 6688, 7200
_O_GATE = 7712
_MAIN_SEGS = (
    ("a_qkv", _O_A_QKV, 1536), ("b_qkv", _O_B_QKV, 1536), ("a_z", _O_A_Z, 512), ("b_z", _O_B_Z, 512),
    ("gate", _O_GATE, 4096), ("c_q", _O_C_Q, 512), ("c_k", _O_C_K, 256), ("c_v", _O_C_V, 256),
    ("c_z", _O_C_Z, 512), ("d_q", _O_D_Q, 256), ("d_k", _O_D_K, 256), ("d_v", _O_D_V, 512),
    ("d_o", _O_D_O, 512), ("d_z", _O_D_Z, 512),
)
_OFF = {}
_o = 0
for _name, _src, _w in _MAIN_SEGS:
    _OFF[_name] = _o
    _o += _w
N_MAIN = _o
_SMALL_SRC = (_O_A_A, _O_A_B, _O_D_I, _O_D_F)
_L_AA, _L_AB, _L_DI, _L_DF = 0, 8, 16, 24

P_DTYPE = F32
BLK = 128
NEG = -1e30


def _cparams(sem):
    return pltpu.CompilerParams(dimension_semantics=sem, vmem_limit_bytes=VMEM_LIMIT_BYTES)


def _sigmoid(x):
    return 1.0 / (1.0 + jnp.exp(-x))


def _silu(x):
    return x * _sigmoid(x)


def _softplus(x):
    return jnp.maximum(x, 0.0) + jnp.log(1.0 + jnp.exp(-jnp.abs(x)))


def _mm(a, b):
    return jnp.dot(a.astype(BF16), b.astype(BF16), preferred_element_type=F32)


def _mm_nt(a, b):
    return lax.dot_general(a.astype(BF16), b.astype(BF16), (((1,), (1,)), ((), ())),
                           preferred_element_type=F32)


def _mm_tn(a, b):
    return lax.dot_general(a.astype(BF16), b.astype(BF16), (((0,), (0,)), ((), ())),
                           preferred_element_type=F32)


def _mm_f32(a, b):
    return jnp.dot(a, b, precision=lax.Precision.HIGHEST, preferred_element_type=F32)


def _chunk_masks(reverse):
    i = lax.broadcasted_iota(jnp.int32, (BLK, BLK), 0)
    j = lax.broadcasted_iota(jnp.int32, (BLK, BLK), 1)
    shift = int(math.log2(DN_CHUNK))
    same = (i >> shift) == (j >> shift)
    if reverse:
        incl = same & (j >= i)
        strict = same & (j > i)
    else:
        incl = same & (j <= i)
        strict = same & (j < i)
    return same, incl, strict, (i == j)


def _as_f32(mask):
    return jnp.where(mask, 1.0, 0.0).astype(F32)


def _inproj_kernel(x_ref, g_ref, w_ref, ws_ref, pm_ref, ps_ref, h_scr):
    @pl.when(pl.program_id(1) == 0)
    def _():
        x = x_ref[...]
        ms = jnp.mean(x * x, axis=-1, keepdims=True)
        h = (x * lax.rsqrt(ms + EPS) * g_ref[...]).astype(BF16)
        h_scr[...] = h
        ps_ref[...] = jnp.dot(h, ws_ref[...], preferred_element_type=F32)

    pm_ref[...] = jnp.dot(h_scr[...], w_ref[...], preferred_element_type=F32).astype(pm_ref.dtype)


def _inproj(x2d, g, w_main, w_small, tm=2048, tn=512):
    m, d = x2d.shape
    tm = min(tm, m)
    return pl.pallas_call(
        _inproj_kernel,
        grid=(m // tm, N_MAIN // tn),
        in_specs=[
            pl.BlockSpec((tm, d), lambda i, j: (i, 0)),
            pl.BlockSpec((1, d), lambda i, j: (0, 0)),
            pl.BlockSpec((d, tn), lambda i, j: (0, j)),
            pl.BlockSpec((d, LANES), lambda i, j: (0, 0)),
        ],
        out_specs=[
            pl.BlockSpec((tm, tn), lambda i, j: (i, j)),
            pl.BlockSpec((tm, LANES), lambda i, j: (i, 0)),
        ],
        out_shape=[jax.ShapeDtypeStruct((m, N_MAIN), P_DTYPE), jax.ShapeDtypeStruct((m, LANES), F32)],
        scratch_shapes=[pltpu.VMEM((tm, d), BF16)],
        compiler_params=_cparams(("parallel", "arbitrary")),
        name="inproj",
    )(x2d, g, w_main, w_small)


GQA_TQ = 512


def _gqa_kernel(q_ref, k_ref, v_ref, z_ref, cos_ref, sin_ref, qg_ref, kg_ref, y_ref, k_scr, v_scr):
    def norm_rope(x, g, cos, sin):
        ms = jnp.mean(x * x, axis=-1, keepdims=True)
        xn = x * lax.rsqrt(ms + EPS) * g
        lane = lax.broadcasted_iota(jnp.int32, xn.shape, 1)
        partner = jnp.where((lane & 63) < 32, pltpu.roll(xn, LANES - 32, 1), pltpu.roll(xn, 32, 1))
        return xn * cos + partner * sin

    k_scr[...] = norm_rope(k_ref[...].astype(F32), kg_ref[...], cos_ref[...], sin_ref[...]).astype(BF16)
    v_scr[...] = v_ref[...].astype(BF16)
    scale = GA_DH ** -0.5
    nq = q_ref.shape[0] // GQA_TQ
    for hh in range(GA_HEADS // GA_KV_HEADS):
        cols = slice(hh * GA_DH, (hh + 1) * GA_DH)

        def body(i, carry, cols=cols):
            rows = pl.ds(pl.multiple_of(i * GQA_TQ, GQA_TQ), GQA_TQ)
            q = norm_rope(q_ref[rows, cols].astype(F32), qg_ref[...], cos_ref[rows, :], sin_ref[rows, :])
            s = _mm_nt(q, k_scr[...]) * scale
            p = jnp.exp(s - jnp.max(s, axis=-1, keepdims=True))
            l = jnp.sum(p, axis=-1, keepdims=True)
            o = _mm(p, v_scr[...]) / l
            y_ref[rows, cols] = (o * _silu(z_ref[rows, cols].astype(F32))).astype(y_ref.dtype)
            return carry

        lax.fori_loop(0, nq, body, 0)


def _gqa(pm3, cos_t, sin_t, qg, kg):
    b, s, _ = pm3.shape
    gw = (GA_HEADS // GA_KV_HEADS) * GA_DH
    return pl.pallas_call(
        _gqa_kernel,
        grid=(b, GA_KV_HEADS),
        in_specs=[
            pl.BlockSpec((None, s, gw), lambda i, j: (i, 0, _OFF["c_q"] // gw + j)),
            pl.BlockSpec((None, s, GA_DH), lambda i, j: (i, 0, _OFF["c_k"] // GA_DH + j)),
            pl.BlockSpec((None, s, GA_DH), lambda i, j: (i, 0, _OFF["c_v"] // GA_DH + j)),
            pl.BlockSpec((None, s, gw), lambda i, j: (i, 0, _OFF["c_z"] // gw + j)),
            pl.BlockSpec((s, GA_DH), lambda i, j: (0, 0)),
            pl.BlockSpec((s, GA_DH), lambda i, j: (0, 0)),
            pl.BlockSpec((1, GA_DH), lambda i, j: (0, 0)),
            pl.BlockSpec((1, GA_DH), lambda i, j: (0, 0)),
        ],
        out_specs=pl.BlockSpec((None, s, gw), lambda i, j: (i, 0, j)),
        out_shape=jax.ShapeDtypeStruct((b, s, BRANCH_W), BF16),
        scratch_shapes=[pltpu.VMEM((s, GA_DH), BF16), pltpu.VMEM((s, GA_DH), BF16)],
        compiler_params=_cparams(("parallel", "parallel")),
        name="gqa",
    )(pm3, pm3, pm3, pm3, cos_t, sin_t, qg, kg)


def _na_kernel(q_ref, k_ref, v_ref, z_ref, bias_ref, qg_ref, kg_ref, y_ref, q_scr, k_scr, v_scr, o_scr):
    s_len = q_ref.shape[0]
    rows = s_len // GRID_W
    kr = min(NA_ROWS, rows)
    lane = lax.broadcasted_iota(jnp.int32, (s_len, 2 * NA_DH), 1)
    lo = lane < NA_DH

    def rms_pair(x, g):
        x2 = x * x
        ms_lo = jnp.sum(jnp.where(lo, x2, 0.0), axis=-1, keepdims=True)
        ms_hi = jnp.sum(jnp.where(lo, 0.0, x2), axis=-1, keepdims=True)
        ms = jnp.where(lo, ms_lo, ms_hi) * (1.0 / NA_DH)
        return x * lax.rsqrt(ms + EPS) * g

    qn = rms_pair(q_ref[...].astype(F32), qg_ref[...])
    kn = rms_pair(k_ref[...].astype(F32), kg_ref[...])
    vf = v_ref[...].astype(F32)
    for hh in range(2):
        cols = slice(hh * NA_DH, (hh + 1) * NA_DH)
        q_scr[hh] = qn[:, cols].astype(BF16)
        k_scr[hh] = kn[:, cols].astype(BF16)
        v_scr[hh] = vf[:, cols].astype(BF16)
    scale = NA_DH ** -0.5
    nkeys = kr * GRID_W

    def body(r, carry):
        r0 = jnp.clip(r - kr // 2, 0, rows - kr)
        var = r0 - r + (NA_ROWS - 1)
        qrows = pl.ds(pl.multiple_of(r * GRID_W, GRID_W), GRID_W)
        krows = pl.ds(pl.multiple_of(r0 * GRID_W, GRID_W), nkeys)
        for hh in range(2):
            s = _mm_nt(q_scr[hh, qrows, :], k_scr[hh, krows, :]) * scale + bias_ref[hh, var]
            p = jnp.exp(s - jnp.max(s, axis=-1, keepdims=True))
            l = jnp.sum(p, axis=-1, keepdims=True)
            o_scr[hh, qrows, :] = _mm(p, v_scr[hh, krows, :]) / l
        return carry

    lax.fori_loop(0, rows, body, 0)
    o = jnp.concatenate([o_scr[0], o_scr[1]], axis=-1)
    y_ref[...] = (o * _silu(z_ref[...].astype(F32))).astype(y_ref.dtype)


def _na_bias_table(rpb, rows):
    kr = min(NA_ROWS, rows)
    c = jnp.arange(GRID_W)
    c0 = jnp.clip(c - NA_COLS // 2, 0, GRID_W - NA_COLS)
    in_win = (c[None, :] >= c0[:, None]) & (c[None, :] < c0[:, None] + NA_COLS)
    col_off = jnp.clip(c[None, :] - c[:, None], -(NA_COLS - 1), NA_COLS - 1) + NA_COLS - 1
    t = rpb[:, :, col_off]
    t = jnp.where(in_win[None, None], t, NEG)
    ro = jnp.arange(NA_ROWS)[:, None] + jnp.arange(kr)[None, :]
    tv = t[:, ro]
    tv = tv.transpose(0, 1, 3, 2, 4)
    return tv.reshape(rpb.shape[0], NA_ROWS, GRID_W, kr * GRID_W).astype(F32)


def _natten(pm3, bias, qg, kg):
    b, s, _ = pm3.shape
    pw = 2 * NA_DH
    npair = NA_HEADS // 2
    hw = NA_HEADS * NA_DH
    bias5 = bias.reshape(npair, 2, *bias.shape[1:])
    return pl.pallas_call(
        _na_kernel,
        grid=(npair, b),
        in_specs=[
            pl.BlockSpec((None, s, pw), lambda p, i: (i, 0, _OFF["b_qkv"] // pw + p)),
            pl.BlockSpec((None, s, pw), lambda p, i: (i, 0, (_OFF["b_qkv"] + hw) // pw + p)),
            pl.BlockSpec((None, s, pw), lambda p, i: (i, 0, (_OFF["b_qkv"] + 2 * hw) // pw + p)),
            pl.BlockSpec((None, s, pw), lambda p, i: (i, 0, _OFF["b_z"] // pw + p)),
            pl.BlockSpec((None,) + bias5.shape[1:], lambda p, i: (p, 0, 0, 0, 0)),
            pl.BlockSpec((1, pw), lambda p, i: (0, 0)),
            pl.BlockSpec((1, pw), lambda p, i: (0, 0)),
        ],
        out_specs=pl.BlockSpec((None, s, pw), lambda p, i: (i, 0, p)),
        out_shape=jax.ShapeDtypeStruct((b, s, BRANCH_W), BF16),
        scratch_shapes=[pltpu.VMEM((2, s, NA_DH), BF16), pltpu.VMEM((2, s, NA_DH), BF16),
                        pltpu.VMEM((2, s, NA_DH), BF16), pltpu.VMEM((2, s, NA_DH), F32)],
        compiler_params=_cparams(("parallel", "parallel")),
        name="natten",
    )(pm3, pm3, pm3, pm3, bias5, qg, kg)


def _tri_inverse(l_mat, eye):
    p = -l_mat
    t = eye + p
    for _ in range(int(math.log2(DN_CHUNK)) - 1):
        p = _mm_f32(p, p)
        t = t + _mm_f32(t, p)
    return t


def _dn_pre_kernel(qkv_ref, sm_ref, cw_ref, lp_ref,
                   uf_ref, ub_ref, wf_ref, wb_ref, qdf_ref, qdb_ref, kdf_ref, kdb_ref,
                   qkf_ref, qkb_ref, gtf_ref, gtb_ref):
    n = pl.program_id(1)
    nblk = pl.num_programs(1)
    s_len = qkv_ref.shape[0]
    halo = SUBLANES if qkv_ref.dtype == F32 else 2 * SUBLANES
    t0 = pl.multiple_of(n * BLK, BLK)
    pstart = pl.multiple_of(jnp.maximum(t0 - halo, 0), halo)
    nstart = pl.multiple_of(jnp.minimum(t0 + BLK, s_len - halo), halo)

    def conv_silu(col0):
        cols = slice(col0, col0 + LANES)
        prev = jnp.where(n > 0, qkv_ref[pl.ds(pstart, halo), cols].astype(F32), 0.0)
        cur = qkv_ref[pl.ds(t0, BLK), cols].astype(F32)
        nxt = jnp.where(n < nblk - 1, qkv_ref[pl.ds(nstart, halo), cols].astype(F32), 0.0)
        xw = jnp.concatenate([prev, cur, nxt], axis=0)
        base = halo - DN_CONV // 2
        acc = xw[base:base + BLK] * cw_ref[0:1, cols]
        for j in range(1, DN_CONV):
            acc = acc + xw[base + j:base + j + BLK] * cw_ref[j:j + 1, cols]
        return _silu(acc)

    def l2n(x):
        return x * lax.rsqrt(jnp.sum(x * x, axis=-1, keepdims=True) + EPS)

    sm = sm_ref[...]
    g_all = -jnp.exp(lp_ref[0:1, :]) * _softplus(sm + lp_ref[1:2, :])
    beta_all = _sigmoid(sm)
    outs = ((uf_ref, wf_ref, qdf_ref, kdf_ref, qkf_ref, gtf_ref),
            (ub_ref, wb_ref, qdb_ref, kdb_ref, qkb_ref, gtb_ref))
    per_dir = []
    for d in range(2):
        same, incl, strict, eye_b = _chunk_masks(reverse=(d == 1))
        gc = _mm_f32(_as_f32(incl), g_all)
        tot = _mm_f32(_as_f32(same), g_all)
        per_dir.append((incl, strict, _as_f32(eye_b), gc, gc.T, tot))

    for h in range(DN_HEADS):
        q = l2n(conv_silu(h * DN_DK)) * (DN_DK ** -0.5)
        k = l2n(conv_silu(DN_HEADS * DN_DK + h * DN_DK))
        v = conv_silu(2 * DN_HEADS * DN_DK + h * DN_DV)
        kk = _mm_nt(k, k)
        qk = _mm_nt(q, k)
        hc = slice(h * LANES, (h + 1) * LANES)
        for d in range(2):
            incl, strict, eye_f, gc, gct, tot = per_dir[d]
            u_ref, w_ref, qd_ref, kd_ref, qk_ref, gt_ref = outs[d]
            c = _L_AA + d * DN_HEADS + h
            gcol = gc[:, c:c + 1]
            grow = gct[c:c + 1, :]
            tcol = tot[:, c:c + 1]
            beta = beta_all[:, _L_AB + d * DN_HEADS + h:_L_AB + d * DN_HEADS + h + 1]
            decay = jnp.where(incl, jnp.exp(jnp.where(incl, gcol - grow, 0.0)), 0.0)
            l_mat = jnp.where(strict, beta * kk * decay, 0.0)
            t_inv = _tri_inverse(l_mat, eye_f)
            egc = jnp.exp(gcol)
            rhs = jnp.concatenate([v * beta, k * (beta * egc)], axis=-1)
            sol = _mm_f32(t_inv, rhs)
            u_ref[:, hc] = sol[:, :DN_DV]
            w_ref[:, hc] = sol[:, DN_DV:].astype(w_ref.dtype)
            qd_ref[:, hc] = (q * egc).astype(qd_ref.dtype)
            kd_ref[:, hc] = (k * jnp.exp(tcol - gcol)).astype(kd_ref.dtype)
            qk_ref[:, hc] = (qk * decay).astype(qk_ref.dtype)
            gtot = jnp.exp(tcol)
            for ci in range(BLK // DN_CHUNK):
                gt_ref[ci * SUBLANES:(ci + 1) * SUBLANES, hc] = jnp.broadcast_to(
                    gtot[ci * DN_CHUNK:ci * DN_CHUNK + SUBLANES, :], (SUBLANES, LANES))


def _dn_pre(pm3, ps3, conv_w8, lane_params):
    b, s, _ = pm3.shape
    nblk = s // BLK
    wq = DN_HEADS * (2 * DN_DK + DN_DV)
    hw = DN_HEADS * LANES
    tok = lambda i, j: (i, j, 0)
    big = lambda dt: jax.ShapeDtypeStruct((b, s, hw), dt)
    gts = jax.ShapeDtypeStruct((b, nblk * 2 * SUBLANES, hw), F32)
    bs_tok = pl.BlockSpec((None, BLK, hw), tok)
    bs_gt = pl.BlockSpec((None, 2 * SUBLANES, hw), tok)
    return pl.pallas_call(
        _dn_pre_kernel,
        grid=(b, nblk),
        in_specs=[
            pl.BlockSpec((None, s, wq), lambda i, j: (i, 0, _OFF["a_qkv"] // wq)),
            pl.BlockSpec((None, BLK, LANES), tok),
            pl.BlockSpec((SUBLANES, wq), lambda i, j: (0, 0)),
            pl.BlockSpec((SUBLANES, LANES), lambda i, j: (0, 0)),
        ],
        out_specs=[bs_tok] * 10 + [bs_gt] * 2,
        out_shape=[big(F32), big(F32), big(BF16), big(BF16), big(BF16), big(BF16), big(BF16), big(BF16),
                   big(BF16), big(BF16), gts, gts],
        compiler_params=_cparams(("parallel", "arbitrary")),
        name="dn_pre",
    )(pm3, ps3, conv_w8, lane_params)


def _dn_scan_kernel(uf_ref, wf_ref, qdf_ref, kdf_ref, qkf_ref, gtf_ref,
                    ub_ref, wb_ref, qdb_ref, kdb_ref, qkb_ref, gtb_ref,
                    of_ref, ob_ref, st_scr):
    @pl.when(pl.program_id(1) == 0)
    def _():
        st_scr[...] = jnp.zeros_like(st_scr)

    nchunk = BLK // DN_CHUNK
    zeros_c = jnp.zeros((DN_CHUNK, DN_DV), F32)
    streams = ((uf_ref, wf_ref, qdf_ref, kdf_ref, qkf_ref, gtf_ref, of_ref, range(nchunk)),
               (ub_ref, wb_ref, qdb_ref, kdb_ref, qkb_ref, gtb_ref, ob_ref, range(nchunk - 1, -1, -1)))
    for d, (u_ref, w_ref, qd_ref, kd_ref, qk_ref, gt_ref, o_ref, order) in enumerate(streams):
        for h in range(DN_HEADS):
            hc = slice(h * LANES, (h + 1) * LANES)
            state = st_scr[d * DN_HEADS + h]
            kd = kd_ref[:, hc]
            for ci in order:
                rows = slice(ci * DN_CHUNK, (ci + 1) * DN_CHUNK)
                wq = jnp.concatenate([w_ref[rows, hc], qd_ref[rows, hc]], axis=0)
                r = _mm(wq, state)
                v_new = u_ref[rows, hc] - r[:DN_CHUNK]
                parts = [zeros_c] * nchunk
                parts[ci] = v_new
                v_pad = jnp.concatenate(parts, axis=0)
                o_ref[rows, hc] = r[DN_CHUNK:] + _mm(qk_ref[rows, hc], v_pad)
                gt = gt_ref[ci * SUBLANES:ci * SUBLANES + 1, hc]
                state = state * gt + _mm_tn(kd, v_pad)
            st_scr[d * DN_HEADS + h] = state


def _dn_scan(pre):
    uf, ub, wf, wb, qdf, qdb, kdf, kdb, qkf, qkb, gtf, gtb = pre
    b, s, hw = uf.shape
    nblk = s // BLK
    fwd = lambda i, j: (i, j, 0)
    bwd = lambda i, j: (i, nblk - 1 - j, 0)
    def specs(imap):
        t = pl.BlockSpec((None, BLK, hw), imap)
        return [t] * 5 + [pl.BlockSpec((None, 2 * SUBLANES, hw), imap)]
    return pl.pallas_call(
        _dn_scan_kernel,
        grid=(b, nblk),
        in_specs=specs(fwd) + specs(bwd),
        out_specs=[pl.BlockSpec((None, BLK, hw), fwd), pl.BlockSpec((None, BLK, hw), bwd)],
        out_shape=[jax.ShapeDtypeStruct((b, s, hw), F32)] * 2,
        scratch_shapes=[pltpu.VMEM((2 * DN_HEADS, DN_DK, DN_DV), F32)],
        compiler_params=_cparams(("parallel", "arbitrary")),
        name="dn_scan",
    )(uf, wf, qdf, kdf, qkf, gtf, ub, wb, qdb, kdb, qkb, gtb)


ML_AUG = 2 * LANES


def _ml_kernel(qf_ref, kf_ref, vf_ref, smf_ref, qb_ref, kb_ref, vb_ref, smb_ref, lp_ref,
               hf_ref, hb_ref, c_scr, m_scr):
    @pl.when(pl.program_id(1) == 0)
    def _():
        c_scr[...] = jnp.zeros_like(c_scr)
        m_scr[...] = jnp.zeros_like(m_scr)

    nchunk = BLK // ML_CHUNK
    lane = lax.broadcasted_iota(jnp.int32, (BLK, LANES), 1)
    ones_col = jnp.where(lane == 0, 1.0, 0.0).astype(BF16)
    zeros_aug = jnp.zeros((ML_CHUNK, ML_AUG), BF16)
    streams = ((qf_ref, kf_ref, vf_ref, smf_ref, hf_ref, range(nchunk)),
               (qb_ref, kb_ref, vb_ref, smb_ref, hb_ref, range(nchunk - 1, -1, -1)))
    for d, (q_ref, k_ref, v_ref, sm_ref, h_ref, order) in enumerate(streams):
        same, incl, _, _ = _chunk_masks(reverse=(d == 1))
        sm = sm_ref[...]
        ig_all = sm + lp_ref[0:1, :]
        x = sm + lp_ref[1:2, :]
        lf_all = jnp.minimum(x, 0.0) - jnp.log(1.0 + jnp.exp(-jnp.abs(x)))
        bc_all = _mm_f32(_as_f32(incl), lf_all)
        tot_all = _mm_f32(_as_f32(same), lf_all)
        for h in range(ML_HEADS):
            ci_lane = _L_DI + d * ML_HEADS + h
            cf_lane = _L_DF + d * ML_HEADS + h
            bcol = bc_all[:, cf_lane:cf_lane + 1]
            tcol = tot_all[:, cf_lane:cf_lane + 1]
            acol = ig_all[:, ci_lane:ci_lane + 1] - bcol
            arow = jnp.broadcast_to(acol, (BLK, BLK)).T
            q = q_ref[:, h * ML_DK:(h + 1) * ML_DK].astype(F32)
            k = k_ref[:, h * ML_DK:(h + 1) * ML_DK].astype(F32) * (ML_DK ** -0.5)
            v_aug = jnp.concatenate([v_ref[:, h * ML_DV:(h + 1) * ML_DV].astype(BF16), ones_col], axis=-1)
            dlog = jnp.where(incl, bcol + arow, NEG)
            m_intra = jnp.max(dlog, axis=-1, keepdims=True)
            s_intra = _mm_nt(q, k) * jnp.exp(dlog - m_intra)
            p_intra = _mm(s_intra, v_aug)[:, :ML_DV]
            r_intra = jnp.sum(s_intra, axis=-1, keepdims=True)
            mwa = jnp.max(jnp.where(same, arow, NEG), axis=-1, keepdims=True)
            wk = k * jnp.exp(acol - mwa)
            ch = d * ML_HEADS + h
            c_st = c_scr[ch]
            m_st = m_scr[ch][0:1, 0:1]
            for ci in order:
                rows = slice(ci * ML_CHUNK, (ci + 1) * ML_CHUNK)
                r1 = slice(ci * ML_CHUNK, ci * ML_CHUNK + 1)
                m_inter = bcol[rows] + m_st
                m_i = jnp.maximum(m_intra[rows], m_inter)
                f_i = jnp.exp(m_intra[rows] - m_i)
                inter = jnp.exp(m_inter - m_i)
                qc = _mm(q[rows], c_st)
                numer = inter * qc[:, :ML_DV] + f_i * p_intra[rows]
                denom = inter * qc[:, ML_DV:ML_DV + 1] + f_i * r_intra[rows]
                hval = numer / jnp.maximum(jnp.abs(denom), jnp.exp(-m_i))
                h_ref[rows, h * ML_DV:(h + 1) * ML_DV] = hval
                tot_c = tcol[r1]
                mw_c = tot_c + mwa[r1]
                m_new = jnp.maximum(tot_c + m_st, mw_c)
                parts = [zeros_aug] * nchunk
                parts[ci] = v_aug[rows]
                kv = _mm_tn(wk, jnp.concatenate(parts, axis=0))
                c_st = jnp.exp(tot_c + m_st - m_new) * c_st + jnp.exp(mw_c - m_new) * kv
                m_st = m_new
            c_scr[ch] = c_st
            m_scr[ch] = jnp.broadcast_to(m_st, (SUBLANES, LANES))


def _mlstm(pm3, ps3, lane_params):
    b, s, _ = pm3.shape
    nblk = s // BLK
    qw = ML_HEADS * ML_DK
    vw = ML_HEADS * ML_DV
    def specs(tmap):
        blk = lambda j: tmap(j)
        return [
            pl.BlockSpec((None, BLK, qw), lambda i, j: (i, blk(j), _OFF["d_q"] // qw)),
            pl.BlockSpec((None, BLK, qw), lambda i, j: (i, blk(j), _OFF["d_k"] // qw)),
            pl.BlockSpec((None, BLK, vw), lambda i, j: (i, blk(j), _OFF["d_v"] // vw)),
            pl.BlockSpec((None, BLK, LANES), lambda i, j: (i, blk(j), 0)),
        ]
    fwd = lambda j: j
    bwd = lambda j: nblk - 1 - j
    return pl.pallas_call(
        _ml_kernel,
        grid=(b, nblk),
        in_specs=specs(fwd) + specs(bwd) + [pl.BlockSpec((SUBLANES, LANES), lambda i, j: (0, 0))],
        out_specs=[pl.BlockSpec((None, BLK, vw), lambda i, j: (i, j, 0)),
                   pl.BlockSpec((None, BLK, vw), lambda i, j: (i, nblk - 1 - j, 0))],
        out_shape=[jax.ShapeDtypeStruct((b, s, vw), F32)] * 2,
        scratch_shapes=[pltpu.VMEM((2 * ML_HEADS, ML_DK, ML_AUG), F32),
                        pltpu.VMEM((2 * ML_HEADS, SUBLANES, LANES), F32)],
        compiler_params=_cparams(("parallel", "arbitrary")),
        name="mlstm",
    )(pm3, pm3, pm3, ps3, pm3, pm3, pm3, ps3, lane_params)


def _comb_kernel(af_ref, ab_ref, df_ref, db_ref, az_ref, dz_ref, do_ref, ag_ref, dg_ref, ya_ref, yd_ref):
    def head_rms(x, g):
        outs = []
        for h in range(x.shape[-1] // LANES):
            xh = x[:, h * LANES:(h + 1) * LANES]
            ms = jnp.mean(xh * xh, axis=-1, keepdims=True)
            outs.append(xh * lax.rsqrt(ms + EPS) * g)
        return jnp.concatenate(outs, axis=-1)

    ya = head_rms(af_ref[...] + ab_ref[...], ag_ref[...]) * _silu(az_ref[...].astype(F32))
    ya_ref[...] = ya.astype(ya_ref.dtype)
    yd = _sigmoid(do_ref[...].astype(F32)) * head_rms(df_ref[...] + db_ref[...], dg_ref[...])
    yd_ref[...] = (yd * _silu(dz_ref[...].astype(F32))).astype(yd_ref.dtype)


def _combine(af, ab, df, db, pm2, ag, dg, tm=512):
    m, w = af.shape
    tok = pl.BlockSpec((tm, w), lambda i: (i, 0))
    col = lambda name: pl.BlockSpec((tm, w), lambda i: (i, _OFF[name] // w))
    vec = pl.BlockSpec((1, LANES), lambda i: (0, 0))
    return pl.pallas_call(
        _comb_kernel,
        grid=(m // tm,),
        in_specs=[tok, tok, tok, tok, col("a_z"), col("d_z"), col("d_o"), vec, vec],
        out_specs=[tok, tok],
        out_shape=[jax.ShapeDtypeStruct((m, w), BF16)] * 2,
        compiler_params=_cparams(("parallel",)),
        name="combine",
    )(af, ab, df, db, pm2, pm2, pm2, ag, dg)


def _merge_kernel(x_ref, ya_ref, yb_ref, yc_ref, yd_ref, gl_ref, wb_ref, wo_ref, o_ref):
    d = x_ref.shape[-1]
    merged = None
    for i, y_ref in enumerate((ya_ref, yb_ref, yc_ref, yd_ref)):
        proj = jnp.dot(y_ref[...], wb_ref[i], preferred_element_type=F32)
        term = _sigmoid(gl_ref[:, i * d:(i + 1) * d].astype(F32)) * proj
        merged = term if merged is None else merged + term
    o_ref[...] = x_ref[...] + jnp.dot(merged.astype(BF16), wo_ref[...], preferred_element_type=F32)


def _merge(x2d, ya, yb, yc, yd, pm2, wb, wo, tm=256):
    m, d = x2d.shape
    gw = N_BRANCH * d
    ytok = pl.BlockSpec((tm, BRANCH_W), lambda i: (i, 0))
    return pl.pallas_call(
        _merge_kernel,
        grid=(m // tm,),
        in_specs=[
            pl.BlockSpec((tm, d), lambda i: (i, 0)),
            ytok, ytok, ytok, ytok,
            pl.BlockSpec((tm, gw), lambda i: (i, _OFF["gate"] // gw)),
            pl.BlockSpec((N_BRANCH, BRANCH_W, d), lambda i: (0, 0, 0)),
            pl.BlockSpec((d, d), lambda i: (0, 0)),
        ],
        out_specs=pl.BlockSpec((tm, d), lambda i: (i, 0)),
        out_shape=jax.ShapeDtypeStruct((m, d), F32),
        compiler_params=_cparams(("parallel",)),
        name="merge",
    )(x2d, ya, yb, yc, yd, pm2, wb, wo)


def _rope_lane_tables(s):
    t = jnp.arange(s)
    row = (t // GRID_W).astype(F32)
    col = (t % GRID_W).astype(F32)
    m = GA_DH // 4
    inv = ROPE_THETA ** (-jnp.arange(m, dtype=F32) / m)
    ar = row[:, None] * inv
    ac = col[:, None] * inv
    cos_t = jnp.concatenate([jnp.cos(ar), jnp.cos(ar), jnp.cos(ac), jnp.cos(ac)], axis=-1)
    sin_t = jnp.concatenate([-jnp.sin(ar), jnp.sin(ar), -jnp.sin(ac), jnp.sin(ac)], axis=-1)
    return cos_t.astype(F32), sin_t.astype(F32)


def _lane_rows(rows):
    tile = jnp.zeros((SUBLANES, LANES), F32)
    for r, (off, vals) in enumerate(rows):
        vals = vals.reshape(-1).astype(F32)
        tile = tile.at[r, off:off + vals.shape[0]].set(vals)
    return tile


def kernel(x, norm_g, w_in, conv_a, dn_a_log, dn_dt_bias, dn_norm_g, na_q_norm, na_k_norm, na_rpb,
           ga_q_norm, ga_k_norm, ml_i_bias, ml_f_bias, ml_norm_g, w_branch, w_out):
    b, s, d = x.shape
    depth = w_in.shape[0]
    cos_t, sin_t = _rope_lane_tables(s)
    x2 = x.reshape(b * s, d)
    for l in range(depth):
        w = w_in[l]
        w_main = jnp.concatenate([w[:, o:o + wd] for _, o, wd in _MAIN_SEGS], axis=1).astype(BF16)
        w_small = jnp.concatenate([w[:, o:o + 8] for o in _SMALL_SRC]
                                  + [jnp.zeros((d, LANES - 32), F32)], axis=1).astype(BF16)
        pm2, ps2 = _inproj(x2, norm_g[l].reshape(1, d), w_main, w_small)
        pm3 = pm2.reshape(b, s, N_MAIN)
        ps3 = ps2.reshape(b, s, LANES)

        conv8 = jnp.zeros((SUBLANES, conv_a.shape[-1]), F32).at[:DN_CONV].set(conv_a[l])
        dn_lp = _lane_rows([(_L_AA, dn_a_log[l]), (_L_AA, dn_dt_bias[l])])
        o_af, o_ab = _dn_scan(_dn_pre(pm3, ps3, conv8, dn_lp))

        ml_lp = _lane_rows([(_L_DI, ml_i_bias[l]), (_L_DF, ml_f_bias[l])])
        h_df, h_db = _mlstm(pm3, ps3, ml_lp)

        hw = BRANCH_W
        ya, yd = _combine(o_af.reshape(b * s, hw), o_ab.reshape(b * s, hw),
                          h_df.reshape(b * s, hw), h_db.reshape(b * s, hw), pm2,
                          dn_norm_g[l].reshape(1, LANES), ml_norm_g[l].reshape(1, LANES))

        bias = _na_bias_table(na_rpb[l], s // GRID_W)
        yb = _natten(pm3, bias, jnp.tile(na_q_norm[l], 2).reshape(1, 2 * NA_DH),
                     jnp.tile(na_k_norm[l], 2).reshape(1, 2 * NA_DH))
        yc = _gqa(pm3, cos_t, sin_t, ga_q_norm[l].reshape(1, GA_DH), ga_k_norm[l].reshape(1, GA_DH))

        x2 = _merge(x2, ya, yb.reshape(b * s, hw), yc.reshape(b * s, hw), yd, pm2,
                    w_branch[l].astype(BF16), w_out[l].astype(BF16))
    return x2.reshape(b, s, d)
```

```python
import functools
import math

import jax
import jax.numpy as jnp
from jax import lax
from jax.experimental import pallas as pl
from jax.experimental.pallas import tpu as pltpu

F32 = jnp.float32
BF16 = jnp.bfloat16

D_MODEL = 1024
GRID_W = 64
N_BRANCH = 4
BRANCH_W = 512
EPS = 1e-6
DN_HEADS, DN_DK, DN_DV, DN_CONV, DN_CHUNK = 4, 128, 128, 5, 64
NA_HEADS, NA_DH, NA_ROWS, NA_COLS = 8, 64, 8, 16
GA_HEADS, GA_KV_HEADS, GA_DH = 4, 2, 128
ROPE_THETA = 10000.0
ML_HEADS, ML_DK, ML_DV, ML_CHUNK = 4, 64, 128, 64

LANES = 128
SUBLANES = 8
VMEM_LIMIT_BYTES = 56 * 1024 * 1024

_O_A_QKV, _O_A_A, _O_A_B, _O_A_Z = 0, 1536, 1544, 1552
_O_B_QKV, _O_B_Z = 2064, 3600
_O_C_Q, _O_C_K, _O_C_V, _O_C_Z = 4112, 4624, 4880, 5136
_O_D_Q, _O_D_K, _O_D_V, _O_D_I, _O_D_F, _O_D_O, _O_D_Z = 5648, 5904, 6160, 6672, 6680, 6688, 7200
_O_GATE = 7712
_MAIN_SEGS = (
    ("a_qkv", _O_A_QKV, 1536), ("b_qkv", _O_B_QKV, 1536), ("a_z", _O_A_Z, 512), ("b_z", _O_B_Z, 512),
    ("gate", _O_GATE, 4096), ("c_q", _O_C_Q, 512), ("c_k", _O_C_K, 256), ("c_v", _O_C_V, 256),
    ("c_z", _O_C_Z, 512), ("d_q", _O_D_Q, 256), ("d_k", _O_D_K, 256), ("d_v", _O_D_V, 512),
    ("d_o", _O_D_O, 512), ("d_z", _O_D_Z, 512),
)
_OFF = {}
_o = 0
for _name, _src, _w in _MAIN_SEGS:
    _OFF[_name] = _o
    _o += _w
N_MAIN = _o
_SMALL_SRC = (_O_A_A, _O_A_B, _O_D_I, _O_D_F)
_L_AA, _L_AB, _L_DI, _L_DF = 0, 8, 16, 24

P_DTYPE = F32
BLK = 128
NEG = -1e30


def _cparams(sem):
    return pltpu.CompilerParams(dimension_semantics=sem, vmem_limit_bytes=VMEM_LIMIT_BYTES)


def _sigmoid(x):
    return 1.0 / (1.0 + jnp.exp(-x))


def _silu(x):
    return x * _sigmoid(x)


def _softplus(x):
    return jnp.maximum(x, 0.0) + jnp.log(1.0 + jnp.exp(-jnp.abs(x)))


def _mm(a, b):
    return jnp.dot(a.astype(BF16), b.astype(BF16), preferred_element_type=F32)


def _mm_nt(a, b):
    return lax.dot_general(a.astype(BF16), b.astype(BF16), (((1,), (1,)), ((), ())),
                           preferred_element_type=F32)


def _mm_tn(a, b):
    return lax.dot_general(a.astype(BF16), b.astype(BF16), (((0,), (0,)), ((), ())),
                           preferred_element_type=F32)


def _mask_sum(mask, x):
    m = jnp.where(mask, 1.0, 0.0).astype(BF16)
    x1 = x.astype(BF16)
    r1 = x - x1.astype(F32)
    x2 = r1.astype(BF16)
    x3 = (r1 - x2.astype(F32)).astype(BF16)
    dot = lambda v: jnp.dot(m, v, preferred_element_type=F32)
    return dot(x1) + (dot(x2) + dot(x3))


def _chunk_masks(reverse):
    i = lax.broadcasted_iota(jnp.int32, (BLK, BLK), 0)
    j = lax.broadcasted_iota(jnp.int32, (BLK, BLK), 1)
    shift = int(math.log2(DN_CHUNK))
    same = (i >> shift) == (j >> shift)
    if reverse:
        incl = same & (j >= i)
        strict = same & (j > i)
    else:
        incl = same & (j <= i)
        strict = same & (j < i)
    return same, incl, strict, (i == j)


def _as_f32(mask):
    return jnp.where(mask, 1.0, 0.0).astype(F32)


def _inproj_kernel(x_ref, g_ref, w_ref, ws_ref, pm_ref, ps_ref, h_scr):
    @pl.when(pl.program_id(1) == 0)
    def _():
        x = x_ref[...]
        ms = jnp.mean(x * x, axis=-1, keepdims=True)
        h = (x * lax.rsqrt(ms + EPS) * g_ref[...]).astype(BF16)
        h_scr[...] = h
        ps_ref[...] = jnp.dot(h, ws_ref[...], preferred_element_type=F32)

    pm_ref[...] = jnp.dot(h_scr[...], w_ref[...], preferred_element_type=F32).astype(pm_ref.dtype)


def _inproj(x2d, g, w_main, w_small, tm=2048, tn=512):
    m, d = x2d.shape
    tm = min(tm, m)
    return pl.pallas_call(
        _inproj_kernel,
        grid=(m // tm, N_MAIN // tn),
        in_specs=[
            pl.BlockSpec((tm, d), lambda i, j: (i, 0)),
            pl.BlockSpec((1, d), lambda i, j: (0, 0)),
            pl.BlockSpec((d, tn), lambda i, j: (0, j)),
            pl.BlockSpec((d, LANES), lambda i, j: (0, 0)),
        ],
        out_specs=[
            pl.BlockSpec((tm, tn), lambda i, j: (i, j)),
            pl.BlockSpec((tm, LANES), lambda i, j: (i, 0)),
        ],
        out_shape=[jax.ShapeDtypeStruct((m, N_MAIN), P_DTYPE), jax.ShapeDtypeStruct((m, LANES), F32)],
        scratch_shapes=[pltpu.VMEM((tm, d), BF16)],
        compiler_params=_cparams(("parallel", "arbitrary")),
        name="inproj",
    )(x2d, g, w_main, w_small)


GQA_TQ = 512


def _gqa_kernel(q_ref, k_ref, v_ref, z_ref, cos_ref, sin_ref, qg_ref, kg_ref, y_ref, k_scr, v_scr):
    def norm_rope(x, g, cos, sin):
        ms = jnp.mean(x * x, axis=-1, keepdims=True)
        xn = x * lax.rsqrt(ms + EPS) * g
        lane = lax.broadcasted_iota(jnp.int32, xn.shape, 1)
        partner = jnp.where((lane & 63) < 32, pltpu.roll(xn, LANES - 32, 1), pltpu.roll(xn, 32, 1))
        return xn * cos + partner * sin

    k_scr[...] = norm_rope(k_ref[...].astype(F32), kg_ref[...], cos_ref[...], sin_ref[...]).astype(BF16)
    v_scr[...] = v_ref[...].astype(BF16)
    scale = GA_DH ** -0.5
    nq = q_ref.shape[0] // GQA_TQ
    for hh in range(GA_HEADS // GA_KV_HEADS):
        cols = slice(hh * GA_DH, (hh + 1) * GA_DH)

        def body(i, carry, cols=cols):
            rows = pl.ds(pl.multiple_of(i * GQA_TQ, GQA_TQ), GQA_TQ)
            q = norm_rope(q_ref[rows, cols].astype(F32), qg_ref[...], cos_ref[rows, :], sin_ref[rows, :])
            s = _mm_nt(q, k_scr[...]) * scale
            p = jnp.exp(s - jnp.max(s, axis=-1, keepdims=True))
            l = jnp.sum(p, axis=-1, keepdims=True)
            o = _mm(p, v_scr[...]) / l
            y_ref[rows, cols] = (o * _silu(z_ref[rows, cols].astype(F32))).astype(y_ref.dtype)
            return carry

        lax.fori_loop(0, nq, body, 0)


def _gqa(pm3, cos_t, sin_t, qg, kg):
    b, s, _ = pm3.shape
    gw = (GA_HEADS // GA_KV_HEADS) * GA_DH
    return pl.pallas_call(
        _gqa_kernel,
        grid=(b, GA_KV_HEADS),
        in_specs=[
            pl.BlockSpec((None, s, gw), lambda i, j: (i, 0, _OFF["c_q"] // gw + j)),
            pl.BlockSpec((None, s, GA_DH), lambda i, j: (i, 0, _OFF["c_k"] // GA_DH + j)),
            pl.BlockSpec((None, s, GA_DH), lambda i, j: (i, 0, _OFF["c_v"] // GA_DH + j)),
            pl.BlockSpec((None, s, gw), lambda i, j: (i, 0, _OFF["c_z"] // gw + j)),
            pl.BlockSpec((s, GA_DH), lambda i, j: (0, 0)),
            pl.BlockSpec((s, GA_DH), lambda i, j: (0, 0)),
            pl.BlockSpec((1, GA_DH), lambda i, j: (0, 0)),
            pl.BlockSpec((1, GA_DH), lambda i, j: (0, 0)),
        ],
        out_specs=pl.BlockSpec((None, s, gw), lambda i, j: (i, 0, j)),
        out_shape=jax.ShapeDtypeStruct((b, s, BRANCH_W), BF16),
        scratch_shapes=[pltpu.VMEM((s, GA_DH), BF16), pltpu.VMEM((s, GA_DH), BF16)],
        compiler_params=_cparams(("parallel", "parallel")),
        name="gqa",
    )(pm3, pm3, pm3, pm3, cos_t, sin_t, qg, kg)


NA_ROW_UNROLL = 4


def _na_kernel(q_ref, k_ref, v_ref, z_ref, bias_ref, qg_ref, kg_ref, y_ref, q_scr, k_scr, v_scr, o_scr):
    s_len = q_ref.shape[0]
    rows = s_len // GRID_W
    kr = min(NA_ROWS, rows)
    lane = lax.broadcasted_iota(jnp.int32, (s_len, 2 * NA_DH), 1)
    lo = lane < NA_DH

    def rms_pair(x, g):
        x2 = x * x
        ms_lo = jnp.sum(jnp.where(lo, x2, 0.0), axis=-1, keepdims=True)
        ms_hi = jnp.sum(jnp.where(lo, 0.0, x2), axis=-1, keepdims=True)
        ms = jnp.where(lo, ms_lo, ms_hi) * (1.0 / NA_DH)
        return x * lax.rsqrt(ms + EPS) * g

    qn = rms_pair(q_ref[...].astype(F32), qg_ref[...])
    kn = rms_pair(k_ref[...].astype(F32), kg_ref[...])
    vf = v_ref[...].astype(F32)
    for hh in range(2):
        cols = slice(hh * NA_DH, (hh + 1) * NA_DH)
        q_scr[hh] = qn[:, cols].astype(BF16)
        k_scr[hh] = kn[:, cols].astype(BF16)
        v_scr[hh] = vf[:, cols].astype(BF16)
    scale = NA_DH ** -0.5
    nkeys = kr * GRID_W

    def body(it, carry):
        units = []
        for u in range(NA_ROW_UNROLL):
            r = it * NA_ROW_UNROLL + u
            r0 = jnp.clip(r - kr // 2, 0, rows - kr)
            var = r0 - r + (NA_ROWS - 1)
            qrows = pl.ds(pl.multiple_of(r * GRID_W, GRID_W), GRID_W)
            krows = pl.ds(pl.multiple_of(r0 * GRID_W, GRID_W), nkeys)
            units += [(hh, var, qrows, krows) for hh in range(2)]
        logits = [_mm_nt(q_scr[hh, qrows, :], k_scr[hh, krows, :]) * scale + bias_ref[hh, var]
                  for hh, var, qrows, krows in units]
        probs = [jnp.exp(s - jnp.max(s, axis=-1, keepdims=True)) for s in logits]
        sums = [jnp.sum(p, axis=-1, keepdims=True) for p in probs]
        outs = [_mm(p, v_scr[hh, krows, :]) for p, (hh, _, _, krows) in zip(probs, units)]
        for o, l, (hh, _, qrows, _) in zip(outs, sums, units):
            o_scr[hh, qrows, :] = o / l
        return carry

    lax.fori_loop(0, rows // NA_ROW_UNROLL, body, 0)
    o = jnp.concatenate([o_scr[0], o_scr[1]], axis=-1)
    y_ref[...] = (o * _silu(z_ref[...].astype(F32))).astype(y_ref.dtype)


def _na_bias_table(rpb, rows):
    kr = min(NA_ROWS, rows)
    c = jnp.arange(GRID_W)
    c0 = jnp.clip(c - NA_COLS // 2, 0, GRID_W - NA_COLS)
    in_win = (c[None, :] >= c0[:, None]) & (c[None, :] < c0[:, None] + NA_COLS)
    col_off = jnp.clip(c[None, :] - c[:, None], -(NA_COLS - 1), NA_COLS - 1) + NA_COLS - 1
    t = rpb[:, :, col_off]
    t = jnp.where(in_win[None, None], t, NEG)
    ro = jnp.arange(NA_ROWS)[:, None] + jnp.arange(kr)[None, :]
    tv = t[:, ro]
    tv = tv.transpose(0, 1, 3, 2, 4)
    return tv.reshape(rpb.shape[0], NA_ROWS, GRID_W, kr * GRID_W).astype(F32)


def _natten(pm3, bias, qg, kg):
    b, s, _ = pm3.shape
    pw = 2 * NA_DH
    npair = NA_HEADS // 2
    hw = NA_HEADS * NA_DH
    bias5 = bias.reshape(npair, 2, *bias.shape[1:])
    return pl.pallas_call(
        _na_kernel,
        grid=(npair, b),
        in_specs=[
            pl.BlockSpec((None, s, pw), lambda p, i: (i, 0, _OFF["b_qkv"] // pw + p)),
            pl.BlockSpec((None, s, pw), lambda p, i: (i, 0, (_OFF["b_qkv"] + hw) // pw + p)),
            pl.BlockSpec((None, s, pw), lambda p, i: (i, 0, (_OFF["b_qkv"] + 2 * hw) // pw + p)),
            pl.BlockSpec((None, s, pw), lambda p, i: (i, 0, _OFF["b_z"] // pw + p)),
            pl.BlockSpec((None,) + bias5.shape[1:], lambda p, i: (p, 0, 0, 0, 0)),
            pl.BlockSpec((1, pw), lambda p, i: (0, 0)),
            pl.BlockSpec((1, pw), lambda p, i: (0, 0)),
        ],
        out_specs=pl.BlockSpec((None, s, pw), lambda p, i: (i, 0, p)),
        out_shape=jax.ShapeDtypeStruct((b, s, BRANCH_W), BF16),
        scratch_shapes=[pltpu.VMEM((2, s, NA_DH), BF16), pltpu.VMEM((2, s, NA_DH), BF16),
                        pltpu.VMEM((2, s, NA_DH), BF16), pltpu.VMEM((2, s, NA_DH), F32)],
        compiler_params=_cparams(("parallel", "parallel")),
        name="natten",
    )(pm3, pm3, pm3, pm3, bias5, qg, kg)


INV_BASE = 8


def _inverse_level_masks():
    i = lax.broadcasted_iota(jnp.int32, (BLK, BLK), 0)
    j = lax.broadcasted_iota(jnp.int32, (BLK, BLK), 1)
    same = lambda size: (i >> int(math.log2(size))) == (j >> int(math.log2(size)))
    base = same(INV_BASE)
    joins = []
    size = INV_BASE
    while size < DN_CHUNK:
        joins.append(same(2 * size) & jnp.logical_not(same(size)))
        size *= 2
    return base, joins


def _tri_inverses(l_mats, eyes, level_masks):
    base, joins = level_masks
    ps = [jnp.where(base, -l, 0.0) for l in l_mats]
    ts = [eye + p for eye, p in zip(eyes, ps)]
    for _ in range(int(math.log2(INV_BASE)) - 1):
        ps = [_mm(p, p) for p in ps]
        ts = [t + _mm(t, p) for t, p in zip(ts, ps)]
    for join in joins:
        mids = [_mm(jnp.where(join, l, 0.0), t) for l, t in zip(l_mats, ts)]
        ts = [t - _mm(t, mid) for t, mid in zip(ts, mids)]
    return ts


def _dn_pre_kernel(qkv_ref, sm_ref, cw_ref, lp_ref,
                   uf_ref, ub_ref, wf_ref, wb_ref, qdf_ref, qdb_ref, kdf_ref, kdb_ref,
                   qkf_ref, qkb_ref, gtf_ref, gtb_ref):
    n = pl.program_id(1)
    nblk = pl.num_programs(1)
    s_len = qkv_ref.shape[0]
    halo = SUBLANES if qkv_ref.dtype == F32 else 2 * SUBLANES
    t0 = pl.multiple_of(n * BLK, BLK)
    pstart = pl.multiple_of(jnp.maximum(t0 - halo, 0), halo)
    nstart = pl.multiple_of(jnp.minimum(t0 + BLK, s_len - halo), halo)

    def conv_silu(col0):
        cols = slice(col0, col0 + LANES)
        prev = jnp.where(n > 0, qkv_ref[pl.ds(pstart, halo), cols].astype(F32), 0.0)
        cur = qkv_ref[pl.ds(t0, BLK), cols].astype(F32)
        nxt = jnp.where(n < nblk - 1, qkv_ref[pl.ds(nstart, halo), cols].astype(F32), 0.0)
        xw = jnp.concatenate([prev, cur, nxt], axis=0)
        base = halo - DN_CONV // 2
        acc = xw[base:base + BLK] * cw_ref[0:1, cols]
        for j in range(1, DN_CONV):
            acc = acc + xw[base + j:base + j + BLK] * cw_ref[j:j + 1, cols]
        return _silu(acc)

    def l2n(x):
        return x * lax.rsqrt(jnp.sum(x * x, axis=-1, keepdims=True) + EPS)

    sm = sm_ref[...]
    g_all = -jnp.exp(lp_ref[0:1, :]) * _softplus(sm + lp_ref[1:2, :])
    beta_all = _sigmoid(sm)
    outs = ((uf_ref, wf_ref, qdf_ref, kdf_ref, qkf_ref, gtf_ref),
            (ub_ref, wb_ref, qdb_ref, kdb_ref, qkb_ref, gtb_ref))
    per_dir = []
    for d in range(2):
        same, incl, strict, eye_b = _chunk_masks(reverse=(d == 1))
        gc = _mask_sum(incl, g_all)
        tot = _mask_sum(same, g_all)
        per_dir.append((incl, strict, _as_f32(eye_b), gc, gc.T, tot))
    level_masks = _inverse_level_masks()

    heads = []
    for h in range(DN_HEADS):
        q = l2n(conv_silu(h * DN_DK)) * (DN_DK ** -0.5)
        k = l2n(conv_silu(DN_HEADS * DN_DK + h * DN_DK))
        v = conv_silu(2 * DN_HEADS * DN_DK + h * DN_DV)
        heads.append((q, k, v))
    kks = [_mm_nt(k, k) for _, k, _ in heads]
    qks = [_mm_nt(q, k) for q, k, _ in heads]

    l_mats, rhss = [], []
    for h, (q, k, v) in enumerate(heads):
        hc = slice(h * LANES, (h + 1) * LANES)
        for d in range(2):
            incl, strict, eye_f, gc, gct, tot = per_dir[d]
            u_ref, w_ref, qd_ref, kd_ref, qk_ref, gt_ref = outs[d]
            c = _L_AA + d * DN_HEADS + h
            gcol = gc[:, c:c + 1]
            grow = gct[c:c + 1, :]
            tcol = tot[:, c:c + 1]
            beta = beta_all[:, _L_AB + d * DN_HEADS + h:_L_AB + d * DN_HEADS + h + 1]
            decay = jnp.where(incl, jnp.exp(jnp.where(incl, gcol - grow, 0.0)), 0.0)
            l_mats.append(jnp.where(strict, beta * kks[h] * decay, 0.0))
            egc = jnp.exp(gcol)
            rhss.append(jnp.concatenate([v * beta, k * (beta * egc)], axis=-1).astype(BF16))
            qd_ref[:, hc] = (q * egc).astype(qd_ref.dtype)
            kd_ref[:, hc] = (k * jnp.exp(tcol - gcol)).astype(kd_ref.dtype)
            qk_ref[:, hc] = (qks[h] * decay).astype(qk_ref.dtype)
            gtot = jnp.exp(tcol)
            for ci in range(BLK // DN_CHUNK):
                gt_ref[ci * SUBLANES:(ci + 1) * SUBLANES, hc] = jnp.broadcast_to(
                    gtot[ci * DN_CHUNK:ci * DN_CHUNK + SUBLANES, :], (SUBLANES, LANES))

    t_invs = _tri_inverses(l_mats, [per_dir[d][2] for _ in range(DN_HEADS) for d in range(2)], level_masks)
    sols = [_mm(t, rhs) for t, rhs in zip(t_invs, rhss)]
    for idx, sol in enumerate(sols):
        h, d = divmod(idx, 2)
        hc = slice(h * LANES, (h + 1) * LANES)
        u_ref, w_ref = outs[d][0], outs[d][1]
        u_ref[:, hc] = sol[:, :DN_DV]
        w_ref[:, hc] = sol[:, DN_DV:].astype(w_ref.dtype)


def _dn_pre(pm3, ps3, conv_w8, lane_params):
    b, s, _ = pm3.shape
    nblk = s // BLK
    wq = DN_HEADS * (2 * DN_DK + DN_DV)
    hw = DN_HEADS * LANES
    tok = lambda i, j: (i, j, 0)
    big = lambda dt: jax.ShapeDtypeStruct((b, s, hw), dt)
    gts = jax.ShapeDtypeStruct((b, nblk * 2 * SUBLANES, hw), F32)
    bs_tok = pl.BlockSpec((None, BLK, hw), tok)
    bs_gt = pl.BlockSpec((None, 2 * SUBLANES, hw), tok)
    return pl.pallas_call(
        _dn_pre_kernel,
        grid=(b, nblk),
        in_specs=[
            pl.BlockSpec((None, s, wq), lambda i, j: (i, 0, _OFF["a_qkv"] // wq)),
            pl.BlockSpec((None, BLK, LANES), tok),
            pl.BlockSpec((SUBLANES, wq), lambda i, j: (0, 0)),
            pl.BlockSpec((SUBLANES, LANES), lambda i, j: (0, 0)),
        ],
        out_specs=[bs_tok] * 10 + [bs_gt] * 2,
        out_shape=[big(F32), big(F32), big(BF16), big(BF16), big(BF16), big(BF16), big(BF16), big(BF16),
                   big(BF16), big(BF16), gts, gts],
        compiler_params=_cparams(("parallel", "arbitrary")),
        name="dn_pre",
    )(pm3, ps3, conv_w8, lane_params)


def _dn_scan_kernel(uf_ref, wf_ref, qdf_ref, kdf_ref, qkf_ref, gtf_ref,
                    ub_ref, wb_ref, qdb_ref, kdb_ref, qkb_ref, gtb_ref,
                    of_ref, ob_ref, st_scr):
    @pl.when(pl.program_id(1) == 0)
    def _():
        st_scr[...] = jnp.zeros_like(st_scr)

    nchunk = BLK // DN_CHUNK
    zeros_c = jnp.zeros((DN_CHUNK, DN_DV), F32)
    streams = ((uf_ref, wf_ref, qdf_ref, kdf_ref, qkf_ref, gtf_ref, of_ref, range(nchunk)),
               (ub_ref, wb_ref, qdb_ref, kdb_ref, qkb_ref, gtb_ref, ob_ref, range(nchunk - 1, -1, -1)))
    chains = [(d, h) + streams[d] for d in range(2) for h in range(DN_HEADS)]
    states = [st_scr[d * DN_HEADS + h] for d, h, *_ in chains]
    for step in range(nchunk):
        rs, v_pads = [], []
        for (d, h, u_ref, w_ref, qd_ref, kd_ref, qk_ref, gt_ref, o_ref, order), state in zip(chains, states):
            hc = slice(h * LANES, (h + 1) * LANES)
            rows = slice(order[step] * DN_CHUNK, (order[step] + 1) * DN_CHUNK)
            rs.append(_mm(jnp.concatenate([w_ref[rows, hc], qd_ref[rows, hc]], axis=0), state))
        for (d, h, u_ref, w_ref, qd_ref, kd_ref, qk_ref, gt_ref, o_ref, order), r in zip(chains, rs):
            hc = slice(h * LANES, (h + 1) * LANES)
            ci = order[step]
            rows = slice(ci * DN_CHUNK, (ci + 1) * DN_CHUNK)
            parts = [zeros_c] * nchunk
            parts[ci] = u_ref[rows, hc] - r[:DN_CHUNK]
            v_pads.append(jnp.concatenate(parts, axis=0))
        new_states = []
        for (d, h, u_ref, w_ref, qd_ref, kd_ref, qk_ref, gt_ref, o_ref, order), r, v_pad, state in zip(
                chains, rs, v_pads, states):
            hc = slice(h * LANES, (h + 1) * LANES)
            ci = order[step]
            rows = slice(ci * DN_CHUNK, (ci + 1) * DN_CHUNK)
            o_ref[rows, hc] = r[DN_CHUNK:] + _mm(qk_ref[rows, hc], v_pad)
            gt = gt_ref[ci * SUBLANES:ci * SUBLANES + 1, hc]
            new_states.append(state * gt + _mm_tn(kd_ref[:, hc], v_pad))
        states = new_states
    for (d, h, *_), state in zip(chains, states):
        st_scr[d * DN_HEADS + h] = state


def _dn_scan(pre):
    uf, ub, wf, wb, qdf, qdb, kdf, kdb, qkf, qkb, gtf, gtb = pre
    b, s, hw = uf.shape
    nblk = s // BLK
    fwd = lambda i, j: (i, j, 0)
    bwd = lambda i, j: (i, nblk - 1 - j, 0)
    def specs(imap):
        t = pl.BlockSpec((None, BLK, hw), imap)
        return [t] * 5 + [pl.BlockSpec((None, 2 * SUBLANES, hw), imap)]
    return pl.pallas_call(
        _dn_scan_kernel,
        grid=(b, nblk),
        in_specs=specs(fwd) + specs(bwd),
        out_specs=[pl.BlockSpec((None, BLK, hw), fwd), pl.BlockSpec((None, BLK, hw), bwd)],
        out_shape=[jax.ShapeDtypeStruct((b, s, hw), F32)] * 2,
        scratch_shapes=[pltpu.VMEM((2 * DN_HEADS, DN_DK, DN_DV), F32)],
        compiler_params=_cparams(("parallel", "arbitrary")),
        name="dn_scan",
    )(uf, wf, qdf, kdf, qkf, gtf, ub, wb, qdb, kdb, qkb, gtb)


ML_AUG = 2 * LANES


def _ml_kernel(qf_ref, kf_ref, vf_ref, smf_ref, qb_ref, kb_ref, vb_ref, smb_ref, lp_ref,
               hf_ref, hb_ref, c_scr, m_scr):
    @pl.when(pl.program_id(1) == 0)
    def _():
        c_scr[...] = jnp.zeros_like(c_scr)
        m_scr[...] = jnp.zeros_like(m_scr)

    nchunk = BLK // ML_CHUNK
    lane = lax.broadcasted_iota(jnp.int32, (BLK, LANES), 1)
    ones_col = jnp.where(lane == 0, 1.0, 0.0).astype(BF16)
    zeros_aug = jnp.zeros((ML_CHUNK, ML_AUG), BF16)
    streams = ((qf_ref, kf_ref, vf_ref, smf_ref, hf_ref, range(nchunk)),
               (qb_ref, kb_ref, vb_ref, smb_ref, hb_ref, range(nchunk - 1, -1, -1)))
    gates = []
    for d, (q_ref, k_ref, v_ref, sm_ref, h_ref, order) in enumerate(streams):
        same, incl, _, _ = _chunk_masks(reverse=(d == 1))
        sm = sm_ref[...]
        ig_all = sm + lp_ref[0:1, :]
        x = sm + lp_ref[1:2, :]
        lf_all = jnp.minimum(x, 0.0) - jnp.log(1.0 + jnp.exp(-jnp.abs(x)))
        bc_all = _mask_sum(incl, lf_all)
        tot_all = _mask_sum(same, lf_all)
        gates.append((same, incl, ig_all, bc_all, tot_all))

    for d, (q_ref, k_ref, v_ref, sm_ref, h_ref, order) in enumerate(streams):
        same, incl, ig_all, bc_all, tot_all = gates[d]
        for h in range(ML_HEADS):
            ci_lane = _L_DI + d * ML_HEADS + h
            cf_lane = _L_DF + d * ML_HEADS + h
            bcol = bc_all[:, cf_lane:cf_lane + 1]
            tcol = tot_all[:, cf_lane:cf_lane + 1]
            acol = ig_all[:, ci_lane:ci_lane + 1] - bcol
            arow = jnp.broadcast_to(acol, (BLK, BLK)).T
            q = q_ref[:, h * ML_DK:(h + 1) * ML_DK].astype(BF16)
            k = k_ref[:, h * ML_DK:(h + 1) * ML_DK].astype(F32) * (ML_DK ** -0.5)
            v_aug = jnp.concatenate([v_ref[:, h * ML_DV:(h + 1) * ML_DV].astype(BF16), ones_col], axis=-1)
            dlog = jnp.where(incl, bcol + arow, NEG)
            m_intra = jnp.max(dlog, axis=-1, keepdims=True)
            s_intra = _mm_nt(q, k) * jnp.exp(dlog - m_intra)
            p_intra = _mm(s_intra, v_aug)[:, :ML_DV]
            r_intra = jnp.sum(s_intra, axis=-1, keepdims=True)
            mwa = jnp.max(jnp.where(same, arow, NEG), axis=-1, keepdims=True)
            wk = k * jnp.exp(acol - mwa)
            ch = d * ML_HEADS + h
            c_st = c_scr[ch]
            m_st = m_scr[ch][0:1, 0:1]
            for ci in order:
                rows = slice(ci * ML_CHUNK, (ci + 1) * ML_CHUNK)
                r1 = slice(ci * ML_CHUNK, ci * ML_CHUNK + 1)
                m_inter = bcol[rows] + m_st
                m_i = jnp.maximum(m_intra[rows], m_inter)
                f_i = jnp.exp(m_intra[rows] - m_i)
                inter = jnp.exp(m_inter - m_i)
                qc = _mm(q[rows], c_st)
                numer = inter * qc[:, :ML_DV] + f_i * p_intra[rows]
                denom = inter * qc[:, ML_DV:ML_DV + 1] + f_i * r_intra[rows]
                h_ref[rows, h * ML_DV:(h + 1) * ML_DV] = numer / jnp.maximum(jnp.abs(denom), jnp.exp(-m_i))
                tot_c = tcol[r1]
                mw_c = tot_c + mwa[r1]
                m_new = jnp.maximum(tot_c + m_st, mw_c)
                parts = [zeros_aug] * nchunk
                parts[ci] = v_aug[rows]
                kv = _mm_tn(wk, jnp.concatenate(parts, axis=0))
                c_st = jnp.exp(tot_c + m_st - m_new) * c_st + jnp.exp(mw_c - m_new) * kv
                m_st = m_new
            c_scr[ch] = c_st
            m_scr[ch] = jnp.broadcast_to(m_st, (SUBLANES, LANES))


def _mlstm(pm3, ps3, lane_params):
    b, s, _ = pm3.shape
    nblk = s // BLK
    qw = ML_HEADS * ML_DK
    vw = ML_HEADS * ML_DV
    def specs(tmap):
        blk = lambda j: tmap(j)
        return [
            pl.BlockSpec((None, BLK, qw), lambda i, j: (i, blk(j), _OFF["d_q"] // qw)),
            pl.BlockSpec((None, BLK, qw), lambda i, j: (i, blk(j), _OFF["d_k"] // qw)),
            pl.BlockSpec((None, BLK, vw), lambda i, j: (i, blk(j), _OFF["d_v"] // vw)),
            pl.BlockSpec((None, BLK, LANES), lambda i, j: (i, blk(j), 0)),
        ]
    fwd = lambda j: j
    bwd = lambda j: nblk - 1 - j
    return pl.pallas_call(
        _ml_kernel,
        grid=(b, nblk),
        in_specs=specs(fwd) + specs(bwd) + [pl.BlockSpec((SUBLANES, LANES), lambda i, j: (0, 0))],
        out_specs=[pl.BlockSpec((None, BLK, vw), lambda i, j: (i, j, 0)),
                   pl.BlockSpec((None, BLK, vw), lambda i, j: (i, nblk - 1 - j, 0))],
        out_shape=[jax.ShapeDtypeStruct((b, s, vw), F32)] * 2,
        scratch_shapes=[pltpu.VMEM((2 * ML_HEADS, ML_DK, ML_AUG), F32),
                        pltpu.VMEM((2 * ML_HEADS, SUBLANES, LANES), F32)],
        compiler_params=_cparams(("parallel", "arbitrary")),
        name="mlstm",
    )(pm3, pm3, pm3, ps3, pm3, pm3, pm3, ps3, lane_params)


def _comb_kernel(af_ref, ab_ref, df_ref, db_ref, az_ref, dz_ref, do_ref, ag_ref, dg_ref, ya_ref, yd_ref):
    def head_rms(x, g):
        outs = []
        for h in range(x.shape[-1] // LANES):
            xh = x[:, h * LANES:(h + 1) * LANES]
            ms = jnp.mean(xh * xh, axis=-1, keepdims=True)
            outs.append(xh * lax.rsqrt(ms + EPS) * g)
        return jnp.concatenate(outs, axis=-1)

    ya = head_rms(af_ref[...] + ab_ref[...], ag_ref[...]) * _silu(az_ref[...].astype(F32))
    ya_ref[...] = ya.astype(ya_ref.dtype)
    yd = _sigmoid(do_ref[...].astype(F32)) * head_rms(df_ref[...] + db_ref[...], dg_ref[...])
    yd_ref[...] = (yd * _silu(dz_ref[...].astype(F32))).astype(yd_ref.dtype)


def _combine(af, ab, df, db, pm2, ag, dg, tm=512):
    m, w = af.shape
    tok = pl.BlockSpec((tm, w), lambda i: (i, 0))
    col = lambda name: pl.BlockSpec((tm, w), lambda i: (i, _OFF[name] // w))
    vec = pl.BlockSpec((1, LANES), lambda i: (0, 0))
    return pl.pallas_call(
        _comb_kernel,
        grid=(m // tm,),
        in_specs=[tok, tok, tok, tok, col("a_z"), col("d_z"), col("d_o"), vec, vec],
        out_specs=[tok, tok],
        out_shape=[jax.ShapeDtypeStruct((m, w), BF16)] * 2,
        compiler_params=_cparams(("parallel",)),
        name="combine",
    )(af, ab, df, db, pm2, pm2, pm2, ag, dg)


def _merge_kernel(x_ref, ya_ref, yb_ref, yc_ref, yd_ref, gl_ref, wb_ref, wo_ref, o_ref):
    d = x_ref.shape[-1]
    merged = None
    for i, y_ref in enumerate((ya_ref, yb_ref, yc_ref, yd_ref)):
        proj = jnp.dot(y_ref[...], wb_ref[i], preferred_element_type=F32)
        term = _sigmoid(gl_ref[:, i * d:(i + 1) * d].astype(F32)) * proj
        merged = term if merged is None else merged + term
    o_ref[...] = x_ref[...] + jnp.dot(merged.astype(BF16), wo_ref[...], preferred_element_type=F32)


def _merge(x2d, ya, yb, yc, yd, pm2, wb, wo, tm=256):
    m, d = x2d.shape
    gw = N_BRANCH * d
    ytok = pl.BlockSpec((tm, BRANCH_W), lambda i: (i, 0))
    return pl.pallas_call(
        _merge_kernel,
        grid=(m // tm,),
        in_specs=[
            pl.BlockSpec((tm, d), lambda i: (i, 0)),
            ytok, ytok, ytok, ytok,
            pl.BlockSpec((tm, gw), lambda i: (i, _OFF["gate"] // gw)),
            pl.BlockSpec((N_BRANCH, BRANCH_W, d), lambda i: (0, 0, 0)),
            pl.BlockSpec((d, d), lambda i: (0, 0)),
        ],
        out_specs=pl.BlockSpec((tm, d), lambda i: (i, 0)),
        out_shape=jax.ShapeDtypeStruct((m, d), F32),
        compiler_params=_cparams(("parallel",)),
        name="merge",
    )(x2d, ya, yb, yc, yd, pm2, wb, wo)


def _rope_lane_tables(s):
    t = jnp.arange(s)
    row = (t // GRID_W).astype(F32)
    col = (t % GRID_W).astype(F32)
    m = GA_DH // 4
    inv = ROPE_THETA ** (-jnp.arange(m, dtype=F32) / m)
    ar = row[:, None] * inv
    ac = col[:, None] * inv
    cos_t = jnp.concatenate([jnp.cos(ar), jnp.cos(ar), jnp.cos(ac), jnp.cos(ac)], axis=-1)
    sin_t = jnp.concatenate([-jnp.sin(ar), jnp.sin(ar), -jnp.sin(ac), jnp.sin(ac)], axis=-1)
    return cos_t.astype(F32), sin_t.astype(F32)


def _lane_rows(rows):
    tile = jnp.zeros((SUBLANES, LANES), F32)
    for r, (off, vals) in enumerate(rows):
        vals = vals.reshape(-1).astype(F32)
        tile = tile.at[r, off:off + vals.shape[0]].set(vals)
    return tile


def kernel(x, norm_g, w_in, conv_a, dn_a_log, dn_dt_bias, dn_norm_g, na_q_norm, na_k_norm, na_rpb,
           ga_q_norm, ga_k_norm, ml_i_bias, ml_f_bias, ml_norm_g, w_branch, w_out):
    b, s, d = x.shape
    depth = w_in.shape[0]
    cos_t, sin_t = _rope_lane_tables(s)
    x2 = x.reshape(b * s, d)
    for l in range(depth):
        w = w_in[l]
        w_main = jnp.concatenate([w[:, o:o + wd] for _, o, wd in _MAIN_SEGS], axis=1).astype(BF16)
        w_small = jnp.concatenate([w[:, o:o + 8] for o in _SMALL_SRC]
                                  + [jnp.zeros((d, LANES - 32), F32)], axis=1).astype(BF16)
        pm2, ps2 = _inproj(x2, norm_g[l].reshape(1, d), w_main, w_small)
        pm3 = pm2.reshape(b, s, N_MAIN)
        ps3 = ps2.reshape(b, s, LANES)

        conv8 = jnp.zeros((SUBLANES, conv_a.shape[-1]), F32).at[:DN_CONV].set(conv_a[l])
        dn_lp = _lane_rows([(_L_AA, dn_a_log[l]), (_L_AA, dn_dt_bias[l])])
        o_af, o_ab = _dn_scan(_dn_pre(pm3, ps3, conv8, dn_lp))

        ml_lp = _lane_rows([(_L_DI, ml_i_bias[l]), (_L_DF, ml_f_bias[l])])
        h_df, h_db = _mlstm(pm3, ps3, ml_lp)

        hw = BRANCH_W
        ya, yd = _combine(o_af.reshape(b * s, hw), o_ab.reshape(b * s, hw),
                          h_df.reshape(b * s, hw), h_db.reshape(b * s, hw), pm2,
                          dn_norm_g[l].reshape(1, LANES), ml_norm_g[l].reshape(1, LANES))

        bias = _na_bias_table(na_rpb[l], s // GRID_W)
        yb = _natten(pm3, bias, jnp.tile(na_q_norm[l], 2).reshape(1, 2 * NA_DH),
                     jnp.tile(na_k_norm[l], 2).reshape(1, 2 * NA_DH))
        yc = _gqa(pm3, cos_t, sin_t, ga_q_norm[l].reshape(1, GA_DH), ga_k_norm[l].reshape(1, GA_DH))

        x2 = _merge(x2, ya, yb.reshape(b * s, hw), yc.reshape(b * s, hw), yd, pm2,
                    w_branch[l].astype(BF16), w_out[l].astype(BF16))
    return x2.reshape(b, s, d)
```

```python
import functools
import math

import jax
import jax.numpy as jnp
from jax import lax
from jax.experimental import pallas as pl
from jax.experimental.pallas import tpu as pltpu

F32 = jnp.float32
BF16 = jnp.bfloat16

D_MODEL = 1024
GRID_W = 64
N_BRANCH = 4
BRANCH_W = 512
EPS = 1e-6
DN_HEADS, DN_DK, DN_DV, DN_CONV, DN_CHUNK = 4, 128, 128, 5, 64
NA_HEADS, NA_DH, NA_ROWS, NA_COLS = 8, 64, 8, 16
GA_HEADS, GA_KV_HEADS, GA_DH = 4, 2, 128
ROPE_THETA = 10000.0
ML_HEADS, ML_DK, ML_DV, ML_CHUNK = 4, 64, 128, 64

LANES = 128
SUBLANES = 8
VMEM_LIMIT_BYTES = 56 * 1024 * 1024

_O_A_QKV, _O_A_A, _O_A_B, _O_A_Z = 0, 1536, 1544, 1552
_O_B_QKV, _O_B_Z = 2064, 3600
_O_C_Q, _O_C_K, _O_C_V, _O_C_Z = 4112, 4624, 4880, 5136
_O_D_Q, _O_D_K, _O_D_V, _O_D_I, _O_D_F, _O_D_O, _O_D_Z = 5648, 5904, 6160, 6672, 6680, 6688, 7200
_O_GATE = 7712
_MAIN_SEGS = (
    ("a_qkv", _O_A_QKV, 1536), ("b_qkv", _O_B_QKV, 1536), ("a_z", _O_A_Z, 512), ("b_z", _O_B_Z, 512),
    ("gate", _O_GATE, 4096), ("c_q", _O_C_Q, 512), ("c_k", _O_C_K, 256), ("c_v", _O_C_V, 256),
    ("c_z", _O_C_Z, 512), ("d_q", _O_D_Q, 256), ("d_k", _O_D_K, 256), ("d_v", _O_D_V, 512),
    ("d_o", _O_D_O, 512), ("d_z", _O_D_Z, 512),
)
_OFF = {}
_o = 0
for _name, _src, _w in _MAIN_SEGS:
    _OFF[_name] = _o
    _o += _w
N_MAIN = _o
_SMALL_SRC = (_O_A_A, _O_A_B, _O_D_I, _O_D_F)
_L_AA, _L_AB, _L_DI, _L_DF = 0, 8, 16, 24

P_DTYPE = F32
BLK = 128
NEG = -1e30


def _cparams(sem):
    return pltpu.CompilerParams(dimension_semantics=sem, vmem_limit_bytes=VMEM_LIMIT_BYTES)


def _sigmoid(x):
    return 1.0 / (1.0 + jnp.exp(-x))


def _silu(x):
    return x * _sigmoid(x)


def _softplus(x):
    return jnp.maximum(x, 0.0) + jnp.log(1.0 + jnp.exp(-jnp.abs(x)))


def _mm(a, b):
    return jnp.dot(a.astype(BF16), b.astype(BF16), preferred_element_type=F32)


def _mm_nt(a, b):
    return lax.dot_general(a.astype(BF16), b.astype(BF16), (((1,), (1,)), ((), ())),
                           preferred_element_type=F32)


def _mm_tn(a, b):
    return lax.dot_general(a.astype(BF16), b.astype(BF16), (((0,), (0,)), ((), ())),
                           preferred_element_type=F32)


def _mask_sum(mask, x):
    m = jnp.where(mask, 1.0, 0.0).astype(BF16)
    x1 = x.astype(BF16)
    r1 = x - x1.astype(F32)
    x2 = r1.astype(BF16)
    x3 = (r1 - x2.astype(F32)).astype(BF16)
    dot = lambda v: jnp.dot(m, v, preferred_element_type=F32)
    return dot(x1) + (dot(x2) + dot(x3))


def _chunk_masks(reverse):
    i = lax.broadcasted_iota(jnp.int32, (BLK, BLK), 0)
    j = lax.broadcasted_iota(jnp.int32, (BLK, BLK), 1)
    shift = int(math.log2(DN_CHUNK))
    same = (i >> shift) == (j >> shift)
    if reverse:
        incl = same & (j >= i)
        strict = same & (j > i)
    else:
        incl = same & (j <= i)
        strict = same & (j < i)
    return same, incl, strict, (i == j)


def _as_f32(mask):
    return jnp.where(mask, 1.0, 0.0).astype(F32)


def _inproj_kernel(x_ref, g_ref, w_ref, ws_ref, pm_ref, ps_ref, h_scr):
    @pl.when(pl.program_id(1) == 0)
    def _():
        x = x_ref[...]
        ms = jnp.mean(x * x, axis=-1, keepdims=True)
        h = (x * lax.rsqrt(ms + EPS) * g_ref[...]).astype(BF16)
        h_scr[...] = h
        ps_ref[...] = jnp.dot(h, ws_ref[...], preferred_element_type=F32)

    pm_ref[...] = jnp.dot(h_scr[...], w_ref[...], preferred_element_type=F32).astype(pm_ref.dtype)


def _inproj(x2d, g, w_main, w_small, tm=2048, tn=512):
    m, d = x2d.shape
    tm = min(tm, m)
    return pl.pallas_call(
        _inproj_kernel,
        grid=(m // tm, N_MAIN // tn),
        in_specs=[
            pl.BlockSpec((tm, d), lambda i, j: (i, 0)),
            pl.BlockSpec((1, d), lambda i, j: (0, 0)),
            pl.BlockSpec((d, tn), lambda i, j: (0, j)),
            pl.BlockSpec((d, LANES), lambda i, j: (0, 0)),
        ],
        out_specs=[
            pl.BlockSpec((tm, tn), lambda i, j: (i, j)),
            pl.BlockSpec((tm, LANES), lambda i, j: (i, 0)),
        ],
        out_shape=[jax.ShapeDtypeStruct((m, N_MAIN), P_DTYPE), jax.ShapeDtypeStruct((m, LANES), F32)],
        scratch_shapes=[pltpu.VMEM((tm, d), BF16)],
        compiler_params=_cparams(("parallel", "arbitrary")),
        name="inproj",
    )(x2d, g, w_main, w_small)


GQA_TQ = 512


def _gqa_kernel(q_ref, k_ref, v_ref, z_ref, cos_ref, sin_ref, qg_ref, kg_ref, y_ref,
                q_scr, k_scr, v_scr, o_scr, s0_scr, s1_scr, p0_scr, p1_scr, l0_scr, l1_scr):
    def norm_rope(x, g, cos, sin):
        ms = jnp.mean(x * x, axis=-1, keepdims=True)
        xn = x * lax.rsqrt(ms + EPS) * g
        lane = lax.broadcasted_iota(jnp.int32, xn.shape, 1)
        partner = jnp.where((lane & 63) < 32, pltpu.roll(xn, LANES - 32, 1), pltpu.roll(xn, 32, 1))
        return xn * cos + partner * sin

    s_len = k_ref.shape[0]
    group = GA_HEADS // GA_KV_HEADS
    k_scr[...] = norm_rope(k_ref[...].astype(F32), kg_ref[...], cos_ref[...], sin_ref[...]).astype(BF16)
    v_scr[...] = v_ref[...].astype(BF16)
    for hh in range(group):
        q = norm_rope(q_ref[:, hh * GA_DH:(hh + 1) * GA_DH].astype(F32), qg_ref[...], cos_ref[...], sin_ref[...])
        q_scr[hh * s_len:(hh + 1) * s_len, :] = q.astype(BF16)
    scale = GA_DH ** -0.5
    n_units = group * s_len // GQA_TQ
    s_bufs, p_bufs, l_bufs = (s0_scr, s1_scr), (p0_scr, p1_scr), (l0_scr, l1_scr)

    def unit_rows(u):
        return pl.ds(pl.multiple_of(u * GQA_TQ, GQA_TQ), GQA_TQ)

    def logits(u, slot):
        s_bufs[slot][...] = _mm_nt(q_scr[unit_rows(u), :], k_scr[...]) * scale

    def softmax(slot):
        s = s_bufs[slot][...]
        p = jnp.exp(s - jnp.max(s, axis=-1, keepdims=True))
        l_bufs[slot][...] = jnp.broadcast_to(jnp.sum(p, axis=-1, keepdims=True), (GQA_TQ, GA_DH))
        p_bufs[slot][...] = p.astype(BF16)

    def weighted(u, slot):
        o_scr[unit_rows(u), :] = _mm(p_bufs[slot][...], v_scr[...]) / l_bufs[slot][...]

    logits(0, 0)
    logits(1, 1)
    softmax(0)

    def body(j, carry):
        u = 2 * j
        logits(u + 2, 0)
        softmax(1)
        weighted(u, 0)
        logits(u + 3, 1)
        softmax(0)
        weighted(u + 1, 1)
        return carry

    lax.fori_loop(0, (n_units - 2) // 2, body, 0)
    softmax(1)
    weighted(n_units - 2, 0)
    weighted(n_units - 1, 1)
    o = jnp.concatenate([o_scr[hh * s_len:(hh + 1) * s_len, :] for hh in range(group)], axis=-1)
    y_ref[...] = (o * _silu(z_ref[...].astype(F32))).astype(y_ref.dtype)


def _gqa(pm3, cos_t, sin_t, qg, kg):
    b, s, _ = pm3.shape
    gw = (GA_HEADS // GA_KV_HEADS) * GA_DH
    return pl.pallas_call(
        _gqa_kernel,
        grid=(b, GA_KV_HEADS),
        in_specs=[
            pl.BlockSpec((None, s, gw), lambda i, j: (i, 0, _OFF["c_q"] // gw + j)),
            pl.BlockSpec((None, s, GA_DH), lambda i, j: (i, 0, _OFF["c_k"] // GA_DH + j)),
            pl.BlockSpec((None, s, GA_DH), lambda i, j: (i, 0, _OFF["c_v"] // GA_DH + j)),
            pl.BlockSpec((None, s, gw), lambda i, j: (i, 0, _OFF["c_z"] // gw + j)),
            pl.BlockSpec((s, GA_DH), lambda i, j: (0, 0)),
            pl.BlockSpec((s, GA_DH), lambda i, j: (0, 0)),
            pl.BlockSpec((1, GA_DH), lambda i, j: (0, 0)),
            pl.BlockSpec((1, GA_DH), lambda i, j: (0, 0)),
        ],
        out_specs=pl.BlockSpec((None, s, gw), lambda i, j: (i, 0, j)),
        out_shape=jax.ShapeDtypeStruct((b, s, BRANCH_W), BF16),
        scratch_shapes=[pltpu.VMEM((gw // GA_DH * s, GA_DH), BF16), pltpu.VMEM((s, GA_DH), BF16),
                        pltpu.VMEM((s, GA_DH), BF16), pltpu.VMEM((gw // GA_DH * s, GA_DH), F32),
                        pltpu.VMEM((GQA_TQ, s), F32), pltpu.VMEM((GQA_TQ, s), F32),
                        pltpu.VMEM((GQA_TQ, s), BF16), pltpu.VMEM((GQA_TQ, s), BF16),
                        pltpu.VMEM((GQA_TQ, GA_DH), F32), pltpu.VMEM((GQA_TQ, GA_DH), F32)],
        compiler_params=_cparams(("parallel", "parallel")),
        name="gqa",
    )(pm3, pm3, pm3, pm3, cos_t, sin_t, qg, kg)


NA_ROW_UNROLL = 4


def _na_kernel(q_ref, k_ref, v_ref, z_ref, bias_ref, qg_ref, kg_ref, y_ref, q_scr, k_scr, v_scr, o_scr):
    s_len = q_ref.shape[0]
    rows = s_len // GRID_W
    kr = min(NA_ROWS, rows)
    lane = lax.broadcasted_iota(jnp.int32, (s_len, 2 * NA_DH), 1)
    lo = lane < NA_DH

    def rms_pair(x, g):
        x2 = x * x
        ms_lo = jnp.sum(jnp.where(lo, x2, 0.0), axis=-1, keepdims=True)
        ms_hi = jnp.sum(jnp.where(lo, 0.0, x2), axis=-1, keepdims=True)
        ms = jnp.where(lo, ms_lo, ms_hi) * (1.0 / NA_DH)
        return x * lax.rsqrt(ms + EPS) * g

    qn = rms_pair(q_ref[...].astype(F32), qg_ref[...])
    kn = rms_pair(k_ref[...].astype(F32), kg_ref[...])
    vf = v_ref[...].astype(F32)
    for hh in range(2):
        cols = slice(hh * NA_DH, (hh + 1) * NA_DH)
        q_scr[hh] = qn[:, cols].astype(BF16)
        k_scr[hh] = kn[:, cols].astype(BF16)
        v_scr[hh] = vf[:, cols].astype(BF16)
    scale = NA_DH ** -0.5
    nkeys = kr * GRID_W

    def body(it, carry):
        units = []
        for u in range(NA_ROW_UNROLL):
            r = it * NA_ROW_UNROLL + u
            r0 = jnp.clip(r - kr // 2, 0, rows - kr)
            var = r0 - r + (NA_ROWS - 1)
            qrows = pl.ds(pl.multiple_of(r * GRID_W, GRID_W), GRID_W)
            krows = pl.ds(pl.multiple_of(r0 * GRID_W, GRID_W), nkeys)
            units += [(hh, var, qrows, krows) for hh in range(2)]
        logits = [_mm_nt(q_scr[hh, qrows, :], k_scr[hh, krows, :]) * scale + bias_ref[hh, var]
                  for hh, var, qrows, krows in units]
        probs = [jnp.exp(s - jnp.max(s, axis=-1, keepdims=True)) for s in logits]
        sums = [jnp.sum(p, axis=-1, keepdims=True) for p in probs]
        outs = [_mm(p, v_scr[hh, krows, :]) for p, (hh, _, _, krows) in zip(probs, units)]
        for o, l, (hh, _, qrows, _) in zip(outs, sums, units):
            o_scr[hh, qrows, :] = o / l
        return carry

    lax.fori_loop(0, rows // NA_ROW_UNROLL, body, 0)
    o = jnp.concatenate([o_scr[0], o_scr[1]], axis=-1)
    y_ref[...] = (o * _silu(z_ref[...].astype(F32))).astype(y_ref.dtype)


def _na_bias_table(rpb, rows):
    kr = min(NA_ROWS, rows)
    c = jnp.arange(GRID_W)
    c0 = jnp.clip(c - NA_COLS // 2, 0, GRID_W - NA_COLS)
    in_win = (c[None, :] >= c0[:, None]) & (c[None, :] < c0[:, None] + NA_COLS)
    col_off = jnp.clip(c[None, :] - c[:, None], -(NA_COLS - 1), NA_COLS - 1) + NA_COLS - 1
    t = rpb[:, :, col_off]
    t = jnp.where(in_win[None, None], t, NEG)
    ro = jnp.arange(NA_ROWS)[:, None] + jnp.arange(kr)[None, :]
    tv = t[:, ro]
    tv = tv.transpose(0, 1, 3, 2, 4)
    return tv.reshape(rpb.shape[0], NA_ROWS, GRID_W, kr * GRID_W).astype(F32)


def _natten(pm3, bias, qg, kg):
    b, s, _ = pm3.shape
    pw = 2 * NA_DH
    npair = NA_HEADS // 2
    hw = NA_HEADS * NA_DH
    bias5 = bias.reshape(npair, 2, *bias.shape[1:])
    return pl.pallas_call(
        _na_kernel,
        grid=(npair, b),
        in_specs=[
            pl.BlockSpec((None, s, pw), lambda p, i: (i, 0, _OFF["b_qkv"] // pw + p)),
            pl.BlockSpec((None, s, pw), lambda p, i: (i, 0, (_OFF["b_qkv"] + hw) // pw + p)),
            pl.BlockSpec((None, s, pw), lambda p, i: (i, 0, (_OFF["b_qkv"] + 2 * hw) // pw + p)),
            pl.BlockSpec((None, s, pw), lambda p, i: (i, 0, _OFF["b_z"] // pw + p)),
            pl.BlockSpec((None,) + bias5.shape[1:], lambda p, i: (p, 0, 0, 0, 0)),
            pl.BlockSpec((1, pw), lambda p, i: (0, 0)),
            pl.BlockSpec((1, pw), lambda p, i: (0, 0)),
        ],
        out_specs=pl.BlockSpec((None, s, pw), lambda p, i: (i, 0, p)),
        out_shape=jax.ShapeDtypeStruct((b, s, BRANCH_W), BF16),
        scratch_shapes=[pltpu.VMEM((2, s, NA_DH), BF16), pltpu.VMEM((2, s, NA_DH), BF16),
                        pltpu.VMEM((2, s, NA_DH), BF16), pltpu.VMEM((2, s, NA_DH), F32)],
        compiler_params=_cparams(("parallel", "parallel")),
        name="natten",
    )(pm3, pm3, pm3, pm3, bias5, qg, kg)


INV_BASE = 8


def _inverse_level_masks():
    i = lax.broadcasted_iota(jnp.int32, (BLK, BLK), 0)
    j = lax.broadcasted_iota(jnp.int32, (BLK, BLK), 1)
    same = lambda size: (i >> int(math.log2(size))) == (j >> int(math.log2(size)))
    base = same(INV_BASE)
    joins = []
    size = INV_BASE
    while size < DN_CHUNK:
        joins.append(same(2 * size) & jnp.logical_not(same(size)))
        size *= 2
    return base, joins


def _tri_inverses(l_mats, eyes, level_masks):
    base, joins = level_masks
    ps = [jnp.where(base, -l, 0.0) for l in l_mats]
    ts = [eye + p for eye, p in zip(eyes, ps)]
    for _ in range(int(math.log2(INV_BASE)) - 1):
        ps = [_mm(p, p) for p in ps]
        ts = [t + _mm(t, p) for t, p in zip(ts, ps)]
    for join in joins:
        mids = [_mm(jnp.where(join, l, 0.0), t) for l, t in zip(l_mats, ts)]
        ts = [t - _mm(t, mid) for t, mid in zip(ts, mids)]
    return ts


def _dn_pre_kernel(qkv_ref, sm_ref, cw_ref, lp_ref,
                   uf_ref, ub_ref, wf_ref, wb_ref, qdf_ref, qdb_ref, kdf_ref, kdb_ref,
                   qkf_ref, qkb_ref, gtf_ref, gtb_ref):
    n = pl.program_id(1)
    nblk = pl.num_programs(1)
    s_len = qkv_ref.shape[0]
    halo = SUBLANES if qkv_ref.dtype == F32 else 2 * SUBLANES
    t0 = pl.multiple_of(n * BLK, BLK)
    pstart = pl.multiple_of(jnp.maximum(t0 - halo, 0), halo)
    nstart = pl.multiple_of(jnp.minimum(t0 + BLK, s_len - halo), halo)

    def conv_silu(col0):
        cols = slice(col0, col0 + LANES)
        prev = jnp.where(n > 0, qkv_ref[pl.ds(pstart, halo), cols].astype(F32), 0.0)
        cur = qkv_ref[pl.ds(t0, BLK), cols].astype(F32)
        nxt = jnp.where(n < nblk - 1, qkv_ref[pl.ds(nstart, halo), cols].astype(F32), 0.0)
        xw = jnp.concatenate([prev, cur, nxt], axis=0)
        base = halo - DN_CONV // 2
        acc = xw[base:base + BLK] * cw_ref[0:1, cols]
        for j in range(1, DN_CONV):
            acc = acc + xw[base + j:base + j + BLK] * cw_ref[j:j + 1, cols]
        return _silu(acc)

    def l2n(x):
        return x * lax.rsqrt(jnp.sum(x * x, axis=-1, keepdims=True) + EPS)

    sm = sm_ref[...]
    g_all = -jnp.exp(lp_ref[0:1, :]) * _softplus(sm + lp_ref[1:2, :])
    beta_all = _sigmoid(sm)
    outs = ((uf_ref, wf_ref, qdf_ref, kdf_ref, qkf_ref, gtf_ref),
            (ub_ref, wb_ref, qdb_ref, kdb_ref, qkb_ref, gtb_ref))
    per_dir = []
    for d in range(2):
        same, incl, strict, eye_b = _chunk_masks(reverse=(d == 1))
        gc = _mask_sum(incl, g_all)
        tot = _mask_sum(same, g_all)
        per_dir.append((incl, strict, _as_f32(eye_b), gc, gc.T, tot))
    level_masks = _inverse_level_masks()

    heads = []
    for h in range(DN_HEADS):
        q = l2n(conv_silu(h * DN_DK)) * (DN_DK ** -0.5)
        k = l2n(conv_silu(DN_HEADS * DN_DK + h * DN_DK))
        v = conv_silu(2 * DN_HEADS * DN_DK + h * DN_DV)
        heads.append((q, k, v))
    kks = [_mm_nt(k, k) for _, k, _ in heads]
    qks = [_mm_nt(q, k) for q, k, _ in heads]

    l_mats, rhss = [], []
    for h, (q, k, v) in enumerate(heads):
        hc = slice(h * LANES, (h + 1) * LANES)
        for d in range(2):
            incl, strict, eye_f, gc, gct, tot = per_dir[d]
            u_ref, w_ref, qd_ref, kd_ref, qk_ref, gt_ref = outs[d]
            c = _L_AA + d * DN_HEADS + h
            gcol = gc[:, c:c + 1]
            grow = gct[c:c + 1, :]
            tcol = tot[:, c:c + 1]
            beta = beta_all[:, _L_AB + d * DN_HEADS + h:_L_AB + d * DN_HEADS + h + 1]
            decay = jnp.where(incl, jnp.exp(jnp.where(incl, gcol - grow, 0.0)), 0.0)
            l_mats.append(jnp.where(strict, beta * kks[h] * decay, 0.0))
            egc = jnp.exp(gcol)
            rhss.append(jnp.concatenate([v * beta, k * (beta * egc)], axis=-1).astype(BF16))
            qd_ref[:, hc] = (q * egc).astype(qd_ref.dtype)
            kd_ref[:, hc] = (k * jnp.exp(tcol - gcol)).astype(kd_ref.dtype)
            qk_ref[:, hc] = (qks[h] * decay).astype(qk_ref.dtype)
            gtot = jnp.exp(tcol)
            for ci in range(BLK // DN_CHUNK):
                gt_ref[ci * SUBLANES:(ci + 1) * SUBLANES, hc] = jnp.broadcast_to(
                    gtot[ci * DN_CHUNK:ci * DN_CHUNK + SUBLANES, :], (SUBLANES, LANES))

    t_invs = _tri_inverses(l_mats, [per_dir[d][2] for _ in range(DN_HEADS) for d in range(2)], level_masks)
    sols = [_mm(t, rhs) for t, rhs in zip(t_invs, rhss)]
    for idx, sol in enumerate(sols):
        h, d = divmod(idx, 2)
        hc = slice(h * LANES, (h + 1) * LANES)
        u_ref, w_ref = outs[d][0], outs[d][1]
        u_ref[:, hc] = sol[:, :DN_DV]
        w_ref[:, hc] = sol[:, DN_DV:].astype(w_ref.dtype)


def _dn_pre(pm3, ps3, conv_w8, lane_params):
    b, s, _ = pm3.shape
    nblk = s // BLK
    wq = DN_HEADS * (2 * DN_DK + DN_DV)
    hw = DN_HEADS * LANES
    tok = lambda i, j: (i, j, 0)
    big = lambda dt: jax.ShapeDtypeStruct((b, s, hw), dt)
    gts = jax.ShapeDtypeStruct((b, nblk * 2 * SUBLANES, hw), F32)
    bs_tok = pl.BlockSpec((None, BLK, hw), tok)
    bs_gt = pl.BlockSpec((None, 2 * SUBLANES, hw), tok)
    return pl.pallas_call(
        _dn_pre_kernel,
        grid=(b, nblk),
        in_specs=[
            pl.BlockSpec((None, s, wq), lambda i, j: (i, 0, _OFF["a_qkv"] // wq)),
            pl.BlockSpec((None, BLK, LANES), tok),
            pl.BlockSpec((SUBLANES, wq), lambda i, j: (0, 0)),
            pl.BlockSpec((SUBLANES, LANES), lambda i, j: (0, 0)),
        ],
        out_specs=[bs_tok] * 10 + [bs_gt] * 2,
        out_shape=[big(F32), big(F32), big(BF16), big(BF16), big(BF16), big(BF16), big(BF16), big(BF16),
                   big(BF16), big(BF16), gts, gts],
        compiler_params=_cparams(("parallel", "arbitrary")),
        name="dn_pre",
    )(pm3, ps3, conv_w8, lane_params)


def _dn_scan_kernel(uf_ref, wf_ref, qdf_ref, kdf_ref, qkf_ref, gtf_ref,
                    ub_ref, wb_ref, qdb_ref, kdb_ref, qkb_ref, gtb_ref,
                    of_ref, ob_ref, st_scr):
    @pl.when(pl.program_id(1) == 0)
    def _():
        st_scr[...] = jnp.zeros_like(st_scr)

    nchunk = BLK // DN_CHUNK
    zeros_c = jnp.zeros((DN_CHUNK, DN_DV), F32)
    streams = ((uf_ref, wf_ref, qdf_ref, kdf_ref, qkf_ref, gtf_ref, of_ref, range(nchunk)),
               (ub_ref, wb_ref, qdb_ref, kdb_ref, qkb_ref, gtb_ref, ob_ref, range(nchunk - 1, -1, -1)))
    chains = [(d, h) + streams[d] for d in range(2) for h in range(DN_HEADS)]
    states = [st_scr[d * DN_HEADS + h] for d, h, *_ in chains]
    for step in range(nchunk):
        rs, v_pads = [], []
        for (d, h, u_ref, w_ref, qd_ref, kd_ref, qk_ref, gt_ref, o_ref, order), state in zip(chains, states):
            hc = slice(h * LANES, (h + 1) * LANES)
            rows = slice(order[step] * DN_CHUNK, (order[step] + 1) * DN_CHUNK)
            rs.append(_mm(jnp.concatenate([w_ref[rows, hc], qd_ref[rows, hc]], axis=0), state))
        for (d, h, u_ref, w_ref, qd_ref, kd_ref, qk_ref, gt_ref, o_ref, order), r in zip(chains, rs):
            hc = slice(h * LANES, (h + 1) * LANES)
            ci = order[step]
            rows = slice(ci * DN_CHUNK, (ci + 1) * DN_CHUNK)
            parts = [zeros_c] * nchunk
            parts[ci] = u_ref[rows, hc] - r[:DN_CHUNK]
            v_pads.append(jnp.concatenate(parts, axis=0))
        new_states = []
        for (d, h, u_ref, w_ref, qd_ref, kd_ref, qk_ref, gt_ref, o_ref, order), r, v_pad, state in zip(
                chains, rs, v_pads, states):
            hc = slice(h * LANES, (h + 1) * LANES)
            ci = order[step]
            rows = slice(ci * DN_CHUNK, (ci + 1) * DN_CHUNK)
            o_ref[rows, hc] = r[DN_CHUNK:] + _mm(qk_ref[rows, hc], v_pad)
            gt = gt_ref[ci * SUBLANES:ci * SUBLANES + 1, hc]
            new_states.append(state * gt + _mm_tn(kd_ref[:, hc], v_pad))
        states = new_states
    for (d, h, *_), state in zip(chains, states):
        st_scr[d * DN_HEADS + h] = state


def _dn_scan(pre):
    uf, ub, wf, wb, qdf, qdb, kdf, kdb, qkf, qkb, gtf, gtb = pre
    b, s, hw = uf.shape
    nblk = s // BLK
    fwd = lambda i, j: (i, j, 0)
    bwd = lambda i, j: (i, nblk - 1 - j, 0)
    def specs(imap):
        t = pl.BlockSpec((None, BLK, hw), imap)
        return [t] * 5 + [pl.BlockSpec((None, 2 * SUBLANES, hw), imap)]
    return pl.pallas_call(
        _dn_scan_kernel,
        grid=(b, nblk),
        in_specs=specs(fwd) + specs(bwd),
        out_specs=[pl.BlockSpec((None, BLK, hw), fwd), pl.BlockSpec((None, BLK, hw), bwd)],
        out_shape=[jax.ShapeDtypeStruct((b, s, hw), F32)] * 2,
        scratch_shapes=[pltpu.VMEM((2 * DN_HEADS, DN_DK, DN_DV), F32)],
        compiler_params=_cparams(("parallel", "arbitrary")),
        name="dn_scan",
    )(uf, wf, qdf, kdf, qkf, gtf, ub, wb, qdb, kdb, qkb, gtb)


ML_AUG = 2 * LANES


def _ml_kernel(qf_ref, kf_ref, vf_ref, smf_ref, qb_ref, kb_ref, vb_ref, smb_ref, lp_ref,
               hf_ref, hb_ref, c_scr, m_scr):
    @pl.when(pl.program_id(1) == 0)
    def _():
        c_scr[...] = jnp.zeros_like(c_scr)
        m_scr[...] = jnp.zeros_like(m_scr)

    nchunk = BLK // ML_CHUNK
    lane = lax.broadcasted_iota(jnp.int32, (BLK, LANES), 1)
    ones_col = jnp.where(lane == 0, 1.0, 0.0).astype(BF16)
    zeros_aug = jnp.zeros((ML_CHUNK, ML_AUG), BF16)
    streams = ((qf_ref, kf_ref, vf_ref, smf_ref, hf_ref, range(nchunk)),
               (qb_ref, kb_ref, vb_ref, smb_ref, hb_ref, range(nchunk - 1, -1, -1)))
    gates = []
    for d, (q_ref, k_ref, v_ref, sm_ref, h_ref, order) in enumerate(streams):
        same, incl, _, _ = _chunk_masks(reverse=(d == 1))
        sm = sm_ref[...]
        ig_all = sm + lp_ref[0:1, :]
        x = sm + lp_ref[1:2, :]
        lf_all = jnp.minimum(x, 0.0) - jnp.log(1.0 + jnp.exp(-jnp.abs(x)))
        bc_all = _mask_sum(incl, lf_all)
        tot_all = _mask_sum(same, lf_all)
        a_all = ig_all - pltpu.roll(bc_all, LANES - (_L_DF - _L_DI), 1)
        mwa_all = jnp.concatenate(
            [jnp.broadcast_to(jnp.max(a_all[ci * ML_CHUNK:(ci + 1) * ML_CHUNK], axis=0, keepdims=True),
                              (ML_CHUNK, LANES)) for ci in range(nchunk)], axis=0)
        gates.append((same, incl, bc_all, tot_all, a_all, a_all.T, mwa_all))

    for d, (q_ref, k_ref, v_ref, sm_ref, h_ref, order) in enumerate(streams):
        same, incl, bc_all, tot_all, a_all, a_t, mwa_all = gates[d]
        lanes = lambda col: jnp.broadcast_to(col, (col.shape[0], LANES))
        for h in range(ML_HEADS):
            ci_lane = _L_DI + d * ML_HEADS + h
            cf_lane = _L_DF + d * ML_HEADS + h
            b_l = lanes(bc_all[:, cf_lane:cf_lane + 1])
            tot_l = lanes(tot_all[:, cf_lane:cf_lane + 1])
            a_l = lanes(a_all[:, ci_lane:ci_lane + 1])
            mwa_l = lanes(mwa_all[:, ci_lane:ci_lane + 1])
            arow = a_t[ci_lane:ci_lane + 1, :]
            q = q_ref[:, h * ML_DK:(h + 1) * ML_DK].astype(BF16)
            k = k_ref[:, h * ML_DK:(h + 1) * ML_DK].astype(F32) * (ML_DK ** -0.5)
            v_aug = jnp.concatenate([v_ref[:, h * ML_DV:(h + 1) * ML_DV].astype(BF16), ones_col], axis=-1)
            dlog = jnp.where(incl, b_l + arow, NEG)
            m_intra = lanes(jnp.max(dlog, axis=-1, keepdims=True))
            s_intra = _mm_nt(q, k) * jnp.exp(dlog - m_intra)
            p_intra = _mm(s_intra, v_aug)[:, :ML_DV]
            r_intra = lanes(jnp.sum(s_intra, axis=-1, keepdims=True))
            wk = k * jnp.exp(a_l - mwa_l)[:, :ML_DK]
            ch = d * ML_HEADS + h
            c_st = c_scr[ch]
            m_st = m_scr[ch][0:1, :]
            for ci in order:
                rows = slice(ci * ML_CHUNK, (ci + 1) * ML_CHUNK)
                r1 = slice(ci * ML_CHUNK, ci * ML_CHUNK + 1)
                m_inter = b_l[rows] + m_st
                m_i = jnp.maximum(m_intra[rows], m_inter)
                f_i = jnp.exp(m_intra[rows] - m_i)
                inter = jnp.exp(m_inter - m_i)
                qc = _mm(q[rows], c_st)
                numer = inter * qc[:, :ML_DV] + f_i * p_intra[rows]
                denom = inter * lanes(qc[:, ML_DV:ML_DV + 1]) + f_i * r_intra[rows]
                h_ref[rows, h * ML_DV:(h + 1) * ML_DV] = numer / jnp.maximum(jnp.abs(denom), jnp.exp(-m_i))
                tot_c = tot_l[r1]
                mw_c = tot_c + mwa_l[r1]
                m_new = jnp.maximum(tot_c + m_st, mw_c)
                parts = [zeros_aug] * nchunk
                parts[ci] = v_aug[rows]
                kv = _mm_tn(wk, jnp.concatenate(parts, axis=0))
                dec = jnp.exp(tot_c + m_st - m_new)
                gain = jnp.exp(mw_c - m_new)
                c_st = jnp.concatenate([dec, dec], axis=-1) * c_st + jnp.concatenate([gain, gain], axis=-1) * kv
                m_st = m_new
            c_scr[ch] = c_st
            m_scr[ch] = jnp.broadcast_to(m_st, (SUBLANES, LANES))


def _mlstm(pm3, ps3, lane_params):
    b, s, _ = pm3.shape
    nblk = s // BLK
    qw = ML_HEADS * ML_DK
    vw = ML_HEADS * ML_DV
    def specs(tmap):
        blk = lambda j: tmap(j)
        return [
            pl.BlockSpec((None, BLK, qw), lambda i, j: (i, blk(j), _OFF["d_q"] // qw)),
            pl.BlockSpec((None, BLK, qw), lambda i, j: (i, blk(j), _OFF["d_k"] // qw)),
            pl.BlockSpec((None, BLK, vw), lambda i, j: (i, blk(j), _OFF["d_v"] // vw)),
            pl.BlockSpec((None, BLK, LANES), lambda i, j: (i, blk(j), 0)),
        ]
    fwd = lambda j: j
    bwd = lambda j: nblk - 1 - j
    return pl.pallas_call(
        _ml_kernel,
        grid=(b, nblk),
        in_specs=specs(fwd) + specs(bwd) + [pl.BlockSpec((SUBLANES, LANES), lambda i, j: (0, 0))],
        out_specs=[pl.BlockSpec((None, BLK, vw), lambda i, j: (i, j, 0)),
                   pl.BlockSpec((None, BLK, vw), lambda i, j: (i, nblk - 1 - j, 0))],
        out_shape=[jax.ShapeDtypeStruct((b, s, vw), F32)] * 2,
        scratch_shapes=[pltpu.VMEM((2 * ML_HEADS, ML_DK, ML_AUG), F32),
                        pltpu.VMEM((2 * ML_HEADS, SUBLANES, LANES), F32)],
        compiler_params=_cparams(("parallel", "arbitrary")),
        name="mlstm",
    )(pm3, pm3, pm3, ps3, pm3, pm3, pm3, ps3, lane_params)


def _comb_kernel(af_ref, ab_ref, df_ref, db_ref, az_ref, dz_ref, do_ref, ag_ref, dg_ref, ya_ref, yd_ref):
    def head_rms(x, g):
        outs = []
        for h in range(x.shape[-1] // LANES):
            xh = x[:, h * LANES:(h + 1) * LANES]
            ms = jnp.mean(xh * xh, axis=-1, keepdims=True)
            outs.append(xh * lax.rsqrt(ms + EPS) * g)
        return jnp.concatenate(outs, axis=-1)

    ya = head_rms(af_ref[...] + ab_ref[...], ag_ref[...]) * _silu(az_ref[...].astype(F32))
    ya_ref[...] = ya.astype(ya_ref.dtype)
    yd = _sigmoid(do_ref[...].astype(F32)) * head_rms(df_ref[...] + db_ref[...], dg_ref[...])
    yd_ref[...] = (yd * _silu(dz_ref[...].astype(F32))).astype(yd_ref.dtype)


def _combine(af, ab, df, db, pm2, ag, dg, tm=512):
    m, w = af.shape
    tok = pl.BlockSpec((tm, w), lambda i: (i, 0))
    col = lambda name: pl.BlockSpec((tm, w), lambda i: (i, _OFF[name] // w))
    vec = pl.BlockSpec((1, LANES), lambda i: (0, 0))
    return pl.pallas_call(
        _comb_kernel,
        grid=(m // tm,),
        in_specs=[tok, tok, tok, tok, col("a_z"), col("d_z"), col("d_o"), vec, vec],
        out_specs=[tok, tok],
        out_shape=[jax.ShapeDtypeStruct((m, w), BF16)] * 2,
        compiler_params=_cparams(("parallel",)),
        name="combine",
    )(af, ab, df, db, pm2, pm2, pm2, ag, dg)


def _merge_kernel(x_ref, ya_ref, yb_ref, yc_ref, yd_ref, gl_ref, wb_ref, wo_ref, o_ref):
    d = x_ref.shape[-1]
    merged = None
    for i, y_ref in enumerate((ya_ref, yb_ref, yc_ref, yd_ref)):
        proj = jnp.dot(y_ref[...], wb_ref[i], preferred_element_type=F32)
        term = _sigmoid(gl_ref[:, i * d:(i + 1) * d].astype(F32)) * proj
        merged = term if merged is None else merged + term
    o_ref[...] = x_ref[...] + jnp.dot(merged.astype(BF16), wo_ref[...], preferred_element_type=F32)


def _merge(x2d, ya, yb, yc, yd, pm2, wb, wo, tm=256):
    m, d = x2d.shape
    gw = N_BRANCH * d
    ytok = pl.BlockSpec((tm, BRANCH_W), lambda i: (i, 0))
    return pl.pallas_call(
        _merge_kernel,
        grid=(m // tm,),
        in_specs=[
            pl.BlockSpec((tm, d), lambda i: (i, 0)),
            ytok, ytok, ytok, ytok,
            pl.BlockSpec((tm, gw), lambda i: (i, _OFF["gate"] // gw)),
            pl.BlockSpec((N_BRANCH, BRANCH_W, d), lambda i: (0, 0, 0)),
            pl.BlockSpec((d, d), lambda i: (0, 0)),
        ],
        out_specs=pl.BlockSpec((tm, d), lambda i: (i, 0)),
        out_shape=jax.ShapeDtypeStruct((m, d), F32),
        compiler_params=_cparams(("parallel",)),
        name="merge",
    )(x2d, ya, yb, yc, yd, pm2, wb, wo)


def _rope_lane_tables(s):
    t = jnp.arange(s)
    row = (t // GRID_W).astype(F32)
    col = (t % GRID_W).astype(F32)
    m = GA_DH // 4
    inv = ROPE_THETA ** (-jnp.arange(m, dtype=F32) / m)
    ar = row[:, None] * inv
    ac = col[:, None] * inv
    cos_t = jnp.concatenate([jnp.cos(ar), jnp.cos(ar), jnp.cos(ac), jnp.cos(ac)], axis=-1)
    sin_t = jnp.concatenate([-jnp.sin(ar), jnp.sin(ar), -jnp.sin(ac), jnp.sin(ac)], axis=-1)
    return cos_t.astype(F32), sin_t.astype(F32)


def _lane_rows(rows):
    tile = jnp.zeros((SUBLANES, LANES), F32)
    for r, (off, vals) in enumerate(rows):
        vals = vals.reshape(-1).astype(F32)
        tile = tile.at[r, off:off + vals.shape[0]].set(vals)
    return tile


def kernel(x, norm_g, w_in, conv_a, dn_a_log, dn_dt_bias, dn_norm_g, na_q_norm, na_k_norm, na_rpb,
           ga_q_norm, ga_k_norm, ml_i_bias, ml_f_bias, ml_norm_g, w_branch, w_out):
    b, s, d = x.shape
    depth = w_in.shape[0]
    cos_t, sin_t = _rope_lane_tables(s)
    x2 = x.reshape(b * s, d)
    for l in range(depth):
        w = w_in[l]
        w_main = jnp.concatenate([w[:, o:o + wd] for _, o, wd in _MAIN_SEGS], axis=1).astype(BF16)
        w_small = jnp.concatenate([w[:, o:o + 8] for o in _SMALL_SRC]
                                  + [jnp.zeros((d, LANES - 32), F32)], axis=1).astype(BF16)
        pm2, ps2 = _inproj(x2, norm_g[l].reshape(1, d), w_main, w_small)
        pm3 = pm2.reshape(b, s, N_MAIN)
        ps3 = ps2.reshape(b, s, LANES)

        conv8 = jnp.zeros((SUBLANES, conv_a.shape[-1]), F32).at[:DN_CONV].set(conv_a[l])
        dn_lp = _lane_rows([(_L_AA, dn_a_log[l]), (_L_AA, dn_dt_bias[l])])
        o_af, o_ab = _dn_scan(_dn_pre(pm3, ps3, conv8, dn_lp))

        ml_lp = _lane_rows([(_L_DI, ml_i_bias[l]), (_L_DF, ml_f_bias[l])])
        h_df, h_db = _mlstm(pm3, ps3, ml_lp)

        hw = BRANCH_W
        ya, yd = _combine(o_af.reshape(b * s, hw), o_ab.reshape(b * s, hw),
                          h_df.reshape(b * s, hw), h_db.reshape(b * s, hw), pm2,
                          dn_norm_g[l].reshape(1, LANES), ml_norm_g[l].reshape(1, LANES))

        bias = _na_bias_table(na_rpb[l], s // GRID_W)
        yb = _natten(pm3, bias, jnp.tile(na_q_norm[l], 2).reshape(1, 2 * NA_DH),
                     jnp.tile(na_k_norm[l], 2).reshape(1, 2 * NA_DH))
        yc = _gqa(pm3, cos_t, sin_t, ga_q_norm[l].reshape(1, GA_DH), ga_k_norm[l].reshape(1, GA_DH))

        x2 = _merge(x2, ya, yb.reshape(b * s, hw), yc.reshape(b * s, hw), yd, pm2,
                    w_branch[l].astype(BF16), w_out[l].astype(BF16))
    return x2.reshape(b, s, d)
```

```python
import functools
import math

import jax
import jax.numpy as jnp
from jax import lax
from jax.experimental import pallas as pl
from jax.experimental.pallas import tpu as pltpu

F32 = jnp.float32
BF16 = jnp.bfloat16

D_MODEL = 1024
GRID_W = 64
N_BRANCH = 4
BRANCH_W = 512
EPS = 1e-6
DN_HEADS, DN_DK, DN_DV, DN_CONV, DN_CHUNK = 4, 128, 128, 5, 64
NA_HEADS, NA_DH, NA_ROWS, NA_COLS = 8, 64, 8, 16
GA_HEADS, GA_KV_HEADS, GA_DH = 4, 2, 128
ROPE_THETA = 10000.0
ML_HEADS, ML_DK, ML_DV, ML_CHUNK = 4, 64, 128, 64

LANES = 128
SUBLANES = 8
VMEM_LIMIT_BYTES = 56 * 1024 * 1024

_O_A_QKV, _O_A_A, _O_A_B, _O_A_Z = 0, 1536, 1544, 1552
_O_B_QKV, _O_B_Z = 2064, 3600
_O_C_Q, _O_C_K, _O_C_V, _O_C_Z = 4112, 4624, 4880, 5136
_O_D_Q, _O_D_K, _O_D_V, _O_D_I, _O_D_F, _O_D_O, _O_D_Z = 5648, 5904, 6160, 6672, 6680, 6688, 7200
_O_GATE = 7712
_MAIN_SEGS = (
    ("a_qkv", _O_A_QKV, 1536), ("b_qkv", _O_B_QKV, 1536), ("a_z", _O_A_Z, 512), ("b_z", _O_B_Z, 512),
    ("gate", _O_GATE, 4096), ("c_q", _O_C_Q, 512), ("c_k", _O_C_K, 256), ("c_v", _O_C_V, 256),
    ("c_z", _O_C_Z, 512), ("d_q", _O_D_Q, 256), ("d_k", _O_D_K, 256), ("d_v", _O_D_V, 512),
    ("d_o", _O_D_O, 512), ("d_z", _O_D_Z, 512),
)
_OFF = {}
_o = 0
for _name, _src, _w in _MAIN_SEGS:
    _OFF[_name] = _o
    _o += _w
N_MAIN = _o
_SMALL_SRC = (_O_A_A, _O_A_B, _O_D_I, _O_D_F)
_L_AA, _L_AB, _L_DI, _L_DF = 0, 8, 16, 24

P_DTYPE = BF16
BLK = 128
NEG = -1e30


def _cparams(sem):
    return pltpu.CompilerParams(dimension_semantics=sem, vmem_limit_bytes=VMEM_LIMIT_BYTES)


def _sigmoid(x):
    return 1.0 / (1.0 + jnp.exp(-x))


def _silu(x):
    return x * _sigmoid(x)


def _softplus(x):
    return jnp.maximum(x, 0.0) + jnp.log(1.0 + jnp.exp(-jnp.abs(x)))


def _mm(a, b):
    return jnp.dot(a.astype(BF16), b.astype(BF16), preferred_element_type=F32)


def _mm_nt(a, b):
    return lax.dot_general(a.astype(BF16), b.astype(BF16), (((1,), (1,)), ((), ())),
                           preferred_element_type=F32)


def _mm_tn(a, b):
    return lax.dot_general(a.astype(BF16), b.astype(BF16), (((0,), (0,)), ((), ())),
                           preferred_element_type=F32)


def _mask_sum(mask, x):
    m = jnp.where(mask, 1.0, 0.0).astype(BF16)
    x1 = x.astype(BF16)
    r1 = x - x1.astype(F32)
    x2 = r1.astype(BF16)
    x3 = (r1 - x2.astype(F32)).astype(BF16)
    dot = lambda v: jnp.dot(m, v, preferred_element_type=F32)
    return dot(x1) + (dot(x2) + dot(x3))


def _chunk_masks(reverse):
    i = lax.broadcasted_iota(jnp.int32, (BLK, BLK), 0)
    j = lax.broadcasted_iota(jnp.int32, (BLK, BLK), 1)
    shift = int(math.log2(DN_CHUNK))
    same = (i >> shift) == (j >> shift)
    if reverse:
        incl = same & (j >= i)
        strict = same & (j > i)
    else:
        incl = same & (j <= i)
        strict = same & (j < i)
    return same, incl, strict, (i == j)


def _as_f32(mask):
    return jnp.where(mask, 1.0, 0.0).astype(F32)


def _inproj_kernel(x_ref, g_ref, w_ref, ws_ref, pm_ref, ps_ref, h_scr):
    @pl.when(pl.program_id(1) == 0)
    def _():
        x = x_ref[...]
        ms = jnp.mean(x * x, axis=-1, keepdims=True)
        h = (x * lax.rsqrt(ms + EPS) * g_ref[...]).astype(BF16)
        h_scr[...] = h
        ps_ref[...] = jnp.dot(h, ws_ref[...], preferred_element_type=F32)

    pm_ref[...] = jnp.dot(h_scr[...], w_ref[...], preferred_element_type=F32).astype(pm_ref.dtype)


def _inproj(x2d, g, w_main, w_small, tm=2048, tn=512):
    m, d = x2d.shape
    tm = min(tm, m)
    return pl.pallas_call(
        _inproj_kernel,
        grid=(m // tm, N_MAIN // tn),
        in_specs=[
            pl.BlockSpec((tm, d), lambda i, j: (i, 0)),
            pl.BlockSpec((1, d), lambda i, j: (0, 0)),
            pl.BlockSpec((d, tn), lambda i, j: (0, j)),
            pl.BlockSpec((d, LANES), lambda i, j: (0, 0)),
        ],
        out_specs=[
            pl.BlockSpec((tm, tn), lambda i, j: (i, j)),
            pl.BlockSpec((tm, LANES), lambda i, j: (i, 0)),
        ],
        out_shape=[jax.ShapeDtypeStruct((m, N_MAIN), P_DTYPE), jax.ShapeDtypeStruct((m, LANES), F32)],
        scratch_shapes=[pltpu.VMEM((tm, d), BF16)],
        compiler_params=_cparams(("parallel", "arbitrary")),
        name="inproj",
    )(x2d, g, w_main, w_small)


GQA_TQ = 512


def _gqa_kernel(q_ref, k_ref, v_ref, z_ref, cos_ref, sin_ref, qg_ref, kg_ref, y_ref,
                q_scr, k_scr, v_scr, o_scr, s0_scr, s1_scr, p0_scr, p1_scr, l0_scr, l1_scr):
    def norm_rope(x, g, cos, sin):
        ms = jnp.mean(x * x, axis=-1, keepdims=True)
        xn = x * lax.rsqrt(ms + EPS) * g
        lane = lax.broadcasted_iota(jnp.int32, xn.shape, 1)
        partner = jnp.where((lane & 63) < 32, pltpu.roll(xn, LANES - 32, 1), pltpu.roll(xn, 32, 1))
        return xn * cos + partner * sin

    s_len = k_ref.shape[0]
    group = GA_HEADS // GA_KV_HEADS
    k_scr[...] = norm_rope(k_ref[...].astype(F32), kg_ref[...], cos_ref[...], sin_ref[...]).astype(BF16)
    v_scr[...] = v_ref[...].astype(BF16)
    for hh in range(group):
        q = norm_rope(q_ref[:, hh * GA_DH:(hh + 1) * GA_DH].astype(F32), qg_ref[...], cos_ref[...], sin_ref[...])
        q_scr[hh * s_len:(hh + 1) * s_len, :] = q.astype(BF16)
    scale = GA_DH ** -0.5
    n_units = group * s_len // GQA_TQ
    s_bufs, p_bufs, l_bufs = (s0_scr, s1_scr), (p0_scr, p1_scr), (l0_scr, l1_scr)

    def unit_rows(u):
        return pl.ds(pl.multiple_of(u * GQA_TQ, GQA_TQ), GQA_TQ)

    def logits(u, slot):
        s_bufs[slot][...] = _mm_nt(q_scr[unit_rows(u), :], k_scr[...]) * scale

    def softmax(slot):
        s = s_bufs[slot][...]
        p = jnp.exp(s - jnp.max(s, axis=-1, keepdims=True))
        l_bufs[slot][...] = jnp.broadcast_to(jnp.sum(p, axis=-1, keepdims=True), (GQA_TQ, GA_DH))
        p_bufs[slot][...] = p.astype(BF16)

    def weighted(u, slot):
        o_scr[unit_rows(u), :] = _mm(p_bufs[slot][...], v_scr[...]) / l_bufs[slot][...]

    logits(0, 0)
    logits(1, 1)
    softmax(0)

    def body(j, carry):
        u = 2 * j
        logits(u + 2, 0)
        softmax(1)
        weighted(u, 0)
        logits(u + 3, 1)
        softmax(0)
        weighted(u + 1, 1)
        return carry

    lax.fori_loop(0, (n_units - 2) // 2, body, 0)
    softmax(1)
    weighted(n_units - 2, 0)
    weighted(n_units - 1, 1)
    o = jnp.concatenate([o_scr[hh * s_len:(hh + 1) * s_len, :] for hh in range(group)], axis=-1)
    y_ref[...] = (o * _silu(z_ref[...].astype(F32))).astype(y_ref.dtype)


def _gqa(pm3, cos_t, sin_t, qg, kg):
    b, s, _ = pm3.shape
    gw = (GA_HEADS // GA_KV_HEADS) * GA_DH
    return pl.pallas_call(
        _gqa_kernel,
        grid=(b, GA_KV_HEADS),
        in_specs=[
            pl.BlockSpec((None, s, gw), lambda i, j: (i, 0, _OFF["c_q"] // gw + j)),
            pl.BlockSpec((None, s, GA_DH), lambda i, j: (i, 0, _OFF["c_k"] // GA_DH + j)),
            pl.BlockSpec((None, s, GA_DH), lambda i, j: (i, 0, _OFF["c_v"] // GA_DH + j)),
            pl.BlockSpec((None, s, gw), lambda i, j: (i, 0, _OFF["c_z"] // gw + j)),
            pl.BlockSpec((s, GA_DH), lambda i, j: (0, 0)),
            pl.BlockSpec((s, GA_DH), lambda i, j: (0, 0)),
            pl.BlockSpec((1, GA_DH), lambda i, j: (0, 0)),
            pl.BlockSpec((1, GA_DH), lambda i, j: (0, 0)),
        ],
        out_specs=pl.BlockSpec((None, s, gw), lambda i, j: (i, 0, j)),
        out_shape=jax.ShapeDtypeStruct((b, s, BRANCH_W), BF16),
        scratch_shapes=[pltpu.VMEM((gw // GA_DH * s, GA_DH), BF16), pltpu.VMEM((s, GA_DH), BF16),
                        pltpu.VMEM((s, GA_DH), BF16), pltpu.VMEM((gw // GA_DH * s, GA_DH), F32),
                        pltpu.VMEM((GQA_TQ, s), F32), pltpu.VMEM((GQA_TQ, s), F32),
                        pltpu.VMEM((GQA_TQ, s), BF16), pltpu.VMEM((GQA_TQ, s), BF16),
                        pltpu.VMEM((GQA_TQ, GA_DH), F32), pltpu.VMEM((GQA_TQ, GA_DH), F32)],
        compiler_params=_cparams(("parallel", "parallel")),
        name="gqa",
    )(pm3, pm3, pm3, pm3, cos_t, sin_t, qg, kg)


NA_ROW_UNROLL = 4


def _na_kernel(q_ref, k_ref, v_ref, z_ref, bias_ref, qg_ref, kg_ref, y_ref, q_scr, k_scr, v_scr, o_scr):
    s_len = q_ref.shape[0]
    rows = s_len // GRID_W
    kr = min(NA_ROWS, rows)
    lane = lax.broadcasted_iota(jnp.int32, (s_len, 2 * NA_DH), 1)
    lo = lane < NA_DH

    def rms_pair(x, g):
        x2 = x * x
        ms_lo = jnp.sum(jnp.where(lo, x2, 0.0), axis=-1, keepdims=True)
        ms_hi = jnp.sum(jnp.where(lo, 0.0, x2), axis=-1, keepdims=True)
        ms = jnp.where(lo, ms_lo, ms_hi) * (1.0 / NA_DH)
        return x * lax.rsqrt(ms + EPS) * g

    qn = rms_pair(q_ref[...].astype(F32), qg_ref[...])
    kn = rms_pair(k_ref[...].astype(F32), kg_ref[...])
    vf = v_ref[...].astype(F32)
    for hh in range(2):
        cols = slice(hh * NA_DH, (hh + 1) * NA_DH)
        q_scr[hh] = qn[:, cols].astype(BF16)
        k_scr[hh] = kn[:, cols].astype(BF16)
        v_scr[hh] = vf[:, cols].astype(BF16)
    scale = NA_DH ** -0.5
    nkeys = kr * GRID_W

    def body(it, carry):
        units = []
        for u in range(NA_ROW_UNROLL):
            r = it * NA_ROW_UNROLL + u
            r0 = jnp.clip(r - kr // 2, 0, rows - kr)
            var = r0 - r + (NA_ROWS - 1)
            qrows = pl.ds(pl.multiple_of(r * GRID_W, GRID_W), GRID_W)
            krows = pl.ds(pl.multiple_of(r0 * GRID_W, GRID_W), nkeys)
            units += [(hh, var, qrows, krows) for hh in range(2)]
        logits = [_mm_nt(q_scr[hh, qrows, :], k_scr[hh, krows, :]) * scale + bias_ref[hh, var]
                  for hh, var, qrows, krows in units]
        probs = [jnp.exp(s - jnp.max(s, axis=-1, keepdims=True)) for s in logits]
        sums = [jnp.sum(p, axis=-1, keepdims=True) for p in probs]
        outs = [_mm(p, v_scr[hh, krows, :]) for p, (hh, _, _, krows) in zip(probs, units)]
        for o, l, (hh, _, qrows, _) in zip(outs, sums, units):
            o_scr[hh, qrows, :] = o / l
        return carry

    lax.fori_loop(0, rows // NA_ROW_UNROLL, body, 0)
    o = jnp.concatenate([o_scr[0], o_scr[1]], axis=-1)
    y_ref[...] = (o * _silu(z_ref[...].astype(F32))).astype(y_ref.dtype)


def _na_bias_table(rpb, rows):
    kr = min(NA_ROWS, rows)
    c = jnp.arange(GRID_W)
    c0 = jnp.clip(c - NA_COLS // 2, 0, GRID_W - NA_COLS)
    in_win = (c[None, :] >= c0[:, None]) & (c[None, :] < c0[:, None] + NA_COLS)
    col_off = jnp.clip(c[None, :] - c[:, None], -(NA_COLS - 1), NA_COLS - 1) + NA_COLS - 1
    t = rpb[:, :, col_off]
    t = jnp.where(in_win[None, None], t, NEG)
    ro = jnp.arange(NA_ROWS)[:, None] + jnp.arange(kr)[None, :]
    tv = t[:, ro]
    tv = tv.transpose(0, 1, 3, 2, 4)
    return tv.reshape(rpb.shape[0], NA_ROWS, GRID_W, kr * GRID_W).astype(F32)


def _natten(pm3, bias, qg, kg):
    b, s, _ = pm3.shape
    pw = 2 * NA_DH
    npair = NA_HEADS // 2
    hw = NA_HEADS * NA_DH
    bias5 = bias.reshape(npair, 2, *bias.shape[1:])
    return pl.pallas_call(
        _na_kernel,
        grid=(npair, b),
        in_specs=[
            pl.BlockSpec((None, s, pw), lambda p, i: (i, 0, _OFF["b_qkv"] // pw + p)),
            pl.BlockSpec((None, s, pw), lambda p, i: (i, 0, (_OFF["b_qkv"] + hw) // pw + p)),
            pl.BlockSpec((None, s, pw), lambda p, i: (i, 0, (_OFF["b_qkv"] + 2 * hw) // pw + p)),
            pl.BlockSpec((None, s, pw), lambda p, i: (i, 0, _OFF["b_z"] // pw + p)),
            pl.BlockSpec((None,) + bias5.shape[1:], lambda p, i: (p, 0, 0, 0, 0)),
            pl.BlockSpec((1, pw), lambda p, i: (0, 0)),
            pl.BlockSpec((1, pw), lambda p, i: (0, 0)),
        ],
        out_specs=pl.BlockSpec((None, s, pw), lambda p, i: (i, 0, p)),
        out_shape=jax.ShapeDtypeStruct((b, s, BRANCH_W), BF16),
        scratch_shapes=[pltpu.VMEM((2, s, NA_DH), BF16), pltpu.VMEM((2, s, NA_DH), BF16),
                        pltpu.VMEM((2, s, NA_DH), BF16), pltpu.VMEM((2, s, NA_DH), F32)],
        compiler_params=_cparams(("parallel", "parallel")),
        name="natten",
    )(pm3, pm3, pm3, pm3, bias5, qg, kg)


INV_BASE = 8


def _inverse_level_masks():
    i = lax.broadcasted_iota(jnp.int32, (BLK, BLK), 0)
    j = lax.broadcasted_iota(jnp.int32, (BLK, BLK), 1)
    same = lambda size: (i >> int(math.log2(size))) == (j >> int(math.log2(size)))
    base = same(INV_BASE)
    joins = []
    size = INV_BASE
    while size < DN_CHUNK:
        joins.append(same(2 * size) & jnp.logical_not(same(size)))
        size *= 2
    return base, joins


def _tri_inverses(l_mats, eyes, level_masks):
    base, joins = level_masks
    ps = [jnp.where(base, -l, 0.0) for l in l_mats]
    ts = [eye + p for eye, p in zip(eyes, ps)]
    for _ in range(int(math.log2(INV_BASE)) - 1):
        ps = [_mm(p, p) for p in ps]
        ts = [t + _mm(t, p) for t, p in zip(ts, ps)]
    for join in joins:
        mids = [_mm(jnp.where(join, l, 0.0), t) for l, t in zip(l_mats, ts)]
        ts = [t - _mm(t, mid) for t, mid in zip(ts, mids)]
    return ts


def _dn_pre_kernel(qkv_ref, sm_ref, cw_ref, lp_ref,
                   uf_ref, ub_ref, wf_ref, wb_ref, qdf_ref, qdb_ref, kdf_ref, kdb_ref,
                   qkf_ref, qkb_ref, gtf_ref, gtb_ref):
    n = pl.program_id(1)
    nblk = pl.num_programs(1)
    s_len = qkv_ref.shape[0]
    halo = SUBLANES if qkv_ref.dtype == F32 else 2 * SUBLANES
    t0 = pl.multiple_of(n * BLK, BLK)
    pstart = pl.multiple_of(jnp.maximum(t0 - halo, 0), halo)
    nstart = pl.multiple_of(jnp.minimum(t0 + BLK, s_len - halo), halo)

    def conv_silu(col0):
        cols = slice(col0, col0 + LANES)
        prev = jnp.where(n > 0, qkv_ref[pl.ds(pstart, halo), cols].astype(F32), 0.0)
        cur = qkv_ref[pl.ds(t0, BLK), cols].astype(F32)
        nxt = jnp.where(n < nblk - 1, qkv_ref[pl.ds(nstart, halo), cols].astype(F32), 0.0)
        xw = jnp.concatenate([prev, cur, nxt], axis=0)
        base = halo - DN_CONV // 2
        acc = xw[base:base + BLK] * cw_ref[0:1, cols]
        for j in range(1, DN_CONV):
            acc = acc + xw[base + j:base + j + BLK] * cw_ref[j:j + 1, cols]
        return _silu(acc)

    def l2n(x):
        return x * lax.rsqrt(jnp.sum(x * x, axis=-1, keepdims=True) + EPS)

    sm = sm_ref[...]
    g_all = -jnp.exp(lp_ref[0:1, :]) * _softplus(sm + lp_ref[1:2, :])
    beta_all = _sigmoid(sm)
    outs = ((uf_ref, wf_ref, qdf_ref, kdf_ref, qkf_ref, gtf_ref),
            (ub_ref, wb_ref, qdb_ref, kdb_ref, qkb_ref, gtb_ref))
    per_dir = []
    for d in range(2):
        same, incl, strict, eye_b = _chunk_masks(reverse=(d == 1))
        gc = _mask_sum(incl, g_all)
        tot = _mask_sum(same, g_all)
        per_dir.append((incl, strict, _as_f32(eye_b), gc, gc.T, tot))
    level_masks = _inverse_level_masks()

    heads = []
    for h in range(DN_HEADS):
        q = l2n(conv_silu(h * DN_DK)) * (DN_DK ** -0.5)
        k = l2n(conv_silu(DN_HEADS * DN_DK + h * DN_DK))
        v = conv_silu(2 * DN_HEADS * DN_DK + h * DN_DV)
        heads.append((q, k, v))
    kks = [_mm_nt(k, k) for _, k, _ in heads]
    qks = [_mm_nt(q, k) for q, k, _ in heads]

    l_mats, rhss = [], []
    for h, (q, k, v) in enumerate(heads):
        hc = slice(h * LANES, (h + 1) * LANES)
        for d in range(2):
            incl, strict, eye_f, gc, gct, tot = per_dir[d]
            u_ref, w_ref, qd_ref, kd_ref, qk_ref, gt_ref = outs[d]
            c = _L_AA + d * DN_HEADS + h
            gcol = gc[:, c:c + 1]
            grow = gct[c:c + 1, :]
            tcol = tot[:, c:c + 1]
            beta = beta_all[:, _L_AB + d * DN_HEADS + h:_L_AB + d * DN_HEADS + h + 1]
            decay = jnp.where(incl, jnp.exp(jnp.where(incl, gcol - grow, 0.0)), 0.0)
            l_mats.append(jnp.where(strict, beta * kks[h] * decay, 0.0))
            egc = jnp.exp(gcol)
            rhss.append(jnp.concatenate([v * beta, k * (beta * egc)], axis=-1).astype(BF16))
            qd_ref[:, hc] = (q * egc).astype(qd_ref.dtype)
            kd_ref[:, hc] = (k * jnp.exp(tcol - gcol)).astype(kd_ref.dtype)
            qk_ref[:, hc] = (qks[h] * decay).astype(qk_ref.dtype)
            gtot = jnp.exp(tcol)
            for ci in range(BLK // DN_CHUNK):
                gt_ref[ci * SUBLANES:(ci + 1) * SUBLANES, hc] = jnp.broadcast_to(
                    gtot[ci * DN_CHUNK:ci * DN_CHUNK + SUBLANES, :], (SUBLANES, LANES))

    t_invs = _tri_inverses(l_mats, [per_dir[d][2] for _ in range(DN_HEADS) for d in range(2)], level_masks)
    sols = [_mm(t, rhs) for t, rhs in zip(t_invs, rhss)]
    for idx, sol in enumerate(sols):
        h, d = divmod(idx, 2)
        hc = slice(h * LANES, (h + 1) * LANES)
        u_ref, w_ref = outs[d][0], outs[d][1]
        u_ref[:, hc] = sol[:, :DN_DV]
        w_ref[:, hc] = sol[:, DN_DV:].astype(w_ref.dtype)


def _dn_pre(pm3, ps3, conv_w8, lane_params):
    b, s, _ = pm3.shape
    nblk = s // BLK
    wq = DN_HEADS * (2 * DN_DK + DN_DV)
    hw = DN_HEADS * LANES
    tok = lambda i, j: (i, j, 0)
    big = lambda dt: jax.ShapeDtypeStruct((b, s, hw), dt)
    gts = jax.ShapeDtypeStruct((b, nblk * 2 * SUBLANES, hw), F32)
    bs_tok = pl.BlockSpec((None, BLK, hw), tok)
    bs_gt = pl.BlockSpec((None, 2 * SUBLANES, hw), tok)
    return pl.pallas_call(
        _dn_pre_kernel,
        grid=(b, nblk),
        in_specs=[
            pl.BlockSpec((None, s, wq), lambda i, j: (i, 0, _OFF["a_qkv"] // wq)),
            pl.BlockSpec((None, BLK, LANES), tok),
            pl.BlockSpec((SUBLANES, wq), lambda i, j: (0, 0)),
            pl.BlockSpec((SUBLANES, LANES), lambda i, j: (0, 0)),
        ],
        out_specs=[bs_tok] * 10 + [bs_gt] * 2,
        out_shape=[big(F32), big(F32), big(BF16), big(BF16), big(BF16), big(BF16), big(BF16), big(BF16),
                   big(BF16), big(BF16), gts, gts],
        compiler_params=_cparams(("parallel", "arbitrary")),
        name="dn_pre",
    )(pm3, ps3, conv_w8, lane_params)


def _dn_scan_kernel(uf_ref, wf_ref, qdf_ref, kdf_ref, qkf_ref, gtf_ref,
                    ub_ref, wb_ref, qdb_ref, kdb_ref, qkb_ref, gtb_ref,
                    of_ref, ob_ref, st_scr):
    @pl.when(pl.program_id(1) == 0)
    def _():
        st_scr[...] = jnp.zeros_like(st_scr)

    nchunk = BLK // DN_CHUNK
    zeros_c = jnp.zeros((DN_CHUNK, DN_DV), F32)
    streams = ((uf_ref, wf_ref, qdf_ref, kdf_ref, qkf_ref, gtf_ref, of_ref, range(nchunk)),
               (ub_ref, wb_ref, qdb_ref, kdb_ref, qkb_ref, gtb_ref, ob_ref, range(nchunk - 1, -1, -1)))
    chains = [(d, h) + streams[d] for d in range(2) for h in range(DN_HEADS)]
    states = [st_scr[d * DN_HEADS + h] for d, h, *_ in chains]
    for step in range(nchunk):
        rs, v_pads = [], []
        for (d, h, u_ref, w_ref, qd_ref, kd_ref, qk_ref, gt_ref, o_ref, order), state in zip(chains, states):
            hc = slice(h * LANES, (h + 1) * LANES)
            rows = slice(order[step] * DN_CHUNK, (order[step] + 1) * DN_CHUNK)
            rs.append(_mm(jnp.concatenate([w_ref[rows, hc], qd_ref[rows, hc]], axis=0), state))
        for (d, h, u_ref, w_ref, qd_ref, kd_ref, qk_ref, gt_ref, o_ref, order), r in zip(chains, rs):
            hc = slice(h * LANES, (h + 1) * LANES)
            ci = order[step]
            rows = slice(ci * DN_CHUNK, (ci + 1) * DN_CHUNK)
            parts = [zeros_c] * nchunk
            parts[ci] = u_ref[rows, hc] - r[:DN_CHUNK]
            v_pads.append(jnp.concatenate(parts, axis=0))
        new_states = []
        for (d, h, u_ref, w_ref, qd_ref, kd_ref, qk_ref, gt_ref, o_ref, order), r, v_pad, state in zip(
                chains, rs, v_pads, states):
            hc = slice(h * LANES, (h + 1) * LANES)
            ci = order[step]
            rows = slice(ci * DN_CHUNK, (ci + 1) * DN_CHUNK)
            o_ref[rows, hc] = r[DN_CHUNK:] + _mm(qk_ref[rows, hc], v_pad)
            gt = gt_ref[ci * SUBLANES:ci * SUBLANES + 1, hc]
            new_states.append(state * gt + _mm_tn(kd_ref[:, hc], v_pad))
        states = new_states
    for (d, h, *_), state in zip(chains, states):
        st_scr[d * DN_HEADS + h] = state


def _dn_scan(pre):
    uf, ub, wf, wb, qdf, qdb, kdf, kdb, qkf, qkb, gtf, gtb = pre
    b, s, hw = uf.shape
    nblk = s // BLK
    fwd = lambda i, j: (i, j, 0)
    bwd = lambda i, j: (i, nblk - 1 - j, 0)
    def specs(imap):
        t = pl.BlockSpec((None, BLK, hw), imap)
        return [t] * 5 + [pl.BlockSpec((None, 2 * SUBLANES, hw), imap)]
    return pl.pallas_call(
        _dn_scan_kernel,
        grid=(b, nblk),
        in_specs=specs(fwd) + specs(bwd),
        out_specs=[pl.BlockSpec((None, BLK, hw), fwd), pl.BlockSpec((None, BLK, hw), bwd)],
        out_shape=[jax.ShapeDtypeStruct((b, s, hw), F32)] * 2,
        scratch_shapes=[pltpu.VMEM((2 * DN_HEADS, DN_DK, DN_DV), F32)],
        compiler_params=_cparams(("parallel", "arbitrary")),
        name="dn_scan",
    )(uf, wf, qdf, kdf, qkf, gtf, ub, wb, qdb, kdb, qkb, gtb)


ML_AUG = 2 * LANES


def _ml_kernel(qf_ref, kf_ref, vf_ref, smf_ref, qb_ref, kb_ref, vb_ref, smb_ref, lp_ref,
               hf_ref, hb_ref, c_scr, m_scr):
    @pl.when(pl.program_id(1) == 0)
    def _():
        c_scr[...] = jnp.zeros_like(c_scr)
        m_scr[...] = jnp.zeros_like(m_scr)

    nchunk = BLK // ML_CHUNK
    lane = lax.broadcasted_iota(jnp.int32, (BLK, LANES), 1)
    ones_col = jnp.where(lane == 0, 1.0, 0.0).astype(BF16)
    zeros_aug = jnp.zeros((ML_CHUNK, ML_AUG), BF16)
    streams = ((qf_ref, kf_ref, vf_ref, smf_ref, hf_ref, range(nchunk)),
               (qb_ref, kb_ref, vb_ref, smb_ref, hb_ref, range(nchunk - 1, -1, -1)))
    gates = []
    for d, (q_ref, k_ref, v_ref, sm_ref, h_ref, order) in enumerate(streams):
        same, incl, _, _ = _chunk_masks(reverse=(d == 1))
        sm = sm_ref[...]
        ig_all = sm + lp_ref[0:1, :]
        x = sm + lp_ref[1:2, :]
        lf_all = jnp.minimum(x, 0.0) - jnp.log(1.0 + jnp.exp(-jnp.abs(x)))
        lf_all = pltpu.roll(lf_all, LANES - (_L_DF - _L_DI), 1)
        bc_all = _mask_sum(incl, lf_all)
        tot_all = _mask_sum(same, lf_all)
        a_all = ig_all - bc_all
        mwa_all = jnp.concatenate(
            [jnp.broadcast_to(jnp.max(a_all[ci * ML_CHUNK:(ci + 1) * ML_CHUNK], axis=0, keepdims=True),
                              (ML_CHUNK, LANES)) for ci in range(nchunk)], axis=0)
        gates.append((same, incl, bc_all, tot_all, a_all.T, jnp.exp(a_all - mwa_all), tot_all + mwa_all))

    lanes = lambda col: jnp.broadcast_to(col, (col.shape[0], LANES))
    chains = [(d, h) for d in range(2) for h in range(ML_HEADS)]
    ins = []
    for d, h in chains:
        q_ref, k_ref, v_ref = streams[d][0], streams[d][1], streams[d][2]
        q = q_ref[:, h * ML_DK:(h + 1) * ML_DK].astype(BF16)
        k = k_ref[:, h * ML_DK:(h + 1) * ML_DK].astype(F32) * (ML_DK ** -0.5)
        v_aug = jnp.concatenate([v_ref[:, h * ML_DV:(h + 1) * ML_DV].astype(BF16), ones_col], axis=-1)
        ins.append((q, k, v_aug))
    qks = [_mm_nt(q, k) for q, k, _ in ins]

    mids = []
    for (d, h), (q, k, v_aug), qk in zip(chains, ins, qks):
        same, incl, bc_all, tot_all, a_t, w_all, mw_all = gates[d]
        c = _L_DI + d * ML_HEADS + h
        b_l = lanes(bc_all[:, c:c + 1])
        dlog = jnp.where(incl, b_l + a_t[c:c + 1, :], NEG)
        m_intra = lanes(jnp.max(dlog, axis=-1, keepdims=True))
        s_intra = qk * jnp.exp(dlog - m_intra)
        r_intra = lanes(jnp.sum(s_intra, axis=-1, keepdims=True))
        wk = (k * lanes(w_all[:, c:c + 1])[:, :ML_DK]).astype(BF16)
        mids.append((b_l, m_intra, s_intra.astype(BF16), r_intra, wk))
    p_intras = [_mm(s_b, v_aug)[:, :ML_DV] for (_, _, s_b, _, _), (_, _, v_aug) in zip(mids, ins)]
    kvs = []
    for (_, _, _, _, wk), (_, _, v_aug) in zip(mids, ins):
        per_chunk = []
        for ci in range(nchunk):
            parts = [zeros_aug] * nchunk
            parts[ci] = v_aug[ci * ML_CHUNK:(ci + 1) * ML_CHUNK]
            per_chunk.append(_mm_tn(wk, jnp.concatenate(parts, axis=0)))
        kvs.append(per_chunk)

    c_sts = [c_scr[d * ML_HEADS + h] for d, h in chains]
    m_sts = [m_scr[d * ML_HEADS + h][0:1, :] for d, h in chains]
    for step in range(nchunk):
        qcs = []
        for (d, h), (q, _, _), c_st in zip(chains, ins, c_sts):
            ci = streams[d][5][step]
            qcs.append(_mm(q[ci * ML_CHUNK:(ci + 1) * ML_CHUNK], c_st))
        for idx, (d, h) in enumerate(chains):
            tot_all, mw_all = gates[d][3], gates[d][6]
            b_l, m_intra, _, r_intra, _ = mids[idx]
            c = _L_DI + d * ML_HEADS + h
            h_ref = streams[d][4]
            ci = streams[d][5][step]
            rows = slice(ci * ML_CHUNK, (ci + 1) * ML_CHUNK)
            r8 = slice(ci * ML_CHUNK, ci * ML_CHUNK + SUBLANES)
            m_st, c_st, qc = m_sts[idx], c_sts[idx], qcs[idx]
            m_inter = b_l[rows] + m_st
            m_i = jnp.maximum(m_intra[rows], m_inter)
            f_i = jnp.exp(m_intra[rows] - m_i)
            inter = jnp.exp(m_inter - m_i)
            numer = inter * qc[:, :ML_DV] + f_i * p_intras[idx][rows]
            denom = inter * lanes(qc[:, ML_DV:ML_DV + 1]) + f_i * r_intra[rows]
            h_ref[rows, h * ML_DV:(h + 1) * ML_DV] = numer / jnp.maximum(jnp.abs(denom), jnp.exp(-m_i))
            tot_c = lanes(tot_all[r8, c:c + 1])[0:1]
            mw_c = lanes(mw_all[r8, c:c + 1])[0:1]
            m_new = jnp.maximum(tot_c + m_st, mw_c)
            dec = jnp.exp(tot_c + m_st - m_new)
            gain = jnp.exp(mw_c - m_new)
            c_sts[idx] = (jnp.concatenate([dec, dec], axis=-1) * c_st
                          + jnp.concatenate([gain, gain], axis=-1) * kvs[idx][ci])
            m_sts[idx] = m_new
    for idx, (d, h) in enumerate(chains):
        c_scr[d * ML_HEADS + h] = c_sts[idx]
        m_scr[d * ML_HEADS + h] = jnp.broadcast_to(m_sts[idx], (SUBLANES, LANES))


def _mlstm(pm3, ps3, lane_params):
    b, s, _ = pm3.shape
    nblk = s // BLK
    qw = ML_HEADS * ML_DK
    vw = ML_HEADS * ML_DV
    def specs(tmap):
        blk = lambda j: tmap(j)
        return [
            pl.BlockSpec((None, BLK, qw), lambda i, j: (i, blk(j), _OFF["d_q"] // qw)),
            pl.BlockSpec((None, BLK, qw), lambda i, j: (i, blk(j), _OFF["d_k"] // qw)),
            pl.BlockSpec((None, BLK, vw), lambda i, j: (i, blk(j), _OFF["d_v"] // vw)),
            pl.BlockSpec((None, BLK, LANES), lambda i, j: (i, blk(j), 0)),
        ]
    fwd = lambda j: j
    bwd = lambda j: nblk - 1 - j
    return pl.pallas_call(
        _ml_kernel,
        grid=(b, nblk),
        in_specs=specs(fwd) + specs(bwd) + [pl.BlockSpec((SUBLANES, LANES), lambda i, j: (0, 0))],
        out_specs=[pl.BlockSpec((None, BLK, vw), lambda i, j: (i, j, 0)),
                   pl.BlockSpec((None, BLK, vw), lambda i, j: (i, nblk - 1 - j, 0))],
        out_shape=[jax.ShapeDtypeStruct((b, s, vw), F32)] * 2,
        scratch_shapes=[pltpu.VMEM((2 * ML_HEADS, ML_DK, ML_AUG), F32),
                        pltpu.VMEM((2 * ML_HEADS, SUBLANES, LANES), F32)],
        compiler_params=_cparams(("parallel", "arbitrary")),
        name="mlstm",
    )(pm3, pm3, pm3, ps3, pm3, pm3, pm3, ps3, lane_params)


def _comb_kernel(af_ref, ab_ref, df_ref, db_ref, az_ref, dz_ref, do_ref, ag_ref, dg_ref, ya_ref, yd_ref):
    def head_rms(x, g):
        outs = []
        for h in range(x.shape[-1] // LANES):
            xh = x[:, h * LANES:(h + 1) * LANES]
            ms = jnp.mean(xh * xh, axis=-1, keepdims=True)
            outs.append(xh * lax.rsqrt(ms + EPS) * g)
        return jnp.concatenate(outs, axis=-1)

    ya = head_rms(af_ref[...] + ab_ref[...], ag_ref[...]) * _silu(az_ref[...].astype(F32))
    ya_ref[...] = ya.astype(ya_ref.dtype)
    yd = _sigmoid(do_ref[...].astype(F32)) * head_rms(df_ref[...] + db_ref[...], dg_ref[...])
    yd_ref[...] = (yd * _silu(dz_ref[...].astype(F32))).astype(yd_ref.dtype)


def _combine(af, ab, df, db, pm2, ag, dg, tm=512):
    m, w = af.shape
    tok = pl.BlockSpec((tm, w), lambda i: (i, 0))
    col = lambda name: pl.BlockSpec((tm, w), lambda i: (i, _OFF[name] // w))
    vec = pl.BlockSpec((1, LANES), lambda i: (0, 0))
    return pl.pallas_call(
        _comb_kernel,
        grid=(m // tm,),
        in_specs=[tok, tok, tok, tok, col("a_z"), col("d_z"), col("d_o"), vec, vec],
        out_specs=[tok, tok],
        out_shape=[jax.ShapeDtypeStruct((m, w), BF16)] * 2,
        compiler_params=_cparams(("parallel",)),
        name="combine",
    )(af, ab, df, db, pm2, pm2, pm2, ag, dg)


def _merge_kernel(x_ref, ya_ref, yb_ref, yc_ref, yd_ref, gl_ref, wb_ref, wo_ref, o_ref):
    d = x_ref.shape[-1]
    merged = None
    for i, y_ref in enumerate((ya_ref, yb_ref, yc_ref, yd_ref)):
        proj = jnp.dot(y_ref[...], wb_ref[i], preferred_element_type=F32)
        term = _sigmoid(gl_ref[:, i * d:(i + 1) * d].astype(F32)) * proj
        merged = term if merged is None else merged + term
    o_ref[...] = x_ref[...] + jnp.dot(merged.astype(BF16), wo_ref[...], preferred_element_type=F32)


def _merge(x2d, ya, yb, yc, yd, pm2, wb, wo, tm=256):
    m, d = x2d.shape
    gw = N_BRANCH * d
    ytok = pl.BlockSpec((tm, BRANCH_W), lambda i: (i, 0))
    return pl.pallas_call(
        _merge_kernel,
        grid=(m // tm,),
        in_specs=[
            pl.BlockSpec((tm, d), lambda i: (i, 0)),
            ytok, ytok, ytok, ytok,
            pl.BlockSpec((tm, gw), lambda i: (i, _OFF["gate"] // gw)),
            pl.BlockSpec((N_BRANCH, BRANCH_W, d), lambda i: (0, 0, 0)),
            pl.BlockSpec((d, d), lambda i: (0, 0)),
        ],
        out_specs=pl.BlockSpec((tm, d), lambda i: (i, 0)),
        out_shape=jax.ShapeDtypeStruct((m, d), F32),
        compiler_params=_cparams(("parallel",)),
        name="merge",
    )(x2d, ya, yb, yc, yd, pm2, wb, wo)


def _rope_lane_tables(s):
    t = jnp.arange(s)
    row = (t // GRID_W).astype(F32)
    col = (t % GRID_W).astype(F32)
    m = GA_DH // 4
    inv = ROPE_THETA ** (-jnp.arange(m, dtype=F32) / m)
    ar = row[:, None] * inv
    ac = col[:, None] * inv
    cos_t = jnp.concatenate([jnp.cos(ar), jnp.cos(ar), jnp.cos(ac), jnp.cos(ac)], axis=-1)
    sin_t = jnp.concatenate([-jnp.sin(ar), jnp.sin(ar), -jnp.sin(ac), jnp.sin(ac)], axis=-1)
    return cos_t.astype(F32), sin_t.astype(F32)


def _lane_rows(rows):
    tile = jnp.zeros((SUBLANES, LANES), F32)
    for r, (off, vals) in enumerate(rows):
        vals = vals.reshape(-1).astype(F32)
        tile = tile.at[r, off:off + vals.shape[0]].set(vals)
    return tile


def kernel(x, norm_g, w_in, conv_a, dn_a_log, dn_dt_bias, dn_norm_g, na_q_norm, na_k_norm, na_rpb,
           ga_q_norm, ga_k_norm, ml_i_bias, ml_f_bias, ml_norm_g, w_branch, w_out):
    b, s, d = x.shape
    depth = w_in.shape[0]
    cos_t, sin_t = _rope_lane_tables(s)
    x2 = x.reshape(b * s, d)
    for l in range(depth):
        w = w_in[l]
        w_main = jnp.concatenate([w[:, o:o + wd] for _, o, wd in _MAIN_SEGS], axis=1).astype(BF16)
        w_small = jnp.concatenate([w[:, o:o + 8] for o in _SMALL_SRC]
                                  + [jnp.zeros((d, LANES - 32), F32)], axis=1).astype(BF16)
        pm2, ps2 = _inproj(x2, norm_g[l].reshape(1, d), w_main, w_small)
        pm3 = pm2.reshape(b, s, N_MAIN)
        ps3 = ps2.reshape(b, s, LANES)

        conv8 = jnp.zeros((SUBLANES, conv_a.shape[-1]), F32).at[:DN_CONV].set(conv_a[l])
        dn_lp = _lane_rows([(_L_AA, dn_a_log[l]), (_L_AA, dn_dt_bias[l])])
        o_af, o_ab = _dn_scan(_dn_pre(pm3, ps3, conv8, dn_lp))

        ml_lp = _lane_rows([(_L_DI, ml_i_bias[l]), (_L_DF, ml_f_bias[l])])
        h_df, h_db = _mlstm(pm3, ps3, ml_lp)

        hw = BRANCH_W
        ya, yd = _combine(o_af.reshape(b * s, hw), o_ab.reshape(b * s, hw),
                          h_df.reshape(b * s, hw), h_db.reshape(b * s, hw), pm2,
                          dn_norm_g[l].reshape(1, LANES), ml_norm_g[l].reshape(1, LANES))

        bias = _na_bias_table(na_rpb[l], s // GRID_W)
        yb = _natten(pm3, bias, jnp.tile(na_q_norm[l], 2).reshape(1, 2 * NA_DH),
                     jnp.tile(na_k_norm[l], 2).reshape(1, 2 * NA_DH))
        yc = _gqa(pm3, cos_t, sin_t, ga_q_norm[l].reshape(1, GA_DH), ga_k_norm[l].reshape(1, GA_DH))

        x2 = _merge(x2, ya, yb.reshape(b * s, hw), yc.reshape(b * s, hw), yd, pm2,
                    w_branch[l].astype(BF16), w_out[l].astype(BF16))
    return x2.reshape(b, s, d)
```

```python
import functools
import math

import jax
import jax.numpy as jnp
from jax import lax
from jax.experimental import pallas as pl
from jax.experimental.pallas import tpu as pltpu

F32 = jnp.float32
BF16 = jnp.bfloat16

D_MODEL = 1024
GRID_W = 64
N_BRANCH = 4
BRANCH_W = 512
EPS = 1e-6
DN_HEADS, DN_DK, DN_DV, DN_CONV, DN_CHUNK = 4, 128, 128, 5, 64
NA_HEADS, NA_DH, NA_ROWS, NA_COLS = 8, 64, 8, 16
GA_HEADS, GA_KV_HEADS, GA_DH = 4, 2, 128
ROPE_THETA = 10000.0
ML_HEADS, ML_DK, ML_DV, ML_CHUNK = 4, 64, 128, 64

LANES = 128
SUBLANES = 8
VMEM_LIMIT_BYTES = 56 * 1024 * 1024

_O_A_QKV, _O_A_A, _O_A_B, _O_A_Z = 0, 1536, 1544, 1552
_O_B_QKV, _O_B_Z = 2064, 3600
_O_C_Q, _O_C_K, _O_C_V, _O_C_Z = 4112, 4624, 4880, 5136
_O_D_Q, _O_D_K, _O_D_V, _O_D_I, _O_D_F, _O_D_O, _O_D_Z = 5648, 5904, 6160, 6672, 6680, 6688, 7200
_O_GATE = 7712
_MAIN_SEGS = (
    ("a_qkv", _O_A_QKV, 1536), ("b_qkv", _O_B_QKV, 1536), ("a_z", _O_A_Z, 512), ("b_z", _O_B_Z, 512),
    ("gate", _O_GATE, 4096), ("c_q", _O_C_Q, 512), ("c_k", _O_C_K, 256), ("c_v", _O_C_V, 256),
    ("c_z", _O_C_Z, 512), ("d_q", _O_D_Q, 256), ("d_k", _O_D_K, 256), ("d_v", _O_D_V, 512),
    ("d_o", _O_D_O, 512), ("d_z", _O_D_Z, 512),
)
_OFF = {}
_o = 0
for _name, _src, _w in _MAIN_SEGS:
    _OFF[_name] = _o
    _o += _w
N_MAIN = _o
_SMALL_SRC = (_O_A_A, _O_A_B, _O_D_I, _O_D_F)
_L_AA, _L_AB, _L_DI, _L_DF = 0, 8, 16, 24

P_DTYPE = BF16
BLK = 128
NEG = -1e30


def _cparams(sem):
    return pltpu.CompilerParams(dimension_semantics=sem, vmem_limit_bytes=VMEM_LIMIT_BYTES)


def _sigmoid(x):
    return 1.0 / (1.0 + jnp.exp(-x))


def _silu(x):
    return x * _sigmoid(x)


def _softplus(x):
    return jnp.maximum(x, 0.0) + jnp.log(1.0 + jnp.exp(-jnp.abs(x)))


def _mm(a, b):
    return jnp.dot(a.astype(BF16), b.astype(BF16), preferred_element_type=F32)


def _mm_nt(a, b):
    return lax.dot_general(a.astype(BF16), b.astype(BF16), (((1,), (1,)), ((), ())),
                           preferred_element_type=F32)


def _mm_tn(a, b):
    return lax.dot_general(a.astype(BF16), b.astype(BF16), (((0,), (0,)), ((), ())),
                           preferred_element_type=F32)


def _mask_sum(mask, x):
    m = jnp.where(mask, 1.0, 0.0).astype(BF16)
    x1 = x.astype(BF16)
    r1 = x - x1.astype(F32)
    x2 = r1.astype(BF16)
    x3 = (r1 - x2.astype(F32)).astype(BF16)
    dot = lambda v: jnp.dot(m, v, preferred_element_type=F32)
    return dot(x1) + (dot(x2) + dot(x3))


def _chunk_masks(reverse):
    i = lax.broadcasted_iota(jnp.int32, (BLK, BLK), 0)
    j = lax.broadcasted_iota(jnp.int32, (BLK, BLK), 1)
    shift = int(math.log2(DN_CHUNK))
    same = (i >> shift) == (j >> shift)
    if reverse:
        incl = same & (j >= i)
        strict = same & (j > i)
    else:
        incl = same & (j <= i)
        strict = same & (j < i)
    return same, incl, strict, (i == j)


def _as_f32(mask):
    return jnp.where(mask, 1.0, 0.0).astype(F32)


def _inproj_kernel(x_ref, g_ref, w_ref, ws_ref, pm_ref, ps_ref, h_scr):
    @pl.when(pl.program_id(1) == 0)
    def _():
        x = x_ref[...]
        ms = jnp.mean(x * x, axis=-1, keepdims=True)
        h = (x * lax.rsqrt(ms + EPS) * g_ref[...]).astype(BF16)
        h_scr[...] = h
        ps_ref[...] = jnp.dot(h, ws_ref[...], preferred_element_type=F32)

    pm_ref[...] = jnp.dot(h_scr[...], w_ref[...], preferred_element_type=F32).astype(pm_ref.dtype)


def _inproj(x2d, g, w_main, w_small, tm=2048, tn=512):
    m, d = x2d.shape
    tm = min(tm, m)
    return pl.pallas_call(
        _inproj_kernel,
        grid=(m // tm, N_MAIN // tn),
        in_specs=[
            pl.BlockSpec((tm, d), lambda i, j: (i, 0)),
            pl.BlockSpec((1, d), lambda i, j: (0, 0)),
            pl.BlockSpec((d, tn), lambda i, j: (0, j)),
            pl.BlockSpec((d, LANES), lambda i, j: (0, 0)),
        ],
        out_specs=[
            pl.BlockSpec((tm, tn), lambda i, j: (i, j)),
            pl.BlockSpec((tm, LANES), lambda i, j: (i, 0)),
        ],
        out_shape=[jax.ShapeDtypeStruct((m, N_MAIN), P_DTYPE), jax.ShapeDtypeStruct((m, LANES), F32)],
        scratch_shapes=[pltpu.VMEM((tm, d), BF16)],
        compiler_params=_cparams(("parallel", "arbitrary")),
        name="inproj",
    )(x2d, g, w_main, w_small)


GQA_TQ = 512


def _gqa_kernel(q_ref, k_ref, v_ref, z_ref, cos_ref, sin_ref, qg_ref, kg_ref, y_ref,
                q_scr, k_scr, v_scr, s0_scr, s1_scr, p0_scr, p1_scr, l0_scr, l1_scr):
    def norm_rope(x, g, cos, sin):
        ms = jnp.mean(x * x, axis=-1, keepdims=True)
        xn = x * lax.rsqrt(ms + EPS) * g
        lane = lax.broadcasted_iota(jnp.int32, xn.shape, 1)
        partner = jnp.where((lane & 63) < 32, pltpu.roll(xn, LANES - 32, 1), pltpu.roll(xn, 32, 1))
        return xn * cos + partner * sin

    s_len = k_ref.shape[0]
    group = GA_HEADS // GA_KV_HEADS
    k_scr[...] = norm_rope(k_ref[...].astype(F32), kg_ref[...], cos_ref[...], sin_ref[...]).astype(BF16)
    v_scr[...] = v_ref[...].astype(BF16)
    assert group == 2
    scale = GA_DH ** -0.5
    n_blk = s_len // GQA_TQ
    s_bufs, p_bufs, l_bufs = (s0_scr, s1_scr), (p0_scr, p1_scr), (l0_scr, l1_scr)

    def stacked(head, blk):
        return pl.ds(pl.multiple_of(head * s_len + blk * GQA_TQ, GQA_TQ), GQA_TQ)

    def prep(head, blk):
        rows = pl.ds(pl.multiple_of(blk * GQA_TQ, GQA_TQ), GQA_TQ)
        q = norm_rope(q_ref[rows, head * GA_DH:(head + 1) * GA_DH].astype(F32), qg_ref[...],
                      cos_ref[rows, :], sin_ref[rows, :])
        q_scr[stacked(head, blk), :] = q.astype(BF16)

    def logits(head, blk):
        s_bufs[head][...] = _mm_nt(q_scr[stacked(head, blk), :], k_scr[...]) * scale

    def softmax(slot):
        s = s_bufs[slot][...]
        p = jnp.exp(s - jnp.max(s, axis=-1, keepdims=True))
        l_bufs[slot][...] = jnp.broadcast_to(jnp.sum(p, axis=-1, keepdims=True), (GQA_TQ, GA_DH))
        p_bufs[slot][...] = p.astype(BF16)

    def weighted(head, blk):
        rows = pl.ds(pl.multiple_of(blk * GQA_TQ, GQA_TQ), GQA_TQ)
        cols = slice(head * GA_DH, (head + 1) * GA_DH)
        o = _mm(p_bufs[head][...], v_scr[...]) / l_bufs[head][...]
        y_ref[rows, cols] = (o * _silu(z_ref[rows, cols].astype(F32))).astype(y_ref.dtype)

    prep(0, 0)
    prep(1, 0)
    prep(0, 1)
    logits(0, 0)
    logits(1, 0)
    softmax(0)

    def body(j, carry):
        logits(0, j + 1)
        softmax(1)
        weighted(0, j)
        prep(1, j + 1)
        logits(1, j + 1)
        softmax(0)
        weighted(1, j)
        prep(0, jnp.minimum(j + 2, n_blk - 1))
        return carry

    lax.fori_loop(0, n_blk - 1, body, 0)
    softmax(1)
    weighted(0, n_blk - 1)
    weighted(1, n_blk - 1)


def _gqa(pm3, cos_t, sin_t, qg, kg):
    b, s, _ = pm3.shape
    gw = (GA_HEADS // GA_KV_HEADS) * GA_DH
    return pl.pallas_call(
        _gqa_kernel,
        grid=(b, GA_KV_HEADS),
        in_specs=[
            pl.BlockSpec((None, s, gw), lambda i, j: (i, 0, _OFF["c_q"] // gw + j)),
            pl.BlockSpec((None, s, GA_DH), lambda i, j: (i, 0, _OFF["c_k"] // GA_DH + j)),
            pl.BlockSpec((None, s, GA_DH), lambda i, j: (i, 0, _OFF["c_v"] // GA_DH + j)),
            pl.BlockSpec((None, s, gw), lambda i, j: (i, 0, _OFF["c_z"] // gw + j)),
            pl.BlockSpec((s, GA_DH), lambda i, j: (0, 0)),
            pl.BlockSpec((s, GA_DH), lambda i, j: (0, 0)),
            pl.BlockSpec((1, GA_DH), lambda i, j: (0, 0)),
            pl.BlockSpec((1, GA_DH), lambda i, j: (0, 0)),
        ],
        out_specs=pl.BlockSpec((None, s, gw), lambda i, j: (i, 0, j)),
        out_shape=jax.ShapeDtypeStruct((b, s, BRANCH_W), BF16),
        scratch_shapes=[pltpu.VMEM((gw // GA_DH * s, GA_DH), BF16), pltpu.VMEM((s, GA_DH), BF16),
                        pltpu.VMEM((s, GA_DH), BF16),
                        pltpu.VMEM((GQA_TQ, s), F32), pltpu.VMEM((GQA_TQ, s), F32),
                        pltpu.VMEM((GQA_TQ, s), BF16), pltpu.VMEM((GQA_TQ, s), BF16),
                        pltpu.VMEM((GQA_TQ, GA_DH), F32), pltpu.VMEM((GQA_TQ, GA_DH), F32)],
        compiler_params=_cparams(("parallel", "parallel")),
        name="gqa",
    )(pm3, pm3, pm3, pm3, cos_t, sin_t, qg, kg)


NA_ROW_UNROLL = 4


def _na_kernel(q_ref, k_ref, v_ref, z_ref, bias_ref, qg_ref, kg_ref, y_ref, q_scr, k_scr, v_scr, o_scr):
    s_len = q_ref.shape[0]
    rows = s_len // GRID_W
    kr = min(NA_ROWS, rows)
    lane = lax.broadcasted_iota(jnp.int32, (s_len, 2 * NA_DH), 1)
    lo = lane < NA_DH

    def rms_pair(x, g):
        x2 = x * x
        ms_lo = jnp.sum(jnp.where(lo, x2, 0.0), axis=-1, keepdims=True)
        ms_hi = jnp.sum(jnp.where(lo, 0.0, x2), axis=-1, keepdims=True)
        ms = jnp.where(lo, ms_lo, ms_hi) * (1.0 / NA_DH)
        return x * lax.rsqrt(ms + EPS) * g

    qn = rms_pair(q_ref[...].astype(F32), qg_ref[...])
    kn = rms_pair(k_ref[...].astype(F32), kg_ref[...])
    vf = v_ref[...].astype(F32)
    for hh in range(2):
        cols = slice(hh * NA_DH, (hh + 1) * NA_DH)
        q_scr[hh] = qn[:, cols].astype(BF16)
        k_scr[hh] = kn[:, cols].astype(BF16)
        v_scr[hh] = vf[:, cols].astype(BF16)
    scale = NA_DH ** -0.5
    nkeys = kr * GRID_W

    def body(it, carry):
        units = []
        for u in range(NA_ROW_UNROLL):
            r = it * NA_ROW_UNROLL + u
            r0 = jnp.clip(r - kr // 2, 0, rows - kr)
            var = r0 - r + (NA_ROWS - 1)
            qrows = pl.ds(pl.multiple_of(r * GRID_W, GRID_W), GRID_W)
            krows = pl.ds(pl.multiple_of(r0 * GRID_W, GRID_W), nkeys)
            units += [(hh, var, qrows, krows) for hh in range(2)]
        logits = [_mm_nt(q_scr[hh, qrows, :], k_scr[hh, krows, :]) * scale + bias_ref[hh, var]
                  for hh, var, qrows, krows in units]
        probs = [jnp.exp(s - jnp.max(s, axis=-1, keepdims=True)) for s in logits]
        sums = [jnp.sum(p, axis=-1, keepdims=True) for p in probs]
        outs = [_mm(p, v_scr[hh, krows, :]) for p, (hh, _, _, krows) in zip(probs, units)]
        for o, l, (hh, _, qrows, _) in zip(outs, sums, units):
            o_scr[hh, qrows, :] = o / l
        return carry

    lax.fori_loop(0, rows // NA_ROW_UNROLL, body, 0)
    o = jnp.concatenate([o_scr[0], o_scr[1]], axis=-1)
    y_ref[...] = (o * _silu(z_ref[...].astype(F32))).astype(y_ref.dtype)


def _na_bias_table(rpb, rows):
    kr = min(NA_ROWS, rows)
    c = jnp.arange(GRID_W)
    c0 = jnp.clip(c - NA_COLS // 2, 0, GRID_W - NA_COLS)
    in_win = (c[None, :] >= c0[:, None]) & (c[None, :] < c0[:, None] + NA_COLS)
    col_off = jnp.clip(c[None, :] - c[:, None], -(NA_COLS - 1), NA_COLS - 1) + NA_COLS - 1
    t = rpb[:, :, col_off]
    t = jnp.where(in_win[None, None], t, NEG)
    ro = jnp.arange(NA_ROWS)[:, None] + jnp.arange(kr)[None, :]
    tv = t[:, ro]
    tv = tv.transpose(0, 1, 3, 2, 4)
    return tv.reshape(rpb.shape[0], NA_ROWS, GRID_W, kr * GRID_W).astype(F32)


def _natten(pm3, bias, qg, kg):
    b, s, _ = pm3.shape
    pw = 2 * NA_DH
    npair = NA_HEADS // 2
    hw = NA_HEADS * NA_DH
    bias5 = bias.reshape(npair, 2, *bias.shape[1:])
    return pl.pallas_call(
        _na_kernel,
        grid=(npair, b),
        in_specs=[
            pl.BlockSpec((None, s, pw), lambda p, i: (i, 0, _OFF["b_qkv"] // pw + p)),
            pl.BlockSpec((None, s, pw), lambda p, i: (i, 0, (_OFF["b_qkv"] + hw) // pw + p)),
            pl.BlockSpec((None, s, pw), lambda p, i: (i, 0, (_OFF["b_qkv"] + 2 * hw) // pw + p)),
            pl.BlockSpec((None, s, pw), lambda p, i: (i, 0, _OFF["b_z"] // pw + p)),
            pl.BlockSpec((None,) + bias5.shape[1:], lambda p, i: (p, 0, 0, 0, 0)),
            pl.BlockSpec((1, pw), lambda p, i: (0, 0)),
            pl.BlockSpec((1, pw), lambda p, i: (0, 0)),
        ],
        out_specs=pl.BlockSpec((None, s, pw), lambda p, i: (i, 0, p)),
        out_shape=jax.ShapeDtypeStruct((b, s, BRANCH_W), BF16),
        scratch_shapes=[pltpu.VMEM((2, s, NA_DH), BF16), pltpu.VMEM((2, s, NA_DH), BF16),
                        pltpu.VMEM((2, s, NA_DH), BF16), pltpu.VMEM((2, s, NA_DH), F32)],
        compiler_params=_cparams(("parallel", "parallel")),
        name="natten",
    )(pm3, pm3, pm3, pm3, bias5, qg, kg)


INV_BASE = 8


def _inverse_level_masks():
    i = lax.broadcasted_iota(jnp.int32, (BLK, BLK), 0)
    j = lax.broadcasted_iota(jnp.int32, (BLK, BLK), 1)
    same = lambda size: (i >> int(math.log2(size))) == (j >> int(math.log2(size)))
    base = same(INV_BASE)
    joins = []
    size = INV_BASE
    while size < DN_CHUNK:
        joins.append(same(2 * size) & jnp.logical_not(same(size)))
        size *= 2
    return base, joins


def _tri_inverses(l_mats, eyes, level_masks):
    base, joins = level_masks
    ps = [jnp.where(base, -l, 0.0) for l in l_mats]
    ts = [eye + p for eye, p in zip(eyes, ps)]
    for _ in range(int(math.log2(INV_BASE)) - 1):
        ps = [_mm(p, p) for p in ps]
        ts = [t + _mm(t, p) for t, p in zip(ts, ps)]
    for join in joins:
        mids = [_mm(jnp.where(join, l, 0.0), t) for l, t in zip(l_mats, ts)]
        ts = [t - _mm(t, mid) for t, mid in zip(ts, mids)]
    return ts


DN_HW = DN_HEADS * LANES
PK_W, PK_QD, PK_KD, PK_QK = (i * DN_HW for i in range(4))


def _dn_pre_kernel(qkv_ref, sm_ref, cw_ref, lp_ref, uf_ref, ub_ref, pkf_ref, pkb_ref, gtf_ref, gtb_ref):
    n = pl.program_id(1)
    nblk = pl.num_programs(1)
    s_len = qkv_ref.shape[0]
    halo = SUBLANES if qkv_ref.dtype == F32 else 2 * SUBLANES
    t0 = pl.multiple_of(n * BLK, BLK)
    pstart = pl.multiple_of(jnp.maximum(t0 - halo, 0), halo)
    nstart = pl.multiple_of(jnp.minimum(t0 + BLK, s_len - halo), halo)

    def conv_silu(col0):
        cols = slice(col0, col0 + LANES)
        prev = jnp.where(n > 0, qkv_ref[pl.ds(pstart, halo), cols].astype(F32), 0.0)
        cur = qkv_ref[pl.ds(t0, BLK), cols].astype(F32)
        nxt = jnp.where(n < nblk - 1, qkv_ref[pl.ds(nstart, halo), cols].astype(F32), 0.0)
        xw = jnp.concatenate([prev, cur, nxt], axis=0)
        base = halo - DN_CONV // 2
        acc = xw[base:base + BLK] * cw_ref[0:1, cols]
        for j in range(1, DN_CONV):
            acc = acc + xw[base + j:base + j + BLK] * cw_ref[j:j + 1, cols]
        return _silu(acc)

    def l2n(x):
        return x * lax.rsqrt(jnp.sum(x * x, axis=-1, keepdims=True) + EPS)

    sm = sm_ref[...]
    g_all = -jnp.exp(lp_ref[0:1, :]) * _softplus(sm + lp_ref[1:2, :])
    beta_all = _sigmoid(sm)
    outs = ((uf_ref, pkf_ref, gtf_ref), (ub_ref, pkb_ref, gtb_ref))
    per_dir = []
    for d in range(2):
        same, incl, strict, eye_b = _chunk_masks(reverse=(d == 1))
        gc = _mask_sum(incl, g_all)
        tot = _mask_sum(same, g_all)
        per_dir.append((incl, strict, _as_f32(eye_b), gc, gc.T, tot))
    level_masks = _inverse_level_masks()

    heads = []
    for h in range(DN_HEADS):
        q = l2n(conv_silu(h * DN_DK)) * (DN_DK ** -0.5)
        k = l2n(conv_silu(DN_HEADS * DN_DK + h * DN_DK))
        v = conv_silu(2 * DN_HEADS * DN_DK + h * DN_DV)
        heads.append((q, k, v))
    kks = [_mm_nt(k, k) for _, k, _ in heads]
    qks = [_mm_nt(q, k) for q, k, _ in heads]

    l_mats, rhss = [], []
    for h, (q, k, v) in enumerate(heads):
        hc = slice(h * LANES, (h + 1) * LANES)
        for d in range(2):
            incl, strict, eye_f, gc, gct, tot = per_dir[d]
            u_ref, pk_ref, gt_ref = outs[d]
            pk = lambda off: slice(off + h * LANES, off + (h + 1) * LANES)
            c = _L_AA + d * DN_HEADS + h
            gcol = gc[:, c:c + 1]
            grow = gct[c:c + 1, :]
            tcol = tot[:, c:c + 1]
            beta = beta_all[:, _L_AB + d * DN_HEADS + h:_L_AB + d * DN_HEADS + h + 1]
            decay = jnp.where(incl, jnp.exp(jnp.where(incl, gcol - grow, 0.0)), 0.0)
            l_mats.append(jnp.where(strict, beta * kks[h] * decay, 0.0))
            egc = jnp.exp(gcol)
            rhss.append(jnp.concatenate([v * beta, k * (beta * egc)], axis=-1).astype(BF16))
            pk_ref[:, pk(PK_QD)] = (q * egc).astype(pk_ref.dtype)
            pk_ref[:, pk(PK_KD)] = (k * jnp.exp(tcol - gcol)).astype(pk_ref.dtype)
            pk_ref[:, pk(PK_QK)] = (qks[h] * decay).astype(pk_ref.dtype)
            gtot = jnp.exp(tcol)
            for ci in range(BLK // DN_CHUNK):
                gt_ref[ci * SUBLANES:(ci + 1) * SUBLANES, hc] = jnp.broadcast_to(
                    gtot[ci * DN_CHUNK:ci * DN_CHUNK + SUBLANES, :], (SUBLANES, LANES))

    t_invs = _tri_inverses(l_mats, [per_dir[d][2] for _ in range(DN_HEADS) for d in range(2)], level_masks)
    sols = [_mm(t, rhs) for t, rhs in zip(t_invs, rhss)]
    for idx, sol in enumerate(sols):
        h, d = divmod(idx, 2)
        hc = slice(h * LANES, (h + 1) * LANES)
        u_ref, pk_ref = outs[d][0], outs[d][1]
        u_ref[:, hc] = sol[:, :DN_DV]
        pk_ref[:, PK_W + h * LANES:PK_W + (h + 1) * LANES] = sol[:, DN_DV:].astype(pk_ref.dtype)


def _dn_pre(pm3, ps3, conv_w8, lane_params):
    b, s, _ = pm3.shape
    nblk = s // BLK
    wq = DN_HEADS * (2 * DN_DK + DN_DV)
    hw = DN_HEADS * LANES
    tok = lambda i, j: (i, j, 0)
    big = lambda dt: jax.ShapeDtypeStruct((b, s, hw), dt)
    gts = jax.ShapeDtypeStruct((b, nblk * 2 * SUBLANES, hw), F32)
    bs_tok = pl.BlockSpec((None, BLK, hw), tok)
    bs_gt = pl.BlockSpec((None, 2 * SUBLANES, hw), tok)
    bs_pk = pl.BlockSpec((None, BLK, 4 * hw), tok)
    packed = jax.ShapeDtypeStruct((b, s, 4 * hw), BF16)
    return pl.pallas_call(
        _dn_pre_kernel,
        grid=(b, nblk),
        in_specs=[
            pl.BlockSpec((None, s, wq), lambda i, j: (i, 0, _OFF["a_qkv"] // wq)),
            pl.BlockSpec((None, BLK, LANES), tok),
            pl.BlockSpec((SUBLANES, wq), lambda i, j: (0, 0)),
            pl.BlockSpec((SUBLANES, LANES), lambda i, j: (0, 0)),
        ],
        out_specs=[bs_tok, bs_tok, bs_pk, bs_pk, bs_gt, bs_gt],
        out_shape=[big(F32), big(F32), packed, packed, gts, gts],
        compiler_params=_cparams(("parallel", "arbitrary")),
        name="dn_pre",
    )(pm3, ps3, conv_w8, lane_params)


SCAN_BLK = 2 * BLK


def _dn_scan_kernel(uf_ref, pkf_ref, gtf_ref, ub_ref, pkb_ref, gtb_ref, of_ref, ob_ref, st_scr):
    @pl.when(pl.program_id(1) == 0)
    def _():
        st_scr[...] = jnp.zeros_like(st_scr)

    per_blk = BLK // DN_CHUNK
    nchunk = SCAN_BLK // DN_CHUNK
    zeros_c = jnp.zeros((DN_CHUNK, DN_DV), F32)
    streams = ((uf_ref, pkf_ref, gtf_ref, of_ref, range(nchunk)),
               (ub_ref, pkb_ref, gtb_ref, ob_ref, range(nchunk - 1, -1, -1)))
    chains = [(d, h) + streams[d] for d in range(2) for h in range(DN_HEADS)]
    states = [st_scr[d * DN_HEADS + h] for d, h, *_ in chains]
    for step in range(nchunk):
        rs, v_pads = [], []
        for (d, h, u_ref, pk_ref, gt_ref, o_ref, order), state in zip(chains, states):
            rows = slice(order[step] * DN_CHUNK, (order[step] + 1) * DN_CHUNK)
            w = pk_ref[rows, PK_W + h * LANES:PK_W + (h + 1) * LANES]
            qd = pk_ref[rows, PK_QD + h * LANES:PK_QD + (h + 1) * LANES]
            rs.append(_mm(jnp.concatenate([w, qd], axis=0), state))
        for (d, h, u_ref, pk_ref, gt_ref, o_ref, order), r in zip(chains, rs):
            ci = order[step]
            rows = slice(ci * DN_CHUNK, (ci + 1) * DN_CHUNK)
            parts = [zeros_c] * per_blk
            parts[ci % per_blk] = u_ref[rows, h * LANES:(h + 1) * LANES] - r[:DN_CHUNK]
            v_pads.append(jnp.concatenate(parts, axis=0))
        new_states = []
        for (d, h, u_ref, pk_ref, gt_ref, o_ref, order), r, v_pad, state in zip(chains, rs, v_pads, states):
            hc = slice(h * LANES, (h + 1) * LANES)
            ci = order[step]
            rows = slice(ci * DN_CHUNK, (ci + 1) * DN_CHUNK)
            blk_rows = slice((ci // per_blk) * BLK, (ci // per_blk + 1) * BLK)
            o_ref[rows, hc] = r[DN_CHUNK:] + _mm(pk_ref[rows, PK_QK + h * LANES:PK_QK + (h + 1) * LANES], v_pad)
            gt = gt_ref[ci * SUBLANES:ci * SUBLANES + 1, hc]
            kd = pk_ref[blk_rows, PK_KD + h * LANES:PK_KD + (h + 1) * LANES]
            new_states.append(state * gt + _mm_tn(kd, v_pad))
        states = new_states
    for (d, h, *_), state in zip(chains, states):
        st_scr[d * DN_HEADS + h] = state


def _dn_scan(pre):
    uf, ub, pkf, pkb, gtf, gtb = pre
    b, s, hw = uf.shape
    nblk = s // SCAN_BLK
    gt_rows = SCAN_BLK // DN_CHUNK * SUBLANES
    fwd = lambda i, j: (i, j, 0)
    bwd = lambda i, j: (i, nblk - 1 - j, 0)
    def specs(imap):
        return [pl.BlockSpec((None, SCAN_BLK, hw), imap), pl.BlockSpec((None, SCAN_BLK, 4 * hw), imap),
                pl.BlockSpec((None, gt_rows, hw), imap)]
    return pl.pallas_call(
        _dn_scan_kernel,
        grid=(b, nblk),
        in_specs=specs(fwd) + specs(bwd),
        out_specs=[pl.BlockSpec((None, SCAN_BLK, hw), fwd), pl.BlockSpec((None, SCAN_BLK, hw), bwd)],
        out_shape=[jax.ShapeDtypeStruct((b, s, hw), F32)] * 2,
        scratch_shapes=[pltpu.VMEM((2 * DN_HEADS, DN_DK, DN_DV), F32)],
        compiler_params=_cparams(("parallel", "arbitrary")),
        name="dn_scan",
    )(uf, pkf, gtf, ub, pkb, gtb)


ML_AUG = 2 * LANES


def _ml_kernel(qf_ref, kf_ref, vf_ref, smf_ref, qb_ref, kb_ref, vb_ref, smb_ref, lp_ref,
               hf_ref, hb_ref, c_scr, m_scr):
    @pl.when(pl.program_id(1) == 0)
    def _():
        c_scr[...] = jnp.zeros_like(c_scr)
        m_scr[...] = jnp.zeros_like(m_scr)

    nchunk = BLK // ML_CHUNK
    lane = lax.broadcasted_iota(jnp.int32, (BLK, LANES), 1)
    ones_col = jnp.where(lane == 0, 1.0, 0.0).astype(BF16)
    zeros_aug = jnp.zeros((ML_CHUNK, ML_AUG), BF16)
    streams = ((qf_ref, kf_ref, vf_ref, smf_ref, hf_ref, range(nchunk)),
               (qb_ref, kb_ref, vb_ref, smb_ref, hb_ref, range(nchunk - 1, -1, -1)))
    gates = []
    for d, (q_ref, k_ref, v_ref, sm_ref, h_ref, order) in enumerate(streams):
        same, incl, _, _ = _chunk_masks(reverse=(d == 1))
        sm = sm_ref[...]
        ig_all = sm + lp_ref[0:1, :]
        x = sm + lp_ref[1:2, :]
        lf_all = jnp.minimum(x, 0.0) - jnp.log(1.0 + jnp.exp(-jnp.abs(x)))
        lf_all = pltpu.roll(lf_all, LANES - (_L_DF - _L_DI), 1)
        bc_all = _mask_sum(incl, lf_all)
        tot_all = _mask_sum(same, lf_all)
        a_all = ig_all - bc_all
        mwa_all = jnp.concatenate(
            [jnp.broadcast_to(jnp.max(a_all[ci * ML_CHUNK:(ci + 1) * ML_CHUNK], axis=0, keepdims=True),
                              (ML_CHUNK, LANES)) for ci in range(nchunk)], axis=0)
        gates.append((same, incl, bc_all, tot_all, a_all.T, jnp.exp(a_all - mwa_all), tot_all + mwa_all))

    lanes = lambda col: jnp.broadcast_to(col, (col.shape[0], LANES))
    chains = [(d, h) for d in range(2) for h in range(ML_HEADS)]
    ins = []
    for d, h in chains:
        q_ref, k_ref, v_ref = streams[d][0], streams[d][1], streams[d][2]
        q = q_ref[:, h * ML_DK:(h + 1) * ML_DK].astype(BF16)
        k = k_ref[:, h * ML_DK:(h + 1) * ML_DK].astype(F32) * (ML_DK ** -0.5)
        v_aug = jnp.concatenate([v_ref[:, h * ML_DV:(h + 1) * ML_DV].astype(BF16), ones_col], axis=-1)
        ins.append((q, k, v_aug))
    qks = [_mm_nt(q, k) for q, k, _ in ins]

    mids = []
    for (d, h), (q, k, v_aug), qk in zip(chains, ins, qks):
        same, incl, bc_all, tot_all, a_t, w_all, mw_all = gates[d]
        c = _L_DI + d * ML_HEADS + h
        b_l = lanes(bc_all[:, c:c + 1])
        dlog = jnp.where(incl, b_l + a_t[c:c + 1, :], NEG)
        m_intra = lanes(jnp.max(dlog, axis=-1, keepdims=True))
        s_intra = qk * jnp.exp(dlog - m_intra)
        r_intra = lanes(jnp.sum(s_intra, axis=-1, keepdims=True))
        wk = (k * lanes(w_all[:, c:c + 1])[:, :ML_DK]).astype(BF16)
        mids.append((b_l, m_intra, s_intra.astype(BF16), r_intra, wk))
    p_intras = [_mm(s_b, v_aug)[:, :ML_DV] for (_, _, s_b, _, _), (_, _, v_aug) in zip(mids, ins)]
    kvs = []
    for (_, _, _, _, wk), (_, _, v_aug) in zip(mids, ins):
        per_chunk = []
        for ci in range(nchunk):
            parts = [zeros_aug] * nchunk
            parts[ci] = v_aug[ci * ML_CHUNK:(ci + 1) * ML_CHUNK]
            per_chunk.append(_mm_tn(wk, jnp.concatenate(parts, axis=0)))
        kvs.append(per_chunk)

    c_sts = [c_scr[d * ML_HEADS + h] for d, h in chains]
    m_sts = [m_scr[d * ML_HEADS + h][0:1, :] for d, h in chains]
    for step in range(nchunk):
        qcs = []
        for (d, h), (q, _, _), c_st in zip(chains, ins, c_sts):
            ci = streams[d][5][step]
            qcs.append(_mm(q[ci * ML_CHUNK:(ci + 1) * ML_CHUNK], c_st))
        for idx, (d, h) in enumerate(chains):
            tot_all, mw_all = gates[d][3], gates[d][6]
            b_l, m_intra, _, r_intra, _ = mids[idx]
            c = _L_DI + d * ML_HEADS + h
            h_ref = streams[d][4]
            ci = streams[d][5][step]
            rows = slice(ci * ML_CHUNK, (ci + 1) * ML_CHUNK)
            r8 = slice(ci * ML_CHUNK, ci * ML_CHUNK + SUBLANES)
            m_st, c_st, qc = m_sts[idx], c_sts[idx], qcs[idx]
            m_inter = b_l[rows] + m_st
            m_i = jnp.maximum(m_intra[rows], m_inter)
            f_i = jnp.exp(m_intra[rows] - m_i)
            inter = jnp.exp(m_inter - m_i)
            numer = inter * qc[:, :ML_DV] + f_i * p_intras[idx][rows]
            denom = inter * lanes(qc[:, ML_DV:ML_DV + 1]) + f_i * r_intra[rows]
            h_ref[rows, h * ML_DV:(h + 1) * ML_DV] = numer / jnp.maximum(jnp.abs(denom), jnp.exp(-m_i))
            tot_c = lanes(tot_all[r8, c:c + 1])[0:1]
            mw_c = lanes(mw_all[r8, c:c + 1])[0:1]
            m_new = jnp.maximum(tot_c + m_st, mw_c)
            dec = jnp.exp(tot_c + m_st - m_new)
            gain = jnp.exp(mw_c - m_new)
            c_sts[idx] = (jnp.concatenate([dec, dec], axis=-1) * c_st
                          + jnp.concatenate([gain, gain], axis=-1) * kvs[idx][ci])
            m_sts[idx] = m_new
    for idx, (d, h) in enumerate(chains):
        c_scr[d * ML_HEADS + h] = c_sts[idx]
        m_scr[d * ML_HEADS + h] = jnp.broadcast_to(m_sts[idx], (SUBLANES, LANES))


def _mlstm(pm3, ps3, lane_params):
    b, s, _ = pm3.shape
    nblk = s // BLK
    qw = ML_HEADS * ML_DK
    vw = ML_HEADS * ML_DV
    def specs(tmap):
        blk = lambda j: tmap(j)
        return [
            pl.BlockSpec((None, BLK, qw), lambda i, j: (i, blk(j), _OFF["d_q"] // qw)),
            pl.BlockSpec((None, BLK, qw), lambda i, j: (i, blk(j), _OFF["d_k"] // qw)),
            pl.BlockSpec((None, BLK, vw), lambda i, j: (i, blk(j), _OFF["d_v"] // vw)),
            pl.BlockSpec((None, BLK, LANES), lambda i, j: (i, blk(j), 0)),
        ]
    fwd = lambda j: j
    bwd = lambda j: nblk - 1 - j
    return pl.pallas_call(
        _ml_kernel,
        grid=(b, nblk),
        in_specs=specs(fwd) + specs(bwd) + [pl.BlockSpec((SUBLANES, LANES), lambda i, j: (0, 0))],
        out_specs=[pl.BlockSpec((None, BLK, vw), lambda i, j: (i, j, 0)),
                   pl.BlockSpec((None, BLK, vw), lambda i, j: (i, nblk - 1 - j, 0))],
        out_shape=[jax.ShapeDtypeStruct((b, s, vw), F32)] * 2,
        scratch_shapes=[pltpu.VMEM((2 * ML_HEADS, ML_DK, ML_AUG), F32),
                        pltpu.VMEM((2 * ML_HEADS, SUBLANES, LANES), F32)],
        compiler_params=_cparams(("parallel", "arbitrary")),
        name="mlstm",
    )(pm3, pm3, pm3, ps3, pm3, pm3, pm3, ps3, lane_params)


def _comb_kernel(af_ref, ab_ref, df_ref, db_ref, az_ref, dz_ref, do_ref, ag_ref, dg_ref, ya_ref, yd_ref):
    def head_rms(x, g):
        outs = []
        for h in range(x.shape[-1] // LANES):
            xh = x[:, h * LANES:(h + 1) * LANES]
            ms = jnp.mean(xh * xh, axis=-1, keepdims=True)
            outs.append(xh * lax.rsqrt(ms + EPS) * g)
        return jnp.concatenate(outs, axis=-1)

    ya = head_rms(af_ref[...] + ab_ref[...], ag_ref[...]) * _silu(az_ref[...].astype(F32))
    ya_ref[...] = ya.astype(ya_ref.dtype)
    yd = _sigmoid(do_ref[...].astype(F32)) * head_rms(df_ref[...] + db_ref[...], dg_ref[...])
    yd_ref[...] = (yd * _silu(dz_ref[...].astype(F32))).astype(yd_ref.dtype)


def _combine(af, ab, df, db, pm2, ag, dg, tm=1024):
    m, w = af.shape
    tok = pl.BlockSpec((tm, w), lambda i: (i, 0))
    col = lambda name: pl.BlockSpec((tm, w), lambda i: (i, _OFF[name] // w))
    vec = pl.BlockSpec((1, LANES), lambda i: (0, 0))
    return pl.pallas_call(
        _comb_kernel,
        grid=(m // tm,),
        in_specs=[tok, tok, tok, tok, col("a_z"), col("d_z"), col("d_o"), vec, vec],
        out_specs=[tok, tok],
        out_shape=[jax.ShapeDtypeStruct((m, w), BF16)] * 2,
        compiler_params=_cparams(("parallel",)),
        name="combine",
    )(af, ab, df, db, pm2, pm2, pm2, ag, dg)


def _merge_kernel(x_ref, ya_ref, yb_ref, yc_ref, yd_ref, gl_ref, wb_ref, wo_ref, o_ref):
    d = x_ref.shape[-1]
    merged = None
    for i, y_ref in enumerate((ya_ref, yb_ref, yc_ref, yd_ref)):
        proj = jnp.dot(y_ref[...], wb_ref[i], preferred_element_type=F32)
        term = _sigmoid(gl_ref[:, i * d:(i + 1) * d].astype(F32)) * proj
        merged = term if merged is None else merged + term
    o_ref[...] = x_ref[...] + jnp.dot(merged.astype(BF16), wo_ref[...], preferred_element_type=F32)


def _merge(x2d, ya, yb, yc, yd, pm2, wb, wo, tm=512):
    m, d = x2d.shape
    gw = N_BRANCH * d
    ytok = pl.BlockSpec((tm, BRANCH_W), lambda i: (i, 0))
    return pl.pallas_call(
        _merge_kernel,
        grid=(m // tm,),
        in_specs=[
            pl.BlockSpec((tm, d), lambda i: (i, 0)),
            ytok, ytok, ytok, ytok,
            pl.BlockSpec((tm, gw), lambda i: (i, _OFF["gate"] // gw)),
            pl.BlockSpec((N_BRANCH, BRANCH_W, d), lambda i: (0, 0, 0)),
            pl.BlockSpec((d, d), lambda i: (0, 0)),
        ],
        out_specs=pl.BlockSpec((tm, d), lambda i: (i, 0)),
        out_shape=jax.ShapeDtypeStruct((m, d), F32),
        compiler_params=_cparams(("parallel",)),
        name="merge",
    )(x2d, ya, yb, yc, yd, pm2, wb, wo)


def _rope_lane_tables(s):
    t = jnp.arange(s)
    row = (t // GRID_W).astype(F32)
    col = (t % GRID_W).astype(F32)
    m = GA_DH // 4
    inv = ROPE_THETA ** (-jnp.arange(m, dtype=F32) / m)
    ar = row[:, None] * inv
    ac = col[:, None] * inv
    cos_t = jnp.concatenate([jnp.cos(ar), jnp.cos(ar), jnp.cos(ac), jnp.cos(ac)], axis=-1)
    sin_t = jnp.concatenate([-jnp.sin(ar), jnp.sin(ar), -jnp.sin(ac), jnp.sin(ac)], axis=-1)
    return cos_t.astype(F32), sin_t.astype(F32)


def _lane_rows(rows):
    tile = jnp.zeros((SUBLANES, LANES), F32)
    for r, (off, vals) in enumerate(rows):
        vals = vals.reshape(-1).astype(F32)
        tile = tile.at[r, off:off + vals.shape[0]].set(vals)
    return tile


def kernel(x, norm_g, w_in, conv_a, dn_a_log, dn_dt_bias, dn_norm_g, na_q_norm, na_k_norm, na_rpb,
           ga_q_norm, ga_k_norm, ml_i_bias, ml_f_bias, ml_norm_g, w_branch, w_out):
    b, s, d = x.shape
    depth = w_in.shape[0]
    cos_t, sin_t = _rope_lane_tables(s)
    x2 = x.reshape(b * s, d)
    for l in range(depth):
        w = w_in[l]
        w_main = jnp.concatenate([w[:, o:o + wd] for _, o, wd in _MAIN_SEGS], axis=1).astype(BF16)
        w_small = jnp.concatenate([w[:, o:o + 8] for o in _SMALL_SRC]
                                  + [jnp.zeros((d, LANES - 32), F32)], axis=1).astype(BF16)
        pm2, ps2 = _inproj(x2, norm_g[l].reshape(1, d), w_main, w_small)
        pm3 = pm2.reshape(b, s, N_MAIN)
        ps3 = ps2.reshape(b, s, LANES)

        conv8 = jnp.zeros((SUBLANES, conv_a.shape[-1]), F32).at[:DN_CONV].set(conv_a[l])
        dn_lp = _lane_rows([(_L_AA, dn_a_log[l]), (_L_AA, dn_dt_bias[l])])
        o_af, o_ab = _dn_scan(_dn_pre(pm3, ps3, conv8, dn_lp))

        ml_lp = _lane_rows([(_L_DI, ml_i_bias[l]), (_L_DF, ml_f_bias[l])])
        h_df, h_db = _mlstm(pm3, ps3, ml_lp)

        hw = BRANCH_W
        ya, yd = _combine(o_af.reshape(b * s, hw), o_ab.reshape(b * s, hw),
                          h_df.reshape(b * s, hw), h_db.reshape(b * s, hw), pm2,
                          dn_norm_g[l].reshape(1, LANES), ml_norm_g[l].reshape(1, LANES))

        bias = _na_bias_table(na_rpb[l], s // GRID_W)
        yb = _natten(pm3, bias, jnp.tile(na_q_norm[l], 2).reshape(1, 2 * NA_DH),
                     jnp.tile(na_k_norm[l], 2).reshape(1, 2 * NA_DH))
        yc = _gqa(pm3, cos_t, sin_t, ga_q_norm[l].reshape(1, GA_DH), ga_k_norm[l].reshape(1, GA_DH))

        x2 = _merge(x2, ya, yb.reshape(b * s, hw), yc.reshape(b * s, hw), yd, pm2,
                    w_branch[l].astype(BF16), w_out[l].astype(BF16))
    return x2.reshape(b, s, d)
```

```python
import functools
import math

import jax
import jax.numpy as jnp
from jax import lax
from jax.experimental import pallas as pl
from jax.experimental.pallas import tpu as pltpu

F32 = jnp.float32
BF16 = jnp.bfloat16

D_MODEL = 1024
GRID_W = 64
N_BRANCH = 4
BRANCH_W = 512
EPS = 1e-6
DN_HEADS, DN_DK, DN_DV, DN_CONV, DN_CHUNK = 4, 128, 128, 5, 64
NA_HEADS, NA_DH, NA_ROWS, NA_COLS = 8, 64, 8, 16
GA_HEADS, GA_KV_HEADS, GA_DH = 4, 2, 128
ROPE_THETA = 10000.0
ML_HEADS, ML_DK, ML_DV, ML_CHUNK = 4, 64, 128, 64

LANES = 128
SUBLANES = 8
VMEM_LIMIT_BYTES = 56 * 1024 * 1024

_O_A_QKV, _O_A_A, _O_A_B, _O_A_Z = 0, 1536, 1544, 1552
_O_B_QKV, _O_B_Z = 2064, 3600
_O_C_Q, _O_C_K, _O_C_V, _O_C_Z = 4112, 4624, 4880, 5136
_O_D_Q, _O_D_K, _O_D_V, _O_D_I, _O_D_F, _O_D_O, _O_D_Z = 5648, 5904, 6160, 6672, 6680, 6688, 7200
_O_GATE = 7712
_MAIN_SEGS = (
    ("a_qkv", _O_A_QKV, 1536), ("b_qkv", _O_B_QKV, 1536), ("a_z", _O_A_Z, 512), ("b_z", _O_B_Z, 512),
    ("gate", _O_GATE, 4096), ("c_q", _O_C_Q, 512), ("c_k", _O_C_K, 256), ("c_v", _O_C_V, 256),
    ("c_z", _O_C_Z, 512), ("d_q", _O_D_Q, 256), ("d_k", _O_D_K, 256), ("d_v", _O_D_V, 512),
    ("d_o", _O_D_O, 512), ("d_z", _O_D_Z, 512),
)
_OFF = {}
_o = 0
for _name, _src, _w in _MAIN_SEGS:
    _OFF[_name] = _o
    _o += _w
N_MAIN = _o
_SMALL_SRC = (_O_A_A, _O_A_B, _O_D_I, _O_D_F)
_L_AA, _L_AB, _L_DI, _L_DF = 0, 8, 16, 24

P_DTYPE = BF16
BLK = 128
NEG = -1e30


def _cparams(sem):
    return pltpu.CompilerParams(dimension_semantics=sem, vmem_limit_bytes=VMEM_LIMIT_BYTES)


def _sigmoid(x):
    return 1.0 / (1.0 + jnp.exp(-x))


def _silu(x):
    return x * _sigmoid(x)


def _softplus(x):
    return jnp.maximum(x, 0.0) + jnp.log(1.0 + jnp.exp(-jnp.abs(x)))


def _mm(a, b):
    return jnp.dot(a.astype(BF16), b.astype(BF16), preferred_element_type=F32)


def _mm_nt(a, b):
    return lax.dot_general(a.astype(BF16), b.astype(BF16), (((1,), (1,)), ((), ())),
                           preferred_element_type=F32)


def _mm_tn(a, b):
    return lax.dot_general(a.astype(BF16), b.astype(BF16), (((0,), (0,)), ((), ())),
                           preferred_element_type=F32)


def _mask_sum(mask, x):
    m = jnp.where(mask, 1.0, 0.0).astype(BF16)
    x1 = x.astype(BF16)
    r1 = x - x1.astype(F32)
    x2 = r1.astype(BF16)
    x3 = (r1 - x2.astype(F32)).astype(BF16)
    dot = lambda v: jnp.dot(m, v, preferred_element_type=F32)
    return dot(x1) + (dot(x2) + dot(x3))


def _chunk_masks(reverse):
    i = lax.broadcasted_iota(jnp.int32, (BLK, BLK), 0)
    j = lax.broadcasted_iota(jnp.int32, (BLK, BLK), 1)
    shift = int(math.log2(DN_CHUNK))
    same = (i >> shift) == (j >> shift)
    if reverse:
        incl = same & (j >= i)
        strict = same & (j > i)
    else:
        incl = same & (j <= i)
        strict = same & (j < i)
    return same, incl, strict, (i == j)


def _as_f32(mask):
    return jnp.where(mask, 1.0, 0.0).astype(F32)


def _inproj_kernel(x_ref, g_ref, w_ref, ws_ref, pm_ref, ps_ref, h_scr):
    @pl.when(pl.program_id(1) == 0)
    def _():
        x = x_ref[...]
        ms = jnp.mean(x * x, axis=-1, keepdims=True)
        h = (x * lax.rsqrt(ms + EPS) * g_ref[...]).astype(BF16)
        h_scr[...] = h
        ps_ref[...] = jnp.dot(h, ws_ref[...], preferred_element_type=F32)

    pm_ref[...] = jnp.dot(h_scr[...], w_ref[...], preferred_element_type=F32).astype(pm_ref.dtype)


def _inproj(x2d, g, w_main, w_small, tm=2048, tn=512):
    m, d = x2d.shape
    tm = min(tm, m)
    return pl.pallas_call(
        _inproj_kernel,
        grid=(m // tm, N_MAIN // tn),
        in_specs=[
            pl.BlockSpec((tm, d), lambda i, j: (i, 0)),
            pl.BlockSpec((1, d), lambda i, j: (0, 0)),
            pl.BlockSpec((d, tn), lambda i, j: (0, j)),
            pl.BlockSpec((d, LANES), lambda i, j: (0, 0)),
        ],
        out_specs=[
            pl.BlockSpec((tm, tn), lambda i, j: (i, j)),
            pl.BlockSpec((tm, LANES), lambda i, j: (i, 0)),
        ],
        out_shape=[jax.ShapeDtypeStruct((m, N_MAIN), P_DTYPE), jax.ShapeDtypeStruct((m, LANES), F32)],
        scratch_shapes=[pltpu.VMEM((tm, d), BF16)],
        compiler_params=_cparams(("parallel", "arbitrary")),
        name="inproj",
    )(x2d, g, w_main, w_small)


GQA_TQ = 512


def _gqa_kernel(q_ref, k_ref, v_ref, z_ref, cos_ref, sin_ref, qg_ref, kg_ref, y_ref,
                q_scr, k_scr, v_scr, s0_scr, s1_scr, p0_scr, p1_scr, l0_scr, l1_scr):
    def norm_rope(x, g, cos, sin):
        ms = jnp.mean(x * x, axis=-1, keepdims=True)
        xn = x * lax.rsqrt(ms + EPS) * g
        lane = lax.broadcasted_iota(jnp.int32, xn.shape, 1)
        partner = jnp.where((lane & 63) < 32, pltpu.roll(xn, LANES - 32, 1), pltpu.roll(xn, 32, 1))
        return xn * cos + partner * sin

    s_len = k_ref.shape[0]
    group = GA_HEADS // GA_KV_HEADS
    k_scr[...] = norm_rope(k_ref[...].astype(F32), kg_ref[...], cos_ref[...], sin_ref[...]).astype(BF16)
    v_scr[...] = v_ref[...].astype(BF16)
    assert group == 2
    scale = GA_DH ** -0.5
    n_blk = s_len // GQA_TQ
    s_bufs, p_bufs, l_bufs = (s0_scr, s1_scr), (p0_scr, p1_scr), (l0_scr, l1_scr)

    def stacked(head, blk):
        return pl.ds(pl.multiple_of(head * s_len + blk * GQA_TQ, GQA_TQ), GQA_TQ)

    def prep(head, blk):
        rows = pl.ds(pl.multiple_of(blk * GQA_TQ, GQA_TQ), GQA_TQ)
        q = norm_rope(q_ref[rows, head * GA_DH:(head + 1) * GA_DH].astype(F32), qg_ref[...],
                      cos_ref[rows, :], sin_ref[rows, :])
        q_scr[stacked(head, blk), :] = q.astype(BF16)

    def logits(head, blk):
        s_bufs[head][...] = _mm_nt(q_scr[stacked(head, blk), :], k_scr[...]) * scale

    def softmax(slot):
        s = s_bufs[slot][...]
        p = jnp.exp(s - jnp.max(s, axis=-1, keepdims=True))
        l_bufs[slot][...] = jnp.broadcast_to(jnp.sum(p, axis=-1, keepdims=True), (GQA_TQ, GA_DH))
        p_bufs[slot][...] = p.astype(BF16)

    def weighted(head, blk):
        rows = pl.ds(pl.multiple_of(blk * GQA_TQ, GQA_TQ), GQA_TQ)
        cols = slice(head * GA_DH, (head + 1) * GA_DH)
        o = _mm(p_bufs[head][...], v_scr[...]) / l_bufs[head][...]
        y_ref[rows, cols] = (o * _silu(z_ref[rows, cols].astype(F32))).astype(y_ref.dtype)

    prep(0, 0)
    prep(1, 0)
    prep(0, 1)
    logits(0, 0)
    logits(1, 0)
    softmax(0)

    def body(j, carry):
        logits(0, j + 1)
        softmax(1)
        weighted(0, j)
        prep(1, j + 1)
        logits(1, j + 1)
        softmax(0)
        weighted(1, j)
        prep(0, jnp.minimum(j + 2, n_blk - 1))
        return carry

    lax.fori_loop(0, n_blk - 1, body, 0)
    softmax(1)
    weighted(0, n_blk - 1)
    weighted(1, n_blk - 1)


def _gqa(pm3, cos_t, sin_t, qg, kg):
    b, s, _ = pm3.shape
    gw = (GA_HEADS // GA_KV_HEADS) * GA_DH
    return pl.pallas_call(
        _gqa_kernel,
        grid=(b, GA_KV_HEADS),
        in_specs=[
            pl.BlockSpec((None, s, gw), lambda i, j: (i, 0, _OFF["c_q"] // gw + j)),
            pl.BlockSpec((None, s, GA_DH), lambda i, j: (i, 0, _OFF["c_k"] // GA_DH + j)),
            pl.BlockSpec((None, s, GA_DH), lambda i, j: (i, 0, _OFF["c_v"] // GA_DH + j)),
            pl.BlockSpec((None, s, gw), lambda i, j: (i, 0, _OFF["c_z"] // gw + j)),
            pl.BlockSpec((s, GA_DH), lambda i, j: (0, 0)),
            pl.BlockSpec((s, GA_DH), lambda i, j: (0, 0)),
            pl.BlockSpec((1, GA_DH), lambda i, j: (0, 0)),
            pl.BlockSpec((1, GA_DH), lambda i, j: (0, 0)),
        ],
        out_specs=pl.BlockSpec((None, s, gw), lambda i, j: (i, 0, j)),
        out_shape=jax.ShapeDtypeStruct((b, s, BRANCH_W), BF16),
        scratch_shapes=[pltpu.VMEM((gw // GA_DH * s, GA_DH), BF16), pltpu.VMEM((s, GA_DH), BF16),
                        pltpu.VMEM((s, GA_DH), BF16),
                        pltpu.VMEM((GQA_TQ, s), F32), pltpu.VMEM((GQA_TQ, s), F32),
                        pltpu.VMEM((GQA_TQ, s), BF16), pltpu.VMEM((GQA_TQ, s), BF16),
                        pltpu.VMEM((GQA_TQ, GA_DH), F32), pltpu.VMEM((GQA_TQ, GA_DH), F32)],
        compiler_params=_cparams(("parallel", "parallel")),
        name="gqa",
    )(pm3, pm3, pm3, pm3, cos_t, sin_t, qg, kg)


NA_ROW_UNROLL = 4


def _na_kernel(q_ref, k_ref, v_ref, z_ref, bias_ref, qg_ref, kg_ref, y_ref, q_scr, k_scr, v_scr, o_scr):
    s_len = q_ref.shape[0]
    rows = s_len // GRID_W
    kr = min(NA_ROWS, rows)
    lane = lax.broadcasted_iota(jnp.int32, (s_len, 2 * NA_DH), 1)
    lo = lane < NA_DH

    def rms_pair(x, g):
        x2 = x * x
        ms_lo = jnp.sum(jnp.where(lo, x2, 0.0), axis=-1, keepdims=True)
        ms_hi = jnp.sum(jnp.where(lo, 0.0, x2), axis=-1, keepdims=True)
        ms = jnp.where(lo, ms_lo, ms_hi) * (1.0 / NA_DH)
        return x * lax.rsqrt(ms + EPS) * g

    scale = NA_DH ** -0.5
    assert math.log2(scale).is_integer()
    qn = rms_pair(q_ref[...].astype(F32), qg_ref[...]) * scale
    kn = rms_pair(k_ref[...].astype(F32), kg_ref[...])
    vf = v_ref[...].astype(F32)
    for hh in range(2):
        cols = slice(hh * NA_DH, (hh + 1) * NA_DH)
        q_scr[hh] = qn[:, cols].astype(BF16)
        k_scr[hh] = kn[:, cols].astype(BF16)
        v_scr[hh] = vf[:, cols].astype(BF16)
    nkeys = kr * GRID_W

    def body(it, carry):
        units = []
        for u in range(NA_ROW_UNROLL):
            r = it * NA_ROW_UNROLL + u
            r0 = jnp.clip(r - kr // 2, 0, rows - kr)
            var = r0 - r + (NA_ROWS - 1)
            qrows = pl.ds(pl.multiple_of(r * GRID_W, GRID_W), GRID_W)
            krows = pl.ds(pl.multiple_of(r0 * GRID_W, GRID_W), nkeys)
            units += [(hh, var, qrows, krows) for hh in range(2)]
        logits = [_mm_nt(q_scr[hh, qrows, :], k_scr[hh, krows, :]) + bias_ref[hh, var]
                  for hh, var, qrows, krows in units]
        probs = [jnp.exp(s - jnp.max(s, axis=-1, keepdims=True)) for s in logits]
        sums = [jnp.sum(p, axis=-1, keepdims=True) for p in probs]
        outs = [_mm(p, v_scr[hh, krows, :]) for p, (hh, _, _, krows) in zip(probs, units)]
        for o, l, (hh, _, qrows, _) in zip(outs, sums, units):
            o_scr[hh, qrows, :] = o / l
        return carry

    lax.fori_loop(0, rows // NA_ROW_UNROLL, body, 0)
    o = jnp.concatenate([o_scr[0], o_scr[1]], axis=-1)
    y_ref[...] = (o * _silu(z_ref[...].astype(F32))).astype(y_ref.dtype)


def _na_bias_table(rpb, rows):
    kr = min(NA_ROWS, rows)
    c = jnp.arange(GRID_W)
    c0 = jnp.clip(c - NA_COLS // 2, 0, GRID_W - NA_COLS)
    in_win = (c[None, :] >= c0[:, None]) & (c[None, :] < c0[:, None] + NA_COLS)
    col_off = jnp.clip(c[None, :] - c[:, None], -(NA_COLS - 1), NA_COLS - 1) + NA_COLS - 1
    t = rpb[:, :, col_off]
    t = jnp.where(in_win[None, None], t, NEG)
    ro = jnp.arange(NA_ROWS)[:, None] + jnp.arange(kr)[None, :]
    tv = t[:, ro]
    tv = tv.transpose(0, 1, 3, 2, 4)
    return tv.reshape(rpb.shape[0], NA_ROWS, GRID_W, kr * GRID_W).astype(F32)


def _natten(pm3, bias, qg, kg):
    b, s, _ = pm3.shape
    pw = 2 * NA_DH
    npair = NA_HEADS // 2
    hw = NA_HEADS * NA_DH
    bias5 = bias.reshape(npair, 2, *bias.shape[1:])
    return pl.pallas_call(
        _na_kernel,
        grid=(npair, b),
        in_specs=[
            pl.BlockSpec((None, s, pw), lambda p, i: (i, 0, _OFF["b_qkv"] // pw + p)),
            pl.BlockSpec((None, s, pw), lambda p, i: (i, 0, (_OFF["b_qkv"] + hw) // pw + p)),
            pl.BlockSpec((None, s, pw), lambda p, i: (i, 0, (_OFF["b_qkv"] + 2 * hw) // pw + p)),
            pl.BlockSpec((None, s, pw), lambda p, i: (i, 0, _OFF["b_z"] // pw + p)),
            pl.BlockSpec((None,) + bias5.shape[1:], lambda p, i: (p, 0, 0, 0, 0)),
            pl.BlockSpec((1, pw), lambda p, i: (0, 0)),
            pl.BlockSpec((1, pw), lambda p, i: (0, 0)),
        ],
        out_specs=pl.BlockSpec((None, s, pw), lambda p, i: (i, 0, p)),
        out_shape=jax.ShapeDtypeStruct((b, s, BRANCH_W), BF16),
        scratch_shapes=[pltpu.VMEM((2, s, NA_DH), BF16), pltpu.VMEM((2, s, NA_DH), BF16),
                        pltpu.VMEM((2, s, NA_DH), BF16), pltpu.VMEM((2, s, NA_DH), F32)],
        compiler_params=_cparams(("parallel", "parallel")),
        name="natten",
    )(pm3, pm3, pm3, pm3, bias5, qg, kg)


INV_BASE = 8


def _inverse_level_masks():
    i = lax.broadcasted_iota(jnp.int32, (BLK, BLK), 0)
    j = lax.broadcasted_iota(jnp.int32, (BLK, BLK), 1)
    same = lambda size: (i >> int(math.log2(size))) == (j >> int(math.log2(size)))
    base = same(INV_BASE)
    joins = []
    size = INV_BASE
    while size < DN_CHUNK:
        joins.append(same(2 * size) & jnp.logical_not(same(size)))
        size *= 2
    return base, joins


def _tri_inverses(l_mats, eyes, level_masks):
    base, joins = level_masks
    ps = [jnp.where(base, -l, 0.0) for l in l_mats]
    ts = [eye + p for eye, p in zip(eyes, ps)]
    for _ in range(int(math.log2(INV_BASE)) - 1):
        ps = [_mm(p, p) for p in ps]
        ts = [t + _mm(t, p) for t, p in zip(ts, ps)]
    for join in joins:
        mids = [_mm(jnp.where(join, l, 0.0), t) for l, t in zip(l_mats, ts)]
        ts = [t - _mm(t, mid) for t, mid in zip(ts, mids)]
    return ts


DN_HW = DN_HEADS * LANES
PK_W, PK_QD, PK_KD, PK_QK = (i * DN_HW for i in range(4))


def _dn_pre_kernel(qkv_ref, sm_ref, cw_ref, lp_ref, uf_ref, ub_ref, pkf_ref, pkb_ref, gtf_ref, gtb_ref):
    n = pl.program_id(1)
    nblk = pl.num_programs(1)
    s_len = qkv_ref.shape[0]
    halo = SUBLANES if qkv_ref.dtype == F32 else 2 * SUBLANES
    t0 = pl.multiple_of(n * BLK, BLK)
    pstart = pl.multiple_of(jnp.maximum(t0 - halo, 0), halo)
    nstart = pl.multiple_of(jnp.minimum(t0 + BLK, s_len - halo), halo)

    def conv_silu(col0):
        cols = slice(col0, col0 + LANES)
        prev = jnp.where(n > 0, qkv_ref[pl.ds(pstart, halo), cols].astype(F32), 0.0)
        cur = qkv_ref[pl.ds(t0, BLK), cols].astype(F32)
        nxt = jnp.where(n < nblk - 1, qkv_ref[pl.ds(nstart, halo), cols].astype(F32), 0.0)
        xw = jnp.concatenate([prev, cur, nxt], axis=0)
        base = halo - DN_CONV // 2
        acc = xw[base:base + BLK] * cw_ref[0:1, cols]
        for j in range(1, DN_CONV):
            acc = acc + xw[base + j:base + j + BLK] * cw_ref[j:j + 1, cols]
        return _silu(acc)

    def l2n(x):
        return x * lax.rsqrt(jnp.sum(x * x, axis=-1, keepdims=True) + EPS)

    sm = sm_ref[...]
    g_all = -jnp.exp(lp_ref[0:1, :]) * _softplus(sm + lp_ref[1:2, :])
    beta_all = _sigmoid(sm)
    outs = ((uf_ref, pkf_ref, gtf_ref), (ub_ref, pkb_ref, gtb_ref))
    per_dir = []
    for d in range(2):
        same, incl, strict, eye_b = _chunk_masks(reverse=(d == 1))
        gc = _mask_sum(incl, g_all)
        tot = _mask_sum(same, g_all)
        per_dir.append((incl, strict, _as_f32(eye_b), gc, gc.T, tot))
    level_masks = _inverse_level_masks()

    heads = []
    for h in range(DN_HEADS):
        q = l2n(conv_silu(h * DN_DK)) * (DN_DK ** -0.5)
        k = l2n(conv_silu(DN_HEADS * DN_DK + h * DN_DK))
        v = conv_silu(2 * DN_HEADS * DN_DK + h * DN_DV)
        heads.append((q, k, v))
    kks = [_mm_nt(k, k) for _, k, _ in heads]
    qks = [_mm_nt(q, k) for q, k, _ in heads]

    l_mats, rhss = [], []
    for h, (q, k, v) in enumerate(heads):
        hc = slice(h * LANES, (h + 1) * LANES)
        for d in range(2):
            incl, strict, eye_f, gc, gct, tot = per_dir[d]
            u_ref, pk_ref, gt_ref = outs[d]
            pk = lambda off: slice(off + h * LANES, off + (h + 1) * LANES)
            c = _L_AA + d * DN_HEADS + h
            gcol = gc[:, c:c + 1]
            grow = gct[c:c + 1, :]
            tcol = tot[:, c:c + 1]
            beta = beta_all[:, _L_AB + d * DN_HEADS + h:_L_AB + d * DN_HEADS + h + 1]
            decay = jnp.where(incl, jnp.exp(jnp.where(incl, gcol - grow, 0.0)), 0.0)
            l_mats.append(jnp.where(strict, beta * kks[h] * decay, 0.0))
            egc = jnp.exp(gcol)
            rhss.append(jnp.concatenate([v * beta, k * (beta * egc)], axis=-1).astype(BF16))
            pk_ref[:, pk(PK_QD)] = (q * egc).astype(pk_ref.dtype)
            pk_ref[:, pk(PK_KD)] = (k * jnp.exp(tcol - gcol)).astype(pk_ref.dtype)
            pk_ref[:, pk(PK_QK)] = (qks[h] * decay).astype(pk_ref.dtype)
            gtot = jnp.exp(tcol)
            for ci in range(BLK // DN_CHUNK):
                gt_ref[ci * SUBLANES:(ci + 1) * SUBLANES, hc] = jnp.broadcast_to(
                    gtot[ci * DN_CHUNK:ci * DN_CHUNK + SUBLANES, :], (SUBLANES, LANES))

    t_invs = _tri_inverses(l_mats, [per_dir[d][2] for _ in range(DN_HEADS) for d in range(2)], level_masks)
    sols = [_mm(t, rhs) for t, rhs in zip(t_invs, rhss)]
    for idx, sol in enumerate(sols):
        h, d = divmod(idx, 2)
        hc = slice(h * LANES, (h + 1) * LANES)
        u_ref, pk_ref = outs[d][0], outs[d][1]
        u_ref[:, hc] = sol[:, :DN_DV]
        pk_ref[:, PK_W + h * LANES:PK_W + (h + 1) * LANES] = sol[:, DN_DV:].astype(pk_ref.dtype)


def _dn_pre(pm3, ps3, conv_w8, lane_params):
    b, s, _ = pm3.shape
    nblk = s // BLK
    wq = DN_HEADS * (2 * DN_DK + DN_DV)
    hw = DN_HEADS * LANES
    tok = lambda i, j: (i, j, 0)
    big = lambda dt: jax.ShapeDtypeStruct((b, s, hw), dt)
    gts = jax.ShapeDtypeStruct((b, nblk * 2 * SUBLANES, hw), F32)
    bs_tok = pl.BlockSpec((None, BLK, hw), tok)
    bs_gt = pl.BlockSpec((None, 2 * SUBLANES, hw), tok)
    bs_pk = pl.BlockSpec((None, BLK, 4 * hw), tok)
    packed = jax.ShapeDtypeStruct((b, s, 4 * hw), BF16)
    return pl.pallas_call(
        _dn_pre_kernel,
        grid=(b, nblk),
        in_specs=[
            pl.BlockSpec((None, s, wq), lambda i, j: (i, 0, _OFF["a_qkv"] // wq)),
            pl.BlockSpec((None, BLK, LANES), tok),
            pl.BlockSpec((SUBLANES, wq), lambda i, j: (0, 0)),
            pl.BlockSpec((SUBLANES, LANES), lambda i, j: (0, 0)),
        ],
        out_specs=[bs_tok, bs_tok, bs_pk, bs_pk, bs_gt, bs_gt],
        out_shape=[big(F32), big(F32), packed, packed, gts, gts],
        compiler_params=_cparams(("parallel", "arbitrary")),
        name="dn_pre",
    )(pm3, ps3, conv_w8, lane_params)


SCAN_BLK = 2 * BLK


def _dn_scan_kernel(uf_ref, pkf_ref, gtf_ref, ub_ref, pkb_ref, gtb_ref, of_ref, ob_ref, st_scr):
    @pl.when(pl.program_id(1) == 0)
    def _():
        st_scr[...] = jnp.zeros_like(st_scr)

    per_blk = BLK // DN_CHUNK
    nchunk = SCAN_BLK // DN_CHUNK
    zeros_c = jnp.zeros((DN_CHUNK, DN_DV), F32)
    streams = ((uf_ref, pkf_ref, gtf_ref, of_ref, range(nchunk)),
               (ub_ref, pkb_ref, gtb_ref, ob_ref, range(nchunk - 1, -1, -1)))
    chains = [(d, h) + streams[d] for d in range(2) for h in range(DN_HEADS)]
    states = [st_scr[d * DN_HEADS + h] for d, h, *_ in chains]
    for step in range(nchunk):
        rs, v_pads = [], []
        for (d, h, u_ref, pk_ref, gt_ref, o_ref, order), state in zip(chains, states):
            rows = slice(order[step] * DN_CHUNK, (order[step] + 1) * DN_CHUNK)
            w = pk_ref[rows, PK_W + h * LANES:PK_W + (h + 1) * LANES]
            qd = pk_ref[rows, PK_QD + h * LANES:PK_QD + (h + 1) * LANES]
            rs.append(_mm(jnp.concatenate([w, qd], axis=0), state))
        for (d, h, u_ref, pk_ref, gt_ref, o_ref, order), r in zip(chains, rs):
            ci = order[step]
            rows = slice(ci * DN_CHUNK, (ci + 1) * DN_CHUNK)
            parts = [zeros_c] * per_blk
            parts[ci % per_blk] = u_ref[rows, h * LANES:(h + 1) * LANES] - r[:DN_CHUNK]
            v_pads.append(jnp.concatenate(parts, axis=0))
        new_states = []
        for (d, h, u_ref, pk_ref, gt_ref, o_ref, order), r, v_pad, state in zip(chains, rs, v_pads, states):
            hc = slice(h * LANES, (h + 1) * LANES)
            ci = order[step]
            rows = slice(ci * DN_CHUNK, (ci + 1) * DN_CHUNK)
            blk_rows = slice((ci // per_blk) * BLK, (ci // per_blk + 1) * BLK)
            o_ref[rows, hc] = r[DN_CHUNK:] + _mm(pk_ref[rows, PK_QK + h * LANES:PK_QK + (h + 1) * LANES], v_pad)
            gt = gt_ref[ci * SUBLANES:ci * SUBLANES + 1, hc]
            kd = pk_ref[blk_rows, PK_KD + h * LANES:PK_KD + (h + 1) * LANES]
            new_states.append(state * gt + _mm_tn(kd, v_pad))
        states = new_states
    for (d, h, *_), state in zip(chains, states):
        st_scr[d * DN_HEADS + h] = state


def _dn_scan(pre):
    uf, ub, pkf, pkb, gtf, gtb = pre
    b, s, hw = uf.shape
    nblk = s // SCAN_BLK
    gt_rows = SCAN_BLK // DN_CHUNK * SUBLANES
    fwd = lambda i, j: (i, j, 0)
    bwd = lambda i, j: (i, nblk - 1 - j, 0)
    def specs(imap):
        return [pl.BlockSpec((None, SCAN_BLK, hw), imap), pl.BlockSpec((None, SCAN_BLK, 4 * hw), imap),
                pl.BlockSpec((None, gt_rows, hw), imap)]
    return pl.pallas_call(
        _dn_scan_kernel,
        grid=(b, nblk),
        in_specs=specs(fwd) + specs(bwd),
        out_specs=[pl.BlockSpec((None, SCAN_BLK, hw), fwd), pl.BlockSpec((None, SCAN_BLK, hw), bwd)],
        out_shape=[jax.ShapeDtypeStruct((b, s, hw), F32)] * 2,
        scratch_shapes=[pltpu.VMEM((2 * DN_HEADS, DN_DK, DN_DV), F32)],
        compiler_params=_cparams(("parallel", "arbitrary")),
        name="dn_scan",
    )(uf, pkf, gtf, ub, pkb, gtb)


ML_AUG = 2 * LANES


def _ml_kernel(qf_ref, kf_ref, vf_ref, smf_ref, qb_ref, kb_ref, vb_ref, smb_ref, lp_ref,
               hf_ref, hb_ref, c_scr, m_scr):
    @pl.when(pl.program_id(1) == 0)
    def _():
        c_scr[...] = jnp.zeros_like(c_scr)
        m_scr[...] = jnp.zeros_like(m_scr)

    nchunk = BLK // ML_CHUNK
    lane = lax.broadcasted_iota(jnp.int32, (BLK, LANES), 1)
    ones_col = jnp.where(lane == 0, 1.0, 0.0).astype(BF16)
    zeros_aug = jnp.zeros((ML_CHUNK, ML_AUG), BF16)
    streams = ((qf_ref, kf_ref, vf_ref, smf_ref, hf_ref, range(nchunk)),
               (qb_ref, kb_ref, vb_ref, smb_ref, hb_ref, range(nchunk - 1, -1, -1)))
    gates = []
    for d, (q_ref, k_ref, v_ref, sm_ref, h_ref, order) in enumerate(streams):
        same, incl, _, _ = _chunk_masks(reverse=(d == 1))
        sm = sm_ref[...]
        ig_all = sm + lp_ref[0:1, :]
        x = sm + lp_ref[1:2, :]
        lf_all = jnp.minimum(x, 0.0) - jnp.log(1.0 + jnp.exp(-jnp.abs(x)))
        lf_all = pltpu.roll(lf_all, LANES - (_L_DF - _L_DI), 1)
        bc_all = _mask_sum(incl, lf_all)
        tot_all = _mask_sum(same, lf_all)
        a_all = ig_all - bc_all
        mwa_all = jnp.concatenate(
            [jnp.broadcast_to(jnp.max(a_all[ci * ML_CHUNK:(ci + 1) * ML_CHUNK], axis=0, keepdims=True),
                              (ML_CHUNK, LANES)) for ci in range(nchunk)], axis=0)
        gates.append((same, incl, bc_all, tot_all, a_all.T, jnp.exp(a_all - mwa_all), tot_all + mwa_all))

    lanes = lambda col: jnp.broadcast_to(col, (col.shape[0], LANES))
    chains = [(d, h) for d in range(2) for h in range(ML_HEADS)]
    ins = []
    for d, h in chains:
        q_ref, k_ref, v_ref = streams[d][0], streams[d][1], streams[d][2]
        q = q_ref[:, h * ML_DK:(h + 1) * ML_DK].astype(BF16)
        k = k_ref[:, h * ML_DK:(h + 1) * ML_DK].astype(F32) * (ML_DK ** -0.5)
        v_aug = jnp.concatenate([v_ref[:, h * ML_DV:(h + 1) * ML_DV].astype(BF16), ones_col], axis=-1)
        ins.append((q, k, v_aug))
    qks = [_mm_nt(q, k) for q, k, _ in ins]

    mids = []
    for (d, h), (q, k, v_aug), qk in zip(chains, ins, qks):
        same, incl, bc_all, tot_all, a_t, w_all, mw_all = gates[d]
        c = _L_DI + d * ML_HEADS + h
        b_l = lanes(bc_all[:, c:c + 1])
        dlog = jnp.where(incl, b_l + a_t[c:c + 1, :], NEG)
        m_intra = lanes(jnp.max(dlog, axis=-1, keepdims=True))
        s_intra = qk * jnp.exp(dlog - m_intra)
        r_intra = lanes(jnp.sum(s_intra, axis=-1, keepdims=True))
        wk = (k * lanes(w_all[:, c:c + 1])[:, :ML_DK]).astype(BF16)
        mids.append((b_l, m_intra, s_intra.astype(BF16), r_intra, wk))
    p_intras = [_mm(s_b, v_aug)[:, :ML_DV] for (_, _, s_b, _, _), (_, _, v_aug) in zip(mids, ins)]
    kvs = []
    for (_, _, _, _, wk), (_, _, v_aug) in zip(mids, ins):
        per_chunk = []
        for ci in range(nchunk):
            parts = [zeros_aug] * nchunk
            parts[ci] = v_aug[ci * ML_CHUNK:(ci + 1) * ML_CHUNK]
            per_chunk.append(_mm_tn(wk, jnp.concatenate(parts, axis=0)))
        kvs.append(per_chunk)

    c_sts = [c_scr[d * ML_HEADS + h] for d, h in chains]
    m_sts = [m_scr[d * ML_HEADS + h][0:1, :] for d, h in chains]
    for step in range(nchunk):
        qcs = []
        for (d, h), (q, _, _), c_st in zip(chains, ins, c_sts):
            ci = streams[d][5][step]
            qcs.append(_mm(q[ci * ML_CHUNK:(ci + 1) * ML_CHUNK], c_st))
        for idx, (d, h) in enumerate(chains):
            tot_all, mw_all = gates[d][3], gates[d][6]
            b_l, m_intra, _, r_intra, _ = mids[idx]
            c = _L_DI + d * ML_HEADS + h
            h_ref = streams[d][4]
            ci = streams[d][5][step]
            rows = slice(ci * ML_CHUNK, (ci + 1) * ML_CHUNK)
            r8 = slice(ci * ML_CHUNK, ci * ML_CHUNK + SUBLANES)
            m_st, c_st, qc = m_sts[idx], c_sts[idx], qcs[idx]
            m_inter = b_l[rows] + m_st
            m_i = jnp.maximum(m_intra[rows], m_inter)
            f_i = jnp.exp(m_intra[rows] - m_i)
            inter = jnp.exp(m_inter - m_i)
            numer = inter * qc[:, :ML_DV] + f_i * p_intras[idx][rows]
            denom = inter * lanes(qc[:, ML_DV:ML_DV + 1]) + f_i * r_intra[rows]
            h_ref[rows, h * ML_DV:(h + 1) * ML_DV] = numer / jnp.maximum(jnp.abs(denom), jnp.exp(-m_i))
            tot_c = lanes(tot_all[r8, c:c + 1])[0:1]
            mw_c = lanes(mw_all[r8, c:c + 1])[0:1]
            m_new = jnp.maximum(tot_c + m_st, mw_c)
            dec = jnp.exp(tot_c + m_st - m_new)
            gain = jnp.exp(mw_c - m_new)
            c_sts[idx] = (jnp.concatenate([dec, dec], axis=-1) * c_st
                          + jnp.concatenate([gain, gain], axis=-1) * kvs[idx][ci])
            m_sts[idx] = m_new
    for idx, (d, h) in enumerate(chains):
        c_scr[d * ML_HEADS + h] = c_sts[idx]
        m_scr[d * ML_HEADS + h] = jnp.broadcast_to(m_sts[idx], (SUBLANES, LANES))


def _mlstm(pm3, ps3, lane_params):
    b, s, _ = pm3.shape
    nblk = s // BLK
    qw = ML_HEADS * ML_DK
    vw = ML_HEADS * ML_DV
    def specs(tmap):
        blk = lambda j: tmap(j)
        return [
            pl.BlockSpec((None, BLK, qw), lambda i, j: (i, blk(j), _OFF["d_q"] // qw)),
            pl.BlockSpec((None, BLK, qw), lambda i, j: (i, blk(j), _OFF["d_k"] // qw)),
            pl.BlockSpec((None, BLK, vw), lambda i, j: (i, blk(j), _OFF["d_v"] // vw)),
            pl.BlockSpec((None, BLK, LANES), lambda i, j: (i, blk(j), 0)),
        ]
    fwd = lambda j: j
    bwd = lambda j: nblk - 1 - j
    return pl.pallas_call(
        _ml_kernel,
        grid=(b, nblk),
        in_specs=specs(fwd) + specs(bwd) + [pl.BlockSpec((SUBLANES, LANES), lambda i, j: (0, 0))],
        out_specs=[pl.BlockSpec((None, BLK, vw), lambda i, j: (i, j, 0)),
                   pl.BlockSpec((None, BLK, vw), lambda i, j: (i, nblk - 1 - j, 0))],
        out_shape=[jax.ShapeDtypeStruct((b, s, vw), F32)] * 2,
        scratch_shapes=[pltpu.VMEM((2 * ML_HEADS, ML_DK, ML_AUG), F32),
                        pltpu.VMEM((2 * ML_HEADS, SUBLANES, LANES), F32)],
        compiler_params=_cparams(("parallel", "arbitrary")),
        name="mlstm",
    )(pm3, pm3, pm3, ps3, pm3, pm3, pm3, ps3, lane_params)


def _merge_kernel(x_ref, af_ref, ab_ref, df_ref, db_ref, yb_ref, yc_ref, az_ref, dz_ref, do_ref, gl_ref,
                  ag_ref, dg_ref, wb_ref, wo_ref, o_ref):
    d = x_ref.shape[-1]

    def head_rms(x, g):
        outs = []
        for h in range(x.shape[-1] // LANES):
            xh = x[:, h * LANES:(h + 1) * LANES]
            ms = jnp.mean(xh * xh, axis=-1, keepdims=True)
            outs.append(xh * lax.rsqrt(ms + EPS) * g)
        return jnp.concatenate(outs, axis=-1)

    ya = head_rms(af_ref[...] + ab_ref[...], ag_ref[...]) * _silu(az_ref[...].astype(F32))
    yd = _sigmoid(do_ref[...].astype(F32)) * head_rms(df_ref[...] + db_ref[...], dg_ref[...])
    yd = yd * _silu(dz_ref[...].astype(F32))
    merged = None
    for i, y in enumerate((ya.astype(BF16), yb_ref[...], yc_ref[...], yd.astype(BF16))):
        proj = jnp.dot(y, wb_ref[i], preferred_element_type=F32)
        term = _sigmoid(gl_ref[:, i * d:(i + 1) * d].astype(F32)) * proj
        merged = term if merged is None else merged + term
    o_ref[...] = x_ref[...] + jnp.dot(merged.astype(BF16), wo_ref[...], preferred_element_type=F32)


def _merge(x2d, af, ab, df, db, yb, yc, pm2, ag, dg, wb, wo, tm=512):
    m, d = x2d.shape
    gw = N_BRANCH * d
    w = BRANCH_W
    tok = pl.BlockSpec((tm, w), lambda i: (i, 0))
    col = lambda name: pl.BlockSpec((tm, w), lambda i: (i, _OFF[name] // w))
    vec = pl.BlockSpec((1, LANES), lambda i: (0, 0))
    return pl.pallas_call(
        _merge_kernel,
        grid=(m // tm,),
        in_specs=[
            pl.BlockSpec((tm, d), lambda i: (i, 0)),
            tok, tok, tok, tok, tok, tok,
            col("a_z"), col("d_z"), col("d_o"),
            pl.BlockSpec((tm, gw), lambda i: (i, _OFF["gate"] // gw)),
            vec, vec,
            pl.BlockSpec((N_BRANCH, w, d), lambda i: (0, 0, 0)),
            pl.BlockSpec((d, d), lambda i: (0, 0)),
        ],
        out_specs=pl.BlockSpec((tm, d), lambda i: (i, 0)),
        out_shape=jax.ShapeDtypeStruct((m, d), F32),
        compiler_params=_cparams(("parallel",)),
        name="merge",
    )(x2d, af, ab, df, db, yb, yc, pm2, pm2, pm2, pm2, ag, dg, wb, wo)


def _rope_lane_tables(s):
    t = jnp.arange(s)
    row = (t // GRID_W).astype(F32)
    col = (t % GRID_W).astype(F32)
    m = GA_DH // 4
    inv = ROPE_THETA ** (-jnp.arange(m, dtype=F32) / m)
    ar = row[:, None] * inv
    ac = col[:, None] * inv
    cos_t = jnp.concatenate([jnp.cos(ar), jnp.cos(ar), jnp.cos(ac), jnp.cos(ac)], axis=-1)
    sin_t = jnp.concatenate([-jnp.sin(ar), jnp.sin(ar), -jnp.sin(ac), jnp.sin(ac)], axis=-1)
    return cos_t.astype(F32), sin_t.astype(F32)


def _lane_rows(rows):
    tile = jnp.zeros((SUBLANES, LANES), F32)
    for r, (off, vals) in enumerate(rows):
        vals = vals.reshape(-1).astype(F32)
        tile = tile.at[r, off:off + vals.shape[0]].set(vals)
    return tile


def kernel(x, norm_g, w_in, conv_a, dn_a_log, dn_dt_bias, dn_norm_g, na_q_norm, na_k_norm, na_rpb,
           ga_q_norm, ga_k_norm, ml_i_bias, ml_f_bias, ml_norm_g, w_branch, w_out):
    b, s, d = x.shape
    depth = w_in.shape[0]
    cos_t, sin_t = _rope_lane_tables(s)
    x2 = x.reshape(b * s, d)
    for l in range(depth):
        w = w_in[l]
        w_main = jnp.concatenate([w[:, o:o + wd] for _, o, wd in _MAIN_SEGS], axis=1).astype(BF16)
        w_small = jnp.concatenate([w[:, o:o + 8] for o in _SMALL_SRC]
                                  + [jnp.zeros((d, LANES - 32), F32)], axis=1).astype(BF16)
        pm2, ps2 = _inproj(x2, norm_g[l].reshape(1, d), w_main, w_small)
        pm3 = pm2.reshape(b, s, N_MAIN)
        ps3 = ps2.reshape(b, s, LANES)

        conv8 = jnp.zeros((SUBLANES, conv_a.shape[-1]), F32).at[:DN_CONV].set(conv_a[l])
        dn_lp = _lane_rows([(_L_AA, dn_a_log[l]), (_L_AA, dn_dt_bias[l])])
        o_af, o_ab = _dn_scan(_dn_pre(pm3, ps3, conv8, dn_lp))

        ml_lp = _lane_rows([(_L_DI, ml_i_bias[l]), (_L_DF, ml_f_bias[l])])
        h_df, h_db = _mlstm(pm3, ps3, ml_lp)

        hw = BRANCH_W
        bias = _na_bias_table(na_rpb[l], s // GRID_W)
        yb = _natten(pm3, bias, jnp.tile(na_q_norm[l], 2).reshape(1, 2 * NA_DH),
                     jnp.tile(na_k_norm[l], 2).reshape(1, 2 * NA_DH))
        yc = _gqa(pm3, cos_t, sin_t, ga_q_norm[l].reshape(1, GA_DH), ga_k_norm[l].reshape(1, GA_DH))

        x2 = _merge(x2, o_af.reshape(b * s, hw), o_ab.reshape(b * s, hw), h_df.reshape(b * s, hw),
                    h_db.reshape(b * s, hw), yb.reshape(b * s, hw), yc.reshape(b * s, hw), pm2,
                    dn_norm_g[l].reshape(1, LANES), ml_norm_g[l].reshape(1, LANES),
                    w_branch[l].astype(BF16), w_out[l].astype(BF16))
    return x2.reshape(b, s, d)
```

```python
import functools
import math

import jax
import jax.numpy as jnp
from jax import lax
from jax.experimental import pallas as pl
from jax.experimental.pallas import tpu as pltpu

F32 = jnp.float32
BF16 = jnp.bfloat16

D_MODEL = 1024
GRID_W = 64
N_BRANCH = 4
BRANCH_W = 512
EPS = 1e-6
DN_HEADS, DN_DK, DN_DV, DN_CONV, DN_CHUNK = 4, 128, 128, 5, 64
NA_HEADS, NA_DH, NA_ROWS, NA_COLS = 8, 64, 8, 16
GA_HEADS, GA_KV_HEADS, GA_DH = 4, 2, 128
ROPE_THETA = 10000.0
ML_HEADS, ML_DK, ML_DV, ML_CHUNK = 4, 64, 128, 64

LANES = 128
SUBLANES = 8
VMEM_LIMIT_BYTES = 56 * 1024 * 1024

_O_A_QKV, _O_A_A, _O_A_B, _O_A_Z = 0, 1536, 1544, 1552
_O_B_QKV, _O_B_Z = 2064, 3600
_O_C_Q, _O_C_K, _O_C_V, _O_C_Z = 4112, 4624, 4880, 5136
_O_D_Q, _O_D_K, _O_D_V, _O_D_I, _O_D_F, _O_D_O, _O_D_Z = 5648, 5904, 6160, 6672, 6680, 6688, 7200
_O_GATE = 7712
_MAIN_SEGS = (
    ("a_qkv", _O_A_QKV, 1536), ("b_qkv", _O_B_QKV, 1536), ("a_z", _O_A_Z, 512), ("b_z", _O_B_Z, 512),
    ("gate", _O_GATE, 4096), ("c_q", _O_C_Q, 512), ("c_k", _O_C_K, 256), ("c_v", _O_C_V, 256),
    ("c_z", _O_C_Z, 512), ("d_q", _O_D_Q, 256), ("d_k", _O_D_K, 256), ("d_v", _O_D_V, 512),
    ("d_o", _O_D_O, 512), ("d_z", _O_D_Z, 512),
)
_OFF = {}
_o = 0
for _name, _src, _w in _MAIN_SEGS:
    _OFF[_name] = _o
    _o += _w
N_MAIN = _o
_SMALL_SRC = (_O_A_A, _O_A_B, _O_D_I, _O_D_F)
_L_AA, _L_AB, _L_DI, _L_DF = 0, 8, 16, 24

P_DTYPE = BF16
BLK = 128
NEG = -1e30


def _cparams(sem):
    return pltpu.CompilerParams(dimension_semantics=sem, vmem_limit_bytes=VMEM_LIMIT_BYTES)


def _sigmoid(x):
    return 1.0 / (1.0 + jnp.exp(-x))


def _silu(x):
    return x * _sigmoid(x)


def _softplus(x):
    return jnp.maximum(x, 0.0) + jnp.log(1.0 + jnp.exp(-jnp.abs(x)))


def _mm(a, b):
    return jnp.dot(a.astype(BF16), b.astype(BF16), preferred_element_type=F32)


def _mm_nt(a, b):
    return lax.dot_general(a.astype(BF16), b.astype(BF16), (((1,), (1,)), ((), ())),
                           preferred_element_type=F32)


def _mm_tn(a, b):
    return lax.dot_general(a.astype(BF16), b.astype(BF16), (((0,), (0,)), ((), ())),
                           preferred_element_type=F32)


def _mask_sum(mask, x):
    m = jnp.where(mask, 1.0, 0.0).astype(BF16)
    x1 = x.astype(BF16)
    r1 = x - x1.astype(F32)
    x2 = r1.astype(BF16)
    x3 = (r1 - x2.astype(F32)).astype(BF16)
    dot = lambda v: jnp.dot(m, v, preferred_element_type=F32)
    return dot(x1) + (dot(x2) + dot(x3))


def _chunk_masks(reverse):
    i = lax.broadcasted_iota(jnp.int32, (BLK, BLK), 0)
    j = lax.broadcasted_iota(jnp.int32, (BLK, BLK), 1)
    shift = int(math.log2(DN_CHUNK))
    same = (i >> shift) == (j >> shift)
    if reverse:
        incl = same & (j >= i)
        strict = same & (j > i)
    else:
        incl = same & (j <= i)
        strict = same & (j < i)
    return same, incl, strict, (i == j)


def _as_f32(mask):
    return jnp.where(mask, 1.0, 0.0).astype(F32)


def _inproj_kernel(x_ref, g_ref, w_ref, ws_ref, pm_ref, ps_ref, h_scr):
    @pl.when(pl.program_id(1) == 0)
    def _():
        x = x_ref[...]
        ms = jnp.mean(x * x, axis=-1, keepdims=True)
        h = (x * lax.rsqrt(ms + EPS) * g_ref[...]).astype(BF16)
        h_scr[...] = h
        ps_ref[...] = jnp.dot(h, ws_ref[...], preferred_element_type=F32)

    pm_ref[...] = jnp.dot(h_scr[...], w_ref[...], preferred_element_type=F32).astype(pm_ref.dtype)


def _inproj(x2d, g, w_main, w_small, tm=1024, tn=N_MAIN // 4):
    m, d = x2d.shape
    tm = min(tm, m)
    return pl.pallas_call(
        _inproj_kernel,
        grid=(m // tm, N_MAIN // tn),
        in_specs=[
            pl.BlockSpec((tm, d), lambda i, j: (i, 0)),
            pl.BlockSpec((1, d), lambda i, j: (0, 0)),
            pl.BlockSpec((d, tn), lambda i, j: (0, j)),
            pl.BlockSpec((d, LANES), lambda i, j: (0, 0)),
        ],
        out_specs=[
            pl.BlockSpec((tm, tn), lambda i, j: (i, j)),
            pl.BlockSpec((tm, LANES), lambda i, j: (i, 0)),
        ],
        out_shape=[jax.ShapeDtypeStruct((m, N_MAIN), P_DTYPE), jax.ShapeDtypeStruct((m, LANES), F32)],
        scratch_shapes=[pltpu.VMEM((tm, d), BF16)],
        compiler_params=_cparams(("parallel", "arbitrary")),
        name="inproj",
    )(x2d, g, w_main, w_small)


GQA_TQ = 512


def _gqa_kernel(q_ref, k_ref, v_ref, z_ref, cos_ref, sin_ref, qg_ref, kg_ref, y_ref,
                q_scr, k_scr, v_scr, s0_scr, s1_scr, p0_scr, p1_scr, l0_scr, l1_scr):
    def norm_rope(x, g, cos, sin):
        ms = jnp.mean(x * x, axis=-1, keepdims=True)
        xn = x * lax.rsqrt(ms + EPS) * g
        lane = lax.broadcasted_iota(jnp.int32, xn.shape, 1)
        partner = jnp.where((lane & 63) < 32, pltpu.roll(xn, LANES - 32, 1), pltpu.roll(xn, 32, 1))
        return xn * cos + partner * sin

    s_len = k_ref.shape[0]
    group = GA_HEADS // GA_KV_HEADS
    k_scr[...] = norm_rope(k_ref[...].astype(F32), kg_ref[...], cos_ref[...], sin_ref[...]).astype(BF16)
    v_scr[...] = v_ref[...].astype(BF16)
    assert group == 2
    scale = GA_DH ** -0.5
    n_blk = s_len // GQA_TQ
    s_bufs, p_bufs, l_bufs = (s0_scr, s1_scr), (p0_scr, p1_scr), (l0_scr, l1_scr)

    def stacked(head, blk):
        return pl.ds(pl.multiple_of(head * s_len + blk * GQA_TQ, GQA_TQ), GQA_TQ)

    def prep(head, blk):
        rows = pl.ds(pl.multiple_of(blk * GQA_TQ, GQA_TQ), GQA_TQ)
        q = norm_rope(q_ref[rows, head * GA_DH:(head + 1) * GA_DH].astype(F32), qg_ref[...],
                      cos_ref[rows, :], sin_ref[rows, :])
        q_scr[stacked(head, blk), :] = q.astype(BF16)

    def logits(head, blk):
        s_bufs[head][...] = _mm_nt(q_scr[stacked(head, blk), :], k_scr[...]) * scale

    def softmax(slot):
        s = s_bufs[slot][...]
        p = jnp.exp(s - jnp.max(s, axis=-1, keepdims=True))
        l_bufs[slot][...] = jnp.broadcast_to(jnp.sum(p, axis=-1, keepdims=True), (GQA_TQ, GA_DH))
        p_bufs[slot][...] = p.astype(BF16)

    def weighted(head, blk):
        rows = pl.ds(pl.multiple_of(blk * GQA_TQ, GQA_TQ), GQA_TQ)
        cols = slice(head * GA_DH, (head + 1) * GA_DH)
        o = _mm(p_bufs[head][...], v_scr[...]) / l_bufs[head][...]
        y_ref[rows, cols] = (o * _silu(z_ref[rows, cols].astype(F32))).astype(y_ref.dtype)

    prep(0, 0)
    prep(1, 0)
    prep(0, 1)
    logits(0, 0)
    logits(1, 0)
    softmax(0)

    def body(j, carry):
        logits(0, j + 1)
        softmax(1)
        weighted(0, j)
        prep(1, j + 1)
        logits(1, j + 1)
        softmax(0)
        weighted(1, j)
        prep(0, jnp.minimum(j + 2, n_blk - 1))
        return carry

    lax.fori_loop(0, n_blk - 1, body, 0)
    softmax(1)
    weighted(0, n_blk - 1)
    weighted(1, n_blk - 1)


def _gqa(pm3, cos_t, sin_t, qg, kg):
    b, s, _ = pm3.shape
    gw = (GA_HEADS // GA_KV_HEADS) * GA_DH
    return pl.pallas_call(
        _gqa_kernel,
        grid=(b, GA_KV_HEADS),
        in_specs=[
            pl.BlockSpec((None, s, gw), lambda i, j: (i, 0, _OFF["c_q"] // gw + j)),
            pl.BlockSpec((None, s, GA_DH), lambda i, j: (i, 0, _OFF["c_k"] // GA_DH + j)),
            pl.BlockSpec((None, s, GA_DH), lambda i, j: (i, 0, _OFF["c_v"] // GA_DH + j)),
            pl.BlockSpec((None, s, gw), lambda i, j: (i, 0, _OFF["c_z"] // gw + j)),
            pl.BlockSpec((s, GA_DH), lambda i, j: (0, 0)),
            pl.BlockSpec((s, GA_DH), lambda i, j: (0, 0)),
            pl.BlockSpec((1, GA_DH), lambda i, j: (0, 0)),
            pl.BlockSpec((1, GA_DH), lambda i, j: (0, 0)),
        ],
        out_specs=pl.BlockSpec((None, s, gw), lambda i, j: (i, 0, j)),
        out_shape=jax.ShapeDtypeStruct((b, s, BRANCH_W), BF16),
        scratch_shapes=[pltpu.VMEM((gw // GA_DH * s, GA_DH), BF16), pltpu.VMEM((s, GA_DH), BF16),
                        pltpu.VMEM((s, GA_DH), BF16),
                        pltpu.VMEM((GQA_TQ, s), F32), pltpu.VMEM((GQA_TQ, s), F32),
                        pltpu.VMEM((GQA_TQ, s), BF16), pltpu.VMEM((GQA_TQ, s), BF16),
                        pltpu.VMEM((GQA_TQ, GA_DH), F32), pltpu.VMEM((GQA_TQ, GA_DH), F32)],
        compiler_params=_cparams(("parallel", "parallel")),
        name="gqa",
    )(pm3, pm3, pm3, pm3, cos_t, sin_t, qg, kg)


NA_ROW_UNROLL = 4


def _na_kernel(q_ref, k_ref, v_ref, z_ref, bias_ref, qg_ref, kg_ref, y_ref, q_scr, k_scr, v_scr, o_scr):
    s_len = q_ref.shape[0]
    rows = s_len // GRID_W
    kr = min(NA_ROWS, rows)
    lane = lax.broadcasted_iota(jnp.int32, (s_len, 2 * NA_DH), 1)
    lo = lane < NA_DH

    def rms_pair(x, g):
        x2 = x * x
        ms_lo = jnp.sum(jnp.where(lo, x2, 0.0), axis=-1, keepdims=True)
        ms_hi = jnp.sum(jnp.where(lo, 0.0, x2), axis=-1, keepdims=True)
        ms = jnp.where(lo, ms_lo, ms_hi) * (1.0 / NA_DH)
        return x * lax.rsqrt(ms + EPS) * g

    scale = NA_DH ** -0.5
    assert math.log2(scale).is_integer()
    qn = rms_pair(q_ref[...].astype(F32), qg_ref[...]) * scale
    kn = rms_pair(k_ref[...].astype(F32), kg_ref[...])
    vf = v_ref[...].astype(F32)
    for hh in range(2):
        cols = slice(hh * NA_DH, (hh + 1) * NA_DH)
        q_scr[hh] = qn[:, cols].astype(BF16)
        k_scr[hh] = kn[:, cols].astype(BF16)
        v_scr[hh] = vf[:, cols].astype(BF16)
    nkeys = kr * GRID_W

    def body(it, carry):
        units = []
        for u in range(NA_ROW_UNROLL):
            r = it * NA_ROW_UNROLL + u
            r0 = jnp.clip(r - kr // 2, 0, rows - kr)
            var = r0 - r + (NA_ROWS - 1)
            qrows = pl.ds(pl.multiple_of(r * GRID_W, GRID_W), GRID_W)
            krows = pl.ds(pl.multiple_of(r0 * GRID_W, GRID_W), nkeys)
            units += [(hh, var, qrows, krows) for hh in range(2)]
        logits = [_mm_nt(q_scr[hh, qrows, :], k_scr[hh, krows, :]) + bias_ref[hh, var]
                  for hh, var, qrows, krows in units]
        probs = [jnp.exp(s - jnp.max(s, axis=-1, keepdims=True)) for s in logits]
        sums = [jnp.sum(p, axis=-1, keepdims=True) for p in probs]
        outs = [_mm(p, v_scr[hh, krows, :]) for p, (hh, _, _, krows) in zip(probs, units)]
        for o, l, (hh, _, qrows, _) in zip(outs, sums, units):
            o_scr[hh, qrows, :] = o / l
        return carry

    lax.fori_loop(0, rows // NA_ROW_UNROLL, body, 0)
    o = jnp.concatenate([o_scr[0], o_scr[1]], axis=-1)
    y_ref[...] = (o * _silu(z_ref[...].astype(F32))).astype(y_ref.dtype)


def _na_bias_table(rpb, rows):
    kr = min(NA_ROWS, rows)
    c = jnp.arange(GRID_W)
    c0 = jnp.clip(c - NA_COLS // 2, 0, GRID_W - NA_COLS)
    in_win = (c[None, :] >= c0[:, None]) & (c[None, :] < c0[:, None] + NA_COLS)
    col_off = jnp.clip(c[None, :] - c[:, None], -(NA_COLS - 1), NA_COLS - 1) + NA_COLS - 1
    t = rpb[:, :, col_off]
    t = jnp.where(in_win[None, None], t, NEG)
    ro = jnp.arange(NA_ROWS)[:, None] + jnp.arange(kr)[None, :]
    tv = t[:, ro]
    tv = tv.transpose(0, 1, 3, 2, 4)
    return tv.reshape(rpb.shape[0], NA_ROWS, GRID_W, kr * GRID_W).astype(F32)


def _natten(pm3, bias, qg, kg):
    b, s, _ = pm3.shape
    pw = 2 * NA_DH
    npair = NA_HEADS // 2
    hw = NA_HEADS * NA_DH
    bias5 = bias.reshape(npair, 2, *bias.shape[1:])
    return pl.pallas_call(
        _na_kernel,
        grid=(npair, b),
        in_specs=[
            pl.BlockSpec((None, s, pw), lambda p, i: (i, 0, _OFF["b_qkv"] // pw + p)),
            pl.BlockSpec((None, s, pw), lambda p, i: (i, 0, (_OFF["b_qkv"] + hw) // pw + p)),
            pl.BlockSpec((None, s, pw), lambda p, i: (i, 0, (_OFF["b_qkv"] + 2 * hw) // pw + p)),
            pl.BlockSpec((None, s, pw), lambda p, i: (i, 0, _OFF["b_z"] // pw + p)),
            pl.BlockSpec((None,) + bias5.shape[1:], lambda p, i: (p, 0, 0, 0, 0)),
            pl.BlockSpec((1, pw), lambda p, i: (0, 0)),
            pl.BlockSpec((1, pw), lambda p, i: (0, 0)),
        ],
        out_specs=pl.BlockSpec((None, s, pw), lambda p, i: (i, 0, p)),
        out_shape=jax.ShapeDtypeStruct((b, s, BRANCH_W), BF16),
        scratch_shapes=[pltpu.VMEM((2, s, NA_DH), BF16), pltpu.VMEM((2, s, NA_DH), BF16),
                        pltpu.VMEM((2, s, NA_DH), BF16), pltpu.VMEM((2, s, NA_DH), F32)],
        compiler_params=_cparams(("parallel", "parallel")),
        name="natten",
    )(pm3, pm3, pm3, pm3, bias5, qg, kg)


INV_BASE = 8
DN_GROUP = 4


def _inverse_level_masks():
    i = lax.broadcasted_iota(jnp.int32, (BLK, BLK), 0)
    j = lax.broadcasted_iota(jnp.int32, (BLK, BLK), 1)
    same = lambda size: (i >> int(math.log2(size))) == (j >> int(math.log2(size)))
    base = same(INV_BASE)
    joins = []
    size = INV_BASE
    while size < DN_CHUNK:
        joins.append(same(2 * size) & jnp.logical_not(same(size)))
        size *= 2
    return base, joins


def _tri_inverses(l_mats, eyes, level_masks):
    base, joins = level_masks
    ps = [jnp.where(base, -l, 0.0) for l in l_mats]
    ts = [eye + p for eye, p in zip(eyes, ps)]
    for _ in range(int(math.log2(INV_BASE)) - 1):
        ps = [_mm(p, p) for p in ps]
        ts = [t + _mm(t, p) for t, p in zip(ts, ps)]
    for join in joins:
        mids = [_mm(jnp.where(join, l, 0.0), t) for l, t in zip(l_mats, ts)]
        ts = [t - _mm(t, mid) for t, mid in zip(ts, mids)]
    return ts


DN_HW = DN_HEADS * LANES
PK_W, PK_QD, PK_KD, PK_QK = (i * DN_HW for i in range(4))


def _dn_pre_kernel(qkv_ref, sm_ref, cw_ref, lp_ref, uf_ref, ub_ref, pkf_ref, pkb_ref, gtf_ref, gtb_ref):
    n = pl.program_id(1)
    nblk = pl.num_programs(1)
    s_len = qkv_ref.shape[0]
    halo = SUBLANES if qkv_ref.dtype == F32 else 2 * SUBLANES
    t0 = pl.multiple_of(n * BLK, BLK)
    pstart = pl.multiple_of(jnp.maximum(t0 - halo, 0), halo)
    nstart = pl.multiple_of(jnp.minimum(t0 + BLK, s_len - halo), halo)

    def conv_silu(col0):
        cols = slice(col0, col0 + LANES)
        prev = jnp.where(n > 0, qkv_ref[pl.ds(pstart, halo), cols].astype(F32), 0.0)
        cur = qkv_ref[pl.ds(t0, BLK), cols].astype(F32)
        nxt = jnp.where(n < nblk - 1, qkv_ref[pl.ds(nstart, halo), cols].astype(F32), 0.0)
        xw = jnp.concatenate([prev, cur, nxt], axis=0)
        base = halo - DN_CONV // 2
        acc = xw[base:base + BLK] * cw_ref[0:1, cols]
        for j in range(1, DN_CONV):
            acc = acc + xw[base + j:base + j + BLK] * cw_ref[j:j + 1, cols]
        return _silu(acc)

    def l2n(x):
        return x * lax.rsqrt(jnp.sum(x * x, axis=-1, keepdims=True) + EPS)

    sm = sm_ref[...]
    g_all = -jnp.exp(lp_ref[0:1, :]) * _softplus(sm + lp_ref[1:2, :])
    beta_all = _sigmoid(sm)
    outs = ((uf_ref, pkf_ref, gtf_ref), (ub_ref, pkb_ref, gtb_ref))
    per_dir = []
    for d in range(2):
        same, incl, strict, eye_b = _chunk_masks(reverse=(d == 1))
        gc = _mask_sum(incl, g_all)
        tot = _mask_sum(same, g_all)
        per_dir.append((incl, strict, _as_f32(eye_b), gc, gc.T, tot))
    level_masks = _inverse_level_masks()

    for g in range(0, DN_HEADS, DN_GROUP):
        heads = []
        for h in range(g, g + DN_GROUP):
            q = l2n(conv_silu(h * DN_DK)) * (DN_DK ** -0.5)
            k = l2n(conv_silu(DN_HEADS * DN_DK + h * DN_DK))
            v = conv_silu(2 * DN_HEADS * DN_DK + h * DN_DV)
            heads.append((h, q, k, v))
        kks = [_mm_nt(k, k) for _, _, k, _ in heads]
        qks = [_mm_nt(q, k) for _, q, k, _ in heads]

        l_mats, rhss, slots = [], [], []
        for (h, q, k, v), kk, qk in zip(heads, kks, qks):
            hc = slice(h * LANES, (h + 1) * LANES)
            for d in range(2):
                incl, strict, eye_f, gc, gct, tot = per_dir[d]
                u_ref, pk_ref, gt_ref = outs[d]
                pk = lambda off: slice(off + h * LANES, off + (h + 1) * LANES)
                c = _L_AA + d * DN_HEADS + h
                gcol = gc[:, c:c + 1]
                grow = gct[c:c + 1, :]
                tcol = tot[:, c:c + 1]
                beta = beta_all[:, _L_AB + d * DN_HEADS + h:_L_AB + d * DN_HEADS + h + 1]
                decay = jnp.where(incl, jnp.exp(jnp.where(incl, gcol - grow, 0.0)), 0.0)
                l_mats.append(jnp.where(strict, beta * kk * decay, 0.0))
                egc = jnp.exp(gcol)
                rhss.append(jnp.concatenate([v * beta, k * (beta * egc)], axis=-1).astype(BF16))
                slots.append((h, d))
                pk_ref[:, pk(PK_QD)] = (q * egc).astype(pk_ref.dtype)
                pk_ref[:, pk(PK_KD)] = (k * jnp.exp(tcol - gcol)).astype(pk_ref.dtype)
                pk_ref[:, pk(PK_QK)] = (qk * decay).astype(pk_ref.dtype)
                gtot = jnp.exp(tcol)
                for ci in range(BLK // DN_CHUNK):
                    gt_ref[ci * SUBLANES:(ci + 1) * SUBLANES, hc] = jnp.broadcast_to(
                        gtot[ci * DN_CHUNK:ci * DN_CHUNK + SUBLANES, :], (SUBLANES, LANES))

        t_invs = _tri_inverses(l_mats, [per_dir[d][2] for _, d in slots], level_masks)
        sols = [_mm(t, rhs) for t, rhs in zip(t_invs, rhss)]
        for (h, d), sol in zip(slots, sols):
            hc = slice(h * LANES, (h + 1) * LANES)
            u_ref, pk_ref = outs[d][0], outs[d][1]
            u_ref[:, hc] = sol[:, :DN_DV]
            pk_ref[:, PK_W + h * LANES:PK_W + (h + 1) * LANES] = sol[:, DN_DV:].astype(pk_ref.dtype)


def _dn_pre(pm3, ps3, conv_w8, lane_params):
    b, s, _ = pm3.shape
    nblk = s // BLK
    wq = DN_HEADS * (2 * DN_DK + DN_DV)
    hw = DN_HEADS * LANES
    tok = lambda i, j: (i, j, 0)
    big = lambda dt: jax.ShapeDtypeStruct((b, s, hw), dt)
    gts = jax.ShapeDtypeStruct((b, nblk * 2 * SUBLANES, hw), F32)
    bs_tok = pl.BlockSpec((None, BLK, hw), tok)
    bs_gt = pl.BlockSpec((None, 2 * SUBLANES, hw), tok)
    bs_pk = pl.BlockSpec((None, BLK, 4 * hw), tok)
    packed = jax.ShapeDtypeStruct((b, s, 4 * hw), BF16)
    return pl.pallas_call(
        _dn_pre_kernel,
        grid=(b, nblk),
        in_specs=[
            pl.BlockSpec((None, s, wq), lambda i, j: (i, 0, _OFF["a_qkv"] // wq)),
            pl.BlockSpec((None, BLK, LANES), tok),
            pl.BlockSpec((SUBLANES, wq), lambda i, j: (0, 0)),
            pl.BlockSpec((SUBLANES, LANES), lambda i, j: (0, 0)),
        ],
        out_specs=[bs_tok, bs_tok, bs_pk, bs_pk, bs_gt, bs_gt],
        out_shape=[big(F32), big(F32), packed, packed, gts, gts],
        compiler_params=_cparams(("parallel", "arbitrary")),
        name="dn_pre",
    )(pm3, ps3, conv_w8, lane_params)


SCAN_BLK = 2 * BLK


def _dn_scan_kernel(uf_ref, pkf_ref, gtf_ref, ub_ref, pkb_ref, gtb_ref, of_ref, ob_ref, st_scr):
    @pl.when(pl.program_id(1) == 0)
    def _():
        st_scr[...] = jnp.zeros_like(st_scr)

    per_blk = BLK // DN_CHUNK
    nchunk = SCAN_BLK // DN_CHUNK
    zeros_c = jnp.zeros((DN_CHUNK, DN_DV), F32)
    streams = ((uf_ref, pkf_ref, gtf_ref, of_ref, range(nchunk)),
               (ub_ref, pkb_ref, gtb_ref, ob_ref, range(nchunk - 1, -1, -1)))
    chains = [(d, h) + streams[d] for d in range(2) for h in range(DN_HEADS)]
    states = [st_scr[d * DN_HEADS + h] for d, h, *_ in chains]
    for step in range(nchunk):
        rs, v_pads = [], []
        for (d, h, u_ref, pk_ref, gt_ref, o_ref, order), state in zip(chains, states):
            rows = slice(order[step] * DN_CHUNK, (order[step] + 1) * DN_CHUNK)
            w = pk_ref[rows, PK_W + h * LANES:PK_W + (h + 1) * LANES]
            qd = pk_ref[rows, PK_QD + h * LANES:PK_QD + (h + 1) * LANES]
            rs.append(_mm(jnp.concatenate([w, qd], axis=0), state))
        for (d, h, u_ref, pk_ref, gt_ref, o_ref, order), r in zip(chains, rs):
            ci = order[step]
            rows = slice(ci * DN_CHUNK, (ci + 1) * DN_CHUNK)
            parts = [zeros_c] * per_blk
            parts[ci % per_blk] = u_ref[rows, h * LANES:(h + 1) * LANES] - r[:DN_CHUNK]
            v_pads.append(jnp.concatenate(parts, axis=0))
        new_states = []
        for (d, h, u_ref, pk_ref, gt_ref, o_ref, order), r, v_pad, state in zip(chains, rs, v_pads, states):
            hc = slice(h * LANES, (h + 1) * LANES)
            ci = order[step]
            rows = slice(ci * DN_CHUNK, (ci + 1) * DN_CHUNK)
            blk_rows = slice((ci // per_blk) * BLK, (ci // per_blk + 1) * BLK)
            o_ref[rows, hc] = r[DN_CHUNK:] + _mm(pk_ref[rows, PK_QK + h * LANES:PK_QK + (h + 1) * LANES], v_pad)
            gt = gt_ref[ci * SUBLANES:ci * SUBLANES + 1, hc]
            kd = pk_ref[blk_rows, PK_KD + h * LANES:PK_KD + (h + 1) * LANES]
            new_states.append(state * gt + _mm_tn(kd, v_pad))
        states = new_states
    for (d, h, *_), state in zip(chains, states):
        st_scr[d * DN_HEADS + h] = state


def _dn_scan(pre):
    uf, ub, pkf, pkb, gtf, gtb = pre
    b, s, hw = uf.shape
    nblk = s // SCAN_BLK
    gt_rows = SCAN_BLK // DN_CHUNK * SUBLANES
    fwd = lambda i, j: (i, j, 0)
    bwd = lambda i, j: (i, nblk - 1 - j, 0)
    def specs(imap):
        return [pl.BlockSpec((None, SCAN_BLK, hw), imap), pl.BlockSpec((None, SCAN_BLK, 4 * hw), imap),
                pl.BlockSpec((None, gt_rows, hw), imap)]
    return pl.pallas_call(
        _dn_scan_kernel,
        grid=(b, nblk),
        in_specs=specs(fwd) + specs(bwd),
        out_specs=[pl.BlockSpec((None, SCAN_BLK, hw), fwd), pl.BlockSpec((None, SCAN_BLK, hw), bwd)],
        out_shape=[jax.ShapeDtypeStruct((b, s, hw), F32)] * 2,
        scratch_shapes=[pltpu.VMEM((2 * DN_HEADS, DN_DK, DN_DV), F32)],
        compiler_params=_cparams(("parallel", "arbitrary")),
        name="dn_scan",
    )(uf, pkf, gtf, ub, pkb, gtb)


ML_AUG = 2 * LANES
ML_GROUP = 4


def _ml_kernel(qf_ref, kf_ref, vf_ref, smf_ref, qb_ref, kb_ref, vb_ref, smb_ref, lp_ref,
               hf_ref, hb_ref, c_scr, m_scr):
    @pl.when(pl.program_id(1) == 0)
    def _():
        c_scr[...] = jnp.zeros_like(c_scr)
        m_scr[...] = jnp.zeros_like(m_scr)

    nchunk = BLK // ML_CHUNK
    lane = lax.broadcasted_iota(jnp.int32, (BLK, LANES), 1)
    ones_col = jnp.where(lane == 0, 1.0, 0.0).astype(BF16)
    zeros_aug = jnp.zeros((ML_CHUNK, ML_AUG), BF16)
    streams = ((qf_ref, kf_ref, vf_ref, smf_ref, hf_ref, range(nchunk)),
               (qb_ref, kb_ref, vb_ref, smb_ref, hb_ref, range(nchunk - 1, -1, -1)))
    gates = []
    for d, (q_ref, k_ref, v_ref, sm_ref, h_ref, order) in enumerate(streams):
        same, incl, _, _ = _chunk_masks(reverse=(d == 1))
        sm = sm_ref[...]
        ig_all = sm + lp_ref[0:1, :]
        x = sm + lp_ref[1:2, :]
        lf_all = jnp.minimum(x, 0.0) - jnp.log(1.0 + jnp.exp(-jnp.abs(x)))
        lf_all = pltpu.roll(lf_all, LANES - (_L_DF - _L_DI), 1)
        bc_all = _mask_sum(incl, lf_all)
        tot_all = _mask_sum(same, lf_all)
        a_all = ig_all - bc_all
        mwa_all = jnp.concatenate(
            [jnp.broadcast_to(jnp.max(a_all[ci * ML_CHUNK:(ci + 1) * ML_CHUNK], axis=0, keepdims=True),
                              (ML_CHUNK, LANES)) for ci in range(nchunk)], axis=0)
        gates.append((same, incl, bc_all, tot_all, a_all.T, jnp.exp(a_all - mwa_all), tot_all + mwa_all))

    lanes = lambda col: jnp.broadcast_to(col, (col.shape[0], LANES))
    groups = [[(d, h) for h in range(g, g + ML_GROUP)] for d in range(2) for g in range(0, ML_HEADS, ML_GROUP)]
    for chains in groups:
        ins = []
        for d, h in chains:
            q_ref, k_ref, v_ref = streams[d][0], streams[d][1], streams[d][2]
            q = q_ref[:, h * ML_DK:(h + 1) * ML_DK].astype(BF16)
            k = k_ref[:, h * ML_DK:(h + 1) * ML_DK].astype(F32) * (ML_DK ** -0.5)
            v_aug = jnp.concatenate([v_ref[:, h * ML_DV:(h + 1) * ML_DV].astype(BF16), ones_col], axis=-1)
            ins.append((q, k, v_aug))
        qks = [_mm_nt(q, k) for q, k, _ in ins]

        mids = []
        for (d, h), (q, k, v_aug), qk in zip(chains, ins, qks):
            same, incl, bc_all, tot_all, a_t, w_all, mw_all = gates[d]
            c = _L_DI + d * ML_HEADS + h
            b_l = lanes(bc_all[:, c:c + 1])
            dlog = jnp.where(incl, b_l + a_t[c:c + 1, :], NEG)
            m_intra = lanes(jnp.max(dlog, axis=-1, keepdims=True))
            s_intra = qk * jnp.exp(dlog - m_intra)
            r_intra = lanes(jnp.sum(s_intra, axis=-1, keepdims=True))
            wk = (k * lanes(w_all[:, c:c + 1])[:, :ML_DK]).astype(BF16)
            mids.append((b_l, m_intra, s_intra.astype(BF16), r_intra, wk))
        p_intras = [_mm(s_b, v_aug)[:, :ML_DV] for (_, _, s_b, _, _), (_, _, v_aug) in zip(mids, ins)]
        kvs = []
        for (_, _, _, _, wk), (_, _, v_aug) in zip(mids, ins):
            per_chunk = []
            for ci in range(nchunk):
                parts = [zeros_aug] * nchunk
                parts[ci] = v_aug[ci * ML_CHUNK:(ci + 1) * ML_CHUNK]
                per_chunk.append(_mm_tn(wk, jnp.concatenate(parts, axis=0)))
            kvs.append(per_chunk)

        c_sts = [c_scr[d * ML_HEADS + h] for d, h in chains]
        m_sts = [m_scr[d * ML_HEADS + h][0:1, :] for d, h in chains]
        for step in range(nchunk):
            qcs = []
            for (d, h), (q, _, _), c_st in zip(chains, ins, c_sts):
                ci = streams[d][5][step]
                qcs.append(_mm(q[ci * ML_CHUNK:(ci + 1) * ML_CHUNK], c_st))
            for idx, (d, h) in enumerate(chains):
                tot_all, mw_all = gates[d][3], gates[d][6]
                b_l, m_intra, _, r_intra, _ = mids[idx]
                c = _L_DI + d * ML_HEADS + h
                h_ref = streams[d][4]
                ci = streams[d][5][step]
                rows = slice(ci * ML_CHUNK, (ci + 1) * ML_CHUNK)
                r8 = slice(ci * ML_CHUNK, ci * ML_CHUNK + SUBLANES)
                m_st, c_st, qc = m_sts[idx], c_sts[idx], qcs[idx]
                m_inter = b_l[rows] + m_st
                m_i = jnp.maximum(m_intra[rows], m_inter)
                f_i = jnp.exp(m_intra[rows] - m_i)
                inter = jnp.exp(m_inter - m_i)
                numer = inter * qc[:, :ML_DV] + f_i * p_intras[idx][rows]
                denom = inter * lanes(qc[:, ML_DV:ML_DV + 1]) + f_i * r_intra[rows]
                h_ref[rows, h * ML_DV:(h + 1) * ML_DV] = numer / jnp.maximum(jnp.abs(denom), jnp.exp(-m_i))
                tot_c = lanes(tot_all[r8, c:c + 1])[0:1]
                mw_c = lanes(mw_all[r8, c:c + 1])[0:1]
                m_new = jnp.maximum(tot_c + m_st, mw_c)
                dec = jnp.exp(tot_c + m_st - m_new)
                gain = jnp.exp(mw_c - m_new)
                c_sts[idx] = (jnp.concatenate([dec, dec], axis=-1) * c_st
                              + jnp.concatenate([gain, gain], axis=-1) * kvs[idx][ci])
                m_sts[idx] = m_new
        for idx, (d, h) in enumerate(chains):
            c_scr[d * ML_HEADS + h] = c_sts[idx]
            m_scr[d * ML_HEADS + h] = jnp.broadcast_to(m_sts[idx], (SUBLANES, LANES))


def _mlstm(pm3, ps3, lane_params):
    b, s, _ = pm3.shape
    nblk = s // BLK
    qw = ML_HEADS * ML_DK
    vw = ML_HEADS * ML_DV
    def specs(tmap):
        blk = lambda j: tmap(j)
        return [
            pl.BlockSpec((None, BLK, qw), lambda i, j: (i, blk(j), _OFF["d_q"] // qw)),
            pl.BlockSpec((None, BLK, qw), lambda i, j: (i, blk(j), _OFF["d_k"] // qw)),
            pl.BlockSpec((None, BLK, vw), lambda i, j: (i, blk(j), _OFF["d_v"] // vw)),
            pl.BlockSpec((None, BLK, LANES), lambda i, j: (i, blk(j), 0)),
        ]
    fwd = lambda j: j
    bwd = lambda j: nblk - 1 - j
    return pl.pallas_call(
        _ml_kernel,
        grid=(b, nblk),
        in_specs=specs(fwd) + specs(bwd) + [pl.BlockSpec((SUBLANES, LANES), lambda i, j: (0, 0))],
        out_specs=[pl.BlockSpec((None, BLK, vw), lambda i, j: (i, j, 0)),
                   pl.BlockSpec((None, BLK, vw), lambda i, j: (i, nblk - 1 - j, 0))],
        out_shape=[jax.ShapeDtypeStruct((b, s, vw), F32)] * 2,
        scratch_shapes=[pltpu.VMEM((2 * ML_HEADS, ML_DK, ML_AUG), F32),
                        pltpu.VMEM((2 * ML_HEADS, SUBLANES, LANES), F32)],
        compiler_params=_cparams(("parallel", "arbitrary")),
        name="mlstm",
    )(pm3, pm3, pm3, ps3, pm3, pm3, pm3, ps3, lane_params)


def _merge_kernel(x_ref, af_ref, ab_ref, df_ref, db_ref, yb_ref, yc_ref, az_ref, dz_ref, do_ref, gl_ref,
                  ag_ref, dg_ref, wb_ref, wo_ref, o_ref):
    d = x_ref.shape[-1]

    def head_rms(x, g):
        outs = []
        for h in range(x.shape[-1] // LANES):
            xh = x[:, h * LANES:(h + 1) * LANES]
            ms = jnp.mean(xh * xh, axis=-1, keepdims=True)
            outs.append(xh * lax.rsqrt(ms + EPS) * g)
        return jnp.concatenate(outs, axis=-1)

    ya = head_rms(af_ref[...] + ab_ref[...], ag_ref[...]) * _silu(az_ref[...].astype(F32))
    yd = _sigmoid(do_ref[...].astype(F32)) * head_rms(df_ref[...] + db_ref[...], dg_ref[...])
    yd = yd * _silu(dz_ref[...].astype(F32))
    merged = None
    for i, y in enumerate((ya.astype(BF16), yb_ref[...], yc_ref[...], yd.astype(BF16))):
        proj = jnp.dot(y, wb_ref[i], preferred_element_type=F32)
        term = _sigmoid(gl_ref[:, i * d:(i + 1) * d].astype(F32)) * proj
        merged = term if merged is None else merged + term
    o_ref[...] = x_ref[...] + jnp.dot(merged.astype(BF16), wo_ref[...], preferred_element_type=F32)


def _merge(x2d, af, ab, df, db, yb, yc, pm2, ag, dg, wb, wo, tm=512):
    m, d = x2d.shape
    gw = N_BRANCH * d
    w = BRANCH_W
    tok = pl.BlockSpec((tm, w), lambda i: (i, 0))
    col = lambda name: pl.BlockSpec((tm, w), lambda i: (i, _OFF[name] // w))
    vec = pl.BlockSpec((1, LANES), lambda i: (0, 0))
    return pl.pallas_call(
        _merge_kernel,
        grid=(m // tm,),
        in_specs=[
            pl.BlockSpec((tm, d), lambda i: (i, 0)),
            tok, tok, tok, tok, tok, tok,
            col("a_z"), col("d_z"), col("d_o"),
            pl.BlockSpec((tm, gw), lambda i: (i, _OFF["gate"] // gw)),
            vec, vec,
            pl.BlockSpec((N_BRANCH, w, d), lambda i: (0, 0, 0)),
            pl.BlockSpec((d, d), lambda i: (0, 0)),
        ],
        out_specs=pl.BlockSpec((tm, d), lambda i: (i, 0)),
        out_shape=jax.ShapeDtypeStruct((m, d), F32),
        compiler_params=_cparams(("parallel",)),
        name="merge",
    )(x2d, af, ab, df, db, yb, yc, pm2, pm2, pm2, pm2, ag, dg, wb, wo)


def _rope_lane_tables(s):
    t = jnp.arange(s)
    row = (t // GRID_W).astype(F32)
    col = (t % GRID_W).astype(F32)
    m = GA_DH // 4
    inv = ROPE_THETA ** (-jnp.arange(m, dtype=F32) / m)
    ar = row[:, None] * inv
    ac = col[:, None] * inv
    cos_t = jnp.concatenate([jnp.cos(ar), jnp.cos(ar), jnp.cos(ac), jnp.cos(ac)], axis=-1)
    sin_t = jnp.concatenate([-jnp.sin(ar), jnp.sin(ar), -jnp.sin(ac), jnp.sin(ac)], axis=-1)
    return cos_t.astype(F32), sin_t.astype(F32)


def _lane_rows(rows):
    tile = jnp.zeros((SUBLANES, LANES), F32)
    for r, (off, vals) in enumerate(rows):
        vals = vals.reshape(-1).astype(F32)
        tile = tile.at[r, off:off + vals.shape[0]].set(vals)
    return tile


def kernel(x, norm_g, w_in, conv_a, dn_a_log, dn_dt_bias, dn_norm_g, na_q_norm, na_k_norm, na_rpb,
           ga_q_norm, ga_k_norm, ml_i_bias, ml_f_bias, ml_norm_g, w_branch, w_out):
    b, s, d = x.shape
    depth = w_in.shape[0]
    cos_t, sin_t = _rope_lane_tables(s)
    x2 = x.reshape(b * s, d)
    for l in range(depth):
        w = w_in[l]
        w_main = jnp.concatenate([w[:, o:o + wd] for _, o, wd in _MAIN_SEGS], axis=1).astype(BF16)
        w_small = jnp.concatenate([w[:, o:o + 8] for o in _SMALL_SRC]
                                  + [jnp.zeros((d, LANES - 32), F32)], axis=1).astype(BF16)
        pm2, ps2 = _inproj(x2, norm_g[l].reshape(1, d), w_main, w_small)
        pm3 = pm2.reshape(b, s, N_MAIN)
        ps3 = ps2.reshape(b, s, LANES)

        conv8 = jnp.zeros((SUBLANES, conv_a.shape[-1]), F32).at[:DN_CONV].set(conv_a[l])
        dn_lp = _lane_rows([(_L_AA, dn_a_log[l]), (_L_AA, dn_dt_bias[l])])
        o_af, o_ab = _dn_scan(_dn_pre(pm3, ps3, conv8, dn_lp))

        ml_lp = _lane_rows([(_L_DI, ml_i_bias[l]), (_L_DF, ml_f_bias[l])])
        h_df, h_db = _mlstm(pm3, ps3, ml_lp)

        hw = BRANCH_W
        bias = _na_bias_table(na_rpb[l], s // GRID_W)
        yb = _natten(pm3, bias, jnp.tile(na_q_norm[l], 2).reshape(1, 2 * NA_DH),
                     jnp.tile(na_k_norm[l], 2).reshape(1, 2 * NA_DH))
        yc = _gqa(pm3, cos_t, sin_t, ga_q_norm[l].reshape(1, GA_DH), ga_k_norm[l].reshape(1, GA_DH))

        x2 = _merge(x2, o_af.reshape(b * s, hw), o_ab.reshape(b * s, hw), h_df.reshape(b * s, hw),
                    h_db.reshape(b * s, hw), yb.reshape(b * s, hw), yc.reshape(b * s, hw), pm2,
                    dn_norm_g[l].reshape(1, LANES), ml_norm_g[l].reshape(1, LANES),
                    w_branch[l].astype(BF16), w_out[l].astype(BF16))
    return x2.reshape(b, s, d)
```

```python
import functools
import math

import jax
import jax.numpy as jnp
from jax import lax
from jax.experimental import pallas as pl
from jax.experimental.pallas import tpu as pltpu

F32 = jnp.float32
BF16 = jnp.bfloat16

D_MODEL = 1024
GRID_W = 64
N_BRANCH = 4
BRANCH_W = 512
EPS = 1e-6
DN_HEADS, DN_DK, DN_DV, DN_CONV, DN_CHUNK = 4, 128, 128, 5, 64
NA_HEADS, NA_DH, NA_ROWS, NA_COLS = 8, 64, 8, 16
GA_HEADS, GA_KV_HEADS, GA_DH = 4, 2, 128
ROPE_THETA = 10000.0
ML_HEADS, ML_DK, ML_DV, ML_CHUNK = 4, 64, 128, 64

LANES = 128
SUBLANES = 8
VMEM_LIMIT_BYTES = 56 * 1024 * 1024

_O_A_QKV, _O_A_A, _O_A_B, _O_A_Z = 0, 1536, 1544, 1552
_O_B_QKV, _O_B_Z = 2064, 3600
_O_C_Q, _O_C_K, _O_C_V, _O_C_Z = 4112, 4624, 4880, 5136
_O_D_Q, _O_D_K, _O_D_V, _O_D_I, _O_D_F, _O_D_O, _O_D_Z = 5648, 5904, 6160, 6672, 6680, 6688, 7200
_O_GATE = 7712
_MAIN_SEGS = (
    ("a_qkv", _O_A_QKV, 1536), ("b_qkv", _O_B_QKV, 1536), ("a_z", _O_A_Z, 512), ("b_z", _O_B_Z, 512),
    ("gate", _O_GATE, 4096), ("c_q", _O_C_Q, 512), ("c_k", _O_C_K, 256), ("c_v", _O_C_V, 256),
    ("c_z", _O_C_Z, 512), ("d_q", _O_D_Q, 256), ("d_k", _O_D_K, 256), ("d_v", _O_D_V, 512),
    ("d_o", _O_D_O, 512), ("d_z", _O_D_Z, 512),
)
_OFF = {}
_o = 0
for _name, _src, _w in _MAIN_SEGS:
    _OFF[_name] = _o
    _o += _w
N_MAIN = _o
_SMALL_SRC = (_O_A_A, _O_A_B, _O_D_I, _O_D_F)
_L_AA, _L_AB, _L_DI, _L_DF = 0, 8, 16, 24

P_DTYPE = BF16
BLK = 128
NEG = -1e30


def _cparams(sem):
    return pltpu.CompilerParams(dimension_semantics=sem, vmem_limit_bytes=VMEM_LIMIT_BYTES)


def _sigmoid(x):
    return 1.0 / (1.0 + jnp.exp(-x))


def _silu(x):
    return x * _sigmoid(x)


def _softplus(x):
    return jnp.maximum(x, 0.0) + jnp.log(1.0 + jnp.exp(-jnp.abs(x)))


def _mm(a, b):
    return jnp.dot(a.astype(BF16), b.astype(BF16), preferred_element_type=F32)


def _mm_nt(a, b):
    return lax.dot_general(a.astype(BF16), b.astype(BF16), (((1,), (1,)), ((), ())),
                           preferred_element_type=F32)


def _mm_tn(a, b):
    return lax.dot_general(a.astype(BF16), b.astype(BF16), (((0,), (0,)), ((), ())),
                           preferred_element_type=F32)


def _mask_sum(mask, x):
    m = jnp.where(mask, 1.0, 0.0).astype(BF16)
    x1 = x.astype(BF16)
    r1 = x - x1.astype(F32)
    x2 = r1.astype(BF16)
    x3 = (r1 - x2.astype(F32)).astype(BF16)
    dot = lambda v: jnp.dot(m, v, preferred_element_type=F32)
    return dot(x1) + (dot(x2) + dot(x3))


def _chunk_masks(reverse):
    i = lax.broadcasted_iota(jnp.int32, (BLK, BLK), 0)
    j = lax.broadcasted_iota(jnp.int32, (BLK, BLK), 1)
    shift = int(math.log2(DN_CHUNK))
    same = (i >> shift) == (j >> shift)
    if reverse:
        incl = same & (j >= i)
        strict = same & (j > i)
    else:
        incl = same & (j <= i)
        strict = same & (j < i)
    return same, incl, strict, (i == j)


def _as_f32(mask):
    return jnp.where(mask, 1.0, 0.0).astype(F32)


def _inproj_kernel(x_ref, g_ref, w_ref, ws_ref, pm_ref, ps_ref, h_scr):
    @pl.when(pl.program_id(1) == 0)
    def _():
        x = x_ref[...]
        ms = jnp.mean(x * x, axis=-1, keepdims=True)
        h = (x * lax.rsqrt(ms + EPS) * g_ref[...]).astype(BF16)
        h_scr[...] = h
        ps_ref[...] = jnp.dot(h, ws_ref[...], preferred_element_type=F32)

    pm_ref[...] = jnp.dot(h_scr[...], w_ref[...], preferred_element_type=F32).astype(pm_ref.dtype)


def _inproj(x2d, g, w_main, w_small, tm=1024, tn=N_MAIN // 4):
    m, d = x2d.shape
    tm = min(tm, m)
    return pl.pallas_call(
        _inproj_kernel,
        grid=(m // tm, N_MAIN // tn),
        in_specs=[
            pl.BlockSpec((tm, d), lambda i, j: (i, 0)),
            pl.BlockSpec((1, d), lambda i, j: (0, 0)),
            pl.BlockSpec((d, tn), lambda i, j: (0, j)),
            pl.BlockSpec((d, LANES), lambda i, j: (0, 0)),
        ],
        out_specs=[
            pl.BlockSpec((tm, tn), lambda i, j: (i, j)),
            pl.BlockSpec((tm, LANES), lambda i, j: (i, 0)),
        ],
        out_shape=[jax.ShapeDtypeStruct((m, N_MAIN), P_DTYPE), jax.ShapeDtypeStruct((m, LANES), F32)],
        scratch_shapes=[pltpu.VMEM((tm, d), BF16)],
        compiler_params=_cparams(("parallel", "arbitrary")),
        name="inproj",
    )(x2d, g, w_main, w_small)


GQA_TQ = 512


def _gqa_kernel(q_ref, k_ref, v_ref, z_ref, cos_ref, sin_ref, qg_ref, kg_ref, y_ref,
                q_scr, k_scr, v_scr, s0_scr, s1_scr, p0_scr, p1_scr, l0_scr, l1_scr):
    def norm_rope(x, g, cos, sin):
        ms = jnp.mean(x * x, axis=-1, keepdims=True)
        xn = x * lax.rsqrt(ms + EPS) * g
        lane = lax.broadcasted_iota(jnp.int32, xn.shape, 1)
        partner = jnp.where((lane & 63) < 32, pltpu.roll(xn, LANES - 32, 1), pltpu.roll(xn, 32, 1))
        return xn * cos + partner * sin

    s_len = k_ref.shape[0]
    group = GA_HEADS // GA_KV_HEADS
    k_scr[...] = norm_rope(k_ref[...].astype(F32), kg_ref[...], cos_ref[...], sin_ref[...]).astype(BF16)
    v_scr[...] = v_ref[...].astype(BF16)
    assert group == 2
    scale = GA_DH ** -0.5
    n_blk = s_len // GQA_TQ
    s_bufs, p_bufs, l_bufs = (s0_scr, s1_scr), (p0_scr, p1_scr), (l0_scr, l1_scr)

    def stacked(head, blk):
        return pl.ds(pl.multiple_of(head * s_len + blk * GQA_TQ, GQA_TQ), GQA_TQ)

    def prep(head, blk):
        rows = pl.ds(pl.multiple_of(blk * GQA_TQ, GQA_TQ), GQA_TQ)
        q = norm_rope(q_ref[rows, head * GA_DH:(head + 1) * GA_DH].astype(F32), qg_ref[...],
                      cos_ref[rows, :], sin_ref[rows, :])
        q_scr[stacked(head, blk), :] = q.astype(BF16)

    def logits(head, blk):
        s_bufs[head][...] = _mm_nt(q_scr[stacked(head, blk), :], k_scr[...]) * scale

    def softmax(slot):
        s = s_bufs[slot][...]
        p = jnp.exp(s - jnp.max(s, axis=-1, keepdims=True))
        l_bufs[slot][...] = jnp.broadcast_to(jnp.sum(p, axis=-1, keepdims=True), (GQA_TQ, GA_DH))
        p_bufs[slot][...] = p.astype(BF16)

    def weighted(head, blk):
        rows = pl.ds(pl.multiple_of(blk * GQA_TQ, GQA_TQ), GQA_TQ)
        cols = slice(head * GA_DH, (head + 1) * GA_DH)
        o = _mm(p_bufs[head][...], v_scr[...]) / l_bufs[head][...]
        y_ref[rows, cols] = (o * _silu(z_ref[rows, cols].astype(F32))).astype(y_ref.dtype)

    prep(0, 0)
    prep(1, 0)
    prep(0, 1)
    logits(0, 0)
    logits(1, 0)
    softmax(0)

    def body(j, carry):
        logits(0, j + 1)
        softmax(1)
        weighted(0, j)
        prep(1, j + 1)
        logits(1, j + 1)
        softmax(0)
        weighted(1, j)
        prep(0, jnp.minimum(j + 2, n_blk - 1))
        return carry

    lax.fori_loop(0, n_blk - 1, body, 0)
    softmax(1)
    weighted(0, n_blk - 1)
    weighted(1, n_blk - 1)


def _gqa(pm3, cos_t, sin_t, qg, kg):
    b, s, _ = pm3.shape
    gw = (GA_HEADS // GA_KV_HEADS) * GA_DH
    return pl.pallas_call(
        _gqa_kernel,
        grid=(b, GA_KV_HEADS),
        in_specs=[
            pl.BlockSpec((None, s, gw), lambda i, j: (i, 0, _OFF["c_q"] // gw + j)),
            pl.BlockSpec((None, s, GA_DH), lambda i, j: (i, 0, _OFF["c_k"] // GA_DH + j)),
            pl.BlockSpec((None, s, GA_DH), lambda i, j: (i, 0, _OFF["c_v"] // GA_DH + j)),
            pl.BlockSpec((None, s, gw), lambda i, j: (i, 0, _OFF["c_z"] // gw + j)),
            pl.BlockSpec((s, GA_DH), lambda i, j: (0, 0)),
            pl.BlockSpec((s, GA_DH), lambda i, j: (0, 0)),
            pl.BlockSpec((1, GA_DH), lambda i, j: (0, 0)),
            pl.BlockSpec((1, GA_DH), lambda i, j: (0, 0)),
        ],
        out_specs=pl.BlockSpec((None, s, gw), lambda i, j: (i, 0, j)),
        out_shape=jax.ShapeDtypeStruct((b, s, BRANCH_W), BF16),
        scratch_shapes=[pltpu.VMEM((gw // GA_DH * s, GA_DH), BF16), pltpu.VMEM((s, GA_DH), BF16),
                        pltpu.VMEM((s, GA_DH), BF16),
                        pltpu.VMEM((GQA_TQ, s), F32), pltpu.VMEM((GQA_TQ, s), F32),
                        pltpu.VMEM((GQA_TQ, s), BF16), pltpu.VMEM((GQA_TQ, s), BF16),
                        pltpu.VMEM((GQA_TQ, GA_DH), F32), pltpu.VMEM((GQA_TQ, GA_DH), F32)],
        compiler_params=_cparams(("parallel", "parallel")),
        name="gqa",
    )(pm3, pm3, pm3, pm3, cos_t, sin_t, qg, kg)


NA_ROW_UNROLL = 16


def _na_kernel(q_ref, k_ref, v_ref, z_ref, bias_ref, qg_ref, kg_ref, y_ref, q_scr, k_scr, v_scr, o_scr):
    s_len = q_ref.shape[0]
    rows = s_len // GRID_W
    kr = min(NA_ROWS, rows)
    hi = lax.broadcasted_iota(jnp.int32, (2 * NA_DH, 2 * NA_DH), 0) >= NA_DH
    hj = lax.broadcasted_iota(jnp.int32, (2 * NA_DH, 2 * NA_DH), 1) >= NA_DH
    same_head = jnp.where(hi == hj, 1.0, 0.0).astype(BF16)

    def rms_pair(x, g):
        x2 = x * x
        x2_hi = x2.astype(BF16)
        x2_lo = (x2 - x2_hi.astype(F32)).astype(BF16)
        ssq = (jnp.dot(x2_hi, same_head, preferred_element_type=F32)
               + jnp.dot(x2_lo, same_head, preferred_element_type=F32))
        return x * lax.rsqrt(ssq * (1.0 / NA_DH) + EPS) * g

    scale = NA_DH ** -0.5
    assert math.log2(scale).is_integer()
    qn = rms_pair(q_ref[...].astype(F32), qg_ref[...]) * scale
    kn = rms_pair(k_ref[...].astype(F32), kg_ref[...])
    vf = v_ref[...].astype(F32)
    for hh in range(2):
        cols = slice(hh * NA_DH, (hh + 1) * NA_DH)
        q_scr[hh] = qn[:, cols].astype(BF16)
        k_scr[hh] = kn[:, cols].astype(BF16)
        v_scr[hh] = vf[:, cols].astype(BF16)
    nkeys = kr * GRID_W

    def body(it, carry):
        units = []
        for u in range(NA_ROW_UNROLL):
            r = it * NA_ROW_UNROLL + u
            r0 = jnp.clip(r - kr // 2, 0, rows - kr)
            var = r0 - r + (NA_ROWS - 1)
            qrows = pl.ds(pl.multiple_of(r * GRID_W, GRID_W), GRID_W)
            krows = pl.ds(pl.multiple_of(r0 * GRID_W, GRID_W), nkeys)
            units += [(hh, var, qrows, krows) for hh in range(2)]
        logits = [_mm_nt(q_scr[hh, qrows, :], k_scr[hh, krows, :]) + bias_ref[hh, var]
                  for hh, var, qrows, krows in units]
        probs = [jnp.exp(s - jnp.max(s, axis=-1, keepdims=True)) for s in logits]
        sums = [jnp.sum(p, axis=-1, keepdims=True) for p in probs]
        outs = [_mm(p, v_scr[hh, krows, :]) for p, (hh, _, _, krows) in zip(probs, units)]
        for o, l, (hh, _, qrows, _) in zip(outs, sums, units):
            o_scr[hh, qrows, :] = o / l
        return carry

    lax.fori_loop(0, rows // NA_ROW_UNROLL, body, 0)
    o = jnp.concatenate([o_scr[0], o_scr[1]], axis=-1)
    y_ref[...] = (o * _silu(z_ref[...].astype(F32))).astype(y_ref.dtype)


def _na_bias_table(rpb, rows):
    kr = min(NA_ROWS, rows)
    c = jnp.arange(GRID_W)
    c0 = jnp.clip(c - NA_COLS // 2, 0, GRID_W - NA_COLS)
    in_win = (c[None, :] >= c0[:, None]) & (c[None, :] < c0[:, None] + NA_COLS)
    col_off = jnp.clip(c[None, :] - c[:, None], -(NA_COLS - 1), NA_COLS - 1) + NA_COLS - 1
    t = rpb[:, :, col_off]
    t = jnp.where(in_win[None, None], t, NEG)
    ro = jnp.arange(NA_ROWS)[:, None] + jnp.arange(kr)[None, :]
    tv = t[:, ro]
    tv = tv.transpose(0, 1, 3, 2, 4)
    return tv.reshape(rpb.shape[0], NA_ROWS, GRID_W, kr * GRID_W).astype(F32)


def _natten(pm3, bias, qg, kg):
    b, s, _ = pm3.shape
    pw = 2 * NA_DH
    npair = NA_HEADS // 2
    hw = NA_HEADS * NA_DH
    bias5 = bias.reshape(npair, 2, *bias.shape[1:])
    return pl.pallas_call(
        _na_kernel,
        grid=(npair, b),
        in_specs=[
            pl.BlockSpec((None, s, pw), lambda p, i: (i, 0, _OFF["b_qkv"] // pw + p)),
            pl.BlockSpec((None, s, pw), lambda p, i: (i, 0, (_OFF["b_qkv"] + hw) // pw + p)),
            pl.BlockSpec((None, s, pw), lambda p, i: (i, 0, (_OFF["b_qkv"] + 2 * hw) // pw + p)),
            pl.BlockSpec((None, s, pw), lambda p, i: (i, 0, _OFF["b_z"] // pw + p)),
            pl.BlockSpec((None,) + bias5.shape[1:], lambda p, i: (p, 0, 0, 0, 0)),
            pl.BlockSpec((1, pw), lambda p, i: (0, 0)),
            pl.BlockSpec((1, pw), lambda p, i: (0, 0)),
        ],
        out_specs=pl.BlockSpec((None, s, pw), lambda p, i: (i, 0, p)),
        out_shape=jax.ShapeDtypeStruct((b, s, BRANCH_W), BF16),
        scratch_shapes=[pltpu.VMEM((2, s, NA_DH), BF16), pltpu.VMEM((2, s, NA_DH), BF16),
                        pltpu.VMEM((2, s, NA_DH), BF16), pltpu.VMEM((2, s, NA_DH), F32)],
        compiler_params=_cparams(("parallel", "parallel")),
        name="natten",
    )(pm3, pm3, pm3, pm3, bias5, qg, kg)


INV_BASE = 8
PRE_BLK = 2 * BLK


def _inverse_level_masks():
    i = lax.broadcasted_iota(jnp.int32, (BLK, BLK), 0)
    j = lax.broadcasted_iota(jnp.int32, (BLK, BLK), 1)
    same = lambda size: (i >> int(math.log2(size))) == (j >> int(math.log2(size)))
    base = same(INV_BASE)
    joins = []
    size = INV_BASE
    while size < DN_CHUNK:
        joins.append(same(2 * size) & jnp.logical_not(same(size)))
        size *= 2
    return base, joins


def _tri_inverses(l_mats, eyes, level_masks):
    base, joins = level_masks
    ps = [jnp.where(base, -l, 0.0) for l in l_mats]
    ts = [eye + p for eye, p in zip(eyes, ps)]
    for _ in range(int(math.log2(INV_BASE)) - 1):
        ps = [_mm(p, p) for p in ps]
        ts = [t + _mm(t, p) for t, p in zip(ts, ps)]
    for join in joins:
        mids = [_mm(jnp.where(join, l, 0.0), t) for l, t in zip(l_mats, ts)]
        ts = [t - _mm(t, mid) for t, mid in zip(ts, mids)]
    return ts


DN_HW = DN_HEADS * LANES
PK_W, PK_QD, PK_KD, PK_QK = (i * DN_HW for i in range(4))


def _dn_pre_kernel(qkv_ref, sm_ref, cw_ref, lp_ref, uf_ref, ub_ref, pkf_ref, pkb_ref, gtf_ref, gtb_ref):
    n = pl.program_id(1)
    s_len = qkv_ref.shape[0]
    assert qkv_ref.dtype == BF16
    halo = 2 * SUBLANES
    n_sub = sm_ref.shape[0] // BLK
    chunks_per_blk = BLK // DN_CHUNK
    out_row = lax.broadcasted_iota(jnp.int32, (BLK, BLK + 2 * halo), 0)
    in_row = lax.broadcasted_iota(jnp.int32, (BLK, BLK + 2 * halo), 1)
    shifts = {j: jnp.where(in_row == out_row + (halo + j - DN_CONV // 2), 1.0, 0.0).astype(BF16)
              for j in range(DN_CONV) if j != DN_CONV // 2}
    dir_masks = [_chunk_masks(reverse=(d == 1)) for d in range(2)]
    level_masks = _inverse_level_masks()
    outs = ((uf_ref, pkf_ref, gtf_ref), (ub_ref, pkb_ref, gtb_ref))

    def l2n(x):
        return x * lax.rsqrt(jnp.sum(x * x, axis=-1, keepdims=True) + EPS)

    blocks = []
    for sub in range(n_sub):
        t0 = pl.multiple_of((n * n_sub + sub) * BLK, BLK)
        pstart = pl.multiple_of(jnp.maximum(t0 - halo, 0), halo)
        nstart = pl.multiple_of(jnp.minimum(t0 + BLK, s_len - halo), halo)
        prev = qkv_ref[pl.ds(pstart, halo), :]
        prev = jnp.where(t0 > 0, prev, jnp.zeros_like(prev))
        cur = qkv_ref[pl.ds(t0, BLK), :]
        nxt = qkv_ref[pl.ds(nstart, halo), :]
        nxt = jnp.where(t0 + BLK < s_len, nxt, jnp.zeros_like(nxt))
        xw = jnp.concatenate([prev, cur, nxt], axis=0)
        conv = cur.astype(F32) * cw_ref[DN_CONV // 2:DN_CONV // 2 + 1, :]
        for j, shift in shifts.items():
            conv = conv + jnp.dot(shift, xw, preferred_element_type=F32) * cw_ref[j:j + 1, :]
        conv = _silu(conv)

        rows = slice(sub * BLK, (sub + 1) * BLK)
        sm = sm_ref[rows, :]
        g_all = -jnp.exp(lp_ref[0:1, :]) * _softplus(sm + lp_ref[1:2, :])
        beta_all = _sigmoid(sm)
        per_dir = []
        for d in range(2):
            same, incl, strict, eye_b = dir_masks[d]
            gc = _mask_sum(incl, g_all)
            tot = _mask_sum(same, g_all)
            per_dir.append((incl, strict, _as_f32(eye_b), gc, gc.T, tot))
        blocks.append((sub, rows, conv, beta_all, per_dir))

    heads = []
    for sub, rows, conv, beta_all, per_dir in blocks:
        for h in range(DN_HEADS):
            q = l2n(conv[:, h * DN_DK:(h + 1) * DN_DK]) * (DN_DK ** -0.5)
            k = l2n(conv[:, (DN_HEADS + h) * DN_DK:(DN_HEADS + h + 1) * DN_DK])
            v = conv[:, 2 * DN_HEADS * DN_DK + h * DN_DV:2 * DN_HEADS * DN_DK + (h + 1) * DN_DV]
            heads.append((sub, rows, beta_all, per_dir, h, q, k, v))
    kks = [_mm_nt(hd[6], hd[6]) for hd in heads]
    qks = [_mm_nt(hd[5], hd[6]) for hd in heads]

    l_mats, rhss, eyes, slots = [], [], [], []
    for (sub, rows, beta_all, per_dir, h, q, k, v), kk, qk in zip(heads, kks, qks):
        hc = slice(h * LANES, (h + 1) * LANES)
        for d in range(2):
            incl, strict, eye_f, gc, gct, tot = per_dir[d]
            u_ref, pk_ref, gt_ref = outs[d]
            pk = lambda off: slice(off + h * LANES, off + (h + 1) * LANES)
            c = _L_AA + d * DN_HEADS + h
            gcol = gc[:, c:c + 1]
            grow = gct[c:c + 1, :]
            tcol = tot[:, c:c + 1]
            beta = beta_all[:, _L_AB + d * DN_HEADS + h:_L_AB + d * DN_HEADS + h + 1]
            decay = jnp.where(incl, jnp.exp(jnp.where(incl, gcol - grow, 0.0)), 0.0)
            l_mats.append(jnp.where(strict, beta * kk * decay, 0.0))
            egc = jnp.exp(gcol)
            rhss.append(jnp.concatenate([v * beta, k * (beta * egc)], axis=-1).astype(BF16))
            eyes.append(eye_f)
            slots.append((rows, h, d))
            pk_ref[rows, pk(PK_QD)] = (q * egc).astype(pk_ref.dtype)
            pk_ref[rows, pk(PK_KD)] = (k * jnp.exp(tcol - gcol)).astype(pk_ref.dtype)
            pk_ref[rows, pk(PK_QK)] = (qk * decay).astype(pk_ref.dtype)
            gtot = jnp.exp(tcol)
            for ci in range(chunks_per_blk):
                g0 = (sub * chunks_per_blk + ci) * SUBLANES
                gt_ref[g0:g0 + SUBLANES, hc] = jnp.broadcast_to(
                    gtot[ci * DN_CHUNK:ci * DN_CHUNK + SUBLANES, :], (SUBLANES, LANES))

    t_invs = _tri_inverses(l_mats, eyes, level_masks)
    sols = [_mm(t, rhs) for t, rhs in zip(t_invs, rhss)]
    for (rows, h, d), sol in zip(slots, sols):
        u_ref, pk_ref = outs[d][0], outs[d][1]
        u_ref[rows, h * LANES:(h + 1) * LANES] = sol[:, :DN_DV]
        pk_ref[rows, PK_W + h * LANES:PK_W + (h + 1) * LANES] = sol[:, DN_DV:].astype(pk_ref.dtype)


def _dn_pre(pm3, ps3, conv_w8, lane_params):
    b, s, _ = pm3.shape
    nblk = s // PRE_BLK
    gt_rows = PRE_BLK // DN_CHUNK * SUBLANES
    wq = DN_HEADS * (2 * DN_DK + DN_DV)
    hw = DN_HEADS * LANES
    tok = lambda i, j: (i, j, 0)
    big = lambda dt: jax.ShapeDtypeStruct((b, s, hw), dt)
    gts = jax.ShapeDtypeStruct((b, nblk * gt_rows, hw), F32)
    bs_tok = pl.BlockSpec((None, PRE_BLK, hw), tok)
    bs_gt = pl.BlockSpec((None, gt_rows, hw), tok)
    bs_pk = pl.BlockSpec((None, PRE_BLK, 4 * hw), tok)
    packed = jax.ShapeDtypeStruct((b, s, 4 * hw), BF16)
    return pl.pallas_call(
        _dn_pre_kernel,
        grid=(b, nblk),
        in_specs=[
            pl.BlockSpec((None, s, wq), lambda i, j: (i, 0, _OFF["a_qkv"] // wq)),
            pl.BlockSpec((None, PRE_BLK, LANES), tok),
            pl.BlockSpec((SUBLANES, wq), lambda i, j: (0, 0)),
            pl.BlockSpec((SUBLANES, LANES), lambda i, j: (0, 0)),
        ],
        out_specs=[bs_tok, bs_tok, bs_pk, bs_pk, bs_gt, bs_gt],
        out_shape=[big(F32), big(F32), packed, packed, gts, gts],
        compiler_params=_cparams(("parallel", "arbitrary")),
        name="dn_pre",
    )(pm3, ps3, conv_w8, lane_params)


SCAN_BLK = 2 * BLK


def _dn_scan_kernel(uf_ref, pkf_ref, gtf_ref, ub_ref, pkb_ref, gtb_ref, of_ref, ob_ref, st_scr):
    @pl.when(pl.program_id(1) == 0)
    def _():
        st_scr[...] = jnp.zeros_like(st_scr)

    per_blk = BLK // DN_CHUNK
    nchunk = SCAN_BLK // DN_CHUNK
    zeros_c = jnp.zeros((DN_CHUNK, DN_DV), F32)
    streams = ((uf_ref, pkf_ref, gtf_ref, of_ref, range(nchunk)),
               (ub_ref, pkb_ref, gtb_ref, ob_ref, range(nchunk - 1, -1, -1)))
    chains = [(d, h) + streams[d] for d in range(2) for h in range(DN_HEADS)]
    states = [st_scr[d * DN_HEADS + h] for d, h, *_ in chains]
    for step in range(nchunk):
        rs, v_pads = [], []
        for (d, h, u_ref, pk_ref, gt_ref, o_ref, order), state in zip(chains, states):
            rows = slice(order[step] * DN_CHUNK, (order[step] + 1) * DN_CHUNK)
            w = pk_ref[rows, PK_W + h * LANES:PK_W + (h + 1) * LANES]
            qd = pk_ref[rows, PK_QD + h * LANES:PK_QD + (h + 1) * LANES]
            rs.append(_mm(jnp.concatenate([w, qd], axis=0), state))
        for (d, h, u_ref, pk_ref, gt_ref, o_ref, order), r in zip(chains, rs):
            ci = order[step]
            rows = slice(ci * DN_CHUNK, (ci + 1) * DN_CHUNK)
            parts = [zeros_c] * per_blk
            parts[ci % per_blk] = u_ref[rows, h * LANES:(h + 1) * LANES] - r[:DN_CHUNK]
            v_pads.append(jnp.concatenate(parts, axis=0))
        new_states = []
        for (d, h, u_ref, pk_ref, gt_ref, o_ref, order), r, v_pad, state in zip(chains, rs, v_pads, states):
            hc = slice(h * LANES, (h + 1) * LANES)
            ci = order[step]
            rows = slice(ci * DN_CHUNK, (ci + 1) * DN_CHUNK)
            blk_rows = slice((ci // per_blk) * BLK, (ci // per_blk + 1) * BLK)
            o_ref[rows, hc] = r[DN_CHUNK:] + _mm(pk_ref[rows, PK_QK + h * LANES:PK_QK + (h + 1) * LANES], v_pad)
            gt = gt_ref[ci * SUBLANES:ci * SUBLANES + 1, hc]
            kd = pk_ref[blk_rows, PK_KD + h * LANES:PK_KD + (h + 1) * LANES]
            new_states.append(state * gt + _mm_tn(kd, v_pad))
        states = new_states
    for (d, h, *_), state in zip(chains, states):
        st_scr[d * DN_HEADS + h] = state


def _dn_scan(pre):
    uf, ub, pkf, pkb, gtf, gtb = pre
    b, s, hw = uf.shape
    nblk = s // SCAN_BLK
    gt_rows = SCAN_BLK // DN_CHUNK * SUBLANES
    fwd = lambda i, j: (i, j, 0)
    bwd = lambda i, j: (i, nblk - 1 - j, 0)
    def specs(imap):
        return [pl.BlockSpec((None, SCAN_BLK, hw), imap), pl.BlockSpec((None, SCAN_BLK, 4 * hw), imap),
                pl.BlockSpec((None, gt_rows, hw), imap)]
    return pl.pallas_call(
        _dn_scan_kernel,
        grid=(b, nblk),
        in_specs=specs(fwd) + specs(bwd),
        out_specs=[pl.BlockSpec((None, SCAN_BLK, hw), fwd), pl.BlockSpec((None, SCAN_BLK, hw), bwd)],
        out_shape=[jax.ShapeDtypeStruct((b, s, hw), F32)] * 2,
        scratch_shapes=[pltpu.VMEM((2 * DN_HEADS, DN_DK, DN_DV), F32)],
        compiler_params=_cparams(("parallel", "arbitrary")),
        name="dn_scan",
    )(uf, pkf, gtf, ub, pkb, gtb)


ML_AUG = 2 * LANES
ML_GROUP = 4


def _ml_kernel(qf_ref, kf_ref, vf_ref, smf_ref, qb_ref, kb_ref, vb_ref, smb_ref, lp_ref,
               hf_ref, hb_ref, c_scr, m_scr):
    @pl.when(pl.program_id(1) == 0)
    def _():
        c_scr[...] = jnp.zeros_like(c_scr)
        m_scr[...] = jnp.zeros_like(m_scr)

    nchunk = BLK // ML_CHUNK
    lane = lax.broadcasted_iota(jnp.int32, (BLK, LANES), 1)
    ones_col = jnp.where(lane == 0, 1.0, 0.0).astype(BF16)
    zeros_aug = jnp.zeros((ML_CHUNK, ML_AUG), BF16)
    streams = ((qf_ref, kf_ref, vf_ref, smf_ref, hf_ref, range(nchunk)),
               (qb_ref, kb_ref, vb_ref, smb_ref, hb_ref, range(nchunk - 1, -1, -1)))
    gates = []
    for d, (q_ref, k_ref, v_ref, sm_ref, h_ref, order) in enumerate(streams):
        same, incl, _, _ = _chunk_masks(reverse=(d == 1))
        sm = sm_ref[...]
        ig_all = sm + lp_ref[0:1, :]
        x = sm + lp_ref[1:2, :]
        lf_all = jnp.minimum(x, 0.0) - jnp.log(1.0 + jnp.exp(-jnp.abs(x)))
        lf_all = pltpu.roll(lf_all, LANES - (_L_DF - _L_DI), 1)
        bc_all = _mask_sum(incl, lf_all)
        tot_all = _mask_sum(same, lf_all)
        a_all = ig_all - bc_all
        mwa_all = jnp.concatenate(
            [jnp.broadcast_to(jnp.max(a_all[ci * ML_CHUNK:(ci + 1) * ML_CHUNK], axis=0, keepdims=True),
                              (ML_CHUNK, LANES)) for ci in range(nchunk)], axis=0)
        gates.append((same, incl, bc_all, tot_all, a_all.T, jnp.exp(a_all - mwa_all), tot_all + mwa_all))

    lanes = lambda col: jnp.broadcast_to(col, (col.shape[0], LANES))
    groups = [[(d, h) for h in range(g, g + ML_GROUP)] for d in range(2) for g in range(0, ML_HEADS, ML_GROUP)]
    for chains in groups:
        ins = []
        for d, h in chains:
            q_ref, k_ref, v_ref = streams[d][0], streams[d][1], streams[d][2]
            q = q_ref[:, h * ML_DK:(h + 1) * ML_DK].astype(BF16)
            k = k_ref[:, h * ML_DK:(h + 1) * ML_DK].astype(F32) * (ML_DK ** -0.5)
            v_aug = jnp.concatenate([v_ref[:, h * ML_DV:(h + 1) * ML_DV].astype(BF16), ones_col], axis=-1)
            ins.append((q, k, v_aug))
        qks = [_mm_nt(q, k) for q, k, _ in ins]

        mids = []
        for (d, h), (q, k, v_aug), qk in zip(chains, ins, qks):
            same, incl, bc_all, tot_all, a_t, w_all, mw_all = gates[d]
            c = _L_DI + d * ML_HEADS + h
            b_l = lanes(bc_all[:, c:c + 1])
            dlog = jnp.where(incl, b_l + a_t[c:c + 1, :], NEG)
            m_intra = lanes(jnp.max(dlog, axis=-1, keepdims=True))
            s_intra = qk * jnp.exp(dlog - m_intra)
            r_intra = lanes(jnp.sum(s_intra, axis=-1, keepdims=True))
            wk = (k * lanes(w_all[:, c:c + 1])[:, :ML_DK]).astype(BF16)
            mids.append((b_l, m_intra, s_intra.astype(BF16), r_intra, wk))
        p_intras = [_mm(s_b, v_aug)[:, :ML_DV] for (_, _, s_b, _, _), (_, _, v_aug) in zip(mids, ins)]
        kvs = []
        for (_, _, _, _, wk), (_, _, v_aug) in zip(mids, ins):
            per_chunk = []
            for ci in range(nchunk):
                parts = [zeros_aug] * nchunk
                parts[ci] = v_aug[ci * ML_CHUNK:(ci + 1) * ML_CHUNK]
                per_chunk.append(_mm_tn(wk, jnp.concatenate(parts, axis=0)))
            kvs.append(per_chunk)

        c_sts = [c_scr[d * ML_HEADS + h] for d, h in chains]
        m_sts = [m_scr[d * ML_HEADS + h][0:1, :] for d, h in chains]
        for step in range(nchunk):
            qcs = []
            for (d, h), (q, _, _), c_st in zip(chains, ins, c_sts):
                ci = streams[d][5][step]
                qcs.append(_mm(q[ci * ML_CHUNK:(ci + 1) * ML_CHUNK], c_st))
            for idx, (d, h) in enumerate(chains):
                tot_all, mw_all = gates[d][3], gates[d][6]
                b_l, m_intra, _, r_intra, _ = mids[idx]
                c = _L_DI + d * ML_HEADS + h
                h_ref = streams[d][4]
                ci = streams[d][5][step]
                rows = slice(ci * ML_CHUNK, (ci + 1) * ML_CHUNK)
                r8 = slice(ci * ML_CHUNK, ci * ML_CHUNK + SUBLANES)
                m_st, c_st, qc = m_sts[idx], c_sts[idx], qcs[idx]
                m_inter = b_l[rows] + m_st
                m_i = jnp.maximum(m_intra[rows], m_inter)
                f_i = jnp.exp(m_intra[rows] - m_i)
                inter = jnp.exp(m_inter - m_i)
                numer = inter * qc[:, :ML_DV] + f_i * p_intras[idx][rows]
                denom = inter * lanes(qc[:, ML_DV:ML_DV + 1]) + f_i * r_intra[rows]
                h_ref[rows, h * ML_DV:(h + 1) * ML_DV] = numer / jnp.maximum(jnp.abs(denom), jnp.exp(-m_i))
                tot_c = lanes(tot_all[r8, c:c + 1])[0:1]
                mw_c = lanes(mw_all[r8, c:c + 1])[0:1]
                m_new = jnp.maximum(tot_c + m_st, mw_c)
                dec = jnp.exp(tot_c + m_st - m_new)
                gain = jnp.exp(mw_c - m_new)
                c_sts[idx] = (jnp.concatenate([dec, dec], axis=-1) * c_st
                              + jnp.concatenate([gain, gain], axis=-1) * kvs[idx][ci])
                m_sts[idx] = m_new
        for idx, (d, h) in enumerate(chains):
            c_scr[d * ML_HEADS + h] = c_sts[idx]
            m_scr[d * ML_HEADS + h] = jnp.broadcast_to(m_sts[idx], (SUBLANES, LANES))


def _mlstm(pm3, ps3, lane_params):
    b, s, _ = pm3.shape
    nblk = s // BLK
    qw = ML_HEADS * ML_DK
    vw = ML_HEADS * ML_DV
    def specs(tmap):
        blk = lambda j: tmap(j)
        return [
            pl.BlockSpec((None, BLK, qw), lambda i, j: (i, blk(j), _OFF["d_q"] // qw)),
            pl.BlockSpec((None, BLK, qw), lambda i, j: (i, blk(j), _OFF["d_k"] // qw)),
            pl.BlockSpec((None, BLK, vw), lambda i, j: (i, blk(j), _OFF["d_v"] // vw)),
            pl.BlockSpec((None, BLK, LANES), lambda i, j: (i, blk(j), 0)),
        ]
    fwd = lambda j: j
    bwd = lambda j: nblk - 1 - j
    return pl.pallas_call(
        _ml_kernel,
        grid=(b, nblk),
        in_specs=specs(fwd) + specs(bwd) + [pl.BlockSpec((SUBLANES, LANES), lambda i, j: (0, 0))],
        out_specs=[pl.BlockSpec((None, BLK, vw), lambda i, j: (i, j, 0)),
                   pl.BlockSpec((None, BLK, vw), lambda i, j: (i, nblk - 1 - j, 0))],
        out_shape=[jax.ShapeDtypeStruct((b, s, vw), F32)] * 2,
        scratch_shapes=[pltpu.VMEM((2 * ML_HEADS, ML_DK, ML_AUG), F32),
                        pltpu.VMEM((2 * ML_HEADS, SUBLANES, LANES), F32)],
        compiler_params=_cparams(("parallel", "arbitrary")),
        name="mlstm",
    )(pm3, pm3, pm3, ps3, pm3, pm3, pm3, ps3, lane_params)


def _merge_kernel(x_ref, af_ref, ab_ref, df_ref, db_ref, yb_ref, yc_ref, az_ref, dz_ref, do_ref, gl_ref,
                  ag_ref, dg_ref, wb_ref, wo_ref, o_ref):
    d = x_ref.shape[-1]

    def head_rms(x, g):
        outs = []
        for h in range(x.shape[-1] // LANES):
            xh = x[:, h * LANES:(h + 1) * LANES]
            ms = jnp.mean(xh * xh, axis=-1, keepdims=True)
            outs.append(xh * lax.rsqrt(ms + EPS) * g)
        return jnp.concatenate(outs, axis=-1)

    ya = head_rms(af_ref[...] + ab_ref[...], ag_ref[...]) * _silu(az_ref[...].astype(F32))
    yd = _sigmoid(do_ref[...].astype(F32)) * head_rms(df_ref[...] + db_ref[...], dg_ref[...])
    yd = yd * _silu(dz_ref[...].astype(F32))
    merged = None
    for i, y in enumerate((ya.astype(BF16), yb_ref[...], yc_ref[...], yd.astype(BF16))):
        proj = jnp.dot(y, wb_ref[i], preferred_element_type=F32)
        term = _sigmoid(gl_ref[:, i * d:(i + 1) * d].astype(F32)) * proj
        merged = term if merged is None else merged + term
    o_ref[...] = x_ref[...] + jnp.dot(merged.astype(BF16), wo_ref[...], preferred_element_type=F32)


def _merge(x2d, af, ab, df, db, yb, yc, pm2, ag, dg, wb, wo, tm=512):
    m, d = x2d.shape
    gw = N_BRANCH * d
    w = BRANCH_W
    tok = pl.BlockSpec((tm, w), lambda i: (i, 0))
    col = lambda name: pl.BlockSpec((tm, w), lambda i: (i, _OFF[name] // w))
    vec = pl.BlockSpec((1, LANES), lambda i: (0, 0))
    return pl.pallas_call(
        _merge_kernel,
        grid=(m // tm,),
        in_specs=[
            pl.BlockSpec((tm, d), lambda i: (i, 0)),
            tok, tok, tok, tok, tok, tok,
            col("a_z"), col("d_z"), col("d_o"),
            pl.BlockSpec((tm, gw), lambda i: (i, _OFF["gate"] // gw)),
            vec, vec,
            pl.BlockSpec((N_BRANCH, w, d), lambda i: (0, 0, 0)),
            pl.BlockSpec((d, d), lambda i: (0, 0)),
        ],
        out_specs=pl.BlockSpec((tm, d), lambda i: (i, 0)),
        out_shape=jax.ShapeDtypeStruct((m, d), F32),
        compiler_params=_cparams(("parallel",)),
        name="merge",
    )(x2d, af, ab, df, db, yb, yc, pm2, pm2, pm2, pm2, ag, dg, wb, wo)


def _rope_lane_tables(s):
    t = jnp.arange(s)
    row = (t // GRID_W).astype(F32)
    col = (t % GRID_W).astype(F32)
    m = GA_DH // 4
    inv = ROPE_THETA ** (-jnp.arange(m, dtype=F32) / m)
    ar = row[:, None] * inv
    ac = col[:, None] * inv
    cos_t = jnp.concatenate([jnp.cos(ar), jnp.cos(ar), jnp.cos(ac), jnp.cos(ac)], axis=-1)
    sin_t = jnp.concatenate([-jnp.sin(ar), jnp.sin(ar), -jnp.sin(ac), jnp.sin(ac)], axis=-1)
    return cos_t.astype(F32), sin_t.astype(F32)


def _lane_rows(rows):
    tile = jnp.zeros((SUBLANES, LANES), F32)
    for r, (off, vals) in enumerate(rows):
        vals = vals.reshape(-1).astype(F32)
        tile = tile.at[r, off:off + vals.shape[0]].set(vals)
    return tile


def kernel(x, norm_g, w_in, conv_a, dn_a_log, dn_dt_bias, dn_norm_g, na_q_norm, na_k_norm, na_rpb,
           ga_q_norm, ga_k_norm, ml_i_bias, ml_f_bias, ml_norm_g, w_branch, w_out):
    b, s, d = x.shape
    depth = w_in.shape[0]
    cos_t, sin_t = _rope_lane_tables(s)
    x2 = x.reshape(b * s, d)
    for l in range(depth):
        w = w_in[l]
        w_main = jnp.concatenate([w[:, o:o + wd] for _, o, wd in _MAIN_SEGS], axis=1).astype(BF16)
        w_small = jnp.concatenate([w[:, o:o + 8] for o in _SMALL_SRC]
                                  + [jnp.zeros((d, LANES - 32), F32)], axis=1).astype(BF16)
        pm2, ps2 = _inproj(x2, norm_g[l].reshape(1, d), w_main, w_small)
        pm3 = pm2.reshape(b, s, N_MAIN)
        ps3 = ps2.reshape(b, s, LANES)

        conv8 = jnp.zeros((SUBLANES, conv_a.shape[-1]), F32).at[:DN_CONV].set(conv_a[l])
        dn_lp = _lane_rows([(_L_AA, dn_a_log[l]), (_L_AA, dn_dt_bias[l])])
        o_af, o_ab = _dn_scan(_dn_pre(pm3, ps3, conv8, dn_lp))

        ml_lp = _lane_rows([(_L_DI, ml_i_bias[l]), (_L_DF, ml_f_bias[l])])
        h_df, h_db = _mlstm(pm3, ps3, ml_lp)

        hw = BRANCH_W
        bias = _na_bias_table(na_rpb[l], s // GRID_W)
        yb = _natten(pm3, bias, jnp.tile(na_q_norm[l], 2).reshape(1, 2 * NA_DH),
                     jnp.tile(na_k_norm[l], 2).reshape(1, 2 * NA_DH))
        yc = _gqa(pm3, cos_t, sin_t, ga_q_norm[l].reshape(1, GA_DH), ga_k_norm[l].reshape(1, GA_DH))

        x2 = _merge(x2, o_af.reshape(b * s, hw), o_ab.reshape(b * s, hw), h_df.reshape(b * s, hw),
                    h_db.reshape(b * s, hw), yb.reshape(b * s, hw), yc.reshape(b * s, hw), pm2,
                    dn_norm_g[l].reshape(1, LANES), ml_norm_g[l].reshape(1, LANES),
                    w_branch[l].astype(BF16), w_out[l].astype(BF16))
    return x2.reshape(b, s, d)
```

```python
import functools
import math

import jax
import jax.numpy as jnp
from jax import lax
from jax.experimental import pallas as pl
from jax.experimental.pallas import tpu as pltpu

F32 = jnp.float32
BF16 = jnp.bfloat16

D_MODEL = 1024
GRID_W = 64
N_BRANCH = 4
BRANCH_W = 512
EPS = 1e-6
DN_HEADS, DN_DK, DN_DV, DN_CONV, DN_CHUNK = 4, 128, 128, 5, 64
NA_HEADS, NA_DH, NA_ROWS, NA_COLS = 8, 64, 8, 16
GA_HEADS, GA_KV_HEADS, GA_DH = 4, 2, 128
ROPE_THETA = 10000.0
ML_HEADS, ML_DK, ML_DV, ML_CHUNK = 4, 64, 128, 64

LANES = 128
SUBLANES = 8
VMEM_LIMIT_BYTES = 56 * 1024 * 1024

_O_A_QKV, _O_A_A, _O_A_B, _O_A_Z = 0, 1536, 1544, 1552
_O_B_QKV, _O_B_Z = 2064, 3600
_O_C_Q, _O_C_K, _O_C_V, _O_C_Z = 4112, 4624, 4880, 5136
_O_D_Q, _O_D_K, _O_D_V, _O_D_I, _O_D_F, _O_D_O, _O_D_Z = 5648, 5904, 6160, 6672, 6680, 6688, 7200
_O_GATE = 7712
_MAIN_SEGS = (
    ("a_qkv", _O_A_QKV, 1536), ("b_qkv", _O_B_QKV, 1536), ("a_z", _O_A_Z, 512), ("b_z", _O_B_Z, 512),
    ("gate", _O_GATE, 4096), ("c_q", _O_C_Q, 512), ("c_k", _O_C_K, 256), ("c_v", _O_C_V, 256),
    ("c_z", _O_C_Z, 512), ("d_q", _O_D_Q, 256), ("d_k", _O_D_K, 256), ("d_v", _O_D_V, 512),
    ("d_o", _O_D_O, 512), ("d_z", _O_D_Z, 512),
)
_OFF = {}
_o = 0
for _name, _src, _w in _MAIN_SEGS:
    _OFF[_name] = _o
    _o += _w
N_MAIN = _o
_SMALL_SRC = (_O_A_A, _O_A_B, _O_D_I, _O_D_F)
_L_AA, _L_AB, _L_DI, _L_DF = 0, 8, 16, 24

P_DTYPE = BF16
BLK = 128
NEG = -1e30


def _cparams(sem):
    return pltpu.CompilerParams(dimension_semantics=sem, vmem_limit_bytes=VMEM_LIMIT_BYTES)


def _sigmoid(x):
    return 1.0 / (1.0 + jnp.exp(-x))


def _silu(x):
    return x * _sigmoid(x)


def _softplus(x):
    return jnp.maximum(x, 0.0) + jnp.log(1.0 + jnp.exp(-jnp.abs(x)))


def _mm(a, b):
    return jnp.dot(a.astype(BF16), b.astype(BF16), preferred_element_type=F32)


def _mm_nt(a, b):
    return lax.dot_general(a.astype(BF16), b.astype(BF16), (((1,), (1,)), ((), ())),
                           preferred_element_type=F32)


def _mm_tn(a, b):
    return lax.dot_general(a.astype(BF16), b.astype(BF16), (((0,), (0,)), ((), ())),
                           preferred_element_type=F32)


def _mask_sum(mask, x):
    m = jnp.where(mask, 1.0, 0.0).astype(BF16)
    x1 = x.astype(BF16)
    r1 = x - x1.astype(F32)
    x2 = r1.astype(BF16)
    x3 = (r1 - x2.astype(F32)).astype(BF16)
    dot = lambda v: jnp.dot(m, v, preferred_element_type=F32)
    return dot(x1) + (dot(x2) + dot(x3))


def _chunk_masks(reverse):
    i = lax.broadcasted_iota(jnp.int32, (BLK, BLK), 0)
    j = lax.broadcasted_iota(jnp.int32, (BLK, BLK), 1)
    shift = int(math.log2(DN_CHUNK))
    same = (i >> shift) == (j >> shift)
    if reverse:
        incl = same & (j >= i)
        strict = same & (j > i)
    else:
        incl = same & (j <= i)
        strict = same & (j < i)
    return same, incl, strict, (i == j)


def _as_f32(mask):
    return jnp.where(mask, 1.0, 0.0).astype(F32)


def _inproj_kernel(x_ref, g_ref, w_ref, ws_ref, pm_ref, ps_ref, h_scr):
    @pl.when(pl.program_id(1) == 0)
    def _():
        x = x_ref[...]
        ms = jnp.mean(x * x, axis=-1, keepdims=True)
        h = (x * lax.rsqrt(ms + EPS) * g_ref[...]).astype(BF16)
        h_scr[...] = h
        ps_ref[...] = jnp.dot(h, ws_ref[...], preferred_element_type=F32)

    pm_ref[...] = jnp.dot(h_scr[...], w_ref[...], preferred_element_type=F32).astype(pm_ref.dtype)


def _inproj(x2d, g, w_main, w_small, tm=1024, tn=N_MAIN // 4):
    m, d = x2d.shape
    tm = min(tm, m)
    return pl.pallas_call(
        _inproj_kernel,
        grid=(m // tm, N_MAIN // tn),
        in_specs=[
            pl.BlockSpec((tm, d), lambda i, j: (i, 0)),
            pl.BlockSpec((1, d), lambda i, j: (0, 0)),
            pl.BlockSpec((d, tn), lambda i, j: (0, j)),
            pl.BlockSpec((d, LANES), lambda i, j: (0, 0)),
        ],
        out_specs=[
            pl.BlockSpec((tm, tn), lambda i, j: (i, j)),
            pl.BlockSpec((tm, LANES), lambda i, j: (i, 0)),
        ],
        out_shape=[jax.ShapeDtypeStruct((m, N_MAIN), P_DTYPE), jax.ShapeDtypeStruct((m, LANES), F32)],
        scratch_shapes=[pltpu.VMEM((tm, d), BF16)],
        compiler_params=_cparams(("parallel", "arbitrary")),
        name="inproj",
    )(x2d, g, w_main, w_small)


GQA_TQ = 512


def _gqa_kernel(q_ref, k_ref, v_ref, z_ref, cos_ref, sin_ref, qg_ref, kg_ref, y_ref,
                q_scr, k_scr, v_scr, s0_scr, s1_scr, p0_scr, p1_scr, l0_scr, l1_scr):
    def norm_rope(x, g, cos, sin):
        ms = jnp.mean(x * x, axis=-1, keepdims=True)
        xn = x * lax.rsqrt(ms + EPS) * g
        lane = lax.broadcasted_iota(jnp.int32, xn.shape, 1)
        partner = jnp.where((lane & 63) < 32, pltpu.roll(xn, LANES - 32, 1), pltpu.roll(xn, 32, 1))
        return xn * cos + partner * sin

    s_len = k_ref.shape[0]
    group = GA_HEADS // GA_KV_HEADS
    k_scr[...] = norm_rope(k_ref[...].astype(F32), kg_ref[...], cos_ref[...], sin_ref[...]).astype(BF16)
    v_scr[...] = v_ref[...].astype(BF16)
    assert group == 2
    scale = GA_DH ** -0.5
    n_blk = s_len // GQA_TQ
    s_bufs, p_bufs, l_bufs = (s0_scr, s1_scr), (p0_scr, p1_scr), (l0_scr, l1_scr)

    def stacked(head, blk):
        return pl.ds(pl.multiple_of(head * s_len + blk * GQA_TQ, GQA_TQ), GQA_TQ)

    def prep(head, blk):
        rows = pl.ds(pl.multiple_of(blk * GQA_TQ, GQA_TQ), GQA_TQ)
        q = norm_rope(q_ref[rows, head * GA_DH:(head + 1) * GA_DH].astype(F32), qg_ref[...],
                      cos_ref[rows, :], sin_ref[rows, :])
        q_scr[stacked(head, blk), :] = q.astype(BF16)

    def logits(head, blk):
        s_bufs[head][...] = _mm_nt(q_scr[stacked(head, blk), :], k_scr[...]) * scale

    def softmax(slot):
        s = s_bufs[slot][...]
        p = jnp.exp(s - jnp.max(s, axis=-1, keepdims=True))
        l_bufs[slot][...] = jnp.broadcast_to(jnp.sum(p, axis=-1, keepdims=True), (GQA_TQ, GA_DH))
        p_bufs[slot][...] = p.astype(BF16)

    def weighted(head, blk):
        rows = pl.ds(pl.multiple_of(blk * GQA_TQ, GQA_TQ), GQA_TQ)
        cols = slice(head * GA_DH, (head + 1) * GA_DH)
        o = _mm(p_bufs[head][...], v_scr[...]) / l_bufs[head][...]
        y_ref[rows, cols] = (o * _silu(z_ref[rows, cols].astype(F32))).astype(y_ref.dtype)

    prep(0, 0)
    prep(1, 0)
    prep(0, 1)
    logits(0, 0)
    logits(1, 0)
    softmax(0)

    def body(j, carry):
        logits(0, j + 1)
        softmax(1)
        weighted(0, j)
        prep(1, j + 1)
        logits(1, j + 1)
        softmax(0)
        weighted(1, j)
        prep(0, jnp.minimum(j + 2, n_blk - 1))
        return carry

    lax.fori_loop(0, n_blk - 1, body, 0)
    softmax(1)
    weighted(0, n_blk - 1)
    weighted(1, n_blk - 1)


def _gqa(pm3, cos_t, sin_t, qg, kg):
    b, s, _ = pm3.shape
    gw = (GA_HEADS // GA_KV_HEADS) * GA_DH
    return pl.pallas_call(
        _gqa_kernel,
        grid=(b, GA_KV_HEADS),
        in_specs=[
            pl.BlockSpec((None, s, gw), lambda i, j: (i, 0, _OFF["c_q"] // gw + j)),
            pl.BlockSpec((None, s, GA_DH), lambda i, j: (i, 0, _OFF["c_k"] // GA_DH + j)),
            pl.BlockSpec((None, s, GA_DH), lambda i, j: (i, 0, _OFF["c_v"] // GA_DH + j)),
            pl.BlockSpec((None, s, gw), lambda i, j: (i, 0, _OFF["c_z"] // gw + j)),
            pl.BlockSpec((s, GA_DH), lambda i, j: (0, 0)),
            pl.BlockSpec((s, GA_DH), lambda i, j: (0, 0)),
            pl.BlockSpec((1, GA_DH), lambda i, j: (0, 0)),
            pl.BlockSpec((1, GA_DH), lambda i, j: (0, 0)),
        ],
        out_specs=pl.BlockSpec((None, s, gw), lambda i, j: (i, 0, j)),
        out_shape=jax.ShapeDtypeStruct((b, s, BRANCH_W), BF16),
        scratch_shapes=[pltpu.VMEM((gw // GA_DH * s, GA_DH), BF16), pltpu.VMEM((s, GA_DH), BF16),
                        pltpu.VMEM((s, GA_DH), BF16),
                        pltpu.VMEM((GQA_TQ, s), F32), pltpu.VMEM((GQA_TQ, s), F32),
                        pltpu.VMEM((GQA_TQ, s), BF16), pltpu.VMEM((GQA_TQ, s), BF16),
                        pltpu.VMEM((GQA_TQ, GA_DH), F32), pltpu.VMEM((GQA_TQ, GA_DH), F32)],
        compiler_params=_cparams(("parallel", "parallel")),
        name="gqa",
    )(pm3, pm3, pm3, pm3, cos_t, sin_t, qg, kg)


NA_ROW_UNROLL = 16


def _na_kernel(q_ref, k_ref, v_ref, z_ref, bias_ref, qg_ref, kg_ref, y_ref, q_scr, k_scr, v_scr, o_scr):
    s_len = q_ref.shape[0]
    rows = s_len // GRID_W
    kr = min(NA_ROWS, rows)
    hi = lax.broadcasted_iota(jnp.int32, (2 * NA_DH, 2 * NA_DH), 0) >= NA_DH
    hj = lax.broadcasted_iota(jnp.int32, (2 * NA_DH, 2 * NA_DH), 1) >= NA_DH
    same_head = jnp.where(hi == hj, 1.0, 0.0).astype(BF16)

    def rms_pair(x, g):
        x2 = x * x
        x2_hi = x2.astype(BF16)
        x2_lo = (x2 - x2_hi.astype(F32)).astype(BF16)
        ssq = (jnp.dot(x2_hi, same_head, preferred_element_type=F32)
               + jnp.dot(x2_lo, same_head, preferred_element_type=F32))
        return x * lax.rsqrt(ssq * (1.0 / NA_DH) + EPS) * g

    scale = NA_DH ** -0.5
    assert math.log2(scale).is_integer()
    qn = rms_pair(q_ref[...].astype(F32), qg_ref[...]) * scale
    kn = rms_pair(k_ref[...].astype(F32), kg_ref[...])
    vf = v_ref[...].astype(F32)
    for hh in range(2):
        cols = slice(hh * NA_DH, (hh + 1) * NA_DH)
        q_scr[hh] = qn[:, cols].astype(BF16)
        k_scr[hh] = kn[:, cols].astype(BF16)
        v_scr[hh] = vf[:, cols].astype(BF16)
    nkeys = kr * GRID_W

    def body(it, carry):
        units = []
        for u in range(NA_ROW_UNROLL):
            r = it * NA_ROW_UNROLL + u
            r0 = jnp.clip(r - kr // 2, 0, rows - kr)
            var = r0 - r + (NA_ROWS - 1)
            qrows = pl.ds(pl.multiple_of(r * GRID_W, GRID_W), GRID_W)
            krows = pl.ds(pl.multiple_of(r0 * GRID_W, GRID_W), nkeys)
            units += [(hh, var, qrows, krows) for hh in range(2)]
        logits = [_mm_nt(q_scr[hh, qrows, :], k_scr[hh, krows, :]) + bias_ref[hh, var]
                  for hh, var, qrows, krows in units]
        probs = [jnp.exp(s - jnp.max(s, axis=-1, keepdims=True)) for s in logits]
        sums = [jnp.sum(p, axis=-1, keepdims=True) for p in probs]
        outs = [_mm(p, v_scr[hh, krows, :]) for p, (hh, _, _, krows) in zip(probs, units)]
        for o, l, (hh, _, qrows, _) in zip(outs, sums, units):
            o_scr[hh, qrows, :] = o / l
        return carry

    lax.fori_loop(0, rows // NA_ROW_UNROLL, body, 0)
    o = jnp.concatenate([o_scr[0], o_scr[1]], axis=-1)
    y_ref[...] = (o * _silu(z_ref[...].astype(F32))).astype(y_ref.dtype)


def _na_bias_table(rpb, rows):
    kr = min(NA_ROWS, rows)
    c = jnp.arange(GRID_W)
    c0 = jnp.clip(c - NA_COLS // 2, 0, GRID_W - NA_COLS)
    in_win = (c[None, :] >= c0[:, None]) & (c[None, :] < c0[:, None] + NA_COLS)
    col_off = jnp.clip(c[None, :] - c[:, None], -(NA_COLS - 1), NA_COLS - 1) + NA_COLS - 1
    t = rpb[:, :, col_off]
    t = jnp.where(in_win[None, None], t, NEG)
    ro = jnp.arange(NA_ROWS)[:, None] + jnp.arange(kr)[None, :]
    tv = t[:, ro]
    tv = tv.transpose(0, 1, 3, 2, 4)
    return tv.reshape(rpb.shape[0], NA_ROWS, GRID_W, kr * GRID_W).astype(F32)


def _natten(pm3, bias, qg, kg):
    b, s, _ = pm3.shape
    pw = 2 * NA_DH
    npair = NA_HEADS // 2
    hw = NA_HEADS * NA_DH
    bias5 = bias.reshape(npair, 2, *bias.shape[1:])
    return pl.pallas_call(
        _na_kernel,
        grid=(npair, b),
        in_specs=[
            pl.BlockSpec((None, s, pw), lambda p, i: (i, 0, _OFF["b_qkv"] // pw + p)),
            pl.BlockSpec((None, s, pw), lambda p, i: (i, 0, (_OFF["b_qkv"] + hw) // pw + p)),
            pl.BlockSpec((None, s, pw), lambda p, i: (i, 0, (_OFF["b_qkv"] + 2 * hw) // pw + p)),
            pl.BlockSpec((None, s, pw), lambda p, i: (i, 0, _OFF["b_z"] // pw + p)),
            pl.BlockSpec((None,) + bias5.shape[1:], lambda p, i: (p, 0, 0, 0, 0)),
            pl.BlockSpec((1, pw), lambda p, i: (0, 0)),
            pl.BlockSpec((1, pw), lambda p, i: (0, 0)),
        ],
        out_specs=pl.BlockSpec((None, s, pw), lambda p, i: (i, 0, p)),
        out_shape=jax.ShapeDtypeStruct((b, s, BRANCH_W), BF16),
        scratch_shapes=[pltpu.VMEM((2, s, NA_DH), BF16), pltpu.VMEM((2, s, NA_DH), BF16),
                        pltpu.VMEM((2, s, NA_DH), BF16), pltpu.VMEM((2, s, NA_DH), F32)],
        compiler_params=_cparams(("parallel", "parallel")),
        name="natten",
    )(pm3, pm3, pm3, pm3, bias5, qg, kg)


INV_BASE = 8
PRE_BLK = 2 * BLK


def _inverse_level_masks():
    i = lax.broadcasted_iota(jnp.int32, (BLK, BLK), 0)
    j = lax.broadcasted_iota(jnp.int32, (BLK, BLK), 1)
    same = lambda size: (i >> int(math.log2(size))) == (j >> int(math.log2(size)))
    base = same(INV_BASE)
    joins = []
    size = INV_BASE
    while size < DN_CHUNK:
        joins.append(same(2 * size) & jnp.logical_not(same(size)))
        size *= 2
    return base, joins


def _tri_inverses(l_mats, eyes, level_masks):
    base, joins = level_masks
    ps = [jnp.where(base, -l, 0.0) for l in l_mats]
    ts = [eye + p for eye, p in zip(eyes, ps)]
    for _ in range(int(math.log2(INV_BASE)) - 1):
        ps = [_mm(p, p) for p in ps]
        ts = [t + _mm(t, p) for t, p in zip(ts, ps)]
    for join in joins:
        mids = [_mm(jnp.where(join, l, 0.0), t) for l, t in zip(l_mats, ts)]
        ts = [t - _mm(t, mid) for t, mid in zip(ts, mids)]
    return ts


DN_HW = DN_HEADS * LANES
PK_W, PK_QD, PK_KD, PK_QK = (i * DN_HW for i in range(4))


def _dn_pre_kernel(qkv_ref, sm_ref, cw_ref, lp_ref, uf_ref, ub_ref, pkf_ref, pkb_ref, gtf_ref, gtb_ref):
    n = pl.program_id(1)
    s_len = qkv_ref.shape[0]
    assert qkv_ref.dtype == BF16
    halo = 2 * SUBLANES
    n_sub = sm_ref.shape[0] // BLK
    chunks_per_blk = BLK // DN_CHUNK
    out_row = lax.broadcasted_iota(jnp.int32, (BLK, BLK + 2 * halo), 0)
    in_row = lax.broadcasted_iota(jnp.int32, (BLK, BLK + 2 * halo), 1)
    shifts = {j: jnp.where(in_row == out_row + (halo + j - DN_CONV // 2), 1.0, 0.0).astype(BF16)
              for j in range(DN_CONV) if j != DN_CONV // 2}
    dir_masks = [_chunk_masks(reverse=(d == 1)) for d in range(2)]
    level_masks = _inverse_level_masks()
    outs = ((uf_ref, pkf_ref, gtf_ref), (ub_ref, pkb_ref, gtb_ref))

    def l2n(x):
        return x * lax.rsqrt(jnp.sum(x * x, axis=-1, keepdims=True) + EPS)

    blocks = []
    for sub in range(n_sub):
        t0 = pl.multiple_of((n * n_sub + sub) * BLK, BLK)
        pstart = pl.multiple_of(jnp.maximum(t0 - halo, 0), halo)
        nstart = pl.multiple_of(jnp.minimum(t0 + BLK, s_len - halo), halo)
        prev = qkv_ref[pl.ds(pstart, halo), :]
        prev = jnp.where(t0 > 0, prev, jnp.zeros_like(prev))
        cur = qkv_ref[pl.ds(t0, BLK), :]
        nxt = qkv_ref[pl.ds(nstart, halo), :]
        nxt = jnp.where(t0 + BLK < s_len, nxt, jnp.zeros_like(nxt))
        xw = jnp.concatenate([prev, cur, nxt], axis=0)
        conv = cur.astype(F32) * cw_ref[DN_CONV // 2:DN_CONV // 2 + 1, :]
        for j, shift in shifts.items():
            conv = conv + jnp.dot(shift, xw, preferred_element_type=F32) * cw_ref[j:j + 1, :]
        conv = _silu(conv)

        rows = slice(sub * BLK, (sub + 1) * BLK)
        sm = sm_ref[rows, :]
        g_all = -jnp.exp(lp_ref[0:1, :]) * _softplus(sm + lp_ref[1:2, :])
        beta_all = _sigmoid(sm)
        per_dir = []
        for d in range(2):
            same, incl, strict, eye_b = dir_masks[d]
            gc = _mask_sum(incl, g_all)
            tot = _mask_sum(same, g_all)
            per_dir.append((incl, strict, _as_f32(eye_b), gc, gc.T, tot))
        blocks.append((sub, rows, conv, beta_all, per_dir))

    heads = []
    for sub, rows, conv, beta_all, per_dir in blocks:
        for h in range(DN_HEADS):
            q = l2n(conv[:, h * DN_DK:(h + 1) * DN_DK]) * (DN_DK ** -0.5)
            k = l2n(conv[:, (DN_HEADS + h) * DN_DK:(DN_HEADS + h + 1) * DN_DK])
            v = conv[:, 2 * DN_HEADS * DN_DK + h * DN_DV:2 * DN_HEADS * DN_DK + (h + 1) * DN_DV]
            heads.append((sub, rows, beta_all, per_dir, h, q, k, v))
    kks = [_mm_nt(hd[6], hd[6]) for hd in heads]
    qks = [_mm_nt(hd[5], hd[6]) for hd in heads]

    l_mats, rhss, eyes, slots = [], [], [], []
    for (sub, rows, beta_all, per_dir, h, q, k, v), kk, qk in zip(heads, kks, qks):
        hc = slice(h * LANES, (h + 1) * LANES)
        for d in range(2):
            incl, strict, eye_f, gc, gct, tot = per_dir[d]
            u_ref, pk_ref, gt_ref = outs[d]
            pk = lambda off: slice(off + h * LANES, off + (h + 1) * LANES)
            c = _L_AA + d * DN_HEADS + h
            gcol = gc[:, c:c + 1]
            grow = gct[c:c + 1, :]
            tcol = tot[:, c:c + 1]
            beta = beta_all[:, _L_AB + d * DN_HEADS + h:_L_AB + d * DN_HEADS + h + 1]
            decay = jnp.where(incl, jnp.exp(jnp.where(incl, gcol - grow, 0.0)), 0.0)
            l_mats.append(jnp.where(strict, beta * kk * decay, 0.0))
            egc = jnp.exp(gcol)
            rhss.append(jnp.concatenate([v * beta, k * (beta * egc)], axis=-1).astype(BF16))
            eyes.append(eye_f)
            slots.append((rows, h, d))
            pk_ref[rows, pk(PK_QD)] = (q * egc).astype(pk_ref.dtype)
            pk_ref[rows, pk(PK_KD)] = (k * jnp.exp(tcol - gcol)).astype(pk_ref.dtype)
            pk_ref[rows, pk(PK_QK)] = (qk * decay).astype(pk_ref.dtype)
            gtot = jnp.exp(tcol)
            for ci in range(chunks_per_blk):
                g0 = (sub * chunks_per_blk + ci) * SUBLANES
                gt_ref[g0:g0 + SUBLANES, hc] = jnp.broadcast_to(
                    gtot[ci * DN_CHUNK:ci * DN_CHUNK + SUBLANES, :], (SUBLANES, LANES))

    t_invs = _tri_inverses(l_mats, eyes, level_masks)
    sols = [_mm(t, rhs) for t, rhs in zip(t_invs, rhss)]
    for (rows, h, d), sol in zip(slots, sols):
        u_ref, pk_ref = outs[d][0], outs[d][1]
        u_ref[rows, h * LANES:(h + 1) * LANES] = sol[:, :DN_DV]
        pk_ref[rows, PK_W + h * LANES:PK_W + (h + 1) * LANES] = sol[:, DN_DV:].astype(pk_ref.dtype)


def _dn_pre(pm3, ps3, conv_w8, lane_params):
    b, s, _ = pm3.shape
    nblk = s // PRE_BLK
    gt_rows = PRE_BLK // DN_CHUNK * SUBLANES
    wq = DN_HEADS * (2 * DN_DK + DN_DV)
    hw = DN_HEADS * LANES
    tok = lambda i, j: (i, j, 0)
    big = lambda dt: jax.ShapeDtypeStruct((b, s, hw), dt)
    gts = jax.ShapeDtypeStruct((b, nblk * gt_rows, hw), F32)
    bs_tok = pl.BlockSpec((None, PRE_BLK, hw), tok)
    bs_gt = pl.BlockSpec((None, gt_rows, hw), tok)
    bs_pk = pl.BlockSpec((None, PRE_BLK, 4 * hw), tok)
    packed = jax.ShapeDtypeStruct((b, s, 4 * hw), BF16)
    return pl.pallas_call(
        _dn_pre_kernel,
        grid=(b, nblk),
        in_specs=[
            pl.BlockSpec((None, s, wq), lambda i, j: (i, 0, _OFF["a_qkv"] // wq)),
            pl.BlockSpec((None, PRE_BLK, LANES), tok),
            pl.BlockSpec((SUBLANES, wq), lambda i, j: (0, 0)),
            pl.BlockSpec((SUBLANES, LANES), lambda i, j: (0, 0)),
        ],
        out_specs=[bs_tok, bs_tok, bs_pk, bs_pk, bs_gt, bs_gt],
        out_shape=[big(F32), big(F32), packed, packed, gts, gts],
        compiler_params=_cparams(("parallel", "arbitrary")),
        name="dn_pre",
    )(pm3, ps3, conv_w8, lane_params)


SCAN_BLK = 2 * BLK


def _dn_scan_kernel(uf_ref, pkf_ref, gtf_ref, ub_ref, pkb_ref, gtb_ref, of_ref, ob_ref, st_scr):
    @pl.when(pl.program_id(1) == 0)
    def _():
        st_scr[...] = jnp.zeros_like(st_scr)

    per_blk = BLK // DN_CHUNK
    nchunk = SCAN_BLK // DN_CHUNK
    zeros_c = jnp.zeros((DN_CHUNK, DN_DV), F32)
    streams = ((uf_ref, pkf_ref, gtf_ref, of_ref, range(nchunk)),
               (ub_ref, pkb_ref, gtb_ref, ob_ref, range(nchunk - 1, -1, -1)))
    chains = [(d, h) + streams[d] for d in range(2) for h in range(DN_HEADS)]
    states = [st_scr[d * DN_HEADS + h] for d, h, *_ in chains]
    for step in range(nchunk):
        rs, v_pads = [], []
        for (d, h, u_ref, pk_ref, gt_ref, o_ref, order), state in zip(chains, states):
            rows = slice(order[step] * DN_CHUNK, (order[step] + 1) * DN_CHUNK)
            w = pk_ref[rows, PK_W + h * LANES:PK_W + (h + 1) * LANES]
            qd = pk_ref[rows, PK_QD + h * LANES:PK_QD + (h + 1) * LANES]
            rs.append(_mm(jnp.concatenate([w, qd], axis=0), state))
        for (d, h, u_ref, pk_ref, gt_ref, o_ref, order), r in zip(chains, rs):
            ci = order[step]
            rows = slice(ci * DN_CHUNK, (ci + 1) * DN_CHUNK)
            parts = [zeros_c] * per_blk
            parts[ci % per_blk] = u_ref[rows, h * LANES:(h + 1) * LANES] - r[:DN_CHUNK]
            v_pads.append(jnp.concatenate(parts, axis=0))
        new_states = []
        for (d, h, u_ref, pk_ref, gt_ref, o_ref, order), r, v_pad, state in zip(chains, rs, v_pads, states):
            hc = slice(h * LANES, (h + 1) * LANES)
            ci = order[step]
            rows = slice(ci * DN_CHUNK, (ci + 1) * DN_CHUNK)
            blk_rows = slice((ci // per_blk) * BLK, (ci // per_blk + 1) * BLK)
            o_ref[rows, hc] = r[DN_CHUNK:] + _mm(pk_ref[rows, PK_QK + h * LANES:PK_QK + (h + 1) * LANES], v_pad)
            gt = gt_ref[ci * SUBLANES:ci * SUBLANES + 1, hc]
            kd = pk_ref[blk_rows, PK_KD + h * LANES:PK_KD + (h + 1) * LANES]
            new_states.append(state * gt + _mm_tn(kd, v_pad))
        states = new_states
    for (d, h, *_), state in zip(chains, states):
        st_scr[d * DN_HEADS + h] = state


def _dn_scan(pre):
    uf, ub, pkf, pkb, gtf, gtb = pre
    b, s, hw = uf.shape
    nblk = s // SCAN_BLK
    gt_rows = SCAN_BLK // DN_CHUNK * SUBLANES
    fwd = lambda i, j: (i, j, 0)
    bwd = lambda i, j: (i, nblk - 1 - j, 0)
    def specs(imap):
        return [pl.BlockSpec((None, SCAN_BLK, hw), imap), pl.BlockSpec((None, SCAN_BLK, 4 * hw), imap),
                pl.BlockSpec((None, gt_rows, hw), imap)]
    return pl.pallas_call(
        _dn_scan_kernel,
        grid=(b, nblk),
        in_specs=specs(fwd) + specs(bwd),
        out_specs=[pl.BlockSpec((None, SCAN_BLK, hw), fwd), pl.BlockSpec((None, SCAN_BLK, hw), bwd)],
        out_shape=[jax.ShapeDtypeStruct((b, s, hw), F32)] * 2,
        scratch_shapes=[pltpu.VMEM((2 * DN_HEADS, DN_DK, DN_DV), F32)],
        compiler_params=_cparams(("parallel", "arbitrary")),
        name="dn_scan",
    )(uf, pkf, gtf, ub, pkb, gtb)


ML_AUG = 2 * LANES
ML_GROUP = 4
ML_BLK = 2 * BLK


def _ml_kernel(qf_ref, kf_ref, vf_ref, smf_ref, qb_ref, kb_ref, vb_ref, smb_ref, lp_ref,
               hf_ref, hb_ref, c_scr, m_scr):
    @pl.when(pl.program_id(1) == 0)
    def _():
        c_scr[...] = jnp.zeros_like(c_scr)
        m_scr[...] = jnp.zeros_like(m_scr)

    nchunk = BLK // ML_CHUNK
    ones_col = jnp.ones((BLK, LANES), BF16)
    zeros_aug = jnp.zeros((ML_CHUNK, ML_AUG), BF16)
    streams = ((qf_ref, kf_ref, vf_ref, smf_ref, hf_ref, range(nchunk)),
               (qb_ref, kb_ref, vb_ref, smb_ref, hb_ref, range(nchunk - 1, -1, -1)))
    lanes = lambda col: jnp.broadcast_to(col, (col.shape[0], LANES))
    n_sub = smf_ref.shape[0] // BLK
    units = [(d, p if d == 0 else n_sub - 1 - p) for p in range(n_sub) for d in range(2)]
    for d, sub in units:
        _ml_block(d, sub, streams[d], lp_ref, c_scr, m_scr, lanes, ones_col, zeros_aug, nchunk)


def _ml_block(d, sub, stream, lp_ref, c_scr, m_scr, lanes, ones_col, zeros_aug, nchunk):
    q_ref, k_ref, v_ref, sm_ref, h_ref, order = stream
    blk_rows = slice(sub * BLK, (sub + 1) * BLK)
    same, incl, _, _ = _chunk_masks(reverse=(d == 1))
    sm = sm_ref[blk_rows, :]
    ig_all = sm + lp_ref[0:1, :]
    x = sm + lp_ref[1:2, :]
    lf_all = jnp.minimum(x, 0.0) - jnp.log(1.0 + jnp.exp(-jnp.abs(x)))
    lf_all = pltpu.roll(lf_all, LANES - (_L_DF - _L_DI), 1)
    bc_all = _mask_sum(incl, lf_all)
    tot_all = _mask_sum(same, lf_all)
    a_all = ig_all - bc_all
    mwa_all = jnp.concatenate(
        [jnp.broadcast_to(jnp.max(a_all[ci * ML_CHUNK:(ci + 1) * ML_CHUNK], axis=0, keepdims=True),
                          (ML_CHUNK, LANES)) for ci in range(nchunk)], axis=0)
    a_t, w_all, mw_all = a_all.T, jnp.exp(a_all - mwa_all), tot_all + mwa_all

    for chains in [[(d, h) for h in range(g, g + ML_GROUP)] for g in range(0, ML_HEADS, ML_GROUP)]:
        ins = []
        for _, h in chains:
            q = q_ref[blk_rows, h * ML_DK:(h + 1) * ML_DK].astype(BF16)
            k = k_ref[blk_rows, h * ML_DK:(h + 1) * ML_DK].astype(F32) * (ML_DK ** -0.5)
            v_aug = jnp.concatenate([v_ref[blk_rows, h * ML_DV:(h + 1) * ML_DV].astype(BF16), ones_col], axis=-1)
            ins.append((q, k, v_aug))
        qks = [_mm_nt(q, k) for q, k, _ in ins]

        mids = []
        for (d, h), (q, k, v_aug), qk in zip(chains, ins, qks):
            c = _L_DI + d * ML_HEADS + h
            b_l = lanes(bc_all[:, c:c + 1])
            dlog = jnp.where(incl, b_l + a_t[c:c + 1, :], NEG)
            m_intra = lanes(jnp.max(dlog, axis=-1, keepdims=True))
            s_intra = qk * jnp.exp(dlog - m_intra)
            wk = (k * lanes(w_all[:, c:c + 1])[:, :ML_DK]).astype(BF16)
            mids.append((b_l, m_intra, s_intra.astype(BF16), wk))
        p_intras = [_mm(s_b, v_aug) for (_, _, s_b, _), (_, _, v_aug) in zip(mids, ins)]
        kvs = []
        for (_, _, _, wk), (_, _, v_aug) in zip(mids, ins):
            per_chunk = []
            for ci in range(nchunk):
                parts = [zeros_aug] * nchunk
                parts[ci] = v_aug[ci * ML_CHUNK:(ci + 1) * ML_CHUNK]
                per_chunk.append(_mm_tn(wk, jnp.concatenate(parts, axis=0)))
            kvs.append(per_chunk)

        c_sts = [c_scr[d * ML_HEADS + h] for d, h in chains]
        m_sts = [m_scr[d * ML_HEADS + h][0:1, :] for d, h in chains]
        for step in range(nchunk):
            qcs = []
            ci = order[step]
            for (q, _, _), c_st in zip(ins, c_sts):
                qcs.append(_mm(q[ci * ML_CHUNK:(ci + 1) * ML_CHUNK], c_st))
            for idx, (_, h) in enumerate(chains):
                b_l, m_intra, _, _ = mids[idx]
                c = _L_DI + d * ML_HEADS + h
                rows = slice(ci * ML_CHUNK, (ci + 1) * ML_CHUNK)
                out_rows = slice(sub * BLK + ci * ML_CHUNK, sub * BLK + (ci + 1) * ML_CHUNK)
                r8 = slice(ci * ML_CHUNK, ci * ML_CHUNK + SUBLANES)
                m_st, c_st, qc = m_sts[idx], c_sts[idx], qcs[idx]
                m_inter = b_l[rows] + m_st
                m_i = jnp.maximum(m_intra[rows], m_inter)
                f_i = jnp.exp(m_intra[rows] - m_i)
                inter = jnp.exp(m_inter - m_i)
                both = (jnp.concatenate([inter, inter], axis=-1) * qc
                        + jnp.concatenate([f_i, f_i], axis=-1) * p_intras[idx][rows])
                numer, denom = both[:, :ML_DV], both[:, ML_DV:]
                h_ref[out_rows, h * ML_DV:(h + 1) * ML_DV] = numer / jnp.maximum(jnp.abs(denom), jnp.exp(-m_i))
                tot_c = lanes(tot_all[r8, c:c + 1])[0:1]
                mw_c = lanes(mw_all[r8, c:c + 1])[0:1]
                m_new = jnp.maximum(tot_c + m_st, mw_c)
                dec = jnp.exp(tot_c + m_st - m_new)
                gain = jnp.exp(mw_c - m_new)
                c_sts[idx] = (jnp.concatenate([dec, dec], axis=-1) * c_st
                              + jnp.concatenate([gain, gain], axis=-1) * kvs[idx][ci])
                m_sts[idx] = m_new
        for idx, (d, h) in enumerate(chains):
            c_scr[d * ML_HEADS + h] = c_sts[idx]
            m_scr[d * ML_HEADS + h] = jnp.broadcast_to(m_sts[idx], (SUBLANES, LANES))


def _mlstm(pm3, ps3, lane_params):
    b, s, _ = pm3.shape
    nblk = s // ML_BLK
    qw = ML_HEADS * ML_DK
    vw = ML_HEADS * ML_DV
    def specs(tmap):
        blk = lambda j: tmap(j)
        return [
            pl.BlockSpec((None, ML_BLK, qw), lambda i, j: (i, blk(j), _OFF["d_q"] // qw)),
            pl.BlockSpec((None, ML_BLK, qw), lambda i, j: (i, blk(j), _OFF["d_k"] // qw)),
            pl.BlockSpec((None, ML_BLK, vw), lambda i, j: (i, blk(j), _OFF["d_v"] // vw)),
            pl.BlockSpec((None, ML_BLK, LANES), lambda i, j: (i, blk(j), 0)),
        ]
    fwd = lambda j: j
    bwd = lambda j: nblk - 1 - j
    return pl.pallas_call(
        _ml_kernel,
        grid=(b, nblk),
        in_specs=specs(fwd) + specs(bwd) + [pl.BlockSpec((SUBLANES, LANES), lambda i, j: (0, 0))],
        out_specs=[pl.BlockSpec((None, ML_BLK, vw), lambda i, j: (i, j, 0)),
                   pl.BlockSpec((None, ML_BLK, vw), lambda i, j: (i, nblk - 1 - j, 0))],
        out_shape=[jax.ShapeDtypeStruct((b, s, vw), F32)] * 2,
        scratch_shapes=[pltpu.VMEM((2 * ML_HEADS, ML_DK, ML_AUG), F32),
                        pltpu.VMEM((2 * ML_HEADS, SUBLANES, LANES), F32)],
        compiler_params=_cparams(("parallel", "arbitrary")),
        name="mlstm",
    )(pm3, pm3, pm3, ps3, pm3, pm3, pm3, ps3, lane_params)


def _merge_kernel(x_ref, af_ref, ab_ref, df_ref, db_ref, yb_ref, yc_ref, az_ref, dz_ref, do_ref, gl_ref,
                  ag_ref, dg_ref, wb_ref, wo_ref, o_ref):
    d = x_ref.shape[-1]

    def head_rms(x, g):
        outs = []
        for h in range(x.shape[-1] // LANES):
            xh = x[:, h * LANES:(h + 1) * LANES]
            ms = jnp.mean(xh * xh, axis=-1, keepdims=True)
            outs.append(xh * lax.rsqrt(ms + EPS) * g)
        return jnp.concatenate(outs, axis=-1)

    ya = head_rms(af_ref[...] + ab_ref[...], ag_ref[...]) * _silu(az_ref[...].astype(F32))
    yd = _sigmoid(do_ref[...].astype(F32)) * head_rms(df_ref[...] + db_ref[...], dg_ref[...])
    yd = yd * _silu(dz_ref[...].astype(F32))
    merged = None
    for i, y in enumerate((ya.astype(BF16), yb_ref[...], yc_ref[...], yd.astype(BF16))):
        proj = jnp.dot(y, wb_ref[i], preferred_element_type=F32)
        term = _sigmoid(gl_ref[:, i * d:(i + 1) * d].astype(F32)) * proj
        merged = term if merged is None else merged + term
    o_ref[...] = x_ref[...] + jnp.dot(merged.astype(BF16), wo_ref[...], preferred_element_type=F32)


def _merge(x2d, af, ab, df, db, yb, yc, pm2, ag, dg, wb, wo, tm=512):
    m, d = x2d.shape
    gw = N_BRANCH * d
    w = BRANCH_W
    tok = pl.BlockSpec((tm, w), lambda i: (i, 0))
    col = lambda name: pl.BlockSpec((tm, w), lambda i: (i, _OFF[name] // w))
    vec = pl.BlockSpec((1, LANES), lambda i: (0, 0))
    return pl.pallas_call(
        _merge_kernel,
        grid=(m // tm,),
        in_specs=[
            pl.BlockSpec((tm, d), lambda i: (i, 0)),
            tok, tok, tok, tok, tok, tok,
            col("a_z"), col("d_z"), col("d_o"),
            pl.BlockSpec((tm, gw), lambda i: (i, _OFF["gate"] // gw)),
            vec, vec,
            pl.BlockSpec((N_BRANCH, w, d), lambda i: (0, 0, 0)),
            pl.BlockSpec((d, d), lambda i: (0, 0)),
        ],
        out_specs=pl.BlockSpec((tm, d), lambda i: (i, 0)),
        out_shape=jax.ShapeDtypeStruct((m, d), F32),
        compiler_params=_cparams(("parallel",)),
        name="merge",
    )(x2d, af, ab, df, db, yb, yc, pm2, pm2, pm2, pm2, ag, dg, wb, wo)


def _rope_lane_tables(s):
    t = jnp.arange(s)
    row = (t // GRID_W).astype(F32)
    col = (t % GRID_W).astype(F32)
    m = GA_DH // 4
    inv = ROPE_THETA ** (-jnp.arange(m, dtype=F32) / m)
    ar = row[:, None] * inv
    ac = col[:, None] * inv
    cos_t = jnp.concatenate([jnp.cos(ar), jnp.cos(ar), jnp.cos(ac), jnp.cos(ac)], axis=-1)
    sin_t = jnp.concatenate([-jnp.sin(ar), jnp.sin(ar), -jnp.sin(ac), jnp.sin(ac)], axis=-1)
    return cos_t.astype(F32), sin_t.astype(F32)


def _lane_rows(rows):
    tile = jnp.zeros((SUBLANES, LANES), F32)
    for r, (off, vals) in enumerate(rows):
        vals = vals.reshape(-1).astype(F32)
        tile = tile.at[r, off:off + vals.shape[0]].set(vals)
    return tile


def kernel(x, norm_g, w_in, conv_a, dn_a_log, dn_dt_bias, dn_norm_g, na_q_norm, na_k_norm, na_rpb,
           ga_q_norm, ga_k_norm, ml_i_bias, ml_f_bias, ml_norm_g, w_branch, w_out):
    b, s, d = x.shape
    depth = w_in.shape[0]
    cos_t, sin_t = _rope_lane_tables(s)
    x2 = x.reshape(b * s, d)
    for l in range(depth):
        w = w_in[l]
        w_main = jnp.concatenate([w[:, o:o + wd] for _, o, wd in _MAIN_SEGS], axis=1).astype(BF16)
        w_small = jnp.concatenate([w[:, o:o + 8] for o in _SMALL_SRC]
                                  + [jnp.zeros((d, LANES - 32), F32)], axis=1).astype(BF16)
        pm2, ps2 = _inproj(x2, norm_g[l].reshape(1, d), w_main, w_small)
        pm3 = pm2.reshape(b, s, N_MAIN)
        ps3 = ps2.reshape(b, s, LANES)

        conv8 = jnp.zeros((SUBLANES, conv_a.shape[-1]), F32).at[:DN_CONV].set(conv_a[l])
        dn_lp = _lane_rows([(_L_AA, dn_a_log[l]), (_L_AA, dn_dt_bias[l])])
        o_af, o_ab = _dn_scan(_dn_pre(pm3, ps3, conv8, dn_lp))

        ml_lp = _lane_rows([(_L_DI, ml_i_bias[l]), (_L_DF, ml_f_bias[l])])
        h_df, h_db = _mlstm(pm3, ps3, ml_lp)

        hw = BRANCH_W
        bias = _na_bias_table(na_rpb[l], s // GRID_W)
        yb = _natten(pm3, bias, jnp.tile(na_q_norm[l], 2).reshape(1, 2 * NA_DH),
                     jnp.tile(na_k_norm[l], 2).reshape(1, 2 * NA_DH))
        yc = _gqa(pm3, cos_t, sin_t, ga_q_norm[l].reshape(1, GA_DH), ga_k_norm[l].reshape(1, GA_DH))

        x2 = _merge(x2, o_af.reshape(b * s, hw), o_ab.reshape(b * s, hw), h_df.reshape(b * s, hw),
                    h_db.reshape(b * s, hw), yb.reshape(b * s, hw), yc.reshape(b * s, hw), pm2,
                    dn_norm_g[l].reshape(1, LANES), ml_norm_g[l].reshape(1, LANES),
                    w_branch[l].astype(BF16), w_out[l].astype(BF16))
    return x2.reshape(b, s, d)
```

```python
import functools
import math

import jax
import jax.numpy as jnp
from jax import lax
from jax.experimental import pallas as pl
from jax.experimental.pallas import tpu as pltpu

F32 = jnp.float32
BF16 = jnp.bfloat16

D_MODEL = 1024
GRID_W = 64
N_BRANCH = 4
BRANCH_W = 512
EPS = 1e-6
DN_HEADS, DN_DK, DN_DV, DN_CONV, DN_CHUNK = 4, 128, 128, 5, 64
NA_HEADS, NA_DH, NA_ROWS, NA_COLS = 8, 64, 8, 16
GA_HEADS, GA_KV_HEADS, GA_DH = 4, 2, 128
ROPE_THETA = 10000.0
ML_HEADS, ML_DK, ML_DV, ML_CHUNK = 4, 64, 128, 64

LANES = 128
SUBLANES = 8
VMEM_LIMIT_BYTES = 56 * 1024 * 1024

_O_A_QKV, _O_A_A, _O_A_B, _O_A_Z = 0, 1536, 1544, 1552
_O_B_QKV, _O_B_Z = 2064, 3600
_O_C_Q, _O_C_K, _O_C_V, _O_C_Z = 4112, 4624, 4880, 5136
_O_D_Q, _O_D_K, _O_D_V, _O_D_I, _O_D_F, _O_D_O, _O_D_Z = 5648, 5904, 6160, 6672, 6680, 6688, 7200
_O_GATE = 7712
_MAIN_SEGS = (
    ("a_qkv", _O_A_QKV, 1536), ("b_qkv", _O_B_QKV, 1536), ("a_z", _O_A_Z, 512), ("b_z", _O_B_Z, 512),
    ("gate", _O_GATE, 4096), ("c_q", _O_C_Q, 512), ("c_k", _O_C_K, 256), ("c_v", _O_C_V, 256),
    ("c_z", _O_C_Z, 512), ("d_q", _O_D_Q, 256), ("d_k", _O_D_K, 256), ("d_v", _O_D_V, 512),
    ("d_o", _O_D_O, 512), ("d_z", _O_D_Z, 512),
)
_OFF = {}
_o = 0
for _name, _src, _w in _MAIN_SEGS:
    _OFF[_name] = _o
    _o += _w
N_MAIN = _o
_SMALL_SRC = (_O_A_A, _O_A_B, _O_D_I, _O_D_F)
_L_AA, _L_AB, _L_DI, _L_DF = 0, 8, 16, 24

P_DTYPE = BF16
BLK = 128
NEG = -1e30


def _cparams(sem):
    return pltpu.CompilerParams(dimension_semantics=sem, vmem_limit_bytes=VMEM_LIMIT_BYTES)


def _sigmoid(x):
    return 1.0 / (1.0 + jnp.exp(-x))


def _silu(x):
    return x * _sigmoid(x)


def _softplus(x):
    return jnp.maximum(x, 0.0) + jnp.log(1.0 + jnp.exp(-jnp.abs(x)))


def _mm(a, b):
    return jnp.dot(a.astype(BF16), b.astype(BF16), preferred_element_type=F32)


def _mm_nt(a, b):
    return lax.dot_general(a.astype(BF16), b.astype(BF16), (((1,), (1,)), ((), ())),
                           preferred_element_type=F32)


def _mm_tn(a, b):
    return lax.dot_general(a.astype(BF16), b.astype(BF16), (((0,), (0,)), ((), ())),
                           preferred_element_type=F32)


def _mask_sum(mask, x):
    m = jnp.where(mask, 1.0, 0.0).astype(BF16)
    x1 = x.astype(BF16)
    r1 = x - x1.astype(F32)
    x2 = r1.astype(BF16)
    x3 = (r1 - x2.astype(F32)).astype(BF16)
    dot = lambda v: jnp.dot(m, v, preferred_element_type=F32)
    return dot(x1) + (dot(x2) + dot(x3))


def _chunk_masks(reverse):
    i = lax.broadcasted_iota(jnp.int32, (BLK, BLK), 0)
    j = lax.broadcasted_iota(jnp.int32, (BLK, BLK), 1)
    shift = int(math.log2(DN_CHUNK))
    same = (i >> shift) == (j >> shift)
    if reverse:
        incl = same & (j >= i)
        strict = same & (j > i)
    else:
        incl = same & (j <= i)
        strict = same & (j < i)
    return same, incl, strict, (i == j)


def _as_f32(mask):
    return jnp.where(mask, 1.0, 0.0).astype(F32)


def _inproj_kernel(x_ref, g_ref, w_ref, ws_ref, pm_ref, ps_ref, h_scr):
    @pl.when(pl.program_id(1) == 0)
    def _():
        x = x_ref[...]
        ms = jnp.mean(x * x, axis=-1, keepdims=True)
        h = (x * lax.rsqrt(ms + EPS) * g_ref[...]).astype(BF16)
        h_scr[...] = h
        ps_ref[...] = jnp.dot(h, ws_ref[...], preferred_element_type=F32)

    pm_ref[...] = jnp.dot(h_scr[...], w_ref[...], preferred_element_type=F32).astype(pm_ref.dtype)


def _inproj(x2d, g, w_main, w_small, tm=1024, tn=N_MAIN // 4):
    m, d = x2d.shape
    tm = min(tm, m)
    return pl.pallas_call(
        _inproj_kernel,
        grid=(m // tm, N_MAIN // tn),
        in_specs=[
            pl.BlockSpec((tm, d), lambda i, j: (i, 0)),
            pl.BlockSpec((1, d), lambda i, j: (0, 0)),
            pl.BlockSpec((d, tn), lambda i, j: (0, j)),
            pl.BlockSpec((d, LANES), lambda i, j: (0, 0)),
        ],
        out_specs=[
            pl.BlockSpec((tm, tn), lambda i, j: (i, j)),
            pl.BlockSpec((tm, LANES), lambda i, j: (i, 0)),
        ],
        out_shape=[jax.ShapeDtypeStruct((m, N_MAIN), P_DTYPE), jax.ShapeDtypeStruct((m, LANES), F32)],
        scratch_shapes=[pltpu.VMEM((tm, d), BF16)],
        compiler_params=_cparams(("parallel", "arbitrary")),
        name="inproj",
    )(x2d, g, w_main, w_small)


GQA_TQ = 512


def _gqa_kernel(q_ref, k_ref, v_ref, z_ref, cos_ref, sin_ref, qg_ref, kg_ref, y_ref,
                q_scr, k_scr, v_scr, s0_scr, s1_scr, p0_scr, p1_scr, l0_scr, l1_scr):
    def norm_rope(x, g, cos, sin):
        ms = jnp.mean(x * x, axis=-1, keepdims=True)
        xn = x * lax.rsqrt(ms + EPS) * g
        lane = lax.broadcasted_iota(jnp.int32, xn.shape, 1)
        partner = jnp.where((lane & 63) < 32, pltpu.roll(xn, LANES - 32, 1), pltpu.roll(xn, 32, 1))
        return xn * cos + partner * sin

    s_len = k_ref.shape[0]
    group = GA_HEADS // GA_KV_HEADS
    k_scr[...] = norm_rope(k_ref[...].astype(F32), kg_ref[...], cos_ref[...], sin_ref[...]).astype(BF16)
    v_scr[...] = v_ref[...].astype(BF16)
    assert group == 2
    scale = GA_DH ** -0.5
    n_blk = s_len // GQA_TQ
    s_bufs, p_bufs, l_bufs = (s0_scr, s1_scr), (p0_scr, p1_scr), (l0_scr, l1_scr)

    def stacked(head, blk):
        return pl.ds(pl.multiple_of(head * s_len + blk * GQA_TQ, GQA_TQ), GQA_TQ)

    def prep(head, blk):
        rows = pl.ds(pl.multiple_of(blk * GQA_TQ, GQA_TQ), GQA_TQ)
        q = norm_rope(q_ref[rows, head * GA_DH:(head + 1) * GA_DH].astype(F32), qg_ref[...],
                      cos_ref[rows, :], sin_ref[rows, :])
        q_scr[stacked(head, blk), :] = q.astype(BF16)

    def logits(head, blk):
        s_bufs[head][...] = _mm_nt(q_scr[stacked(head, blk), :], k_scr[...]) * scale

    def softmax(slot):
        s = s_bufs[slot][...]
        p = jnp.exp(s - jnp.max(s, axis=-1, keepdims=True))
        l_bufs[slot][...] = jnp.broadcast_to(jnp.sum(p, axis=-1, keepdims=True), (GQA_TQ, GA_DH))
        p_bufs[slot][...] = p.astype(BF16)

    def weighted(head, blk):
        rows = pl.ds(pl.multiple_of(blk * GQA_TQ, GQA_TQ), GQA_TQ)
        cols = slice(head * GA_DH, (head + 1) * GA_DH)
        o = _mm(p_bufs[head][...], v_scr[...]) / l_bufs[head][...]
        y_ref[rows, cols] = (o * _silu(z_ref[rows, cols].astype(F32))).astype(y_ref.dtype)

    prep(0, 0)
    prep(1, 0)
    prep(0, 1)
    logits(0, 0)
    logits(1, 0)
    softmax(0)

    def body(j, carry):
        logits(0, j + 1)
        softmax(1)
        weighted(0, j)
        prep(1, j + 1)
        logits(1, j + 1)
        softmax(0)
        weighted(1, j)
        prep(0, jnp.minimum(j + 2, n_blk - 1))
        return carry

    lax.fori_loop(0, n_blk - 1, body, 0)
    softmax(1)
    weighted(0, n_blk - 1)
    weighted(1, n_blk - 1)


def _gqa(pm3, cos_t, sin_t, qg, kg):
    b, s, _ = pm3.shape
    gw = (GA_HEADS // GA_KV_HEADS) * GA_DH
    return pl.pallas_call(
        _gqa_kernel,
        grid=(b, GA_KV_HEADS),
        in_specs=[
            pl.BlockSpec((None, s, gw), lambda i, j: (i, 0, _OFF["c_q"] // gw + j)),
            pl.BlockSpec((None, s, GA_DH), lambda i, j: (i, 0, _OFF["c_k"] // GA_DH + j)),
            pl.BlockSpec((None, s, GA_DH), lambda i, j: (i, 0, _OFF["c_v"] // GA_DH + j)),
            pl.BlockSpec((None, s, gw), lambda i, j: (i, 0, _OFF["c_z"] // gw + j)),
            pl.BlockSpec((s, GA_DH), lambda i, j: (0, 0)),
            pl.BlockSpec((s, GA_DH), lambda i, j: (0, 0)),
            pl.BlockSpec((1, GA_DH), lambda i, j: (0, 0)),
            pl.BlockSpec((1, GA_DH), lambda i, j: (0, 0)),
        ],
        out_specs=pl.BlockSpec((None, s, gw), lambda i, j: (i, 0, j)),
        out_shape=jax.ShapeDtypeStruct((b, s, BRANCH_W), BF16),
        scratch_shapes=[pltpu.VMEM((gw // GA_DH * s, GA_DH), BF16), pltpu.VMEM((s, GA_DH), BF16),
                        pltpu.VMEM((s, GA_DH), BF16),
                        pltpu.VMEM((GQA_TQ, s), F32), pltpu.VMEM((GQA_TQ, s), F32),
                        pltpu.VMEM((GQA_TQ, s), BF16), pltpu.VMEM((GQA_TQ, s), BF16),
                        pltpu.VMEM((GQA_TQ, GA_DH), F32), pltpu.VMEM((GQA_TQ, GA_DH), F32)],
        compiler_params=_cparams(("parallel", "parallel")),
        name="gqa",
    )(pm3, pm3, pm3, pm3, cos_t, sin_t, qg, kg)


NA_ROW_UNROLL = 16


def _na_kernel(q_ref, k_ref, v_ref, z_ref, bias_ref, qg_ref, kg_ref, y_ref, q_scr, k_scr, v_scr, o_scr):
    s_len = q_ref.shape[0]
    rows = s_len // GRID_W
    kr = min(NA_ROWS, rows)
    hi = lax.broadcasted_iota(jnp.int32, (2 * NA_DH, 2 * NA_DH), 0) >= NA_DH
    hj = lax.broadcasted_iota(jnp.int32, (2 * NA_DH, 2 * NA_DH), 1) >= NA_DH
    same_head = jnp.where(hi == hj, 1.0, 0.0).astype(BF16)

    def rms_pair(x, g):
        x2 = x * x
        x2_hi = x2.astype(BF16)
        x2_lo = (x2 - x2_hi.astype(F32)).astype(BF16)
        ssq = (jnp.dot(x2_hi, same_head, preferred_element_type=F32)
               + jnp.dot(x2_lo, same_head, preferred_element_type=F32))
        return x * lax.rsqrt(ssq * (1.0 / NA_DH) + EPS) * g

    scale = NA_DH ** -0.5
    assert math.log2(scale).is_integer()
    qn = rms_pair(q_ref[...].astype(F32), qg_ref[...]) * scale
    kn = rms_pair(k_ref[...].astype(F32), kg_ref[...])
    vf = v_ref[...].astype(F32)
    for hh in range(2):
        cols = slice(hh * NA_DH, (hh + 1) * NA_DH)
        q_scr[hh] = qn[:, cols].astype(BF16)
        k_scr[hh] = kn[:, cols].astype(BF16)
        v_scr[hh] = vf[:, cols].astype(BF16)
    nkeys = kr * GRID_W

    def body(it, carry):
        units = []
        for u in range(NA_ROW_UNROLL):
            r = it * NA_ROW_UNROLL + u
            r0 = jnp.clip(r - kr // 2, 0, rows - kr)
            var = r0 - r + (NA_ROWS - 1)
            qrows = pl.ds(pl.multiple_of(r * GRID_W, GRID_W), GRID_W)
            krows = pl.ds(pl.multiple_of(r0 * GRID_W, GRID_W), nkeys)
            units += [(hh, var, qrows, krows) for hh in range(2)]
        logits = [_mm_nt(q_scr[hh, qrows, :], k_scr[hh, krows, :]) + bias_ref[hh, var]
                  for hh, var, qrows, krows in units]
        probs = [jnp.exp(s - jnp.max(s, axis=-1, keepdims=True)) for s in logits]
        sums = [jnp.sum(p, axis=-1, keepdims=True) for p in probs]
        outs = [_mm(p, v_scr[hh, krows, :]) for p, (hh, _, _, krows) in zip(probs, units)]
        for o, l, (hh, _, qrows, _) in zip(outs, sums, units):
            o_scr[hh, qrows, :] = o / l
        return carry

    lax.fori_loop(0, rows // NA_ROW_UNROLL, body, 0)
    o = jnp.concatenate([o_scr[0], o_scr[1]], axis=-1)
    y_ref[...] = (o * _silu(z_ref[...].astype(F32))).astype(y_ref.dtype)


def _na_bias_table(rpb, rows):
    kr = min(NA_ROWS, rows)
    c = jnp.arange(GRID_W)
    c0 = jnp.clip(c - NA_COLS // 2, 0, GRID_W - NA_COLS)
    in_win = (c[None, :] >= c0[:, None]) & (c[None, :] < c0[:, None] + NA_COLS)
    col_off = jnp.clip(c[None, :] - c[:, None], -(NA_COLS - 1), NA_COLS - 1) + NA_COLS - 1
    t = rpb[:, :, col_off]
    t = jnp.where(in_win[None, None], t, NEG)
    ro = jnp.arange(NA_ROWS)[:, None] + jnp.arange(kr)[None, :]
    tv = t[:, ro]
    tv = tv.transpose(0, 1, 3, 2, 4)
    return tv.reshape(rpb.shape[0], NA_ROWS, GRID_W, kr * GRID_W).astype(F32)


def _natten(pm3, bias, qg, kg):
    b, s, _ = pm3.shape
    pw = 2 * NA_DH
    npair = NA_HEADS // 2
    hw = NA_HEADS * NA_DH
    bias5 = bias.reshape(npair, 2, *bias.shape[1:])
    return pl.pallas_call(
        _na_kernel,
        grid=(npair, b),
        in_specs=[
            pl.BlockSpec((None, s, pw), lambda p, i: (i, 0, _OFF["b_qkv"] // pw + p)),
            pl.BlockSpec((None, s, pw), lambda p, i: (i, 0, (_OFF["b_qkv"] + hw) // pw + p)),
            pl.BlockSpec((None, s, pw), lambda p, i: (i, 0, (_OFF["b_qkv"] + 2 * hw) // pw + p)),
            pl.BlockSpec((None, s, pw), lambda p, i: (i, 0, _OFF["b_z"] // pw + p)),
            pl.BlockSpec((None,) + bias5.shape[1:], lambda p, i: (p, 0, 0, 0, 0)),
            pl.BlockSpec((1, pw), lambda p, i: (0, 0)),
            pl.BlockSpec((1, pw), lambda p, i: (0, 0)),
        ],
        out_specs=pl.BlockSpec((None, s, pw), lambda p, i: (i, 0, p)),
        out_shape=jax.ShapeDtypeStruct((b, s, BRANCH_W), BF16),
        scratch_shapes=[pltpu.VMEM((2, s, NA_DH), BF16), pltpu.VMEM((2, s, NA_DH), BF16),
                        pltpu.VMEM((2, s, NA_DH), BF16), pltpu.VMEM((2, s, NA_DH), F32)],
        compiler_params=_cparams(("parallel", "parallel")),
        name="natten",
    )(pm3, pm3, pm3, pm3, bias5, qg, kg)


INV_BASE = 8
PRE_BLK = 4 * BLK


def _inverse_level_masks():
    i = lax.broadcasted_iota(jnp.int32, (BLK, BLK), 0)
    j = lax.broadcasted_iota(jnp.int32, (BLK, BLK), 1)
    same = lambda size: (i >> int(math.log2(size))) == (j >> int(math.log2(size)))
    base = same(INV_BASE)
    joins = []
    size = INV_BASE
    while size < DN_CHUNK:
        joins.append(same(2 * size) & jnp.logical_not(same(size)))
        size *= 2
    return base, joins


def _tri_inverses(l_mats, eyes, level_masks):
    base, joins = level_masks
    ps = [jnp.where(base, -l, 0.0) for l in l_mats]
    ts = [eye + p for eye, p in zip(eyes, ps)]
    for _ in range(int(math.log2(INV_BASE)) - 1):
        ps = [_mm(p, p) for p in ps]
        ts = [t + _mm(t, p) for t, p in zip(ts, ps)]
    for join in joins:
        mids = [_mm(jnp.where(join, l, 0.0), t) for l, t in zip(l_mats, ts)]
        ts = [t - _mm(t, mid) for t, mid in zip(ts, mids)]
    return ts


DN_HW = DN_HEADS * LANES
PK_W, PK_QD, PK_KD, PK_QK = (i * DN_HW for i in range(4))


def _dn_pre_kernel(qkv_ref, sm_ref, cw_ref, lp_ref, uf_ref, ub_ref, pkf_ref, pkb_ref, gtf_ref, gtb_ref):
    n = pl.program_id(1)
    s_len = qkv_ref.shape[0]
    assert qkv_ref.dtype == BF16
    halo = 2 * SUBLANES
    n_sub = sm_ref.shape[0] // BLK
    chunks_per_blk = BLK // DN_CHUNK
    out_row = lax.broadcasted_iota(jnp.int32, (BLK, BLK + 2 * halo), 0)
    in_row = lax.broadcasted_iota(jnp.int32, (BLK, BLK + 2 * halo), 1)
    shifts = {j: jnp.where(in_row == out_row + (halo + j - DN_CONV // 2), 1.0, 0.0).astype(BF16)
              for j in range(DN_CONV) if j != DN_CONV // 2}
    dir_masks = [_chunk_masks(reverse=(d == 1)) for d in range(2)]
    level_masks = _inverse_level_masks()
    outs = ((uf_ref, pkf_ref, gtf_ref), (ub_ref, pkb_ref, gtb_ref))

    def l2n(x):
        return x * lax.rsqrt(jnp.sum(x * x, axis=-1, keepdims=True) + EPS)

    blocks = []
    for sub in range(n_sub):
        t0 = pl.multiple_of((n * n_sub + sub) * BLK, BLK)
        pstart = pl.multiple_of(jnp.maximum(t0 - halo, 0), halo)
        nstart = pl.multiple_of(jnp.minimum(t0 + BLK, s_len - halo), halo)
        prev = qkv_ref[pl.ds(pstart, halo), :]
        prev = jnp.where(t0 > 0, prev, jnp.zeros_like(prev))
        cur = qkv_ref[pl.ds(t0, BLK), :]
        nxt = qkv_ref[pl.ds(nstart, halo), :]
        nxt = jnp.where(t0 + BLK < s_len, nxt, jnp.zeros_like(nxt))
        xw = jnp.concatenate([prev, cur, nxt], axis=0)
        conv = cur.astype(F32) * cw_ref[DN_CONV // 2:DN_CONV // 2 + 1, :]
        for j, shift in shifts.items():
            conv = conv + jnp.dot(shift, xw, preferred_element_type=F32) * cw_ref[j:j + 1, :]
        conv = _silu(conv)

        rows = slice(sub * BLK, (sub + 1) * BLK)
        sm = sm_ref[rows, :]
        g_all = -jnp.exp(lp_ref[0:1, :]) * _softplus(sm + lp_ref[1:2, :])
        beta_all = _sigmoid(sm)
        per_dir = []
        for d in range(2):
            same, incl, strict, eye_b = dir_masks[d]
            gc = _mask_sum(incl, g_all)
            tot = _mask_sum(same, g_all)
            per_dir.append((incl, strict, _as_f32(eye_b), gc, gc.T, tot))
        blocks.append((sub, rows, conv, beta_all, per_dir))

    heads = []
    for sub, rows, conv, beta_all, per_dir in blocks:
        for h in range(DN_HEADS):
            q = l2n(conv[:, h * DN_DK:(h + 1) * DN_DK]) * (DN_DK ** -0.5)
            k = l2n(conv[:, (DN_HEADS + h) * DN_DK:(DN_HEADS + h + 1) * DN_DK])
            v = conv[:, 2 * DN_HEADS * DN_DK + h * DN_DV:2 * DN_HEADS * DN_DK + (h + 1) * DN_DV]
            heads.append((sub, rows, beta_all, per_dir, h, q, k, v))
    kks = [_mm_nt(hd[6], hd[6]) for hd in heads]
    qks = [_mm_nt(hd[5], hd[6]) for hd in heads]

    l_mats, rhss, eyes, slots = [], [], [], []
    for (sub, rows, beta_all, per_dir, h, q, k, v), kk, qk in zip(heads, kks, qks):
        hc = slice(h * LANES, (h + 1) * LANES)
        for d in range(2):
            incl, strict, eye_f, gc, gct, tot = per_dir[d]
            u_ref, pk_ref, gt_ref = outs[d]
            pk = lambda off: slice(off + h * LANES, off + (h + 1) * LANES)
            c = _L_AA + d * DN_HEADS + h
            gcol = gc[:, c:c + 1]
            grow = gct[c:c + 1, :]
            tcol = tot[:, c:c + 1]
            beta = beta_all[:, _L_AB + d * DN_HEADS + h:_L_AB + d * DN_HEADS + h + 1]
            decay = jnp.where(incl, jnp.exp(jnp.where(incl, gcol - grow, 0.0)), 0.0)
            l_mats.append(jnp.where(strict, beta * kk * decay, 0.0))
            egc = jnp.exp(gcol)
            rhss.append(jnp.concatenate([v * beta, k * (beta * egc)], axis=-1).astype(BF16))
            eyes.append(eye_f)
            slots.append((rows, h, d))
            pk_ref[rows, pk(PK_QD)] = (q * egc).astype(pk_ref.dtype)
            pk_ref[rows, pk(PK_KD)] = (k * jnp.exp(tcol - gcol)).astype(pk_ref.dtype)
            pk_ref[rows, pk(PK_QK)] = (qk * decay).astype(pk_ref.dtype)
            gtot = jnp.exp(tcol)
            for ci in range(chunks_per_blk):
                g0 = (sub * chunks_per_blk + ci) * SUBLANES
                gt_ref[g0:g0 + SUBLANES, hc] = jnp.broadcast_to(
                    gtot[ci * DN_CHUNK:ci * DN_CHUNK + SUBLANES, :], (SUBLANES, LANES))

    t_invs = _tri_inverses(l_mats, eyes, level_masks)
    sols = [_mm(t, rhs) for t, rhs in zip(t_invs, rhss)]
    for (rows, h, d), sol in zip(slots, sols):
        u_ref, pk_ref = outs[d][0], outs[d][1]
        u_ref[rows, h * LANES:(h + 1) * LANES] = sol[:, :DN_DV]
        pk_ref[rows, PK_W + h * LANES:PK_W + (h + 1) * LANES] = sol[:, DN_DV:].astype(pk_ref.dtype)


def _dn_pre(pm3, ps3, conv_w8, lane_params):
    b, s, _ = pm3.shape
    nblk = s // PRE_BLK
    gt_rows = PRE_BLK // DN_CHUNK * SUBLANES
    wq = DN_HEADS * (2 * DN_DK + DN_DV)
    hw = DN_HEADS * LANES
    tok = lambda i, j: (i, j, 0)
    big = lambda dt: jax.ShapeDtypeStruct((b, s, hw), dt)
    gts = jax.ShapeDtypeStruct((b, nblk * gt_rows, hw), F32)
    bs_tok = pl.BlockSpec((None, PRE_BLK, hw), tok)
    bs_gt = pl.BlockSpec((None, gt_rows, hw), tok)
    bs_pk = pl.BlockSpec((None, PRE_BLK, 4 * hw), tok)
    packed = jax.ShapeDtypeStruct((b, s, 4 * hw), BF16)
    return pl.pallas_call(
        _dn_pre_kernel,
        grid=(b, nblk),
        in_specs=[
            pl.BlockSpec((None, s, wq), lambda i, j: (i, 0, _OFF["a_qkv"] // wq)),
            pl.BlockSpec((None, PRE_BLK, LANES), tok),
            pl.BlockSpec((SUBLANES, wq), lambda i, j: (0, 0)),
            pl.BlockSpec((SUBLANES, LANES), lambda i, j: (0, 0)),
        ],
        out_specs=[bs_tok, bs_tok, bs_pk, bs_pk, bs_gt, bs_gt],
        out_shape=[big(F32), big(F32), packed, packed, gts, gts],
        compiler_params=_cparams(("parallel", "arbitrary")),
        name="dn_pre",
    )(pm3, ps3, conv_w8, lane_params)


SCAN_BLK = 4 * BLK


def _dn_scan_kernel(uf_ref, pkf_ref, gtf_ref, ub_ref, pkb_ref, gtb_ref, of_ref, ob_ref, st_scr):
    @pl.when(pl.program_id(1) == 0)
    def _():
        st_scr[...] = jnp.zeros_like(st_scr)

    per_blk = BLK // DN_CHUNK
    nchunk = SCAN_BLK // DN_CHUNK
    zeros_c = jnp.zeros((DN_CHUNK, DN_DV), F32)
    streams = ((uf_ref, pkf_ref, gtf_ref, of_ref, range(nchunk)),
               (ub_ref, pkb_ref, gtb_ref, ob_ref, range(nchunk - 1, -1, -1)))
    chains = [(d, h) + streams[d] for d in range(2) for h in range(DN_HEADS)]
    states = [st_scr[d * DN_HEADS + h] for d, h, *_ in chains]
    for step in range(nchunk):
        rs, v_pads = [], []
        for (d, h, u_ref, pk_ref, gt_ref, o_ref, order), state in zip(chains, states):
            rows = slice(order[step] * DN_CHUNK, (order[step] + 1) * DN_CHUNK)
            w = pk_ref[rows, PK_W + h * LANES:PK_W + (h + 1) * LANES]
            qd = pk_ref[rows, PK_QD + h * LANES:PK_QD + (h + 1) * LANES]
            rs.append(_mm(jnp.concatenate([w, qd], axis=0), state))
        for (d, h, u_ref, pk_ref, gt_ref, o_ref, order), r in zip(chains, rs):
            ci = order[step]
            rows = slice(ci * DN_CHUNK, (ci + 1) * DN_CHUNK)
            parts = [zeros_c] * per_blk
            parts[ci % per_blk] = u_ref[rows, h * LANES:(h + 1) * LANES] - r[:DN_CHUNK]
            v_pads.append(jnp.concatenate(parts, axis=0))
        new_states = []
        for (d, h, u_ref, pk_ref, gt_ref, o_ref, order), r, v_pad, state in zip(chains, rs, v_pads, states):
            hc = slice(h * LANES, (h + 1) * LANES)
            ci = order[step]
            rows = slice(ci * DN_CHUNK, (ci + 1) * DN_CHUNK)
            blk_rows = slice((ci // per_blk) * BLK, (ci // per_blk + 1) * BLK)
            o_ref[rows, hc] = r[DN_CHUNK:] + _mm(pk_ref[rows, PK_QK + h * LANES:PK_QK + (h + 1) * LANES], v_pad)
            gt = gt_ref[ci * SUBLANES:ci * SUBLANES + 1, hc]
            kd = pk_ref[blk_rows, PK_KD + h * LANES:PK_KD + (h + 1) * LANES]
            new_states.append(state * gt + _mm_tn(kd, v_pad))
        states = new_states
    for (d, h, *_), state in zip(chains, states):
        st_scr[d * DN_HEADS + h] = state


def _dn_scan(pre):
    uf, ub, pkf, pkb, gtf, gtb = pre
    b, s, hw = uf.shape
    nblk = s // SCAN_BLK
    gt_rows = SCAN_BLK // DN_CHUNK * SUBLANES
    fwd = lambda i, j: (i, j, 0)
    bwd = lambda i, j: (i, nblk - 1 - j, 0)
    def specs(imap):
        return [pl.BlockSpec((None, SCAN_BLK, hw), imap), pl.BlockSpec((None, SCAN_BLK, 4 * hw), imap),
                pl.BlockSpec((None, gt_rows, hw), imap)]
    return pl.pallas_call(
        _dn_scan_kernel,
        grid=(b, nblk),
        in_specs=specs(fwd) + specs(bwd),
        out_specs=[pl.BlockSpec((None, SCAN_BLK, hw), fwd), pl.BlockSpec((None, SCAN_BLK, hw), bwd)],
        out_shape=[jax.ShapeDtypeStruct((b, s, hw), F32)] * 2,
        scratch_shapes=[pltpu.VMEM((2 * DN_HEADS, DN_DK, DN_DV), F32)],
        compiler_params=_cparams(("parallel", "arbitrary")),
        name="dn_scan",
    )(uf, pkf, gtf, ub, pkb, gtb)


ML_AUG = 2 * LANES
ML_GROUP = 4
ML_BLK = 4 * BLK


def _ml_kernel(qf_ref, kf_ref, vf_ref, smf_ref, qb_ref, kb_ref, vb_ref, smb_ref, lp_ref,
               hf_ref, hb_ref, c_scr, m_scr):
    @pl.when(pl.program_id(1) == 0)
    def _():
        c_scr[...] = jnp.zeros_like(c_scr)
        m_scr[...] = jnp.zeros_like(m_scr)

    nchunk = BLK // ML_CHUNK
    ones_col = jnp.ones((BLK, LANES), BF16)
    zeros_aug = jnp.zeros((ML_CHUNK, ML_AUG), BF16)
    streams = ((qf_ref, kf_ref, vf_ref, smf_ref, hf_ref, range(nchunk)),
               (qb_ref, kb_ref, vb_ref, smb_ref, hb_ref, range(nchunk - 1, -1, -1)))
    lanes = lambda col: jnp.broadcast_to(col, (col.shape[0], LANES))
    n_sub = smf_ref.shape[0] // BLK
    units = [(d, p if d == 0 else n_sub - 1 - p) for p in range(n_sub) for d in range(2)]
    for d, sub in units:
        _ml_block(d, sub, streams[d], lp_ref, c_scr, m_scr, lanes, ones_col, zeros_aug, nchunk)


def _ml_block(d, sub, stream, lp_ref, c_scr, m_scr, lanes, ones_col, zeros_aug, nchunk):
    q_ref, k_ref, v_ref, sm_ref, h_ref, order = stream
    blk_rows = slice(sub * BLK, (sub + 1) * BLK)
    same, incl, _, _ = _chunk_masks(reverse=(d == 1))
    sm = sm_ref[blk_rows, :]
    ig_all = sm + lp_ref[0:1, :]
    x = sm + lp_ref[1:2, :]
    lf_all = jnp.minimum(x, 0.0) - jnp.log(1.0 + jnp.exp(-jnp.abs(x)))
    lf_all = pltpu.roll(lf_all, LANES - (_L_DF - _L_DI), 1)
    bc_all = _mask_sum(incl, lf_all)
    tot_all = _mask_sum(same, lf_all)
    a_all = ig_all - bc_all
    mwa_all = jnp.concatenate(
        [jnp.broadcast_to(jnp.max(a_all[ci * ML_CHUNK:(ci + 1) * ML_CHUNK], axis=0, keepdims=True),
                          (ML_CHUNK, LANES)) for ci in range(nchunk)], axis=0)
    a_t, w_all, mw_all = a_all.T, jnp.exp(a_all - mwa_all), tot_all + mwa_all

    for chains in [[(d, h) for h in range(g, g + ML_GROUP)] for g in range(0, ML_HEADS, ML_GROUP)]:
        ins = []
        for _, h in chains:
            q = q_ref[blk_rows, h * ML_DK:(h + 1) * ML_DK].astype(BF16)
            k = k_ref[blk_rows, h * ML_DK:(h + 1) * ML_DK].astype(F32) * (ML_DK ** -0.5)
            v_aug = jnp.concatenate([v_ref[blk_rows, h * ML_DV:(h + 1) * ML_DV].astype(BF16), ones_col], axis=-1)
            ins.append((q, k, v_aug))
        qks = [_mm_nt(q, k) for q, k, _ in ins]

        mids = []
        for (d, h), (q, k, v_aug), qk in zip(chains, ins, qks):
            c = _L_DI + d * ML_HEADS + h
            b_l = lanes(bc_all[:, c:c + 1])
            dlog = jnp.where(incl, b_l + a_t[c:c + 1, :], NEG)
            m_intra = lanes(jnp.max(dlog, axis=-1, keepdims=True))
            s_intra = qk * jnp.exp(dlog - m_intra)
            wk = (k * lanes(w_all[:, c:c + 1])[:, :ML_DK]).astype(BF16)
            mids.append((b_l, m_intra, s_intra.astype(BF16), wk))
        p_intras = [_mm(s_b, v_aug) for (_, _, s_b, _), (_, _, v_aug) in zip(mids, ins)]
        kvs = []
        for (_, _, _, wk), (_, _, v_aug) in zip(mids, ins):
            per_chunk = []
            for ci in range(nchunk):
                parts = [zeros_aug] * nchunk
                parts[ci] = v_aug[ci * ML_CHUNK:(ci + 1) * ML_CHUNK]
                per_chunk.append(_mm_tn(wk, jnp.concatenate(parts, axis=0)))
            kvs.append(per_chunk)

        c_sts = [c_scr[d * ML_HEADS + h] for d, h in chains]
        m_sts = [m_scr[d * ML_HEADS + h][0:1, :] for d, h in chains]
        for step in range(nchunk):
            qcs = []
            ci = order[step]
            for (q, _, _), c_st in zip(ins, c_sts):
                qcs.append(_mm(q[ci * ML_CHUNK:(ci + 1) * ML_CHUNK], c_st))
            for idx, (_, h) in enumerate(chains):
                b_l, m_intra, _, _ = mids[idx]
                c = _L_DI + d * ML_HEADS + h
                rows = slice(ci * ML_CHUNK, (ci + 1) * ML_CHUNK)
                out_rows = slice(sub * BLK + ci * ML_CHUNK, sub * BLK + (ci + 1) * ML_CHUNK)
                r8 = slice(ci * ML_CHUNK, ci * ML_CHUNK + SUBLANES)
                m_st, c_st, qc = m_sts[idx], c_sts[idx], qcs[idx]
                m_inter = b_l[rows] + m_st
                m_i = jnp.maximum(m_intra[rows], m_inter)
                f_i = jnp.exp(m_intra[rows] - m_i)
                inter = jnp.exp(m_inter - m_i)
                both = (jnp.concatenate([inter, inter], axis=-1) * qc
                        + jnp.concatenate([f_i, f_i], axis=-1) * p_intras[idx][rows])
                numer, denom = both[:, :ML_DV], both[:, ML_DV:]
                h_ref[out_rows, h * ML_DV:(h + 1) * ML_DV] = numer / jnp.maximum(jnp.abs(denom), jnp.exp(-m_i))
                tot_c = lanes(tot_all[r8, c:c + 1])[0:1]
                mw_c = lanes(mw_all[r8, c:c + 1])[0:1]
                m_new = jnp.maximum(tot_c + m_st, mw_c)
                dec = jnp.exp(tot_c + m_st - m_new)
                gain = jnp.exp(mw_c - m_new)
                c_sts[idx] = (jnp.concatenate([dec, dec], axis=-1) * c_st
                              + jnp.concatenate([gain, gain], axis=-1) * kvs[idx][ci])
                m_sts[idx] = m_new
        for idx, (d, h) in enumerate(chains):
            c_scr[d * ML_HEADS + h] = c_sts[idx]
            m_scr[d * ML_HEADS + h] = jnp.broadcast_to(m_sts[idx], (SUBLANES, LANES))


def _mlstm(pm3, ps3, lane_params):
    b, s, _ = pm3.shape
    nblk = s // ML_BLK
    qw = ML_HEADS * ML_DK
    vw = ML_HEADS * ML_DV
    def specs(tmap):
        blk = lambda j: tmap(j)
        return [
            pl.BlockSpec((None, ML_BLK, qw), lambda i, j: (i, blk(j), _OFF["d_q"] // qw)),
            pl.BlockSpec((None, ML_BLK, qw), lambda i, j: (i, blk(j), _OFF["d_k"] // qw)),
            pl.BlockSpec((None, ML_BLK, vw), lambda i, j: (i, blk(j), _OFF["d_v"] // vw)),
            pl.BlockSpec((None, ML_BLK, LANES), lambda i, j: (i, blk(j), 0)),
        ]
    fwd = lambda j: j
    bwd = lambda j: nblk - 1 - j
    return pl.pallas_call(
        _ml_kernel,
        grid=(b, nblk),
        in_specs=specs(fwd) + specs(bwd) + [pl.BlockSpec((SUBLANES, LANES), lambda i, j: (0, 0))],
        out_specs=[pl.BlockSpec((None, ML_BLK, vw), lambda i, j: (i, j, 0)),
                   pl.BlockSpec((None, ML_BLK, vw), lambda i, j: (i, nblk - 1 - j, 0))],
        out_shape=[jax.ShapeDtypeStruct((b, s, vw), F32)] * 2,
        scratch_shapes=[pltpu.VMEM((2 * ML_HEADS, ML_DK, ML_AUG), F32),
                        pltpu.VMEM((2 * ML_HEADS, SUBLANES, LANES), F32)],
        compiler_params=_cparams(("parallel", "arbitrary")),
        name="mlstm",
    )(pm3, pm3, pm3, ps3, pm3, pm3, pm3, ps3, lane_params)


def _merge_kernel(x_ref, af_ref, ab_ref, df_ref, db_ref, yb_ref, yc_ref, az_ref, dz_ref, do_ref, gl_ref,
                  ag_ref, dg_ref, wb_ref, wo_ref, o_ref):
    d = x_ref.shape[-1]

    def head_rms(x, g):
        outs = []
        for h in range(x.shape[-1] // LANES):
            xh = x[:, h * LANES:(h + 1) * LANES]
            ms = jnp.mean(xh * xh, axis=-1, keepdims=True)
            outs.append(xh * lax.rsqrt(ms + EPS) * g)
        return jnp.concatenate(outs, axis=-1)

    ya = head_rms(af_ref[...] + ab_ref[...], ag_ref[...]) * _silu(az_ref[...].astype(F32))
    yd = _sigmoid(do_ref[...].astype(F32)) * head_rms(df_ref[...] + db_ref[...], dg_ref[...])
    yd = yd * _silu(dz_ref[...].astype(F32))
    merged = None
    for i, y in enumerate((ya.astype(BF16), yb_ref[...], yc_ref[...], yd.astype(BF16))):
        proj = jnp.dot(y, wb_ref[i], preferred_element_type=F32)
        term = _sigmoid(gl_ref[:, i * d:(i + 1) * d].astype(F32)) * proj
        merged = term if merged is None else merged + term
    o_ref[...] = x_ref[...] + jnp.dot(merged.astype(BF16), wo_ref[...], preferred_element_type=F32)


def _merge(x2d, af, ab, df, db, yb, yc, pm2, ag, dg, wb, wo, tm=512):
    m, d = x2d.shape
    gw = N_BRANCH * d
    w = BRANCH_W
    tok = pl.BlockSpec((tm, w), lambda i: (i, 0))
    col = lambda name: pl.BlockSpec((tm, w), lambda i: (i, _OFF[name] // w))
    vec = pl.BlockSpec((1, LANES), lambda i: (0, 0))
    return pl.pallas_call(
        _merge_kernel,
        grid=(m // tm,),
        in_specs=[
            pl.BlockSpec((tm, d), lambda i: (i, 0)),
            tok, tok, tok, tok, tok, tok,
            col("a_z"), col("d_z"), col("d_o"),
            pl.BlockSpec((tm, gw), lambda i: (i, _OFF["gate"] // gw)),
            vec, vec,
            pl.BlockSpec((N_BRANCH, w, d), lambda i: (0, 0, 0)),
            pl.BlockSpec((d, d), lambda i: (0, 0)),
        ],
        out_specs=pl.BlockSpec((tm, d), lambda i: (i, 0)),
        out_shape=jax.ShapeDtypeStruct((m, d), F32),
        compiler_params=_cparams(("parallel",)),
        name="merge",
    )(x2d, af, ab, df, db, yb, yc, pm2, pm2, pm2, pm2, ag, dg, wb, wo)


def _rope_lane_tables(s):
    t = jnp.arange(s)
    row = (t // GRID_W).astype(F32)
    col = (t % GRID_W).astype(F32)
    m = GA_DH // 4
    inv = ROPE_THETA ** (-jnp.arange(m, dtype=F32) / m)
    ar = row[:, None] * inv
    ac = col[:, None] * inv
    cos_t = jnp.concatenate([jnp.cos(ar), jnp.cos(ar), jnp.cos(ac), jnp.cos(ac)], axis=-1)
    sin_t = jnp.concatenate([-jnp.sin(ar), jnp.sin(ar), -jnp.sin(ac), jnp.sin(ac)], axis=-1)
    return cos_t.astype(F32), sin_t.astype(F32)


def _lane_rows(rows):
    tile = jnp.zeros((SUBLANES, LANES), F32)
    for r, (off, vals) in enumerate(rows):
        vals = vals.reshape(-1).astype(F32)
        tile = tile.at[r, off:off + vals.shape[0]].set(vals)
    return tile


def kernel(x, norm_g, w_in, conv_a, dn_a_log, dn_dt_bias, dn_norm_g, na_q_norm, na_k_norm, na_rpb,
           ga_q_norm, ga_k_norm, ml_i_bias, ml_f_bias, ml_norm_g, w_branch, w_out):
    b, s, d = x.shape
    depth = w_in.shape[0]
    cos_t, sin_t = _rope_lane_tables(s)
    x2 = x.reshape(b * s, d)
    for l in range(depth):
        w = w_in[l]
        w_main = jnp.concatenate([w[:, o:o + wd] for _, o, wd in _MAIN_SEGS], axis=1).astype(BF16)
        w_small = jnp.concatenate([w[:, o:o + 8] for o in _SMALL_SRC]
                                  + [jnp.zeros((d, LANES - 32), F32)], axis=1).astype(BF16)
        pm2, ps2 = _inproj(x2, norm_g[l].reshape(1, d), w_main, w_small)
        pm3 = pm2.reshape(b, s, N_MAIN)
        ps3 = ps2.reshape(b, s, LANES)

        conv8 = jnp.zeros((SUBLANES, conv_a.shape[-1]), F32).at[:DN_CONV].set(conv_a[l])
        dn_lp = _lane_rows([(_L_AA, dn_a_log[l]), (_L_AA, dn_dt_bias[l])])
        o_af, o_ab = _dn_scan(_dn_pre(pm3, ps3, conv8, dn_lp))

        ml_lp = _lane_rows([(_L_DI, ml_i_bias[l]), (_L_DF, ml_f_bias[l])])
        h_df, h_db = _mlstm(pm3, ps3, ml_lp)

        hw = BRANCH_W
        bias = _na_bias_table(na_rpb[l], s // GRID_W)
        yb = _natten(pm3, bias, jnp.tile(na_q_norm[l], 2).reshape(1, 2 * NA_DH),
                     jnp.tile(na_k_norm[l], 2).reshape(1, 2 * NA_DH))
        yc = _gqa(pm3, cos_t, sin_t, ga_q_norm[l].reshape(1, GA_DH), ga_k_norm[l].reshape(1, GA_DH))

        x2 = _merge(x2, o_af.reshape(b * s, hw), o_ab.reshape(b * s, hw), h_df.reshape(b * s, hw),
                    h_db.reshape(b * s, hw), yb.reshape(b * s, hw), yc.reshape(b * s, hw), pm2,
                    dn_norm_g[l].reshape(1, LANES), ml_norm_g[l].reshape(1, LANES),
                    w_branch[l].astype(BF16), w_out[l].astype(BF16))
    return x2.reshape(b, s, d)
```

```python
import functools
import math

import jax
import jax.numpy as jnp
from jax import lax
from jax.experimental import pallas as pl
from jax.experimental.pallas import tpu as pltpu

F32 = jnp.float32
BF16 = jnp.bfloat16

D_MODEL = 1024
GRID_W = 64
N_BRANCH = 4
BRANCH_W = 512
EPS = 1e-6
DN_HEADS, DN_DK, DN_DV, DN_CONV, DN_CHUNK = 4, 128, 128, 5, 64
NA_HEADS, NA_DH, NA_ROWS, NA_COLS = 8, 64, 8, 16
GA_HEADS, GA_KV_HEADS, GA_DH = 4, 2, 128
ROPE_THETA = 10000.0
ML_HEADS, ML_DK, ML_DV, ML_CHUNK = 4, 64, 128, 64

LANES = 128
SUBLANES = 8
VMEM_LIMIT_BYTES = 56 * 1024 * 1024

_O_A_QKV, _O_A_A, _O_A_B, _O_A_Z = 0, 1536, 1544, 1552
_O_B_QKV, _O_B_Z = 2064, 3600
_O_C_Q, _O_C_K, _O_C_V, _O_C_Z = 4112, 4624, 4880, 5136
_O_D_Q, _O_D_K, _O_D_V, _O_D_I, _O_D_F, _O_D_O, _O_D_Z = 5648, 5904, 6160, 6672, 6680, 6688, 7200
_O_GATE = 7712
_MAIN_SEGS = (
    ("a_qkv", _O_A_QKV, 1536), ("b_qkv", _O_B_QKV, 1536), ("a_z", _O_A_Z, 512), ("b_z", _O_B_Z, 512),
    ("gate", _O_GATE, 4096), ("c_q", _O_C_Q, 512), ("c_k", _O_C_K, 256), ("c_v", _O_C_V, 256),
    ("c_z", _O_C_Z, 512), ("d_q", _O_D_Q, 256), ("d_k", _O_D_K, 256), ("d_v", _O_D_V, 512),
    ("d_o", _O_D_O, 512), ("d_z", _O_D_Z, 512),
)
_OFF = {}
_o = 0
for _name, _src, _w in _MAIN_SEGS:
    _OFF[_name] = _o
    _o += _w
N_MAIN = _o
_SMALL_SRC = (_O_A_A, _O_A_B, _O_D_I, _O_D_F)
_L_AA, _L_AB, _L_DI, _L_DF = 0, 8, 16, 24

P_DTYPE = BF16
BLK = 128
NEG = -1e30


def _cparams(sem):
    return pltpu.CompilerParams(dimension_semantics=sem, vmem_limit_bytes=VMEM_LIMIT_BYTES)


def _sigmoid(x):
    return 0.5 * jnp.tanh(0.5 * x) + 0.5


def _silu(x):
    return x * _sigmoid(x)


def _softplus(x):
    return jnp.maximum(x, 0.0) + jnp.log(1.0 + jnp.exp(-jnp.abs(x)))


def _mm(a, b):
    return jnp.dot(a.astype(BF16), b.astype(BF16), preferred_element_type=F32)


def _mm_nt(a, b):
    return lax.dot_general(a.astype(BF16), b.astype(BF16), (((1,), (1,)), ((), ())),
                           preferred_element_type=F32)


def _mm_tn(a, b):
    return lax.dot_general(a.astype(BF16), b.astype(BF16), (((0,), (0,)), ((), ())),
                           preferred_element_type=F32)


def _mask_sum(mask, x):
    m = jnp.where(mask, 1.0, 0.0).astype(BF16)
    x1 = x.astype(BF16)
    r1 = x - x1.astype(F32)
    x2 = r1.astype(BF16)
    x3 = (r1 - x2.astype(F32)).astype(BF16)
    dot = lambda v: jnp.dot(m, v, preferred_element_type=F32)
    return dot(x1) + (dot(x2) + dot(x3))


def _chunk_masks(reverse):
    i = lax.broadcasted_iota(jnp.int32, (BLK, BLK), 0)
    j = lax.broadcasted_iota(jnp.int32, (BLK, BLK), 1)
    shift = int(math.log2(DN_CHUNK))
    same = (i >> shift) == (j >> shift)
    if reverse:
        incl = same & (j >= i)
        strict = same & (j > i)
    else:
        incl = same & (j <= i)
        strict = same & (j < i)
    return same, incl, strict, (i == j)


def _as_f32(mask):
    return jnp.where(mask, 1.0, 0.0).astype(F32)


def _inproj_kernel(x_ref, g_ref, w_ref, ws_ref, pm_ref, ps_ref, h_scr):
    @pl.when(pl.program_id(1) == 0)
    def _():
        x = x_ref[...]
        ms = jnp.mean(x * x, axis=-1, keepdims=True)
        h = (x * lax.rsqrt(ms + EPS) * g_ref[...]).astype(BF16)
        h_scr[...] = h
        ps_ref[...] = jnp.dot(h, ws_ref[...], preferred_element_type=F32)

    pm_ref[...] = jnp.dot(h_scr[...], w_ref[...], preferred_element_type=F32).astype(pm_ref.dtype)


def _inproj(x2d, g, w_main, w_small, layer, tm=1024, tn=N_MAIN // 4):
    m, d = x2d.shape
    tm = min(tm, m)
    return pl.pallas_call(
        _inproj_kernel,
        grid=(m // tm, N_MAIN // tn),
        in_specs=[
            pl.BlockSpec((tm, d), lambda i, j: (i, 0)),
            pl.BlockSpec((1, d), lambda i, j: (0, 0)),
            pl.BlockSpec((None, d, tn), lambda i, j: (layer, 0, j)),
            pl.BlockSpec((None, d, LANES), lambda i, j: (layer, 0, 0)),
        ],
        out_specs=[
            pl.BlockSpec((tm, tn), lambda i, j: (i, j)),
            pl.BlockSpec((tm, LANES), lambda i, j: (i, 0)),
        ],
        out_shape=[jax.ShapeDtypeStruct((m, N_MAIN), P_DTYPE), jax.ShapeDtypeStruct((m, LANES), F32)],
        scratch_shapes=[pltpu.VMEM((tm, d), BF16)],
        compiler_params=_cparams(("parallel", "arbitrary")),
        name="inproj",
    )(x2d, g, w_main, w_small)


GQA_TQ = 512


def _gqa_kernel(q_ref, k_ref, v_ref, z_ref, cos_ref, sin_ref, qg_ref, kg_ref, y_ref,
                q_scr, k_scr, v_scr, s0_scr, s1_scr, p0_scr, p1_scr, l0_scr, l1_scr):
    def norm_rope(x, g, cos, sin):
        ms = jnp.mean(x * x, axis=-1, keepdims=True)
        xn = x * lax.rsqrt(ms + EPS) * g
        lane = lax.broadcasted_iota(jnp.int32, xn.shape, 1)
        partner = jnp.where((lane & 63) < 32, pltpu.roll(xn, LANES - 32, 1), pltpu.roll(xn, 32, 1))
        return xn * cos + partner * sin

    s_len = k_ref.shape[0]
    group = GA_HEADS // GA_KV_HEADS
    k_scr[...] = norm_rope(k_ref[...].astype(F32), kg_ref[...], cos_ref[...], sin_ref[...]).astype(BF16)
    v_scr[...] = v_ref[...].astype(BF16)
    assert group == 2
    scale = GA_DH ** -0.5
    n_blk = s_len // GQA_TQ
    s_bufs, p_bufs, l_bufs = (s0_scr, s1_scr), (p0_scr, p1_scr), (l0_scr, l1_scr)

    def stacked(head, blk):
        return pl.ds(pl.multiple_of(head * s_len + blk * GQA_TQ, GQA_TQ), GQA_TQ)

    def prep(head, blk):
        rows = pl.ds(pl.multiple_of(blk * GQA_TQ, GQA_TQ), GQA_TQ)
        q = norm_rope(q_ref[rows, head * GA_DH:(head + 1) * GA_DH].astype(F32), qg_ref[...],
                      cos_ref[rows, :], sin_ref[rows, :])
        q_scr[stacked(head, blk), :] = q.astype(BF16)

    def logits(head, blk):
        s_bufs[head][...] = _mm_nt(q_scr[stacked(head, blk), :], k_scr[...]) * scale

    def softmax(slot):
        s = s_bufs[slot][...]
        p = jnp.exp(s - jnp.max(s, axis=-1, keepdims=True))
        l_bufs[slot][...] = jnp.broadcast_to(jnp.sum(p, axis=-1, keepdims=True), (GQA_TQ, GA_DH))
        p_bufs[slot][...] = p.astype(BF16)

    def weighted(head, blk):
        rows = pl.ds(pl.multiple_of(blk * GQA_TQ, GQA_TQ), GQA_TQ)
        cols = slice(head * GA_DH, (head + 1) * GA_DH)
        o = _mm(p_bufs[head][...], v_scr[...]) / l_bufs[head][...]
        y_ref[rows, cols] = (o * _silu(z_ref[rows, cols].astype(F32))).astype(y_ref.dtype)

    prep(0, 0)
    prep(1, 0)
    prep(0, 1)
    logits(0, 0)
    logits(1, 0)
    softmax(0)

    def body(j, carry):
        logits(0, j + 1)
        softmax(1)
        weighted(0, j)
        prep(1, j + 1)
        logits(1, j + 1)
        softmax(0)
        weighted(1, j)
        prep(0, jnp.minimum(j + 2, n_blk - 1))
        return carry

    lax.fori_loop(0, n_blk - 1, body, 0)
    softmax(1)
    weighted(0, n_blk - 1)
    weighted(1, n_blk - 1)


def _gqa(pm3, cos_t, sin_t, qg, kg):
    b, s, _ = pm3.shape
    gw = (GA_HEADS // GA_KV_HEADS) * GA_DH
    return pl.pallas_call(
        _gqa_kernel,
        grid=(b, GA_KV_HEADS),
        in_specs=[
            pl.BlockSpec((None, s, gw), lambda i, j: (i, 0, _OFF["c_q"] // gw + j)),
            pl.BlockSpec((None, s, GA_DH), lambda i, j: (i, 0, _OFF["c_k"] // GA_DH + j)),
            pl.BlockSpec((None, s, GA_DH), lambda i, j: (i, 0, _OFF["c_v"] // GA_DH + j)),
            pl.BlockSpec((None, s, gw), lambda i, j: (i, 0, _OFF["c_z"] // gw + j)),
            pl.BlockSpec((s, GA_DH), lambda i, j: (0, 0)),
            pl.BlockSpec((s, GA_DH), lambda i, j: (0, 0)),
            pl.BlockSpec((1, GA_DH), lambda i, j: (0, 0)),
            pl.BlockSpec((1, GA_DH), lambda i, j: (0, 0)),
        ],
        out_specs=pl.BlockSpec((None, s, gw), lambda i, j: (i, 0, j)),
        out_shape=jax.ShapeDtypeStruct((b, s, BRANCH_W), BF16),
        scratch_shapes=[pltpu.VMEM((gw // GA_DH * s, GA_DH), BF16), pltpu.VMEM((s, GA_DH), BF16),
                        pltpu.VMEM((s, GA_DH), BF16),
                        pltpu.VMEM((GQA_TQ, s), F32), pltpu.VMEM((GQA_TQ, s), F32),
                        pltpu.VMEM((GQA_TQ, s), BF16), pltpu.VMEM((GQA_TQ, s), BF16),
                        pltpu.VMEM((GQA_TQ, GA_DH), F32), pltpu.VMEM((GQA_TQ, GA_DH), F32)],
        compiler_params=_cparams(("parallel", "parallel")),
        name="gqa",
    )(pm3, pm3, pm3, pm3, cos_t, sin_t, qg, kg)


NA_ROW_UNROLL = 16


def _na_kernel(q_ref, k_ref, v_ref, z_ref, bias_ref, qg_ref, kg_ref, y_ref, q_scr, k_scr, v_scr, o_scr):
    s_len = q_ref.shape[0]
    rows = s_len // GRID_W
    kr = min(NA_ROWS, rows)
    hi = lax.broadcasted_iota(jnp.int32, (2 * NA_DH, 2 * NA_DH), 0) >= NA_DH
    hj = lax.broadcasted_iota(jnp.int32, (2 * NA_DH, 2 * NA_DH), 1) >= NA_DH
    same_head = jnp.where(hi == hj, 1.0, 0.0).astype(BF16)

    def rms_pair(x, g):
        x2 = x * x
        x2_hi = x2.astype(BF16)
        x2_lo = (x2 - x2_hi.astype(F32)).astype(BF16)
        ssq = (jnp.dot(x2_hi, same_head, preferred_element_type=F32)
               + jnp.dot(x2_lo, same_head, preferred_element_type=F32))
        return x * lax.rsqrt(ssq * (1.0 / NA_DH) + EPS) * g

    scale = NA_DH ** -0.5
    assert math.log2(scale).is_integer()
    qn = rms_pair(q_ref[...].astype(F32), qg_ref[...]) * scale
    kn = rms_pair(k_ref[...].astype(F32), kg_ref[...])
    vf = v_ref[...].astype(F32)
    for hh in range(2):
        cols = slice(hh * NA_DH, (hh + 1) * NA_DH)
        q_scr[hh] = qn[:, cols].astype(BF16)
        k_scr[hh] = kn[:, cols].astype(BF16)
        v_scr[hh] = vf[:, cols].astype(BF16)
    nkeys = kr * GRID_W

    def body(it, carry):
        units = []
        for u in range(NA_ROW_UNROLL):
            r = it * NA_ROW_UNROLL + u
            r0 = jnp.clip(r - kr // 2, 0, rows - kr)
            var = r0 - r + (NA_ROWS - 1)
            qrows = pl.ds(pl.multiple_of(r * GRID_W, GRID_W), GRID_W)
            krows = pl.ds(pl.multiple_of(r0 * GRID_W, GRID_W), nkeys)
            units += [(hh, var, qrows, krows) for hh in range(2)]
        logits = [_mm_nt(q_scr[hh, qrows, :], k_scr[hh, krows, :]) + bias_ref[hh, var]
                  for hh, var, qrows, krows in units]
        probs = [jnp.exp(s - jnp.max(s, axis=-1, keepdims=True)) for s in logits]
        sums = [jnp.sum(p, axis=-1, keepdims=True) for p in probs]
        outs = [_mm(p, v_scr[hh, krows, :]) for p, (hh, _, _, krows) in zip(probs, units)]
        for o, l, (hh, _, qrows, _) in zip(outs, sums, units):
            o_scr[hh, qrows, :] = o / l
        return carry

    lax.fori_loop(0, rows // NA_ROW_UNROLL, body, 0)
    o = jnp.concatenate([o_scr[0], o_scr[1]], axis=-1)
    y_ref[...] = (o * _silu(z_ref[...].astype(F32))).astype(y_ref.dtype)


def _na_bias_table(rpb, rows):
    kr = min(NA_ROWS, rows)
    c = jnp.arange(GRID_W)
    c0 = jnp.clip(c - NA_COLS // 2, 0, GRID_W - NA_COLS)
    in_win = (c[None, :] >= c0[:, None]) & (c[None, :] < c0[:, None] + NA_COLS)
    col_off = jnp.clip(c[None, :] - c[:, None], -(NA_COLS - 1), NA_COLS - 1) + NA_COLS - 1
    t = jnp.where(in_win, rpb[..., col_off], NEG)
    ro = jnp.arange(NA_ROWS)[:, None] + jnp.arange(kr)[None, :]
    tv = jnp.take(t, ro, axis=-3)
    tv = jnp.swapaxes(tv, -3, -2)
    return tv.reshape(*rpb.shape[:-2], NA_ROWS, GRID_W, kr * GRID_W).astype(F32)


def _natten(pm3, bias, qg, kg, layer):
    b, s, _ = pm3.shape
    pw = 2 * NA_DH
    npair = NA_HEADS // 2
    hw = NA_HEADS * NA_DH
    bias5 = bias.reshape(bias.shape[0], npair, 2, *bias.shape[2:])
    return pl.pallas_call(
        _na_kernel,
        grid=(npair, b),
        in_specs=[
            pl.BlockSpec((None, s, pw), lambda p, i: (i, 0, _OFF["b_qkv"] // pw + p)),
            pl.BlockSpec((None, s, pw), lambda p, i: (i, 0, (_OFF["b_qkv"] + hw) // pw + p)),
            pl.BlockSpec((None, s, pw), lambda p, i: (i, 0, (_OFF["b_qkv"] + 2 * hw) // pw + p)),
            pl.BlockSpec((None, s, pw), lambda p, i: (i, 0, _OFF["b_z"] // pw + p)),
            pl.BlockSpec((None, None) + bias5.shape[2:], lambda p, i: (layer, p, 0, 0, 0, 0)),
            pl.BlockSpec((1, pw), lambda p, i: (0, 0)),
            pl.BlockSpec((1, pw), lambda p, i: (0, 0)),
        ],
        out_specs=pl.BlockSpec((None, s, pw), lambda p, i: (i, 0, p)),
        out_shape=jax.ShapeDtypeStruct((b, s, BRANCH_W), BF16),
        scratch_shapes=[pltpu.VMEM((2, s, NA_DH), BF16), pltpu.VMEM((2, s, NA_DH), BF16),
                        pltpu.VMEM((2, s, NA_DH), BF16), pltpu.VMEM((2, s, NA_DH), F32)],
        compiler_params=_cparams(("parallel", "parallel")),
        name="natten",
    )(pm3, pm3, pm3, pm3, bias5, qg, kg)


INV_BASE = 8
PRE_BLK = 4 * BLK


def _inverse_level_masks():
    i = lax.broadcasted_iota(jnp.int32, (BLK, BLK), 0)
    j = lax.broadcasted_iota(jnp.int32, (BLK, BLK), 1)
    same = lambda size: (i >> int(math.log2(size))) == (j >> int(math.log2(size)))
    base = same(INV_BASE)
    joins = []
    size = INV_BASE
    while size < DN_CHUNK:
        joins.append(same(2 * size) & jnp.logical_not(same(size)))
        size *= 2
    return base, joins


def _tri_inverses(l_mats, eyes, level_masks):
    base, joins = level_masks
    ps = [jnp.where(base, -l, 0.0) for l in l_mats]
    ts = [eye + p for eye, p in zip(eyes, ps)]
    for _ in range(int(math.log2(INV_BASE)) - 1):
        ps = [_mm(p, p) for p in ps]
        ts = [t + _mm(t, p) for t, p in zip(ts, ps)]
    for join in joins:
        mids = [_mm(jnp.where(join, l, 0.0), t) for l, t in zip(l_mats, ts)]
        ts = [t - _mm(t, mid) for t, mid in zip(ts, mids)]
    return ts


DN_HW = DN_HEADS * LANES
PK_W, PK_QD, PK_KD, PK_QK = (i * DN_HW for i in range(4))


def _dn_pre_kernel(qkv_ref, sm_ref, cw_ref, lp_ref, uf_ref, ub_ref, pkf_ref, pkb_ref, gtf_ref, gtb_ref):
    n = pl.program_id(1)
    s_len = qkv_ref.shape[0]
    assert qkv_ref.dtype == BF16
    halo = 2 * SUBLANES
    n_sub = sm_ref.shape[0] // BLK
    chunks_per_blk = BLK // DN_CHUNK
    out_row = lax.broadcasted_iota(jnp.int32, (BLK, BLK + 2 * halo), 0)
    in_row = lax.broadcasted_iota(jnp.int32, (BLK, BLK + 2 * halo), 1)
    shifts = {j: jnp.where(in_row == out_row + (halo + j - DN_CONV // 2), 1.0, 0.0).astype(BF16)
              for j in range(DN_CONV) if j != DN_CONV // 2}
    dir_masks = [_chunk_masks(reverse=(d == 1)) for d in range(2)]
    level_masks = _inverse_level_masks()
    outs = ((uf_ref, pkf_ref, gtf_ref), (ub_ref, pkb_ref, gtb_ref))

    def l2n(x):
        return x * lax.rsqrt(jnp.sum(x * x, axis=-1, keepdims=True) + EPS)

    blocks = []
    for sub in range(n_sub):
        t0 = pl.multiple_of((n * n_sub + sub) * BLK, BLK)
        pstart = pl.multiple_of(jnp.maximum(t0 - halo, 0), halo)
        nstart = pl.multiple_of(jnp.minimum(t0 + BLK, s_len - halo), halo)
        prev = qkv_ref[pl.ds(pstart, halo), :]
        prev = jnp.where(t0 > 0, prev, jnp.zeros_like(prev))
        cur = qkv_ref[pl.ds(t0, BLK), :]
        nxt = qkv_ref[pl.ds(nstart, halo), :]
        nxt = jnp.where(t0 + BLK < s_len, nxt, jnp.zeros_like(nxt))
        xw = jnp.concatenate([prev, cur, nxt], axis=0)
        conv = cur.astype(F32) * cw_ref[DN_CONV // 2:DN_CONV // 2 + 1, :]
        for j, shift in shifts.items():
            conv = conv + jnp.dot(shift, xw, preferred_element_type=F32) * cw_ref[j:j + 1, :]
        conv = _silu(conv)

        rows = slice(sub * BLK, (sub + 1) * BLK)
        sm = sm_ref[rows, :]
        g_all = -jnp.exp(lp_ref[0:1, :]) * _softplus(sm + lp_ref[1:2, :])
        beta_all = _sigmoid(sm)
        per_dir = []
        for d in range(2):
            same, incl, strict, eye_b = dir_masks[d]
            gc = _mask_sum(incl, g_all)
            tot = _mask_sum(same, g_all)
            per_dir.append((incl, strict, _as_f32(eye_b), gc, gc.T, tot))
        blocks.append((sub, rows, conv, beta_all, per_dir))

    heads = []
    for sub, rows, conv, beta_all, per_dir in blocks:
        for h in range(DN_HEADS):
            q = l2n(conv[:, h * DN_DK:(h + 1) * DN_DK]) * (DN_DK ** -0.5)
            k = l2n(conv[:, (DN_HEADS + h) * DN_DK:(DN_HEADS + h + 1) * DN_DK])
            v = conv[:, 2 * DN_HEADS * DN_DK + h * DN_DV:2 * DN_HEADS * DN_DK + (h + 1) * DN_DV]
            heads.append((sub, rows, beta_all, per_dir, h, q, k, v))
    kks = [_mm_nt(hd[6], hd[6]) for hd in heads]
    qks = [_mm_nt(hd[5], hd[6]) for hd in heads]

    l_mats, rhss, eyes, slots = [], [], [], []
    for (sub, rows, beta_all, per_dir, h, q, k, v), kk, qk in zip(heads, kks, qks):
        hc = slice(h * LANES, (h + 1) * LANES)
        for d in range(2):
            incl, strict, eye_f, gc, gct, tot = per_dir[d]
            u_ref, pk_ref, gt_ref = outs[d]
            pk = lambda off: slice(off + h * LANES, off + (h + 1) * LANES)
            c = _L_AA + d * DN_HEADS + h
            gcol = gc[:, c:c + 1]
            grow = gct[c:c + 1, :]
            tcol = tot[:, c:c + 1]
            beta = beta_all[:, _L_AB + d * DN_HEADS + h:_L_AB + d * DN_HEADS + h + 1]
            decay = jnp.where(incl, jnp.exp(jnp.where(incl, gcol - grow, 0.0)), 0.0)
            l_mats.append(jnp.where(strict, beta * kk * decay, 0.0))
            egc = jnp.exp(gcol)
            rhss.append(jnp.concatenate([v * beta, k * (beta * egc)], axis=-1).astype(BF16))
            eyes.append(eye_f)
            slots.append((rows, h, d))
            pk_ref[rows, pk(PK_QD)] = (q * egc).astype(pk_ref.dtype)
            pk_ref[rows, pk(PK_KD)] = (k * jnp.exp(tcol - gcol)).astype(pk_ref.dtype)
            pk_ref[rows, pk(PK_QK)] = (qk * decay).astype(pk_ref.dtype)
            gtot = jnp.exp(tcol)
            for ci in range(chunks_per_blk):
                g0 = (sub * chunks_per_blk + ci) * SUBLANES
                gt_ref[g0:g0 + SUBLANES, hc] = jnp.broadcast_to(
                    gtot[ci * DN_CHUNK:ci * DN_CHUNK + SUBLANES, :], (SUBLANES, LANES))

    t_invs = _tri_inverses(l_mats, eyes, level_masks)
    sols = [_mm(t, rhs) for t, rhs in zip(t_invs, rhss)]
    for (rows, h, d), sol in zip(slots, sols):
        u_ref, pk_ref = outs[d][0], outs[d][1]
        u_ref[rows, h * LANES:(h + 1) * LANES] = sol[:, :DN_DV]
        pk_ref[rows, PK_W + h * LANES:PK_W + (h + 1) * LANES] = sol[:, DN_DV:].astype(pk_ref.dtype)


def _dn_pre(pm3, ps3, conv_w8, lane_params):
    b, s, _ = pm3.shape
    nblk = s // PRE_BLK
    gt_rows = PRE_BLK // DN_CHUNK * SUBLANES
    wq = DN_HEADS * (2 * DN_DK + DN_DV)
    hw = DN_HEADS * LANES
    tok = lambda i, j: (i, j, 0)
    big = lambda dt: jax.ShapeDtypeStruct((b, s, hw), dt)
    gts = jax.ShapeDtypeStruct((b, nblk * gt_rows, hw), F32)
    bs_tok = pl.BlockSpec((None, PRE_BLK, hw), tok)
    bs_gt = pl.BlockSpec((None, gt_rows, hw), tok)
    bs_pk = pl.BlockSpec((None, PRE_BLK, 4 * hw), tok)
    packed = jax.ShapeDtypeStruct((b, s, 4 * hw), BF16)
    return pl.pallas_call(
        _dn_pre_kernel,
        grid=(b, nblk),
        in_specs=[
            pl.BlockSpec((None, s, wq), lambda i, j: (i, 0, _OFF["a_qkv"] // wq)),
            pl.BlockSpec((None, PRE_BLK, LANES), tok),
            pl.BlockSpec((SUBLANES, wq), lambda i, j: (0, 0)),
            pl.BlockSpec((SUBLANES, LANES), lambda i, j: (0, 0)),
        ],
        out_specs=[bs_tok, bs_tok, bs_pk, bs_pk, bs_gt, bs_gt],
        out_shape=[big(F32), big(F32), packed, packed, gts, gts],
        compiler_params=_cparams(("parallel", "arbitrary")),
        name="dn_pre",
    )(pm3, ps3, conv_w8, lane_params)


SCAN_BLK = 4 * BLK


def _dn_scan_kernel(uf_ref, pkf_ref, gtf_ref, ub_ref, pkb_ref, gtb_ref, of_ref, ob_ref, st_scr):
    @pl.when(pl.program_id(1) == 0)
    def _():
        st_scr[...] = jnp.zeros_like(st_scr)

    per_blk = BLK // DN_CHUNK
    nchunk = SCAN_BLK // DN_CHUNK
    zeros_c = jnp.zeros((DN_CHUNK, DN_DV), F32)
    streams = ((uf_ref, pkf_ref, gtf_ref, of_ref, range(nchunk)),
               (ub_ref, pkb_ref, gtb_ref, ob_ref, range(nchunk - 1, -1, -1)))
    chains = [(d, h) + streams[d] for d in range(2) for h in range(DN_HEADS)]
    states = [st_scr[d * DN_HEADS + h] for d, h, *_ in chains]
    for step in range(nchunk):
        rs, v_pads = [], []
        for (d, h, u_ref, pk_ref, gt_ref, o_ref, order), state in zip(chains, states):
            rows = slice(order[step] * DN_CHUNK, (order[step] + 1) * DN_CHUNK)
            w = pk_ref[rows, PK_W + h * LANES:PK_W + (h + 1) * LANES]
            qd = pk_ref[rows, PK_QD + h * LANES:PK_QD + (h + 1) * LANES]
            rs.append(_mm(jnp.concatenate([w, qd], axis=0), state))
        for (d, h, u_ref, pk_ref, gt_ref, o_ref, order), r in zip(chains, rs):
            ci = order[step]
            rows = slice(ci * DN_CHUNK, (ci + 1) * DN_CHUNK)
            parts = [zeros_c] * per_blk
            parts[ci % per_blk] = u_ref[rows, h * LANES:(h + 1) * LANES] - r[:DN_CHUNK]
            v_pads.append(jnp.concatenate(parts, axis=0))
        new_states = []
        for (d, h, u_ref, pk_ref, gt_ref, o_ref, order), r, v_pad, state in zip(chains, rs, v_pads, states):
            hc = slice(h * LANES, (h + 1) * LANES)
            ci = order[step]
            rows = slice(ci * DN_CHUNK, (ci + 1) * DN_CHUNK)
            blk_rows = slice((ci // per_blk) * BLK, (ci // per_blk + 1) * BLK)
            o_ref[rows, hc] = r[DN_CHUNK:] + _mm(pk_ref[rows, PK_QK + h * LANES:PK_QK + (h + 1) * LANES], v_pad)
            gt = gt_ref[ci * SUBLANES:ci * SUBLANES + 1, hc]
            kd = pk_ref[blk_rows, PK_KD + h * LANES:PK_KD + (h + 1) * LANES]
            new_states.append(state * gt + _mm_tn(kd, v_pad))
        states = new_states
    for (d, h, *_), state in zip(chains, states):
        st_scr[d * DN_HEADS + h] = state


def _dn_scan(pre):
    uf, ub, pkf, pkb, gtf, gtb = pre
    b, s, hw = uf.shape
    nblk = s // SCAN_BLK
    gt_rows = SCAN_BLK // DN_CHUNK * SUBLANES
    fwd = lambda i, j: (i, j, 0)
    bwd = lambda i, j: (i, nblk - 1 - j, 0)
    def specs(imap):
        return [pl.BlockSpec((None, SCAN_BLK, hw), imap), pl.BlockSpec((None, SCAN_BLK, 4 * hw), imap),
                pl.BlockSpec((None, gt_rows, hw), imap)]
    return pl.pallas_call(
        _dn_scan_kernel,
        grid=(b, nblk),
        in_specs=specs(fwd) + specs(bwd),
        out_specs=[pl.BlockSpec((None, SCAN_BLK, hw), fwd), pl.BlockSpec((None, SCAN_BLK, hw), bwd)],
        out_shape=[jax.ShapeDtypeStruct((b, s, hw), F32)] * 2,
        scratch_shapes=[pltpu.VMEM((2 * DN_HEADS, DN_DK, DN_DV), F32)],
        compiler_params=_cparams(("parallel", "arbitrary")),
        name="dn_scan",
    )(uf, pkf, gtf, ub, pkb, gtb)


ML_AUG = 2 * LANES
ML_GROUP = 4
ML_BLK = 4 * BLK


def _ml_kernel(qf_ref, kf_ref, vf_ref, smf_ref, qb_ref, kb_ref, vb_ref, smb_ref, lp_ref,
               hf_ref, hb_ref, c_scr, m_scr):
    @pl.when(pl.program_id(1) == 0)
    def _():
        c_scr[...] = jnp.zeros_like(c_scr)
        m_scr[...] = jnp.zeros_like(m_scr)

    nchunk = BLK // ML_CHUNK
    ones_col = jnp.ones((BLK, LANES), BF16)
    zeros_aug = jnp.zeros((ML_CHUNK, ML_AUG), BF16)
    streams = ((qf_ref, kf_ref, vf_ref, smf_ref, hf_ref, range(nchunk)),
               (qb_ref, kb_ref, vb_ref, smb_ref, hb_ref, range(nchunk - 1, -1, -1)))
    lanes = lambda col: jnp.broadcast_to(col, (col.shape[0], LANES))
    n_sub = smf_ref.shape[0] // BLK
    units = [(d, p if d == 0 else n_sub - 1 - p) for p in range(n_sub) for d in range(2)]
    for d, sub in units:
        _ml_block(d, sub, streams[d], lp_ref, c_scr, m_scr, lanes, ones_col, zeros_aug, nchunk)


def _ml_block(d, sub, stream, lp_ref, c_scr, m_scr, lanes, ones_col, zeros_aug, nchunk):
    q_ref, k_ref, v_ref, sm_ref, h_ref, order = stream
    blk_rows = slice(sub * BLK, (sub + 1) * BLK)
    same, incl, _, _ = _chunk_masks(reverse=(d == 1))
    sm = sm_ref[blk_rows, :]
    ig_all = sm + lp_ref[0:1, :]
    x = sm + lp_ref[1:2, :]
    lf_all = jnp.minimum(x, 0.0) - jnp.log(1.0 + jnp.exp(-jnp.abs(x)))
    lf_all = pltpu.roll(lf_all, LANES - (_L_DF - _L_DI), 1)
    bc_all = _mask_sum(incl, lf_all)
    tot_all = _mask_sum(same, lf_all)
    a_all = ig_all - bc_all
    mwa_all = jnp.concatenate(
        [jnp.broadcast_to(jnp.max(a_all[ci * ML_CHUNK:(ci + 1) * ML_CHUNK], axis=0, keepdims=True),
                          (ML_CHUNK, LANES)) for ci in range(nchunk)], axis=0)
    a_t, w_all, mw_all = a_all.T, jnp.exp(a_all - mwa_all), tot_all + mwa_all

    for chains in [[(d, h) for h in range(g, g + ML_GROUP)] for g in range(0, ML_HEADS, ML_GROUP)]:
        ins = []
        for _, h in chains:
            q = q_ref[blk_rows, h * ML_DK:(h + 1) * ML_DK].astype(BF16)
            k = k_ref[blk_rows, h * ML_DK:(h + 1) * ML_DK].astype(F32) * (ML_DK ** -0.5)
            v_aug = jnp.concatenate([v_ref[blk_rows, h * ML_DV:(h + 1) * ML_DV].astype(BF16), ones_col], axis=-1)
            ins.append((q, k, v_aug))
        qks = [_mm_nt(q, k) for q, k, _ in ins]

        mids = []
        for (d, h), (q, k, v_aug), qk in zip(chains, ins, qks):
            c = _L_DI + d * ML_HEADS + h
            b_l = lanes(bc_all[:, c:c + 1])
            dlog = jnp.where(incl, b_l + a_t[c:c + 1, :], NEG)
            m_intra = lanes(jnp.max(dlog, axis=-1, keepdims=True))
            s_intra = qk * jnp.exp(dlog - m_intra)
            wk = (k * lanes(w_all[:, c:c + 1])[:, :ML_DK]).astype(BF16)
            mids.append((b_l, m_intra, s_intra.astype(BF16), wk))
        p_intras = [_mm(s_b, v_aug) for (_, _, s_b, _), (_, _, v_aug) in zip(mids, ins)]
        kvs = []
        for (_, _, _, wk), (_, _, v_aug) in zip(mids, ins):
            per_chunk = []
            for ci in range(nchunk):
                parts = [zeros_aug] * nchunk
                parts[ci] = v_aug[ci * ML_CHUNK:(ci + 1) * ML_CHUNK]
                per_chunk.append(_mm_tn(wk, jnp.concatenate(parts, axis=0)))
            kvs.append(per_chunk)

        c_sts = [c_scr[d * ML_HEADS + h] for d, h in chains]
        m_sts = [m_scr[d * ML_HEADS + h][0:1, :] for d, h in chains]
        for step in range(nchunk):
            qcs = []
            ci = order[step]
            for (q, _, _), c_st in zip(ins, c_sts):
                qcs.append(_mm(q[ci * ML_CHUNK:(ci + 1) * ML_CHUNK], c_st))
            for idx, (_, h) in enumerate(chains):
                b_l, m_intra, _, _ = mids[idx]
                c = _L_DI + d * ML_HEADS + h
                rows = slice(ci * ML_CHUNK, (ci + 1) * ML_CHUNK)
                out_rows = slice(sub * BLK + ci * ML_CHUNK, sub * BLK + (ci + 1) * ML_CHUNK)
                r8 = slice(ci * ML_CHUNK, ci * ML_CHUNK + SUBLANES)
                m_st, c_st, qc = m_sts[idx], c_sts[idx], qcs[idx]
                m_inter = b_l[rows] + m_st
                m_i = jnp.maximum(m_intra[rows], m_inter)
                f_i = jnp.exp(m_intra[rows] - m_i)
                inter = jnp.exp(m_inter - m_i)
                both = (jnp.concatenate([inter, inter], axis=-1) * qc
                        + jnp.concatenate([f_i, f_i], axis=-1) * p_intras[idx][rows])
                numer, denom = both[:, :ML_DV], both[:, ML_DV:]
                h_ref[out_rows, h * ML_DV:(h + 1) * ML_DV] = numer / jnp.maximum(jnp.abs(denom), jnp.exp(-m_i))
                tot_c = lanes(tot_all[r8, c:c + 1])[0:1]
                mw_c = lanes(mw_all[r8, c:c + 1])[0:1]
                m_new = jnp.maximum(tot_c + m_st, mw_c)
                dec = jnp.exp(tot_c + m_st - m_new)
                gain = jnp.exp(mw_c - m_new)
                c_sts[idx] = (jnp.concatenate([dec, dec], axis=-1) * c_st
                              + jnp.concatenate([gain, gain], axis=-1) * kvs[idx][ci])
                m_sts[idx] = m_new
        for idx, (d, h) in enumerate(chains):
            c_scr[d * ML_HEADS + h] = c_sts[idx]
            m_scr[d * ML_HEADS + h] = jnp.broadcast_to(m_sts[idx], (SUBLANES, LANES))


def _mlstm(pm3, ps3, lane_params):
    b, s, _ = pm3.shape
    nblk = s // ML_BLK
    qw = ML_HEADS * ML_DK
    vw = ML_HEADS * ML_DV
    def specs(tmap):
        blk = lambda j: tmap(j)
        return [
            pl.BlockSpec((None, ML_BLK, qw), lambda i, j: (i, blk(j), _OFF["d_q"] // qw)),
            pl.BlockSpec((None, ML_BLK, qw), lambda i, j: (i, blk(j), _OFF["d_k"] // qw)),
            pl.BlockSpec((None, ML_BLK, vw), lambda i, j: (i, blk(j), _OFF["d_v"] // vw)),
            pl.BlockSpec((None, ML_BLK, LANES), lambda i, j: (i, blk(j), 0)),
        ]
    fwd = lambda j: j
    bwd = lambda j: nblk - 1 - j
    return pl.pallas_call(
        _ml_kernel,
        grid=(b, nblk),
        in_specs=specs(fwd) + specs(bwd) + [pl.BlockSpec((SUBLANES, LANES), lambda i, j: (0, 0))],
        out_specs=[pl.BlockSpec((None, ML_BLK, vw), lambda i, j: (i, j, 0)),
                   pl.BlockSpec((None, ML_BLK, vw), lambda i, j: (i, nblk - 1 - j, 0))],
        out_shape=[jax.ShapeDtypeStruct((b, s, vw), F32)] * 2,
        scratch_shapes=[pltpu.VMEM((2 * ML_HEADS, ML_DK, ML_AUG), F32),
                        pltpu.VMEM((2 * ML_HEADS, SUBLANES, LANES), F32)],
        compiler_params=_cparams(("parallel", "arbitrary")),
        name="mlstm",
    )(pm3, pm3, pm3, ps3, pm3, pm3, pm3, ps3, lane_params)


def _merge_kernel(x_ref, af_ref, ab_ref, df_ref, db_ref, yb_ref, yc_ref, az_ref, dz_ref, do_ref, gl_ref,
                  ag_ref, dg_ref, wb_ref, wo_ref, o_ref):
    d = x_ref.shape[-1]

    def head_rms(x, g):
        outs = []
        for h in range(x.shape[-1] // LANES):
            xh = x[:, h * LANES:(h + 1) * LANES]
            ms = jnp.mean(xh * xh, axis=-1, keepdims=True)
            outs.append(xh * lax.rsqrt(ms + EPS) * g)
        return jnp.concatenate(outs, axis=-1)

    ya = head_rms(af_ref[...] + ab_ref[...], ag_ref[...]) * _silu(az_ref[...].astype(F32))
    yd = _sigmoid(do_ref[...].astype(F32)) * head_rms(df_ref[...] + db_ref[...], dg_ref[...])
    yd = yd * _silu(dz_ref[...].astype(F32))
    twice = None
    for i, y in enumerate((ya.astype(BF16), yb_ref[...], yc_ref[...], yd.astype(BF16))):
        proj = jnp.dot(y, wb_ref[i], preferred_element_type=F32)
        term = proj + jnp.tanh(0.5 * gl_ref[:, i * d:(i + 1) * d].astype(F32)) * proj
        twice = term if twice is None else twice + term
    merged = (0.5 * twice).astype(BF16)
    o_ref[...] = x_ref[...] + jnp.dot(merged, wo_ref[...], preferred_element_type=F32)


def _merge(x2d, af, ab, df, db, yb, yc, pm2, ag, dg, wb, wo, layer, tm=512):
    m, d = x2d.shape
    gw = N_BRANCH * d
    w = BRANCH_W
    tok = pl.BlockSpec((tm, w), lambda i: (i, 0))
    col = lambda name: pl.BlockSpec((tm, w), lambda i: (i, _OFF[name] // w))
    vec = pl.BlockSpec((1, LANES), lambda i: (0, 0))
    return pl.pallas_call(
        _merge_kernel,
        grid=(m // tm,),
        in_specs=[
            pl.BlockSpec((tm, d), lambda i: (i, 0)),
            tok, tok, tok, tok, tok, tok,
            col("a_z"), col("d_z"), col("d_o"),
            pl.BlockSpec((tm, gw), lambda i: (i, _OFF["gate"] // gw)),
            vec, vec,
            pl.BlockSpec((None, N_BRANCH, w, d), lambda i: (layer, 0, 0, 0)),
            pl.BlockSpec((None, d, d), lambda i: (layer, 0, 0)),
        ],
        out_specs=pl.BlockSpec((tm, d), lambda i: (i, 0)),
        out_shape=jax.ShapeDtypeStruct((m, d), F32),
        compiler_params=_cparams(("parallel",)),
        name="merge",
    )(x2d, af, ab, df, db, yb, yc, pm2, pm2, pm2, pm2, ag, dg, wb, wo)


def _rope_lane_tables(s):
    t = jnp.arange(s)
    row = (t // GRID_W).astype(F32)
    col = (t % GRID_W).astype(F32)
    m = GA_DH // 4
    inv = ROPE_THETA ** (-jnp.arange(m, dtype=F32) / m)
    ar = row[:, None] * inv
    ac = col[:, None] * inv
    cos_t = jnp.concatenate([jnp.cos(ar), jnp.cos(ar), jnp.cos(ac), jnp.cos(ac)], axis=-1)
    sin_t = jnp.concatenate([-jnp.sin(ar), jnp.sin(ar), -jnp.sin(ac), jnp.sin(ac)], axis=-1)
    return cos_t.astype(F32), sin_t.astype(F32)


def _lane_tiles(rows):
    padded = []
    for off, vals in rows:
        vals = vals.reshape(vals.shape[0], 1, -1).astype(F32)
        padded.append(jnp.pad(vals, ((0, 0), (0, 0), (off, LANES - off - vals.shape[-1]))))
    tiles = jnp.concatenate(padded, axis=1)
    return jnp.pad(tiles, ((0, 0), (0, SUBLANES - len(rows)), (0, 0)))


def kernel(x, norm_g, w_in, conv_a, dn_a_log, dn_dt_bias, dn_norm_g, na_q_norm, na_k_norm, na_rpb,
           ga_q_norm, ga_k_norm, ml_i_bias, ml_f_bias, ml_norm_g, w_branch, w_out):
    b, s, d = x.shape
    depth = w_in.shape[0]
    hw = BRANCH_W
    cos_t, sin_t = _rope_lane_tables(s)
    w_main = jnp.concatenate([w_in[:, :, o:o + wd] for _, o, wd in _MAIN_SEGS], axis=2).astype(BF16)
    w_small = jnp.pad(jnp.concatenate([w_in[:, :, o:o + 8] for o in _SMALL_SRC], axis=2),
                      ((0, 0), (0, 0), (0, LANES - 8 * len(_SMALL_SRC)))).astype(BF16)
    conv8 = jnp.pad(conv_a.astype(F32), ((0, 0), (0, SUBLANES - DN_CONV), (0, 0)))
    dn_lp = _lane_tiles([(_L_AA, dn_a_log), (_L_AA, dn_dt_bias)])
    ml_lp = _lane_tiles([(_L_DI, ml_i_bias), (_L_DF, ml_f_bias)])
    na_bias = _na_bias_table(na_rpb, s // GRID_W)
    na_qg = jnp.tile(na_q_norm, (1, 2)).reshape(depth, 1, 2 * NA_DH)
    na_kg = jnp.tile(na_k_norm, (1, 2)).reshape(depth, 1, 2 * NA_DH)
    wb_bf, wo_bf = w_branch.astype(BF16), w_out.astype(BF16)

    x2 = x.reshape(b * s, d)
    for l in range(depth):
        pm2, ps2 = _inproj(x2, norm_g[l].reshape(1, d), w_main, w_small, l)
        pm3 = pm2.reshape(b, s, N_MAIN)
        ps3 = ps2.reshape(b, s, LANES)
        o_af, o_ab = _dn_scan(_dn_pre(pm3, ps3, conv8[l], dn_lp[l]))
        h_df, h_db = _mlstm(pm3, ps3, ml_lp[l])
        yb = _natten(pm3, na_bias, na_qg[l], na_kg[l], l)
        yc = _gqa(pm3, cos_t, sin_t, ga_q_norm[l].reshape(1, GA_DH), ga_k_norm[l].reshape(1, GA_DH))
        x2 = _merge(x2, o_af.reshape(b * s, hw), o_ab.reshape(b * s, hw), h_df.reshape(b * s, hw),
                    h_db.reshape(b * s, hw), yb.reshape(b * s, hw), yc.reshape(b * s, hw), pm2,
                    dn_norm_g[l].reshape(1, LANES), ml_norm_g[l].reshape(1, LANES), wb_bf, wo_bf, l)
    return x2.reshape(b, s, d)
```

```python
import functools
import math

import jax
import jax.numpy as jnp
from jax import lax
from jax.experimental import pallas as pl
from jax.experimental.pallas import tpu as pltpu

F32 = jnp.float32
BF16 = jnp.bfloat16

D_MODEL = 1024
GRID_W = 64
N_BRANCH = 4
BRANCH_W = 512
EPS = 1e-6
DN_HEADS, DN_DK, DN_DV, DN_CONV, DN_CHUNK = 4, 128, 128, 5, 128
NA_HEADS, NA_DH, NA_ROWS, NA_COLS = 8, 64, 8, 16
GA_HEADS, GA_KV_HEADS, GA_DH = 4, 2, 128
ROPE_THETA = 10000.0
ML_HEADS, ML_DK, ML_DV, ML_CHUNK = 4, 64, 128, 128

LANES = 128
SUBLANES = 8
VMEM_LIMIT_BYTES = 56 * 1024 * 1024

_O_A_QKV, _O_A_A, _O_A_B, _O_A_Z = 0, 1536, 1544, 1552
_O_B_QKV, _O_B_Z = 2064, 3600
_O_C_Q, _O_C_K, _O_C_V, _O_C_Z = 4112, 4624, 4880, 5136
_O_D_Q, _O_D_K, _O_D_V, _O_D_I, _O_D_F, _O_D_O, _O_D_Z = 5648, 5904, 6160, 6672, 6680, 6688, 7200
_O_GATE = 7712
_MAIN_SEGS = (
    ("a_qkv", _O_A_QKV, 1536), ("b_qkv", _O_B_QKV, 1536), ("a_z", _O_A_Z, 512), ("b_z", _O_B_Z, 512),
    ("gate", _O_GATE, 4096), ("c_q", _O_C_Q, 512), ("c_k", _O_C_K, 256), ("c_v", _O_C_V, 256),
    ("c_z", _O_C_Z, 512), ("d_q", _O_D_Q, 256), ("d_k", _O_D_K, 256), ("d_v", _O_D_V, 512),
    ("d_o", _O_D_O, 512), ("d_z", _O_D_Z, 512),
)
_OFF = {}
_o = 0
for _name, _src, _w in _MAIN_SEGS:
    _OFF[_name] = _o
    _o += _w
N_MAIN = _o
_SMALL_SRC = (_O_A_A, _O_A_B, _O_D_I, _O_D_F)
_L_AA, _L_AB, _L_DI, _L_DF = 0, 8, 16, 24

P_DTYPE = BF16
BLK = 128
NEG = -1e30


def _cparams(sem):
    return pltpu.CompilerParams(dimension_semantics=sem, vmem_limit_bytes=VMEM_LIMIT_BYTES)


def _sigmoid(x):
    return 0.5 * jnp.tanh(0.5 * x) + 0.5


def _silu(x):
    return x * _sigmoid(x)


def _softplus(x):
    return jnp.maximum(x, 0.0) + jnp.log(1.0 + jnp.exp(-jnp.abs(x)))


def _mm(a, b):
    return jnp.dot(a.astype(BF16), b.astype(BF16), preferred_element_type=F32)


def _mm_nt(a, b):
    return lax.dot_general(a.astype(BF16), b.astype(BF16), (((1,), (1,)), ((), ())),
                           preferred_element_type=F32)


def _mm_tn(a, b):
    return lax.dot_general(a.astype(BF16), b.astype(BF16), (((0,), (0,)), ((), ())),
                           preferred_element_type=F32)


def _mask_sum(mask, x):
    m = jnp.where(mask, 1.0, 0.0).astype(BF16)
    x1 = x.astype(BF16)
    r1 = x - x1.astype(F32)
    x2 = r1.astype(BF16)
    x3 = (r1 - x2.astype(F32)).astype(BF16)
    dot = lambda v: jnp.dot(m, v, preferred_element_type=F32)
    return dot(x1) + (dot(x2) + dot(x3))


def _chunk_masks(reverse):
    i = lax.broadcasted_iota(jnp.int32, (BLK, BLK), 0)
    j = lax.broadcasted_iota(jnp.int32, (BLK, BLK), 1)
    shift = int(math.log2(DN_CHUNK))
    same = (i >> shift) == (j >> shift)
    if reverse:
        incl = same & (j >= i)
        strict = same & (j > i)
    else:
        incl = same & (j <= i)
        strict = same & (j < i)
    return same, incl, strict, (i == j)


def _as_f32(mask):
    return jnp.where(mask, 1.0, 0.0).astype(F32)


def _inproj_kernel(x_ref, g_ref, w_ref, ws_ref, pm_ref, ps_ref, h_scr):
    @pl.when(pl.program_id(1) == 0)
    def _():
        x = x_ref[...]
        ms = jnp.mean(x * x, axis=-1, keepdims=True)
        h = (x * lax.rsqrt(ms + EPS) * g_ref[...]).astype(BF16)
        h_scr[...] = h
        ps_ref[...] = jnp.dot(h, ws_ref[...], preferred_element_type=F32)

    pm_ref[...] = jnp.dot(h_scr[...], w_ref[...], preferred_element_type=F32).astype(pm_ref.dtype)


def _inproj(x2d, g, w_main, w_small, layer, tm=1024, tn=N_MAIN // 4):
    m, d = x2d.shape
    tm = min(tm, m)
    return pl.pallas_call(
        _inproj_kernel,
        grid=(m // tm, N_MAIN // tn),
        in_specs=[
            pl.BlockSpec((tm, d), lambda i, j: (i, 0)),
            pl.BlockSpec((1, d), lambda i, j: (0, 0)),
            pl.BlockSpec((None, d, tn), lambda i, j: (layer, 0, j)),
            pl.BlockSpec((None, d, LANES), lambda i, j: (layer, 0, 0)),
        ],
        out_specs=[
            pl.BlockSpec((tm, tn), lambda i, j: (i, j)),
            pl.BlockSpec((tm, LANES), lambda i, j: (i, 0)),
        ],
        out_shape=[jax.ShapeDtypeStruct((m, N_MAIN), P_DTYPE), jax.ShapeDtypeStruct((m, LANES), F32)],
        scratch_shapes=[pltpu.VMEM((tm, d), BF16)],
        compiler_params=_cparams(("parallel", "arbitrary")),
        name="inproj",
    )(x2d, g, w_main, w_small)


GQA_TQ = 512


def _gqa_kernel(q_ref, k_ref, v_ref, z_ref, cos_ref, sin_ref, qg_ref, kg_ref, y_ref,
                q_scr, k_scr, v_scr, s0_scr, s1_scr, p0_scr, p1_scr, l0_scr, l1_scr):
    def norm_rope(x, g, cos, sin):
        ms = jnp.mean(x * x, axis=-1, keepdims=True)
        xn = x * lax.rsqrt(ms + EPS) * g
        lane = lax.broadcasted_iota(jnp.int32, xn.shape, 1)
        partner = jnp.where((lane & 63) < 32, pltpu.roll(xn, LANES - 32, 1), pltpu.roll(xn, 32, 1))
        return xn * cos + partner * sin

    s_len = k_ref.shape[0]
    group = GA_HEADS // GA_KV_HEADS
    k_scr[...] = norm_rope(k_ref[...].astype(F32), kg_ref[...], cos_ref[...], sin_ref[...]).astype(BF16)
    v_scr[...] = v_ref[...].astype(BF16)
    assert group == 2
    scale = GA_DH ** -0.5
    n_blk = s_len // GQA_TQ
    s_bufs, p_bufs, l_bufs = (s0_scr, s1_scr), (p0_scr, p1_scr), (l0_scr, l1_scr)

    def stacked(head, blk):
        return pl.ds(pl.multiple_of(head * s_len + blk * GQA_TQ, GQA_TQ), GQA_TQ)

    def prep(head, blk):
        rows = pl.ds(pl.multiple_of(blk * GQA_TQ, GQA_TQ), GQA_TQ)
        q = norm_rope(q_ref[rows, head * GA_DH:(head + 1) * GA_DH].astype(F32), qg_ref[...],
                      cos_ref[rows, :], sin_ref[rows, :])
        q_scr[stacked(head, blk), :] = q.astype(BF16)

    def logits(head, blk):
        s_bufs[head][...] = _mm_nt(q_scr[stacked(head, blk), :], k_scr[...]) * scale

    def softmax(slot):
        s = s_bufs[slot][...]
        p = jnp.exp(s - jnp.max(s, axis=-1, keepdims=True))
        l_bufs[slot][...] = jnp.broadcast_to(jnp.sum(p, axis=-1, keepdims=True), (GQA_TQ, GA_DH))
        p_bufs[slot][...] = p.astype(BF16)

    def weighted(head, blk):
        rows = pl.ds(pl.multiple_of(blk * GQA_TQ, GQA_TQ), GQA_TQ)
        cols = slice(head * GA_DH, (head + 1) * GA_DH)
        o = _mm(p_bufs[head][...], v_scr[...]) / l_bufs[head][...]
        y_ref[rows, cols] = (o * _silu(z_ref[rows, cols].astype(F32))).astype(y_ref.dtype)

    prep(0, 0)
    prep(1, 0)
    prep(0, 1)
    logits(0, 0)
    logits(1, 0)
    softmax(0)

    def body(j, carry):
        logits(0, j + 1)
        softmax(1)
        weighted(0, j)
        prep(1, j + 1)
        logits(1, j + 1)
        softmax(0)
        weighted(1, j)
        prep(0, jnp.minimum(j + 2, n_blk - 1))
        return carry

    lax.fori_loop(0, n_blk - 1, body, 0)
    softmax(1)
    weighted(0, n_blk - 1)
    weighted(1, n_blk - 1)


def _gqa(pm3, cos_t, sin_t, qg, kg):
    b, s, _ = pm3.shape
    gw = (GA_HEADS // GA_KV_HEADS) * GA_DH
    return pl.pallas_call(
        _gqa_kernel,
        grid=(b, GA_KV_HEADS),
        in_specs=[
            pl.BlockSpec((None, s, gw), lambda i, j: (i, 0, _OFF["c_q"] // gw + j)),
            pl.BlockSpec((None, s, GA_DH), lambda i, j: (i, 0, _OFF["c_k"] // GA_DH + j)),
            pl.BlockSpec((None, s, GA_DH), lambda i, j: (i, 0, _OFF["c_v"] // GA_DH + j)),
            pl.BlockSpec((None, s, gw), lambda i, j: (i, 0, _OFF["c_z"] // gw + j)),
            pl.BlockSpec((s, GA_DH), lambda i, j: (0, 0)),
            pl.BlockSpec((s, GA_DH), lambda i, j: (0, 0)),
            pl.BlockSpec((1, GA_DH), lambda i, j: (0, 0)),
            pl.BlockSpec((1, GA_DH), lambda i, j: (0, 0)),
        ],
        out_specs=pl.BlockSpec((None, s, gw), lambda i, j: (i, 0, j)),
        out_shape=jax.ShapeDtypeStruct((b, s, BRANCH_W), BF16),
        scratch_shapes=[pltpu.VMEM((gw // GA_DH * s, GA_DH), BF16), pltpu.VMEM((s, GA_DH), BF16),
                        pltpu.VMEM((s, GA_DH), BF16),
                        pltpu.VMEM((GQA_TQ, s), F32), pltpu.VMEM((GQA_TQ, s), F32),
                        pltpu.VMEM((GQA_TQ, s), BF16), pltpu.VMEM((GQA_TQ, s), BF16),
                        pltpu.VMEM((GQA_TQ, GA_DH), F32), pltpu.VMEM((GQA_TQ, GA_DH), F32)],
        compiler_params=_cparams(("parallel", "parallel")),
        name="gqa",
    )(pm3, pm3, pm3, pm3, cos_t, sin_t, qg, kg)


NA_ROW_UNROLL = 16


def _na_kernel(q_ref, k_ref, v_ref, z_ref, bias_ref, qg_ref, kg_ref, y_ref, q_scr, k_scr, v_scr, o_scr):
    s_len = q_ref.shape[0]
    rows = s_len // GRID_W
    kr = min(NA_ROWS, rows)
    hi = lax.broadcasted_iota(jnp.int32, (2 * NA_DH, 2 * NA_DH), 0) >= NA_DH
    hj = lax.broadcasted_iota(jnp.int32, (2 * NA_DH, 2 * NA_DH), 1) >= NA_DH
    same_head = jnp.where(hi == hj, 1.0, 0.0).astype(BF16)

    def rms_pair(x, g):
        x2 = x * x
        x2_hi = x2.astype(BF16)
        x2_lo = (x2 - x2_hi.astype(F32)).astype(BF16)
        ssq = (jnp.dot(x2_hi, same_head, preferred_element_type=F32)
               + jnp.dot(x2_lo, same_head, preferred_element_type=F32))
        return x * lax.rsqrt(ssq * (1.0 / NA_DH) + EPS) * g

    scale = NA_DH ** -0.5
    assert math.log2(scale).is_integer()
    qn = rms_pair(q_ref[...].astype(F32), qg_ref[...]) * scale
    kn = rms_pair(k_ref[...].astype(F32), kg_ref[...])
    vf = v_ref[...].astype(F32)
    for hh in range(2):
        cols = slice(hh * NA_DH, (hh + 1) * NA_DH)
        q_scr[hh] = qn[:, cols].astype(BF16)
        k_scr[hh] = kn[:, cols].astype(BF16)
        v_scr[hh] = vf[:, cols].astype(BF16)
    nkeys = kr * GRID_W

    def body(it, carry):
        units = []
        for u in range(NA_ROW_UNROLL):
            r = it * NA_ROW_UNROLL + u
            r0 = jnp.clip(r - kr // 2, 0, rows - kr)
            var = r0 - r + (NA_ROWS - 1)
            qrows = pl.ds(pl.multiple_of(r * GRID_W, GRID_W), GRID_W)
            krows = pl.ds(pl.multiple_of(r0 * GRID_W, GRID_W), nkeys)
            units += [(hh, var, qrows, krows) for hh in range(2)]
        logits = [_mm_nt(q_scr[hh, qrows, :], k_scr[hh, krows, :]) + bias_ref[hh, var]
                  for hh, var, qrows, krows in units]
        probs = [jnp.exp(s - jnp.max(s, axis=-1, keepdims=True)) for s in logits]
        sums = [jnp.sum(p, axis=-1, keepdims=True) for p in probs]
        outs = [_mm(p, v_scr[hh, krows, :]) for p, (hh, _, _, krows) in zip(probs, units)]
        for o, l, (hh, _, qrows, _) in zip(outs, sums, units):
            o_scr[hh, qrows, :] = o / l
        return carry

    lax.fori_loop(0, rows // NA_ROW_UNROLL, body, 0)
    o = jnp.concatenate([o_scr[0], o_scr[1]], axis=-1)
    y_ref[...] = (o * _silu(z_ref[...].astype(F32))).astype(y_ref.dtype)


def _na_bias_table(rpb, rows):
    kr = min(NA_ROWS, rows)
    c = jnp.arange(GRID_W)
    c0 = jnp.clip(c - NA_COLS // 2, 0, GRID_W - NA_COLS)
    in_win = (c[None, :] >= c0[:, None]) & (c[None, :] < c0[:, None] + NA_COLS)
    col_off = jnp.clip(c[None, :] - c[:, None], -(NA_COLS - 1), NA_COLS - 1) + NA_COLS - 1
    t = jnp.where(in_win, rpb[..., col_off], NEG)
    ro = jnp.arange(NA_ROWS)[:, None] + jnp.arange(kr)[None, :]
    tv = jnp.take(t, ro, axis=-3)
    tv = jnp.swapaxes(tv, -3, -2)
    return tv.reshape(*rpb.shape[:-2], NA_ROWS, GRID_W, kr * GRID_W).astype(F32)


def _natten(pm3, bias, qg, kg, layer):
    b, s, _ = pm3.shape
    pw = 2 * NA_DH
    npair = NA_HEADS // 2
    hw = NA_HEADS * NA_DH
    bias5 = bias.reshape(bias.shape[0], npair, 2, *bias.shape[2:])
    return pl.pallas_call(
        _na_kernel,
        grid=(npair, b),
        in_specs=[
            pl.BlockSpec((None, s, pw), lambda p, i: (i, 0, _OFF["b_qkv"] // pw + p)),
            pl.BlockSpec((None, s, pw), lambda p, i: (i, 0, (_OFF["b_qkv"] + hw) // pw + p)),
            pl.BlockSpec((None, s, pw), lambda p, i: (i, 0, (_OFF["b_qkv"] + 2 * hw) // pw + p)),
            pl.BlockSpec((None, s, pw), lambda p, i: (i, 0, _OFF["b_z"] // pw + p)),
            pl.BlockSpec((None, None) + bias5.shape[2:], lambda p, i: (layer, p, 0, 0, 0, 0)),
            pl.BlockSpec((1, pw), lambda p, i: (0, 0)),
            pl.BlockSpec((1, pw), lambda p, i: (0, 0)),
        ],
        out_specs=pl.BlockSpec((None, s, pw), lambda p, i: (i, 0, p)),
        out_shape=jax.ShapeDtypeStruct((b, s, BRANCH_W), BF16),
        scratch_shapes=[pltpu.VMEM((2, s, NA_DH), BF16), pltpu.VMEM((2, s, NA_DH), BF16),
                        pltpu.VMEM((2, s, NA_DH), BF16), pltpu.VMEM((2, s, NA_DH), F32)],
        compiler_params=_cparams(("parallel", "parallel")),
        name="natten",
    )(pm3, pm3, pm3, pm3, bias5, qg, kg)


INV_BASE = 8
PRE_BLK = 4 * BLK


def _inverse_level_masks():
    i = lax.broadcasted_iota(jnp.int32, (BLK, BLK), 0)
    j = lax.broadcasted_iota(jnp.int32, (BLK, BLK), 1)
    same = lambda size: (i >> int(math.log2(size))) == (j >> int(math.log2(size)))
    base = same(INV_BASE)
    joins = []
    size = INV_BASE
    while size < DN_CHUNK:
        joins.append(same(2 * size) & jnp.logical_not(same(size)))
        size *= 2
    return base, joins


def _tri_inverses(l_mats, eyes, level_masks):
    base, joins = level_masks
    ps = [jnp.where(base, -l, 0.0) for l in l_mats]
    ts = [eye + p for eye, p in zip(eyes, ps)]
    for _ in range(int(math.log2(INV_BASE)) - 1):
        ps = [_mm(p, p) for p in ps]
        ts = [t + _mm(t, p) for t, p in zip(ts, ps)]
    for join in joins:
        mids = [_mm(jnp.where(join, l, 0.0), t) for l, t in zip(l_mats, ts)]
        ts = [t - _mm(t, mid) for t, mid in zip(ts, mids)]
    return ts


DN_HW = DN_HEADS * LANES
PK_W, PK_QD, PK_KD, PK_QK = (i * DN_HW for i in range(4))


def _dn_pre_kernel(qkv_ref, sm_ref, cw_ref, lp_ref, uf_ref, ub_ref, pkf_ref, pkb_ref, gtf_ref, gtb_ref):
    n = pl.program_id(1)
    s_len = qkv_ref.shape[0]
    assert qkv_ref.dtype == BF16
    halo = 2 * SUBLANES
    n_sub = sm_ref.shape[0] // BLK
    chunks_per_blk = BLK // DN_CHUNK
    out_row = lax.broadcasted_iota(jnp.int32, (BLK, BLK + 2 * halo), 0)
    in_row = lax.broadcasted_iota(jnp.int32, (BLK, BLK + 2 * halo), 1)
    shifts = {j: jnp.where(in_row == out_row + (halo + j - DN_CONV // 2), 1.0, 0.0).astype(BF16)
              for j in range(DN_CONV) if j != DN_CONV // 2}
    dir_masks = [_chunk_masks(reverse=(d == 1)) for d in range(2)]
    level_masks = _inverse_level_masks()
    outs = ((uf_ref, pkf_ref, gtf_ref), (ub_ref, pkb_ref, gtb_ref))

    def l2n(x):
        return x * lax.rsqrt(jnp.sum(x * x, axis=-1, keepdims=True) + EPS)

    blocks = []
    for sub in range(n_sub):
        t0 = pl.multiple_of((n * n_sub + sub) * BLK, BLK)
        pstart = pl.multiple_of(jnp.maximum(t0 - halo, 0), halo)
        nstart = pl.multiple_of(jnp.minimum(t0 + BLK, s_len - halo), halo)
        prev = qkv_ref[pl.ds(pstart, halo), :]
        prev = jnp.where(t0 > 0, prev, jnp.zeros_like(prev))
        cur = qkv_ref[pl.ds(t0, BLK), :]
        nxt = qkv_ref[pl.ds(nstart, halo), :]
        nxt = jnp.where(t0 + BLK < s_len, nxt, jnp.zeros_like(nxt))
        xw = jnp.concatenate([prev, cur, nxt], axis=0)
        conv = cur.astype(F32) * cw_ref[DN_CONV // 2:DN_CONV // 2 + 1, :]
        for j, shift in shifts.items():
            conv = conv + jnp.dot(shift, xw, preferred_element_type=F32) * cw_ref[j:j + 1, :]
        conv = _silu(conv)

        rows = slice(sub * BLK, (sub + 1) * BLK)
        sm = sm_ref[rows, :]
        g_all = -jnp.exp(lp_ref[0:1, :]) * _softplus(sm + lp_ref[1:2, :])
        beta_all = _sigmoid(sm)
        per_dir = []
        for d in range(2):
            same, incl, strict, eye_b = dir_masks[d]
            gc = _mask_sum(incl, g_all)
            tot = _mask_sum(same, g_all)
            per_dir.append((incl, strict, _as_f32(eye_b), gc, gc.T, tot))
        blocks.append((sub, rows, conv, beta_all, per_dir))

    heads = []
    for sub, rows, conv, beta_all, per_dir in blocks:
        for h in range(DN_HEADS):
            q = l2n(conv[:, h * DN_DK:(h + 1) * DN_DK]) * (DN_DK ** -0.5)
            k = l2n(conv[:, (DN_HEADS + h) * DN_DK:(DN_HEADS + h + 1) * DN_DK])
            v = conv[:, 2 * DN_HEADS * DN_DK + h * DN_DV:2 * DN_HEADS * DN_DK + (h + 1) * DN_DV]
            heads.append((sub, rows, beta_all, per_dir, h, q, k, v))
    kks = [_mm_nt(hd[6], hd[6]) for hd in heads]
    qks = [_mm_nt(hd[5], hd[6]) for hd in heads]

    l_mats, rhss, eyes, slots = [], [], [], []
    for (sub, rows, beta_all, per_dir, h, q, k, v), kk, qk in zip(heads, kks, qks):
        hc = slice(h * LANES, (h + 1) * LANES)
        for d in range(2):
            incl, strict, eye_f, gc, gct, tot = per_dir[d]
            u_ref, pk_ref, gt_ref = outs[d]
            pk = lambda off: slice(off + h * LANES, off + (h + 1) * LANES)
            c = _L_AA + d * DN_HEADS + h
            gcol = gc[:, c:c + 1]
            grow = gct[c:c + 1, :]
            tcol = tot[:, c:c + 1]
            beta = beta_all[:, _L_AB + d * DN_HEADS + h:_L_AB + d * DN_HEADS + h + 1]
            decay = jnp.where(incl, jnp.exp(jnp.where(incl, gcol - grow, 0.0)), 0.0)
            l_mats.append(jnp.where(strict, beta * kk * decay, 0.0))
            egc = jnp.exp(gcol)
            rhss.append(jnp.concatenate([v * beta, k * (beta * egc)], axis=-1).astype(BF16))
            eyes.append(eye_f)
            slots.append((rows, h, d))
            pk_ref[rows, pk(PK_QD)] = (q * egc).astype(pk_ref.dtype)
            pk_ref[rows, pk(PK_KD)] = (k * jnp.exp(tcol - gcol)).astype(pk_ref.dtype)
            pk_ref[rows, pk(PK_QK)] = (qk * decay).astype(pk_ref.dtype)
            gtot = jnp.exp(tcol)
            for ci in range(chunks_per_blk):
                g0 = (sub * chunks_per_blk + ci) * SUBLANES
                gt_ref[g0:g0 + SUBLANES, hc] = jnp.broadcast_to(
                    gtot[ci * DN_CHUNK:ci * DN_CHUNK + SUBLANES, :], (SUBLANES, LANES))

    t_invs = _tri_inverses(l_mats, eyes, level_masks)
    sols = [_mm(t, rhs) for t, rhs in zip(t_invs, rhss)]
    for (rows, h, d), sol in zip(slots, sols):
        u_ref, pk_ref = outs[d][0], outs[d][1]
        u_ref[rows, h * LANES:(h + 1) * LANES] = sol[:, :DN_DV]
        pk_ref[rows, PK_W + h * LANES:PK_W + (h + 1) * LANES] = sol[:, DN_DV:].astype(pk_ref.dtype)


def _dn_pre(pm3, ps3, conv_w8, lane_params):
    b, s, _ = pm3.shape
    nblk = s // PRE_BLK
    gt_rows = PRE_BLK // DN_CHUNK * SUBLANES
    wq = DN_HEADS * (2 * DN_DK + DN_DV)
    hw = DN_HEADS * LANES
    tok = lambda i, j: (i, j, 0)
    big = lambda dt: jax.ShapeDtypeStruct((b, s, hw), dt)
    gts = jax.ShapeDtypeStruct((b, nblk * gt_rows, hw), F32)
    bs_tok = pl.BlockSpec((None, PRE_BLK, hw), tok)
    bs_gt = pl.BlockSpec((None, gt_rows, hw), tok)
    bs_pk = pl.BlockSpec((None, PRE_BLK, 4 * hw), tok)
    packed = jax.ShapeDtypeStruct((b, s, 4 * hw), BF16)
    return pl.pallas_call(
        _dn_pre_kernel,
        grid=(b, nblk),
        in_specs=[
            pl.BlockSpec((None, s, wq), lambda i, j: (i, 0, _OFF["a_qkv"] // wq)),
            pl.BlockSpec((None, PRE_BLK, LANES), tok),
            pl.BlockSpec((SUBLANES, wq), lambda i, j: (0, 0)),
            pl.BlockSpec((SUBLANES, LANES), lambda i, j: (0, 0)),
        ],
        out_specs=[bs_tok, bs_tok, bs_pk, bs_pk, bs_gt, bs_gt],
        out_shape=[big(F32), big(F32), packed, packed, gts, gts],
        compiler_params=_cparams(("parallel", "arbitrary")),
        name="dn_pre",
    )(pm3, ps3, conv_w8, lane_params)


SCAN_BLK = 4 * BLK


def _dn_scan_kernel(uf_ref, pkf_ref, gtf_ref, ub_ref, pkb_ref, gtb_ref, of_ref, ob_ref, st_scr):
    @pl.when(pl.program_id(1) == 0)
    def _():
        st_scr[...] = jnp.zeros_like(st_scr)

    per_blk = BLK // DN_CHUNK
    nchunk = SCAN_BLK // DN_CHUNK
    zeros_c = jnp.zeros((DN_CHUNK, DN_DV), F32)
    streams = ((uf_ref, pkf_ref, gtf_ref, of_ref, range(nchunk)),
               (ub_ref, pkb_ref, gtb_ref, ob_ref, range(nchunk - 1, -1, -1)))
    chains = [(d, h) + streams[d] for d in range(2) for h in range(DN_HEADS)]
    states = [st_scr[d * DN_HEADS + h] for d, h, *_ in chains]
    for step in range(nchunk):
        rs, v_pads = [], []
        for (d, h, u_ref, pk_ref, gt_ref, o_ref, order), state in zip(chains, states):
            rows = slice(order[step] * DN_CHUNK, (order[step] + 1) * DN_CHUNK)
            w = pk_ref[rows, PK_W + h * LANES:PK_W + (h + 1) * LANES]
            qd = pk_ref[rows, PK_QD + h * LANES:PK_QD + (h + 1) * LANES]
            rs.append(_mm(jnp.concatenate([w, qd], axis=0), state))
        for (d, h, u_ref, pk_ref, gt_ref, o_ref, order), r in zip(chains, rs):
            ci = order[step]
            rows = slice(ci * DN_CHUNK, (ci + 1) * DN_CHUNK)
            parts = [zeros_c] * per_blk
            parts[ci % per_blk] = u_ref[rows, h * LANES:(h + 1) * LANES] - r[:DN_CHUNK]
            v_pads.append(jnp.concatenate(parts, axis=0))
        new_states = []
        for (d, h, u_ref, pk_ref, gt_ref, o_ref, order), r, v_pad, state in zip(chains, rs, v_pads, states):
            hc = slice(h * LANES, (h + 1) * LANES)
            ci = order[step]
            rows = slice(ci * DN_CHUNK, (ci + 1) * DN_CHUNK)
            blk_rows = slice((ci // per_blk) * BLK, (ci // per_blk + 1) * BLK)
            o_ref[rows, hc] = r[DN_CHUNK:] + _mm(pk_ref[rows, PK_QK + h * LANES:PK_QK + (h + 1) * LANES], v_pad)
            gt = gt_ref[ci * SUBLANES:ci * SUBLANES + 1, hc]
            kd = pk_ref[blk_rows, PK_KD + h * LANES:PK_KD + (h + 1) * LANES]
            new_states.append(state * gt + _mm_tn(kd, v_pad))
        states = new_states
    for (d, h, *_), state in zip(chains, states):
        st_scr[d * DN_HEADS + h] = state


def _dn_scan(pre):
    uf, ub, pkf, pkb, gtf, gtb = pre
    b, s, hw = uf.shape
    nblk = s // SCAN_BLK
    gt_rows = SCAN_BLK // DN_CHUNK * SUBLANES
    fwd = lambda i, j: (i, j, 0)
    bwd = lambda i, j: (i, nblk - 1 - j, 0)
    def specs(imap):
        return [pl.BlockSpec((None, SCAN_BLK, hw), imap), pl.BlockSpec((None, SCAN_BLK, 4 * hw), imap),
                pl.BlockSpec((None, gt_rows, hw), imap)]
    return pl.pallas_call(
        _dn_scan_kernel,
        grid=(b, nblk),
        in_specs=specs(fwd) + specs(bwd),
        out_specs=[pl.BlockSpec((None, SCAN_BLK, hw), fwd), pl.BlockSpec((None, SCAN_BLK, hw), bwd)],
        out_shape=[jax.ShapeDtypeStruct((b, s, hw), F32)] * 2,
        scratch_shapes=[pltpu.VMEM((2 * DN_HEADS, DN_DK, DN_DV), F32)],
        compiler_params=_cparams(("parallel", "arbitrary")),
        name="dn_scan",
    )(uf, pkf, gtf, ub, pkb, gtb)


ML_AUG = 2 * LANES
ML_GROUP = 4
ML_BLK = 4 * BLK


def _ml_kernel(qf_ref, kf_ref, vf_ref, smf_ref, qb_ref, kb_ref, vb_ref, smb_ref, lp_ref,
               hf_ref, hb_ref, c_scr, m_scr):
    @pl.when(pl.program_id(1) == 0)
    def _():
        c_scr[...] = jnp.zeros_like(c_scr)
        m_scr[...] = jnp.zeros_like(m_scr)

    nchunk = BLK // ML_CHUNK
    ones_col = jnp.ones((BLK, LANES), BF16)
    zeros_aug = jnp.zeros((ML_CHUNK, ML_AUG), BF16)
    streams = ((qf_ref, kf_ref, vf_ref, smf_ref, hf_ref, range(nchunk)),
               (qb_ref, kb_ref, vb_ref, smb_ref, hb_ref, range(nchunk - 1, -1, -1)))
    lanes = lambda col: jnp.broadcast_to(col, (col.shape[0], LANES))
    n_sub = smf_ref.shape[0] // BLK
    units = [(d, p if d == 0 else n_sub - 1 - p) for p in range(n_sub) for d in range(2)]
    for d, sub in units:
        _ml_block(d, sub, streams[d], lp_ref, c_scr, m_scr, lanes, ones_col, zeros_aug, nchunk)


def _ml_block(d, sub, stream, lp_ref, c_scr, m_scr, lanes, ones_col, zeros_aug, nchunk):
    q_ref, k_ref, v_ref, sm_ref, h_ref, order = stream
    blk_rows = slice(sub * BLK, (sub + 1) * BLK)
    same, incl, _, _ = _chunk_masks(reverse=(d == 1))
    sm = sm_ref[blk_rows, :]
    ig_all = sm + lp_ref[0:1, :]
    x = sm + lp_ref[1:2, :]
    lf_all = jnp.minimum(x, 0.0) - jnp.log(1.0 + jnp.exp(-jnp.abs(x)))
    lf_all = pltpu.roll(lf_all, LANES - (_L_DF - _L_DI), 1)
    bc_all = _mask_sum(incl, lf_all)
    tot_all = _mask_sum(same, lf_all)
    a_all = ig_all - bc_all
    mwa_all = jnp.concatenate(
        [jnp.broadcast_to(jnp.max(a_all[ci * ML_CHUNK:(ci + 1) * ML_CHUNK], axis=0, keepdims=True),
                          (ML_CHUNK, LANES)) for ci in range(nchunk)], axis=0)
    a_t, w_all, mw_all = a_all.T, jnp.exp(a_all - mwa_all), tot_all + mwa_all

    for chains in [[(d, h) for h in range(g, g + ML_GROUP)] for g in range(0, ML_HEADS, ML_GROUP)]:
        ins = []
        for _, h in chains:
            q = q_ref[blk_rows, h * ML_DK:(h + 1) * ML_DK].astype(BF16)
            k = k_ref[blk_rows, h * ML_DK:(h + 1) * ML_DK].astype(F32) * (ML_DK ** -0.5)
            v_aug = jnp.concatenate([v_ref[blk_rows, h * ML_DV:(h + 1) * ML_DV].astype(BF16), ones_col], axis=-1)
            ins.append((q, k, v_aug))
        qks = [_mm_nt(q, k) for q, k, _ in ins]

        mids = []
        for (d, h), (q, k, v_aug), qk in zip(chains, ins, qks):
            c = _L_DI + d * ML_HEADS + h
            b_l = lanes(bc_all[:, c:c + 1])
            dlog = jnp.where(incl, b_l + a_t[c:c + 1, :], NEG)
            m_intra = lanes(jnp.max(dlog, axis=-1, keepdims=True))
            s_intra = qk * jnp.exp(dlog - m_intra)
            wk = (k * lanes(w_all[:, c:c + 1])[:, :ML_DK]).astype(BF16)
            mids.append((b_l, m_intra, s_intra.astype(BF16), wk))
        p_intras = [_mm(s_b, v_aug) for (_, _, s_b, _), (_, _, v_aug) in zip(mids, ins)]
        kvs = []
        for (_, _, _, wk), (_, _, v_aug) in zip(mids, ins):
            per_chunk = []
            for ci in range(nchunk):
                parts = [zeros_aug] * nchunk
                parts[ci] = v_aug[ci * ML_CHUNK:(ci + 1) * ML_CHUNK]
                per_chunk.append(_mm_tn(wk, jnp.concatenate(parts, axis=0)))
            kvs.append(per_chunk)

        c_sts = [c_scr[d * ML_HEADS + h] for d, h in chains]
        m_sts = [m_scr[d * ML_HEADS + h][0:1, :] for d, h in chains]
        for step in range(nchunk):
            qcs = []
            ci = order[step]
            for (q, _, _), c_st in zip(ins, c_sts):
                qcs.append(_mm(q[ci * ML_CHUNK:(ci + 1) * ML_CHUNK], c_st))
            for idx, (_, h) in enumerate(chains):
                b_l, m_intra, _, _ = mids[idx]
                c = _L_DI + d * ML_HEADS + h
                rows = slice(ci * ML_CHUNK, (ci + 1) * ML_CHUNK)
                out_rows = slice(sub * BLK + ci * ML_CHUNK, sub * BLK + (ci + 1) * ML_CHUNK)
                r8 = slice(ci * ML_CHUNK, ci * ML_CHUNK + SUBLANES)
                m_st, c_st, qc = m_sts[idx], c_sts[idx], qcs[idx]
                m_inter = b_l[rows] + m_st
                m_i = jnp.maximum(m_intra[rows], m_inter)
                f_i = jnp.exp(m_intra[rows] - m_i)
                inter = jnp.exp(m_inter - m_i)
                both = (jnp.concatenate([inter, inter], axis=-1) * qc
                        + jnp.concatenate([f_i, f_i], axis=-1) * p_intras[idx][rows])
                numer, denom = both[:, :ML_DV], both[:, ML_DV:]
                h_ref[out_rows, h * ML_DV:(h + 1) * ML_DV] = numer / jnp.maximum(jnp.abs(denom), jnp.exp(-m_i))
                tot_c = lanes(tot_all[r8, c:c + 1])[0:1]
                mw_c = lanes(mw_all[r8, c:c + 1])[0:1]
                m_new = jnp.maximum(tot_c + m_st, mw_c)
                dec = jnp.exp(tot_c + m_st - m_new)
                gain = jnp.exp(mw_c - m_new)
                c_sts[idx] = (jnp.concatenate([dec, dec], axis=-1) * c_st
                              + jnp.concatenate([gain, gain], axis=-1) * kvs[idx][ci])
                m_sts[idx] = m_new
        for idx, (d, h) in enumerate(chains):
            c_scr[d * ML_HEADS + h] = c_sts[idx]
            m_scr[d * ML_HEADS + h] = jnp.broadcast_to(m_sts[idx], (SUBLANES, LANES))


def _mlstm(pm3, ps3, lane_params):
    b, s, _ = pm3.shape
    nblk = s // ML_BLK
    qw = ML_HEADS * ML_DK
    vw = ML_HEADS * ML_DV
    def specs(tmap):
        blk = lambda j: tmap(j)
        return [
            pl.BlockSpec((None, ML_BLK, qw), lambda i, j: (i, blk(j), _OFF["d_q"] // qw)),
            pl.BlockSpec((None, ML_BLK, qw), lambda i, j: (i, blk(j), _OFF["d_k"] // qw)),
            pl.BlockSpec((None, ML_BLK, vw), lambda i, j: (i, blk(j), _OFF["d_v"] // vw)),
            pl.BlockSpec((None, ML_BLK, LANES), lambda i, j: (i, blk(j), 0)),
        ]
    fwd = lambda j: j
    bwd = lambda j: nblk - 1 - j
    return pl.pallas_call(
        _ml_kernel,
        grid=(b, nblk),
        in_specs=specs(fwd) + specs(bwd) + [pl.BlockSpec((SUBLANES, LANES), lambda i, j: (0, 0))],
        out_specs=[pl.BlockSpec((None, ML_BLK, vw), lambda i, j: (i, j, 0)),
                   pl.BlockSpec((None, ML_BLK, vw), lambda i, j: (i, nblk - 1 - j, 0))],
        out_shape=[jax.ShapeDtypeStruct((b, s, vw), F32)] * 2,
        scratch_shapes=[pltpu.VMEM((2 * ML_HEADS, ML_DK, ML_AUG), F32),
                        pltpu.VMEM((2 * ML_HEADS, SUBLANES, LANES), F32)],
        compiler_params=_cparams(("parallel", "arbitrary")),
        name="mlstm",
    )(pm3, pm3, pm3, ps3, pm3, pm3, pm3, ps3, lane_params)


def _merge_kernel(x_ref, af_ref, ab_ref, df_ref, db_ref, yb_ref, yc_ref, az_ref, dz_ref, do_ref, gl_ref,
                  ag_ref, dg_ref, wb_ref, wo_ref, o_ref):
    d = x_ref.shape[-1]

    def head_rms(x, g):
        outs = []
        for h in range(x.shape[-1] // LANES):
            xh = x[:, h * LANES:(h + 1) * LANES]
            ms = jnp.mean(xh * xh, axis=-1, keepdims=True)
            outs.append(xh * lax.rsqrt(ms + EPS) * g)
        return jnp.concatenate(outs, axis=-1)

    ya = head_rms(af_ref[...] + ab_ref[...], ag_ref[...]) * _silu(az_ref[...].astype(F32))
    yd = _sigmoid(do_ref[...].astype(F32)) * head_rms(df_ref[...] + db_ref[...], dg_ref[...])
    yd = yd * _silu(dz_ref[...].astype(F32))
    twice = None
    for i, y in enumerate((ya.astype(BF16), yb_ref[...], yc_ref[...], yd.astype(BF16))):
        proj = jnp.dot(y, wb_ref[i], preferred_element_type=F32)
        term = proj + jnp.tanh(0.5 * gl_ref[:, i * d:(i + 1) * d].astype(F32)) * proj
        twice = term if twice is None else twice + term
    merged = (0.5 * twice).astype(BF16)
    o_ref[...] = x_ref[...] + jnp.dot(merged, wo_ref[...], preferred_element_type=F32)


def _merge(x2d, af, ab, df, db, yb, yc, pm2, ag, dg, wb, wo, layer, tm=512):
    m, d = x2d.shape
    gw = N_BRANCH * d
    w = BRANCH_W
    tok = pl.BlockSpec((tm, w), lambda i: (i, 0))
    col = lambda name: pl.BlockSpec((tm, w), lambda i: (i, _OFF[name] // w))
    vec = pl.BlockSpec((1, LANES), lambda i: (0, 0))
    return pl.pallas_call(
        _merge_kernel,
        grid=(m // tm,),
        in_specs=[
            pl.BlockSpec((tm, d), lambda i: (i, 0)),
            tok, tok, tok, tok, tok, tok,
            col("a_z"), col("d_z"), col("d_o"),
            pl.BlockSpec((tm, gw), lambda i: (i, _OFF["gate"] // gw)),
            vec, vec,
            pl.BlockSpec((None, N_BRANCH, w, d), lambda i: (layer, 0, 0, 0)),
            pl.BlockSpec((None, d, d), lambda i: (layer, 0, 0)),
        ],
        out_specs=pl.BlockSpec((tm, d), lambda i: (i, 0)),
        out_shape=jax.ShapeDtypeStruct((m, d), F32),
        compiler_params=_cparams(("parallel",)),
        name="merge",
    )(x2d, af, ab, df, db, yb, yc, pm2, pm2, pm2, pm2, ag, dg, wb, wo)


def _rope_lane_tables(s):
    t = jnp.arange(s)
    row = (t // GRID_W).astype(F32)
    col = (t % GRID_W).astype(F32)
    m = GA_DH // 4
    inv = ROPE_THETA ** (-jnp.arange(m, dtype=F32) / m)
    ar = row[:, None] * inv
    ac = col[:, None] * inv
    cos_t = jnp.concatenate([jnp.cos(ar), jnp.cos(ar), jnp.cos(ac), jnp.cos(ac)], axis=-1)
    sin_t = jnp.concatenate([-jnp.sin(ar), jnp.sin(ar), -jnp.sin(ac), jnp.sin(ac)], axis=-1)
    return cos_t.astype(F32), sin_t.astype(F32)


def _lane_tiles(rows):
    padded = []
    for off, vals in rows:
        vals = vals.reshape(vals.shape[0], 1, -1).astype(F32)
        padded.append(jnp.pad(vals, ((0, 0), (0, 0), (off, LANES - off - vals.shape[-1]))))
    tiles = jnp.concatenate(padded, axis=1)
    return jnp.pad(tiles, ((0, 0), (0, SUBLANES - len(rows)), (0, 0)))


def kernel(x, norm_g, w_in, conv_a, dn_a_log, dn_dt_bias, dn_norm_g, na_q_norm, na_k_norm, na_rpb,
           ga_q_norm, ga_k_norm, ml_i_bias, ml_f_bias, ml_norm_g, w_branch, w_out):
    b, s, d = x.shape
    depth = w_in.shape[0]
    hw = BRANCH_W
    cos_t, sin_t = _rope_lane_tables(s)
    w_main = jnp.concatenate([w_in[:, :, o:o + wd] for _, o, wd in _MAIN_SEGS], axis=2).astype(BF16)
    w_small = jnp.pad(jnp.concatenate([w_in[:, :, o:o + 8] for o in _SMALL_SRC], axis=2),
                      ((0, 0), (0, 0), (0, LANES - 8 * len(_SMALL_SRC)))).astype(BF16)
    conv8 = jnp.pad(conv_a.astype(F32), ((0, 0), (0, SUBLANES - DN_CONV), (0, 0)))
    dn_lp = _lane_tiles([(_L_AA, dn_a_log), (_L_AA, dn_dt_bias)])
    ml_lp = _lane_tiles([(_L_DI, ml_i_bias), (_L_DF, ml_f_bias)])
    na_bias = _na_bias_table(na_rpb, s // GRID_W)
    na_qg = jnp.tile(na_q_norm, (1, 2)).reshape(depth, 1, 2 * NA_DH)
    na_kg = jnp.tile(na_k_norm, (1, 2)).reshape(depth, 1, 2 * NA_DH)
    wb_bf, wo_bf = w_branch.astype(BF16), w_out.astype(BF16)

    x2 = x.reshape(b * s, d)
    for l in range(depth):
        pm2, ps2 = _inproj(x2, norm_g[l].reshape(1, d), w_main, w_small, l)
        pm3 = pm2.reshape(b, s, N_MAIN)
        ps3 = ps2.reshape(b, s, LANES)
        o_af, o_ab = _dn_scan(_dn_pre(pm3, ps3, conv8[l], dn_lp[l]))
        h_df, h_db = _mlstm(pm3, ps3, ml_lp[l])
        yb = _natten(pm3, na_bias, na_qg[l], na_kg[l], l)
        yc = _gqa(pm3, cos_t, sin_t, ga_q_norm[l].reshape(1, GA_DH), ga_k_norm[l].reshape(1, GA_DH))
        x2 = _merge(x2, o_af.reshape(b * s, hw), o_ab.reshape(b * s, hw), h_df.reshape(b * s, hw),
                    h_db.reshape(b * s, hw), yb.reshape(b * s, hw), yc.reshape(b * s, hw), pm2,
                    dn_norm_g[l].reshape(1, LANES), ml_norm_g[l].reshape(1, LANES), wb_bf, wo_bf, l)
    return x2.reshape(b, s, d)
```

```python
import functools
import math

import jax
import jax.numpy as jnp
from jax import lax
from jax.experimental import pallas as pl
from jax.experimental.pallas import tpu as pltpu

F32 = jnp.float32
BF16 = jnp.bfloat16

D_MODEL = 1024
GRID_W = 64
N_BRANCH = 4
BRANCH_W = 512
EPS = 1e-6
DN_HEADS, DN_DK, DN_DV, DN_CONV, DN_CHUNK = 4, 128, 128, 5, 64
NA_HEADS, NA_DH, NA_ROWS, NA_COLS = 8, 64, 8, 16
GA_HEADS, GA_KV_HEADS, GA_DH = 4, 2, 128
ROPE_THETA = 10000.0
ML_HEADS, ML_DK, ML_DV, ML_CHUNK = 4, 64, 128, 128

LANES = 128
SUBLANES = 8
VMEM_LIMIT_BYTES = 56 * 1024 * 1024

_O_A_QKV, _O_A_A, _O_A_B, _O_A_Z = 0, 1536, 1544, 1552
_O_B_QKV, _O_B_Z = 2064, 3600
_O_C_Q, _O_C_K, _O_C_V, _O_C_Z = 4112, 4624, 4880, 5136
_O_D_Q, _O_D_K, _O_D_V, _O_D_I, _O_D_F, _O_D_O, _O_D_Z = 5648, 5904, 6160, 6672, 6680, 6688, 7200
_O_GATE = 7712
_MAIN_SEGS = (
    ("a_qkv", _O_A_QKV, 1536), ("b_qkv", _O_B_QKV, 1536), ("a_z", _O_A_Z, 512), ("b_z", _O_B_Z, 512),
    ("gate", _O_GATE, 4096), ("c_q", _O_C_Q, 512), ("c_k", _O_C_K, 256), ("c_v", _O_C_V, 256),
    ("c_z", _O_C_Z, 512), ("d_q", _O_D_Q, 256), ("d_k", _O_D_K, 256), ("d_v", _O_D_V, 512),
    ("d_o", _O_D_O, 512), ("d_z", _O_D_Z, 512),
)
_OFF = {}
_o = 0
for _name, _src, _w in _MAIN_SEGS:
    _OFF[_name] = _o
    _o += _w
N_MAIN = _o
_SMALL_SRC = (_O_A_A, _O_A_B, _O_D_I, _O_D_F)
_L_AA, _L_AB, _L_DI, _L_DF = 0, 8, 16, 24

P_DTYPE = BF16
BLK = 128
NEG = -1e30


def _cparams(sem):
    return pltpu.CompilerParams(dimension_semantics=sem, vmem_limit_bytes=VMEM_LIMIT_BYTES)


def _sigmoid(x):
    return 0.5 * jnp.tanh(0.5 * x) + 0.5


def _silu(x):
    return x * _sigmoid(x)


def _softplus(x):
    return jnp.maximum(x, 0.0) + jnp.log(1.0 + jnp.exp(-jnp.abs(x)))


def _mm(a, b):
    return jnp.dot(a.astype(BF16), b.astype(BF16), preferred_element_type=F32)


def _mm_nt(a, b):
    return lax.dot_general(a.astype(BF16), b.astype(BF16), (((1,), (1,)), ((), ())),
                           preferred_element_type=F32)


def _mm_tn(a, b):
    return lax.dot_general(a.astype(BF16), b.astype(BF16), (((0,), (0,)), ((), ())),
                           preferred_element_type=F32)


def _mask_sum(mask, x):
    m = jnp.where(mask, 1.0, 0.0).astype(BF16)
    x1 = x.astype(BF16)
    r1 = x - x1.astype(F32)
    x2 = r1.astype(BF16)
    x3 = (r1 - x2.astype(F32)).astype(BF16)
    dot = lambda v: jnp.dot(m, v, preferred_element_type=F32)
    return dot(x1) + (dot(x2) + dot(x3))


def _chunk_masks(reverse, chunk):
    i = lax.broadcasted_iota(jnp.int32, (BLK, BLK), 0)
    j = lax.broadcasted_iota(jnp.int32, (BLK, BLK), 1)
    shift = int(math.log2(chunk))
    same = (i >> shift) == (j >> shift)
    if reverse:
        incl = same & (j >= i)
        strict = same & (j > i)
    else:
        incl = same & (j <= i)
        strict = same & (j < i)
    return same, incl, strict, (i == j)


def _as_f32(mask):
    return jnp.where(mask, 1.0, 0.0).astype(F32)


def _inproj_kernel(x_ref, g_ref, w_ref, ws_ref, pm_ref, ps_ref, h_scr):
    @pl.when(pl.program_id(1) == 0)
    def _():
        x = x_ref[...]
        ms = jnp.mean(x * x, axis=-1, keepdims=True)
        h = (x * lax.rsqrt(ms + EPS) * g_ref[...]).astype(BF16)
        h_scr[...] = h
        ps_ref[...] = jnp.dot(h, ws_ref[...], preferred_element_type=F32)

    pm_ref[...] = jnp.dot(h_scr[...], w_ref[...], preferred_element_type=F32).astype(pm_ref.dtype)


def _inproj(x2d, g, w_main, w_small, layer, tm=1024, tn=N_MAIN // 4):
    m, d = x2d.shape
    tm = min(tm, m)
    return pl.pallas_call(
        _inproj_kernel,
        grid=(m // tm, N_MAIN // tn),
        in_specs=[
            pl.BlockSpec((tm, d), lambda i, j: (i, 0)),
            pl.BlockSpec((1, d), lambda i, j: (0, 0)),
            pl.BlockSpec((None, d, tn), lambda i, j: (layer, 0, j)),
            pl.BlockSpec((None, d, LANES), lambda i, j: (layer, 0, 0)),
        ],
        out_specs=[
            pl.BlockSpec((tm, tn), lambda i, j: (i, j)),
            pl.BlockSpec((tm, LANES), lambda i, j: (i, 0)),
        ],
        out_shape=[jax.ShapeDtypeStruct((m, N_MAIN), P_DTYPE), jax.ShapeDtypeStruct((m, LANES), F32)],
        scratch_shapes=[pltpu.VMEM((tm, d), BF16)],
        compiler_params=_cparams(("parallel", "arbitrary")),
        name="inproj",
    )(x2d, g, w_main, w_small)


GQA_TQ = 512


def _gqa_kernel(q_ref, k_ref, v_ref, z_ref, cos_ref, sin_ref, qg_ref, kg_ref, y_ref,
                q_scr, k_scr, v_scr, s0_scr, s1_scr, p0_scr, p1_scr, l0_scr, l1_scr):
    def norm_rope(x, g, cos, sin):
        ms = jnp.mean(x * x, axis=-1, keepdims=True)
        xn = x * lax.rsqrt(ms + EPS) * g
        lane = lax.broadcasted_iota(jnp.int32, xn.shape, 1)
        partner = jnp.where((lane & 63) < 32, pltpu.roll(xn, LANES - 32, 1), pltpu.roll(xn, 32, 1))
        return xn * cos + partner * sin

    s_len = k_ref.shape[0]
    group = GA_HEADS // GA_KV_HEADS
    assert group == 2
    scale = GA_DH ** -0.5
    n_blk = s_len // GQA_TQ
    s_bufs, p_bufs, l_bufs = (s0_scr, s1_scr), (p0_scr, p1_scr), (l0_scr, l1_scr)
    for kv in range(k_ref.shape[1] // GA_DH):
        _gqa_pipeline(kv, group, s_len, n_blk, scale, norm_rope, q_ref, k_ref, v_ref, z_ref, cos_ref, sin_ref,
                      qg_ref, kg_ref, y_ref, q_scr, k_scr, v_scr, s_bufs, p_bufs, l_bufs)


def _gqa_pipeline(kv, group, s_len, n_blk, scale, norm_rope, q_ref, k_ref, v_ref, z_ref, cos_ref, sin_ref,
                  qg_ref, kg_ref, y_ref, q_scr, k_scr, v_scr, s_bufs, p_bufs, l_bufs):
    kcols = slice(kv * GA_DH, (kv + 1) * GA_DH)
    k_scr[kv] = norm_rope(k_ref[:, kcols].astype(F32), kg_ref[...], cos_ref[...], sin_ref[...]).astype(BF16)
    v_scr[kv] = v_ref[:, kcols].astype(BF16)

    def stacked(head, blk):
        return pl.ds(pl.multiple_of(head * s_len + blk * GQA_TQ, GQA_TQ), GQA_TQ)

    def head_cols(head):
        return slice((kv * group + head) * GA_DH, (kv * group + head + 1) * GA_DH)

    def prep(head, blk):
        rows = pl.ds(pl.multiple_of(blk * GQA_TQ, GQA_TQ), GQA_TQ)
        q = norm_rope(q_ref[rows, head_cols(head)].astype(F32), qg_ref[...], cos_ref[rows, :], sin_ref[rows, :])
        q_scr[stacked(head, blk), :] = q.astype(BF16)

    def logits(head, blk):
        s_bufs[head][...] = _mm_nt(q_scr[stacked(head, blk), :], k_scr[kv]) * scale

    def softmax(slot):
        s = s_bufs[slot][...]
        p = jnp.exp(s - jnp.max(s, axis=-1, keepdims=True))
        l_bufs[slot][...] = jnp.broadcast_to(jnp.sum(p, axis=-1, keepdims=True), (GQA_TQ, GA_DH))
        p_bufs[slot][...] = p.astype(BF16)

    def weighted(head, blk):
        rows = pl.ds(pl.multiple_of(blk * GQA_TQ, GQA_TQ), GQA_TQ)
        cols = head_cols(head)
        o = _mm(p_bufs[head][...], v_scr[kv]) / l_bufs[head][...]
        y_ref[rows, cols] = (o * _silu(z_ref[rows, cols].astype(F32))).astype(y_ref.dtype)

    prep(0, 0)
    prep(1, 0)
    prep(0, 1)
    logits(0, 0)
    logits(1, 0)
    softmax(0)

    def body(j, carry):
        logits(0, j + 1)
        softmax(1)
        weighted(0, j)
        prep(1, j + 1)
        logits(1, j + 1)
        softmax(0)
        weighted(1, j)
        prep(0, jnp.minimum(j + 2, n_blk - 1))
        return carry

    lax.fori_loop(0, n_blk - 1, body, 0)
    softmax(1)
    weighted(0, n_blk - 1)
    weighted(1, n_blk - 1)


def _gqa(pm3, cos_t, sin_t, qg, kg):
    b, s, _ = pm3.shape
    qw = GA_HEADS * GA_DH
    kw = GA_KV_HEADS * GA_DH
    group = GA_HEADS // GA_KV_HEADS
    return pl.pallas_call(
        _gqa_kernel,
        grid=(b,),
        in_specs=[
            pl.BlockSpec((None, s, qw), lambda i: (i, 0, _OFF["c_q"] // qw)),
            pl.BlockSpec((None, s, kw), lambda i: (i, 0, _OFF["c_k"] // kw)),
            pl.BlockSpec((None, s, kw), lambda i: (i, 0, _OFF["c_v"] // kw)),
            pl.BlockSpec((None, s, qw), lambda i: (i, 0, _OFF["c_z"] // qw)),
            pl.BlockSpec((s, GA_DH), lambda i: (0, 0)),
            pl.BlockSpec((s, GA_DH), lambda i: (0, 0)),
            pl.BlockSpec((1, GA_DH), lambda i: (0, 0)),
            pl.BlockSpec((1, GA_DH), lambda i: (0, 0)),
        ],
        out_specs=pl.BlockSpec((None, s, qw), lambda i: (i, 0, 0)),
        out_shape=jax.ShapeDtypeStruct((b, s, BRANCH_W), BF16),
        scratch_shapes=[pltpu.VMEM((group * s, GA_DH), BF16), pltpu.VMEM((GA_KV_HEADS, s, GA_DH), BF16),
                        pltpu.VMEM((GA_KV_HEADS, s, GA_DH), BF16),
                        pltpu.VMEM((GQA_TQ, s), F32), pltpu.VMEM((GQA_TQ, s), F32),
                        pltpu.VMEM((GQA_TQ, s), BF16), pltpu.VMEM((GQA_TQ, s), BF16),
                        pltpu.VMEM((GQA_TQ, GA_DH), F32), pltpu.VMEM((GQA_TQ, GA_DH), F32)],
        compiler_params=_cparams(("parallel",)),
        name="gqa",
    )(pm3, pm3, pm3, pm3, cos_t, sin_t, qg, kg)


NA_ROW_UNROLL = 16


def _na_kernel(q_ref, k_ref, v_ref, z_ref, bias_ref, qg_ref, kg_ref, y_ref, q_scr, k_scr, v_scr, o_scr):
    s_len = q_ref.shape[0]
    rows = s_len // GRID_W
    kr = min(NA_ROWS, rows)
    hi = lax.broadcasted_iota(jnp.int32, (2 * NA_DH, 2 * NA_DH), 0) >= NA_DH
    hj = lax.broadcasted_iota(jnp.int32, (2 * NA_DH, 2 * NA_DH), 1) >= NA_DH
    same_head = jnp.where(hi == hj, 1.0, 0.0).astype(BF16)

    def rms_pair(x, g):
        x2 = x * x
        x2_hi = x2.astype(BF16)
        x2_lo = (x2 - x2_hi.astype(F32)).astype(BF16)
        ssq = (jnp.dot(x2_hi, same_head, preferred_element_type=F32)
               + jnp.dot(x2_lo, same_head, preferred_element_type=F32))
        return x * lax.rsqrt(ssq * (1.0 / NA_DH) + EPS) * g

    scale = NA_DH ** -0.5
    assert math.log2(scale).is_integer()
    qn = rms_pair(q_ref[...].astype(F32), qg_ref[...]) * scale
    kn = rms_pair(k_ref[...].astype(F32), kg_ref[...])
    vf = v_ref[...].astype(F32)
    for hh in range(2):
        cols = slice(hh * NA_DH, (hh + 1) * NA_DH)
        q_scr[hh] = qn[:, cols].astype(BF16)
        k_scr[hh] = kn[:, cols].astype(BF16)
        v_scr[hh] = vf[:, cols].astype(BF16)
    nkeys = kr * GRID_W

    def body(it, carry):
        units = []
        for u in range(NA_ROW_UNROLL):
            r = it * NA_ROW_UNROLL + u
            r0 = jnp.clip(r - kr // 2, 0, rows - kr)
            var = r0 - r + (NA_ROWS - 1)
            qrows = pl.ds(pl.multiple_of(r * GRID_W, GRID_W), GRID_W)
            krows = pl.ds(pl.multiple_of(r0 * GRID_W, GRID_W), nkeys)
            units += [(hh, var, qrows, krows) for hh in range(2)]
        logits = [_mm_nt(q_scr[hh, qrows, :], k_scr[hh, krows, :]) + bias_ref[hh, var]
                  for hh, var, qrows, krows in units]
        probs = [jnp.exp(s - jnp.max(s, axis=-1, keepdims=True)) for s in logits]
        sums = [jnp.sum(p, axis=-1, keepdims=True) for p in probs]
        outs = [_mm(p, v_scr[hh, krows, :]) for p, (hh, _, _, krows) in zip(probs, units)]
        for o, l, (hh, _, qrows, _) in zip(outs, sums, units):
            o_scr[hh, qrows, :] = o / l
        return carry

    lax.fori_loop(0, rows // NA_ROW_UNROLL, body, 0)
    o = jnp.concatenate([o_scr[0], o_scr[1]], axis=-1)
    y_ref[...] = (o * _silu(z_ref[...].astype(F32))).astype(y_ref.dtype)


def _na_bias_table(rpb, rows):
    kr = min(NA_ROWS, rows)
    c = jnp.arange(GRID_W)
    c0 = jnp.clip(c - NA_COLS // 2, 0, GRID_W - NA_COLS)
    in_win = (c[None, :] >= c0[:, None]) & (c[None, :] < c0[:, None] + NA_COLS)
    col_off = jnp.clip(c[None, :] - c[:, None], -(NA_COLS - 1), NA_COLS - 1) + NA_COLS - 1
    t = jnp.where(in_win, rpb[..., col_off], NEG)
    ro = jnp.arange(NA_ROWS)[:, None] + jnp.arange(kr)[None, :]
    tv = jnp.take(t, ro, axis=-3)
    tv = jnp.swapaxes(tv, -3, -2)
    return tv.reshape(*rpb.shape[:-2], NA_ROWS, GRID_W, kr * GRID_W).astype(F32)


def _natten(pm3, bias, qg, kg, layer):
    b, s, _ = pm3.shape
    pw = 2 * NA_DH
    npair = NA_HEADS // 2
    hw = NA_HEADS * NA_DH
    bias5 = bias.reshape(bias.shape[0], npair, 2, *bias.shape[2:])
    return pl.pallas_call(
        _na_kernel,
        grid=(npair, b),
        in_specs=[
            pl.BlockSpec((None, s, pw), lambda p, i: (i, 0, _OFF["b_qkv"] // pw + p)),
            pl.BlockSpec((None, s, pw), lambda p, i: (i, 0, (_OFF["b_qkv"] + hw) // pw + p)),
            pl.BlockSpec((None, s, pw), lambda p, i: (i, 0, (_OFF["b_qkv"] + 2 * hw) // pw + p)),
            pl.BlockSpec((None, s, pw), lambda p, i: (i, 0, _OFF["b_z"] // pw + p)),
            pl.BlockSpec((None, None) + bias5.shape[2:], lambda p, i: (layer, p, 0, 0, 0, 0)),
            pl.BlockSpec((1, pw), lambda p, i: (0, 0)),
            pl.BlockSpec((1, pw), lambda p, i: (0, 0)),
        ],
        out_specs=pl.BlockSpec((None, s, pw), lambda p, i: (i, 0, p)),
        out_shape=jax.ShapeDtypeStruct((b, s, BRANCH_W), BF16),
        scratch_shapes=[pltpu.VMEM((2, s, NA_DH), BF16), pltpu.VMEM((2, s, NA_DH), BF16),
                        pltpu.VMEM((2, s, NA_DH), BF16), pltpu.VMEM((2, s, NA_DH), F32)],
        compiler_params=_cparams(("parallel", "parallel")),
        name="natten",
    )(pm3, pm3, pm3, pm3, bias5, qg, kg)


INV_BASE = 8
PRE_BLK = 4 * BLK


def _inverse_level_masks():
    i = lax.broadcasted_iota(jnp.int32, (BLK, BLK), 0)
    j = lax.broadcasted_iota(jnp.int32, (BLK, BLK), 1)
    same = lambda size: (i >> int(math.log2(size))) == (j >> int(math.log2(size)))
    base = same(INV_BASE)
    joins = []
    size = INV_BASE
    while size < DN_CHUNK:
        joins.append(same(2 * size) & jnp.logical_not(same(size)))
        size *= 2
    return base, joins


def _tri_inverses(l_mats, eyes, level_masks):
    base, joins = level_masks
    ps = [jnp.where(base, -l, 0.0) for l in l_mats]
    ts = [eye + p for eye, p in zip(eyes, ps)]
    for _ in range(int(math.log2(INV_BASE)) - 1):
        ps = [_mm(p, p) for p in ps]
        ts = [t + _mm(t, p) for t, p in zip(ts, ps)]
    for join in joins:
        mids = [_mm(jnp.where(join, l, 0.0), t) for l, t in zip(l_mats, ts)]
        ts = [t - _mm(t, mid) for t, mid in zip(ts, mids)]
    return ts


DN_HW = DN_HEADS * LANES
PK_W, PK_QD, PK_KD, PK_QK = (i * DN_HW for i in range(4))


def _dn_pre_kernel(qkv_ref, sm_ref, cw_ref, lp_ref, uf_ref, ub_ref, pkf_ref, pkb_ref, gtf_ref, gtb_ref):
    n = pl.program_id(1)
    s_len = qkv_ref.shape[0]
    assert qkv_ref.dtype == BF16
    halo = 2 * SUBLANES
    n_sub = sm_ref.shape[0] // BLK
    chunks_per_blk = BLK // DN_CHUNK
    out_row = lax.broadcasted_iota(jnp.int32, (BLK, BLK + 2 * halo), 0)
    in_row = lax.broadcasted_iota(jnp.int32, (BLK, BLK + 2 * halo), 1)
    shifts = {j: jnp.where(in_row == out_row + (halo + j - DN_CONV // 2), 1.0, 0.0).astype(BF16)
              for j in range(DN_CONV) if j != DN_CONV // 2}
    dir_masks = [_chunk_masks(d == 1, DN_CHUNK) for d in range(2)]
    level_masks = _inverse_level_masks()
    outs = ((uf_ref, pkf_ref, gtf_ref), (ub_ref, pkb_ref, gtb_ref))

    def l2n(x):
        return x * lax.rsqrt(jnp.sum(x * x, axis=-1, keepdims=True) + EPS)

    blocks = []
    for sub in range(n_sub):
        t0 = pl.multiple_of((n * n_sub + sub) * BLK, BLK)
        pstart = pl.multiple_of(jnp.maximum(t0 - halo, 0), halo)
        nstart = pl.multiple_of(jnp.minimum(t0 + BLK, s_len - halo), halo)
        prev = qkv_ref[pl.ds(pstart, halo), :]
        prev = jnp.where(t0 > 0, prev, jnp.zeros_like(prev))
        cur = qkv_ref[pl.ds(t0, BLK), :]
        nxt = qkv_ref[pl.ds(nstart, halo), :]
        nxt = jnp.where(t0 + BLK < s_len, nxt, jnp.zeros_like(nxt))
        xw = jnp.concatenate([prev, cur, nxt], axis=0)
        conv = cur.astype(F32) * cw_ref[DN_CONV // 2:DN_CONV // 2 + 1, :]
        for j, shift in shifts.items():
            conv = conv + jnp.dot(shift, xw, preferred_element_type=F32) * cw_ref[j:j + 1, :]
        conv = _silu(conv)

        rows = slice(sub * BLK, (sub + 1) * BLK)
        sm = sm_ref[rows, :]
        g_all = -jnp.exp(lp_ref[0:1, :]) * _softplus(sm + lp_ref[1:2, :])
        beta_all = _sigmoid(sm)
        per_dir = []
        for d in range(2):
            same, incl, strict, eye_b = dir_masks[d]
            gc = _mask_sum(incl, g_all)
            tot = _mask_sum(same, g_all)
            per_dir.append((incl, strict, _as_f32(eye_b), gc, gc.T, tot))
        blocks.append((sub, rows, conv, beta_all, per_dir))

    heads = []
    for sub, rows, conv, beta_all, per_dir in blocks:
        for h in range(DN_HEADS):
            q = l2n(conv[:, h * DN_DK:(h + 1) * DN_DK]) * (DN_DK ** -0.5)
            k = l2n(conv[:, (DN_HEADS + h) * DN_DK:(DN_HEADS + h + 1) * DN_DK])
            v = conv[:, 2 * DN_HEADS * DN_DK + h * DN_DV:2 * DN_HEADS * DN_DK + (h + 1) * DN_DV]
            heads.append((sub, rows, beta_all, per_dir, h, q, k, v))
    kks = [_mm_nt(hd[6], hd[6]) for hd in heads]
    qks = [_mm_nt(hd[5], hd[6]) for hd in heads]

    l_mats, rhss, eyes, slots = [], [], [], []
    for (sub, rows, beta_all, per_dir, h, q, k, v), kk, qk in zip(heads, kks, qks):
        hc = slice(h * LANES, (h + 1) * LANES)
        for d in range(2):
            incl, strict, eye_f, gc, gct, tot = per_dir[d]
            u_ref, pk_ref, gt_ref = outs[d]
            pk = lambda off: slice(off + h * LANES, off + (h + 1) * LANES)
            c = _L_AA + d * DN_HEADS + h
            gcol = gc[:, c:c + 1]
            grow = gct[c:c + 1, :]
            tcol = tot[:, c:c + 1]
            beta = beta_all[:, _L_AB + d * DN_HEADS + h:_L_AB + d * DN_HEADS + h + 1]
            decay = jnp.where(incl, jnp.exp(jnp.where(incl, gcol - grow, 0.0)), 0.0)
            l_mats.append(jnp.where(strict, beta * kk * decay, 0.0))
            egc = jnp.exp(gcol)
            rhss.append(jnp.concatenate([v * beta, k * (beta * egc)], axis=-1).astype(BF16))
            eyes.append(eye_f)
            slots.append((rows, h, d))
            pk_ref[rows, pk(PK_QD)] = (q * egc).astype(pk_ref.dtype)
            pk_ref[rows, pk(PK_KD)] = (k * jnp.exp(tcol - gcol)).astype(pk_ref.dtype)
            pk_ref[rows, pk(PK_QK)] = (qk * decay).astype(pk_ref.dtype)
            gtot = jnp.exp(tcol)
            for ci in range(chunks_per_blk):
                g0 = (sub * chunks_per_blk + ci) * SUBLANES
                gt_ref[g0:g0 + SUBLANES, hc] = jnp.broadcast_to(
                    gtot[ci * DN_CHUNK:ci * DN_CHUNK + SUBLANES, :], (SUBLANES, LANES))

    t_invs = _tri_inverses(l_mats, eyes, level_masks)
    sols = [_mm(t, rhs) for t, rhs in zip(t_invs, rhss)]
    for (rows, h, d), sol in zip(slots, sols):
        u_ref, pk_ref = outs[d][0], outs[d][1]
        u_ref[rows, h * LANES:(h + 1) * LANES] = sol[:, :DN_DV]
        pk_ref[rows, PK_W + h * LANES:PK_W + (h + 1) * LANES] = sol[:, DN_DV:].astype(pk_ref.dtype)


def _dn_pre(pm3, ps3, conv_w8, lane_params):
    b, s, _ = pm3.shape
    nblk = s // PRE_BLK
    gt_rows = PRE_BLK // DN_CHUNK * SUBLANES
    wq = DN_HEADS * (2 * DN_DK + DN_DV)
    hw = DN_HEADS * LANES
    tok = lambda i, j: (i, j, 0)
    big = lambda dt: jax.ShapeDtypeStruct((b, s, hw), dt)
    gts = jax.ShapeDtypeStruct((b, nblk * gt_rows, hw), F32)
    bs_tok = pl.BlockSpec((None, PRE_BLK, hw), tok)
    bs_gt = pl.BlockSpec((None, gt_rows, hw), tok)
    bs_pk = pl.BlockSpec((None, PRE_BLK, 4 * hw), tok)
    packed = jax.ShapeDtypeStruct((b, s, 4 * hw), BF16)
    return pl.pallas_call(
        _dn_pre_kernel,
        grid=(b, nblk),
        in_specs=[
            pl.BlockSpec((None, s, wq), lambda i, j: (i, 0, _OFF["a_qkv"] // wq)),
            pl.BlockSpec((None, PRE_BLK, LANES), tok),
            pl.BlockSpec((SUBLANES, wq), lambda i, j: (0, 0)),
            pl.BlockSpec((SUBLANES, LANES), lambda i, j: (0, 0)),
        ],
        out_specs=[bs_tok, bs_tok, bs_pk, bs_pk, bs_gt, bs_gt],
        out_shape=[big(F32), big(F32), packed, packed, gts, gts],
        compiler_params=_cparams(("parallel", "arbitrary")),
        name="dn_pre",
    )(pm3, ps3, conv_w8, lane_params)


SCAN_BLK = 4 * BLK


def _dn_scan_kernel(uf_ref, pkf_ref, gtf_ref, ub_ref, pkb_ref, gtb_ref, of_ref, ob_ref, st_scr):
    @pl.when(pl.program_id(1) == 0)
    def _():
        st_scr[...] = jnp.zeros_like(st_scr)

    per_blk = BLK // DN_CHUNK
    nchunk = SCAN_BLK // DN_CHUNK
    zeros_c = jnp.zeros((DN_CHUNK, DN_DV), F32)
    streams = ((uf_ref, pkf_ref, gtf_ref, of_ref, range(nchunk)),
               (ub_ref, pkb_ref, gtb_ref, ob_ref, range(nchunk - 1, -1, -1)))
    chains = [(d, h) + streams[d] for d in range(2) for h in range(DN_HEADS)]
    states = [st_scr[d * DN_HEADS + h] for d, h, *_ in chains]
    for step in range(nchunk):
        rs, v_pads = [], []
        for (d, h, u_ref, pk_ref, gt_ref, o_ref, order), state in zip(chains, states):
            rows = slice(order[step] * DN_CHUNK, (order[step] + 1) * DN_CHUNK)
            w = pk_ref[rows, PK_W + h * LANES:PK_W + (h + 1) * LANES]
            qd = pk_ref[rows, PK_QD + h * LANES:PK_QD + (h + 1) * LANES]
            rs.append(_mm(jnp.concatenate([w, qd], axis=0), state))
        for (d, h, u_ref, pk_ref, gt_ref, o_ref, order), r in zip(chains, rs):
            ci = order[step]
            rows = slice(ci * DN_CHUNK, (ci + 1) * DN_CHUNK)
            parts = [zeros_c] * per_blk
            parts[ci % per_blk] = u_ref[rows, h * LANES:(h + 1) * LANES] - r[:DN_CHUNK]
            v_pads.append(jnp.concatenate(parts, axis=0))
        new_states = []
        for (d, h, u_ref, pk_ref, gt_ref, o_ref, order), r, v_pad, state in zip(chains, rs, v_pads, states):
            hc = slice(h * LANES, (h + 1) * LANES)
            ci = order[step]
            rows = slice(ci * DN_CHUNK, (ci + 1) * DN_CHUNK)
            blk_rows = slice((ci // per_blk) * BLK, (ci // per_blk + 1) * BLK)
            o_ref[rows, hc] = r[DN_CHUNK:] + _mm(pk_ref[rows, PK_QK + h * LANES:PK_QK + (h + 1) * LANES], v_pad)
            gt = gt_ref[ci * SUBLANES:ci * SUBLANES + 1, hc]
            kd = pk_ref[blk_rows, PK_KD + h * LANES:PK_KD + (h + 1) * LANES]
            new_states.append(state * gt + _mm_tn(kd, v_pad))
        states = new_states
    for (d, h, *_), state in zip(chains, states):
        st_scr[d * DN_HEADS + h] = state


def _dn_scan(pre):
    uf, ub, pkf, pkb, gtf, gtb = pre
    b, s, hw = uf.shape
    nblk = s // SCAN_BLK
    gt_rows = SCAN_BLK // DN_CHUNK * SUBLANES
    fwd = lambda i, j: (i, j, 0)
    bwd = lambda i, j: (i, nblk - 1 - j, 0)
    def specs(imap):
        return [pl.BlockSpec((None, SCAN_BLK, hw), imap), pl.BlockSpec((None, SCAN_BLK, 4 * hw), imap),
                pl.BlockSpec((None, gt_rows, hw), imap)]
    return pl.pallas_call(
        _dn_scan_kernel,
        grid=(b, nblk),
        in_specs=specs(fwd) + specs(bwd),
        out_specs=[pl.BlockSpec((None, SCAN_BLK, hw), fwd), pl.BlockSpec((None, SCAN_BLK, hw), bwd)],
        out_shape=[jax.ShapeDtypeStruct((b, s, hw), F32)] * 2,
        scratch_shapes=[pltpu.VMEM((2 * DN_HEADS, DN_DK, DN_DV), F32)],
        compiler_params=_cparams(("parallel", "arbitrary")),
        name="dn_scan",
    )(uf, pkf, gtf, ub, pkb, gtb)


ML_AUG = 2 * LANES
ML_GROUP = 4
ML_BLK = 4 * BLK


def _ml_kernel(qf_ref, kf_ref, vf_ref, smf_ref, qb_ref, kb_ref, vb_ref, smb_ref, lp_ref,
               hf_ref, hb_ref, c_scr, m_scr):
    @pl.when(pl.program_id(1) == 0)
    def _():
        c_scr[...] = jnp.zeros_like(c_scr)
        m_scr[...] = jnp.zeros_like(m_scr)

    nchunk = BLK // ML_CHUNK
    ones_col = jnp.ones((BLK, LANES), BF16)
    zeros_aug = jnp.zeros((ML_CHUNK, ML_AUG), BF16)
    streams = ((qf_ref, kf_ref, vf_ref, smf_ref, hf_ref, range(nchunk)),
               (qb_ref, kb_ref, vb_ref, smb_ref, hb_ref, range(nchunk - 1, -1, -1)))
    lanes = lambda col: jnp.broadcast_to(col, (col.shape[0], LANES))
    n_sub = smf_ref.shape[0] // BLK
    units = [(d, p if d == 0 else n_sub - 1 - p) for p in range(n_sub) for d in range(2)]
    for d, sub in units:
        _ml_block(d, sub, streams[d], lp_ref, c_scr, m_scr, lanes, ones_col, zeros_aug, nchunk)


def _ml_block(d, sub, stream, lp_ref, c_scr, m_scr, lanes, ones_col, zeros_aug, nchunk):
    q_ref, k_ref, v_ref, sm_ref, h_ref, order = stream
    blk_rows = slice(sub * BLK, (sub + 1) * BLK)
    same, incl, _, _ = _chunk_masks(d == 1, ML_CHUNK)
    sm = sm_ref[blk_rows, :]
    ig_all = sm + lp_ref[0:1, :]
    x = sm + lp_ref[1:2, :]
    lf_all = jnp.minimum(x, 0.0) - jnp.log(1.0 + jnp.exp(-jnp.abs(x)))
    lf_all = pltpu.roll(lf_all, LANES - (_L_DF - _L_DI), 1)
    bc_all = _mask_sum(incl, lf_all)
    tot_all = _mask_sum(same, lf_all)
    a_all = ig_all - bc_all
    mwa_all = jnp.concatenate(
        [jnp.broadcast_to(jnp.max(a_all[ci * ML_CHUNK:(ci + 1) * ML_CHUNK], axis=0, keepdims=True),
                          (ML_CHUNK, LANES)) for ci in range(nchunk)], axis=0)
    a_t, w_all, mw_all = a_all.T, jnp.exp(a_all - mwa_all), tot_all + mwa_all

    for chains in [[(d, h) for h in range(g, g + ML_GROUP)] for g in range(0, ML_HEADS, ML_GROUP)]:
        ins = []
        for _, h in chains:
            q = q_ref[blk_rows, h * ML_DK:(h + 1) * ML_DK].astype(BF16)
            k = k_ref[blk_rows, h * ML_DK:(h + 1) * ML_DK].astype(F32) * (ML_DK ** -0.5)
            v_aug = jnp.concatenate([v_ref[blk_rows, h * ML_DV:(h + 1) * ML_DV].astype(BF16), ones_col], axis=-1)
            ins.append((q, k, v_aug))
        qks = [_mm_nt(q, k) for q, k, _ in ins]

        mids = []
        for (d, h), (q, k, v_aug), qk in zip(chains, ins, qks):
            c = _L_DI + d * ML_HEADS + h
            b_l = lanes(bc_all[:, c:c + 1])
            dlog = jnp.where(incl, b_l + a_t[c:c + 1, :], NEG)
            m_intra = lanes(jnp.max(dlog, axis=-1, keepdims=True))
            s_intra = qk * jnp.exp(dlog - m_intra)
            wk = (k * lanes(w_all[:, c:c + 1])[:, :ML_DK]).astype(BF16)
            mids.append((b_l, m_intra, s_intra.astype(BF16), wk))
        p_intras = [_mm(s_b, v_aug) for (_, _, s_b, _), (_, _, v_aug) in zip(mids, ins)]
        kvs = []
        for (_, _, _, wk), (_, _, v_aug) in zip(mids, ins):
            per_chunk = []
            for ci in range(nchunk):
                parts = [zeros_aug] * nchunk
                parts[ci] = v_aug[ci * ML_CHUNK:(ci + 1) * ML_CHUNK]
                per_chunk.append(_mm_tn(wk, jnp.concatenate(parts, axis=0)))
            kvs.append(per_chunk)

        c_sts = [c_scr[d * ML_HEADS + h] for d, h in chains]
        m_sts = [m_scr[d * ML_HEADS + h][0:1, :] for d, h in chains]
        for step in range(nchunk):
            qcs = []
            ci = order[step]
            for (q, _, _), c_st in zip(ins, c_sts):
                qcs.append(_mm(q[ci * ML_CHUNK:(ci + 1) * ML_CHUNK], c_st))
            for idx, (_, h) in enumerate(chains):
                b_l, m_intra, _, _ = mids[idx]
                c = _L_DI + d * ML_HEADS + h
                rows = slice(ci * ML_CHUNK, (ci + 1) * ML_CHUNK)
                out_rows = slice(sub * BLK + ci * ML_CHUNK, sub * BLK + (ci + 1) * ML_CHUNK)
                r8 = slice(ci * ML_CHUNK, ci * ML_CHUNK + SUBLANES)
                m_st, c_st, qc = m_sts[idx], c_sts[idx], qcs[idx]
                m_inter = b_l[rows] + m_st
                m_i = jnp.maximum(m_intra[rows], m_inter)
                f_i = jnp.exp(m_intra[rows] - m_i)
                inter = jnp.exp(m_inter - m_i)
                both = (jnp.concatenate([inter, inter], axis=-1) * qc
                        + jnp.concatenate([f_i, f_i], axis=-1) * p_intras[idx][rows])
                numer, denom = both[:, :ML_DV], both[:, ML_DV:]
                h_ref[out_rows, h * ML_DV:(h + 1) * ML_DV] = numer / jnp.maximum(jnp.abs(denom), jnp.exp(-m_i))
                tot_c = lanes(tot_all[r8, c:c + 1])[0:1]
                mw_c = lanes(mw_all[r8, c:c + 1])[0:1]
                m_new = jnp.maximum(tot_c + m_st, mw_c)
                dec = jnp.exp(tot_c + m_st - m_new)
                gain = jnp.exp(mw_c - m_new)
                c_sts[idx] = (jnp.concatenate([dec, dec], axis=-1) * c_st
                              + jnp.concatenate([gain, gain], axis=-1) * kvs[idx][ci])
                m_sts[idx] = m_new
        for idx, (d, h) in enumerate(chains):
            c_scr[d * ML_HEADS + h] = c_sts[idx]
            m_scr[d * ML_HEADS + h] = jnp.broadcast_to(m_sts[idx], (SUBLANES, LANES))


def _mlstm(pm3, ps3, lane_params):
    b, s, _ = pm3.shape
    nblk = s // ML_BLK
    qw = ML_HEADS * ML_DK
    vw = ML_HEADS * ML_DV
    def specs(tmap):
        blk = lambda j: tmap(j)
        return [
            pl.BlockSpec((None, ML_BLK, qw), lambda i, j: (i, blk(j), _OFF["d_q"] // qw)),
            pl.BlockSpec((None, ML_BLK, qw), lambda i, j: (i, blk(j), _OFF["d_k"] // qw)),
            pl.BlockSpec((None, ML_BLK, vw), lambda i, j: (i, blk(j), _OFF["d_v"] // vw)),
            pl.BlockSpec((None, ML_BLK, LANES), lambda i, j: (i, blk(j), 0)),
        ]
    fwd = lambda j: j
    bwd = lambda j: nblk - 1 - j
    return pl.pallas_call(
        _ml_kernel,
        grid=(b, nblk),
        in_specs=specs(fwd) + specs(bwd) + [pl.BlockSpec((SUBLANES, LANES), lambda i, j: (0, 0))],
        out_specs=[pl.BlockSpec((None, ML_BLK, vw), lambda i, j: (i, j, 0)),
                   pl.BlockSpec((None, ML_BLK, vw), lambda i, j: (i, nblk - 1 - j, 0))],
        out_shape=[jax.ShapeDtypeStruct((b, s, vw), F32)] * 2,
        scratch_shapes=[pltpu.VMEM((2 * ML_HEADS, ML_DK, ML_AUG), F32),
                        pltpu.VMEM((2 * ML_HEADS, SUBLANES, LANES), F32)],
        compiler_params=_cparams(("parallel", "arbitrary")),
        name="mlstm",
    )(pm3, pm3, pm3, ps3, pm3, pm3, pm3, ps3, lane_params)


def _merge_kernel(x_ref, af_ref, ab_ref, df_ref, db_ref, yb_ref, yc_ref, az_ref, dz_ref, do_ref, gl_ref,
                  ag_ref, dg_ref, wb_ref, wo_ref, o_ref):
    d = x_ref.shape[-1]

    def head_rms(x, g):
        outs = []
        for h in range(x.shape[-1] // LANES):
            xh = x[:, h * LANES:(h + 1) * LANES]
            ms = jnp.mean(xh * xh, axis=-1, keepdims=True)
            outs.append(xh * lax.rsqrt(ms + EPS) * g)
        return jnp.concatenate(outs, axis=-1)

    ya = head_rms(af_ref[...] + ab_ref[...], ag_ref[...]) * _silu(az_ref[...].astype(F32))
    yd = _sigmoid(do_ref[...].astype(F32)) * head_rms(df_ref[...] + db_ref[...], dg_ref[...])
    yd = yd * _silu(dz_ref[...].astype(F32))
    twice = None
    for i, y in enumerate((ya.astype(BF16), yb_ref[...], yc_ref[...], yd.astype(BF16))):
        proj = jnp.dot(y, wb_ref[i], preferred_element_type=F32)
        term = proj + jnp.tanh(0.5 * gl_ref[:, i * d:(i + 1) * d].astype(F32)) * proj
        twice = term if twice is None else twice + term
    merged = (0.5 * twice).astype(BF16)
    o_ref[...] = x_ref[...] + jnp.dot(merged, wo_ref[...], preferred_element_type=F32)


def _merge(x2d, af, ab, df, db, yb, yc, pm2, ag, dg, wb, wo, layer, tm=512):
    m, d = x2d.shape
    gw = N_BRANCH * d
    w = BRANCH_W
    tok = pl.BlockSpec((tm, w), lambda i: (i, 0))
    col = lambda name: pl.BlockSpec((tm, w), lambda i: (i, _OFF[name] // w))
    vec = pl.BlockSpec((1, LANES), lambda i: (0, 0))
    return pl.pallas_call(
        _merge_kernel,
        grid=(m // tm,),
        in_specs=[
            pl.BlockSpec((tm, d), lambda i: (i, 0)),
            tok, tok, tok, tok, tok, tok,
            col("a_z"), col("d_z"), col("d_o"),
            pl.BlockSpec((tm, gw), lambda i: (i, _OFF["gate"] // gw)),
            vec, vec,
            pl.BlockSpec((None, N_BRANCH, w, d), lambda i: (layer, 0, 0, 0)),
            pl.BlockSpec((None, d, d), lambda i: (layer, 0, 0)),
        ],
        out_specs=pl.BlockSpec((tm, d), lambda i: (i, 0)),
        out_shape=jax.ShapeDtypeStruct((m, d), F32),
        compiler_params=_cparams(("parallel",)),
        name="merge",
    )(x2d, af, ab, df, db, yb, yc, pm2, pm2, pm2, pm2, ag, dg, wb, wo)


def _rope_lane_tables(s):
    t = jnp.arange(s)
    row = (t // GRID_W).astype(F32)
    col = (t % GRID_W).astype(F32)
    m = GA_DH // 4
    inv = ROPE_THETA ** (-jnp.arange(m, dtype=F32) / m)
    ar = row[:, None] * inv
    ac = col[:, None] * inv
    cos_t = jnp.concatenate([jnp.cos(ar), jnp.cos(ar), jnp.cos(ac), jnp.cos(ac)], axis=-1)
    sin_t = jnp.concatenate([-jnp.sin(ar), jnp.sin(ar), -jnp.sin(ac), jnp.sin(ac)], axis=-1)
    return cos_t.astype(F32), sin_t.astype(F32)


def _lane_tiles(rows):
    padded = []
    for off, vals in rows:
        vals = vals.reshape(vals.shape[0], 1, -1).astype(F32)
        padded.append(jnp.pad(vals, ((0, 0), (0, 0), (off, LANES - off - vals.shape[-1]))))
    tiles = jnp.concatenate(padded, axis=1)
    return jnp.pad(tiles, ((0, 0), (0, SUBLANES - len(rows)), (0, 0)))


def kernel(x, norm_g, w_in, conv_a, dn_a_log, dn_dt_bias, dn_norm_g, na_q_norm, na_k_norm, na_rpb,
           ga_q_norm, ga_k_norm, ml_i_bias, ml_f_bias, ml_norm_g, w_branch, w_out):
    b, s, d = x.shape
    depth = w_in.shape[0]
    hw = BRANCH_W
    cos_t, sin_t = _rope_lane_tables(s)
    w_main = jnp.concatenate([w_in[:, :, o:o + wd] for _, o, wd in _MAIN_SEGS], axis=2).astype(BF16)
    w_small = jnp.pad(jnp.concatenate([w_in[:, :, o:o + 8] for o in _SMALL_SRC], axis=2),
                      ((0, 0), (0, 0), (0, LANES - 8 * len(_SMALL_SRC)))).astype(BF16)
    conv8 = jnp.pad(conv_a.astype(F32), ((0, 0), (0, SUBLANES - DN_CONV), (0, 0)))
    dn_lp = _lane_tiles([(_L_AA, dn_a_log), (_L_AA, dn_dt_bias)])
    ml_lp = _lane_tiles([(_L_DI, ml_i_bias), (_L_DF, ml_f_bias)])
    na_bias = _na_bias_table(na_rpb, s // GRID_W)
    na_qg = jnp.tile(na_q_norm, (1, 2)).reshape(depth, 1, 2 * NA_DH)
    na_kg = jnp.tile(na_k_norm, (1, 2)).reshape(depth, 1, 2 * NA_DH)
    wb_bf, wo_bf = w_branch.astype(BF16), w_out.astype(BF16)

    x2 = x.reshape(b * s, d)
    for l in range(depth):
        pm2, ps2 = _inproj(x2, norm_g[l].reshape(1, d), w_main, w_small, l)
        pm3 = pm2.reshape(b, s, N_MAIN)
        ps3 = ps2.reshape(b, s, LANES)
        o_af, o_ab = _dn_scan(_dn_pre(pm3, ps3, conv8[l], dn_lp[l]))
        h_df, h_db = _mlstm(pm3, ps3, ml_lp[l])
        yb = _natten(pm3, na_bias, na_qg[l], na_kg[l], l)
        yc = _gqa(pm3, cos_t, sin_t, ga_q_norm[l].reshape(1, GA_DH), ga_k_norm[l].reshape(1, GA_DH))
        x2 = _merge(x2, o_af.reshape(b * s, hw), o_ab.reshape(b * s, hw), h_df.reshape(b * s, hw),
                    h_db.reshape(b * s, hw), yb.reshape(b * s, hw), yc.reshape(b * s, hw), pm2,
                    dn_norm_g[l].reshape(1, LANES), ml_norm_g[l].reshape(1, LANES), wb_bf, wo_bf, l)
    return x2.reshape(b, s, d)
```

```python
import functools
import math

import jax
import jax.numpy as jnp
from jax import lax
from jax.experimental import pallas as pl
from jax.experimental.pallas import tpu as pltpu

F32 = jnp.float32
BF16 = jnp.bfloat16

D_MODEL = 1024
GRID_W = 64
N_BRANCH = 4
BRANCH_W = 512
EPS = 1e-6
DN_HEADS, DN_DK, DN_DV, DN_CONV, DN_CHUNK = 4, 128, 128, 5, 64
NA_HEADS, NA_DH, NA_ROWS, NA_COLS = 8, 64, 8, 16
GA_HEADS, GA_KV_HEADS, GA_DH = 4, 2, 128
ROPE_THETA = 10000.0
ML_HEADS, ML_DK, ML_DV, ML_CHUNK = 4, 64, 128, 128

LANES = 128
SUBLANES = 8
VMEM_LIMIT_BYTES = 56 * 1024 * 1024

_O_A_QKV, _O_A_A, _O_A_B, _O_A_Z = 0, 1536, 1544, 1552
_O_B_QKV, _O_B_Z = 2064, 3600
_O_C_Q, _O_C_K, _O_C_V, _O_C_Z = 4112, 4624, 4880, 5136
_O_D_Q, _O_D_K, _O_D_V, _O_D_I, _O_D_F, _O_D_O, _O_D_Z = 5648, 5904, 6160, 6672, 6680, 6688, 7200
_O_GATE = 7712
_MAIN_SEGS = (
    ("a_qkv", _O_A_QKV, 1536), ("b_qkv", _O_B_QKV, 1536), ("a_z", _O_A_Z, 512), ("b_z", _O_B_Z, 512),
    ("gate", _O_GATE, 4096), ("c_q", _O_C_Q, 512), ("c_k", _O_C_K, 256), ("c_v", _O_C_V, 256),
    ("c_z", _O_C_Z, 512), ("d_q", _O_D_Q, 256), ("d_k", _O_D_K, 256), ("d_v", _O_D_V, 512),
    ("d_o", _O_D_O, 512), ("d_z", _O_D_Z, 512),
)
_OFF = {}
_o = 0
for _name, _src, _w in _MAIN_SEGS:
    _OFF[_name] = _o
    _o += _w
N_MAIN = _o
_SMALL_SRC = (_O_A_A, _O_A_B, _O_D_I, _O_D_F)
_L_AA, _L_AB, _L_DI, _L_DF = 0, 8, 16, 24

P_DTYPE = BF16
BLK = 128
NEG = -1e30


def _cparams(sem):
    return pltpu.CompilerParams(dimension_semantics=sem, vmem_limit_bytes=VMEM_LIMIT_BYTES)


def _sigmoid(x):
    return 0.5 * jnp.tanh(0.5 * x) + 0.5


def _silu(x):
    return x * _sigmoid(x)


def _softplus(x):
    return jnp.maximum(x, 0.0) + jnp.log(1.0 + jnp.exp(-jnp.abs(x)))


def _mm(a, b):
    return jnp.dot(a.astype(BF16), b.astype(BF16), preferred_element_type=F32)


def _mm_nt(a, b):
    return lax.dot_general(a.astype(BF16), b.astype(BF16), (((1,), (1,)), ((), ())),
                           preferred_element_type=F32)


def _mm_tn(a, b):
    return lax.dot_general(a.astype(BF16), b.astype(BF16), (((0,), (0,)), ((), ())),
                           preferred_element_type=F32)


def _mask_sum(mask, x):
    m = jnp.where(mask, 1.0, 0.0).astype(BF16)
    x1 = x.astype(BF16)
    r1 = x - x1.astype(F32)
    x2 = r1.astype(BF16)
    x3 = (r1 - x2.astype(F32)).astype(BF16)
    dot = lambda v: jnp.dot(m, v, preferred_element_type=F32)
    return dot(x1) + (dot(x2) + dot(x3))


def _chunk_masks(reverse, chunk):
    i = lax.broadcasted_iota(jnp.int32, (BLK, BLK), 0)
    j = lax.broadcasted_iota(jnp.int32, (BLK, BLK), 1)
    shift = int(math.log2(chunk))
    same = (i >> shift) == (j >> shift)
    if reverse:
        incl = same & (j >= i)
        strict = same & (j > i)
    else:
        incl = same & (j <= i)
        strict = same & (j < i)
    return same, incl, strict, (i == j)


def _as_f32(mask):
    return jnp.where(mask, 1.0, 0.0).astype(F32)


def _inproj_kernel(x_ref, g_ref, w_ref, ws_ref, pm_ref, ps_ref, h_scr):
    @pl.when(pl.program_id(1) == 0)
    def _():
        x = x_ref[...]
        ms = jnp.mean(x * x, axis=-1, keepdims=True)
        h = (x * lax.rsqrt(ms + EPS) * g_ref[...]).astype(BF16)
        h_scr[...] = h
        ps_ref[...] = jnp.dot(h, ws_ref[...], preferred_element_type=F32)

    pm_ref[...] = jnp.dot(h_scr[...], w_ref[...], preferred_element_type=F32).astype(pm_ref.dtype)


def _inproj(x2d, g, w_main, w_small, layer, tm=1024, tn=N_MAIN // 4):
    m, d = x2d.shape
    tm = min(tm, m)
    return pl.pallas_call(
        _inproj_kernel,
        grid=(m // tm, N_MAIN // tn),
        in_specs=[
            pl.BlockSpec((tm, d), lambda i, j: (i, 0)),
            pl.BlockSpec((1, d), lambda i, j: (0, 0)),
            pl.BlockSpec((None, d, tn), lambda i, j: (layer, 0, j)),
            pl.BlockSpec((None, d, LANES), lambda i, j: (layer, 0, 0)),
        ],
        out_specs=[
            pl.BlockSpec((tm, tn), lambda i, j: (i, j)),
            pl.BlockSpec((tm, LANES), lambda i, j: (i, 0)),
        ],
        out_shape=[jax.ShapeDtypeStruct((m, N_MAIN), P_DTYPE), jax.ShapeDtypeStruct((m, LANES), F32)],
        scratch_shapes=[pltpu.VMEM((tm, d), BF16)],
        compiler_params=_cparams(("parallel", "arbitrary")),
        name="inproj",
    )(x2d, g, w_main, w_small)


GQA_TQ = 512


def _gqa_kernel(q_ref, k_ref, v_ref, z_ref, cos_ref, sin_ref, qg_ref, kg_ref, y_ref,
                q_scr, k_scr, v_scr, s0_scr, s1_scr, p0_scr, p1_scr, l0_scr, l1_scr):
    def norm_rope(x, g, cos, sin):
        ms = jnp.mean(x * x, axis=-1, keepdims=True)
        xn = x * lax.rsqrt(ms + EPS) * g
        lane = lax.broadcasted_iota(jnp.int32, xn.shape, 1)
        partner = jnp.where((lane & 63) < 32, pltpu.roll(xn, LANES - 32, 1), pltpu.roll(xn, 32, 1))
        return xn * cos + partner * sin

    s_len = k_ref.shape[0]
    group = GA_HEADS // GA_KV_HEADS
    assert group == 2
    scale = GA_DH ** -0.5
    n_blk = s_len // GQA_TQ
    s_bufs, p_bufs, l_bufs = (s0_scr, s1_scr), (p0_scr, p1_scr), (l0_scr, l1_scr)
    for kv in range(k_ref.shape[1] // GA_DH):
        _gqa_pipeline(kv, group, s_len, n_blk, scale, norm_rope, q_ref, k_ref, v_ref, z_ref, cos_ref, sin_ref,
                      qg_ref, kg_ref, y_ref, q_scr, k_scr, v_scr, s_bufs, p_bufs, l_bufs)


def _gqa_pipeline(kv, group, s_len, n_blk, scale, norm_rope, q_ref, k_ref, v_ref, z_ref, cos_ref, sin_ref,
                  qg_ref, kg_ref, y_ref, q_scr, k_scr, v_scr, s_bufs, p_bufs, l_bufs):
    kcols = slice(kv * GA_DH, (kv + 1) * GA_DH)
    k_scr[kv] = norm_rope(k_ref[:, kcols].astype(F32), kg_ref[...], cos_ref[...], sin_ref[...]).astype(BF16)
    v_scr[kv] = v_ref[:, kcols].astype(BF16)

    def stacked(head, blk):
        return pl.ds(pl.multiple_of(head * s_len + blk * GQA_TQ, GQA_TQ), GQA_TQ)

    def head_cols(head):
        return slice((kv * group + head) * GA_DH, (kv * group + head + 1) * GA_DH)

    def prep(head, blk):
        rows = pl.ds(pl.multiple_of(blk * GQA_TQ, GQA_TQ), GQA_TQ)
        q = norm_rope(q_ref[rows, head_cols(head)].astype(F32), qg_ref[...], cos_ref[rows, :], sin_ref[rows, :])
        q_scr[stacked(head, blk), :] = q.astype(BF16)

    def logits(head, blk):
        s_bufs[head][...] = _mm_nt(q_scr[stacked(head, blk), :], k_scr[kv]) * scale

    def softmax(slot):
        s = s_bufs[slot][...]
        p = jnp.exp(s - jnp.max(s, axis=-1, keepdims=True))
        l_bufs[slot][...] = jnp.broadcast_to(jnp.sum(p, axis=-1, keepdims=True), (GQA_TQ, GA_DH))
        p_bufs[slot][...] = p.astype(BF16)

    def weighted(head, blk):
        rows = pl.ds(pl.multiple_of(blk * GQA_TQ, GQA_TQ), GQA_TQ)
        cols = head_cols(head)
        o = _mm(p_bufs[head][...], v_scr[kv]) / l_bufs[head][...]
        y_ref[rows, cols] = (o * _silu(z_ref[rows, cols].astype(F32))).astype(y_ref.dtype)

    prep(0, 0)
    prep(1, 0)
    prep(0, 1)
    logits(0, 0)
    logits(1, 0)
    softmax(0)

    def body(j, carry):
        logits(0, j + 1)
        softmax(1)
        weighted(0, j)
        prep(1, j + 1)
        logits(1, j + 1)
        softmax(0)
        weighted(1, j)
        prep(0, jnp.minimum(j + 2, n_blk - 1))
        return carry

    lax.fori_loop(0, n_blk - 1, body, 0)
    softmax(1)
    weighted(0, n_blk - 1)
    weighted(1, n_blk - 1)


def _gqa(pm3, cos_t, sin_t, qg, kg):
    b, s, _ = pm3.shape
    qw = GA_HEADS * GA_DH
    kw = GA_KV_HEADS * GA_DH
    group = GA_HEADS // GA_KV_HEADS
    return pl.pallas_call(
        _gqa_kernel,
        grid=(b,),
        in_specs=[
            pl.BlockSpec((None, s, qw), lambda i: (i, 0, _OFF["c_q"] // qw)),
            pl.BlockSpec((None, s, kw), lambda i: (i, 0, _OFF["c_k"] // kw)),
            pl.BlockSpec((None, s, kw), lambda i: (i, 0, _OFF["c_v"] // kw)),
            pl.BlockSpec((None, s, qw), lambda i: (i, 0, _OFF["c_z"] // qw)),
            pl.BlockSpec((s, GA_DH), lambda i: (0, 0)),
            pl.BlockSpec((s, GA_DH), lambda i: (0, 0)),
            pl.BlockSpec((1, GA_DH), lambda i: (0, 0)),
            pl.BlockSpec((1, GA_DH), lambda i: (0, 0)),
        ],
        out_specs=pl.BlockSpec((None, s, qw), lambda i: (i, 0, 0)),
        out_shape=jax.ShapeDtypeStruct((b, s, BRANCH_W), BF16),
        scratch_shapes=[pltpu.VMEM((group * s, GA_DH), BF16), pltpu.VMEM((GA_KV_HEADS, s, GA_DH), BF16),
                        pltpu.VMEM((GA_KV_HEADS, s, GA_DH), BF16),
                        pltpu.VMEM((GQA_TQ, s), F32), pltpu.VMEM((GQA_TQ, s), F32),
                        pltpu.VMEM((GQA_TQ, s), BF16), pltpu.VMEM((GQA_TQ, s), BF16),
                        pltpu.VMEM((GQA_TQ, GA_DH), F32), pltpu.VMEM((GQA_TQ, GA_DH), F32)],
        compiler_params=_cparams(("parallel",)),
        name="gqa",
    )(pm3, pm3, pm3, pm3, cos_t, sin_t, qg, kg)


NA_ROW_UNROLL = 16


def _na_kernel(q_ref, k_ref, v_ref, z_ref, bias_ref, qg_ref, kg_ref, y_ref, q_scr, k_scr, v_scr, o_scr):
    s_len = q_ref.shape[0]
    rows = s_len // GRID_W
    kr = min(NA_ROWS, rows)
    hi = lax.broadcasted_iota(jnp.int32, (2 * NA_DH, 2 * NA_DH), 0) >= NA_DH
    hj = lax.broadcasted_iota(jnp.int32, (2 * NA_DH, 2 * NA_DH), 1) >= NA_DH
    same_head = jnp.where(hi == hj, 1.0, 0.0).astype(BF16)

    def rms_pair(x, g):
        x2 = x * x
        x2_hi = x2.astype(BF16)
        x2_lo = (x2 - x2_hi.astype(F32)).astype(BF16)
        ssq = (jnp.dot(x2_hi, same_head, preferred_element_type=F32)
               + jnp.dot(x2_lo, same_head, preferred_element_type=F32))
        return x * lax.rsqrt(ssq * (1.0 / NA_DH) + EPS) * g

    scale = NA_DH ** -0.5
    assert math.log2(scale).is_integer()
    nkeys = kr * GRID_W
    for pair in range(q_ref.shape[1] // (2 * NA_DH)):
        pc = slice(pair * 2 * NA_DH, (pair + 1) * 2 * NA_DH)
        qn = rms_pair(q_ref[:, pc].astype(F32), qg_ref[...]) * scale
        kn = rms_pair(k_ref[:, pc].astype(F32), kg_ref[...])
        vf = v_ref[:, pc].astype(F32)
        for hh in range(2):
            cols = slice(hh * NA_DH, (hh + 1) * NA_DH)
            q_scr[hh] = qn[:, cols].astype(BF16)
            k_scr[hh] = kn[:, cols].astype(BF16)
            v_scr[hh] = vf[:, cols].astype(BF16)

        def body(it, carry, pair=pair):
            units = []
            for u in range(NA_ROW_UNROLL):
                r = it * NA_ROW_UNROLL + u
                r0 = jnp.clip(r - kr // 2, 0, rows - kr)
                var = r0 - r + (NA_ROWS - 1)
                qrows = pl.ds(pl.multiple_of(r * GRID_W, GRID_W), GRID_W)
                krows = pl.ds(pl.multiple_of(r0 * GRID_W, GRID_W), nkeys)
                units += [(hh, var, qrows, krows) for hh in range(2)]
            logits = [_mm_nt(q_scr[hh, qrows, :], k_scr[hh, krows, :]) + bias_ref[pair, hh, var]
                      for hh, var, qrows, krows in units]
            probs = [jnp.exp(s - jnp.max(s, axis=-1, keepdims=True)) for s in logits]
            sums = [jnp.sum(p, axis=-1, keepdims=True) for p in probs]
            outs = [_mm(p, v_scr[hh, krows, :]) for p, (hh, _, _, krows) in zip(probs, units)]
            for o, l, (hh, _, qrows, _) in zip(outs, sums, units):
                o_scr[hh, qrows, :] = o / l
            return carry

        lax.fori_loop(0, rows // NA_ROW_UNROLL, body, 0)
        o = jnp.concatenate([o_scr[0], o_scr[1]], axis=-1)
        y_ref[:, pc] = (o * _silu(z_ref[:, pc].astype(F32))).astype(y_ref.dtype)


def _na_bias_table(rpb, rows):
    kr = min(NA_ROWS, rows)
    c = jnp.arange(GRID_W)
    c0 = jnp.clip(c - NA_COLS // 2, 0, GRID_W - NA_COLS)
    in_win = (c[None, :] >= c0[:, None]) & (c[None, :] < c0[:, None] + NA_COLS)
    col_off = jnp.clip(c[None, :] - c[:, None], -(NA_COLS - 1), NA_COLS - 1) + NA_COLS - 1
    t = jnp.where(in_win, rpb[..., col_off], NEG)
    ro = jnp.arange(NA_ROWS)[:, None] + jnp.arange(kr)[None, :]
    tv = jnp.take(t, ro, axis=-3)
    tv = jnp.swapaxes(tv, -3, -2)
    return tv.reshape(*rpb.shape[:-2], NA_ROWS, GRID_W, kr * GRID_W).astype(F32)


def _natten(pm3, bias, qg, kg, layer):
    b, s, _ = pm3.shape
    pw = 2 * NA_DH
    npair = NA_HEADS // 2
    hw = NA_HEADS * NA_DH
    bias5 = bias.reshape(bias.shape[0], npair, 2, *bias.shape[2:])
    return pl.pallas_call(
        _na_kernel,
        grid=(b,),
        in_specs=[
            pl.BlockSpec((None, s, hw), lambda i: (i, 0, _OFF["b_qkv"] // hw)),
            pl.BlockSpec((None, s, hw), lambda i: (i, 0, _OFF["b_qkv"] // hw + 1)),
            pl.BlockSpec((None, s, hw), lambda i: (i, 0, _OFF["b_qkv"] // hw + 2)),
            pl.BlockSpec((None, s, hw), lambda i: (i, 0, _OFF["b_z"] // hw)),
            pl.BlockSpec((None,) + bias5.shape[1:], lambda i: (layer, 0, 0, 0, 0, 0)),
            pl.BlockSpec((1, pw), lambda i: (0, 0)),
            pl.BlockSpec((1, pw), lambda i: (0, 0)),
        ],
        out_specs=pl.BlockSpec((None, s, hw), lambda i: (i, 0, 0)),
        out_shape=jax.ShapeDtypeStruct((b, s, BRANCH_W), BF16),
        scratch_shapes=[pltpu.VMEM((2, s, NA_DH), BF16), pltpu.VMEM((2, s, NA_DH), BF16),
                        pltpu.VMEM((2, s, NA_DH), BF16), pltpu.VMEM((2, s, NA_DH), F32)],
        compiler_params=_cparams(("parallel",)),
        name="natten",
    )(pm3, pm3, pm3, pm3, bias5, qg, kg)


INV_BASE = 8
PRE_BLK = 4 * BLK


def _inverse_level_masks():
    i = lax.broadcasted_iota(jnp.int32, (BLK, BLK), 0)
    j = lax.broadcasted_iota(jnp.int32, (BLK, BLK), 1)
    same = lambda size: (i >> int(math.log2(size))) == (j >> int(math.log2(size)))
    base = same(INV_BASE)
    joins = []
    size = INV_BASE
    while size < DN_CHUNK:
        joins.append(same(2 * size) & jnp.logical_not(same(size)))
        size *= 2
    return base, joins


def _tri_inverses(l_mats, eyes, level_masks):
    base, joins = level_masks
    ps = [jnp.where(base, -l, 0.0) for l in l_mats]
    ts = [eye + p for eye, p in zip(eyes, ps)]
    for _ in range(int(math.log2(INV_BASE)) - 1):
        ps = [_mm(p, p) for p in ps]
        ts = [t + _mm(t, p) for t, p in zip(ts, ps)]
    for join in joins:
        mids = [_mm(jnp.where(join, l, 0.0), t) for l, t in zip(l_mats, ts)]
        ts = [t - _mm(t, mid) for t, mid in zip(ts, mids)]
    return ts


DN_HW = DN_HEADS * LANES
PK_W, PK_QD, PK_KD, PK_QK = (i * DN_HW for i in range(4))


def _dn_pre_kernel(qkv_ref, sm_ref, cw_ref, lp_ref, uf_ref, ub_ref, pkf_ref, pkb_ref, gtf_ref, gtb_ref):
    n = pl.program_id(1)
    s_len = qkv_ref.shape[0]
    assert qkv_ref.dtype == BF16
    halo = 2 * SUBLANES
    n_sub = sm_ref.shape[0] // BLK
    chunks_per_blk = BLK // DN_CHUNK
    out_row = lax.broadcasted_iota(jnp.int32, (BLK, BLK + 2 * halo), 0)
    in_row = lax.broadcasted_iota(jnp.int32, (BLK, BLK + 2 * halo), 1)
    shifts = {j: jnp.where(in_row == out_row + (halo + j - DN_CONV // 2), 1.0, 0.0).astype(BF16)
              for j in range(DN_CONV) if j != DN_CONV // 2}
    dir_masks = [_chunk_masks(d == 1, DN_CHUNK) for d in range(2)]
    level_masks = _inverse_level_masks()
    outs = ((uf_ref, pkf_ref, gtf_ref), (ub_ref, pkb_ref, gtb_ref))

    def l2n(x):
        return x * lax.rsqrt(jnp.sum(x * x, axis=-1, keepdims=True) + EPS)

    blocks = []
    for sub in range(n_sub):
        t0 = pl.multiple_of((n * n_sub + sub) * BLK, BLK)
        pstart = pl.multiple_of(jnp.maximum(t0 - halo, 0), halo)
        nstart = pl.multiple_of(jnp.minimum(t0 + BLK, s_len - halo), halo)
        prev = qkv_ref[pl.ds(pstart, halo), :]
        prev = jnp.where(t0 > 0, prev, jnp.zeros_like(prev))
        cur = qkv_ref[pl.ds(t0, BLK), :]
        nxt = qkv_ref[pl.ds(nstart, halo), :]
        nxt = jnp.where(t0 + BLK < s_len, nxt, jnp.zeros_like(nxt))
        xw = jnp.concatenate([prev, cur, nxt], axis=0)
        conv = cur.astype(F32) * cw_ref[DN_CONV // 2:DN_CONV // 2 + 1, :]
        for j, shift in shifts.items():
            conv = conv + jnp.dot(shift, xw, preferred_element_type=F32) * cw_ref[j:j + 1, :]
        conv = _silu(conv)

        rows = slice(sub * BLK, (sub + 1) * BLK)
        sm = sm_ref[rows, :]
        g_all = -jnp.exp(lp_ref[0:1, :]) * _softplus(sm + lp_ref[1:2, :])
        beta_all = _sigmoid(sm)
        per_dir = []
        for d in range(2):
            same, incl, strict, eye_b = dir_masks[d]
            gc = _mask_sum(incl, g_all)
            tot = _mask_sum(same, g_all)
            per_dir.append((incl, strict, _as_f32(eye_b), gc, gc.T, tot))
        blocks.append((sub, rows, conv, beta_all, per_dir))

    heads = []
    for sub, rows, conv, beta_all, per_dir in blocks:
        for h in range(DN_HEADS):
            q = l2n(conv[:, h * DN_DK:(h + 1) * DN_DK]) * (DN_DK ** -0.5)
            k = l2n(conv[:, (DN_HEADS + h) * DN_DK:(DN_HEADS + h + 1) * DN_DK])
            v = conv[:, 2 * DN_HEADS * DN_DK + h * DN_DV:2 * DN_HEADS * DN_DK + (h + 1) * DN_DV]
            heads.append((sub, rows, beta_all, per_dir, h, q, k, v))
    kks = [_mm_nt(hd[6], hd[6]) for hd in heads]
    qks = [_mm_nt(hd[5], hd[6]) for hd in heads]

    l_mats, rhss, eyes, slots = [], [], [], []
    for (sub, rows, beta_all, per_dir, h, q, k, v), kk, qk in zip(heads, kks, qks):
        hc = slice(h * LANES, (h + 1) * LANES)
        for d in range(2):
            incl, strict, eye_f, gc, gct, tot = per_dir[d]
            u_ref, pk_ref, gt_ref = outs[d]
            pk = lambda off: slice(off + h * LANES, off + (h + 1) * LANES)
            c = _L_AA + d * DN_HEADS + h
            gcol = gc[:, c:c + 1]
            grow = gct[c:c + 1, :]
            tcol = tot[:, c:c + 1]
            beta = beta_all[:, _L_AB + d * DN_HEADS + h:_L_AB + d * DN_HEADS + h + 1]
            decay = jnp.where(incl, jnp.exp(jnp.where(incl, gcol - grow, 0.0)), 0.0)
            l_mats.append(jnp.where(strict, beta * kk * decay, 0.0))
            egc = jnp.exp(gcol)
            rhss.append(jnp.concatenate([v * beta, k * (beta * egc)], axis=-1).astype(BF16))
            eyes.append(eye_f)
            slots.append((rows, h, d))
            pk_ref[rows, pk(PK_QD)] = (q * egc).astype(pk_ref.dtype)
            pk_ref[rows, pk(PK_KD)] = (k * jnp.exp(tcol - gcol)).astype(pk_ref.dtype)
            pk_ref[rows, pk(PK_QK)] = (qk * decay).astype(pk_ref.dtype)
            gtot = jnp.exp(tcol)
            for ci in range(chunks_per_blk):
                g0 = (sub * chunks_per_blk + ci) * SUBLANES
                gt_ref[g0:g0 + SUBLANES, hc] = jnp.broadcast_to(
                    gtot[ci * DN_CHUNK:ci * DN_CHUNK + SUBLANES, :], (SUBLANES, LANES))

    t_invs = _tri_inverses(l_mats, eyes, level_masks)
    sols = [_mm(t, rhs) for t, rhs in zip(t_invs, rhss)]
    for (rows, h, d), sol in zip(slots, sols):
        u_ref, pk_ref = outs[d][0], outs[d][1]
        u_ref[rows, h * LANES:(h + 1) * LANES] = sol[:, :DN_DV]
        pk_ref[rows, PK_W + h * LANES:PK_W + (h + 1) * LANES] = sol[:, DN_DV:].astype(pk_ref.dtype)


def _dn_pre(pm3, ps3, conv_w8, lane_params):
    b, s, _ = pm3.shape
    nblk = s // PRE_BLK
    gt_rows = PRE_BLK // DN_CHUNK * SUBLANES
    wq = DN_HEADS * (2 * DN_DK + DN_DV)
    hw = DN_HEADS * LANES
    tok = lambda i, j: (i, j, 0)
    big = lambda dt: jax.ShapeDtypeStruct((b, s, hw), dt)
    gts = jax.ShapeDtypeStruct((b, nblk * gt_rows, hw), F32)
    bs_tok = pl.BlockSpec((None, PRE_BLK, hw), tok)
    bs_gt = pl.BlockSpec((None, gt_rows, hw), tok)
    bs_pk = pl.BlockSpec((None, PRE_BLK, 4 * hw), tok)
    packed = jax.ShapeDtypeStruct((b, s, 4 * hw), BF16)
    return pl.pallas_call(
        _dn_pre_kernel,
        grid=(b, nblk),
        in_specs=[
            pl.BlockSpec((None, s, wq), lambda i, j: (i, 0, _OFF["a_qkv"] // wq)),
            pl.BlockSpec((None, PRE_BLK, LANES), tok),
            pl.BlockSpec((SUBLANES, wq), lambda i, j: (0, 0)),
            pl.BlockSpec((SUBLANES, LANES), lambda i, j: (0, 0)),
        ],
        out_specs=[bs_tok, bs_tok, bs_pk, bs_pk, bs_gt, bs_gt],
        out_shape=[big(F32), big(F32), packed, packed, gts, gts],
        compiler_params=_cparams(("parallel", "arbitrary")),
        name="dn_pre",
    )(pm3, ps3, conv_w8, lane_params)


SCAN_BLK = 4 * BLK


def _dn_scan_kernel(uf_ref, pkf_ref, gtf_ref, ub_ref, pkb_ref, gtb_ref, of_ref, ob_ref, st_scr):
    @pl.when(pl.program_id(1) == 0)
    def _():
        st_scr[...] = jnp.zeros_like(st_scr)

    per_blk = BLK // DN_CHUNK
    nchunk = SCAN_BLK // DN_CHUNK
    zeros_c = jnp.zeros((DN_CHUNK, DN_DV), F32)
    streams = ((uf_ref, pkf_ref, gtf_ref, of_ref, range(nchunk)),
               (ub_ref, pkb_ref, gtb_ref, ob_ref, range(nchunk - 1, -1, -1)))
    chains = [(d, h) + streams[d] for d in range(2) for h in range(DN_HEADS)]
    states = [st_scr[d * DN_HEADS + h] for d, h, *_ in chains]
    for step in range(nchunk):
        rs, v_pads = [], []
        for (d, h, u_ref, pk_ref, gt_ref, o_ref, order), state in zip(chains, states):
            rows = slice(order[step] * DN_CHUNK, (order[step] + 1) * DN_CHUNK)
            w = pk_ref[rows, PK_W + h * LANES:PK_W + (h + 1) * LANES]
            qd = pk_ref[rows, PK_QD + h * LANES:PK_QD + (h + 1) * LANES]
            rs.append(_mm(jnp.concatenate([w, qd], axis=0), state))
        for (d, h, u_ref, pk_ref, gt_ref, o_ref, order), r in zip(chains, rs):
            ci = order[step]
            rows = slice(ci * DN_CHUNK, (ci + 1) * DN_CHUNK)
            parts = [zeros_c] * per_blk
            parts[ci % per_blk] = u_ref[rows, h * LANES:(h + 1) * LANES] - r[:DN_CHUNK]
            v_pads.append(jnp.concatenate(parts, axis=0))
        new_states = []
        for (d, h, u_ref, pk_ref, gt_ref, o_ref, order), r, v_pad, state in zip(chains, rs, v_pads, states):
            hc = slice(h * LANES, (h + 1) * LANES)
            ci = order[step]
            rows = slice(ci * DN_CHUNK, (ci + 1) * DN_CHUNK)
            blk_rows = slice((ci // per_blk) * BLK, (ci // per_blk + 1) * BLK)
            o_ref[rows, hc] = r[DN_CHUNK:] + _mm(pk_ref[rows, PK_QK + h * LANES:PK_QK + (h + 1) * LANES], v_pad)
            gt = gt_ref[ci * SUBLANES:ci * SUBLANES + 1, hc]
            kd = pk_ref[blk_rows, PK_KD + h * LANES:PK_KD + (h + 1) * LANES]
            new_states.append(state * gt + _mm_tn(kd, v_pad))
        states = new_states
    for (d, h, *_), state in zip(chains, states):
        st_scr[d * DN_HEADS + h] = state


def _dn_scan(pre):
    uf, ub, pkf, pkb, gtf, gtb = pre
    b, s, hw = uf.shape
    nblk = s // SCAN_BLK
    gt_rows = SCAN_BLK // DN_CHUNK * SUBLANES
    fwd = lambda i, j: (i, j, 0)
    bwd = lambda i, j: (i, nblk - 1 - j, 0)
    def specs(imap):
        return [pl.BlockSpec((None, SCAN_BLK, hw), imap), pl.BlockSpec((None, SCAN_BLK, 4 * hw), imap),
                pl.BlockSpec((None, gt_rows, hw), imap)]
    return pl.pallas_call(
        _dn_scan_kernel,
        grid=(b, nblk),
        in_specs=specs(fwd) + specs(bwd),
        out_specs=[pl.BlockSpec((None, SCAN_BLK, hw), fwd), pl.BlockSpec((None, SCAN_BLK, hw), bwd)],
        out_shape=[jax.ShapeDtypeStruct((b, s, hw), F32)] * 2,
        scratch_shapes=[pltpu.VMEM((2 * DN_HEADS, DN_DK, DN_DV), F32)],
        compiler_params=_cparams(("parallel", "arbitrary")),
        name="dn_scan",
    )(uf, pkf, gtf, ub, pkb, gtb)


ML_AUG = 2 * LANES
ML_GROUP = 4
ML_BLK = 4 * BLK


def _ml_kernel(qf_ref, kf_ref, vf_ref, smf_ref, qb_ref, kb_ref, vb_ref, smb_ref, lp_ref,
               hf_ref, hb_ref, c_scr, m_scr):
    @pl.when(pl.program_id(1) == 0)
    def _():
        c_scr[...] = jnp.zeros_like(c_scr)
        m_scr[...] = jnp.zeros_like(m_scr)

    nchunk = BLK // ML_CHUNK
    ones_col = jnp.ones((BLK, LANES), BF16)
    zeros_aug = jnp.zeros((ML_CHUNK, ML_AUG), BF16)
    streams = ((qf_ref, kf_ref, vf_ref, smf_ref, hf_ref, range(nchunk)),
               (qb_ref, kb_ref, vb_ref, smb_ref, hb_ref, range(nchunk - 1, -1, -1)))
    lanes = lambda col: jnp.broadcast_to(col, (col.shape[0], LANES))
    n_sub = smf_ref.shape[0] // BLK
    units = [(d, p if d == 0 else n_sub - 1 - p) for p in range(n_sub) for d in range(2)]
    for d, sub in units:
        _ml_block(d, sub, streams[d], lp_ref, c_scr, m_scr, lanes, ones_col, zeros_aug, nchunk)


def _ml_block(d, sub, stream, lp_ref, c_scr, m_scr, lanes, ones_col, zeros_aug, nchunk):
    q_ref, k_ref, v_ref, sm_ref, h_ref, order = stream
    blk_rows = slice(sub * BLK, (sub + 1) * BLK)
    same, incl, _, _ = _chunk_masks(d == 1, ML_CHUNK)
    sm = sm_ref[blk_rows, :]
    ig_all = sm + lp_ref[0:1, :]
    x = sm + lp_ref[1:2, :]
    lf_all = jnp.minimum(x, 0.0) - jnp.log(1.0 + jnp.exp(-jnp.abs(x)))
    lf_all = pltpu.roll(lf_all, LANES - (_L_DF - _L_DI), 1)
    bc_all = _mask_sum(incl, lf_all)
    tot_all = _mask_sum(same, lf_all)
    a_all = ig_all - bc_all
    mwa_all = jnp.concatenate(
        [jnp.broadcast_to(jnp.max(a_all[ci * ML_CHUNK:(ci + 1) * ML_CHUNK], axis=0, keepdims=True),
                          (ML_CHUNK, LANES)) for ci in range(nchunk)], axis=0)
    a_t, w_all, mw_all = a_all.T, jnp.exp(a_all - mwa_all), tot_all + mwa_all

    for chains in [[(d, h) for h in range(g, g + ML_GROUP)] for g in range(0, ML_HEADS, ML_GROUP)]:
        ins = []
        for _, h in chains:
            q = q_ref[blk_rows, h * ML_DK:(h + 1) * ML_DK].astype(BF16)
            k = k_ref[blk_rows, h * ML_DK:(h + 1) * ML_DK].astype(F32) * (ML_DK ** -0.5)
            v_aug = jnp.concatenate([v_ref[blk_rows, h * ML_DV:(h + 1) * ML_DV].astype(BF16), ones_col], axis=-1)
            ins.append((q, k, v_aug))
        qks = [_mm_nt(q, k) for q, k, _ in ins]

        mids = []
        for (d, h), (q, k, v_aug), qk in zip(chains, ins, qks):
            c = _L_DI + d * ML_HEADS + h
            b_l = lanes(bc_all[:, c:c + 1])
            dlog = jnp.where(incl, b_l + a_t[c:c + 1, :], NEG)
            m_intra = lanes(jnp.max(dlog, axis=-1, keepdims=True))
            s_intra = qk * jnp.exp(dlog - m_intra)
            wk = (k * lanes(w_all[:, c:c + 1])[:, :ML_DK]).astype(BF16)
            mids.append((b_l, m_intra, s_intra.astype(BF16), wk))
        p_intras = [_mm(s_b, v_aug) for (_, _, s_b, _), (_, _, v_aug) in zip(mids, ins)]
        kvs = []
        for (_, _, _, wk), (_, _, v_aug) in zip(mids, ins):
            per_chunk = []
            for ci in range(nchunk):
                parts = [zeros_aug] * nchunk
                parts[ci] = v_aug[ci * ML_CHUNK:(ci + 1) * ML_CHUNK]
                per_chunk.append(_mm_tn(wk, jnp.concatenate(parts, axis=0)))
            kvs.append(per_chunk)

        c_sts = [c_scr[d * ML_HEADS + h] for d, h in chains]
        m_sts = [m_scr[d * ML_HEADS + h][0:1, :] for d, h in chains]
        for step in range(nchunk):
            qcs = []
            ci = order[step]
            for (q, _, _), c_st in zip(ins, c_sts):
                qcs.append(_mm(q[ci * ML_CHUNK:(ci + 1) * ML_CHUNK], c_st))
            for idx, (_, h) in enumerate(chains):
                b_l, m_intra, _, _ = mids[idx]
                c = _L_DI + d * ML_HEADS + h
                rows = slice(ci * ML_CHUNK, (ci + 1) * ML_CHUNK)
                out_rows = slice(sub * BLK + ci * ML_CHUNK, sub * BLK + (ci + 1) * ML_CHUNK)
                r8 = slice(ci * ML_CHUNK, ci * ML_CHUNK + SUBLANES)
                m_st, c_st, qc = m_sts[idx], c_sts[idx], qcs[idx]
                m_inter = b_l[rows] + m_st
                m_i = jnp.maximum(m_intra[rows], m_inter)
                f_i = jnp.exp(m_intra[rows] - m_i)
                inter = jnp.exp(m_inter - m_i)
                both = (jnp.concatenate([inter, inter], axis=-1) * qc
                        + jnp.concatenate([f_i, f_i], axis=-1) * p_intras[idx][rows])
                numer, denom = both[:, :ML_DV], both[:, ML_DV:]
                h_ref[out_rows, h * ML_DV:(h + 1) * ML_DV] = numer / jnp.maximum(jnp.abs(denom), jnp.exp(-m_i))
                tot_c = lanes(tot_all[r8, c:c + 1])[0:1]
                mw_c = lanes(mw_all[r8, c:c + 1])[0:1]
                m_new = jnp.maximum(tot_c + m_st, mw_c)
                dec = jnp.exp(tot_c + m_st - m_new)
                gain = jnp.exp(mw_c - m_new)
                c_sts[idx] = (jnp.concatenate([dec, dec], axis=-1) * c_st
                              + jnp.concatenate([gain, gain], axis=-1) * kvs[idx][ci])
                m_sts[idx] = m_new
        for idx, (d, h) in enumerate(chains):
            c_scr[d * ML_HEADS + h] = c_sts[idx]
            m_scr[d * ML_HEADS + h] = jnp.broadcast_to(m_sts[idx], (SUBLANES, LANES))


def _mlstm(pm3, ps3, lane_params):
    b, s, _ = pm3.shape
    nblk = s // ML_BLK
    qw = ML_HEADS * ML_DK
    vw = ML_HEADS * ML_DV
    def specs(tmap):
        blk = lambda j: tmap(j)
        return [
            pl.BlockSpec((None, ML_BLK, qw), lambda i, j: (i, blk(j), _OFF["d_q"] // qw)),
            pl.BlockSpec((None, ML_BLK, qw), lambda i, j: (i, blk(j), _OFF["d_k"] // qw)),
            pl.BlockSpec((None, ML_BLK, vw), lambda i, j: (i, blk(j), _OFF["d_v"] // vw)),
            pl.BlockSpec((None, ML_BLK, LANES), lambda i, j: (i, blk(j), 0)),
        ]
    fwd = lambda j: j
    bwd = lambda j: nblk - 1 - j
    return pl.pallas_call(
        _ml_kernel,
        grid=(b, nblk),
        in_specs=specs(fwd) + specs(bwd) + [pl.BlockSpec((SUBLANES, LANES), lambda i, j: (0, 0))],
        out_specs=[pl.BlockSpec((None, ML_BLK, vw), lambda i, j: (i, j, 0)),
                   pl.BlockSpec((None, ML_BLK, vw), lambda i, j: (i, nblk - 1 - j, 0))],
        out_shape=[jax.ShapeDtypeStruct((b, s, vw), F32)] * 2,
        scratch_shapes=[pltpu.VMEM((2 * ML_HEADS, ML_DK, ML_AUG), F32),
                        pltpu.VMEM((2 * ML_HEADS, SUBLANES, LANES), F32)],
        compiler_params=_cparams(("parallel", "arbitrary")),
        name="mlstm",
    )(pm3, pm3, pm3, ps3, pm3, pm3, pm3, ps3, lane_params)


def _merge_kernel(x_ref, af_ref, ab_ref, df_ref, db_ref, yb_ref, yc_ref, az_ref, dz_ref, do_ref, gl_ref,
                  ag_ref, dg_ref, wb_ref, wo_ref, o_ref):
    d = x_ref.shape[-1]

    def head_rms(x, g):
        outs = []
        for h in range(x.shape[-1] // LANES):
            xh = x[:, h * LANES:(h + 1) * LANES]
            ms = jnp.mean(xh * xh, axis=-1, keepdims=True)
            outs.append(xh * lax.rsqrt(ms + EPS) * g)
        return jnp.concatenate(outs, axis=-1)

    ya = head_rms(af_ref[...] + ab_ref[...], ag_ref[...]) * _silu(az_ref[...].astype(F32))
    yd = _sigmoid(do_ref[...].astype(F32)) * head_rms(df_ref[...] + db_ref[...], dg_ref[...])
    yd = yd * _silu(dz_ref[...].astype(F32))
    twice = None
    for i, y in enumerate((ya.astype(BF16), yb_ref[...], yc_ref[...], yd.astype(BF16))):
        proj = jnp.dot(y, wb_ref[i], preferred_element_type=F32)
        term = proj + jnp.tanh(0.5 * gl_ref[:, i * d:(i + 1) * d].astype(F32)) * proj
        twice = term if twice is None else twice + term
    merged = (0.5 * twice).astype(BF16)
    o_ref[...] = x_ref[...] + jnp.dot(merged, wo_ref[...], preferred_element_type=F32)


def _merge(x2d, af, ab, df, db, yb, yc, pm2, ag, dg, wb, wo, layer, tm=512):
    m, d = x2d.shape
    gw = N_BRANCH * d
    w = BRANCH_W
    tok = pl.BlockSpec((tm, w), lambda i: (i, 0))
    col = lambda name: pl.BlockSpec((tm, w), lambda i: (i, _OFF[name] // w))
    vec = pl.BlockSpec((1, LANES), lambda i: (0, 0))
    return pl.pallas_call(
        _merge_kernel,
        grid=(m // tm,),
        in_specs=[
            pl.BlockSpec((tm, d), lambda i: (i, 0)),
            tok, tok, tok, tok, tok, tok,
            col("a_z"), col("d_z"), col("d_o"),
            pl.BlockSpec((tm, gw), lambda i: (i, _OFF["gate"] // gw)),
            vec, vec,
            pl.BlockSpec((None, N_BRANCH, w, d), lambda i: (layer, 0, 0, 0)),
            pl.BlockSpec((None, d, d), lambda i: (layer, 0, 0)),
        ],
        out_specs=pl.BlockSpec((tm, d), lambda i: (i, 0)),
        out_shape=jax.ShapeDtypeStruct((m, d), F32),
        compiler_params=_cparams(("parallel",)),
        name="merge",
    )(x2d, af, ab, df, db, yb, yc, pm2, pm2, pm2, pm2, ag, dg, wb, wo)


def _rope_lane_tables(s):
    t = jnp.arange(s)
    row = (t // GRID_W).astype(F32)
    col = (t % GRID_W).astype(F32)
    m = GA_DH // 4
    inv = ROPE_THETA ** (-jnp.arange(m, dtype=F32) / m)
    ar = row[:, None] * inv
    ac = col[:, None] * inv
    cos_t = jnp.concatenate([jnp.cos(ar), jnp.cos(ar), jnp.cos(ac), jnp.cos(ac)], axis=-1)
    sin_t = jnp.concatenate([-jnp.sin(ar), jnp.sin(ar), -jnp.sin(ac), jnp.sin(ac)], axis=-1)
    return cos_t.astype(F32), sin_t.astype(F32)


def _lane_tiles(rows):
    padded = []
    for off, vals in rows:
        vals = vals.reshape(vals.shape[0], 1, -1).astype(F32)
        padded.append(jnp.pad(vals, ((0, 0), (0, 0), (off, LANES - off - vals.shape[-1]))))
    tiles = jnp.concatenate(padded, axis=1)
    return jnp.pad(tiles, ((0, 0), (0, SUBLANES - len(rows)), (0, 0)))


def kernel(x, norm_g, w_in, conv_a, dn_a_log, dn_dt_bias, dn_norm_g, na_q_norm, na_k_norm, na_rpb,
           ga_q_norm, ga_k_norm, ml_i_bias, ml_f_bias, ml_norm_g, w_branch, w_out):
    b, s, d = x.shape
    depth = w_in.shape[0]
    hw = BRANCH_W
    cos_t, sin_t = _rope_lane_tables(s)
    w_main = jnp.concatenate([w_in[:, :, o:o + wd] for _, o, wd in _MAIN_SEGS], axis=2).astype(BF16)
    w_small = jnp.pad(jnp.concatenate([w_in[:, :, o:o + 8] for o in _SMALL_SRC], axis=2),
                      ((0, 0), (0, 0), (0, LANES - 8 * len(_SMALL_SRC)))).astype(BF16)
    conv8 = jnp.pad(conv_a.astype(F32), ((0, 0), (0, SUBLANES - DN_CONV), (0, 0)))
    dn_lp = _lane_tiles([(_L_AA, dn_a_log), (_L_AA, dn_dt_bias)])
    ml_lp = _lane_tiles([(_L_DI, ml_i_bias), (_L_DF, ml_f_bias)])
    na_bias = _na_bias_table(na_rpb, s // GRID_W)
    na_qg = jnp.tile(na_q_norm, (1, 2)).reshape(depth, 1, 2 * NA_DH)
    na_kg = jnp.tile(na_k_norm, (1, 2)).reshape(depth, 1, 2 * NA_DH)
    wb_bf, wo_bf = w_branch.astype(BF16), w_out.astype(BF16)

    x2 = x.reshape(b * s, d)
    for l in range(depth):
        pm2, ps2 = _inproj(x2, norm_g[l].reshape(1, d), w_main, w_small, l)
        pm3 = pm2.reshape(b, s, N_MAIN)
        ps3 = ps2.reshape(b, s, LANES)
        o_af, o_ab = _dn_scan(_dn_pre(pm3, ps3, conv8[l], dn_lp[l]))
        h_df, h_db = _mlstm(pm3, ps3, ml_lp[l])
        yb = _natten(pm3, na_bias, na_qg[l], na_kg[l], l)
        yc = _gqa(pm3, cos_t, sin_t, ga_q_norm[l].reshape(1, GA_DH), ga_k_norm[l].reshape(1, GA_DH))
        x2 = _merge(x2, o_af.reshape(b * s, hw), o_ab.reshape(b * s, hw), h_df.reshape(b * s, hw),
                    h_db.reshape(b * s, hw), yb.reshape(b * s, hw), yc.reshape(b * s, hw), pm2,
                    dn_norm_g[l].reshape(1, LANES), ml_norm_g[l].reshape(1, LANES), wb_bf, wo_bf, l)
    return x2.reshape(b, s, d)
```

```python
import functools
import math

import jax
import jax.numpy as jnp
from jax import lax
from jax.experimental import pallas as pl
from jax.experimental.pallas import tpu as pltpu

F32 = jnp.float32
BF16 = jnp.bfloat16

D_MODEL = 1024
GRID_W = 64
N_BRANCH = 4
BRANCH_W = 512
EPS = 1e-6
DN_HEADS, DN_DK, DN_DV, DN_CONV, DN_CHUNK = 4, 128, 128, 5, 64
NA_HEADS, NA_DH, NA_ROWS, NA_COLS = 8, 64, 8, 16
GA_HEADS, GA_KV_HEADS, GA_DH = 4, 2, 128
ROPE_THETA = 10000.0
ML_HEADS, ML_DK, ML_DV, ML_CHUNK = 4, 64, 128, 128

LANES = 128
SUBLANES = 8
VMEM_LIMIT_BYTES = 56 * 1024 * 1024

_O_A_QKV, _O_A_A, _O_A_B, _O_A_Z = 0, 1536, 1544, 1552
_O_B_QKV, _O_B_Z = 2064, 3600
_O_C_Q, _O_C_K, _O_C_V, _O_C_Z = 4112, 4624, 4880, 5136
_O_D_Q, _O_D_K, _O_D_V, _O_D_I, _O_D_F, _O_D_O, _O_D_Z = 5648, 5904, 6160, 6672, 6680, 6688, 7200
_O_GATE = 7712
_MAIN_SEGS = (
    ("a_qkv", _O_A_QKV, 1536), ("b_qkv", _O_B_QKV, 1536), ("a_z", _O_A_Z, 512), ("b_z", _O_B_Z, 512),
    ("gate", _O_GATE, 4096), ("c_q", _O_C_Q, 512), ("c_k", _O_C_K, 256), ("c_v", _O_C_V, 256),
    ("c_z", _O_C_Z, 512), ("d_q", _O_D_Q, 256), ("d_k", _O_D_K, 256), ("d_v", _O_D_V, 512),
    ("d_o", _O_D_O, 512), ("d_z", _O_D_Z, 512),
)
_OFF = {}
_o = 0
for _name, _src, _w in _MAIN_SEGS:
    _OFF[_name] = _o
    _o += _w
N_MAIN = _o
_SMALL_SRC = (_O_A_A, _O_A_B, _O_D_I, _O_D_F)
_L_AA, _L_AB, _L_DI, _L_DF = 0, 8, 16, 24

P_DTYPE = BF16
BLK = 128
NEG = -1e30


def _cparams(sem):
    return pltpu.CompilerParams(dimension_semantics=sem, vmem_limit_bytes=VMEM_LIMIT_BYTES)


def _sigmoid(x):
    return 0.5 * jnp.tanh(0.5 * x) + 0.5


def _silu(x):
    return x * _sigmoid(x)


def _softplus(x):
    return jnp.maximum(x, 0.0) + jnp.log(1.0 + jnp.exp(-jnp.abs(x)))


def _mm(a, b):
    return jnp.dot(a.astype(BF16), b.astype(BF16), preferred_element_type=F32)


def _mm_nt(a, b):
    return lax.dot_general(a.astype(BF16), b.astype(BF16), (((1,), (1,)), ((), ())),
                           preferred_element_type=F32)


def _mm_tn(a, b):
    return lax.dot_general(a.astype(BF16), b.astype(BF16), (((0,), (0,)), ((), ())),
                           preferred_element_type=F32)


def _mask_sum(mask, x):
    m = jnp.where(mask, 1.0, 0.0).astype(BF16)
    x1 = x.astype(BF16)
    r1 = x - x1.astype(F32)
    x2 = r1.astype(BF16)
    x3 = (r1 - x2.astype(F32)).astype(BF16)
    dot = lambda v: jnp.dot(m, v, preferred_element_type=F32)
    return dot(x1) + (dot(x2) + dot(x3))


def _chunk_masks(reverse, chunk):
    i = lax.broadcasted_iota(jnp.int32, (BLK, BLK), 0)
    j = lax.broadcasted_iota(jnp.int32, (BLK, BLK), 1)
    shift = int(math.log2(chunk))
    same = (i >> shift) == (j >> shift)
    if reverse:
        incl = same & (j >= i)
        strict = same & (j > i)
    else:
        incl = same & (j <= i)
        strict = same & (j < i)
    return same, incl, strict, (i == j)


def _as_f32(mask):
    return jnp.where(mask, 1.0, 0.0).astype(F32)


def _inproj_kernel(x_ref, g_ref, w_ref, ws_ref, pm_ref, ps_ref, h_scr):
    @pl.when(pl.program_id(1) == 0)
    def _():
        x = x_ref[...]
        ms = jnp.mean(x * x, axis=-1, keepdims=True)
        h = (x * lax.rsqrt(ms + EPS) * g_ref[...]).astype(BF16)
        h_scr[...] = h
        ps_ref[...] = jnp.dot(h, ws_ref[...], preferred_element_type=F32)

    pm_ref[...] = jnp.dot(h_scr[...], w_ref[...], preferred_element_type=F32).astype(pm_ref.dtype)


def _inproj(x2d, g, w_main, w_small, layer, tm=1024, tn=N_MAIN // 4):
    m, d = x2d.shape
    tm = min(tm, m)
    return pl.pallas_call(
        _inproj_kernel,
        grid=(m // tm, N_MAIN // tn),
        in_specs=[
            pl.BlockSpec((tm, d), lambda i, j: (i, 0)),
            pl.BlockSpec((1, d), lambda i, j: (0, 0)),
            pl.BlockSpec((None, d, tn), lambda i, j: (layer, 0, j)),
            pl.BlockSpec((None, d, LANES), lambda i, j: (layer, 0, 0)),
        ],
        out_specs=[
            pl.BlockSpec((tm, tn), lambda i, j: (i, j)),
            pl.BlockSpec((tm, LANES), lambda i, j: (i, 0)),
        ],
        out_shape=[jax.ShapeDtypeStruct((m, N_MAIN), P_DTYPE), jax.ShapeDtypeStruct((m, LANES), F32)],
        scratch_shapes=[pltpu.VMEM((tm, d), BF16)],
        compiler_params=_cparams(("parallel", "arbitrary")),
        name="inproj",
    )(x2d, g, w_main, w_small)


GQA_TQ = 512


def _gqa_kernel(q_ref, k_ref, v_ref, z_ref, cos_ref, sin_ref, qg_ref, kg_ref, y_ref,
                q_scr, k_scr, v_scr, s0_scr, s1_scr, p0_scr, p1_scr, l0_scr, l1_scr):
    def norm_rope(x, g, cos, sin):
        ms = jnp.mean(x * x, axis=-1, keepdims=True)
        xn = x * lax.rsqrt(ms + EPS) * g
        lane = lax.broadcasted_iota(jnp.int32, xn.shape, 1)
        partner = jnp.where((lane & 63) < 32, pltpu.roll(xn, LANES - 32, 1), pltpu.roll(xn, 32, 1))
        return xn * cos + partner * sin

    s_len = k_ref.shape[0]
    group = GA_HEADS // GA_KV_HEADS
    assert group == 2
    scale = GA_DH ** -0.5
    n_blk = s_len // GQA_TQ
    s_bufs, p_bufs, l_bufs = (s0_scr, s1_scr), (p0_scr, p1_scr), (l0_scr, l1_scr)
    for kv in range(k_ref.shape[1] // GA_DH):
        _gqa_pipeline(kv, group, s_len, n_blk, scale, norm_rope, q_ref, k_ref, v_ref, z_ref, cos_ref, sin_ref,
                      qg_ref, kg_ref, y_ref, q_scr, k_scr, v_scr, s_bufs, p_bufs, l_bufs)


def _gqa_pipeline(kv, group, s_len, n_blk, scale, norm_rope, q_ref, k_ref, v_ref, z_ref, cos_ref, sin_ref,
                  qg_ref, kg_ref, y_ref, q_scr, k_scr, v_scr, s_bufs, p_bufs, l_bufs):
    kcols = slice(kv * GA_DH, (kv + 1) * GA_DH)
    k_scr[kv] = norm_rope(k_ref[:, kcols].astype(F32), kg_ref[...], cos_ref[...], sin_ref[...]).astype(BF16)
    v_scr[kv] = v_ref[:, kcols].astype(BF16)

    def stacked(head, blk):
        return pl.ds(pl.multiple_of(head * s_len + blk * GQA_TQ, GQA_TQ), GQA_TQ)

    def head_cols(head):
        return slice((kv * group + head) * GA_DH, (kv * group + head + 1) * GA_DH)

    def prep(head, blk):
        rows = pl.ds(pl.multiple_of(blk * GQA_TQ, GQA_TQ), GQA_TQ)
        q = norm_rope(q_ref[rows, head_cols(head)].astype(F32), qg_ref[...], cos_ref[rows, :], sin_ref[rows, :])
        q_scr[stacked(head, blk), :] = q.astype(BF16)

    def logits(head, blk):
        s_bufs[head][...] = _mm_nt(q_scr[stacked(head, blk), :], k_scr[kv]) * scale

    def softmax(slot):
        s = s_bufs[slot][...]
        p = jnp.exp(s - jnp.max(s, axis=-1, keepdims=True))
        l_bufs[slot][...] = jnp.broadcast_to(jnp.sum(p, axis=-1, keepdims=True), (GQA_TQ, GA_DH))
        p_bufs[slot][...] = p.astype(BF16)

    def weighted(head, blk):
        rows = pl.ds(pl.multiple_of(blk * GQA_TQ, GQA_TQ), GQA_TQ)
        cols = head_cols(head)
        o = _mm(p_bufs[head][...], v_scr[kv]) / l_bufs[head][...]
        y_ref[rows, cols] = (o * _silu(z_ref[rows, cols].astype(F32))).astype(y_ref.dtype)

    prep(0, 0)
    prep(1, 0)
    prep(0, 1)
    logits(0, 0)
    logits(1, 0)
    softmax(0)

    def body(j, carry):
        logits(0, j + 1)
        softmax(1)
        weighted(0, j)
        prep(1, j + 1)
        logits(1, j + 1)
        softmax(0)
        weighted(1, j)
        prep(0, jnp.minimum(j + 2, n_blk - 1))
        return carry

    lax.fori_loop(0, n_blk - 1, body, 0)
    softmax(1)
    weighted(0, n_blk - 1)
    weighted(1, n_blk - 1)


def _gqa(pm3, cos_t, sin_t, qg, kg):
    b, s, _ = pm3.shape
    qw = GA_HEADS * GA_DH
    kw = GA_KV_HEADS * GA_DH
    group = GA_HEADS // GA_KV_HEADS
    return pl.pallas_call(
        _gqa_kernel,
        grid=(b,),
        in_specs=[
            pl.BlockSpec((None, s, qw), lambda i: (i, 0, _OFF["c_q"] // qw)),
            pl.BlockSpec((None, s, kw), lambda i: (i, 0, _OFF["c_k"] // kw)),
            pl.BlockSpec((None, s, kw), lambda i: (i, 0, _OFF["c_v"] // kw)),
            pl.BlockSpec((None, s, qw), lambda i: (i, 0, _OFF["c_z"] // qw)),
            pl.BlockSpec((s, GA_DH), lambda i: (0, 0)),
            pl.BlockSpec((s, GA_DH), lambda i: (0, 0)),
            pl.BlockSpec((1, GA_DH), lambda i: (0, 0)),
            pl.BlockSpec((1, GA_DH), lambda i: (0, 0)),
        ],
        out_specs=pl.BlockSpec((None, s, qw), lambda i: (i, 0, 0)),
        out_shape=jax.ShapeDtypeStruct((b, s, BRANCH_W), BF16),
        scratch_shapes=[pltpu.VMEM((group * s, GA_DH), BF16), pltpu.VMEM((GA_KV_HEADS, s, GA_DH), BF16),
                        pltpu.VMEM((GA_KV_HEADS, s, GA_DH), BF16),
                        pltpu.VMEM((GQA_TQ, s), F32), pltpu.VMEM((GQA_TQ, s), F32),
                        pltpu.VMEM((GQA_TQ, s), BF16), pltpu.VMEM((GQA_TQ, s), BF16),
                        pltpu.VMEM((GQA_TQ, GA_DH), F32), pltpu.VMEM((GQA_TQ, GA_DH), F32)],
        compiler_params=_cparams(("parallel",)),
        name="gqa",
    )(pm3, pm3, pm3, pm3, cos_t, sin_t, qg, kg)


NA_ROW_UNROLL = 16


def _na_kernel(q_ref, k_ref, v_ref, z_ref, bias_ref, qg_ref, kg_ref, y_ref, q_scr, k_scr):
    s_len = q_ref.shape[0]
    rows = s_len // GRID_W
    kr = min(NA_ROWS, rows)
    hi = lax.broadcasted_iota(jnp.int32, (2 * NA_DH, 2 * NA_DH), 0) >= NA_DH
    hj = lax.broadcasted_iota(jnp.int32, (2 * NA_DH, 2 * NA_DH), 1) >= NA_DH
    same_head = jnp.where(hi == hj, 1.0, 0.0).astype(BF16)

    def rms_pair(x, g):
        x2 = x * x
        x2_hi = x2.astype(BF16)
        x2_lo = (x2 - x2_hi.astype(F32)).astype(BF16)
        ssq = (jnp.dot(x2_hi, same_head, preferred_element_type=F32)
               + jnp.dot(x2_lo, same_head, preferred_element_type=F32))
        return x * lax.rsqrt(ssq * (1.0 / NA_DH) + EPS) * g

    scale = NA_DH ** -0.5
    assert math.log2(scale).is_integer()
    nkeys = kr * GRID_W
    first = lax.broadcasted_iota(jnp.int32, (s_len, 2 * NA_DH), 1) < NA_DH
    first_q = lax.broadcasted_iota(jnp.int32, (GRID_W, 2 * NA_DH), 1) < NA_DH
    for pair in range(q_ref.shape[1] // (2 * NA_DH)):
        pc = slice(pair * 2 * NA_DH, (pair + 1) * 2 * NA_DH)
        qn = rms_pair(q_ref[:, pc].astype(F32), qg_ref[...]) * scale
        q_scr[0] = jnp.where(first, qn, 0.0).astype(BF16)
        q_scr[1] = jnp.where(first, 0.0, qn).astype(BF16)
        k_scr[...] = rms_pair(k_ref[:, pc].astype(F32), kg_ref[...]).astype(BF16)

        def body(it, carry, pair=pair, pc=pc):
            units = []
            for u in range(NA_ROW_UNROLL):
                r = it * NA_ROW_UNROLL + u
                r0 = jnp.clip(r - kr // 2, 0, rows - kr)
                var = r0 - r + (NA_ROWS - 1)
                qrows = pl.ds(pl.multiple_of(r * GRID_W, GRID_W), GRID_W)
                krows = pl.ds(pl.multiple_of(r0 * GRID_W, GRID_W), nkeys)
                units.append((var, qrows, krows))
            logits = [_mm_nt(jnp.concatenate([q_scr[0, qrows, :], q_scr[1, qrows, :]], axis=0), k_scr[krows, :])
                      + bias_ref[pair, var] for var, qrows, krows in units]
            probs = [jnp.exp(s - jnp.max(s, axis=-1, keepdims=True)) for s in logits]
            sums = [jnp.sum(p, axis=-1, keepdims=True) for p in probs]
            outs = [_mm(p, v_ref[krows, pc]) / l for p, l, (_, _, krows) in zip(probs, sums, units)]
            for o, (_, qrows, _) in zip(outs, units):
                o_pair = jnp.where(first_q, o[:GRID_W], o[GRID_W:])
                y_ref[qrows, pc] = (o_pair * _silu(z_ref[qrows, pc].astype(F32))).astype(y_ref.dtype)
            return carry

        lax.fori_loop(0, rows // NA_ROW_UNROLL, body, 0)


def _na_bias_table(rpb, rows):
    kr = min(NA_ROWS, rows)
    c = jnp.arange(GRID_W)
    c0 = jnp.clip(c - NA_COLS // 2, 0, GRID_W - NA_COLS)
    in_win = (c[None, :] >= c0[:, None]) & (c[None, :] < c0[:, None] + NA_COLS)
    col_off = jnp.clip(c[None, :] - c[:, None], -(NA_COLS - 1), NA_COLS - 1) + NA_COLS - 1
    t = jnp.where(in_win, rpb[..., col_off], NEG)
    ro = jnp.arange(NA_ROWS)[:, None] + jnp.arange(kr)[None, :]
    tv = jnp.take(t, ro, axis=-3)
    tv = jnp.swapaxes(tv, -3, -2).reshape(*rpb.shape[:-2], NA_ROWS, GRID_W, kr * GRID_W)
    lead = rpb.shape[:-3]
    tv = tv.reshape(*lead, rpb.shape[-3] // 2, 2, NA_ROWS, GRID_W, kr * GRID_W)
    tv = jnp.swapaxes(tv, -4, -3)
    return tv.reshape(*lead, rpb.shape[-3] // 2, NA_ROWS, 2 * GRID_W, kr * GRID_W).astype(F32)


def _natten(pm3, bias, qg, kg, layer):
    b, s, _ = pm3.shape
    pw = 2 * NA_DH
    hw = NA_HEADS * NA_DH
    return pl.pallas_call(
        _na_kernel,
        grid=(b,),
        in_specs=[
            pl.BlockSpec((None, s, hw), lambda i: (i, 0, _OFF["b_qkv"] // hw)),
            pl.BlockSpec((None, s, hw), lambda i: (i, 0, _OFF["b_qkv"] // hw + 1)),
            pl.BlockSpec((None, s, hw), lambda i: (i, 0, _OFF["b_qkv"] // hw + 2)),
            pl.BlockSpec((None, s, hw), lambda i: (i, 0, _OFF["b_z"] // hw)),
            pl.BlockSpec((None,) + bias.shape[1:], lambda i: (layer, 0, 0, 0, 0)),
            pl.BlockSpec((1, pw), lambda i: (0, 0)),
            pl.BlockSpec((1, pw), lambda i: (0, 0)),
        ],
        out_specs=pl.BlockSpec((None, s, hw), lambda i: (i, 0, 0)),
        out_shape=jax.ShapeDtypeStruct((b, s, BRANCH_W), BF16),
        scratch_shapes=[pltpu.VMEM((2, s, pw), BF16), pltpu.VMEM((s, pw), BF16)],
        compiler_params=_cparams(("parallel",)),
        name="natten",
    )(pm3, pm3, pm3, pm3, bias, qg, kg)


INV_BASE = 8
PRE_BLK = 4 * BLK


def _inverse_level_masks():
    i = lax.broadcasted_iota(jnp.int32, (BLK, BLK), 0)
    j = lax.broadcasted_iota(jnp.int32, (BLK, BLK), 1)
    same = lambda size: (i >> int(math.log2(size))) == (j >> int(math.log2(size)))
    base = same(INV_BASE)
    joins = []
    size = INV_BASE
    while size < DN_CHUNK:
        joins.append(same(2 * size) & jnp.logical_not(same(size)))
        size *= 2
    return base, joins


def _tri_inverses(l_mats, eyes, level_masks):
    base, joins = level_masks
    ps = [jnp.where(base, -l, 0.0) for l in l_mats]
    ts = [eye + p for eye, p in zip(eyes, ps)]
    for _ in range(int(math.log2(INV_BASE)) - 1):
        ps = [_mm(p, p) for p in ps]
        ts = [t + _mm(t, p) for t, p in zip(ts, ps)]
    for join in joins:
        mids = [_mm(jnp.where(join, l, 0.0), t) for l, t in zip(l_mats, ts)]
        ts = [t - _mm(t, mid) for t, mid in zip(ts, mids)]
    return ts


DN_HW = DN_HEADS * LANES
PK_W, PK_QD, PK_KD, PK_QK = (i * DN_HW for i in range(4))


def _dn_pre_kernel(qkv_ref, sm_ref, cw_ref, lp_ref, uf_ref, ub_ref, pkf_ref, pkb_ref, gtf_ref, gtb_ref):
    n = pl.program_id(1)
    s_len = qkv_ref.shape[0]
    assert qkv_ref.dtype == BF16
    halo = 2 * SUBLANES
    n_sub = sm_ref.shape[0] // BLK
    chunks_per_blk = BLK // DN_CHUNK
    out_row = lax.broadcasted_iota(jnp.int32, (BLK, BLK + 2 * halo), 0)
    in_row = lax.broadcasted_iota(jnp.int32, (BLK, BLK + 2 * halo), 1)
    shifts = {j: jnp.where(in_row == out_row + (halo + j - DN_CONV // 2), 1.0, 0.0).astype(BF16)
              for j in range(DN_CONV) if j != DN_CONV // 2}
    dir_masks = [_chunk_masks(d == 1, DN_CHUNK) for d in range(2)]
    level_masks = _inverse_level_masks()
    outs = ((uf_ref, pkf_ref, gtf_ref), (ub_ref, pkb_ref, gtb_ref))

    def l2n(x):
        return x * lax.rsqrt(jnp.sum(x * x, axis=-1, keepdims=True) + EPS)

    blocks = []
    for sub in range(n_sub):
        t0 = pl.multiple_of((n * n_sub + sub) * BLK, BLK)
        pstart = pl.multiple_of(jnp.maximum(t0 - halo, 0), halo)
        nstart = pl.multiple_of(jnp.minimum(t0 + BLK, s_len - halo), halo)
        prev = qkv_ref[pl.ds(pstart, halo), :]
        prev = jnp.where(t0 > 0, prev, jnp.zeros_like(prev))
        cur = qkv_ref[pl.ds(t0, BLK), :]
        nxt = qkv_ref[pl.ds(nstart, halo), :]
        nxt = jnp.where(t0 + BLK < s_len, nxt, jnp.zeros_like(nxt))
        xw = jnp.concatenate([prev, cur, nxt], axis=0)
        conv = cur.astype(F32) * cw_ref[DN_CONV // 2:DN_CONV // 2 + 1, :]
        for j, shift in shifts.items():
            conv = conv + jnp.dot(shift, xw, preferred_element_type=F32) * cw_ref[j:j + 1, :]
        conv = _silu(conv)

        rows = slice(sub * BLK, (sub + 1) * BLK)
        sm = sm_ref[rows, :]
        g_all = -jnp.exp(lp_ref[0:1, :]) * _softplus(sm + lp_ref[1:2, :])
        beta_all = _sigmoid(sm)
        per_dir = []
        for d in range(2):
            same, incl, strict, eye_b = dir_masks[d]
            gc = _mask_sum(incl, g_all)
            tot = _mask_sum(same, g_all)
            per_dir.append((incl, strict, _as_f32(eye_b), gc, gc.T, tot))
        blocks.append((sub, rows, conv, beta_all, per_dir))

    heads = []
    for sub, rows, conv, beta_all, per_dir in blocks:
        for h in range(DN_HEADS):
            q = l2n(conv[:, h * DN_DK:(h + 1) * DN_DK]) * (DN_DK ** -0.5)
            k = l2n(conv[:, (DN_HEADS + h) * DN_DK:(DN_HEADS + h + 1) * DN_DK])
            v = conv[:, 2 * DN_HEADS * DN_DK + h * DN_DV:2 * DN_HEADS * DN_DK + (h + 1) * DN_DV]
            heads.append((sub, rows, beta_all, per_dir, h, q, k, v))
    kks = [_mm_nt(hd[6], hd[6]) for hd in heads]
    qks = [_mm_nt(hd[5], hd[6]) for hd in heads]

    l_mats, rhss, eyes, slots = [], [], [], []
    for (sub, rows, beta_all, per_dir, h, q, k, v), kk, qk in zip(heads, kks, qks):
        hc = slice(h * LANES, (h + 1) * LANES)
        for d in range(2):
            incl, strict, eye_f, gc, gct, tot = per_dir[d]
            u_ref, pk_ref, gt_ref = outs[d]
            pk = lambda off: slice(off + h * LANES, off + (h + 1) * LANES)
            c = _L_AA + d * DN_HEADS + h
            gcol = gc[:, c:c + 1]
            grow = gct[c:c + 1, :]
            tcol = tot[:, c:c + 1]
            beta = beta_all[:, _L_AB + d * DN_HEADS + h:_L_AB + d * DN_HEADS + h + 1]
            decay = jnp.where(incl, jnp.exp(jnp.where(incl, gcol - grow, 0.0)), 0.0)
            l_mats.append(jnp.where(strict, beta * kk * decay, 0.0))
            egc = jnp.exp(gcol)
            rhss.append(jnp.concatenate([v * beta, k * (beta * egc)], axis=-1).astype(BF16))
            eyes.append(eye_f)
            slots.append((rows, h, d))
            pk_ref[rows, pk(PK_QD)] = (q * egc).astype(pk_ref.dtype)
            pk_ref[rows, pk(PK_KD)] = (k * jnp.exp(tcol - gcol)).astype(pk_ref.dtype)
            pk_ref[rows, pk(PK_QK)] = (qk * decay).astype(pk_ref.dtype)
            gtot = jnp.exp(tcol)
            for ci in range(chunks_per_blk):
                g0 = (sub * chunks_per_blk + ci) * SUBLANES
                gt_ref[g0:g0 + SUBLANES, hc] = jnp.broadcast_to(
                    gtot[ci * DN_CHUNK:ci * DN_CHUNK + SUBLANES, :], (SUBLANES, LANES))

    t_invs = _tri_inverses(l_mats, eyes, level_masks)
    sols = [_mm(t, rhs) for t, rhs in zip(t_invs, rhss)]
    for (rows, h, d), sol in zip(slots, sols):
        u_ref, pk_ref = outs[d][0], outs[d][1]
        u_ref[rows, h * LANES:(h + 1) * LANES] = sol[:, :DN_DV]
        pk_ref[rows, PK_W + h * LANES:PK_W + (h + 1) * LANES] = sol[:, DN_DV:].astype(pk_ref.dtype)


def _dn_pre(pm3, ps3, conv_w8, lane_params):
    b, s, _ = pm3.shape
    nblk = s // PRE_BLK
    gt_rows = PRE_BLK // DN_CHUNK * SUBLANES
    wq = DN_HEADS * (2 * DN_DK + DN_DV)
    hw = DN_HEADS * LANES
    tok = lambda i, j: (i, j, 0)
    big = lambda dt: jax.ShapeDtypeStruct((b, s, hw), dt)
    gts = jax.ShapeDtypeStruct((b, nblk * gt_rows, hw), F32)
    bs_tok = pl.BlockSpec((None, PRE_BLK, hw), tok)
    bs_gt = pl.BlockSpec((None, gt_rows, hw), tok)
    bs_pk = pl.BlockSpec((None, PRE_BLK, 4 * hw), tok)
    packed = jax.ShapeDtypeStruct((b, s, 4 * hw), BF16)
    return pl.pallas_call(
        _dn_pre_kernel,
        grid=(b, nblk),
        in_specs=[
            pl.BlockSpec((None, s, wq), lambda i, j: (i, 0, _OFF["a_qkv"] // wq)),
            pl.BlockSpec((None, PRE_BLK, LANES), tok),
            pl.BlockSpec((SUBLANES, wq), lambda i, j: (0, 0)),
            pl.BlockSpec((SUBLANES, LANES), lambda i, j: (0, 0)),
        ],
        out_specs=[bs_tok, bs_tok, bs_pk, bs_pk, bs_gt, bs_gt],
        out_shape=[big(F32), big(F32), packed, packed, gts, gts],
        compiler_params=_cparams(("parallel", "arbitrary")),
        name="dn_pre",
    )(pm3, ps3, conv_w8, lane_params)


SCAN_BLK = 4 * BLK


def _dn_scan_kernel(uf_ref, pkf_ref, gtf_ref, ub_ref, pkb_ref, gtb_ref, of_ref, ob_ref, st_scr):
    @pl.when(pl.program_id(1) == 0)
    def _():
        st_scr[...] = jnp.zeros_like(st_scr)

    per_blk = BLK // DN_CHUNK
    nchunk = SCAN_BLK // DN_CHUNK
    zeros_c = jnp.zeros((DN_CHUNK, DN_DV), F32)
    streams = ((uf_ref, pkf_ref, gtf_ref, of_ref, range(nchunk)),
               (ub_ref, pkb_ref, gtb_ref, ob_ref, range(nchunk - 1, -1, -1)))
    chains = [(d, h) + streams[d] for d in range(2) for h in range(DN_HEADS)]
    states = [st_scr[d * DN_HEADS + h] for d, h, *_ in chains]
    for step in range(nchunk):
        rs, v_pads = [], []
        for (d, h, u_ref, pk_ref, gt_ref, o_ref, order), state in zip(chains, states):
            rows = slice(order[step] * DN_CHUNK, (order[step] + 1) * DN_CHUNK)
            w = pk_ref[rows, PK_W + h * LANES:PK_W + (h + 1) * LANES]
            qd = pk_ref[rows, PK_QD + h * LANES:PK_QD + (h + 1) * LANES]
            rs.append(_mm(jnp.concatenate([w, qd], axis=0), state))
        for (d, h, u_ref, pk_ref, gt_ref, o_ref, order), r in zip(chains, rs):
            ci = order[step]
            rows = slice(ci * DN_CHUNK, (ci + 1) * DN_CHUNK)
            parts = [zeros_c] * per_blk
            parts[ci % per_blk] = u_ref[rows, h * LANES:(h + 1) * LANES] - r[:DN_CHUNK]
            v_pads.append(jnp.concatenate(parts, axis=0))
        new_states = []
        for (d, h, u_ref, pk_ref, gt_ref, o_ref, order), r, v_pad, state in zip(chains, rs, v_pads, states):
            hc = slice(h * LANES, (h + 1) * LANES)
            ci = order[step]
            rows = slice(ci * DN_CHUNK, (ci + 1) * DN_CHUNK)
            blk_rows = slice((ci // per_blk) * BLK, (ci // per_blk + 1) * BLK)
            o_ref[rows, hc] = r[DN_CHUNK:] + _mm(pk_ref[rows, PK_QK + h * LANES:PK_QK + (h + 1) * LANES], v_pad)
            gt = gt_ref[ci * SUBLANES:ci * SUBLANES + 1, hc]
            kd = pk_ref[blk_rows, PK_KD + h * LANES:PK_KD + (h + 1) * LANES]
            new_states.append(state * gt + _mm_tn(kd, v_pad))
        states = new_states
    for (d, h, *_), state in zip(chains, states):
        st_scr[d * DN_HEADS + h] = state


def _dn_scan(pre):
    uf, ub, pkf, pkb, gtf, gtb = pre
    b, s, hw = uf.shape
    nblk = s // SCAN_BLK
    gt_rows = SCAN_BLK // DN_CHUNK * SUBLANES
    fwd = lambda i, j: (i, j, 0)
    bwd = lambda i, j: (i, nblk - 1 - j, 0)
    def specs(imap):
        return [pl.BlockSpec((None, SCAN_BLK, hw), imap), pl.BlockSpec((None, SCAN_BLK, 4 * hw), imap),
                pl.BlockSpec((None, gt_rows, hw), imap)]
    return pl.pallas_call(
        _dn_scan_kernel,
        grid=(b, nblk),
        in_specs=specs(fwd) + specs(bwd),
        out_specs=[pl.BlockSpec((None, SCAN_BLK, hw), fwd), pl.BlockSpec((None, SCAN_BLK, hw), bwd)],
        out_shape=[jax.ShapeDtypeStruct((b, s, hw), F32)] * 2,
        scratch_shapes=[pltpu.VMEM((2 * DN_HEADS, DN_DK, DN_DV), F32)],
        compiler_params=_cparams(("parallel", "arbitrary")),
        name="dn_scan",
    )(uf, pkf, gtf, ub, pkb, gtb)


ML_AUG = 2 * LANES
ML_GROUP = 4
ML_BLK = 4 * BLK


def _ml_kernel(qf_ref, kf_ref, vf_ref, smf_ref, qb_ref, kb_ref, vb_ref, smb_ref, lp_ref,
               hf_ref, hb_ref, c_scr, m_scr):
    @pl.when(pl.program_id(1) == 0)
    def _():
        c_scr[...] = jnp.zeros_like(c_scr)
        m_scr[...] = jnp.zeros_like(m_scr)

    nchunk = BLK // ML_CHUNK
    ones_col = jnp.ones((BLK, LANES), BF16)
    zeros_aug = jnp.zeros((ML_CHUNK, ML_AUG), BF16)
    streams = ((qf_ref, kf_ref, vf_ref, smf_ref, hf_ref, range(nchunk)),
               (qb_ref, kb_ref, vb_ref, smb_ref, hb_ref, range(nchunk - 1, -1, -1)))
    lanes = lambda col: jnp.broadcast_to(col, (col.shape[0], LANES))
    n_sub = smf_ref.shape[0] // BLK
    units = [(d, p if d == 0 else n_sub - 1 - p) for p in range(n_sub) for d in range(2)]
    for d, sub in units:
        _ml_block(d, sub, streams[d], lp_ref, c_scr, m_scr, lanes, ones_col, zeros_aug, nchunk)


def _ml_block(d, sub, stream, lp_ref, c_scr, m_scr, lanes, ones_col, zeros_aug, nchunk):
    q_ref, k_ref, v_ref, sm_ref, h_ref, order = stream
    blk_rows = slice(sub * BLK, (sub + 1) * BLK)
    same, incl, _, _ = _chunk_masks(d == 1, ML_CHUNK)
    sm = sm_ref[blk_rows, :]
    ig_all = sm + lp_ref[0:1, :]
    x = sm + lp_ref[1:2, :]
    lf_all = jnp.minimum(x, 0.0) - jnp.log(1.0 + jnp.exp(-jnp.abs(x)))
    lf_all = pltpu.roll(lf_all, LANES - (_L_DF - _L_DI), 1)
    bc_all = _mask_sum(incl, lf_all)
    tot_all = _mask_sum(same, lf_all)
    a_all = ig_all - bc_all
    mwa_all = jnp.concatenate(
        [jnp.broadcast_to(jnp.max(a_all[ci * ML_CHUNK:(ci + 1) * ML_CHUNK], axis=0, keepdims=True),
                          (ML_CHUNK, LANES)) for ci in range(nchunk)], axis=0)
    a_t, w_all, mw_all = a_all.T, jnp.exp(a_all - mwa_all), tot_all + mwa_all

    for chains in [[(d, h) for h in range(g, g + ML_GROUP)] for g in range(0, ML_HEADS, ML_GROUP)]:
        ins = []
        for _, h in chains:
            q = q_ref[blk_rows, h * ML_DK:(h + 1) * ML_DK].astype(BF16)
            k = k_ref[blk_rows, h * ML_DK:(h + 1) * ML_DK].astype(F32) * (ML_DK ** -0.5)
            v_aug = jnp.concatenate([v_ref[blk_rows, h * ML_DV:(h + 1) * ML_DV].astype(BF16), ones_col], axis=-1)
            ins.append((q, k, v_aug))
        qks = [_mm_nt(q, k) for q, k, _ in ins]

        mids = []
        for (d, h), (q, k, v_aug), qk in zip(chains, ins, qks):
            c = _L_DI + d * ML_HEADS + h
            b_l = lanes(bc_all[:, c:c + 1])
            dlog = jnp.where(incl, b_l + a_t[c:c + 1, :], NEG)
            m_intra = lanes(jnp.max(dlog, axis=-1, keepdims=True))
            s_intra = qk * jnp.exp(dlog - m_intra)
            wk = (k * lanes(w_all[:, c:c + 1])[:, :ML_DK]).astype(BF16)
            mids.append((b_l, m_intra, s_intra.astype(BF16), wk))
        p_intras = [_mm(s_b, v_aug) for (_, _, s_b, _), (_, _, v_aug) in zip(mids, ins)]
        kvs = []
        for (_, _, _, wk), (_, _, v_aug) in zip(mids, ins):
            per_chunk = []
            for ci in range(nchunk):
                parts = [zeros_aug] * nchunk
                parts[ci] = v_aug[ci * ML_CHUNK:(ci + 1) * ML_CHUNK]
                per_chunk.append(_mm_tn(wk, jnp.concatenate(parts, axis=0)))
            kvs.append(per_chunk)

        c_sts = [c_scr[d * ML_HEADS + h] for d, h in chains]
        m_sts = [m_scr[d * ML_HEADS + h][0:1, :] for d, h in chains]
        for step in range(nchunk):
            qcs = []
            ci = order[step]
            for (q, _, _), c_st in zip(ins, c_sts):
                qcs.append(_mm(q[ci * ML_CHUNK:(ci + 1) * ML_CHUNK], c_st))
            for idx, (_, h) in enumerate(chains):
                b_l, m_intra, _, _ = mids[idx]
                c = _L_DI + d * ML_HEADS + h
                rows = slice(ci * ML_CHUNK, (ci + 1) * ML_CHUNK)
                out_rows = slice(sub * BLK + ci * ML_CHUNK, sub * BLK + (ci + 1) * ML_CHUNK)
                r8 = slice(ci * ML_CHUNK, ci * ML_CHUNK + SUBLANES)
                m_st, c_st, qc = m_sts[idx], c_sts[idx], qcs[idx]
                m_inter = b_l[rows] + m_st
                m_i = jnp.maximum(m_intra[rows], m_inter)
                f_i = jnp.exp(m_intra[rows] - m_i)
                inter = jnp.exp(m_inter - m_i)
                both = (jnp.concatenate([inter, inter], axis=-1) * qc
                        + jnp.concatenate([f_i, f_i], axis=-1) * p_intras[idx][rows])
                numer, denom = both[:, :ML_DV], both[:, ML_DV:]
                h_ref[out_rows, h * ML_DV:(h + 1) * ML_DV] = numer / jnp.maximum(jnp.abs(denom), jnp.exp(-m_i))
                tot_c = lanes(tot_all[r8, c:c + 1])[0:1]
                mw_c = lanes(mw_all[r8, c:c + 1])[0:1]
                m_new = jnp.maximum(tot_c + m_st, mw_c)
                dec = jnp.exp(tot_c + m_st - m_new)
                gain = jnp.exp(mw_c - m_new)
                c_sts[idx] = (jnp.concatenate([dec, dec], axis=-1) * c_st
                              + jnp.concatenate([gain, gain], axis=-1) * kvs[idx][ci])
                m_sts[idx] = m_new
        for idx, (d, h) in enumerate(chains):
            c_scr[d * ML_HEADS + h] = c_sts[idx]
            m_scr[d * ML_HEADS + h] = jnp.broadcast_to(m_sts[idx], (SUBLANES, LANES))


def _mlstm(pm3, ps3, lane_params):
    b, s, _ = pm3.shape
    nblk = s // ML_BLK
    qw = ML_HEADS * ML_DK
    vw = ML_HEADS * ML_DV
    def specs(tmap):
        blk = lambda j: tmap(j)
        return [
            pl.BlockSpec((None, ML_BLK, qw), lambda i, j: (i, blk(j), _OFF["d_q"] // qw)),
            pl.BlockSpec((None, ML_BLK, qw), lambda i, j: (i, blk(j), _OFF["d_k"] // qw)),
            pl.BlockSpec((None, ML_BLK, vw), lambda i, j: (i, blk(j), _OFF["d_v"] // vw)),
            pl.BlockSpec((None, ML_BLK, LANES), lambda i, j: (i, blk(j), 0)),
        ]
    fwd = lambda j: j
    bwd = lambda j: nblk - 1 - j
    return pl.pallas_call(
        _ml_kernel,
        grid=(b, nblk),
        in_specs=specs(fwd) + specs(bwd) + [pl.BlockSpec((SUBLANES, LANES), lambda i, j: (0, 0))],
        out_specs=[pl.BlockSpec((None, ML_BLK, vw), lambda i, j: (i, j, 0)),
                   pl.BlockSpec((None, ML_BLK, vw), lambda i, j: (i, nblk - 1 - j, 0))],
        out_shape=[jax.ShapeDtypeStruct((b, s, vw), F32)] * 2,
        scratch_shapes=[pltpu.VMEM((2 * ML_HEADS, ML_DK, ML_AUG), F32),
                        pltpu.VMEM((2 * ML_HEADS, SUBLANES, LANES), F32)],
        compiler_params=_cparams(("parallel", "arbitrary")),
        name="mlstm",
    )(pm3, pm3, pm3, ps3, pm3, pm3, pm3, ps3, lane_params)


def _merge_kernel(x_ref, af_ref, ab_ref, df_ref, db_ref, yb_ref, yc_ref, az_ref, dz_ref, do_ref, gl_ref,
                  ag_ref, dg_ref, wb_ref, wo_ref, o_ref):
    d = x_ref.shape[-1]

    def head_rms(x, g):
        outs = []
        for h in range(x.shape[-1] // LANES):
            xh = x[:, h * LANES:(h + 1) * LANES]
            ms = jnp.mean(xh * xh, axis=-1, keepdims=True)
            outs.append(xh * lax.rsqrt(ms + EPS) * g)
        return jnp.concatenate(outs, axis=-1)

    ya = head_rms(af_ref[...] + ab_ref[...], ag_ref[...]) * _silu(az_ref[...].astype(F32))
    yd = _sigmoid(do_ref[...].astype(F32)) * head_rms(df_ref[...] + db_ref[...], dg_ref[...])
    yd = yd * _silu(dz_ref[...].astype(F32))
    twice = None
    for i, y in enumerate((ya.astype(BF16), yb_ref[...], yc_ref[...], yd.astype(BF16))):
        proj = jnp.dot(y, wb_ref[i], preferred_element_type=F32)
        term = proj + jnp.tanh(0.5 * gl_ref[:, i * d:(i + 1) * d].astype(F32)) * proj
        twice = term if twice is None else twice + term
    merged = (0.5 * twice).astype(BF16)
    o_ref[...] = x_ref[...] + jnp.dot(merged, wo_ref[...], preferred_element_type=F32)


def _merge(x2d, af, ab, df, db, yb, yc, pm2, ag, dg, wb, wo, layer, tm=512):
    m, d = x2d.shape
    gw = N_BRANCH * d
    w = BRANCH_W
    tok = pl.BlockSpec((tm, w), lambda i: (i, 0))
    col = lambda name: pl.BlockSpec((tm, w), lambda i: (i, _OFF[name] // w))
    vec = pl.BlockSpec((1, LANES), lambda i: (0, 0))
    return pl.pallas_call(
        _merge_kernel,
        grid=(m // tm,),
        in_specs=[
            pl.BlockSpec((tm, d), lambda i: (i, 0)),
            tok, tok, tok, tok, tok, tok,
            col("a_z"), col("d_z"), col("d_o"),
            pl.BlockSpec((tm, gw), lambda i: (i, _OFF["gate"] // gw)),
            vec, vec,
            pl.BlockSpec((None, N_BRANCH, w, d), lambda i: (layer, 0, 0, 0)),
            pl.BlockSpec((None, d, d), lambda i: (layer, 0, 0)),
        ],
        out_specs=pl.BlockSpec((tm, d), lambda i: (i, 0)),
        out_shape=jax.ShapeDtypeStruct((m, d), F32),
        compiler_params=_cparams(("parallel",)),
        name="merge",
    )(x2d, af, ab, df, db, yb, yc, pm2, pm2, pm2, pm2, ag, dg, wb, wo)


def _rope_lane_tables(s):
    t = jnp.arange(s)
    row = (t // GRID_W).astype(F32)
    col = (t % GRID_W).astype(F32)
    m = GA_DH // 4
    inv = ROPE_THETA ** (-jnp.arange(m, dtype=F32) / m)
    ar = row[:, None] * inv
    ac = col[:, None] * inv
    cos_t = jnp.concatenate([jnp.cos(ar), jnp.cos(ar), jnp.cos(ac), jnp.cos(ac)], axis=-1)
    sin_t = jnp.concatenate([-jnp.sin(ar), jnp.sin(ar), -jnp.sin(ac), jnp.sin(ac)], axis=-1)
    return cos_t.astype(F32), sin_t.astype(F32)


def _lane_tiles(rows):
    padded = []
    for off, vals in rows:
        vals = vals.reshape(vals.shape[0], 1, -1).astype(F32)
        padded.append(jnp.pad(vals, ((0, 0), (0, 0), (off, LANES - off - vals.shape[-1]))))
    tiles = jnp.concatenate(padded, axis=1)
    return jnp.pad(tiles, ((0, 0), (0, SUBLANES - len(rows)), (0, 0)))


def kernel(x, norm_g, w_in, conv_a, dn_a_log, dn_dt_bias, dn_norm_g, na_q_norm, na_k_norm, na_rpb,
           ga_q_norm, ga_k_norm, ml_i_bias, ml_f_bias, ml_norm_g, w_branch, w_out):
    b, s, d = x.shape
    depth = w_in.shape[0]
    hw = BRANCH_W
    cos_t, sin_t = _rope_lane_tables(s)
    w_main = jnp.concatenate([w_in[:, :, o:o + wd] for _, o, wd in _MAIN_SEGS], axis=2).astype(BF16)
    w_small = jnp.pad(jnp.concatenate([w_in[:, :, o:o + 8] for o in _SMALL_SRC], axis=2),
                      ((0, 0), (0, 0), (0, LANES - 8 * len(_SMALL_SRC)))).astype(BF16)
    conv8 = jnp.pad(conv_a.astype(F32), ((0, 0), (0, SUBLANES - DN_CONV), (0, 0)))
    dn_lp = _lane_tiles([(_L_AA, dn_a_log), (_L_AA, dn_dt_bias)])
    ml_lp = _lane_tiles([(_L_DI, ml_i_bias), (_L_DF, ml_f_bias)])
    na_bias = _na_bias_table(na_rpb, s // GRID_W)
    na_qg = jnp.tile(na_q_norm, (1, 2)).reshape(depth, 1, 2 * NA_DH)
    na_kg = jnp.tile(na_k_norm, (1, 2)).reshape(depth, 1, 2 * NA_DH)
    wb_bf, wo_bf = w_branch.astype(BF16), w_out.astype(BF16)

    x2 = x.reshape(b * s, d)
    for l in range(depth):
        pm2, ps2 = _inproj(x2, norm_g[l].reshape(1, d), w_main, w_small, l)
        pm3 = pm2.reshape(b, s, N_MAIN)
        ps3 = ps2.reshape(b, s, LANES)
        o_af, o_ab = _dn_scan(_dn_pre(pm3, ps3, conv8[l], dn_lp[l]))
        h_df, h_db = _mlstm(pm3, ps3, ml_lp[l])
        yb = _natten(pm3, na_bias, na_qg[l], na_kg[l], l)
        yc = _gqa(pm3, cos_t, sin_t, ga_q_norm[l].reshape(1, GA_DH), ga_k_norm[l].reshape(1, GA_DH))
        x2 = _merge(x2, o_af.reshape(b * s, hw), o_ab.reshape(b * s, hw), h_df.reshape(b * s, hw),
                    h_db.reshape(b * s, hw), yb.reshape(b * s, hw), yc.reshape(b * s, hw), pm2,
                    dn_norm_g[l].reshape(1, LANES), ml_norm_g[l].reshape(1, LANES), wb_bf, wo_bf, l)
    return x2.reshape(b, s, d)
```

```python
import functools
import math

import jax
import jax.numpy as jnp
from jax import lax
from jax.experimental import pallas as pl
from jax.experimental.pallas import tpu as pltpu

F32 = jnp.float32
BF16 = jnp.bfloat16

D_MODEL = 1024
GRID_W = 64
N_BRANCH = 4
BRANCH_W = 512
EPS = 1e-6
DN_HEADS, DN_DK, DN_DV, DN_CONV, DN_CHUNK = 4, 128, 128, 5, 64
NA_HEADS, NA_DH, NA_ROWS, NA_COLS = 8, 64, 8, 16
GA_HEADS, GA_KV_HEADS, GA_DH = 4, 2, 128
ROPE_THETA = 10000.0
ML_HEADS, ML_DK, ML_DV, ML_CHUNK = 4, 64, 128, 128

LANES = 128
SUBLANES = 8
VMEM_LIMIT_BYTES = 56 * 1024 * 1024

_O_A_QKV, _O_A_A, _O_A_B, _O_A_Z = 0, 1536, 1544, 1552
_O_B_QKV, _O_B_Z = 2064, 3600
_O_C_Q, _O_C_K, _O_C_V, _O_C_Z = 4112, 4624, 4880, 5136
_O_D_Q, _O_D_K, _O_D_V, _O_D_I, _O_D_F, _O_D_O, _O_D_Z = 5648, 5904, 6160, 6672, 6680, 6688, 7200
_O_GATE = 7712
_MAIN_SEGS = (
    ("a_qkv", _O_A_QKV, 1536), ("b_qkv", _O_B_QKV, 1536), ("a_z", _O_A_Z, 512), ("b_z", _O_B_Z, 512),
    ("gate", _O_GATE, 4096), ("c_q", _O_C_Q, 512), ("c_k", _O_C_K, 256), ("c_v", _O_C_V, 256),
    ("c_z", _O_C_Z, 512), ("d_q", _O_D_Q, 256), ("d_k", _O_D_K, 256), ("d_v", _O_D_V, 512),
    ("d_o", _O_D_O, 512), ("d_z", _O_D_Z, 512),
)
_OFF = {}
_o = 0
for _name, _src, _w in _MAIN_SEGS:
    _OFF[_name] = _o
    _o += _w
N_MAIN = _o
_SMALL_SRC = (_O_A_A, _O_A_B, _O_D_I, _O_D_F)
_L_AA, _L_AB, _L_DI, _L_DF = 0, 8, 16, 24

P_DTYPE = BF16
BLK = 128
NEG = -1e30


def _cparams(sem):
    return pltpu.CompilerParams(dimension_semantics=sem, vmem_limit_bytes=VMEM_LIMIT_BYTES)


def _sigmoid(x):
    return 0.5 * jnp.tanh(0.5 * x) + 0.5


def _silu(x):
    return x * _sigmoid(x)


def _softplus(x):
    return jnp.maximum(x, 0.0) + jnp.log(1.0 + jnp.exp(-jnp.abs(x)))


def _mm(a, b):
    return jnp.dot(a.astype(BF16), b.astype(BF16), preferred_element_type=F32)


def _mm_nt(a, b):
    return lax.dot_general(a.astype(BF16), b.astype(BF16), (((1,), (1,)), ((), ())),
                           preferred_element_type=F32)


def _mm_tn(a, b):
    return lax.dot_general(a.astype(BF16), b.astype(BF16), (((0,), (0,)), ((), ())),
                           preferred_element_type=F32)


def _mask_sum(mask, x):
    m = jnp.where(mask, 1.0, 0.0).astype(BF16)
    x1 = x.astype(BF16)
    r1 = x - x1.astype(F32)
    x2 = r1.astype(BF16)
    x3 = (r1 - x2.astype(F32)).astype(BF16)
    dot = lambda v: jnp.dot(m, v, preferred_element_type=F32)
    return dot(x1) + (dot(x2) + dot(x3))


def _chunk_masks(reverse, chunk):
    i = lax.broadcasted_iota(jnp.int32, (BLK, BLK), 0)
    j = lax.broadcasted_iota(jnp.int32, (BLK, BLK), 1)
    shift = int(math.log2(chunk))
    same = (i >> shift) == (j >> shift)
    if reverse:
        incl = same & (j >= i)
        strict = same & (j > i)
    else:
        incl = same & (j <= i)
        strict = same & (j < i)
    return same, incl, strict, (i == j)


def _as_f32(mask):
    return jnp.where(mask, 1.0, 0.0).astype(F32)


def _inproj_kernel(x_ref, g_ref, w_ref, ws_ref, pm_ref, ps_ref, h_scr):
    @pl.when(pl.program_id(1) == 0)
    def _():
        x = x_ref[...]
        ms = jnp.mean(x * x, axis=-1, keepdims=True)
        h = (x * lax.rsqrt(ms + EPS) * g_ref[...]).astype(BF16)
        h_scr[...] = h
        ps_ref[...] = jnp.dot(h, ws_ref[...], preferred_element_type=F32)

    pm_ref[...] = jnp.dot(h_scr[...], w_ref[...], preferred_element_type=F32).astype(pm_ref.dtype)


def _inproj(x2d, g, w_main, w_small, layer, tm=1024, tn=N_MAIN // 4):
    m, d = x2d.shape
    tm = min(tm, m)
    return pl.pallas_call(
        _inproj_kernel,
        grid=(m // tm, N_MAIN // tn),
        in_specs=[
            pl.BlockSpec((tm, d), lambda i, j: (i, 0)),
            pl.BlockSpec((1, d), lambda i, j: (0, 0)),
            pl.BlockSpec((None, d, tn), lambda i, j: (layer, 0, j)),
            pl.BlockSpec((None, d, LANES), lambda i, j: (layer, 0, 0)),
        ],
        out_specs=[
            pl.BlockSpec((tm, tn), lambda i, j: (i, j)),
            pl.BlockSpec((tm, LANES), lambda i, j: (i, 0)),
        ],
        out_shape=[jax.ShapeDtypeStruct((m, N_MAIN), P_DTYPE), jax.ShapeDtypeStruct((m, LANES), F32)],
        scratch_shapes=[pltpu.VMEM((tm, d), BF16)],
        compiler_params=_cparams(("parallel", "arbitrary")),
        name="inproj",
    )(x2d, g, w_main, w_small)


GQA_TQ = 512


def _gqa_kernel(q_ref, k_ref, v_ref, z_ref, cos_ref, sin_ref, qg_ref, kg_ref, y_ref,
                q_scr, k_scr, v_scr, s0_scr, s1_scr, p0_scr, p1_scr, l0_scr, l1_scr):
    def norm_rope(x, g, cos, sin):
        ms = jnp.mean(x * x, axis=-1, keepdims=True)
        xn = x * lax.rsqrt(ms + EPS) * g
        return xn * cos + pltpu.roll(xn, GA_DH // 2, 1) * sin

    s_len = k_ref.shape[0]
    group = GA_HEADS // GA_KV_HEADS
    assert group == 2
    scale = GA_DH ** -0.5
    n_blk = s_len // GQA_TQ
    s_bufs, p_bufs, l_bufs = (s0_scr, s1_scr), (p0_scr, p1_scr), (l0_scr, l1_scr)
    for kv in range(k_ref.shape[1] // GA_DH):
        _gqa_pipeline(kv, group, s_len, n_blk, scale, norm_rope, q_ref, k_ref, v_ref, z_ref, cos_ref, sin_ref,
                      qg_ref, kg_ref, y_ref, q_scr, k_scr, v_scr, s_bufs, p_bufs, l_bufs)


def _gqa_pipeline(kv, group, s_len, n_blk, scale, norm_rope, q_ref, k_ref, v_ref, z_ref, cos_ref, sin_ref,
                  qg_ref, kg_ref, y_ref, q_scr, k_scr, v_scr, s_bufs, p_bufs, l_bufs):
    kcols = slice(kv * GA_DH, (kv + 1) * GA_DH)
    k_scr[kv] = norm_rope(k_ref[:, kcols].astype(F32), kg_ref[...], cos_ref[...], sin_ref[...]).astype(BF16)
    v_scr[kv] = v_ref[:, kcols].astype(BF16)

    def stacked(head, blk):
        return pl.ds(pl.multiple_of(head * s_len + blk * GQA_TQ, GQA_TQ), GQA_TQ)

    def head_cols(head):
        return slice((kv * group + head) * GA_DH, (kv * group + head + 1) * GA_DH)

    def prep(head, blk):
        rows = pl.ds(pl.multiple_of(blk * GQA_TQ, GQA_TQ), GQA_TQ)
        q = norm_rope(q_ref[rows, head_cols(head)].astype(F32), qg_ref[...], cos_ref[rows, :], sin_ref[rows, :])
        q_scr[stacked(head, blk), :] = q.astype(BF16)

    def logits(head, blk):
        s_bufs[head][...] = _mm_nt(q_scr[stacked(head, blk), :], k_scr[kv]) * scale

    def softmax(slot):
        s = s_bufs[slot][...]
        p = jnp.exp(s - jnp.max(s, axis=-1, keepdims=True))
        l_bufs[slot][...] = jnp.broadcast_to(jnp.sum(p, axis=-1, keepdims=True), (GQA_TQ, GA_DH))
        p_bufs[slot][...] = p.astype(BF16)

    def weighted(head, blk):
        rows = pl.ds(pl.multiple_of(blk * GQA_TQ, GQA_TQ), GQA_TQ)
        cols = head_cols(head)
        o = _mm(p_bufs[head][...], v_scr[kv]) / l_bufs[head][...]
        y_ref[rows, cols] = (o * _silu(z_ref[rows, cols].astype(F32))).astype(y_ref.dtype)

    prep(0, 0)
    prep(1, 0)
    prep(0, 1)
    logits(0, 0)
    logits(1, 0)
    softmax(0)

    def body(j, carry):
        logits(0, j + 1)
        softmax(1)
        weighted(0, j)
        prep(1, j + 1)
        logits(1, j + 1)
        softmax(0)
        weighted(1, j)
        prep(0, jnp.minimum(j + 2, n_blk - 1))
        return carry

    lax.fori_loop(0, n_blk - 1, body, 0)
    softmax(1)
    weighted(0, n_blk - 1)
    weighted(1, n_blk - 1)


def _gqa(pm3, cos_t, sin_t, qg, kg):
    b, s, _ = pm3.shape
    qw = GA_HEADS * GA_DH
    kw = GA_KV_HEADS * GA_DH
    group = GA_HEADS // GA_KV_HEADS
    return pl.pallas_call(
        _gqa_kernel,
        grid=(b,),
        in_specs=[
            pl.BlockSpec((None, s, qw), lambda i: (i, 0, _OFF["c_q"] // qw)),
            pl.BlockSpec((None, s, kw), lambda i: (i, 0, _OFF["c_k"] // kw)),
            pl.BlockSpec((None, s, kw), lambda i: (i, 0, _OFF["c_v"] // kw)),
            pl.BlockSpec((None, s, qw), lambda i: (i, 0, _OFF["c_z"] // qw)),
            pl.BlockSpec((s, GA_DH), lambda i: (0, 0)),
            pl.BlockSpec((s, GA_DH), lambda i: (0, 0)),
            pl.BlockSpec((1, GA_DH), lambda i: (0, 0)),
            pl.BlockSpec((1, GA_DH), lambda i: (0, 0)),
        ],
        out_specs=pl.BlockSpec((None, s, qw), lambda i: (i, 0, 0)),
        out_shape=jax.ShapeDtypeStruct((b, s, BRANCH_W), BF16),
        scratch_shapes=[pltpu.VMEM((group * s, GA_DH), BF16), pltpu.VMEM((GA_KV_HEADS, s, GA_DH), BF16),
                        pltpu.VMEM((GA_KV_HEADS, s, GA_DH), BF16),
                        pltpu.VMEM((GQA_TQ, s), F32), pltpu.VMEM((GQA_TQ, s), F32),
                        pltpu.VMEM((GQA_TQ, s), BF16), pltpu.VMEM((GQA_TQ, s), BF16),
                        pltpu.VMEM((GQA_TQ, GA_DH), F32), pltpu.VMEM((GQA_TQ, GA_DH), F32)],
        compiler_params=_cparams(("parallel",)),
        name="gqa",
    )(pm3, pm3, pm3, pm3, cos_t, sin_t, qg, kg)


NA_ROW_UNROLL = 16


def _na_kernel(q_ref, k_ref, v_ref, z_ref, bias_ref, qg_ref, kg_ref, y_ref, q_scr, k_scr):
    s_len = q_ref.shape[0]
    rows = s_len // GRID_W
    kr = min(NA_ROWS, rows)
    hi = lax.broadcasted_iota(jnp.int32, (2 * NA_DH, 2 * NA_DH), 0) >= NA_DH
    hj = lax.broadcasted_iota(jnp.int32, (2 * NA_DH, 2 * NA_DH), 1) >= NA_DH
    same_head = jnp.where(hi == hj, 1.0, 0.0).astype(BF16)

    def rms_pair(x, g):
        x2 = x * x
        x2_hi = x2.astype(BF16)
        x2_lo = (x2 - x2_hi.astype(F32)).astype(BF16)
        ssq = (jnp.dot(x2_hi, same_head, preferred_element_type=F32)
               + jnp.dot(x2_lo, same_head, preferred_element_type=F32))
        return x * lax.rsqrt(ssq * (1.0 / NA_DH) + EPS) * g

    scale = NA_DH ** -0.5
    assert math.log2(scale).is_integer()
    nkeys = kr * GRID_W
    first = lax.broadcasted_iota(jnp.int32, (s_len, 2 * NA_DH), 1) < NA_DH
    first_q = lax.broadcasted_iota(jnp.int32, (GRID_W, 2 * NA_DH), 1) < NA_DH
    for pair in range(q_ref.shape[1] // (2 * NA_DH)):
        pc = slice(pair * 2 * NA_DH, (pair + 1) * 2 * NA_DH)
        qn = rms_pair(q_ref[:, pc].astype(F32), qg_ref[...]) * scale
        q_scr[0] = jnp.where(first, qn, 0.0).astype(BF16)
        q_scr[1] = jnp.where(first, 0.0, qn).astype(BF16)
        k_scr[...] = rms_pair(k_ref[:, pc].astype(F32), kg_ref[...]).astype(BF16)

        def body(it, carry, pair=pair, pc=pc):
            units = []
            for u in range(NA_ROW_UNROLL):
                r = it * NA_ROW_UNROLL + u
                r0 = jnp.clip(r - kr // 2, 0, rows - kr)
                var = r0 - r + (NA_ROWS - 1)
                qrows = pl.ds(pl.multiple_of(r * GRID_W, GRID_W), GRID_W)
                krows = pl.ds(pl.multiple_of(r0 * GRID_W, GRID_W), nkeys)
                units.append((var, qrows, krows))
            logits = [_mm_nt(jnp.concatenate([q_scr[0, qrows, :], q_scr[1, qrows, :]], axis=0), k_scr[krows, :])
                      + bias_ref[pair, var] for var, qrows, krows in units]
            probs = [jnp.exp(s - jnp.max(s, axis=-1, keepdims=True)) for s in logits]
            sums = [jnp.sum(p, axis=-1, keepdims=True) for p in probs]
            outs = [_mm(p, v_ref[krows, pc]) / l for p, l, (_, _, krows) in zip(probs, sums, units)]
            for o, (_, qrows, _) in zip(outs, units):
                o_pair = jnp.where(first_q, o[:GRID_W], o[GRID_W:])
                y_ref[qrows, pc] = (o_pair * _silu(z_ref[qrows, pc].astype(F32))).astype(y_ref.dtype)
            return carry

        lax.fori_loop(0, rows // NA_ROW_UNROLL, body, 0)


def _na_bias_table(rpb, rows):
    kr = min(NA_ROWS, rows)
    c = jnp.arange(GRID_W)
    c0 = jnp.clip(c - NA_COLS // 2, 0, GRID_W - NA_COLS)
    in_win = (c[None, :] >= c0[:, None]) & (c[None, :] < c0[:, None] + NA_COLS)
    col_off = jnp.clip(c[None, :] - c[:, None], -(NA_COLS - 1), NA_COLS - 1) + NA_COLS - 1
    t = jnp.where(in_win, rpb[..., col_off], NEG)
    ro = jnp.arange(NA_ROWS)[:, None] + jnp.arange(kr)[None, :]
    tv = jnp.take(t, ro, axis=-3)
    tv = jnp.swapaxes(tv, -3, -2).reshape(*rpb.shape[:-2], NA_ROWS, GRID_W, kr * GRID_W)
    lead = rpb.shape[:-3]
    tv = tv.reshape(*lead, rpb.shape[-3] // 2, 2, NA_ROWS, GRID_W, kr * GRID_W)
    tv = jnp.swapaxes(tv, -4, -3)
    return tv.reshape(*lead, rpb.shape[-3] // 2, NA_ROWS, 2 * GRID_W, kr * GRID_W).astype(F32)


def _natten(pm3, bias, qg, kg, layer):
    b, s, _ = pm3.shape
    pw = 2 * NA_DH
    hw = NA_HEADS * NA_DH
    return pl.pallas_call(
        _na_kernel,
        grid=(b,),
        in_specs=[
            pl.BlockSpec((None, s, hw), lambda i: (i, 0, _OFF["b_qkv"] // hw)),
            pl.BlockSpec((None, s, hw), lambda i: (i, 0, _OFF["b_qkv"] // hw + 1)),
            pl.BlockSpec((None, s, hw), lambda i: (i, 0, _OFF["b_qkv"] // hw + 2)),
            pl.BlockSpec((None, s, hw), lambda i: (i, 0, _OFF["b_z"] // hw)),
            pl.BlockSpec((None,) + bias.shape[1:], lambda i: (layer, 0, 0, 0, 0)),
            pl.BlockSpec((1, pw), lambda i: (0, 0)),
            pl.BlockSpec((1, pw), lambda i: (0, 0)),
        ],
        out_specs=pl.BlockSpec((None, s, hw), lambda i: (i, 0, 0)),
        out_shape=jax.ShapeDtypeStruct((b, s, BRANCH_W), BF16),
        scratch_shapes=[pltpu.VMEM((2, s, pw), BF16), pltpu.VMEM((s, pw), BF16)],
        compiler_params=_cparams(("parallel",)),
        name="natten",
    )(pm3, pm3, pm3, pm3, bias, qg, kg)


INV_BASE = 8
PRE_BLK = 4 * BLK


def _inverse_level_masks():
    i = lax.broadcasted_iota(jnp.int32, (BLK, BLK), 0)
    j = lax.broadcasted_iota(jnp.int32, (BLK, BLK), 1)
    same = lambda size: (i >> int(math.log2(size))) == (j >> int(math.log2(size)))
    base = same(INV_BASE)
    joins = []
    size = INV_BASE
    while size < DN_CHUNK:
        joins.append(same(2 * size) & jnp.logical_not(same(size)))
        size *= 2
    return base, joins


def _tri_inverses(l_mats, eyes, level_masks):
    base, joins = level_masks
    ps = [jnp.where(base, -l, 0.0) for l in l_mats]
    ts = [eye + p for eye, p in zip(eyes, ps)]
    for _ in range(int(math.log2(INV_BASE)) - 1):
        ps = [_mm(p, p) for p in ps]
        ts = [t + _mm(t, p) for t, p in zip(ts, ps)]
    for join in joins:
        mids = [_mm(jnp.where(join, l, 0.0), t) for l, t in zip(l_mats, ts)]
        ts = [t - _mm(t, mid) for t, mid in zip(ts, mids)]
    return ts


DN_HW = DN_HEADS * LANES
PK_W, PK_QD, PK_KD, PK_QK = (i * DN_HW for i in range(4))


def _dn_pre_kernel(qkv_ref, sm_ref, cw_ref, lp_ref, uf_ref, ub_ref, pkf_ref, pkb_ref, gtf_ref, gtb_ref):
    n = pl.program_id(1)
    s_len = qkv_ref.shape[0]
    assert qkv_ref.dtype == BF16
    halo = 2 * SUBLANES
    n_sub = sm_ref.shape[0] // BLK
    chunks_per_blk = BLK // DN_CHUNK
    out_row = lax.broadcasted_iota(jnp.int32, (BLK, BLK + 2 * halo), 0)
    in_row = lax.broadcasted_iota(jnp.int32, (BLK, BLK + 2 * halo), 1)
    shifts = {j: jnp.where(in_row == out_row + (halo + j - DN_CONV // 2), 1.0, 0.0).astype(BF16)
              for j in range(DN_CONV) if j != DN_CONV // 2}
    dir_masks = [_chunk_masks(d == 1, DN_CHUNK) for d in range(2)]
    level_masks = _inverse_level_masks()
    outs = ((uf_ref, pkf_ref, gtf_ref), (ub_ref, pkb_ref, gtb_ref))

    def l2n(x):
        return x * lax.rsqrt(jnp.sum(x * x, axis=-1, keepdims=True) + EPS)

    blocks = []
    for sub in range(n_sub):
        t0 = pl.multiple_of((n * n_sub + sub) * BLK, BLK)
        pstart = pl.multiple_of(jnp.maximum(t0 - halo, 0), halo)
        nstart = pl.multiple_of(jnp.minimum(t0 + BLK, s_len - halo), halo)
        prev = qkv_ref[pl.ds(pstart, halo), :]
        prev = jnp.where(t0 > 0, prev, jnp.zeros_like(prev))
        cur = qkv_ref[pl.ds(t0, BLK), :]
        nxt = qkv_ref[pl.ds(nstart, halo), :]
        nxt = jnp.where(t0 + BLK < s_len, nxt, jnp.zeros_like(nxt))
        xw = jnp.concatenate([prev, cur, nxt], axis=0)
        conv = cur.astype(F32) * cw_ref[DN_CONV // 2:DN_CONV // 2 + 1, :]
        for j, shift in shifts.items():
            conv = conv + jnp.dot(shift, xw, preferred_element_type=F32) * cw_ref[j:j + 1, :]
        conv = _silu(conv)

        rows = slice(sub * BLK, (sub + 1) * BLK)
        sm = sm_ref[rows, :]
        g_all = -jnp.exp(lp_ref[0:1, :]) * _softplus(sm + lp_ref[1:2, :])
        beta_all = _sigmoid(sm)
        per_dir = []
        for d in range(2):
            same, incl, strict, eye_b = dir_masks[d]
            gc = _mask_sum(incl, g_all)
            tot = _mask_sum(same, g_all)
            per_dir.append((incl, strict, _as_f32(eye_b), gc, gc.T, tot))
        blocks.append((sub, rows, conv, beta_all, per_dir))

    heads = []
    for sub, rows, conv, beta_all, per_dir in blocks:
        for h in range(DN_HEADS):
            q = l2n(conv[:, h * DN_DK:(h + 1) * DN_DK]) * (DN_DK ** -0.5)
            k = l2n(conv[:, (DN_HEADS + h) * DN_DK:(DN_HEADS + h + 1) * DN_DK])
            v = conv[:, 2 * DN_HEADS * DN_DK + h * DN_DV:2 * DN_HEADS * DN_DK + (h + 1) * DN_DV]
            heads.append((sub, rows, beta_all, per_dir, h, q, k, v))
    kks = [_mm_nt(hd[6], hd[6]) for hd in heads]
    qks = [_mm_nt(hd[5], hd[6]) for hd in heads]

    l_mats, rhss, eyes, slots = [], [], [], []
    for (sub, rows, beta_all, per_dir, h, q, k, v), kk, qk in zip(heads, kks, qks):
        hc = slice(h * LANES, (h + 1) * LANES)
        for d in range(2):
            incl, strict, eye_f, gc, gct, tot = per_dir[d]
            u_ref, pk_ref, gt_ref = outs[d]
            pk = lambda off: slice(off + h * LANES, off + (h + 1) * LANES)
            c = _L_AA + d * DN_HEADS + h
            gcol = gc[:, c:c + 1]
            grow = gct[c:c + 1, :]
            tcol = tot[:, c:c + 1]
            beta = beta_all[:, _L_AB + d * DN_HEADS + h:_L_AB + d * DN_HEADS + h + 1]
            decay = jnp.where(incl, jnp.exp(jnp.where(incl, gcol - grow, 0.0)), 0.0)
            l_mats.append(jnp.where(strict, beta * kk * decay, 0.0))
            egc = jnp.exp(gcol)
            rhss.append(jnp.concatenate([v * beta, k * (beta * egc)], axis=-1).astype(BF16))
            eyes.append(eye_f)
            slots.append((rows, h, d))
            pk_ref[rows, pk(PK_QD)] = (q * egc).astype(pk_ref.dtype)
            pk_ref[rows, pk(PK_KD)] = (k * jnp.exp(tcol - gcol)).astype(pk_ref.dtype)
            pk_ref[rows, pk(PK_QK)] = (qk * decay).astype(pk_ref.dtype)
            gtot = jnp.exp(tcol)
            for ci in range(chunks_per_blk):
                g0 = (sub * chunks_per_blk + ci) * SUBLANES
                gt_ref[g0:g0 + SUBLANES, hc] = jnp.broadcast_to(
                    gtot[ci * DN_CHUNK:ci * DN_CHUNK + SUBLANES, :], (SUBLANES, LANES))

    t_invs = _tri_inverses(l_mats, eyes, level_masks)
    sols = [_mm(t, rhs) for t, rhs in zip(t_invs, rhss)]
    for (rows, h, d), sol in zip(slots, sols):
        u_ref, pk_ref = outs[d][0], outs[d][1]
        u_ref[rows, h * LANES:(h + 1) * LANES] = sol[:, :DN_DV]
        pk_ref[rows, PK_W + h * LANES:PK_W + (h + 1) * LANES] = sol[:, DN_DV:].astype(pk_ref.dtype)


def _dn_pre(pm3, ps3, conv_w8, lane_params):
    b, s, _ = pm3.shape
    nblk = s // PRE_BLK
    gt_rows = PRE_BLK // DN_CHUNK * SUBLANES
    wq = DN_HEADS * (2 * DN_DK + DN_DV)
    hw = DN_HEADS * LANES
    tok = lambda i, j: (i, j, 0)
    big = lambda dt: jax.ShapeDtypeStruct((b, s, hw), dt)
    gts = jax.ShapeDtypeStruct((b, nblk * gt_rows, hw), F32)
    bs_tok = pl.BlockSpec((None, PRE_BLK, hw), tok)
    bs_gt = pl.BlockSpec((None, gt_rows, hw), tok)
    bs_pk = pl.BlockSpec((None, PRE_BLK, 4 * hw), tok)
    packed = jax.ShapeDtypeStruct((b, s, 4 * hw), BF16)
    return pl.pallas_call(
        _dn_pre_kernel,
        grid=(b, nblk),
        in_specs=[
            pl.BlockSpec((None, s, wq), lambda i, j: (i, 0, _OFF["a_qkv"] // wq)),
            pl.BlockSpec((None, PRE_BLK, LANES), tok),
            pl.BlockSpec((SUBLANES, wq), lambda i, j: (0, 0)),
            pl.BlockSpec((SUBLANES, LANES), lambda i, j: (0, 0)),
        ],
        out_specs=[bs_tok, bs_tok, bs_pk, bs_pk, bs_gt, bs_gt],
        out_shape=[big(F32), big(F32), packed, packed, gts, gts],
        compiler_params=_cparams(("parallel", "arbitrary")),
        name="dn_pre",
    )(pm3, ps3, conv_w8, lane_params)


SCAN_BLK = 4 * BLK


def _dn_scan_kernel(uf_ref, pkf_ref, gtf_ref, ub_ref, pkb_ref, gtb_ref, of_ref, ob_ref, st_scr):
    @pl.when(pl.program_id(1) == 0)
    def _():
        st_scr[...] = jnp.zeros_like(st_scr)

    per_blk = BLK // DN_CHUNK
    nchunk = SCAN_BLK // DN_CHUNK
    zeros_c = jnp.zeros((DN_CHUNK, DN_DV), F32)
    streams = ((uf_ref, pkf_ref, gtf_ref, of_ref, range(nchunk)),
               (ub_ref, pkb_ref, gtb_ref, ob_ref, range(nchunk - 1, -1, -1)))
    chains = [(d, h) + streams[d] for d in range(2) for h in range(DN_HEADS)]
    states = [st_scr[d * DN_HEADS + h] for d, h, *_ in chains]
    for step in range(nchunk):
        rs, v_pads = [], []
        for (d, h, u_ref, pk_ref, gt_ref, o_ref, order), state in zip(chains, states):
            rows = slice(order[step] * DN_CHUNK, (order[step] + 1) * DN_CHUNK)
            w = pk_ref[rows, PK_W + h * LANES:PK_W + (h + 1) * LANES]
            qd = pk_ref[rows, PK_QD + h * LANES:PK_QD + (h + 1) * LANES]
            rs.append(_mm(jnp.concatenate([w, qd], axis=0), state))
        for (d, h, u_ref, pk_ref, gt_ref, o_ref, order), r in zip(chains, rs):
            ci = order[step]
            rows = slice(ci * DN_CHUNK, (ci + 1) * DN_CHUNK)
            parts = [zeros_c] * per_blk
            parts[ci % per_blk] = u_ref[rows, h * LANES:(h + 1) * LANES] - r[:DN_CHUNK]
            v_pads.append(jnp.concatenate(parts, axis=0))
        new_states = []
        for (d, h, u_ref, pk_ref, gt_ref, o_ref, order), r, v_pad, state in zip(chains, rs, v_pads, states):
            hc = slice(h * LANES, (h + 1) * LANES)
            ci = order[step]
            rows = slice(ci * DN_CHUNK, (ci + 1) * DN_CHUNK)
            blk_rows = slice((ci // per_blk) * BLK, (ci // per_blk + 1) * BLK)
            o_ref[rows, hc] = r[DN_CHUNK:] + _mm(pk_ref[rows, PK_QK + h * LANES:PK_QK + (h + 1) * LANES], v_pad)
            gt = gt_ref[ci * SUBLANES:ci * SUBLANES + 1, hc]
            kd = pk_ref[blk_rows, PK_KD + h * LANES:PK_KD + (h + 1) * LANES]
            new_states.append(state * gt + _mm_tn(kd, v_pad))
        states = new_states
    for (d, h, *_), state in zip(chains, states):
        st_scr[d * DN_HEADS + h] = state


def _dn_scan(pre):
    uf, ub, pkf, pkb, gtf, gtb = pre
    b, s, hw = uf.shape
    nblk = s // SCAN_BLK
    gt_rows = SCAN_BLK // DN_CHUNK * SUBLANES
    fwd = lambda i, j: (i, j, 0)
    bwd = lambda i, j: (i, nblk - 1 - j, 0)
    def specs(imap):
        return [pl.BlockSpec((None, SCAN_BLK, hw), imap), pl.BlockSpec((None, SCAN_BLK, 4 * hw), imap),
                pl.BlockSpec((None, gt_rows, hw), imap)]
    return pl.pallas_call(
        _dn_scan_kernel,
        grid=(b, nblk),
        in_specs=specs(fwd) + specs(bwd),
        out_specs=[pl.BlockSpec((None, SCAN_BLK, hw), fwd), pl.BlockSpec((None, SCAN_BLK, hw), bwd)],
        out_shape=[jax.ShapeDtypeStruct((b, s, hw), F32)] * 2,
        scratch_shapes=[pltpu.VMEM((2 * DN_HEADS, DN_DK, DN_DV), F32)],
        compiler_params=_cparams(("parallel", "arbitrary")),
        name="dn_scan",
    )(uf, pkf, gtf, ub, pkb, gtb)


ML_AUG = 2 * LANES
ML_GROUP = 4
ML_BLK = 4 * BLK


def _ml_kernel(qf_ref, kf_ref, vf_ref, smf_ref, qb_ref, kb_ref, vb_ref, smb_ref, lp_ref,
               hf_ref, hb_ref, c_scr, m_scr):
    @pl.when(pl.program_id(1) == 0)
    def _():
        c_scr[...] = jnp.zeros_like(c_scr)
        m_scr[...] = jnp.zeros_like(m_scr)

    nchunk = BLK // ML_CHUNK
    ones_col = jnp.ones((BLK, LANES), BF16)
    zeros_aug = jnp.zeros((ML_CHUNK, ML_AUG), BF16)
    streams = ((qf_ref, kf_ref, vf_ref, smf_ref, hf_ref, range(nchunk)),
               (qb_ref, kb_ref, vb_ref, smb_ref, hb_ref, range(nchunk - 1, -1, -1)))
    lanes = lambda col: jnp.broadcast_to(col, (col.shape[0], LANES))
    n_sub = smf_ref.shape[0] // BLK
    units = [(d, p if d == 0 else n_sub - 1 - p) for p in range(n_sub) for d in range(2)]
    for d, sub in units:
        _ml_block(d, sub, streams[d], lp_ref, c_scr, m_scr, lanes, ones_col, zeros_aug, nchunk)


def _ml_block(d, sub, stream, lp_ref, c_scr, m_scr, lanes, ones_col, zeros_aug, nchunk):
    q_ref, k_ref, v_ref, sm_ref, h_ref, order = stream
    blk_rows = slice(sub * BLK, (sub + 1) * BLK)
    same, incl, _, _ = _chunk_masks(d == 1, ML_CHUNK)
    sm = sm_ref[blk_rows, :]
    ig_all = sm + lp_ref[0:1, :]
    x = sm + lp_ref[1:2, :]
    lf_all = jnp.minimum(x, 0.0) - jnp.log(1.0 + jnp.exp(-jnp.abs(x)))
    lf_all = pltpu.roll(lf_all, LANES - (_L_DF - _L_DI), 1)
    bc_all = _mask_sum(incl, lf_all)
    tot_all = _mask_sum(same, lf_all)
    a_all = ig_all - bc_all
    mwa_all = jnp.concatenate(
        [jnp.broadcast_to(jnp.max(a_all[ci * ML_CHUNK:(ci + 1) * ML_CHUNK], axis=0, keepdims=True),
                          (ML_CHUNK, LANES)) for ci in range(nchunk)], axis=0)
    a_t, w_all, mw_all = a_all.T, jnp.exp(a_all - mwa_all), tot_all + mwa_all

    for chains in [[(d, h) for h in range(g, g + ML_GROUP)] for g in range(0, ML_HEADS, ML_GROUP)]:
        ins = []
        for _, h in chains:
            q = q_ref[blk_rows, h * ML_DK:(h + 1) * ML_DK].astype(BF16)
            k = k_ref[blk_rows, h * ML_DK:(h + 1) * ML_DK].astype(F32) * (ML_DK ** -0.5)
            v_aug = jnp.concatenate([v_ref[blk_rows, h * ML_DV:(h + 1) * ML_DV].astype(BF16), ones_col], axis=-1)
            ins.append((q, k, v_aug))
        qks = [_mm_nt(q, k) for q, k, _ in ins]

        mids = []
        for (d, h), (q, k, v_aug), qk in zip(chains, ins, qks):
            c = _L_DI + d * ML_HEADS + h
            b_l = lanes(bc_all[:, c:c + 1])
            dlog = jnp.where(incl, b_l + a_t[c:c + 1, :], NEG)
            m_intra = lanes(jnp.max(dlog, axis=-1, keepdims=True))
            s_intra = qk * jnp.exp(dlog - m_intra)
            wk = (k * lanes(w_all[:, c:c + 1])[:, :ML_DK]).astype(BF16)
            mids.append((b_l, m_intra, s_intra.astype(BF16), wk))
        p_intras = [_mm(s_b, v_aug) for (_, _, s_b, _), (_, _, v_aug) in zip(mids, ins)]
        kvs = []
        for (_, _, _, wk), (_, _, v_aug) in zip(mids, ins):
            per_chunk = []
            for ci in range(nchunk):
                parts = [zeros_aug] * nchunk
                parts[ci] = v_aug[ci * ML_CHUNK:(ci + 1) * ML_CHUNK]
                per_chunk.append(_mm_tn(wk, jnp.concatenate(parts, axis=0)))
            kvs.append(per_chunk)

        c_sts = [c_scr[d * ML_HEADS + h] for d, h in chains]
        m_sts = [m_scr[d * ML_HEADS + h][0:1, :] for d, h in chains]
        for step in range(nchunk):
            qcs = []
            ci = order[step]
            for (q, _, _), c_st in zip(ins, c_sts):
                qcs.append(_mm(q[ci * ML_CHUNK:(ci + 1) * ML_CHUNK], c_st))
            for idx, (_, h) in enumerate(chains):
                b_l, m_intra, _, _ = mids[idx]
                c = _L_DI + d * ML_HEADS + h
                rows = slice(ci * ML_CHUNK, (ci + 1) * ML_CHUNK)
                out_rows = slice(sub * BLK + ci * ML_CHUNK, sub * BLK + (ci + 1) * ML_CHUNK)
                r8 = slice(ci * ML_CHUNK, ci * ML_CHUNK + SUBLANES)
                m_st, c_st, qc = m_sts[idx], c_sts[idx], qcs[idx]
                m_inter = b_l[rows] + m_st
                m_i = jnp.maximum(m_intra[rows], m_inter)
                f_i = jnp.exp(m_intra[rows] - m_i)
                inter = jnp.exp(m_inter - m_i)
                both = (jnp.concatenate([inter, inter], axis=-1) * qc
                        + jnp.concatenate([f_i, f_i], axis=-1) * p_intras[idx][rows])
                numer, denom = both[:, :ML_DV], both[:, ML_DV:]
                h_ref[out_rows, h * ML_DV:(h + 1) * ML_DV] = numer / jnp.maximum(jnp.abs(denom), jnp.exp(-m_i))
                tot_c = lanes(tot_all[r8, c:c + 1])[0:1]
                mw_c = lanes(mw_all[r8, c:c + 1])[0:1]
                m_new = jnp.maximum(tot_c + m_st, mw_c)
                dec = jnp.exp(tot_c + m_st - m_new)
                gain = jnp.exp(mw_c - m_new)
                c_sts[idx] = (jnp.concatenate([dec, dec], axis=-1) * c_st
                              + jnp.concatenate([gain, gain], axis=-1) * kvs[idx][ci])
                m_sts[idx] = m_new
        for idx, (d, h) in enumerate(chains):
            c_scr[d * ML_HEADS + h] = c_sts[idx]
            m_scr[d * ML_HEADS + h] = jnp.broadcast_to(m_sts[idx], (SUBLANES, LANES))


def _mlstm(pm3, ps3, lane_params):
    b, s, _ = pm3.shape
    nblk = s // ML_BLK
    qw = ML_HEADS * ML_DK
    vw = ML_HEADS * ML_DV
    def specs(tmap):
        blk = lambda j: tmap(j)
        return [
            pl.BlockSpec((None, ML_BLK, qw), lambda i, j: (i, blk(j), _OFF["d_q"] // qw)),
            pl.BlockSpec((None, ML_BLK, qw), lambda i, j: (i, blk(j), _OFF["d_k"] // qw)),
            pl.BlockSpec((None, ML_BLK, vw), lambda i, j: (i, blk(j), _OFF["d_v"] // vw)),
            pl.BlockSpec((None, ML_BLK, LANES), lambda i, j: (i, blk(j), 0)),
        ]
    fwd = lambda j: j
    bwd = lambda j: nblk - 1 - j
    return pl.pallas_call(
        _ml_kernel,
        grid=(b, nblk),
        in_specs=specs(fwd) + specs(bwd) + [pl.BlockSpec((SUBLANES, LANES), lambda i, j: (0, 0))],
        out_specs=[pl.BlockSpec((None, ML_BLK, vw), lambda i, j: (i, j, 0)),
                   pl.BlockSpec((None, ML_BLK, vw), lambda i, j: (i, nblk - 1 - j, 0))],
        out_shape=[jax.ShapeDtypeStruct((b, s, vw), F32)] * 2,
        scratch_shapes=[pltpu.VMEM((2 * ML_HEADS, ML_DK, ML_AUG), F32),
                        pltpu.VMEM((2 * ML_HEADS, SUBLANES, LANES), F32)],
        compiler_params=_cparams(("parallel", "arbitrary")),
        name="mlstm",
    )(pm3, pm3, pm3, ps3, pm3, pm3, pm3, ps3, lane_params)


def _merge_kernel(x_ref, af_ref, ab_ref, df_ref, db_ref, yb_ref, yc_ref, az_ref, dz_ref, do_ref, gl_ref,
                  ag_ref, dg_ref, wb_ref, wo_ref, o_ref):
    d = x_ref.shape[-1]

    def head_rms(x, g):
        outs = []
        for h in range(x.shape[-1] // LANES):
            xh = x[:, h * LANES:(h + 1) * LANES]
            ms = jnp.mean(xh * xh, axis=-1, keepdims=True)
            outs.append(xh * lax.rsqrt(ms + EPS) * g)
        return jnp.concatenate(outs, axis=-1)

    ya = head_rms(af_ref[...] + ab_ref[...], ag_ref[...]) * _silu(az_ref[...].astype(F32))
    yd = _sigmoid(do_ref[...].astype(F32)) * head_rms(df_ref[...] + db_ref[...], dg_ref[...])
    yd = yd * _silu(dz_ref[...].astype(F32))
    twice = None
    for i, y in enumerate((ya.astype(BF16), yb_ref[...], yc_ref[...], yd.astype(BF16))):
        proj = jnp.dot(y, wb_ref[i], preferred_element_type=F32)
        term = proj + jnp.tanh(0.5 * gl_ref[:, i * d:(i + 1) * d].astype(F32)) * proj
        twice = term if twice is None else twice + term
    merged = (0.5 * twice).astype(BF16)
    o_ref[...] = x_ref[...] + jnp.dot(merged, wo_ref[...], preferred_element_type=F32)


def _merge(x2d, af, ab, df, db, yb, yc, pm2, ag, dg, wb, wo, layer, tm=512):
    m, d = x2d.shape
    gw = N_BRANCH * d
    w = BRANCH_W
    tok = pl.BlockSpec((tm, w), lambda i: (i, 0))
    col = lambda name: pl.BlockSpec((tm, w), lambda i: (i, _OFF[name] // w))
    vec = pl.BlockSpec((1, LANES), lambda i: (0, 0))
    return pl.pallas_call(
        _merge_kernel,
        grid=(m // tm,),
        in_specs=[
            pl.BlockSpec((tm, d), lambda i: (i, 0)),
            tok, tok, tok, tok, tok, tok,
            col("a_z"), col("d_z"), col("d_o"),
            pl.BlockSpec((tm, gw), lambda i: (i, _OFF["gate"] // gw)),
            vec, vec,
            pl.BlockSpec((None, N_BRANCH, w, d), lambda i: (layer, 0, 0, 0)),
            pl.BlockSpec((None, d, d), lambda i: (layer, 0, 0)),
        ],
        out_specs=pl.BlockSpec((tm, d), lambda i: (i, 0)),
        out_shape=jax.ShapeDtypeStruct((m, d), F32),
        compiler_params=_cparams(("parallel",)),
        name="merge",
    )(x2d, af, ab, df, db, yb, yc, pm2, pm2, pm2, pm2, ag, dg, wb, wo)


def _rope_lanes(a):
    m = GA_DH // 4
    shape = a.shape
    a = a.reshape(*shape[:-1], shape[-1] // GA_DH, 4, m)
    a = jnp.stack([a[..., 0, :], a[..., 2, :], a[..., 1, :], a[..., 3, :]], axis=-2)
    return a.reshape(shape)


def _rope_lane_tables(s):
    t = jnp.arange(s)
    row = (t // GRID_W).astype(F32)
    col = (t % GRID_W).astype(F32)
    m = GA_DH // 4
    inv = ROPE_THETA ** (-jnp.arange(m, dtype=F32) / m)
    ar = row[:, None] * inv
    ac = col[:, None] * inv
    cos_t = jnp.concatenate([jnp.cos(ar), jnp.cos(ac), jnp.cos(ar), jnp.cos(ac)], axis=-1)
    sin_t = jnp.concatenate([-jnp.sin(ar), -jnp.sin(ac), jnp.sin(ar), jnp.sin(ac)], axis=-1)
    return cos_t.astype(F32), sin_t.astype(F32)


def _main_columns(w):
    segs = []
    for name, o, wd in _MAIN_SEGS:
        seg = w[..., o:o + wd]
        segs.append(_rope_lanes(seg) if name in ("c_q", "c_k") else seg)
    return jnp.concatenate(segs, axis=-1)


def _lane_tiles(rows):
    padded = []
    for off, vals in rows:
        vals = vals.reshape(vals.shape[0], 1, -1).astype(F32)
        padded.append(jnp.pad(vals, ((0, 0), (0, 0), (off, LANES - off - vals.shape[-1]))))
    tiles = jnp.concatenate(padded, axis=1)
    return jnp.pad(tiles, ((0, 0), (0, SUBLANES - len(rows)), (0, 0)))


def kernel(x, norm_g, w_in, conv_a, dn_a_log, dn_dt_bias, dn_norm_g, na_q_norm, na_k_norm, na_rpb,
           ga_q_norm, ga_k_norm, ml_i_bias, ml_f_bias, ml_norm_g, w_branch, w_out):
    b, s, d = x.shape
    depth = w_in.shape[0]
    hw = BRANCH_W
    cos_t, sin_t = _rope_lane_tables(s)
    w_main = _main_columns(w_in).astype(BF16)
    w_small = jnp.pad(jnp.concatenate([w_in[:, :, o:o + 8] for o in _SMALL_SRC], axis=2),
                      ((0, 0), (0, 0), (0, LANES - 8 * len(_SMALL_SRC)))).astype(BF16)
    conv8 = jnp.pad(conv_a.astype(F32), ((0, 0), (0, SUBLANES - DN_CONV), (0, 0)))
    dn_lp = _lane_tiles([(_L_AA, dn_a_log), (_L_AA, dn_dt_bias)])
    ml_lp = _lane_tiles([(_L_DI, ml_i_bias), (_L_DF, ml_f_bias)])
    na_bias = _na_bias_table(na_rpb, s // GRID_W)
    na_qg = jnp.tile(na_q_norm, (1, 2)).reshape(depth, 1, 2 * NA_DH)
    na_kg = jnp.tile(na_k_norm, (1, 2)).reshape(depth, 1, 2 * NA_DH)
    ga_qg = _rope_lanes(ga_q_norm).reshape(depth, 1, GA_DH)
    ga_kg = _rope_lanes(ga_k_norm).reshape(depth, 1, GA_DH)
    wb_bf, wo_bf = w_branch.astype(BF16), w_out.astype(BF16)

    x2 = x.reshape(b * s, d)
    for l in range(depth):
        pm2, ps2 = _inproj(x2, norm_g[l].reshape(1, d), w_main, w_small, l)
        pm3 = pm2.reshape(b, s, N_MAIN)
        ps3 = ps2.reshape(b, s, LANES)
        o_af, o_ab = _dn_scan(_dn_pre(pm3, ps3, conv8[l], dn_lp[l]))
        h_df, h_db = _mlstm(pm3, ps3, ml_lp[l])
        yb = _natten(pm3, na_bias, na_qg[l], na_kg[l], l)
        yc = _gqa(pm3, cos_t, sin_t, ga_qg[l], ga_kg[l])
        x2 = _merge(x2, o_af.reshape(b * s, hw), o_ab.reshape(b * s, hw), h_df.reshape(b * s, hw),
                    h_db.reshape(b * s, hw), yb.reshape(b * s, hw), yc.reshape(b * s, hw), pm2,
                    dn_norm_g[l].reshape(1, LANES), ml_norm_g[l].reshape(1, LANES), wb_bf, wo_bf, l)
    return x2.reshape(b, s, d)
```

```python
import functools
import math

import jax
import jax.numpy as jnp
from jax import lax
from jax.experimental import pallas as pl
from jax.experimental.pallas import tpu as pltpu

F32 = jnp.float32
BF16 = jnp.bfloat16

D_MODEL = 1024
GRID_W = 64
N_BRANCH = 4
BRANCH_W = 512
EPS = 1e-6
DN_HEADS, DN_DK, DN_DV, DN_CONV, DN_CHUNK = 4, 128, 128, 5, 64
NA_HEADS, NA_DH, NA_ROWS, NA_COLS = 8, 64, 8, 16
GA_HEADS, GA_KV_HEADS, GA_DH = 4, 2, 128
ROPE_THETA = 10000.0
ML_HEADS, ML_DK, ML_DV, ML_CHUNK = 4, 64, 128, 128

LANES = 128
SUBLANES = 8
VMEM_LIMIT_BYTES = 56 * 1024 * 1024

_O_A_QKV, _O_A_A, _O_A_B, _O_A_Z = 0, 1536, 1544, 1552
_O_B_QKV, _O_B_Z = 2064, 3600
_O_C_Q, _O_C_K, _O_C_V, _O_C_Z = 4112, 4624, 4880, 5136
_O_D_Q, _O_D_K, _O_D_V, _O_D_I, _O_D_F, _O_D_O, _O_D_Z = 5648, 5904, 6160, 6672, 6680, 6688, 7200
_O_GATE = 7712
_MAIN_SEGS = (
    ("a_qkv", _O_A_QKV, 1536), ("b_qkv", _O_B_QKV, 1536), ("a_z", _O_A_Z, 512), ("b_z", _O_B_Z, 512),
    ("gate", _O_GATE, 4096), ("c_q", _O_C_Q, 512), ("c_k", _O_C_K, 256), ("c_v", _O_C_V, 256),
    ("c_z", _O_C_Z, 512), ("d_q", _O_D_Q, 256), ("d_k", _O_D_K, 256), ("d_v", _O_D_V, 512),
    ("d_o", _O_D_O, 512), ("d_z", _O_D_Z, 512),
)
_OFF = {}
_o = 0
for _name, _src, _w in _MAIN_SEGS:
    _OFF[_name] = _o
    _o += _w
N_MAIN = _o
_SMALL_SRC = (_O_A_A, _O_A_B, _O_D_I, _O_D_F)
_L_AA, _L_AB, _L_DI, _L_DF = 0, 8, 16, 24

P_DTYPE = BF16
BLK = 128
NEG = -1e30


def _cparams(sem):
    return pltpu.CompilerParams(dimension_semantics=sem, vmem_limit_bytes=VMEM_LIMIT_BYTES)


def _sigmoid(x):
    return 0.5 * jnp.tanh(0.5 * x) + 0.5


def _silu(x):
    return x * _sigmoid(x)


def _softplus(x):
    return jnp.maximum(x, 0.0) + jnp.log(1.0 + jnp.exp(-jnp.abs(x)))


def _mm(a, b):
    return jnp.dot(a.astype(BF16), b.astype(BF16), preferred_element_type=F32)


def _mm_nt(a, b):
    return lax.dot_general(a.astype(BF16), b.astype(BF16), (((1,), (1,)), ((), ())),
                           preferred_element_type=F32)


def _mm_tn(a, b):
    return lax.dot_general(a.astype(BF16), b.astype(BF16), (((0,), (0,)), ((), ())),
                           preferred_element_type=F32)


def _mask_sum(mask, x):
    m = jnp.where(mask, 1.0, 0.0).astype(BF16)
    x1 = x.astype(BF16)
    r1 = x - x1.astype(F32)
    x2 = r1.astype(BF16)
    x3 = (r1 - x2.astype(F32)).astype(BF16)
    dot = lambda v: jnp.dot(m, v, preferred_element_type=F32)
    return dot(x1) + (dot(x2) + dot(x3))


def _chunk_masks(reverse, chunk):
    i = lax.broadcasted_iota(jnp.int32, (BLK, BLK), 0)
    j = lax.broadcasted_iota(jnp.int32, (BLK, BLK), 1)
    shift = int(math.log2(chunk))
    same = (i >> shift) == (j >> shift)
    if reverse:
        incl = same & (j >= i)
        strict = same & (j > i)
    else:
        incl = same & (j <= i)
        strict = same & (j < i)
    return same, incl, strict, (i == j)


def _chunk_totals(cum, chunk, reverse):
    pieces = []
    for c0 in range(0, BLK, chunk):
        r = c0 if reverse else c0 + chunk - 1
        pieces.append(jnp.broadcast_to(cum[r:r + 1, :], (chunk, cum.shape[1])))
    return jnp.concatenate(pieces, axis=0)


def _as_f32(mask):
    return jnp.where(mask, 1.0, 0.0).astype(F32)


def _inproj_kernel(x_ref, g_ref, w_ref, ws_ref, pm_ref, ps_ref, h_scr):
    @pl.when(pl.program_id(1) == 0)
    def _():
        x = x_ref[...]
        ms = jnp.mean(x * x, axis=-1, keepdims=True)
        h = (x * lax.rsqrt(ms + EPS) * g_ref[...]).astype(BF16)
        h_scr[...] = h
        ps_ref[...] = jnp.dot(h, ws_ref[...], preferred_element_type=F32)

    pm_ref[...] = jnp.dot(h_scr[...], w_ref[...], preferred_element_type=F32).astype(pm_ref.dtype)


def _inproj(x2d, g, w_main, w_small, layer, tm=1024, tn=N_MAIN // 4):
    m, d = x2d.shape
    tm = min(tm, m)
    return pl.pallas_call(
        _inproj_kernel,
        grid=(m // tm, N_MAIN // tn),
        in_specs=[
            pl.BlockSpec((tm, d), lambda i, j: (i, 0)),
            pl.BlockSpec((1, d), lambda i, j: (0, 0)),
            pl.BlockSpec((None, d, tn), lambda i, j: (layer, 0, j)),
            pl.BlockSpec((None, d, LANES), lambda i, j: (layer, 0, 0)),
        ],
        out_specs=[
            pl.BlockSpec((tm, tn), lambda i, j: (i, j)),
            pl.BlockSpec((tm, LANES), lambda i, j: (i, 0)),
        ],
        out_shape=[jax.ShapeDtypeStruct((m, N_MAIN), P_DTYPE), jax.ShapeDtypeStruct((m, LANES), F32)],
        scratch_shapes=[pltpu.VMEM((tm, d), BF16)],
        compiler_params=_cparams(("parallel", "arbitrary")),
        name="inproj",
    )(x2d, g, w_main, w_small)


GQA_TQ = 512


def _gqa_kernel(q_ref, k_ref, v_ref, z_ref, cos_ref, sin_ref, qg_ref, kg_ref, y_ref,
                q_scr, k_scr, v_scr, s0_scr, s1_scr, p0_scr, p1_scr, l0_scr, l1_scr):
    def norm_rope(x, g, cos, sin):
        ms = jnp.mean(x * x, axis=-1, keepdims=True)
        xn = x * lax.rsqrt(ms + EPS) * g
        lane = lax.broadcasted_iota(jnp.int32, xn.shape, 1)
        partner = jnp.where((lane & 63) < 32, pltpu.roll(xn, LANES - 32, 1), pltpu.roll(xn, 32, 1))
        return xn * cos + partner * sin

    s_len = k_ref.shape[0]
    group = GA_HEADS // GA_KV_HEADS
    assert group == 2
    scale = GA_DH ** -0.5
    n_blk = s_len // GQA_TQ
    s_bufs, p_bufs, l_bufs = (s0_scr, s1_scr), (p0_scr, p1_scr), (l0_scr, l1_scr)
    for kv in range(k_ref.shape[1] // GA_DH):
        _gqa_pipeline(kv, group, s_len, n_blk, scale, norm_rope, q_ref, k_ref, v_ref, z_ref, cos_ref, sin_ref,
                      qg_ref, kg_ref, y_ref, q_scr, k_scr, v_scr, s_bufs, p_bufs, l_bufs)


def _gqa_pipeline(kv, group, s_len, n_blk, scale, norm_rope, q_ref, k_ref, v_ref, z_ref, cos_ref, sin_ref,
                  qg_ref, kg_ref, y_ref, q_scr, k_scr, v_scr, s_bufs, p_bufs, l_bufs):
    kcols = slice(kv * GA_DH, (kv + 1) * GA_DH)
    k_scr[kv] = norm_rope(k_ref[:, kcols].astype(F32), kg_ref[...], cos_ref[...], sin_ref[...]).astype(BF16)
    v_scr[kv] = v_ref[:, kcols].astype(BF16)

    def stacked(head, blk):
        return pl.ds(pl.multiple_of(head * s_len + blk * GQA_TQ, GQA_TQ), GQA_TQ)

    def head_cols(head):
        return slice((kv * group + head) * GA_DH, (kv * group + head + 1) * GA_DH)

    def prep(head, blk):
        rows = pl.ds(pl.multiple_of(blk * GQA_TQ, GQA_TQ), GQA_TQ)
        q = norm_rope(q_ref[rows, head_cols(head)].astype(F32), qg_ref[...], cos_ref[rows, :], sin_ref[rows, :])
        q_scr[stacked(head, blk), :] = q.astype(BF16)

    def logits(head, blk):
        s_bufs[head][...] = _mm_nt(q_scr[stacked(head, blk), :], k_scr[kv]) * scale

    def softmax(slot):
        s = s_bufs[slot][...]
        p = jnp.exp(s - jnp.max(s, axis=-1, keepdims=True))
        l_bufs[slot][...] = jnp.broadcast_to(jnp.sum(p, axis=-1, keepdims=True), (GQA_TQ, GA_DH))
        p_bufs[slot][...] = p.astype(BF16)

    def weighted(head, blk):
        rows = pl.ds(pl.multiple_of(blk * GQA_TQ, GQA_TQ), GQA_TQ)
        cols = head_cols(head)
        o = _mm(p_bufs[head][...], v_scr[kv]) / l_bufs[head][...]
        y_ref[rows, cols] = (o * _silu(z_ref[rows, cols].astype(F32))).astype(y_ref.dtype)

    prep(0, 0)
    prep(1, 0)
    prep(0, 1)
    logits(0, 0)
    logits(1, 0)
    softmax(0)

    def body(j, carry):
        logits(0, j + 1)
        softmax(1)
        weighted(0, j)
        prep(1, j + 1)
        logits(1, j + 1)
        softmax(0)
        weighted(1, j)
        prep(0, jnp.minimum(j + 2, n_blk - 1))
        return carry

    lax.fori_loop(0, n_blk - 1, body, 0)
    softmax(1)
    weighted(0, n_blk - 1)
    weighted(1, n_blk - 1)


def _gqa(pm3, cos_t, sin_t, qg, kg):
    b, s, _ = pm3.shape
    qw = GA_HEADS * GA_DH
    kw = GA_KV_HEADS * GA_DH
    group = GA_HEADS // GA_KV_HEADS
    return pl.pallas_call(
        _gqa_kernel,
        grid=(b,),
        in_specs=[
            pl.BlockSpec((None, s, qw), lambda i: (i, 0, _OFF["c_q"] // qw)),
            pl.BlockSpec((None, s, kw), lambda i: (i, 0, _OFF["c_k"] // kw)),
            pl.BlockSpec((None, s, kw), lambda i: (i, 0, _OFF["c_v"] // kw)),
            pl.BlockSpec((None, s, qw), lambda i: (i, 0, _OFF["c_z"] // qw)),
            pl.BlockSpec((s, GA_DH), lambda i: (0, 0)),
            pl.BlockSpec((s, GA_DH), lambda i: (0, 0)),
            pl.BlockSpec((1, GA_DH), lambda i: (0, 0)),
            pl.BlockSpec((1, GA_DH), lambda i: (0, 0)),
        ],
        out_specs=pl.BlockSpec((None, s, qw), lambda i: (i, 0, 0)),
        out_shape=jax.ShapeDtypeStruct((b, s, BRANCH_W), BF16),
        scratch_shapes=[pltpu.VMEM((group * s, GA_DH), BF16), pltpu.VMEM((GA_KV_HEADS, s, GA_DH), BF16),
                        pltpu.VMEM((GA_KV_HEADS, s, GA_DH), BF16),
                        pltpu.VMEM((GQA_TQ, s), F32), pltpu.VMEM((GQA_TQ, s), F32),
                        pltpu.VMEM((GQA_TQ, s), BF16), pltpu.VMEM((GQA_TQ, s), BF16),
                        pltpu.VMEM((GQA_TQ, GA_DH), F32), pltpu.VMEM((GQA_TQ, GA_DH), F32)],
        compiler_params=_cparams(("parallel",)),
        name="gqa",
    )(pm3, pm3, pm3, pm3, cos_t, sin_t, qg, kg)


NA_ROW_UNROLL = 16


def _na_kernel(q_ref, k_ref, v_ref, z_ref, bias_ref, qg_ref, kg_ref, y_ref, q_scr, k_scr):
    s_len = q_ref.shape[0]
    rows = s_len // GRID_W
    kr = min(NA_ROWS, rows)
    hi = lax.broadcasted_iota(jnp.int32, (2 * NA_DH, 2 * NA_DH), 0) >= NA_DH
    hj = lax.broadcasted_iota(jnp.int32, (2 * NA_DH, 2 * NA_DH), 1) >= NA_DH
    same_head = jnp.where(hi == hj, 1.0, 0.0).astype(BF16)

    def rms_pair(x, g):
        x2 = x * x
        x2_hi = x2.astype(BF16)
        x2_lo = (x2 - x2_hi.astype(F32)).astype(BF16)
        ssq = (jnp.dot(x2_hi, same_head, preferred_element_type=F32)
               + jnp.dot(x2_lo, same_head, preferred_element_type=F32))
        return x * lax.rsqrt(ssq * (1.0 / NA_DH) + EPS) * g

    scale = NA_DH ** -0.5
    assert math.log2(scale).is_integer()
    nkeys = kr * GRID_W
    first = lax.broadcasted_iota(jnp.int32, (s_len, 2 * NA_DH), 1) < NA_DH
    first_q = lax.broadcasted_iota(jnp.int32, (GRID_W, 2 * NA_DH), 1) < NA_DH
    for pair in range(q_ref.shape[1] // (2 * NA_DH)):
        pc = slice(pair * 2 * NA_DH, (pair + 1) * 2 * NA_DH)
        qn = rms_pair(q_ref[:, pc].astype(F32), qg_ref[...]) * scale
        q_scr[0] = jnp.where(first, qn, 0.0).astype(BF16)
        q_scr[1] = jnp.where(first, 0.0, qn).astype(BF16)
        k_scr[...] = rms_pair(k_ref[:, pc].astype(F32), kg_ref[...]).astype(BF16)

        def body(it, carry, pair=pair, pc=pc):
            units = []
            for u in range(NA_ROW_UNROLL):
                r = it * NA_ROW_UNROLL + u
                r0 = jnp.clip(r - kr // 2, 0, rows - kr)
                var = r0 - r + (NA_ROWS - 1)
                qrows = pl.ds(pl.multiple_of(r * GRID_W, GRID_W), GRID_W)
                krows = pl.ds(pl.multiple_of(r0 * GRID_W, GRID_W), nkeys)
                units.append((var, qrows, krows))
            logits = [_mm_nt(jnp.concatenate([q_scr[0, qrows, :], q_scr[1, qrows, :]], axis=0), k_scr[krows, :])
                      + bias_ref[pair, var] for var, qrows, krows in units]
            probs = [jnp.exp(s - jnp.max(s, axis=-1, keepdims=True)) for s in logits]
            sums = [jnp.sum(p, axis=-1, keepdims=True) for p in probs]
            outs = [_mm(p, v_ref[krows, pc]) / l for p, l, (_, _, krows) in zip(probs, sums, units)]
            for o, (_, qrows, _) in zip(outs, units):
                o_pair = jnp.where(first_q, o[:GRID_W], o[GRID_W:])
                y_ref[qrows, pc] = (o_pair * _silu(z_ref[qrows, pc].astype(F32))).astype(y_ref.dtype)
            return carry

        lax.fori_loop(0, rows // NA_ROW_UNROLL, body, 0)


def _na_bias_table(rpb, rows):
    kr = min(NA_ROWS, rows)
    c = jnp.arange(GRID_W)
    c0 = jnp.clip(c - NA_COLS // 2, 0, GRID_W - NA_COLS)
    in_win = (c[None, :] >= c0[:, None]) & (c[None, :] < c0[:, None] + NA_COLS)
    col_off = jnp.clip(c[None, :] - c[:, None], -(NA_COLS - 1), NA_COLS - 1) + NA_COLS - 1
    t = jnp.where(in_win, rpb[..., col_off], NEG)
    ro = jnp.arange(NA_ROWS)[:, None] + jnp.arange(kr)[None, :]
    tv = jnp.take(t, ro, axis=-3)
    tv = jnp.swapaxes(tv, -3, -2).reshape(*rpb.shape[:-2], NA_ROWS, GRID_W, kr * GRID_W)
    lead = rpb.shape[:-3]
    tv = tv.reshape(*lead, rpb.shape[-3] // 2, 2, NA_ROWS, GRID_W, kr * GRID_W)
    tv = jnp.swapaxes(tv, -4, -3)
    return tv.reshape(*lead, rpb.shape[-3] // 2, NA_ROWS, 2 * GRID_W, kr * GRID_W).astype(F32)


def _natten(pm3, bias, qg, kg, layer):
    b, s, _ = pm3.shape
    pw = 2 * NA_DH
    hw = NA_HEADS * NA_DH
    return pl.pallas_call(
        _na_kernel,
        grid=(b,),
        in_specs=[
            pl.BlockSpec((None, s, hw), lambda i: (i, 0, _OFF["b_qkv"] // hw)),
            pl.BlockSpec((None, s, hw), lambda i: (i, 0, _OFF["b_qkv"] // hw + 1)),
            pl.BlockSpec((None, s, hw), lambda i: (i, 0, _OFF["b_qkv"] // hw + 2)),
            pl.BlockSpec((None, s, hw), lambda i: (i, 0, _OFF["b_z"] // hw)),
            pl.BlockSpec((None,) + bias.shape[1:], lambda i: (layer, 0, 0, 0, 0)),
            pl.BlockSpec((1, pw), lambda i: (0, 0)),
            pl.BlockSpec((1, pw), lambda i: (0, 0)),
        ],
        out_specs=pl.BlockSpec((None, s, hw), lambda i: (i, 0, 0)),
        out_shape=jax.ShapeDtypeStruct((b, s, BRANCH_W), BF16),
        scratch_shapes=[pltpu.VMEM((2, s, pw), BF16), pltpu.VMEM((s, pw), BF16)],
        compiler_params=_cparams(("parallel",)),
        name="natten",
    )(pm3, pm3, pm3, pm3, bias, qg, kg)


INV_BASE = 8
PRE_BLK = 4 * BLK


def _inverse_level_masks():
    i = lax.broadcasted_iota(jnp.int32, (BLK, BLK), 0)
    j = lax.broadcasted_iota(jnp.int32, (BLK, BLK), 1)
    same = lambda size: (i >> int(math.log2(size))) == (j >> int(math.log2(size)))
    base = same(INV_BASE)
    joins = []
    size = INV_BASE
    while size < DN_CHUNK:
        joins.append(same(2 * size) & jnp.logical_not(same(size)))
        size *= 2
    return base, joins


def _tri_inverses(l_mats, eyes, level_masks):
    base, joins = level_masks
    ps = [jnp.where(base, -l, 0.0) for l in l_mats]
    ts = [eye + p for eye, p in zip(eyes, ps)]
    for _ in range(int(math.log2(INV_BASE)) - 1):
        ps = [_mm(p, p) for p in ps]
        ts = [t + _mm(t, p) for t, p in zip(ts, ps)]
    for join in joins:
        mids = [_mm(jnp.where(join, l, 0.0), t) for l, t in zip(l_mats, ts)]
        ts = [t - _mm(t, mid) for t, mid in zip(ts, mids)]
    return ts


DN_HW = DN_HEADS * LANES
PK_W, PK_QD, PK_KD, PK_QK = (i * DN_HW for i in range(4))


def _dn_pre_kernel(qkv_ref, sm_ref, cw_ref, lp_ref, uf_ref, ub_ref, pkf_ref, pkb_ref, gtf_ref, gtb_ref):
    n = pl.program_id(1)
    s_len = qkv_ref.shape[0]
    assert qkv_ref.dtype == BF16
    halo = 2 * SUBLANES
    n_sub = sm_ref.shape[0] // BLK
    chunks_per_blk = BLK // DN_CHUNK
    out_row = lax.broadcasted_iota(jnp.int32, (BLK, BLK + 2 * halo), 0)
    in_row = lax.broadcasted_iota(jnp.int32, (BLK, BLK + 2 * halo), 1)
    shifts = {j: jnp.where(in_row == out_row + (halo + j - DN_CONV // 2), 1.0, 0.0).astype(BF16)
              for j in range(DN_CONV) if j != DN_CONV // 2}
    dir_masks = [_chunk_masks(d == 1, DN_CHUNK) for d in range(2)]
    level_masks = _inverse_level_masks()
    outs = ((uf_ref, pkf_ref, gtf_ref), (ub_ref, pkb_ref, gtb_ref))

    def l2n(x):
        return x * lax.rsqrt(jnp.sum(x * x, axis=-1, keepdims=True) + EPS)

    blocks = []
    for sub in range(n_sub):
        t0 = pl.multiple_of((n * n_sub + sub) * BLK, BLK)
        pstart = pl.multiple_of(jnp.maximum(t0 - halo, 0), halo)
        nstart = pl.multiple_of(jnp.minimum(t0 + BLK, s_len - halo), halo)
        prev = qkv_ref[pl.ds(pstart, halo), :]
        prev = jnp.where(t0 > 0, prev, jnp.zeros_like(prev))
        cur = qkv_ref[pl.ds(t0, BLK), :]
        nxt = qkv_ref[pl.ds(nstart, halo), :]
        nxt = jnp.where(t0 + BLK < s_len, nxt, jnp.zeros_like(nxt))
        xw = jnp.concatenate([prev, cur, nxt], axis=0)
        conv = cur.astype(F32) * cw_ref[DN_CONV // 2:DN_CONV // 2 + 1, :]
        for j, shift in shifts.items():
            conv = conv + jnp.dot(shift, xw, preferred_element_type=F32) * cw_ref[j:j + 1, :]
        conv = _silu(conv)

        rows = slice(sub * BLK, (sub + 1) * BLK)
        sm = sm_ref[rows, :]
        g_all = -jnp.exp(lp_ref[0:1, :]) * _softplus(sm + lp_ref[1:2, :])
        beta_all = _sigmoid(sm)
        per_dir = []
        for d in range(2):
            same, incl, strict, eye_b = dir_masks[d]
            gc = _mask_sum(incl, g_all)
            tot = _mask_sum(same, g_all)
            per_dir.append((incl, strict, _as_f32(eye_b), gc, gc.T, tot))
        blocks.append((sub, rows, conv, beta_all, per_dir))

    heads = []
    for sub, rows, conv, beta_all, per_dir in blocks:
        for h in range(DN_HEADS):
            q = l2n(conv[:, h * DN_DK:(h + 1) * DN_DK]) * (DN_DK ** -0.5)
            k = l2n(conv[:, (DN_HEADS + h) * DN_DK:(DN_HEADS + h + 1) * DN_DK])
            v = conv[:, 2 * DN_HEADS * DN_DK + h * DN_DV:2 * DN_HEADS * DN_DK + (h + 1) * DN_DV]
            heads.append((sub, rows, beta_all, per_dir, h, q, k, v))
    kks = [_mm_nt(hd[6], hd[6]) for hd in heads]
    qks = [_mm_nt(hd[5], hd[6]) for hd in heads]

    l_mats, rhss, eyes, slots = [], [], [], []
    for (sub, rows, beta_all, per_dir, h, q, k, v), kk, qk in zip(heads, kks, qks):
        hc = slice(h * LANES, (h + 1) * LANES)
        for d in range(2):
            incl, strict, eye_f, gc, gct, tot = per_dir[d]
            u_ref, pk_ref, gt_ref = outs[d]
            pk = lambda off: slice(off + h * LANES, off + (h + 1) * LANES)
            c = _L_AA + d * DN_HEADS + h
            gcol = gc[:, c:c + 1]
            grow = gct[c:c + 1, :]
            tcol = tot[:, c:c + 1]
            beta = beta_all[:, _L_AB + d * DN_HEADS + h:_L_AB + d * DN_HEADS + h + 1]
            decay = jnp.where(incl, jnp.exp(jnp.where(incl, gcol - grow, 0.0)), 0.0)
            l_mats.append(jnp.where(strict, beta * kk * decay, 0.0))
            egc = jnp.exp(gcol)
            rhss.append(jnp.concatenate([v * beta, k * (beta * egc)], axis=-1).astype(BF16))
            eyes.append(eye_f)
            slots.append((rows, h, d))
            pk_ref[rows, pk(PK_QD)] = (q * egc).astype(pk_ref.dtype)
            pk_ref[rows, pk(PK_KD)] = (k * jnp.exp(tcol - gcol)).astype(pk_ref.dtype)
            pk_ref[rows, pk(PK_QK)] = (qk * decay).astype(pk_ref.dtype)
            gtot = jnp.exp(tcol)
            for ci in range(chunks_per_blk):
                g0 = (sub * chunks_per_blk + ci) * SUBLANES
                gt_ref[g0:g0 + SUBLANES, hc] = jnp.broadcast_to(
                    gtot[ci * DN_CHUNK:ci * DN_CHUNK + SUBLANES, :], (SUBLANES, LANES))

    t_invs = _tri_inverses(l_mats, eyes, level_masks)
    sols = [_mm(t, rhs) for t, rhs in zip(t_invs, rhss)]
    for (rows, h, d), sol in zip(slots, sols):
        u_ref, pk_ref = outs[d][0], outs[d][1]
        u_ref[rows, h * LANES:(h + 1) * LANES] = sol[:, :DN_DV]
        pk_ref[rows, PK_W + h * LANES:PK_W + (h + 1) * LANES] = sol[:, DN_DV:].astype(pk_ref.dtype)


def _dn_pre(pm3, ps3, conv_w8, lane_params):
    b, s, _ = pm3.shape
    nblk = s // PRE_BLK
    gt_rows = PRE_BLK // DN_CHUNK * SUBLANES
    wq = DN_HEADS * (2 * DN_DK + DN_DV)
    hw = DN_HEADS * LANES
    tok = lambda i, j: (i, j, 0)
    big = lambda dt: jax.ShapeDtypeStruct((b, s, hw), dt)
    gts = jax.ShapeDtypeStruct((b, nblk * gt_rows, hw), F32)
    bs_tok = pl.BlockSpec((None, PRE_BLK, hw), tok)
    bs_gt = pl.BlockSpec((None, gt_rows, hw), tok)
    bs_pk = pl.BlockSpec((None, PRE_BLK, 4 * hw), tok)
    packed = jax.ShapeDtypeStruct((b, s, 4 * hw), BF16)
    return pl.pallas_call(
        _dn_pre_kernel,
        grid=(b, nblk),
        in_specs=[
            pl.BlockSpec((None, s, wq), lambda i, j: (i, 0, _OFF["a_qkv"] // wq)),
            pl.BlockSpec((None, PRE_BLK, LANES), tok),
            pl.BlockSpec((SUBLANES, wq), lambda i, j: (0, 0)),
            pl.BlockSpec((SUBLANES, LANES), lambda i, j: (0, 0)),
        ],
        out_specs=[bs_tok, bs_tok, bs_pk, bs_pk, bs_gt, bs_gt],
        out_shape=[big(F32), big(F32), packed, packed, gts, gts],
        compiler_params=_cparams(("parallel", "arbitrary")),
        name="dn_pre",
    )(pm3, ps3, conv_w8, lane_params)


SCAN_BLK = 4 * BLK


def _dn_scan_kernel(uf_ref, pkf_ref, gtf_ref, ub_ref, pkb_ref, gtb_ref, of_ref, ob_ref, st_scr):
    @pl.when(pl.program_id(1) == 0)
    def _():
        st_scr[...] = jnp.zeros_like(st_scr)

    per_blk = BLK // DN_CHUNK
    nchunk = SCAN_BLK // DN_CHUNK
    zeros_c = jnp.zeros((DN_CHUNK, DN_DV), F32)
    streams = ((uf_ref, pkf_ref, gtf_ref, of_ref, range(nchunk)),
               (ub_ref, pkb_ref, gtb_ref, ob_ref, range(nchunk - 1, -1, -1)))
    chains = [(d, h) + streams[d] for d in range(2) for h in range(DN_HEADS)]
    states = [st_scr[d * DN_HEADS + h] for d, h, *_ in chains]
    for step in range(nchunk):
        rs, v_pads = [], []
        for (d, h, u_ref, pk_ref, gt_ref, o_ref, order), state in zip(chains, states):
            rows = slice(order[step] * DN_CHUNK, (order[step] + 1) * DN_CHUNK)
            w = pk_ref[rows, PK_W + h * LANES:PK_W + (h + 1) * LANES]
            qd = pk_ref[rows, PK_QD + h * LANES:PK_QD + (h + 1) * LANES]
            rs.append(_mm(jnp.concatenate([w, qd], axis=0), state))
        for (d, h, u_ref, pk_ref, gt_ref, o_ref, order), r in zip(chains, rs):
            ci = order[step]
            rows = slice(ci * DN_CHUNK, (ci + 1) * DN_CHUNK)
            parts = [zeros_c] * per_blk
            parts[ci % per_blk] = u_ref[rows, h * LANES:(h + 1) * LANES] - r[:DN_CHUNK]
            v_pads.append(jnp.concatenate(parts, axis=0))
        new_states = []
        for (d, h, u_ref, pk_ref, gt_ref, o_ref, order), r, v_pad, state in zip(chains, rs, v_pads, states):
            hc = slice(h * LANES, (h + 1) * LANES)
            ci = order[step]
            rows = slice(ci * DN_CHUNK, (ci + 1) * DN_CHUNK)
            blk_rows = slice((ci // per_blk) * BLK, (ci // per_blk + 1) * BLK)
            o_ref[rows, hc] = r[DN_CHUNK:] + _mm(pk_ref[rows, PK_QK + h * LANES:PK_QK + (h + 1) * LANES], v_pad)
            gt = gt_ref[ci * SUBLANES:ci * SUBLANES + 1, hc]
            kd = pk_ref[blk_rows, PK_KD + h * LANES:PK_KD + (h + 1) * LANES]
            new_states.append(state * gt + _mm_tn(kd, v_pad))
        states = new_states
    for (d, h, *_), state in zip(chains, states):
        st_scr[d * DN_HEADS + h] = state


def _dn_scan(pre):
    uf, ub, pkf, pkb, gtf, gtb = pre
    b, s, hw = uf.shape
    nblk = s // SCAN_BLK
    gt_rows = SCAN_BLK // DN_CHUNK * SUBLANES
    fwd = lambda i, j: (i, j, 0)
    bwd = lambda i, j: (i, nblk - 1 - j, 0)
    def specs(imap):
        return [pl.BlockSpec((None, SCAN_BLK, hw), imap), pl.BlockSpec((None, SCAN_BLK, 4 * hw), imap),
                pl.BlockSpec((None, gt_rows, hw), imap)]
    return pl.pallas_call(
        _dn_scan_kernel,
        grid=(b, nblk),
        in_specs=specs(fwd) + specs(bwd),
        out_specs=[pl.BlockSpec((None, SCAN_BLK, hw), fwd), pl.BlockSpec((None, SCAN_BLK, hw), bwd)],
        out_shape=[jax.ShapeDtypeStruct((b, s, hw), F32)] * 2,
        scratch_shapes=[pltpu.VMEM((2 * DN_HEADS, DN_DK, DN_DV), F32)],
        compiler_params=_cparams(("parallel", "arbitrary")),
        name="dn_scan",
    )(uf, pkf, gtf, ub, pkb, gtb)


ML_AUG = 2 * LANES
ML_GROUP = 4
ML_BLK = 4 * BLK


def _ml_kernel(qf_ref, kf_ref, vf_ref, smf_ref, qb_ref, kb_ref, vb_ref, smb_ref, lp_ref,
               hf_ref, hb_ref, c_scr, m_scr):
    @pl.when(pl.program_id(1) == 0)
    def _():
        c_scr[...] = jnp.zeros_like(c_scr)
        m_scr[...] = jnp.zeros_like(m_scr)

    nchunk = BLK // ML_CHUNK
    ones_col = jnp.ones((BLK, LANES), BF16)
    zeros_aug = jnp.zeros((ML_CHUNK, ML_AUG), BF16)
    streams = ((qf_ref, kf_ref, vf_ref, smf_ref, hf_ref, range(nchunk)),
               (qb_ref, kb_ref, vb_ref, smb_ref, hb_ref, range(nchunk - 1, -1, -1)))
    lanes = lambda col: jnp.broadcast_to(col, (col.shape[0], LANES))
    n_sub = smf_ref.shape[0] // BLK
    units = [(d, p if d == 0 else n_sub - 1 - p) for p in range(n_sub) for d in range(2)]
    for d, sub in units:
        _ml_block(d, sub, streams[d], lp_ref, c_scr, m_scr, lanes, ones_col, zeros_aug, nchunk)


def _ml_block(d, sub, stream, lp_ref, c_scr, m_scr, lanes, ones_col, zeros_aug, nchunk):
    q_ref, k_ref, v_ref, sm_ref, h_ref, order = stream
    blk_rows = slice(sub * BLK, (sub + 1) * BLK)
    same, incl, _, _ = _chunk_masks(d == 1, ML_CHUNK)
    sm = sm_ref[blk_rows, :]
    ig_all = sm + lp_ref[0:1, :]
    x = sm + lp_ref[1:2, :]
    lf_all = jnp.minimum(x, 0.0) - jnp.log(1.0 + jnp.exp(-jnp.abs(x)))
    lf_all = pltpu.roll(lf_all, LANES - (_L_DF - _L_DI), 1)
    bc_all = _mask_sum(incl, lf_all)
    tot_all = _chunk_totals(bc_all, ML_CHUNK, reverse=(d == 1))
    a_all = ig_all - bc_all
    mwa_all = jnp.concatenate(
        [jnp.broadcast_to(jnp.max(a_all[ci * ML_CHUNK:(ci + 1) * ML_CHUNK], axis=0, keepdims=True),
                          (ML_CHUNK, LANES)) for ci in range(nchunk)], axis=0)
    a_t, w_all, mw_all = a_all.T, jnp.exp(a_all - mwa_all), tot_all + mwa_all

    for chains in [[(d, h) for h in range(g, g + ML_GROUP)] for g in range(0, ML_HEADS, ML_GROUP)]:
        ins = []
        for _, h in chains:
            q = q_ref[blk_rows, h * ML_DK:(h + 1) * ML_DK].astype(BF16)
            k = k_ref[blk_rows, h * ML_DK:(h + 1) * ML_DK].astype(F32) * (ML_DK ** -0.5)
            v_aug = jnp.concatenate([v_ref[blk_rows, h * ML_DV:(h + 1) * ML_DV].astype(BF16), ones_col], axis=-1)
            ins.append((q, k, v_aug))
        qks = [_mm_nt(q, k) for q, k, _ in ins]

        mids = []
        for (d, h), (q, k, v_aug), qk in zip(chains, ins, qks):
            c = _L_DI + d * ML_HEADS + h
            b_l = lanes(bc_all[:, c:c + 1])
            dlog = jnp.where(incl, b_l + a_t[c:c + 1, :], NEG)
            m_intra = lanes(jnp.max(dlog, axis=-1, keepdims=True))
            s_intra = qk * jnp.exp(dlog - m_intra)
            wk = (k * lanes(w_all[:, c:c + 1])[:, :ML_DK]).astype(BF16)
            mids.append((b_l, m_intra, s_intra.astype(BF16), wk))
        p_intras = [_mm(s_b, v_aug) for (_, _, s_b, _), (_, _, v_aug) in zip(mids, ins)]
        kvs = []
        for (_, _, _, wk), (_, _, v_aug) in zip(mids, ins):
            per_chunk = []
            for ci in range(nchunk):
                parts = [zeros_aug] * nchunk
                parts[ci] = v_aug[ci * ML_CHUNK:(ci + 1) * ML_CHUNK]
                per_chunk.append(_mm_tn(wk, jnp.concatenate(parts, axis=0)))
            kvs.append(per_chunk)

        c_sts = [c_scr[d * ML_HEADS + h] for d, h in chains]
        m_sts = [m_scr[d * ML_HEADS + h][0:1, :] for d, h in chains]
        for step in range(nchunk):
            qcs = []
            ci = order[step]
            for (q, _, _), c_st in zip(ins, c_sts):
                qcs.append(_mm(q[ci * ML_CHUNK:(ci + 1) * ML_CHUNK], c_st))
            for idx, (_, h) in enumerate(chains):
                b_l, m_intra, _, _ = mids[idx]
                c = _L_DI + d * ML_HEADS + h
                rows = slice(ci * ML_CHUNK, (ci + 1) * ML_CHUNK)
                out_rows = slice(sub * BLK + ci * ML_CHUNK, sub * BLK + (ci + 1) * ML_CHUNK)
                r8 = slice(ci * ML_CHUNK, ci * ML_CHUNK + SUBLANES)
                m_st, c_st, qc = m_sts[idx], c_sts[idx], qcs[idx]
                m_inter = b_l[rows] + m_st
                m_i = jnp.maximum(m_intra[rows], m_inter)
                f_i = jnp.exp(m_intra[rows] - m_i)
                inter = jnp.exp(m_inter - m_i)
                both = (jnp.concatenate([inter, inter], axis=-1) * qc
                        + jnp.concatenate([f_i, f_i], axis=-1) * p_intras[idx][rows])
                numer, denom = both[:, :ML_DV], both[:, ML_DV:]
                h_ref[out_rows, h * ML_DV:(h + 1) * ML_DV] = numer / jnp.maximum(jnp.abs(denom), jnp.exp(-m_i))
                tot_c = lanes(tot_all[r8, c:c + 1])[0:1]
                mw_c = lanes(mw_all[r8, c:c + 1])[0:1]
                m_new = jnp.maximum(tot_c + m_st, mw_c)
                dec = jnp.exp(tot_c + m_st - m_new)
                gain = jnp.exp(mw_c - m_new)
                c_sts[idx] = (jnp.concatenate([dec, dec], axis=-1) * c_st
                              + jnp.concatenate([gain, gain], axis=-1) * kvs[idx][ci])
                m_sts[idx] = m_new
        for idx, (d, h) in enumerate(chains):
            c_scr[d * ML_HEADS + h] = c_sts[idx]
            m_scr[d * ML_HEADS + h] = jnp.broadcast_to(m_sts[idx], (SUBLANES, LANES))


def _mlstm(pm3, ps3, lane_params):
    b, s, _ = pm3.shape
    nblk = s // ML_BLK
    qw = ML_HEADS * ML_DK
    vw = ML_HEADS * ML_DV
    def specs(tmap):
        blk = lambda j: tmap(j)
        return [
            pl.BlockSpec((None, ML_BLK, qw), lambda i, j: (i, blk(j), _OFF["d_q"] // qw)),
            pl.BlockSpec((None, ML_BLK, qw), lambda i, j: (i, blk(j), _OFF["d_k"] // qw)),
            pl.BlockSpec((None, ML_BLK, vw), lambda i, j: (i, blk(j), _OFF["d_v"] // vw)),
            pl.BlockSpec((None, ML_BLK, LANES), lambda i, j: (i, blk(j), 0)),
        ]
    fwd = lambda j: j
    bwd = lambda j: nblk - 1 - j
    return pl.pallas_call(
        _ml_kernel,
        grid=(b, nblk),
        in_specs=specs(fwd) + specs(bwd) + [pl.BlockSpec((SUBLANES, LANES), lambda i, j: (0, 0))],
        out_specs=[pl.BlockSpec((None, ML_BLK, vw), lambda i, j: (i, j, 0)),
                   pl.BlockSpec((None, ML_BLK, vw), lambda i, j: (i, nblk - 1 - j, 0))],
        out_shape=[jax.ShapeDtypeStruct((b, s, vw), F32)] * 2,
        scratch_shapes=[pltpu.VMEM((2 * ML_HEADS, ML_DK, ML_AUG), F32),
                        pltpu.VMEM((2 * ML_HEADS, SUBLANES, LANES), F32)],
        compiler_params=_cparams(("parallel", "arbitrary")),
        name="mlstm",
    )(pm3, pm3, pm3, ps3, pm3, pm3, pm3, ps3, lane_params)


def _merge_kernel(x_ref, af_ref, ab_ref, df_ref, db_ref, yb_ref, yc_ref, az_ref, dz_ref, do_ref, gl_ref,
                  ag_ref, dg_ref, wb_ref, wo_ref, o_ref):
    d = x_ref.shape[-1]

    def head_rms(x, g):
        outs = []
        for h in range(x.shape[-1] // LANES):
            xh = x[:, h * LANES:(h + 1) * LANES]
            ms = jnp.mean(xh * xh, axis=-1, keepdims=True)
            outs.append(xh * lax.rsqrt(ms + EPS) * g)
        return jnp.concatenate(outs, axis=-1)

    ya = head_rms(af_ref[...] + ab_ref[...], ag_ref[...]) * _silu(az_ref[...].astype(F32))
    yd = _sigmoid(do_ref[...].astype(F32)) * head_rms(df_ref[...] + db_ref[...], dg_ref[...])
    yd = yd * _silu(dz_ref[...].astype(F32))
    twice = None
    for i, y in enumerate((ya.astype(BF16), yb_ref[...], yc_ref[...], yd.astype(BF16))):
        proj = jnp.dot(y, wb_ref[i], preferred_element_type=F32)
        term = proj + jnp.tanh(0.5 * gl_ref[:, i * d:(i + 1) * d].astype(F32)) * proj
        twice = term if twice is None else twice + term
    merged = (0.5 * twice).astype(BF16)
    o_ref[...] = x_ref[...] + jnp.dot(merged, wo_ref[...], preferred_element_type=F32)


def _merge(x2d, af, ab, df, db, yb, yc, pm2, ag, dg, wb, wo, layer, tm=512):
    m, d = x2d.shape
    gw = N_BRANCH * d
    w = BRANCH_W
    tok = pl.BlockSpec((tm, w), lambda i: (i, 0))
    col = lambda name: pl.BlockSpec((tm, w), lambda i: (i, _OFF[name] // w))
    vec = pl.BlockSpec((1, LANES), lambda i: (0, 0))
    return pl.pallas_call(
        _merge_kernel,
        grid=(m // tm,),
        in_specs=[
            pl.BlockSpec((tm, d), lambda i: (i, 0)),
            tok, tok, tok, tok, tok, tok,
            col("a_z"), col("d_z"), col("d_o"),
            pl.BlockSpec((tm, gw), lambda i: (i, _OFF["gate"] // gw)),
            vec, vec,
            pl.BlockSpec((None, N_BRANCH, w, d), lambda i: (layer, 0, 0, 0)),
            pl.BlockSpec((None, d, d), lambda i: (layer, 0, 0)),
        ],
        out_specs=pl.BlockSpec((tm, d), lambda i: (i, 0)),
        out_shape=jax.ShapeDtypeStruct((m, d), F32),
        compiler_params=_cparams(("parallel",)),
        name="merge",
    )(x2d, af, ab, df, db, yb, yc, pm2, pm2, pm2, pm2, ag, dg, wb, wo)


def _rope_lane_tables(s):
    t = jnp.arange(s)
    row = (t // GRID_W).astype(F32)
    col = (t % GRID_W).astype(F32)
    m = GA_DH // 4
    inv = ROPE_THETA ** (-jnp.arange(m, dtype=F32) / m)
    ar = row[:, None] * inv
    ac = col[:, None] * inv
    cos_t = jnp.concatenate([jnp.cos(ar), jnp.cos(ar), jnp.cos(ac), jnp.cos(ac)], axis=-1)
    sin_t = jnp.concatenate([-jnp.sin(ar), jnp.sin(ar), -jnp.sin(ac), jnp.sin(ac)], axis=-1)
    return cos_t.astype(F32), sin_t.astype(F32)


def _main_columns(w):
    return jnp.concatenate([w[..., o:o + wd] for _, o, wd in _MAIN_SEGS], axis=-1)


def _lane_tiles(rows):
    padded = []
    for off, vals in rows:
        vals = vals.reshape(vals.shape[0], 1, -1).astype(F32)
        padded.append(jnp.pad(vals, ((0, 0), (0, 0), (off, LANES - off - vals.shape[-1]))))
    tiles = jnp.concatenate(padded, axis=1)
    return jnp.pad(tiles, ((0, 0), (0, SUBLANES - len(rows)), (0, 0)))


def kernel(x, norm_g, w_in, conv_a, dn_a_log, dn_dt_bias, dn_norm_g, na_q_norm, na_k_norm, na_rpb,
           ga_q_norm, ga_k_norm, ml_i_bias, ml_f_bias, ml_norm_g, w_branch, w_out):
    b, s, d = x.shape
    depth = w_in.shape[0]
    hw = BRANCH_W
    cos_t, sin_t = _rope_lane_tables(s)
    w_main = _main_columns(w_in).astype(BF16)
    w_small = jnp.pad(jnp.concatenate([w_in[:, :, o:o + 8] for o in _SMALL_SRC], axis=2),
                      ((0, 0), (0, 0), (0, LANES - 8 * len(_SMALL_SRC)))).astype(BF16)
    conv8 = jnp.pad(conv_a.astype(F32), ((0, 0), (0, SUBLANES - DN_CONV), (0, 0)))
    dn_lp = _lane_tiles([(_L_AA, dn_a_log), (_L_AA, dn_dt_bias)])
    ml_lp = _lane_tiles([(_L_DI, ml_i_bias), (_L_DF, ml_f_bias)])
    na_bias = _na_bias_table(na_rpb, s // GRID_W)
    na_qg = jnp.tile(na_q_norm, (1, 2)).reshape(depth, 1, 2 * NA_DH)
    na_kg = jnp.tile(na_k_norm, (1, 2)).reshape(depth, 1, 2 * NA_DH)
    ga_qg = ga_q_norm.reshape(depth, 1, GA_DH)
    ga_kg = ga_k_norm.reshape(depth, 1, GA_DH)
    wb_bf, wo_bf = w_branch.astype(BF16), w_out.astype(BF16)

    x2 = x.reshape(b * s, d)
    for l in range(depth):
        pm2, ps2 = _inproj(x2, norm_g[l].reshape(1, d), w_main, w_small, l)
        pm3 = pm2.reshape(b, s, N_MAIN)
        ps3 = ps2.reshape(b, s, LANES)
        o_af, o_ab = _dn_scan(_dn_pre(pm3, ps3, conv8[l], dn_lp[l]))
        h_df, h_db = _mlstm(pm3, ps3, ml_lp[l])
        yb = _natten(pm3, na_bias, na_qg[l], na_kg[l], l)
        yc = _gqa(pm3, cos_t, sin_t, ga_qg[l], ga_kg[l])
        x2 = _merge(x2, o_af.reshape(b * s, hw), o_ab.reshape(b * s, hw), h_df.reshape(b * s, hw),
                    h_db.reshape(b * s, hw), yb.reshape(b * s, hw), yc.reshape(b * s, hw), pm2,
                    dn_norm_g[l].reshape(1, LANES), ml_norm_g[l].reshape(1, LANES), wb_bf, wo_bf, l)
    return x2.reshape(b, s, d)
```

```python
import functools
import math

import jax
import jax.numpy as jnp
from jax import lax
from jax.experimental import pallas as pl
from jax.experimental.pallas import tpu as pltpu

F32 = jnp.float32
BF16 = jnp.bfloat16

D_MODEL = 1024
GRID_W = 64
N_BRANCH = 4
BRANCH_W = 512
EPS = 1e-6
DN_HEADS, DN_DK, DN_DV, DN_CONV, DN_CHUNK = 4, 128, 128, 5, 64
NA_HEADS, NA_DH, NA_ROWS, NA_COLS = 8, 64, 8, 16
GA_HEADS, GA_KV_HEADS, GA_DH = 4, 2, 128
ROPE_THETA = 10000.0
ML_HEADS, ML_DK, ML_DV, ML_CHUNK = 4, 64, 128, 128

LANES = 128
SUBLANES = 8
VMEM_LIMIT_BYTES = 56 * 1024 * 1024

_O_A_QKV, _O_A_A, _O_A_B, _O_A_Z = 0, 1536, 1544, 1552
_O_B_QKV, _O_B_Z = 2064, 3600
_O_C_Q, _O_C_K, _O_C_V, _O_C_Z = 4112, 4624, 4880, 5136
_O_D_Q, _O_D_K, _O_D_V, _O_D_I, _O_D_F, _O_D_O, _O_D_Z = 5648, 5904, 6160, 6672, 6680, 6688, 7200
_O_GATE = 7712
_MAIN_SEGS = (
    ("a_qkv", _O_A_QKV, 1536), ("b_qkv", _O_B_QKV, 1536), ("a_z", _O_A_Z, 512), ("b_z", _O_B_Z, 512),
    ("gate", _O_GATE, 4096), ("c_q", _O_C_Q, 512), ("c_k", _O_C_K, 256), ("c_v", _O_C_V, 256),
    ("c_z", _O_C_Z, 512), ("d_q", _O_D_Q, 256), ("d_k", _O_D_K, 256), ("d_v", _O_D_V, 512),
    ("d_o", _O_D_O, 512), ("d_z", _O_D_Z, 512),
)
_OFF = {}
_o = 0
for _name, _src, _w in _MAIN_SEGS:
    _OFF[_name] = _o
    _o += _w
N_MAIN = _o
_SMALL_SRC = (_O_A_A, _O_A_B, _O_D_I, _O_D_F)
_L_AA, _L_AB, _L_DI, _L_DF = 0, 8, 16, 24

P_DTYPE = BF16
BLK = 128
NEG = -1e30


def _cparams(sem):
    return pltpu.CompilerParams(dimension_semantics=sem, vmem_limit_bytes=VMEM_LIMIT_BYTES)


def _sigmoid(x):
    return 0.5 * jnp.tanh(0.5 * x) + 0.5


def _silu(x):
    return x * _sigmoid(x)


def _softplus(x):
    return jnp.maximum(x, 0.0) + jnp.log(1.0 + jnp.exp(-jnp.abs(x)))


def _mm(a, b):
    return jnp.dot(a.astype(BF16), b.astype(BF16), preferred_element_type=F32)


def _mm_nt(a, b):
    return lax.dot_general(a.astype(BF16), b.astype(BF16), (((1,), (1,)), ((), ())),
                           preferred_element_type=F32)


def _mm_tn(a, b):
    return lax.dot_general(a.astype(BF16), b.astype(BF16), (((0,), (0,)), ((), ())),
                           preferred_element_type=F32)


def _mask_sum(mask, x):
    m = jnp.where(mask, 1.0, 0.0).astype(BF16)
    x1 = x.astype(BF16)
    r1 = x - x1.astype(F32)
    x2 = r1.astype(BF16)
    x3 = (r1 - x2.astype(F32)).astype(BF16)
    dot = lambda v: jnp.dot(m, v, preferred_element_type=F32)
    return dot(x1) + (dot(x2) + dot(x3))


def _chunk_masks(reverse, chunk):
    i = lax.broadcasted_iota(jnp.int32, (BLK, BLK), 0)
    j = lax.broadcasted_iota(jnp.int32, (BLK, BLK), 1)
    shift = int(math.log2(chunk))
    same = (i >> shift) == (j >> shift)
    if reverse:
        incl = same & (j >= i)
        strict = same & (j > i)
    else:
        incl = same & (j <= i)
        strict = same & (j < i)
    return same, incl, strict, (i == j)


def _chunk_totals(cum, chunk, reverse):
    pieces = []
    for c0 in range(0, BLK, chunk):
        r = c0 if reverse else c0 + chunk - 1
        pieces.append(jnp.broadcast_to(cum[r:r + 1, :], (chunk, cum.shape[1])))
    return jnp.concatenate(pieces, axis=0)


def _as_f32(mask):
    return jnp.where(mask, 1.0, 0.0).astype(F32)


def _inproj_kernel(x_ref, g_ref, w_ref, ws_ref, pm_ref, ps_ref, h_scr):
    @pl.when(pl.program_id(1) == 0)
    def _():
        x = x_ref[...]
        ms = jnp.mean(x * x, axis=-1, keepdims=True)
        h = (x * lax.rsqrt(ms + EPS) * g_ref[...]).astype(BF16)
        h_scr[...] = h
        ps_ref[...] = jnp.dot(h, ws_ref[...], preferred_element_type=F32)

    pm_ref[...] = jnp.dot(h_scr[...], w_ref[...], preferred_element_type=F32).astype(pm_ref.dtype)


def _inproj(x2d, g, w_main, w_small, layer, tm=1024, tn=N_MAIN // 4):
    m, d = x2d.shape
    tm = min(tm, m)
    return pl.pallas_call(
        _inproj_kernel,
        grid=(m // tm, N_MAIN // tn),
        in_specs=[
            pl.BlockSpec((tm, d), lambda i, j: (i, 0)),
            pl.BlockSpec((1, d), lambda i, j: (0, 0)),
            pl.BlockSpec((None, d, tn), lambda i, j: (layer, 0, j)),
            pl.BlockSpec((None, d, LANES), lambda i, j: (layer, 0, 0)),
        ],
        out_specs=[
            pl.BlockSpec((tm, tn), lambda i, j: (i, j)),
            pl.BlockSpec((tm, LANES), lambda i, j: (i, 0)),
        ],
        out_shape=[jax.ShapeDtypeStruct((m, N_MAIN), P_DTYPE), jax.ShapeDtypeStruct((m, LANES), F32)],
        scratch_shapes=[pltpu.VMEM((tm, d), BF16)],
        compiler_params=_cparams(("parallel", "arbitrary")),
        name="inproj",
    )(x2d, g, w_main, w_small)


GQA_TQ = 512


def _gqa_kernel(q_ref, k_ref, v_ref, z_ref, cos_ref, sin_ref, qg_ref, kg_ref, y_ref,
                q_scr, k_scr, v_scr, s0_scr, s1_scr, p0_scr, p1_scr, l0_scr, l1_scr):
    def norm_rope(x, g, cos, sin):
        ms = jnp.mean(x * x, axis=-1, keepdims=True)
        xn = x * lax.rsqrt(ms + EPS) * g
        lane = lax.broadcasted_iota(jnp.int32, xn.shape, 1)
        partner = jnp.where((lane & 63) < 32, pltpu.roll(xn, LANES - 32, 1), pltpu.roll(xn, 32, 1))
        return xn * cos + partner * sin

    s_len = k_ref.shape[0]
    group = GA_HEADS // GA_KV_HEADS
    assert group == 2
    scale = GA_DH ** -0.5
    n_blk = s_len // GQA_TQ
    s_bufs, p_bufs, l_bufs = (s0_scr, s1_scr), (p0_scr, p1_scr), (l0_scr, l1_scr)
    for kv in range(k_ref.shape[1] // GA_DH):
        _gqa_pipeline(kv, group, s_len, n_blk, scale, norm_rope, q_ref, k_ref, v_ref, z_ref, cos_ref, sin_ref,
                      qg_ref, kg_ref, y_ref, q_scr, k_scr, v_scr, s_bufs, p_bufs, l_bufs)


def _gqa_pipeline(kv, group, s_len, n_blk, scale, norm_rope, q_ref, k_ref, v_ref, z_ref, cos_ref, sin_ref,
                  qg_ref, kg_ref, y_ref, q_scr, k_scr, v_scr, s_bufs, p_bufs, l_bufs):
    kcols = slice(kv * GA_DH, (kv + 1) * GA_DH)
    k_scr[kv] = norm_rope(k_ref[:, kcols].astype(F32), kg_ref[...], cos_ref[...], sin_ref[...]).astype(BF16)
    v_scr[kv] = v_ref[:, kcols].astype(BF16)

    def stacked(head, blk):
        return pl.ds(pl.multiple_of(head * s_len + blk * GQA_TQ, GQA_TQ), GQA_TQ)

    def head_cols(head):
        return slice((kv * group + head) * GA_DH, (kv * group + head + 1) * GA_DH)

    def prep(head, blk):
        rows = pl.ds(pl.multiple_of(blk * GQA_TQ, GQA_TQ), GQA_TQ)
        q = norm_rope(q_ref[rows, head_cols(head)].astype(F32), qg_ref[...], cos_ref[rows, :], sin_ref[rows, :])
        q_scr[stacked(head, blk), :] = q.astype(BF16)

    def logits(head, blk):
        s_bufs[head][...] = _mm_nt(q_scr[stacked(head, blk), :], k_scr[kv]) * scale

    def softmax(slot):
        s = s_bufs[slot][...]
        p = jnp.exp(s - jnp.max(s, axis=-1, keepdims=True))
        l_bufs[slot][...] = jnp.broadcast_to(jnp.sum(p, axis=-1, keepdims=True), (GQA_TQ, GA_DH))
        p_bufs[slot][...] = p.astype(BF16)

    def weighted(head, blk):
        rows = pl.ds(pl.multiple_of(blk * GQA_TQ, GQA_TQ), GQA_TQ)
        cols = head_cols(head)
        o = _mm(p_bufs[head][...], v_scr[kv]) / l_bufs[head][...]
        y_ref[rows, cols] = (o * _silu(z_ref[rows, cols].astype(F32))).astype(y_ref.dtype)

    prep(0, 0)
    prep(1, 0)
    prep(0, 1)
    logits(0, 0)
    logits(1, 0)
    softmax(0)

    def body(j, carry):
        logits(0, j + 1)
        softmax(1)
        weighted(0, j)
        prep(1, j + 1)
        logits(1, j + 1)
        softmax(0)
        weighted(1, j)
        prep(0, jnp.minimum(j + 2, n_blk - 1))
        return carry

    lax.fori_loop(0, n_blk - 1, body, 0)
    softmax(1)
    weighted(0, n_blk - 1)
    weighted(1, n_blk - 1)


def _gqa(pm3, cos_t, sin_t, qg, kg):
    b, s, _ = pm3.shape
    qw = GA_HEADS * GA_DH
    kw = GA_KV_HEADS * GA_DH
    group = GA_HEADS // GA_KV_HEADS
    return pl.pallas_call(
        _gqa_kernel,
        grid=(b,),
        in_specs=[
            pl.BlockSpec((None, s, qw), lambda i: (i, 0, _OFF["c_q"] // qw)),
            pl.BlockSpec((None, s, kw), lambda i: (i, 0, _OFF["c_k"] // kw)),
            pl.BlockSpec((None, s, kw), lambda i: (i, 0, _OFF["c_v"] // kw)),
            pl.BlockSpec((None, s, qw), lambda i: (i, 0, _OFF["c_z"] // qw)),
            pl.BlockSpec((s, GA_DH), lambda i: (0, 0)),
            pl.BlockSpec((s, GA_DH), lambda i: (0, 0)),
            pl.BlockSpec((1, GA_DH), lambda i: (0, 0)),
            pl.BlockSpec((1, GA_DH), lambda i: (0, 0)),
        ],
        out_specs=pl.BlockSpec((None, s, qw), lambda i: (i, 0, 0)),
        out_shape=jax.ShapeDtypeStruct((b, s, BRANCH_W), BF16),
        scratch_shapes=[pltpu.VMEM((group * s, GA_DH), BF16), pltpu.VMEM((GA_KV_HEADS, s, GA_DH), BF16),
                        pltpu.VMEM((GA_KV_HEADS, s, GA_DH), BF16),
                        pltpu.VMEM((GQA_TQ, s), F32), pltpu.VMEM((GQA_TQ, s), F32),
                        pltpu.VMEM((GQA_TQ, s), BF16), pltpu.VMEM((GQA_TQ, s), BF16),
                        pltpu.VMEM((GQA_TQ, GA_DH), F32), pltpu.VMEM((GQA_TQ, GA_DH), F32)],
        compiler_params=_cparams(("parallel",)),
        name="gqa",
    )(pm3, pm3, pm3, pm3, cos_t, sin_t, qg, kg)


NA_ROW_UNROLL = 16


def _na_kernel(q_ref, k_ref, v_ref, z_ref, bias_ref, qg_ref, kg_ref, y_ref, q_scr, k_scr):
    s_len = q_ref.shape[0]
    rows = s_len // GRID_W
    kr = min(NA_ROWS, rows)
    hi = lax.broadcasted_iota(jnp.int32, (2 * NA_DH, 2 * NA_DH), 0) >= NA_DH
    hj = lax.broadcasted_iota(jnp.int32, (2 * NA_DH, 2 * NA_DH), 1) >= NA_DH
    same_head = jnp.where(hi == hj, 1.0, 0.0).astype(BF16)

    def rms_pair(x, g):
        x2 = x * x
        x2_hi = x2.astype(BF16)
        x2_lo = (x2 - x2_hi.astype(F32)).astype(BF16)
        ssq = (jnp.dot(x2_hi, same_head, preferred_element_type=F32)
               + jnp.dot(x2_lo, same_head, preferred_element_type=F32))
        return x * lax.rsqrt(ssq * (1.0 / NA_DH) + EPS) * g

    scale = NA_DH ** -0.5
    assert math.log2(scale).is_integer()
    nkeys = kr * GRID_W
    first = lax.broadcasted_iota(jnp.int32, (s_len, 2 * NA_DH), 1) < NA_DH
    first_q = lax.broadcasted_iota(jnp.int32, (GRID_W, 2 * NA_DH), 1) < NA_DH
    for pair in range(q_ref.shape[1] // (2 * NA_DH)):
        pc = slice(pair * 2 * NA_DH, (pair + 1) * 2 * NA_DH)
        qn = rms_pair(q_ref[:, pc].astype(F32), qg_ref[...]) * scale
        q_scr[0] = jnp.where(first, qn, 0.0).astype(BF16)
        q_scr[1] = jnp.where(first, 0.0, qn).astype(BF16)
        k_scr[...] = rms_pair(k_ref[:, pc].astype(F32), kg_ref[...]).astype(BF16)

        def body(it, carry, pair=pair, pc=pc):
            units = []
            for u in range(NA_ROW_UNROLL):
                r = it * NA_ROW_UNROLL + u
                r0 = jnp.clip(r - kr // 2, 0, rows - kr)
                var = r0 - r + (NA_ROWS - 1)
                qrows = pl.ds(pl.multiple_of(r * GRID_W, GRID_W), GRID_W)
                krows = pl.ds(pl.multiple_of(r0 * GRID_W, GRID_W), nkeys)
                units.append((var, qrows, krows))
            logits = [_mm_nt(jnp.concatenate([q_scr[0, qrows, :], q_scr[1, qrows, :]], axis=0), k_scr[krows, :])
                      + bias_ref[pair, var] for var, qrows, krows in units]
            probs = [jnp.exp(s - jnp.max(s, axis=-1, keepdims=True)) for s in logits]
            sums = [jnp.sum(p, axis=-1, keepdims=True) for p in probs]
            outs = [_mm(p, v_ref[krows, pc]) / l for p, l, (_, _, krows) in zip(probs, sums, units)]
            for o, (_, qrows, _) in zip(outs, units):
                o_pair = jnp.where(first_q, o[:GRID_W], o[GRID_W:])
                y_ref[qrows, pc] = (o_pair * _silu(z_ref[qrows, pc].astype(F32))).astype(y_ref.dtype)
            return carry

        lax.fori_loop(0, rows // NA_ROW_UNROLL, body, 0)


def _na_bias_table(rpb, rows):
    kr = min(NA_ROWS, rows)
    c = jnp.arange(GRID_W)
    c0 = jnp.clip(c - NA_COLS // 2, 0, GRID_W - NA_COLS)
    in_win = (c[None, :] >= c0[:, None]) & (c[None, :] < c0[:, None] + NA_COLS)
    col_off = jnp.clip(c[None, :] - c[:, None], -(NA_COLS - 1), NA_COLS - 1) + NA_COLS - 1
    t = jnp.where(in_win, rpb[..., col_off], NEG)
    ro = jnp.arange(NA_ROWS)[:, None] + jnp.arange(kr)[None, :]
    tv = jnp.take(t, ro, axis=-3)
    tv = jnp.swapaxes(tv, -3, -2).reshape(*rpb.shape[:-2], NA_ROWS, GRID_W, kr * GRID_W)
    lead = rpb.shape[:-3]
    tv = tv.reshape(*lead, rpb.shape[-3] // 2, 2, NA_ROWS, GRID_W, kr * GRID_W)
    tv = jnp.swapaxes(tv, -4, -3)
    return tv.reshape(*lead, rpb.shape[-3] // 2, NA_ROWS, 2 * GRID_W, kr * GRID_W).astype(F32)


def _natten(pm3, bias, qg, kg, layer):
    b, s, _ = pm3.shape
    pw = 2 * NA_DH
    hw = NA_HEADS * NA_DH
    return pl.pallas_call(
        _na_kernel,
        grid=(b,),
        in_specs=[
            pl.BlockSpec((None, s, hw), lambda i: (i, 0, _OFF["b_qkv"] // hw)),
            pl.BlockSpec((None, s, hw), lambda i: (i, 0, _OFF["b_qkv"] // hw + 1)),
            pl.BlockSpec((None, s, hw), lambda i: (i, 0, _OFF["b_qkv"] // hw + 2)),
            pl.BlockSpec((None, s, hw), lambda i: (i, 0, _OFF["b_z"] // hw)),
            pl.BlockSpec((None,) + bias.shape[1:], lambda i: (layer, 0, 0, 0, 0)),
            pl.BlockSpec((1, pw), lambda i: (0, 0)),
            pl.BlockSpec((1, pw), lambda i: (0, 0)),
        ],
        out_specs=pl.BlockSpec((None, s, hw), lambda i: (i, 0, 0)),
        out_shape=jax.ShapeDtypeStruct((b, s, BRANCH_W), BF16),
        scratch_shapes=[pltpu.VMEM((2, s, pw), BF16), pltpu.VMEM((s, pw), BF16)],
        compiler_params=_cparams(("parallel",)),
        name="natten",
    )(pm3, pm3, pm3, pm3, bias, qg, kg)


INV_BASE = 8
PRE_BLK = 4 * BLK


def _inverse_level_masks():
    i = lax.broadcasted_iota(jnp.int32, (BLK, BLK), 0)
    j = lax.broadcasted_iota(jnp.int32, (BLK, BLK), 1)
    same = lambda size: (i >> int(math.log2(size))) == (j >> int(math.log2(size)))
    base = same(INV_BASE)
    joins = []
    size = INV_BASE
    while size < DN_CHUNK:
        joins.append(same(2 * size) & jnp.logical_not(same(size)))
        size *= 2
    return base, joins


def _tri_inverses(l_mats, eyes, level_masks):
    base, joins = level_masks
    ps = [jnp.where(base, -l, 0.0) for l in l_mats]
    ts = [eye + p for eye, p in zip(eyes, ps)]
    for _ in range(int(math.log2(INV_BASE)) - 1):
        ps = [_mm(p, p) for p in ps]
        ts = [t + _mm(t, p) for t, p in zip(ts, ps)]
    for join in joins:
        mids = [_mm(jnp.where(join, l, 0.0), t) for l, t in zip(l_mats, ts)]
        ts = [t - _mm(t, mid) for t, mid in zip(ts, mids)]
    return ts


DN_HW = DN_HEADS * LANES
PK_W, PK_QD, PK_KD, PK_QK = (i * DN_HW for i in range(4))


def _dn_pre_kernel(qkv_ref, sm_ref, cw_ref, lp_ref, uf_ref, ub_ref, pkf_ref, pkb_ref, gtf_ref, gtb_ref):
    n = pl.program_id(1)
    s_len = qkv_ref.shape[0]
    assert qkv_ref.dtype == BF16
    halo = 2 * SUBLANES
    n_sub = sm_ref.shape[0] // BLK
    chunks_per_blk = BLK // DN_CHUNK
    out_row = lax.broadcasted_iota(jnp.int32, (BLK, BLK + 2 * halo), 0)
    in_row = lax.broadcasted_iota(jnp.int32, (BLK, BLK + 2 * halo), 1)
    shifts = {j: jnp.where(in_row == out_row + (halo + j - DN_CONV // 2), 1.0, 0.0).astype(BF16)
              for j in range(DN_CONV) if j != DN_CONV // 2}
    dir_masks = [_chunk_masks(d == 1, DN_CHUNK) for d in range(2)]
    level_masks = _inverse_level_masks()
    outs = ((uf_ref, pkf_ref, gtf_ref), (ub_ref, pkb_ref, gtb_ref))

    def l2n(x):
        return x * lax.rsqrt(jnp.sum(x * x, axis=-1, keepdims=True) + EPS)

    blocks = []
    for sub in range(n_sub):
        t0 = pl.multiple_of((n * n_sub + sub) * BLK, BLK)
        pstart = pl.multiple_of(jnp.maximum(t0 - halo, 0), halo)
        nstart = pl.multiple_of(jnp.minimum(t0 + BLK, s_len - halo), halo)
        prev = qkv_ref[pl.ds(pstart, halo), :]
        prev = jnp.where(t0 > 0, prev, jnp.zeros_like(prev))
        cur = qkv_ref[pl.ds(t0, BLK), :]
        nxt = qkv_ref[pl.ds(nstart, halo), :]
        nxt = jnp.where(t0 + BLK < s_len, nxt, jnp.zeros_like(nxt))
        xw = jnp.concatenate([prev, cur, nxt], axis=0)
        conv = cur.astype(F32) * cw_ref[DN_CONV // 2:DN_CONV // 2 + 1, :]
        for j, shift in shifts.items():
            conv = conv + jnp.dot(shift, xw, preferred_element_type=F32) * cw_ref[j:j + 1, :]
        conv = _silu(conv)

        rows = slice(sub * BLK, (sub + 1) * BLK)
        sm = sm_ref[rows, :]
        g_all = -jnp.exp(lp_ref[0:1, :]) * _softplus(sm + lp_ref[1:2, :])
        beta_all = _sigmoid(sm)
        per_dir = []
        for d in range(2):
            same, incl, strict, eye_b = dir_masks[d]
            gc = _mask_sum(incl, g_all)
            tot = _mask_sum(same, g_all)
            per_dir.append((incl, strict, _as_f32(eye_b), gc, gc.T, tot))
        blocks.append((sub, rows, conv, beta_all, per_dir))

    heads = []
    for sub, rows, conv, beta_all, per_dir in blocks:
        for h in range(DN_HEADS):
            q = l2n(conv[:, h * DN_DK:(h + 1) * DN_DK]) * (DN_DK ** -0.5)
            k = l2n(conv[:, (DN_HEADS + h) * DN_DK:(DN_HEADS + h + 1) * DN_DK])
            v = conv[:, 2 * DN_HEADS * DN_DK + h * DN_DV:2 * DN_HEADS * DN_DK + (h + 1) * DN_DV]
            heads.append((sub, rows, beta_all, per_dir, h, q, k, v))
    kks = [_mm_nt(hd[6], hd[6]) for hd in heads]
    qks = [_mm_nt(hd[5], hd[6]) for hd in heads]

    l_mats, rhss, eyes, slots = [], [], [], []
    for (sub, rows, beta_all, per_dir, h, q, k, v), kk, qk in zip(heads, kks, qks):
        hc = slice(h * LANES, (h + 1) * LANES)
        for d in range(2):
            incl, strict, eye_f, gc, gct, tot = per_dir[d]
            u_ref, pk_ref, gt_ref = outs[d]
            pk = lambda off: slice(off + h * LANES, off + (h + 1) * LANES)
            c = _L_AA + d * DN_HEADS + h
            gcol = gc[:, c:c + 1]
            grow = gct[c:c + 1, :]
            tcol = tot[:, c:c + 1]
            beta = beta_all[:, _L_AB + d * DN_HEADS + h:_L_AB + d * DN_HEADS + h + 1]
            decay = jnp.where(incl, jnp.exp(jnp.where(incl, gcol - grow, 0.0)), 0.0)
            l_mats.append(jnp.where(strict, beta * kk * decay, 0.0))
            egc = jnp.exp(gcol)
            rhss.append(jnp.concatenate([v * beta, k * (beta * egc)], axis=-1).astype(BF16))
            eyes.append(eye_f)
            slots.append((rows, h, d))
            pk_ref[rows, pk(PK_QD)] = (q * egc).astype(pk_ref.dtype)
            pk_ref[rows, pk(PK_KD)] = (k * jnp.exp(tcol - gcol)).astype(pk_ref.dtype)
            pk_ref[rows, pk(PK_QK)] = (qk * decay).astype(pk_ref.dtype)
            gtot = jnp.exp(tcol)
            for ci in range(chunks_per_blk):
                g0 = (sub * chunks_per_blk + ci) * SUBLANES
                gt_ref[g0:g0 + SUBLANES, hc] = jnp.broadcast_to(
                    gtot[ci * DN_CHUNK:ci * DN_CHUNK + SUBLANES, :], (SUBLANES, LANES))

    t_invs = _tri_inverses(l_mats, eyes, level_masks)
    sols = [_mm(t, rhs) for t, rhs in zip(t_invs, rhss)]
    for (rows, h, d), sol in zip(slots, sols):
        u_ref, pk_ref = outs[d][0], outs[d][1]
        u_ref[rows, h * LANES:(h + 1) * LANES] = sol[:, :DN_DV]
        pk_ref[rows, PK_W + h * LANES:PK_W + (h + 1) * LANES] = sol[:, DN_DV:].astype(pk_ref.dtype)


def _dn_pre(pm3, ps3, conv_w8, lane_params):
    b, s, _ = pm3.shape
    nblk = s // PRE_BLK
    gt_rows = PRE_BLK // DN_CHUNK * SUBLANES
    wq = DN_HEADS * (2 * DN_DK + DN_DV)
    hw = DN_HEADS * LANES
    tok = lambda i, j: (i, j, 0)
    big = lambda dt: jax.ShapeDtypeStruct((b, s, hw), dt)
    gts = jax.ShapeDtypeStruct((b, nblk * gt_rows, hw), F32)
    bs_tok = pl.BlockSpec((None, PRE_BLK, hw), tok)
    bs_gt = pl.BlockSpec((None, gt_rows, hw), tok)
    bs_pk = pl.BlockSpec((None, PRE_BLK, 4 * hw), tok)
    packed = jax.ShapeDtypeStruct((b, s, 4 * hw), BF16)
    return pl.pallas_call(
        _dn_pre_kernel,
        grid=(b, nblk),
        in_specs=[
            pl.BlockSpec((None, s, wq), lambda i, j: (i, 0, _OFF["a_qkv"] // wq)),
            pl.BlockSpec((None, PRE_BLK, LANES), tok),
            pl.BlockSpec((SUBLANES, wq), lambda i, j: (0, 0)),
            pl.BlockSpec((SUBLANES, LANES), lambda i, j: (0, 0)),
        ],
        out_specs=[bs_tok, bs_tok, bs_pk, bs_pk, bs_gt, bs_gt],
        out_shape=[big(F32), big(F32), packed, packed, gts, gts],
        compiler_params=_cparams(("parallel", "arbitrary")),
        name="dn_pre",
    )(pm3, ps3, conv_w8, lane_params)


SCAN_BLK = 4 * BLK


def _zero_at_sequence_start(*scratch):
    @pl.when(pl.program_id(1) == 0)
    def _():
        for ref in scratch:
            ref[...] = jnp.zeros_like(ref)


def _dn_scan_kernel(uf_ref, pkf_ref, gtf_ref, ub_ref, pkb_ref, gtb_ref, of_ref, ob_ref, st_scr):
    _zero_at_sequence_start(st_scr)
    _dn_scan_body(uf_ref, pkf_ref, gtf_ref, ub_ref, pkb_ref, gtb_ref, of_ref, ob_ref, st_scr)


def _dn_scan_body(uf_ref, pkf_ref, gtf_ref, ub_ref, pkb_ref, gtb_ref, of_ref, ob_ref, st_scr):
    per_blk = BLK // DN_CHUNK
    nchunk = SCAN_BLK // DN_CHUNK
    zeros_c = jnp.zeros((DN_CHUNK, DN_DV), F32)
    streams = ((uf_ref, pkf_ref, gtf_ref, of_ref, range(nchunk)),
               (ub_ref, pkb_ref, gtb_ref, ob_ref, range(nchunk - 1, -1, -1)))
    chains = [(d, h) + streams[d] for d in range(2) for h in range(DN_HEADS)]
    states = [st_scr[d * DN_HEADS + h] for d, h, *_ in chains]
    for step in range(nchunk):
        rs, v_pads = [], []
        for (d, h, u_ref, pk_ref, gt_ref, o_ref, order), state in zip(chains, states):
            rows = slice(order[step] * DN_CHUNK, (order[step] + 1) * DN_CHUNK)
            w = pk_ref[rows, PK_W + h * LANES:PK_W + (h + 1) * LANES]
            qd = pk_ref[rows, PK_QD + h * LANES:PK_QD + (h + 1) * LANES]
            rs.append(_mm(jnp.concatenate([w, qd], axis=0), state))
        for (d, h, u_ref, pk_ref, gt_ref, o_ref, order), r in zip(chains, rs):
            ci = order[step]
            rows = slice(ci * DN_CHUNK, (ci + 1) * DN_CHUNK)
            parts = [zeros_c] * per_blk
            parts[ci % per_blk] = u_ref[rows, h * LANES:(h + 1) * LANES] - r[:DN_CHUNK]
            v_pads.append(jnp.concatenate(parts, axis=0))
        new_states = []
        for (d, h, u_ref, pk_ref, gt_ref, o_ref, order), r, v_pad, state in zip(chains, rs, v_pads, states):
            hc = slice(h * LANES, (h + 1) * LANES)
            ci = order[step]
            rows = slice(ci * DN_CHUNK, (ci + 1) * DN_CHUNK)
            blk_rows = slice((ci // per_blk) * BLK, (ci // per_blk + 1) * BLK)
            o_ref[rows, hc] = r[DN_CHUNK:] + _mm(pk_ref[rows, PK_QK + h * LANES:PK_QK + (h + 1) * LANES], v_pad)
            gt = gt_ref[ci * SUBLANES:ci * SUBLANES + 1, hc]
            kd = pk_ref[blk_rows, PK_KD + h * LANES:PK_KD + (h + 1) * LANES]
            new_states.append(state * gt + _mm_tn(kd, v_pad))
        states = new_states
    for (d, h, *_), state in zip(chains, states):
        st_scr[d * DN_HEADS + h] = state


def _dn_scan_parts(pre):
    uf, ub, pkf, pkb, gtf, gtb = pre
    b, s, hw = uf.shape
    nblk = s // SCAN_BLK
    gt_rows = SCAN_BLK // DN_CHUNK * SUBLANES
    fwd = lambda i, j: (i, j, 0)
    bwd = lambda i, j: (i, nblk - 1 - j, 0)
    def specs(imap):
        return [pl.BlockSpec((None, SCAN_BLK, hw), imap), pl.BlockSpec((None, SCAN_BLK, 4 * hw), imap),
                pl.BlockSpec((None, gt_rows, hw), imap)]
    return dict(
        grid=(b, nblk), args=(uf, pkf, gtf, ub, pkb, gtb), in_specs=specs(fwd) + specs(bwd),
        out_specs=[pl.BlockSpec((None, SCAN_BLK, hw), fwd), pl.BlockSpec((None, SCAN_BLK, hw), bwd)],
        out_shape=[jax.ShapeDtypeStruct((b, s, hw), F32)] * 2,
        scratch_shapes=[pltpu.VMEM((2 * DN_HEADS, DN_DK, DN_DV), F32)])


def _call_parts(kernel_fn, name, parts):
    return pl.pallas_call(
        kernel_fn, grid=parts["grid"], in_specs=parts["in_specs"], out_specs=parts["out_specs"],
        out_shape=parts["out_shape"], scratch_shapes=parts["scratch_shapes"],
        compiler_params=_cparams(("parallel", "arbitrary")), name=name)(*parts["args"])


def _dn_scan(pre):
    return _call_parts(_dn_scan_kernel, "dn_scan", _dn_scan_parts(pre))


def _scan_and_mlstm(pre, pm3, ps3, ml_lane_params):
    dn, ml = _dn_scan_parts(pre), _mlstm_parts(pm3, ps3, ml_lane_params)
    assert dn["grid"] == ml["grid"]
    n_in = (len(dn["args"]), len(ml["args"]))
    n_out = (len(dn["out_shape"]), len(ml["out_shape"]))

    def both(*refs):
        ins, rest = refs[:sum(n_in)], refs[sum(n_in):]
        outs, scr = rest[:sum(n_out)], rest[sum(n_out):]
        n_dn_scr = len(dn["scratch_shapes"])
        _zero_at_sequence_start(*scr)
        _dn_scan_body(*ins[:n_in[0]], *outs[:n_out[0]], *scr[:n_dn_scr])
        _ml_body(*ins[n_in[0]:], *outs[n_out[0]:], *scr[n_dn_scr:])

    fused = {k: dn[k] + ml[k] for k in ("args", "in_specs", "out_specs", "out_shape", "scratch_shapes")}
    fused["grid"] = dn["grid"]
    return _call_parts(both, "scan_mlstm", fused)


ML_AUG = 2 * LANES
ML_GROUP = 4
ML_BLK = 4 * BLK


def _ml_kernel(qf_ref, kf_ref, vf_ref, smf_ref, qb_ref, kb_ref, vb_ref, smb_ref, lp_ref,
               hf_ref, hb_ref, c_scr, m_scr):
    _zero_at_sequence_start(c_scr, m_scr)
    _ml_body(qf_ref, kf_ref, vf_ref, smf_ref, qb_ref, kb_ref, vb_ref, smb_ref, lp_ref, hf_ref, hb_ref, c_scr, m_scr)


def _ml_body(qf_ref, kf_ref, vf_ref, smf_ref, qb_ref, kb_ref, vb_ref, smb_ref, lp_ref,
             hf_ref, hb_ref, c_scr, m_scr):
    nchunk = BLK // ML_CHUNK
    ones_col = jnp.ones((BLK, LANES), BF16)
    zeros_aug = jnp.zeros((ML_CHUNK, ML_AUG), BF16)
    streams = ((qf_ref, kf_ref, vf_ref, smf_ref, hf_ref, range(nchunk)),
               (qb_ref, kb_ref, vb_ref, smb_ref, hb_ref, range(nchunk - 1, -1, -1)))
    lanes = lambda col: jnp.broadcast_to(col, (col.shape[0], LANES))
    n_sub = smf_ref.shape[0] // BLK
    units = [(d, p if d == 0 else n_sub - 1 - p) for p in range(n_sub) for d in range(2)]
    for d, sub in units:
        _ml_block(d, sub, streams[d], lp_ref, c_scr, m_scr, lanes, ones_col, zeros_aug, nchunk)


def _ml_block(d, sub, stream, lp_ref, c_scr, m_scr, lanes, ones_col, zeros_aug, nchunk):
    q_ref, k_ref, v_ref, sm_ref, h_ref, order = stream
    blk_rows = slice(sub * BLK, (sub + 1) * BLK)
    same, incl, _, _ = _chunk_masks(d == 1, ML_CHUNK)
    sm = sm_ref[blk_rows, :]
    ig_all = sm + lp_ref[0:1, :]
    x = sm + lp_ref[1:2, :]
    lf_all = jnp.minimum(x, 0.0) - jnp.log(1.0 + jnp.exp(-jnp.abs(x)))
    lf_all = pltpu.roll(lf_all, LANES - (_L_DF - _L_DI), 1)
    bc_all = _mask_sum(incl, lf_all)
    tot_all = _chunk_totals(bc_all, ML_CHUNK, reverse=(d == 1))
    a_all = ig_all - bc_all
    mwa_all = jnp.concatenate(
        [jnp.broadcast_to(jnp.max(a_all[ci * ML_CHUNK:(ci + 1) * ML_CHUNK], axis=0, keepdims=True),
                          (ML_CHUNK, LANES)) for ci in range(nchunk)], axis=0)
    a_t, w_all, mw_all = a_all.T, jnp.exp(a_all - mwa_all), tot_all + mwa_all

    for chains in [[(d, h) for h in range(g, g + ML_GROUP)] for g in range(0, ML_HEADS, ML_GROUP)]:
        ins = []
        for _, h in chains:
            q = q_ref[blk_rows, h * ML_DK:(h + 1) * ML_DK].astype(BF16)
            k = k_ref[blk_rows, h * ML_DK:(h + 1) * ML_DK].astype(F32) * (ML_DK ** -0.5)
            v_aug = jnp.concatenate([v_ref[blk_rows, h * ML_DV:(h + 1) * ML_DV].astype(BF16), ones_col], axis=-1)
            ins.append((q, k, v_aug))
        qks = [_mm_nt(q, k) for q, k, _ in ins]

        mids = []
        for (d, h), (q, k, v_aug), qk in zip(chains, ins, qks):
            c = _L_DI + d * ML_HEADS + h
            b_l = lanes(bc_all[:, c:c + 1])
            dlog = jnp.where(incl, b_l + a_t[c:c + 1, :], NEG)
            m_intra = lanes(jnp.max(dlog, axis=-1, keepdims=True))
            s_intra = qk * jnp.exp(dlog - m_intra)
            wk = (k * lanes(w_all[:, c:c + 1])[:, :ML_DK]).astype(BF16)
            mids.append((b_l, m_intra, s_intra.astype(BF16), wk))
        p_intras = [_mm(s_b, v_aug) for (_, _, s_b, _), (_, _, v_aug) in zip(mids, ins)]
        kvs = []
        for (_, _, _, wk), (_, _, v_aug) in zip(mids, ins):
            per_chunk = []
            for ci in range(nchunk):
                parts = [zeros_aug] * nchunk
                parts[ci] = v_aug[ci * ML_CHUNK:(ci + 1) * ML_CHUNK]
                per_chunk.append(_mm_tn(wk, jnp.concatenate(parts, axis=0)))
            kvs.append(per_chunk)

        c_sts = [c_scr[d * ML_HEADS + h] for d, h in chains]
        m_sts = [m_scr[d * ML_HEADS + h][0:1, :] for d, h in chains]
        for step in range(nchunk):
            qcs = []
            ci = order[step]
            for (q, _, _), c_st in zip(ins, c_sts):
                qcs.append(_mm(q[ci * ML_CHUNK:(ci + 1) * ML_CHUNK], c_st))
            for idx, (_, h) in enumerate(chains):
                b_l, m_intra, _, _ = mids[idx]
                c = _L_DI + d * ML_HEADS + h
                rows = slice(ci * ML_CHUNK, (ci + 1) * ML_CHUNK)
                out_rows = slice(sub * BLK + ci * ML_CHUNK, sub * BLK + (ci + 1) * ML_CHUNK)
                r8 = slice(ci * ML_CHUNK, ci * ML_CHUNK + SUBLANES)
                m_st, c_st, qc = m_sts[idx], c_sts[idx], qcs[idx]
                m_inter = b_l[rows] + m_st
                m_i = jnp.maximum(m_intra[rows], m_inter)
                f_i = jnp.exp(m_intra[rows] - m_i)
                inter = jnp.exp(m_inter - m_i)
                both = (jnp.concatenate([inter, inter], axis=-1) * qc
                        + jnp.concatenate([f_i, f_i], axis=-1) * p_intras[idx][rows])
                numer, denom = both[:, :ML_DV], both[:, ML_DV:]
                h_ref[out_rows, h * ML_DV:(h + 1) * ML_DV] = numer / jnp.maximum(jnp.abs(denom), jnp.exp(-m_i))
                tot_c = lanes(tot_all[r8, c:c + 1])[0:1]
                mw_c = lanes(mw_all[r8, c:c + 1])[0:1]
                m_new = jnp.maximum(tot_c + m_st, mw_c)
                dec = jnp.exp(tot_c + m_st - m_new)
                gain = jnp.exp(mw_c - m_new)
                c_sts[idx] = (jnp.concatenate([dec, dec], axis=-1) * c_st
                              + jnp.concatenate([gain, gain], axis=-1) * kvs[idx][ci])
                m_sts[idx] = m_new
        for idx, (d, h) in enumerate(chains):
            c_scr[d * ML_HEADS + h] = c_sts[idx]
            m_scr[d * ML_HEADS + h] = jnp.broadcast_to(m_sts[idx], (SUBLANES, LANES))


def _mlstm(pm3, ps3, lane_params):
    return _call_parts(_ml_kernel, "mlstm", _mlstm_parts(pm3, ps3, lane_params))


def _mlstm_parts(pm3, ps3, lane_params):
    b, s, _ = pm3.shape
    nblk = s // ML_BLK
    qw = ML_HEADS * ML_DK
    vw = ML_HEADS * ML_DV
    def specs(tmap):
        blk = lambda j: tmap(j)
        return [
            pl.BlockSpec((None, ML_BLK, qw), lambda i, j: (i, blk(j), _OFF["d_q"] // qw)),
            pl.BlockSpec((None, ML_BLK, qw), lambda i, j: (i, blk(j), _OFF["d_k"] // qw)),
            pl.BlockSpec((None, ML_BLK, vw), lambda i, j: (i, blk(j), _OFF["d_v"] // vw)),
            pl.BlockSpec((None, ML_BLK, LANES), lambda i, j: (i, blk(j), 0)),
        ]
    fwd = lambda j: j
    bwd = lambda j: nblk - 1 - j
    return dict(
        grid=(b, nblk), args=(pm3, pm3, pm3, ps3, pm3, pm3, pm3, ps3, lane_params),
        in_specs=specs(fwd) + specs(bwd) + [pl.BlockSpec((SUBLANES, LANES), lambda i, j: (0, 0))],
        out_specs=[pl.BlockSpec((None, ML_BLK, vw), lambda i, j: (i, j, 0)),
                   pl.BlockSpec((None, ML_BLK, vw), lambda i, j: (i, nblk - 1 - j, 0))],
        out_shape=[jax.ShapeDtypeStruct((b, s, vw), F32)] * 2,
        scratch_shapes=[pltpu.VMEM((2 * ML_HEADS, ML_DK, ML_AUG), F32),
                        pltpu.VMEM((2 * ML_HEADS, SUBLANES, LANES), F32)])


def _merge_kernel(x_ref, af_ref, ab_ref, df_ref, db_ref, yb_ref, yc_ref, az_ref, dz_ref, do_ref, gl_ref,
                  ag_ref, dg_ref, wb_ref, wo_ref, o_ref):
    d = x_ref.shape[-1]

    def head_rms(x, g):
        outs = []
        for h in range(x.shape[-1] // LANES):
            xh = x[:, h * LANES:(h + 1) * LANES]
            ms = jnp.mean(xh * xh, axis=-1, keepdims=True)
            outs.append(xh * lax.rsqrt(ms + EPS) * g)
        return jnp.concatenate(outs, axis=-1)

    ya = head_rms(af_ref[...] + ab_ref[...], ag_ref[...]) * _silu(az_ref[...].astype(F32))
    yd = _sigmoid(do_ref[...].astype(F32)) * head_rms(df_ref[...] + db_ref[...], dg_ref[...])
    yd = yd * _silu(dz_ref[...].astype(F32))
    twice = None
    for i, y in enumerate((ya.astype(BF16), yb_ref[...], yc_ref[...], yd.astype(BF16))):
        proj = jnp.dot(y, wb_ref[i], preferred_element_type=F32)
        term = proj + jnp.tanh(0.5 * gl_ref[:, i * d:(i + 1) * d].astype(F32)) * proj
        twice = term if twice is None else twice + term
    merged = (0.5 * twice).astype(BF16)
    o_ref[...] = x_ref[...] + jnp.dot(merged, wo_ref[...], preferred_element_type=F32)


def _merge(x2d, af, ab, df, db, yb, yc, pm2, ag, dg, wb, wo, layer, tm=512):
    m, d = x2d.shape
    gw = N_BRANCH * d
    w = BRANCH_W
    tok = pl.BlockSpec((tm, w), lambda i: (i, 0))
    col = lambda name: pl.BlockSpec((tm, w), lambda i: (i, _OFF[name] // w))
    vec = pl.BlockSpec((1, LANES), lambda i: (0, 0))
    return pl.pallas_call(
        _merge_kernel,
        grid=(m // tm,),
        in_specs=[
            pl.BlockSpec((tm, d), lambda i: (i, 0)),
            tok, tok, tok, tok, tok, tok,
            col("a_z"), col("d_z"), col("d_o"),
            pl.BlockSpec((tm, gw), lambda i: (i, _OFF["gate"] // gw)),
            vec, vec,
            pl.BlockSpec((None, N_BRANCH, w, d), lambda i: (layer, 0, 0, 0)),
            pl.BlockSpec((None, d, d), lambda i: (layer, 0, 0)),
        ],
        out_specs=pl.BlockSpec((tm, d), lambda i: (i, 0)),
        out_shape=jax.ShapeDtypeStruct((m, d), F32),
        compiler_params=_cparams(("parallel",)),
        name="merge",
    )(x2d, af, ab, df, db, yb, yc, pm2, pm2, pm2, pm2, ag, dg, wb, wo)


def _rope_lane_tables(s):
    t = jnp.arange(s)
    row = (t // GRID_W).astype(F32)
    col = (t % GRID_W).astype(F32)
    m = GA_DH // 4
    inv = ROPE_THETA ** (-jnp.arange(m, dtype=F32) / m)
    ar = row[:, None] * inv
    ac = col[:, None] * inv
    cos_t = jnp.concatenate([jnp.cos(ar), jnp.cos(ar), jnp.cos(ac), jnp.cos(ac)], axis=-1)
    sin_t = jnp.concatenate([-jnp.sin(ar), jnp.sin(ar), -jnp.sin(ac), jnp.sin(ac)], axis=-1)
    return cos_t.astype(F32), sin_t.astype(F32)


def _main_columns(w):
    return jnp.concatenate([w[..., o:o + wd] for _, o, wd in _MAIN_SEGS], axis=-1)


def _lane_tiles(rows):
    padded = []
    for off, vals in rows:
        vals = vals.reshape(vals.shape[0], 1, -1).astype(F32)
        padded.append(jnp.pad(vals, ((0, 0), (0, 0), (off, LANES - off - vals.shape[-1]))))
    tiles = jnp.concatenate(padded, axis=1)
    return jnp.pad(tiles, ((0, 0), (0, SUBLANES - len(rows)), (0, 0)))


def kernel(x, norm_g, w_in, conv_a, dn_a_log, dn_dt_bias, dn_norm_g, na_q_norm, na_k_norm, na_rpb,
           ga_q_norm, ga_k_norm, ml_i_bias, ml_f_bias, ml_norm_g, w_branch, w_out):
    b, s, d = x.shape
    depth = w_in.shape[0]
    hw = BRANCH_W
    cos_t, sin_t = _rope_lane_tables(s)
    w_main = _main_columns(w_in).astype(BF16)
    w_small = jnp.pad(jnp.concatenate([w_in[:, :, o:o + 8] for o in _SMALL_SRC], axis=2),
                      ((0, 0), (0, 0), (0, LANES - 8 * len(_SMALL_SRC)))).astype(BF16)
    conv8 = jnp.pad(conv_a.astype(F32), ((0, 0), (0, SUBLANES - DN_CONV), (0, 0)))
    dn_lp = _lane_tiles([(_L_AA, dn_a_log), (_L_AA, dn_dt_bias)])
    ml_lp = _lane_tiles([(_L_DI, ml_i_bias), (_L_DF, ml_f_bias)])
    na_bias = _na_bias_table(na_rpb, s // GRID_W)
    na_qg = jnp.tile(na_q_norm, (1, 2)).reshape(depth, 1, 2 * NA_DH)
    na_kg = jnp.tile(na_k_norm, (1, 2)).reshape(depth, 1, 2 * NA_DH)
    ga_qg = ga_q_norm.reshape(depth, 1, GA_DH)
    ga_kg = ga_k_norm.reshape(depth, 1, GA_DH)
    wb_bf, wo_bf = w_branch.astype(BF16), w_out.astype(BF16)

    x2 = x.reshape(b * s, d)
    for l in range(depth):
        pm2, ps2 = _inproj(x2, norm_g[l].reshape(1, d), w_main, w_small, l)
        pm3 = pm2.reshape(b, s, N_MAIN)
        ps3 = ps2.reshape(b, s, LANES)
        o_af, o_ab, h_df, h_db = _scan_and_mlstm(_dn_pre(pm3, ps3, conv8[l], dn_lp[l]), pm3, ps3, ml_lp[l])
        yb = _natten(pm3, na_bias, na_qg[l], na_kg[l], l)
        yc = _gqa(pm3, cos_t, sin_t, ga_qg[l], ga_kg[l])
        x2 = _merge(x2, o_af.reshape(b * s, hw), o_ab.reshape(b * s, hw), h_df.reshape(b * s, hw),
                    h_db.reshape(b * s, hw), yb.reshape(b * s, hw), yc.reshape(b * s, hw), pm2,
                    dn_norm_g[l].reshape(1, LANES), ml_norm_g[l].reshape(1, LANES), wb_bf, wo_bf, l)
    return x2.reshape(b, s, d)
```

```python
import math

import jax
import jax.numpy as jnp
from jax import lax
from jax.experimental import pallas as pl
from jax.experimental.pallas import tpu as pltpu

F32 = jnp.float32
BF16 = jnp.bfloat16

GRID_W = 64
N_BRANCH = 4
BRANCH_W = 512
EPS = 1e-6
DN_HEADS, DN_DK, DN_DV, DN_CONV, DN_CHUNK = 4, 128, 128, 5, 64
NA_HEADS, NA_DH, NA_ROWS, NA_COLS = 8, 64, 8, 16
GA_HEADS, GA_KV_HEADS, GA_DH = 4, 2, 128
ROPE_THETA = 10000.0
ML_HEADS, ML_DK, ML_DV, ML_CHUNK = 4, 64, 128, 128

LANES = 128
SUBLANES = 8
VMEM_LIMIT_BYTES = 56 * 1024 * 1024

_O_A_QKV, _O_A_A, _O_A_B, _O_A_Z = 0, 1536, 1544, 1552
_O_B_QKV, _O_B_Z = 2064, 3600
_O_C_Q, _O_C_K, _O_C_V, _O_C_Z = 4112, 4624, 4880, 5136
_O_D_Q, _O_D_K, _O_D_V, _O_D_I, _O_D_F, _O_D_O, _O_D_Z = 5648, 5904, 6160, 6672, 6680, 6688, 7200
_O_GATE = 7712
_MAIN_SEGS = (
    ("a_qkv", _O_A_QKV, 1536), ("b_qkv", _O_B_QKV, 1536), ("a_z", _O_A_Z, 512), ("b_z", _O_B_Z, 512),
    ("gate", _O_GATE, 4096), ("c_q", _O_C_Q, 512), ("c_k", _O_C_K, 256), ("c_v", _O_C_V, 256),
    ("c_z", _O_C_Z, 512), ("d_q", _O_D_Q, 256), ("d_k", _O_D_K, 256), ("d_v", _O_D_V, 512),
    ("d_o", _O_D_O, 512), ("d_z", _O_D_Z, 512),
)
_OFF = {}
_o = 0
for _name, _src, _w in _MAIN_SEGS:
    _OFF[_name] = _o
    _o += _w
N_MAIN = _o
_SMALL_SRC = (_O_A_A, _O_A_B, _O_D_I, _O_D_F)
_L_AA, _L_AB, _L_DI, _L_DF = 0, 8, 16, 24

P_DTYPE = BF16
BLK = 128
NEG = -1e30


def _cparams(sem):
    return pltpu.CompilerParams(dimension_semantics=sem, vmem_limit_bytes=VMEM_LIMIT_BYTES)


def _sigmoid(x):
    return 0.5 * jnp.tanh(0.5 * x) + 0.5


def _silu(x):
    return x * _sigmoid(x)


def _softplus(x):
    return jnp.maximum(x, 0.0) + jnp.log(1.0 + jnp.exp(-jnp.abs(x)))


def _mm(a, b):
    return jnp.dot(a.astype(BF16), b.astype(BF16), preferred_element_type=F32)


def _mm_nt(a, b):
    return lax.dot_general(a.astype(BF16), b.astype(BF16), (((1,), (1,)), ((), ())),
                           preferred_element_type=F32)


def _mm_tn(a, b):
    return lax.dot_general(a.astype(BF16), b.astype(BF16), (((0,), (0,)), ((), ())),
                           preferred_element_type=F32)


def _mask_sum(mask, x):
    m = jnp.where(mask, 1.0, 0.0).astype(BF16)
    x1 = x.astype(BF16)
    r1 = x - x1.astype(F32)
    x2 = r1.astype(BF16)
    x3 = (r1 - x2.astype(F32)).astype(BF16)
    dot = lambda v: jnp.dot(m, v, preferred_element_type=F32)
    return dot(x1) + (dot(x2) + dot(x3))


def _chunk_masks(reverse, chunk):
    i = lax.broadcasted_iota(jnp.int32, (BLK, BLK), 0)
    j = lax.broadcasted_iota(jnp.int32, (BLK, BLK), 1)
    shift = int(math.log2(chunk))
    same = (i >> shift) == (j >> shift)
    if reverse:
        incl = same & (j >= i)
        strict = same & (j > i)
    else:
        incl = same & (j <= i)
        strict = same & (j < i)
    return same, incl, strict, (i == j)


def _chunk_totals(cum, chunk, reverse):
    pieces = []
    for c0 in range(0, BLK, chunk):
        r = c0 if reverse else c0 + chunk - 1
        pieces.append(jnp.broadcast_to(cum[r:r + 1, :], (chunk, cum.shape[1])))
    return jnp.concatenate(pieces, axis=0)


def _rows(start, size):
    if isinstance(start, int):
        return slice(start, start + size)
    return pl.ds(pl.multiple_of(start, size), size)


def _as_f32(mask):
    return jnp.where(mask, 1.0, 0.0).astype(F32)


def _inproj_kernel(x_ref, g_ref, w_ref, ws_ref, pm_ref, ps_ref, h_scr):
    @pl.when(pl.program_id(1) == 0)
    def _():
        x = x_ref[...]
        ms = jnp.mean(x * x, axis=-1, keepdims=True)
        h = (x * lax.rsqrt(ms + EPS) * g_ref[...]).astype(BF16)
        h_scr[...] = h
        ps_ref[...] = jnp.dot(h, ws_ref[...], preferred_element_type=F32)

    pm_ref[...] = jnp.dot(h_scr[...], w_ref[...], preferred_element_type=F32).astype(pm_ref.dtype)


def _inproj(x2d, g, w_main, w_small, layer, tm=1024, tn=N_MAIN // 4):
    m, d = x2d.shape
    tm = min(tm, m)
    return pl.pallas_call(
        _inproj_kernel,
        grid=(m // tm, N_MAIN // tn),
        in_specs=[
            pl.BlockSpec((tm, d), lambda i, j: (i, 0)),
            pl.BlockSpec((1, d), lambda i, j: (0, 0)),
            pl.BlockSpec((None, d, tn), lambda i, j: (layer, 0, j)),
            pl.BlockSpec((None, d, LANES), lambda i, j: (layer, 0, 0)),
        ],
        out_specs=[
            pl.BlockSpec((tm, tn), lambda i, j: (i, j)),
            pl.BlockSpec((tm, LANES), lambda i, j: (i, 0)),
        ],
        out_shape=[jax.ShapeDtypeStruct((m, N_MAIN), P_DTYPE), jax.ShapeDtypeStruct((m, LANES), F32)],
        scratch_shapes=[pltpu.VMEM((tm, d), BF16)],
        compiler_params=_cparams(("parallel", "arbitrary")),
        name="inproj",
    )(x2d, g, w_main, w_small)


GQA_TQ = 512


def _gqa_kernel(q_ref, k_ref, v_ref, z_ref, cos_ref, sin_ref, qg_ref, kg_ref, y_ref,
                q_scr, k_scr, v_scr, s0_scr, s1_scr, p0_scr, p1_scr, l0_scr, l1_scr):
    def norm_rope(x, g, cos, sin):
        ms = jnp.mean(x * x, axis=-1, keepdims=True)
        xn = x * lax.rsqrt(ms + EPS) * g
        lane = lax.broadcasted_iota(jnp.int32, xn.shape, 1)
        partner = jnp.where((lane & 63) < 32, pltpu.roll(xn, LANES - 32, 1), pltpu.roll(xn, 32, 1))
        return xn * cos + partner * sin

    s_len = k_ref.shape[0]
    group = GA_HEADS // GA_KV_HEADS
    assert group == 2
    scale = GA_DH ** -0.5
    n_blk = s_len // GQA_TQ
    s_bufs, p_bufs, l_bufs = (s0_scr, s1_scr), (p0_scr, p1_scr), (l0_scr, l1_scr)
    for kv in range(k_ref.shape[1] // GA_DH):
        _gqa_pipeline(kv, group, s_len, n_blk, scale, norm_rope, q_ref, k_ref, v_ref, z_ref, cos_ref, sin_ref,
                      qg_ref, kg_ref, y_ref, q_scr, k_scr, v_scr, s_bufs, p_bufs, l_bufs)


def _gqa_pipeline(kv, group, s_len, n_blk, scale, norm_rope, q_ref, k_ref, v_ref, z_ref, cos_ref, sin_ref,
                  qg_ref, kg_ref, y_ref, q_scr, k_scr, v_scr, s_bufs, p_bufs, l_bufs):
    kcols = slice(kv * GA_DH, (kv + 1) * GA_DH)
    k_scr[kv] = norm_rope(k_ref[:, kcols].astype(F32), kg_ref[...], cos_ref[...], sin_ref[...]).astype(BF16)
    v_scr[kv] = v_ref[:, kcols].astype(BF16)

    def stacked(head, blk):
        return _rows(head * s_len + blk * GQA_TQ, GQA_TQ)

    def head_cols(head):
        return slice((kv * group + head) * GA_DH, (kv * group + head + 1) * GA_DH)

    def prep(head, blk):
        rows = _rows(blk * GQA_TQ, GQA_TQ)
        q = norm_rope(q_ref[rows, head_cols(head)].astype(F32), qg_ref[...], cos_ref[rows, :], sin_ref[rows, :])
        q_scr[stacked(head, blk), :] = q.astype(BF16)

    def logits(head, blk):
        s_bufs[head][...] = _mm_nt(q_scr[stacked(head, blk), :], k_scr[kv]) * scale

    def softmax(slot):
        s = s_bufs[slot][...]
        p = jnp.exp(s - jnp.max(s, axis=-1, keepdims=True))
        l_bufs[slot][...] = jnp.broadcast_to(jnp.sum(p, axis=-1, keepdims=True), (GQA_TQ, GA_DH))
        p_bufs[slot][...] = p.astype(BF16)

    def weighted(head, blk):
        rows = _rows(blk * GQA_TQ, GQA_TQ)
        cols = head_cols(head)
        o = _mm(p_bufs[head][...], v_scr[kv]) / l_bufs[head][...]
        y_ref[rows, cols] = (o * _silu(z_ref[rows, cols].astype(F32))).astype(y_ref.dtype)

    prep(0, 0)
    prep(1, 0)
    prep(0, 1)
    logits(0, 0)
    logits(1, 0)
    softmax(0)

    def body(j, carry):
        logits(0, j + 1)
        softmax(1)
        weighted(0, j)
        prep(1, j + 1)
        logits(1, j + 1)
        softmax(0)
        weighted(1, j)
        prep(0, jnp.minimum(j + 2, n_blk - 1))
        return carry

    lax.fori_loop(0, n_blk - 1, body, 0)
    softmax(1)
    weighted(0, n_blk - 1)
    weighted(1, n_blk - 1)


def _gqa(pm3, cos_t, sin_t, qg, kg):
    b, s, _ = pm3.shape
    qw = GA_HEADS * GA_DH
    kw = GA_KV_HEADS * GA_DH
    group = GA_HEADS // GA_KV_HEADS
    return pl.pallas_call(
        _gqa_kernel,
        grid=(b,),
        in_specs=[
            pl.BlockSpec((None, s, qw), lambda i: (i, 0, _OFF["c_q"] // qw)),
            pl.BlockSpec((None, s, kw), lambda i: (i, 0, _OFF["c_k"] // kw)),
            pl.BlockSpec((None, s, kw), lambda i: (i, 0, _OFF["c_v"] // kw)),
            pl.BlockSpec((None, s, qw), lambda i: (i, 0, _OFF["c_z"] // qw)),
            pl.BlockSpec((s, GA_DH), lambda i: (0, 0)),
            pl.BlockSpec((s, GA_DH), lambda i: (0, 0)),
            pl.BlockSpec((1, GA_DH), lambda i: (0, 0)),
            pl.BlockSpec((1, GA_DH), lambda i: (0, 0)),
        ],
        out_specs=pl.BlockSpec((None, s, qw), lambda i: (i, 0, 0)),
        out_shape=jax.ShapeDtypeStruct((b, s, BRANCH_W), BF16),
        scratch_shapes=[pltpu.VMEM((group * s, GA_DH), BF16), pltpu.VMEM((GA_KV_HEADS, s, GA_DH), BF16),
                        pltpu.VMEM((GA_KV_HEADS, s, GA_DH), BF16),
                        pltpu.VMEM((GQA_TQ, s), F32), pltpu.VMEM((GQA_TQ, s), F32),
                        pltpu.VMEM((GQA_TQ, s), BF16), pltpu.VMEM((GQA_TQ, s), BF16),
                        pltpu.VMEM((GQA_TQ, GA_DH), F32), pltpu.VMEM((GQA_TQ, GA_DH), F32)],
        compiler_params=_cparams(("parallel",)),
        name="gqa",
    )(pm3, pm3, pm3, pm3, cos_t, sin_t, qg, kg)


NA_ROW_UNROLL = 16


def _na_kernel(q_ref, k_ref, v_ref, z_ref, bias_ref, qg_ref, kg_ref, y_ref, q_scr, k_scr):
    s_len = q_ref.shape[0]
    rows = s_len // GRID_W
    kr = min(NA_ROWS, rows)
    hi = lax.broadcasted_iota(jnp.int32, (2 * NA_DH, 2 * NA_DH), 0) >= NA_DH
    hj = lax.broadcasted_iota(jnp.int32, (2 * NA_DH, 2 * NA_DH), 1) >= NA_DH
    same_head = jnp.where(hi == hj, 1.0, 0.0).astype(BF16)

    def rms_pair(x, g):
        x2 = x * x
        x2_hi = x2.astype(BF16)
        x2_lo = (x2 - x2_hi.astype(F32)).astype(BF16)
        ssq = (jnp.dot(x2_hi, same_head, preferred_element_type=F32)
               + jnp.dot(x2_lo, same_head, preferred_element_type=F32))
        return x * lax.rsqrt(ssq * (1.0 / NA_DH) + EPS) * g

    scale = NA_DH ** -0.5
    assert math.log2(scale).is_integer()
    nkeys = kr * GRID_W
    first = lax.broadcasted_iota(jnp.int32, (s_len, 2 * NA_DH), 1) < NA_DH
    first_q = lax.broadcasted_iota(jnp.int32, (GRID_W, 2 * NA_DH), 1) < NA_DH
    for pair in range(q_ref.shape[1] // (2 * NA_DH)):
        pc = slice(pair * 2 * NA_DH, (pair + 1) * 2 * NA_DH)
        qn = rms_pair(q_ref[:, pc].astype(F32), qg_ref[...]) * scale
        q_scr[0] = jnp.where(first, qn, 0.0).astype(BF16)
        q_scr[1] = jnp.where(first, 0.0, qn).astype(BF16)
        k_scr[...] = rms_pair(k_ref[:, pc].astype(F32), kg_ref[...]).astype(BF16)

        def body(it, carry, pair=pair, pc=pc):
            units = []
            for u in range(NA_ROW_UNROLL):
                r = it * NA_ROW_UNROLL + u
                r0 = jnp.clip(r - kr // 2, 0, rows - kr)
                var = r0 - r + (NA_ROWS - 1)
                qrows = pl.ds(pl.multiple_of(r * GRID_W, GRID_W), GRID_W)
                krows = pl.ds(pl.multiple_of(r0 * GRID_W, GRID_W), nkeys)
                units.append((var, qrows, krows))
            logits = [_mm_nt(jnp.concatenate([q_scr[0, qrows, :], q_scr[1, qrows, :]], axis=0), k_scr[krows, :])
                      + bias_ref[pair, var] for var, qrows, krows in units]
            probs = [jnp.exp(s - jnp.max(s, axis=-1, keepdims=True)) for s in logits]
            sums = [jnp.sum(p, axis=-1, keepdims=True) for p in probs]
            outs = [_mm(p, v_ref[krows, pc]) / l for p, l, (_, _, krows) in zip(probs, sums, units)]
            for o, (_, qrows, _) in zip(outs, units):
                o_pair = jnp.where(first_q, o[:GRID_W], o[GRID_W:])
                y_ref[qrows, pc] = (o_pair * _silu(z_ref[qrows, pc].astype(F32))).astype(y_ref.dtype)
            return carry

        lax.fori_loop(0, rows // NA_ROW_UNROLL, body, 0)


def _na_bias_table(rpb, rows):
    kr = min(NA_ROWS, rows)
    c = jnp.arange(GRID_W)
    c0 = jnp.clip(c - NA_COLS // 2, 0, GRID_W - NA_COLS)
    in_win = (c[None, :] >= c0[:, None]) & (c[None, :] < c0[:, None] + NA_COLS)
    col_off = jnp.clip(c[None, :] - c[:, None], -(NA_COLS - 1), NA_COLS - 1) + NA_COLS - 1
    t = jnp.where(in_win, rpb[..., col_off], NEG)
    ro = jnp.arange(NA_ROWS)[:, None] + jnp.arange(kr)[None, :]
    tv = jnp.take(t, ro, axis=-3)
    tv = jnp.swapaxes(tv, -3, -2).reshape(*rpb.shape[:-2], NA_ROWS, GRID_W, kr * GRID_W)
    lead = rpb.shape[:-3]
    tv = tv.reshape(*lead, rpb.shape[-3] // 2, 2, NA_ROWS, GRID_W, kr * GRID_W)
    tv = jnp.swapaxes(tv, -4, -3)
    return tv.reshape(*lead, rpb.shape[-3] // 2, NA_ROWS, 2 * GRID_W, kr * GRID_W).astype(F32)


def _natten(pm3, bias, qg, kg, layer):
    b, s, _ = pm3.shape
    pw = 2 * NA_DH
    hw = NA_HEADS * NA_DH
    return pl.pallas_call(
        _na_kernel,
        grid=(b,),
        in_specs=[
            pl.BlockSpec((None, s, hw), lambda i: (i, 0, _OFF["b_qkv"] // hw)),
            pl.BlockSpec((None, s, hw), lambda i: (i, 0, _OFF["b_qkv"] // hw + 1)),
            pl.BlockSpec((None, s, hw), lambda i: (i, 0, _OFF["b_qkv"] // hw + 2)),
            pl.BlockSpec((None, s, hw), lambda i: (i, 0, _OFF["b_z"] // hw)),
            pl.BlockSpec((None,) + bias.shape[1:], lambda i: (layer, 0, 0, 0, 0)),
            pl.BlockSpec((1, pw), lambda i: (0, 0)),
            pl.BlockSpec((1, pw), lambda i: (0, 0)),
        ],
        out_specs=pl.BlockSpec((None, s, hw), lambda i: (i, 0, 0)),
        out_shape=jax.ShapeDtypeStruct((b, s, BRANCH_W), BF16),
        scratch_shapes=[pltpu.VMEM((2, s, pw), BF16), pltpu.VMEM((s, pw), BF16)],
        compiler_params=_cparams(("parallel",)),
        name="natten",
    )(pm3, pm3, pm3, pm3, bias, qg, kg)


INV_BASE = 8
PRE_BLK = 4 * BLK


def _inverse_level_masks():
    i = lax.broadcasted_iota(jnp.int32, (BLK, BLK), 0)
    j = lax.broadcasted_iota(jnp.int32, (BLK, BLK), 1)
    same = lambda size: (i >> int(math.log2(size))) == (j >> int(math.log2(size)))
    base = same(INV_BASE)
    joins = []
    size = INV_BASE
    while size < DN_CHUNK:
        joins.append(same(2 * size) & jnp.logical_not(same(size)))
        size *= 2
    return base, joins


def _tri_inverses(l_mats, eyes, level_masks):
    base, joins = level_masks
    ps = [jnp.where(base, -l, 0.0) for l in l_mats]
    ts = [eye + p for eye, p in zip(eyes, ps)]
    for _ in range(int(math.log2(INV_BASE)) - 1):
        ps = [_mm(p, p) for p in ps]
        ts = [t + _mm(t, p) for t, p in zip(ts, ps)]
    for join in joins:
        mids = [_mm(jnp.where(join, l, 0.0), t) for l, t in zip(l_mats, ts)]
        ts = [t - _mm(t, mid) for t, mid in zip(ts, mids)]
    return ts


DN_HW = DN_HEADS * LANES
PK_W, PK_QD, PK_KD, PK_QK = (i * DN_HW for i in range(4))


def _dn_pre_kernel(qkv_ref, sm_ref, cw_ref, lp_ref, uf_ref, ub_ref, pkf_ref, pkb_ref, gtf_ref, gtb_ref):
    n = pl.program_id(1)
    s_len = qkv_ref.shape[0]
    assert qkv_ref.dtype == BF16
    halo = 2 * SUBLANES
    n_sub = sm_ref.shape[0] // BLK
    chunks_per_blk = BLK // DN_CHUNK
    out_row = lax.broadcasted_iota(jnp.int32, (BLK, BLK + 2 * halo), 0)
    in_row = lax.broadcasted_iota(jnp.int32, (BLK, BLK + 2 * halo), 1)
    shifts = {j: jnp.where(in_row == out_row + (halo + j - DN_CONV // 2), 1.0, 0.0).astype(BF16)
              for j in range(DN_CONV) if j != DN_CONV // 2}
    dir_masks = [_chunk_masks(d == 1, DN_CHUNK) for d in range(2)]
    level_masks = _inverse_level_masks()
    outs = ((uf_ref, pkf_ref, gtf_ref), (ub_ref, pkb_ref, gtb_ref))

    def l2n(x):
        return x * lax.rsqrt(jnp.sum(x * x, axis=-1, keepdims=True) + EPS)

    blocks = []
    for sub in range(n_sub):
        t0 = pl.multiple_of((n * n_sub + sub) * BLK, BLK)
        pstart = pl.multiple_of(jnp.maximum(t0 - halo, 0), halo)
        nstart = pl.multiple_of(jnp.minimum(t0 + BLK, s_len - halo), halo)
        prev = qkv_ref[pl.ds(pstart, halo), :]
        prev = jnp.where(t0 > 0, prev, jnp.zeros_like(prev))
        cur = qkv_ref[pl.ds(t0, BLK), :]
        nxt = qkv_ref[pl.ds(nstart, halo), :]
        nxt = jnp.where(t0 + BLK < s_len, nxt, jnp.zeros_like(nxt))
        xw = jnp.concatenate([prev, cur, nxt], axis=0)
        conv = cur.astype(F32) * cw_ref[DN_CONV // 2:DN_CONV // 2 + 1, :]
        for j, shift in shifts.items():
            conv = conv + jnp.dot(shift, xw, preferred_element_type=F32) * cw_ref[j:j + 1, :]
        conv = _silu(conv)

        rows = slice(sub * BLK, (sub + 1) * BLK)
        sm = sm_ref[rows, :]
        g_all = -jnp.exp(lp_ref[0:1, :]) * _softplus(sm + lp_ref[1:2, :])
        beta_all = _sigmoid(sm)
        per_dir = []
        for d in range(2):
            same, incl, strict, eye_b = dir_masks[d]
            gc = _mask_sum(incl, g_all)
            tot = _mask_sum(same, g_all)
            per_dir.append((incl, strict, _as_f32(eye_b), gc, gc.T, tot))
        blocks.append((sub, rows, conv, beta_all, per_dir))

    heads = []
    for sub, rows, conv, beta_all, per_dir in blocks:
        for h in range(DN_HEADS):
            q = l2n(conv[:, h * DN_DK:(h + 1) * DN_DK]) * (DN_DK ** -0.5)
            k = l2n(conv[:, (DN_HEADS + h) * DN_DK:(DN_HEADS + h + 1) * DN_DK])
            v = conv[:, 2 * DN_HEADS * DN_DK + h * DN_DV:2 * DN_HEADS * DN_DK + (h + 1) * DN_DV]
            heads.append((sub, rows, beta_all, per_dir, h, q, k, v))
    kks = [_mm_nt(hd[6], hd[6]) for hd in heads]
    qks = [_mm_nt(hd[5], hd[6]) for hd in heads]

    l_mats, rhss, eyes, slots = [], [], [], []
    for (sub, rows, beta_all, per_dir, h, q, k, v), kk, qk in zip(heads, kks, qks):
        hc = slice(h * LANES, (h + 1) * LANES)
        for d in range(2):
            incl, strict, eye_f, gc, gct, tot = per_dir[d]
            u_ref, pk_ref, gt_ref = outs[d]
            pk = lambda off: slice(off + h * LANES, off + (h + 1) * LANES)
            c = _L_AA + d * DN_HEADS + h
            gcol = gc[:, c:c + 1]
            grow = gct[c:c + 1, :]
            tcol = tot[:, c:c + 1]
            beta = beta_all[:, _L_AB + d * DN_HEADS + h:_L_AB + d * DN_HEADS + h + 1]
            decay = jnp.where(incl, jnp.exp(jnp.where(incl, gcol - grow, 0.0)), 0.0)
            l_mats.append(jnp.where(strict, beta * kk * decay, 0.0))
            egc = jnp.exp(gcol)
            rhss.append(jnp.concatenate([v * beta, k * (beta * egc)], axis=-1).astype(BF16))
            eyes.append(eye_f)
            slots.append((rows, h, d))
            pk_ref[rows, pk(PK_QD)] = (q * egc).astype(pk_ref.dtype)
            pk_ref[rows, pk(PK_KD)] = (k * jnp.exp(tcol - gcol)).astype(pk_ref.dtype)
            pk_ref[rows, pk(PK_QK)] = (qk * decay).astype(pk_ref.dtype)
            gtot = jnp.exp(tcol)
            for ci in range(chunks_per_blk):
                g0 = (sub * chunks_per_blk + ci) * SUBLANES
                gt_ref[g0:g0 + SUBLANES, hc] = jnp.broadcast_to(
                    gtot[ci * DN_CHUNK:ci * DN_CHUNK + SUBLANES, :], (SUBLANES, LANES))

    t_invs = _tri_inverses(l_mats, eyes, level_masks)
    sols = [_mm(t, rhs) for t, rhs in zip(t_invs, rhss)]
    for (rows, h, d), sol in zip(slots, sols):
        u_ref, pk_ref = outs[d][0], outs[d][1]
        u_ref[rows, h * LANES:(h + 1) * LANES] = sol[:, :DN_DV]
        pk_ref[rows, PK_W + h * LANES:PK_W + (h + 1) * LANES] = sol[:, DN_DV:].astype(pk_ref.dtype)


def _dn_pre(pm3, ps3, conv_w8, lane_params):
    b, s, _ = pm3.shape
    nblk = s // PRE_BLK
    gt_rows = PRE_BLK // DN_CHUNK * SUBLANES
    wq = DN_HEADS * (2 * DN_DK + DN_DV)
    hw = DN_HEADS * LANES
    tok = lambda i, j: (i, j, 0)
    big = lambda dt: jax.ShapeDtypeStruct((b, s, hw), dt)
    gts = jax.ShapeDtypeStruct((b, nblk * gt_rows, hw), F32)
    bs_tok = pl.BlockSpec((None, PRE_BLK, hw), tok)
    bs_gt = pl.BlockSpec((None, gt_rows, hw), tok)
    bs_pk = pl.BlockSpec((None, PRE_BLK, 4 * hw), tok)
    packed = jax.ShapeDtypeStruct((b, s, 4 * hw), BF16)
    return pl.pallas_call(
        _dn_pre_kernel,
        grid=(b, nblk),
        in_specs=[
            pl.BlockSpec((None, s, wq), lambda i, j: (i, 0, _OFF["a_qkv"] // wq)),
            pl.BlockSpec((None, PRE_BLK, LANES), tok),
            pl.BlockSpec((SUBLANES, wq), lambda i, j: (0, 0)),
            pl.BlockSpec((SUBLANES, LANES), lambda i, j: (0, 0)),
        ],
        out_specs=[bs_tok, bs_tok, bs_pk, bs_pk, bs_gt, bs_gt],
        out_shape=[big(F32), big(F32), packed, packed, gts, gts],
        compiler_params=_cparams(("parallel", "arbitrary")),
        name="dn_pre",
    )(pm3, ps3, conv_w8, lane_params)


SCAN_BLK = 4 * BLK


def _zero_at_sequence_start(*scratch):
    @pl.when(pl.program_id(1) == 0)
    def _():
        for ref in scratch:
            ref[...] = jnp.zeros_like(ref)


def _dn_scan_body(uf_ref, pkf_ref, gtf_ref, ub_ref, pkb_ref, gtb_ref, of_ref, ob_ref, st_scr):
    per_blk = BLK // DN_CHUNK
    nchunk = SCAN_BLK // DN_CHUNK
    zeros_c = jnp.zeros((DN_CHUNK, DN_DV), F32)
    streams = ((uf_ref, pkf_ref, gtf_ref, of_ref, range(nchunk)),
               (ub_ref, pkb_ref, gtb_ref, ob_ref, range(nchunk - 1, -1, -1)))
    chains = [(d, h) + streams[d] for d in range(2) for h in range(DN_HEADS)]
    states = [st_scr[d * DN_HEADS + h] for d, h, *_ in chains]
    for step in range(nchunk):
        rs, v_pads = [], []
        for (d, h, u_ref, pk_ref, gt_ref, o_ref, order), state in zip(chains, states):
            rows = slice(order[step] * DN_CHUNK, (order[step] + 1) * DN_CHUNK)
            w = pk_ref[rows, PK_W + h * LANES:PK_W + (h + 1) * LANES]
            qd = pk_ref[rows, PK_QD + h * LANES:PK_QD + (h + 1) * LANES]
            rs.append(_mm(jnp.concatenate([w, qd], axis=0), state))
        for (d, h, u_ref, pk_ref, gt_ref, o_ref, order), r in zip(chains, rs):
            ci = order[step]
            rows = slice(ci * DN_CHUNK, (ci + 1) * DN_CHUNK)
            parts = [zeros_c] * per_blk
            parts[ci % per_blk] = u_ref[rows, h * LANES:(h + 1) * LANES] - r[:DN_CHUNK]
            v_pads.append(jnp.concatenate(parts, axis=0))
        new_states = []
        for (d, h, u_ref, pk_ref, gt_ref, o_ref, order), r, v_pad, state in zip(chains, rs, v_pads, states):
            hc = slice(h * LANES, (h + 1) * LANES)
            ci = order[step]
            rows = slice(ci * DN_CHUNK, (ci + 1) * DN_CHUNK)
            blk_rows = slice((ci // per_blk) * BLK, (ci // per_blk + 1) * BLK)
            o_ref[rows, hc] = r[DN_CHUNK:] + _mm(pk_ref[rows, PK_QK + h * LANES:PK_QK + (h + 1) * LANES], v_pad)
            gt = gt_ref[ci * SUBLANES:ci * SUBLANES + 1, hc]
            kd = pk_ref[blk_rows, PK_KD + h * LANES:PK_KD + (h + 1) * LANES]
            new_states.append(state * gt + _mm_tn(kd, v_pad))
        states = new_states
    for (d, h, *_), state in zip(chains, states):
        st_scr[d * DN_HEADS + h] = state


def _dn_scan_parts(pre):
    uf, ub, pkf, pkb, gtf, gtb = pre
    b, s, hw = uf.shape
    nblk = s // SCAN_BLK
    gt_rows = SCAN_BLK // DN_CHUNK * SUBLANES
    fwd = lambda i, j: (i, j, 0)
    bwd = lambda i, j: (i, nblk - 1 - j, 0)
    def specs(imap):
        return [pl.BlockSpec((None, SCAN_BLK, hw), imap), pl.BlockSpec((None, SCAN_BLK, 4 * hw), imap),
                pl.BlockSpec((None, gt_rows, hw), imap)]
    return dict(
        grid=(b, nblk), args=(uf, pkf, gtf, ub, pkb, gtb), in_specs=specs(fwd) + specs(bwd),
        out_specs=[pl.BlockSpec((None, SCAN_BLK, hw), fwd), pl.BlockSpec((None, SCAN_BLK, hw), bwd)],
        out_shape=[jax.ShapeDtypeStruct((b, s, hw), F32)] * 2,
        scratch_shapes=[pltpu.VMEM((2 * DN_HEADS, DN_DK, DN_DV), F32)])


def _call_parts(kernel_fn, name, parts):
    return pl.pallas_call(
        kernel_fn, grid=parts["grid"], in_specs=parts["in_specs"], out_specs=parts["out_specs"],
        out_shape=parts["out_shape"], scratch_shapes=parts["scratch_shapes"],
        compiler_params=_cparams(("parallel", "arbitrary")), name=name)(*parts["args"])


def _scan_and_mlstm(pre, pm3, ps3, ml_lane_params):
    dn, ml = _dn_scan_parts(pre), _mlstm_parts(pm3, ps3, ml_lane_params)
    assert dn["grid"] == ml["grid"]
    n_in = (len(dn["args"]), len(ml["args"]))
    n_out = (len(dn["out_shape"]), len(ml["out_shape"]))

    def both(*refs):
        ins, rest = refs[:sum(n_in)], refs[sum(n_in):]
        outs, scr = rest[:sum(n_out)], rest[sum(n_out):]
        n_dn_scr = len(dn["scratch_shapes"])
        _zero_at_sequence_start(*scr)
        _dn_scan_body(*ins[:n_in[0]], *outs[:n_out[0]], *scr[:n_dn_scr])
        _ml_body(*ins[n_in[0]:], *outs[n_out[0]:], *scr[n_dn_scr:])

    fused = {k: dn[k] + ml[k] for k in ("args", "in_specs", "out_specs", "out_shape", "scratch_shapes")}
    fused["grid"] = dn["grid"]
    return _call_parts(both, "scan_mlstm", fused)


ML_AUG = 2 * LANES
ML_GROUP = 4
ML_BLK = 4 * BLK


def _ml_body(qf_ref, kf_ref, vf_ref, smf_ref, qb_ref, kb_ref, vb_ref, smb_ref, lp_ref,
             hf_ref, hb_ref, c_scr, m_scr):
    nchunk = BLK // ML_CHUNK
    ones_col = jnp.ones((BLK, LANES), BF16)
    zeros_aug = jnp.zeros((ML_CHUNK, ML_AUG), BF16)
    streams = ((qf_ref, kf_ref, vf_ref, smf_ref, hf_ref, range(nchunk)),
               (qb_ref, kb_ref, vb_ref, smb_ref, hb_ref, range(nchunk - 1, -1, -1)))
    lanes = lambda col: jnp.broadcast_to(col, (col.shape[0], LANES))
    n_sub = smf_ref.shape[0] // BLK
    units = [(d, p if d == 0 else n_sub - 1 - p) for p in range(n_sub) for d in range(2)]
    for d, sub in units:
        _ml_block(d, sub, streams[d], lp_ref, c_scr, m_scr, lanes, ones_col, zeros_aug, nchunk)


def _ml_block(d, sub, stream, lp_ref, c_scr, m_scr, lanes, ones_col, zeros_aug, nchunk):
    q_ref, k_ref, v_ref, sm_ref, h_ref, order = stream
    blk_rows = slice(sub * BLK, (sub + 1) * BLK)
    same, incl, _, _ = _chunk_masks(d == 1, ML_CHUNK)
    sm = sm_ref[blk_rows, :]
    ig_all = sm + lp_ref[0:1, :]
    x = sm + lp_ref[1:2, :]
    lf_all = jnp.minimum(x, 0.0) - jnp.log(1.0 + jnp.exp(-jnp.abs(x)))
    lf_all = pltpu.roll(lf_all, LANES - (_L_DF - _L_DI), 1)
    bc_all = _mask_sum(incl, lf_all)
    tot_all = _chunk_totals(bc_all, ML_CHUNK, reverse=(d == 1))
    a_all = ig_all - bc_all
    mwa_all = jnp.concatenate(
        [jnp.broadcast_to(jnp.max(a_all[ci * ML_CHUNK:(ci + 1) * ML_CHUNK], axis=0, keepdims=True),
                          (ML_CHUNK, LANES)) for ci in range(nchunk)], axis=0)
    a_t, w_all, mw_all = a_all.T, jnp.exp(a_all - mwa_all), tot_all + mwa_all

    for chains in [[(d, h) for h in range(g, g + ML_GROUP)] for g in range(0, ML_HEADS, ML_GROUP)]:
        ins = []
        for _, h in chains:
            q = q_ref[blk_rows, h * ML_DK:(h + 1) * ML_DK].astype(BF16)
            k = k_ref[blk_rows, h * ML_DK:(h + 1) * ML_DK].astype(F32) * (ML_DK ** -0.5)
            v_aug = jnp.concatenate([v_ref[blk_rows, h * ML_DV:(h + 1) * ML_DV].astype(BF16), ones_col], axis=-1)
            ins.append((q, k, v_aug))
        qks = [_mm_nt(q, k) for q, k, _ in ins]

        mids = []
        for (d, h), (q, k, v_aug), qk in zip(chains, ins, qks):
            c = _L_DI + d * ML_HEADS + h
            b_l = lanes(bc_all[:, c:c + 1])
            dlog = jnp.where(incl, b_l + a_t[c:c + 1, :], NEG)
            m_intra = lanes(jnp.max(dlog, axis=-1, keepdims=True))
            s_intra = qk * jnp.exp(dlog - m_intra)
            wk = (k * lanes(w_all[:, c:c + 1])[:, :ML_DK]).astype(BF16)
            mids.append((b_l, m_intra, s_intra.astype(BF16), wk))
        p_intras = [_mm(s_b, v_aug) for (_, _, s_b, _), (_, _, v_aug) in zip(mids, ins)]
        kvs = []
        for (_, _, _, wk), (_, _, v_aug) in zip(mids, ins):
            per_chunk = []
            for ci in range(nchunk):
                parts = [zeros_aug] * nchunk
                parts[ci] = v_aug[ci * ML_CHUNK:(ci + 1) * ML_CHUNK]
                per_chunk.append(_mm_tn(wk, jnp.concatenate(parts, axis=0)))
            kvs.append(per_chunk)

        c_sts = [c_scr[d * ML_HEADS + h] for d, h in chains]
        m_sts = [m_scr[d * ML_HEADS + h][0:1, :] for d, h in chains]
        for step in range(nchunk):
            qcs = []
            ci = order[step]
            for (q, _, _), c_st in zip(ins, c_sts):
                qcs.append(_mm(q[ci * ML_CHUNK:(ci + 1) * ML_CHUNK], c_st))
            for idx, (_, h) in enumerate(chains):
                b_l, m_intra, _, _ = mids[idx]
                c = _L_DI + d * ML_HEADS + h
                rows = slice(ci * ML_CHUNK, (ci + 1) * ML_CHUNK)
                out_rows = slice(sub * BLK + ci * ML_CHUNK, sub * BLK + (ci + 1) * ML_CHUNK)
                r8 = slice(ci * ML_CHUNK, ci * ML_CHUNK + SUBLANES)
                m_st, c_st, qc = m_sts[idx], c_sts[idx], qcs[idx]
                m_inter = b_l[rows] + m_st
                m_i = jnp.maximum(m_intra[rows], m_inter)
                f_i = jnp.exp(m_intra[rows] - m_i)
                inter = jnp.exp(m_inter - m_i)
                both = (jnp.concatenate([inter, inter], axis=-1) * qc
                        + jnp.concatenate([f_i, f_i], axis=-1) * p_intras[idx][rows])
                numer, denom = both[:, :ML_DV], both[:, ML_DV:]
                h_ref[out_rows, h * ML_DV:(h + 1) * ML_DV] = numer / jnp.maximum(jnp.abs(denom), jnp.exp(-m_i))
                tot_c = lanes(tot_all[r8, c:c + 1])[0:1]
                mw_c = lanes(mw_all[r8, c:c + 1])[0:1]
                m_new = jnp.maximum(tot_c + m_st, mw_c)
                dec = jnp.exp(tot_c + m_st - m_new)
                gain = jnp.exp(mw_c - m_new)
                c_sts[idx] = (jnp.concatenate([dec, dec], axis=-1) * c_st
                              + jnp.concatenate([gain, gain], axis=-1) * kvs[idx][ci])
                m_sts[idx] = m_new
        for idx, (d, h) in enumerate(chains):
            c_scr[d * ML_HEADS + h] = c_sts[idx]
            m_scr[d * ML_HEADS + h] = jnp.broadcast_to(m_sts[idx], (SUBLANES, LANES))


def _mlstm_parts(pm3, ps3, lane_params):
    b, s, _ = pm3.shape
    nblk = s // ML_BLK
    qw = ML_HEADS * ML_DK
    vw = ML_HEADS * ML_DV
    def specs(tmap):
        blk = lambda j: tmap(j)
        return [
            pl.BlockSpec((None, ML_BLK, qw), lambda i, j: (i, blk(j), _OFF["d_q"] // qw)),
            pl.BlockSpec((None, ML_BLK, qw), lambda i, j: (i, blk(j), _OFF["d_k"] // qw)),
            pl.BlockSpec((None, ML_BLK, vw), lambda i, j: (i, blk(j), _OFF["d_v"] // vw)),
            pl.BlockSpec((None, ML_BLK, LANES), lambda i, j: (i, blk(j), 0)),
        ]
    fwd = lambda j: j
    bwd = lambda j: nblk - 1 - j
    return dict(
        grid=(b, nblk), args=(pm3, pm3, pm3, ps3, pm3, pm3, pm3, ps3, lane_params),
        in_specs=specs(fwd) + specs(bwd) + [pl.BlockSpec((SUBLANES, LANES), lambda i, j: (0, 0))],
        out_specs=[pl.BlockSpec((None, ML_BLK, vw), lambda i, j: (i, j, 0)),
                   pl.BlockSpec((None, ML_BLK, vw), lambda i, j: (i, nblk - 1 - j, 0))],
        out_shape=[jax.ShapeDtypeStruct((b, s, vw), F32)] * 2,
        scratch_shapes=[pltpu.VMEM((2 * ML_HEADS, ML_DK, ML_AUG), F32),
                        pltpu.VMEM((2 * ML_HEADS, SUBLANES, LANES), F32)])


def _merge_kernel(x_ref, af_ref, ab_ref, df_ref, db_ref, yb_ref, yc_ref, az_ref, dz_ref, do_ref, gl_ref,
                  ag_ref, dg_ref, wb_ref, wo_ref, o_ref):
    d = x_ref.shape[-1]

    def head_rms(x, g):
        outs = []
        for h in range(x.shape[-1] // LANES):
            xh = x[:, h * LANES:(h + 1) * LANES]
            ms = jnp.mean(xh * xh, axis=-1, keepdims=True)
            outs.append(xh * lax.rsqrt(ms + EPS) * g)
        return jnp.concatenate(outs, axis=-1)

    ya = head_rms(af_ref[...] + ab_ref[...], ag_ref[...]) * _silu(az_ref[...].astype(F32))
    yd = _sigmoid(do_ref[...].astype(F32)) * head_rms(df_ref[...] + db_ref[...], dg_ref[...])
    yd = yd * _silu(dz_ref[...].astype(F32))
    twice = None
    for i, y in enumerate((ya.astype(BF16), yb_ref[...], yc_ref[...], yd.astype(BF16))):
        proj = jnp.dot(y, wb_ref[i], preferred_element_type=F32)
        term = proj + jnp.tanh(0.5 * gl_ref[:, i * d:(i + 1) * d].astype(F32)) * proj
        twice = term if twice is None else twice + term
    merged = (0.5 * twice).astype(BF16)
    o_ref[...] = x_ref[...] + jnp.dot(merged, wo_ref[...], preferred_element_type=F32)


def _merge(x2d, af, ab, df, db, yb, yc, pm2, ag, dg, wb, wo, layer, tm=512):
    m, d = x2d.shape
    gw = N_BRANCH * d
    w = BRANCH_W
    tok = pl.BlockSpec((tm, w), lambda i: (i, 0))
    col = lambda name: pl.BlockSpec((tm, w), lambda i: (i, _OFF[name] // w))
    vec = pl.BlockSpec((1, LANES), lambda i: (0, 0))
    return pl.pallas_call(
        _merge_kernel,
        grid=(m // tm,),
        in_specs=[
            pl.BlockSpec((tm, d), lambda i: (i, 0)),
            tok, tok, tok, tok, tok, tok,
            col("a_z"), col("d_z"), col("d_o"),
            pl.BlockSpec((tm, gw), lambda i: (i, _OFF["gate"] // gw)),
            vec, vec,
            pl.BlockSpec((None, N_BRANCH, w, d), lambda i: (layer, 0, 0, 0)),
            pl.BlockSpec((None, d, d), lambda i: (layer, 0, 0)),
        ],
        out_specs=pl.BlockSpec((tm, d), lambda i: (i, 0)),
        out_shape=jax.ShapeDtypeStruct((m, d), F32),
        compiler_params=_cparams(("parallel",)),
        name="merge",
    )(x2d, af, ab, df, db, yb, yc, pm2, pm2, pm2, pm2, ag, dg, wb, wo)


def _rope_lane_tables(s):
    t = jnp.arange(s)
    row = (t // GRID_W).astype(F32)
    col = (t % GRID_W).astype(F32)
    m = GA_DH // 4
    inv = ROPE_THETA ** (-jnp.arange(m, dtype=F32) / m)
    ar = row[:, None] * inv
    ac = col[:, None] * inv
    cos_t = jnp.concatenate([jnp.cos(ar), jnp.cos(ar), jnp.cos(ac), jnp.cos(ac)], axis=-1)
    sin_t = jnp.concatenate([-jnp.sin(ar), jnp.sin(ar), -jnp.sin(ac), jnp.sin(ac)], axis=-1)
    return cos_t.astype(F32), sin_t.astype(F32)


def _main_columns(w):
    return jnp.concatenate([w[..., o:o + wd] for _, o, wd in _MAIN_SEGS], axis=-1)


def _lane_tiles(rows):
    padded = []
    for off, vals in rows:
        vals = vals.reshape(vals.shape[0], 1, -1).astype(F32)
        padded.append(jnp.pad(vals, ((0, 0), (0, 0), (off, LANES - off - vals.shape[-1]))))
    tiles = jnp.concatenate(padded, axis=1)
    return jnp.pad(tiles, ((0, 0), (0, SUBLANES - len(rows)), (0, 0)))


def kernel(x, norm_g, w_in, conv_a, dn_a_log, dn_dt_bias, dn_norm_g, na_q_norm, na_k_norm, na_rpb,
           ga_q_norm, ga_k_norm, ml_i_bias, ml_f_bias, ml_norm_g, w_branch, w_out):
    b, s, d = x.shape
    depth = w_in.shape[0]
    hw = BRANCH_W
    cos_t, sin_t = _rope_lane_tables(s)
    w_main = _main_columns(w_in).astype(BF16)
    w_small = jnp.pad(jnp.concatenate([w_in[:, :, o:o + 8] for o in _SMALL_SRC], axis=2),
                      ((0, 0), (0, 0), (0, LANES - 8 * len(_SMALL_SRC)))).astype(BF16)
    conv8 = jnp.pad(conv_a.astype(F32), ((0, 0), (0, SUBLANES - DN_CONV), (0, 0)))
    dn_lp = _lane_tiles([(_L_AA, dn_a_log), (_L_AA, dn_dt_bias)])
    ml_lp = _lane_tiles([(_L_DI, ml_i_bias), (_L_DF, ml_f_bias)])
    na_bias = _na_bias_table(na_rpb, s // GRID_W)
    na_qg = jnp.tile(na_q_norm, (1, 2)).reshape(depth, 1, 2 * NA_DH)
    na_kg = jnp.tile(na_k_norm, (1, 2)).reshape(depth, 1, 2 * NA_DH)
    ga_qg = ga_q_norm.reshape(depth, 1, GA_DH)
    ga_kg = ga_k_norm.reshape(depth, 1, GA_DH)
    wb_bf, wo_bf = w_branch.astype(BF16), w_out.astype(BF16)

    x2 = x.reshape(b * s, d)
    for l in range(depth):
        pm2, ps2 = _inproj(x2, norm_g[l].reshape(1, d), w_main, w_small, l)
        pm3 = pm2.reshape(b, s, N_MAIN)
        ps3 = ps2.reshape(b, s, LANES)
        o_af, o_ab, h_df, h_db = _scan_and_mlstm(_dn_pre(pm3, ps3, conv8[l], dn_lp[l]), pm3, ps3, ml_lp[l])
        yb = _natten(pm3, na_bias, na_qg[l], na_kg[l], l)
        yc = _gqa(pm3, cos_t, sin_t, ga_qg[l], ga_kg[l])
        x2 = _merge(x2, o_af.reshape(b * s, hw), o_ab.reshape(b * s, hw), h_df.reshape(b * s, hw),
                    h_db.reshape(b * s, hw), yb.reshape(b * s, hw), yc.reshape(b * s, hw), pm2,
                    dn_norm_g[l].reshape(1, LANES), ml_norm_g[l].reshape(1, LANES), wb_bf, wo_bf, l)
    return x2.reshape(b, s, d)
```

```python
import math

import jax
import jax.numpy as jnp
from jax import lax
from jax.experimental import pallas as pl
from jax.experimental.pallas import tpu as pltpu

F32 = jnp.float32
BF16 = jnp.bfloat16

GRID_W = 64
N_BRANCH = 4
BRANCH_W = 512
EPS = 1e-6
DN_HEADS, DN_DK, DN_DV, DN_CONV, DN_CHUNK = 4, 128, 128, 5, 64
NA_HEADS, NA_DH, NA_ROWS, NA_COLS = 8, 64, 8, 16
GA_HEADS, GA_KV_HEADS, GA_DH = 4, 2, 128
ROPE_THETA = 10000.0
ML_HEADS, ML_DK, ML_DV, ML_CHUNK = 4, 64, 128, 128

LANES = 128
SUBLANES = 8
VMEM_LIMIT_BYTES = 56 * 1024 * 1024

_O_A_QKV, _O_A_A, _O_A_B, _O_A_Z = 0, 1536, 1544, 1552
_O_B_QKV, _O_B_Z = 2064, 3600
_O_C_Q, _O_C_K, _O_C_V, _O_C_Z = 4112, 4624, 4880, 5136
_O_D_Q, _O_D_K, _O_D_V, _O_D_I, _O_D_F, _O_D_O, _O_D_Z = 5648, 5904, 6160, 6672, 6680, 6688, 7200
_O_GATE = 7712
_MAIN_SEGS = (
    ("a_qkv", _O_A_QKV, 1536), ("b_qkv", _O_B_QKV, 1536), ("a_z", _O_A_Z, 512), ("b_z", _O_B_Z, 512),
    ("gate", _O_GATE, 4096), ("c_q", _O_C_Q, 512), ("c_k", _O_C_K, 256), ("c_v", _O_C_V, 256),
    ("c_z", _O_C_Z, 512), ("d_q", _O_D_Q, 256), ("d_k", _O_D_K, 256), ("d_v", _O_D_V, 512),
    ("d_o", _O_D_O, 512), ("d_z", _O_D_Z, 512),
)
_OFF = {}
_o = 0
for _name, _src, _w in _MAIN_SEGS:
    _OFF[_name] = _o
    _o += _w
N_MAIN = _o
_SMALL_SRC = (_O_A_A, _O_A_B, _O_D_I, _O_D_F)
_L_AA, _L_AB, _L_DI, _L_DF = 0, 8, 16, 24

P_DTYPE = BF16
BLK = 128
NEG = -1e30


def _cparams(sem):
    return pltpu.CompilerParams(dimension_semantics=sem, vmem_limit_bytes=VMEM_LIMIT_BYTES)


def _sigmoid(x):
    return 0.5 * jnp.tanh(0.5 * x) + 0.5


def _silu(x):
    return x * _sigmoid(x)


def _softplus(x):
    return jnp.maximum(x, 0.0) + jnp.log(1.0 + jnp.exp(-jnp.abs(x)))


def _mm(a, b):
    return jnp.dot(a.astype(BF16), b.astype(BF16), preferred_element_type=F32)


def _mm_nt(a, b):
    return lax.dot_general(a.astype(BF16), b.astype(BF16), (((1,), (1,)), ((), ())),
                           preferred_element_type=F32)


def _mm_tn(a, b):
    return lax.dot_general(a.astype(BF16), b.astype(BF16), (((0,), (0,)), ((), ())),
                           preferred_element_type=F32)


def _mask_sum(mask, x):
    m = jnp.where(mask, 1.0, 0.0).astype(BF16)
    x1 = x.astype(BF16)
    r1 = x - x1.astype(F32)
    x2 = r1.astype(BF16)
    x3 = (r1 - x2.astype(F32)).astype(BF16)
    dot = lambda v: jnp.dot(m, v, preferred_element_type=F32)
    return dot(x1) + (dot(x2) + dot(x3))


def _chunk_masks(reverse, chunk):
    i = lax.broadcasted_iota(jnp.int32, (BLK, BLK), 0)
    j = lax.broadcasted_iota(jnp.int32, (BLK, BLK), 1)
    shift = int(math.log2(chunk))
    same = (i >> shift) == (j >> shift)
    if reverse:
        incl = same & (j >= i)
        strict = same & (j > i)
    else:
        incl = same & (j <= i)
        strict = same & (j < i)
    return same, incl, strict, (i == j)


def _chunk_totals(cum, chunk, reverse):
    pieces = []
    for c0 in range(0, BLK, chunk):
        r = c0 if reverse else c0 + chunk - 1
        pieces.append(jnp.broadcast_to(cum[r:r + 1, :], (chunk, cum.shape[1])))
    return jnp.concatenate(pieces, axis=0)


def _rows(start, size):
    if isinstance(start, int):
        return slice(start, start + size)
    return pl.ds(pl.multiple_of(start, size), size)


def _as_f32(mask):
    return jnp.where(mask, 1.0, 0.0).astype(F32)


def _inproj_kernel(x_ref, g_ref, w_ref, ws_ref, pm_ref, ps_ref, h_scr):
    @pl.when(pl.program_id(1) == 0)
    def _():
        x = x_ref[...]
        ms = jnp.mean(x * x, axis=-1, keepdims=True)
        h = (x * lax.rsqrt(ms + EPS) * g_ref[...]).astype(BF16)
        h_scr[...] = h
        ps_ref[...] = jnp.dot(h, ws_ref[...], preferred_element_type=F32)

    pm_ref[...] = jnp.dot(h_scr[...], w_ref[...], preferred_element_type=F32).astype(pm_ref.dtype)


def _inproj(x2d, g, w_main, w_small, layer, tm=1024, tn=N_MAIN // 4):
    m, d = x2d.shape
    tm = min(tm, m)
    return pl.pallas_call(
        _inproj_kernel,
        grid=(m // tm, N_MAIN // tn),
        in_specs=[
            pl.BlockSpec((tm, d), lambda i, j: (i, 0)),
            pl.BlockSpec((1, d), lambda i, j: (0, 0)),
            pl.BlockSpec((None, d, tn), lambda i, j: (layer, 0, j)),
            pl.BlockSpec((None, d, LANES), lambda i, j: (layer, 0, 0)),
        ],
        out_specs=[
            pl.BlockSpec((tm, tn), lambda i, j: (i, j)),
            pl.BlockSpec((tm, LANES), lambda i, j: (i, 0)),
        ],
        out_shape=[jax.ShapeDtypeStruct((m, N_MAIN), P_DTYPE), jax.ShapeDtypeStruct((m, LANES), F32)],
        scratch_shapes=[pltpu.VMEM((tm, d), BF16)],
        compiler_params=_cparams(("parallel", "arbitrary")),
        name="inproj",
    )(x2d, g, w_main, w_small)


GQA_TQ = 512


def _gqa_kernel(q_ref, k_ref, v_ref, z_ref, cos_ref, sin_ref, qg_ref, kg_ref, y_ref,
                q_scr, k_scr, v_scr, s0_scr, s1_scr, p0_scr, p1_scr, l0_scr, l1_scr):
    def norm_rope(x, g, cos, sin):
        ms = jnp.mean(x * x, axis=-1, keepdims=True)
        xn = x * lax.rsqrt(ms + EPS) * g
        lane = lax.broadcasted_iota(jnp.int32, xn.shape, 1)
        partner = jnp.where((lane & 63) < 32, pltpu.roll(xn, LANES - 32, 1), pltpu.roll(xn, 32, 1))
        return xn * cos + partner * sin

    s_len = k_ref.shape[0]
    group = GA_HEADS // GA_KV_HEADS
    assert group == 2
    scale = GA_DH ** -0.5
    n_blk = s_len // GQA_TQ
    s_bufs, p_bufs, l_bufs = (s0_scr, s1_scr), (p0_scr, p1_scr), (l0_scr, l1_scr)
    for kv in range(k_ref.shape[1] // GA_DH):
        _gqa_pipeline(kv, group, s_len, n_blk, scale, norm_rope, q_ref, k_ref, v_ref, z_ref, cos_ref, sin_ref,
                      qg_ref, kg_ref, y_ref, q_scr, k_scr, v_scr, s_bufs, p_bufs, l_bufs)


def _gqa_pipeline(kv, group, s_len, n_blk, scale, norm_rope, q_ref, k_ref, v_ref, z_ref, cos_ref, sin_ref,
                  qg_ref, kg_ref, y_ref, q_scr, k_scr, v_scr, s_bufs, p_bufs, l_bufs):
    kcols = slice(kv * GA_DH, (kv + 1) * GA_DH)
    k_scr[kv] = norm_rope(k_ref[:, kcols].astype(F32), kg_ref[...], cos_ref[...], sin_ref[...]).astype(BF16)
    v_scr[kv] = v_ref[:, kcols].astype(BF16)

    def stacked(head, blk):
        return _rows(head * s_len + blk * GQA_TQ, GQA_TQ)

    def head_cols(head):
        return slice((kv * group + head) * GA_DH, (kv * group + head + 1) * GA_DH)

    def prep(head, blk):
        rows = _rows(blk * GQA_TQ, GQA_TQ)
        q = norm_rope(q_ref[rows, head_cols(head)].astype(F32), qg_ref[...], cos_ref[rows, :], sin_ref[rows, :])
        q_scr[stacked(head, blk), :] = q.astype(BF16)

    def logits(head, blk):
        s_bufs[head][...] = _mm_nt(q_scr[stacked(head, blk), :], k_scr[kv]) * scale

    def softmax(slot):
        s = s_bufs[slot][...]
        p = jnp.exp(s - jnp.max(s, axis=-1, keepdims=True))
        l_bufs[slot][...] = jnp.broadcast_to(jnp.sum(p, axis=-1, keepdims=True), (GQA_TQ, GA_DH))
        p_bufs[slot][...] = p.astype(BF16)

    def weighted(head, blk):
        rows = _rows(blk * GQA_TQ, GQA_TQ)
        cols = head_cols(head)
        o = _mm(p_bufs[head][...], v_scr[kv]) / l_bufs[head][...]
        y_ref[rows, cols] = (o * _silu(z_ref[rows, cols].astype(F32))).astype(y_ref.dtype)

    prep(0, 0)
    prep(1, 0)
    prep(0, 1)
    logits(0, 0)
    logits(1, 0)
    softmax(0)

    def body(j, carry):
        logits(0, j + 1)
        softmax(1)
        weighted(0, j)
        prep(1, j + 1)
        logits(1, j + 1)
        softmax(0)
        weighted(1, j)
        prep(0, jnp.minimum(j + 2, n_blk - 1))
        return carry

    lax.fori_loop(0, n_blk - 1, body, 0)
    softmax(1)
    weighted(0, n_blk - 1)
    weighted(1, n_blk - 1)


def _gqa(pm3, cos_t, sin_t, qg, kg):
    b, s, _ = pm3.shape
    qw = GA_HEADS * GA_DH
    kw = GA_KV_HEADS * GA_DH
    group = GA_HEADS // GA_KV_HEADS
    return pl.pallas_call(
        _gqa_kernel,
        grid=(b,),
        in_specs=[
            pl.BlockSpec((None, s, qw), lambda i: (i, 0, _OFF["c_q"] // qw)),
            pl.BlockSpec((None, s, kw), lambda i: (i, 0, _OFF["c_k"] // kw)),
            pl.BlockSpec((None, s, kw), lambda i: (i, 0, _OFF["c_v"] // kw)),
            pl.BlockSpec((None, s, qw), lambda i: (i, 0, _OFF["c_z"] // qw)),
            pl.BlockSpec((s, GA_DH), lambda i: (0, 0)),
            pl.BlockSpec((s, GA_DH), lambda i: (0, 0)),
            pl.BlockSpec((1, GA_DH), lambda i: (0, 0)),
            pl.BlockSpec((1, GA_DH), lambda i: (0, 0)),
        ],
        out_specs=pl.BlockSpec((None, s, qw), lambda i: (i, 0, 0)),
        out_shape=jax.ShapeDtypeStruct((b, s, BRANCH_W), BF16),
        scratch_shapes=[pltpu.VMEM((group * s, GA_DH), BF16), pltpu.VMEM((GA_KV_HEADS, s, GA_DH), BF16),
                        pltpu.VMEM((GA_KV_HEADS, s, GA_DH), BF16),
                        pltpu.VMEM((GQA_TQ, s), F32), pltpu.VMEM((GQA_TQ, s), F32),
                        pltpu.VMEM((GQA_TQ, s), BF16), pltpu.VMEM((GQA_TQ, s), BF16),
                        pltpu.VMEM((GQA_TQ, GA_DH), F32), pltpu.VMEM((GQA_TQ, GA_DH), F32)],
        compiler_params=_cparams(("parallel",)),
        name="gqa",
    )(pm3, pm3, pm3, pm3, cos_t, sin_t, qg, kg)


NA_ROW_UNROLL = 16


def _na_kernel(q_ref, k_ref, v_ref, z_ref, bias_ref, qg_ref, kg_ref, y_ref, q_scr, k_scr):
    s_len = q_ref.shape[0]
    rows = s_len // GRID_W
    kr = min(NA_ROWS, rows)
    hi = lax.broadcasted_iota(jnp.int32, (2 * NA_DH, 2 * NA_DH), 0) >= NA_DH
    hj = lax.broadcasted_iota(jnp.int32, (2 * NA_DH, 2 * NA_DH), 1) >= NA_DH
    same_head = jnp.where(hi == hj, 1.0, 0.0).astype(BF16)

    def rms_pair(x, g):
        x2 = x * x
        x2_hi = x2.astype(BF16)
        x2_lo = (x2 - x2_hi.astype(F32)).astype(BF16)
        ssq = (jnp.dot(x2_hi, same_head, preferred_element_type=F32)
               + jnp.dot(x2_lo, same_head, preferred_element_type=F32))
        return x * lax.rsqrt(ssq * (1.0 / NA_DH) + EPS) * g

    scale = NA_DH ** -0.5
    assert math.log2(scale).is_integer()
    nkeys = kr * GRID_W
    first = lax.broadcasted_iota(jnp.int32, (s_len, 2 * NA_DH), 1) < NA_DH
    first_q = lax.broadcasted_iota(jnp.int32, (GRID_W, 2 * NA_DH), 1) < NA_DH
    for pair in range(q_ref.shape[1] // (2 * NA_DH)):
        pc = slice(pair * 2 * NA_DH, (pair + 1) * 2 * NA_DH)
        qn = rms_pair(q_ref[:, pc].astype(F32), qg_ref[...]) * scale
        q_scr[0] = jnp.where(first, qn, 0.0).astype(BF16)
        q_scr[1] = jnp.where(first, 0.0, qn).astype(BF16)
        k_scr[...] = rms_pair(k_ref[:, pc].astype(F32), kg_ref[...]).astype(BF16)

        def body(it, carry, pair=pair, pc=pc):
            units = []
            for u in range(NA_ROW_UNROLL):
                r = it * NA_ROW_UNROLL + u
                r0 = jnp.clip(r - kr // 2, 0, rows - kr)
                var = r0 - r + (NA_ROWS - 1)
                qrows = pl.ds(pl.multiple_of(r * GRID_W, GRID_W), GRID_W)
                krows = pl.ds(pl.multiple_of(r0 * GRID_W, GRID_W), nkeys)
                units.append((var, qrows, krows))
            logits = [_mm_nt(jnp.concatenate([q_scr[0, qrows, :], q_scr[1, qrows, :]], axis=0), k_scr[krows, :])
                      + bias_ref[pair, var] for var, qrows, krows in units]
            probs = [jnp.exp(s - jnp.max(s, axis=-1, keepdims=True)) for s in logits]
            sums = [jnp.sum(p, axis=-1, keepdims=True) for p in probs]
            outs = [_mm(p, v_ref[krows, pc]) / l for p, l, (_, _, krows) in zip(probs, sums, units)]
            for o, (_, qrows, _) in zip(outs, units):
                o_pair = jnp.where(first_q, o[:GRID_W], o[GRID_W:])
                y_ref[qrows, pc] = (o_pair * _silu(z_ref[qrows, pc].astype(F32))).astype(y_ref.dtype)
            return carry

        lax.fori_loop(0, rows // NA_ROW_UNROLL, body, 0)


def _na_bias_table(rpb, rows):
    kr = min(NA_ROWS, rows)
    r = jnp.arange(2 * GRID_W)
    k = jnp.arange(kr * GRID_W)
    hh, q = r // GRID_W, r % GRID_W
    j, kw = k // GRID_W, k % GRID_W
    head = 2 * jnp.arange(rpb.shape[-3] // 2)[:, None, None, None] + hh[None, None, :, None]
    row_off = jnp.arange(NA_ROWS)[None, :, None, None] + j[None, None, None, :]
    col_off = jnp.clip(kw[None, :] - q[:, None], -(NA_COLS - 1), NA_COLS - 1) + NA_COLS - 1
    c0 = jnp.clip(q - NA_COLS // 2, 0, GRID_W - NA_COLS)
    in_win = (kw[None, :] >= c0[:, None]) & (kw[None, :] < c0[:, None] + NA_COLS)
    table = rpb[..., head, row_off, col_off[None, None]]
    return jnp.where(in_win, table, NEG).astype(F32)


def _natten(pm3, bias, qg, kg, layer):
    b, s, _ = pm3.shape
    pw = 2 * NA_DH
    hw = NA_HEADS * NA_DH
    return pl.pallas_call(
        _na_kernel,
        grid=(b,),
        in_specs=[
            pl.BlockSpec((None, s, hw), lambda i: (i, 0, _OFF["b_qkv"] // hw)),
            pl.BlockSpec((None, s, hw), lambda i: (i, 0, _OFF["b_qkv"] // hw + 1)),
            pl.BlockSpec((None, s, hw), lambda i: (i, 0, _OFF["b_qkv"] // hw + 2)),
            pl.BlockSpec((None, s, hw), lambda i: (i, 0, _OFF["b_z"] // hw)),
            pl.BlockSpec((None,) + bias.shape[1:], lambda i: (layer, 0, 0, 0, 0)),
            pl.BlockSpec((1, pw), lambda i: (0, 0)),
            pl.BlockSpec((1, pw), lambda i: (0, 0)),
        ],
        out_specs=pl.BlockSpec((None, s, hw), lambda i: (i, 0, 0)),
        out_shape=jax.ShapeDtypeStruct((b, s, BRANCH_W), BF16),
        scratch_shapes=[pltpu.VMEM((2, s, pw), BF16), pltpu.VMEM((s, pw), BF16)],
        compiler_params=_cparams(("parallel",)),
        name="natten",
    )(pm3, pm3, pm3, pm3, bias, qg, kg)


INV_BASE = 8
PRE_BLK = 4 * BLK


def _inverse_level_masks():
    i = lax.broadcasted_iota(jnp.int32, (BLK, BLK), 0)
    j = lax.broadcasted_iota(jnp.int32, (BLK, BLK), 1)
    same = lambda size: (i >> int(math.log2(size))) == (j >> int(math.log2(size)))
    base = same(INV_BASE)
    joins = []
    size = INV_BASE
    while size < DN_CHUNK:
        joins.append(same(2 * size) & jnp.logical_not(same(size)))
        size *= 2
    return base, joins


def _tri_inverses(l_mats, eyes, level_masks):
    base, joins = level_masks
    ps = [jnp.where(base, -l, 0.0) for l in l_mats]
    ts = [eye + p for eye, p in zip(eyes, ps)]
    for _ in range(int(math.log2(INV_BASE)) - 1):
        ps = [_mm(p, p) for p in ps]
        ts = [t + _mm(t, p) for t, p in zip(ts, ps)]
    for join in joins:
        mids = [_mm(jnp.where(join, l, 0.0), t) for l, t in zip(l_mats, ts)]
        ts = [t - _mm(t, mid) for t, mid in zip(ts, mids)]
    return ts


DN_HW = DN_HEADS * LANES
PK_W, PK_QD, PK_KD, PK_QK = (i * DN_HW for i in range(4))


def _dn_pre_kernel(qkv_ref, sm_ref, cw_ref, lp_ref, uf_ref, ub_ref, pkf_ref, pkb_ref, gtf_ref, gtb_ref):
    n = pl.program_id(1)
    s_len = qkv_ref.shape[0]
    assert qkv_ref.dtype == BF16
    halo = 2 * SUBLANES
    n_sub = sm_ref.shape[0] // BLK
    chunks_per_blk = BLK // DN_CHUNK
    out_row = lax.broadcasted_iota(jnp.int32, (BLK, BLK + 2 * halo), 0)
    in_row = lax.broadcasted_iota(jnp.int32, (BLK, BLK + 2 * halo), 1)
    shifts = {j: jnp.where(in_row == out_row + (halo + j - DN_CONV // 2), 1.0, 0.0).astype(BF16)
              for j in range(DN_CONV) if j != DN_CONV // 2}
    dir_masks = [_chunk_masks(d == 1, DN_CHUNK) for d in range(2)]
    level_masks = _inverse_level_masks()
    outs = ((uf_ref, pkf_ref, gtf_ref), (ub_ref, pkb_ref, gtb_ref))

    def l2n(x):
        return x * lax.rsqrt(jnp.sum(x * x, axis=-1, keepdims=True) + EPS)

    blocks = []
    for sub in range(n_sub):
        t0 = pl.multiple_of((n * n_sub + sub) * BLK, BLK)
        pstart = pl.multiple_of(jnp.maximum(t0 - halo, 0), halo)
        nstart = pl.multiple_of(jnp.minimum(t0 + BLK, s_len - halo), halo)
        prev = qkv_ref[pl.ds(pstart, halo), :]
        prev = jnp.where(t0 > 0, prev, jnp.zeros_like(prev))
        cur = qkv_ref[pl.ds(t0, BLK), :]
        nxt = qkv_ref[pl.ds(nstart, halo), :]
        nxt = jnp.where(t0 + BLK < s_len, nxt, jnp.zeros_like(nxt))
        xw = jnp.concatenate([prev, cur, nxt], axis=0)
        conv = cur.astype(F32) * cw_ref[DN_CONV // 2:DN_CONV // 2 + 1, :]
        for j, shift in shifts.items():
            conv = conv + jnp.dot(shift, xw, preferred_element_type=F32) * cw_ref[j:j + 1, :]
        conv = _silu(conv)

        rows = slice(sub * BLK, (sub + 1) * BLK)
        sm = sm_ref[rows, :]
        g_all = -jnp.exp(lp_ref[0:1, :]) * _softplus(sm + lp_ref[1:2, :])
        beta_all = _sigmoid(sm)
        per_dir = []
        for d in range(2):
            same, incl, strict, eye_b = dir_masks[d]
            gc = _mask_sum(incl, g_all)
            tot = _mask_sum(same, g_all)
            per_dir.append((incl, strict, _as_f32(eye_b), gc, gc.T, tot))
        blocks.append((sub, rows, conv, beta_all, per_dir))

    heads = []
    for sub, rows, conv, beta_all, per_dir in blocks:
        for h in range(DN_HEADS):
            q = l2n(conv[:, h * DN_DK:(h + 1) * DN_DK]) * (DN_DK ** -0.5)
            k = l2n(conv[:, (DN_HEADS + h) * DN_DK:(DN_HEADS + h + 1) * DN_DK])
            v = conv[:, 2 * DN_HEADS * DN_DK + h * DN_DV:2 * DN_HEADS * DN_DK + (h + 1) * DN_DV]
            heads.append((sub, rows, beta_all, per_dir, h, q, k, v))
    kks = [_mm_nt(hd[6], hd[6]) for hd in heads]
    qks = [_mm_nt(hd[5], hd[6]) for hd in heads]

    l_mats, rhss, eyes, slots = [], [], [], []
    for (sub, rows, beta_all, per_dir, h, q, k, v), kk, qk in zip(heads, kks, qks):
        hc = slice(h * LANES, (h + 1) * LANES)
        for d in range(2):
            incl, strict, eye_f, gc, gct, tot = per_dir[d]
            u_ref, pk_ref, gt_ref = outs[d]
            pk = lambda off: slice(off + h * LANES, off + (h + 1) * LANES)
            c = _L_AA + d * DN_HEADS + h
            gcol = gc[:, c:c + 1]
            grow = gct[c:c + 1, :]
            tcol = tot[:, c:c + 1]
            beta = beta_all[:, _L_AB + d * DN_HEADS + h:_L_AB + d * DN_HEADS + h + 1]
            decay = jnp.where(incl, jnp.exp(jnp.where(incl, gcol - grow, 0.0)), 0.0)
            l_mats.append(jnp.where(strict, beta * kk * decay, 0.0))
            egc = jnp.exp(gcol)
            rhss.append(jnp.concatenate([v * beta, k * (beta * egc)], axis=-1).astype(BF16))
            eyes.append(eye_f)
            slots.append((rows, h, d))
            pk_ref[rows, pk(PK_QD)] = (q * egc).astype(pk_ref.dtype)
            pk_ref[rows, pk(PK_KD)] = (k * jnp.exp(tcol - gcol)).astype(pk_ref.dtype)
            pk_ref[rows, pk(PK_QK)] = (qk * decay).astype(pk_ref.dtype)
            gtot = jnp.exp(tcol)
            for ci in range(chunks_per_blk):
                g0 = (sub * chunks_per_blk + ci) * SUBLANES
                gt_ref[g0:g0 + SUBLANES, hc] = jnp.broadcast_to(
                    gtot[ci * DN_CHUNK:ci * DN_CHUNK + SUBLANES, :], (SUBLANES, LANES))

    t_invs = _tri_inverses(l_mats, eyes, level_masks)
    sols = [_mm(t, rhs) for t, rhs in zip(t_invs, rhss)]
    for (rows, h, d), sol in zip(slots, sols):
        u_ref, pk_ref = outs[d][0], outs[d][1]
        u_ref[rows, h * LANES:(h + 1) * LANES] = sol[:, :DN_DV]
        pk_ref[rows, PK_W + h * LANES:PK_W + (h + 1) * LANES] = sol[:, DN_DV:].astype(pk_ref.dtype)


def _dn_pre(pm3, ps3, conv_w8, lane_params):
    b, s, _ = pm3.shape
    nblk = s // PRE_BLK
    gt_rows = PRE_BLK // DN_CHUNK * SUBLANES
    wq = DN_HEADS * (2 * DN_DK + DN_DV)
    hw = DN_HEADS * LANES
    tok = lambda i, j: (i, j, 0)
    big = lambda dt: jax.ShapeDtypeStruct((b, s, hw), dt)
    gts = jax.ShapeDtypeStruct((b, nblk * gt_rows, hw), F32)
    bs_tok = pl.BlockSpec((None, PRE_BLK, hw), tok)
    bs_gt = pl.BlockSpec((None, gt_rows, hw), tok)
    bs_pk = pl.BlockSpec((None, PRE_BLK, 4 * hw), tok)
    packed = jax.ShapeDtypeStruct((b, s, 4 * hw), BF16)
    return pl.pallas_call(
        _dn_pre_kernel,
        grid=(b, nblk),
        in_specs=[
            pl.BlockSpec((None, s, wq), lambda i, j: (i, 0, _OFF["a_qkv"] // wq)),
            pl.BlockSpec((None, PRE_BLK, LANES), tok),
            pl.BlockSpec((SUBLANES, wq), lambda i, j: (0, 0)),
            pl.BlockSpec((SUBLANES, LANES), lambda i, j: (0, 0)),
        ],
        out_specs=[bs_tok, bs_tok, bs_pk, bs_pk, bs_gt, bs_gt],
        out_shape=[big(F32), big(F32), packed, packed, gts, gts],
        compiler_params=_cparams(("parallel", "arbitrary")),
        name="dn_pre",
    )(pm3, ps3, conv_w8, lane_params)


SCAN_BLK = 4 * BLK


def _zero_at_sequence_start(*scratch):
    @pl.when(pl.program_id(1) == 0)
    def _():
        for ref in scratch:
            ref[...] = jnp.zeros_like(ref)


def _dn_scan_body(uf_ref, pkf_ref, gtf_ref, ub_ref, pkb_ref, gtb_ref, of_ref, ob_ref, st_scr):
    per_blk = BLK // DN_CHUNK
    nchunk = SCAN_BLK // DN_CHUNK
    zeros_c = jnp.zeros((DN_CHUNK, DN_DV), F32)
    streams = ((uf_ref, pkf_ref, gtf_ref, of_ref, range(nchunk)),
               (ub_ref, pkb_ref, gtb_ref, ob_ref, range(nchunk - 1, -1, -1)))
    chains = [(d, h) + streams[d] for d in range(2) for h in range(DN_HEADS)]
    states = [st_scr[d * DN_HEADS + h] for d, h, *_ in chains]
    for step in range(nchunk):
        rs, v_pads = [], []
        for (d, h, u_ref, pk_ref, gt_ref, o_ref, order), state in zip(chains, states):
            rows = slice(order[step] * DN_CHUNK, (order[step] + 1) * DN_CHUNK)
            w = pk_ref[rows, PK_W + h * LANES:PK_W + (h + 1) * LANES]
            qd = pk_ref[rows, PK_QD + h * LANES:PK_QD + (h + 1) * LANES]
            rs.append(_mm(jnp.concatenate([w, qd], axis=0), state))
        for (d, h, u_ref, pk_ref, gt_ref, o_ref, order), r in zip(chains, rs):
            ci = order[step]
            rows = slice(ci * DN_CHUNK, (ci + 1) * DN_CHUNK)
            parts = [zeros_c] * per_blk
            parts[ci % per_blk] = u_ref[rows, h * LANES:(h + 1) * LANES] - r[:DN_CHUNK]
            v_pads.append(jnp.concatenate(parts, axis=0))
        new_states = []
        for (d, h, u_ref, pk_ref, gt_ref, o_ref, order), r, v_pad, state in zip(chains, rs, v_pads, states):
            hc = slice(h * LANES, (h + 1) * LANES)
            ci = order[step]
            rows = slice(ci * DN_CHUNK, (ci + 1) * DN_CHUNK)
            blk_rows = slice((ci // per_blk) * BLK, (ci // per_blk + 1) * BLK)
            o_ref[rows, hc] = r[DN_CHUNK:] + _mm(pk_ref[rows, PK_QK + h * LANES:PK_QK + (h + 1) * LANES], v_pad)
            gt = gt_ref[ci * SUBLANES:ci * SUBLANES + 1, hc]
            kd = pk_ref[blk_rows, PK_KD + h * LANES:PK_KD + (h + 1) * LANES]
            new_states.append(state * gt + _mm_tn(kd, v_pad))
        states = new_states
    for (d, h, *_), state in zip(chains, states):
        st_scr[d * DN_HEADS + h] = state


def _dn_scan_parts(pre):
    uf, ub, pkf, pkb, gtf, gtb = pre
    b, s, hw = uf.shape
    nblk = s // SCAN_BLK
    gt_rows = SCAN_BLK // DN_CHUNK * SUBLANES
    fwd = lambda i, j: (i, j, 0)
    bwd = lambda i, j: (i, nblk - 1 - j, 0)
    def specs(imap):
        return [pl.BlockSpec((None, SCAN_BLK, hw), imap), pl.BlockSpec((None, SCAN_BLK, 4 * hw), imap),
                pl.BlockSpec((None, gt_rows, hw), imap)]
    return dict(
        grid=(b, nblk), args=(uf, pkf, gtf, ub, pkb, gtb), in_specs=specs(fwd) + specs(bwd),
        out_specs=[pl.BlockSpec((None, SCAN_BLK, hw), fwd), pl.BlockSpec((None, SCAN_BLK, hw), bwd)],
        out_shape=[jax.ShapeDtypeStruct((b, s, hw), F32)] * 2,
        scratch_shapes=[pltpu.VMEM((2 * DN_HEADS, DN_DK, DN_DV), F32)])


def _call_parts(kernel_fn, name, parts):
    return pl.pallas_call(
        kernel_fn, grid=parts["grid"], in_specs=parts["in_specs"], out_specs=parts["out_specs"],
        out_shape=parts["out_shape"], scratch_shapes=parts["scratch_shapes"],
        compiler_params=_cparams(("parallel", "arbitrary")), name=name)(*parts["args"])


def _scan_and_mlstm(pre, pm3, ps3, ml_lane_params):
    dn, ml = _dn_scan_parts(pre), _mlstm_parts(pm3, ps3, ml_lane_params)
    assert dn["grid"] == ml["grid"]
    n_in = (len(dn["args"]), len(ml["args"]))
    n_out = (len(dn["out_shape"]), len(ml["out_shape"]))

    def both(*refs):
        ins, rest = refs[:sum(n_in)], refs[sum(n_in):]
        outs, scr = rest[:sum(n_out)], rest[sum(n_out):]
        n_dn_scr = len(dn["scratch_shapes"])
        _zero_at_sequence_start(*scr)
        _dn_scan_body(*ins[:n_in[0]], *outs[:n_out[0]], *scr[:n_dn_scr])
        _ml_body(*ins[n_in[0]:], *outs[n_out[0]:], *scr[n_dn_scr:])

    fused = {k: dn[k] + ml[k] for k in ("args", "in_specs", "out_specs", "out_shape", "scratch_shapes")}
    fused["grid"] = dn["grid"]
    return _call_parts(both, "scan_mlstm", fused)


ML_AUG = 2 * LANES
ML_GROUP = 4
ML_BLK = 4 * BLK


def _ml_body(qf_ref, kf_ref, vf_ref, smf_ref, qb_ref, kb_ref, vb_ref, smb_ref, lp_ref,
             hf_ref, hb_ref, c_scr, m_scr):
    nchunk = BLK // ML_CHUNK
    ones_col = jnp.ones((BLK, LANES), BF16)
    zeros_aug = jnp.zeros((ML_CHUNK, ML_AUG), BF16)
    streams = ((qf_ref, kf_ref, vf_ref, smf_ref, hf_ref, range(nchunk)),
               (qb_ref, kb_ref, vb_ref, smb_ref, hb_ref, range(nchunk - 1, -1, -1)))
    lanes = lambda col: jnp.broadcast_to(col, (col.shape[0], LANES))
    n_sub = smf_ref.shape[0] // BLK
    units = [(d, p if d == 0 else n_sub - 1 - p) for p in range(n_sub) for d in range(2)]
    for d, sub in units:
        _ml_block(d, sub, streams[d], lp_ref, c_scr, m_scr, lanes, ones_col, zeros_aug, nchunk)


def _ml_block(d, sub, stream, lp_ref, c_scr, m_scr, lanes, ones_col, zeros_aug, nchunk):
    q_ref, k_ref, v_ref, sm_ref, h_ref, order = stream
    blk_rows = slice(sub * BLK, (sub + 1) * BLK)
    same, incl, _, _ = _chunk_masks(d == 1, ML_CHUNK)
    sm = sm_ref[blk_rows, :]
    ig_all = sm + lp_ref[0:1, :]
    x = sm + lp_ref[1:2, :]
    lf_all = jnp.minimum(x, 0.0) - jnp.log(1.0 + jnp.exp(-jnp.abs(x)))
    lf_all = pltpu.roll(lf_all, LANES - (_L_DF - _L_DI), 1)
    bc_all = _mask_sum(incl, lf_all)
    tot_all = _chunk_totals(bc_all, ML_CHUNK, reverse=(d == 1))
    a_all = ig_all - bc_all
    mwa_all = jnp.concatenate(
        [jnp.broadcast_to(jnp.max(a_all[ci * ML_CHUNK:(ci + 1) * ML_CHUNK], axis=0, keepdims=True),
                          (ML_CHUNK, LANES)) for ci in range(nchunk)], axis=0)
    a_t, w_all, mw_all = a_all.T, jnp.exp(a_all - mwa_all), tot_all + mwa_all

    for chains in [[(d, h) for h in range(g, g + ML_GROUP)] for g in range(0, ML_HEADS, ML_GROUP)]:
        ins = []
        for _, h in chains:
            q = q_ref[blk_rows, h * ML_DK:(h + 1) * ML_DK].astype(BF16)
            k = k_ref[blk_rows, h * ML_DK:(h + 1) * ML_DK].astype(F32) * (ML_DK ** -0.5)
            v_aug = jnp.concatenate([v_ref[blk_rows, h * ML_DV:(h + 1) * ML_DV].astype(BF16), ones_col], axis=-1)
            ins.append((q, k, v_aug))
        qks = [_mm_nt(q, k) for q, k, _ in ins]

        mids = []
        for (d, h), (q, k, v_aug), qk in zip(chains, ins, qks):
            c = _L_DI + d * ML_HEADS + h
            b_l = lanes(bc_all[:, c:c + 1])
            dlog = jnp.where(incl, b_l + a_t[c:c + 1, :], NEG)
            m_intra = lanes(jnp.max(dlog, axis=-1, keepdims=True))
            s_intra = qk * jnp.exp(dlog - m_intra)
            wk = (k * lanes(w_all[:, c:c + 1])[:, :ML_DK]).astype(BF16)
            mids.append((b_l, m_intra, s_intra.astype(BF16), wk))
        p_intras = [_mm(s_b, v_aug) for (_, _, s_b, _), (_, _, v_aug) in zip(mids, ins)]
        kvs = []
        for (_, _, _, wk), (_, _, v_aug) in zip(mids, ins):
            per_chunk = []
            for ci in range(nchunk):
                parts = [zeros_aug] * nchunk
                parts[ci] = v_aug[ci * ML_CHUNK:(ci + 1) * ML_CHUNK]
                per_chunk.append(_mm_tn(wk, jnp.concatenate(parts, axis=0)))
            kvs.append(per_chunk)

        c_sts = [c_scr[d * ML_HEADS + h] for d, h in chains]
        m_sts = [m_scr[d * ML_HEADS + h][0:1, :] for d, h in chains]
        for step in range(nchunk):
            qcs = []
            ci = order[step]
            for (q, _, _), c_st in zip(ins, c_sts):
                qcs.append(_mm(q[ci * ML_CHUNK:(ci + 1) * ML_CHUNK], c_st))
            for idx, (_, h) in enumerate(chains):
                b_l, m_intra, _, _ = mids[idx]
                c = _L_DI + d * ML_HEADS + h
                rows = slice(ci * ML_CHUNK, (ci + 1) * ML_CHUNK)
                out_rows = slice(sub * BLK + ci * ML_CHUNK, sub * BLK + (ci + 1) * ML_CHUNK)
                r8 = slice(ci * ML_CHUNK, ci * ML_CHUNK + SUBLANES)
                m_st, c_st, qc = m_sts[idx], c_sts[idx], qcs[idx]
                m_inter = b_l[rows] + m_st
                m_i = jnp.maximum(m_intra[rows], m_inter)
                f_i = jnp.exp(m_intra[rows] - m_i)
                inter = jnp.exp(m_inter - m_i)
                both = (jnp.concatenate([inter, inter], axis=-1) * qc
                        + jnp.concatenate([f_i, f_i], axis=-1) * p_intras[idx][rows])
                numer, denom = both[:, :ML_DV], both[:, ML_DV:]
                h_ref[out_rows, h * ML_DV:(h + 1) * ML_DV] = numer / jnp.maximum(jnp.abs(denom), jnp.exp(-m_i))
                tot_c = lanes(tot_all[r8, c:c + 1])[0:1]
                mw_c = lanes(mw_all[r8, c:c + 1])[0:1]
                m_new = jnp.maximum(tot_c + m_st, mw_c)
                dec = jnp.exp(tot_c + m_st - m_new)
                gain = jnp.exp(mw_c - m_new)
                c_sts[idx] = (jnp.concatenate([dec, dec], axis=-1) * c_st
                              + jnp.concatenate([gain, gain], axis=-1) * kvs[idx][ci])
                m_sts[idx] = m_new
        for idx, (d, h) in enumerate(chains):
            c_scr[d * ML_HEADS + h] = c_sts[idx]
            m_scr[d * ML_HEADS + h] = jnp.broadcast_to(m_sts[idx], (SUBLANES, LANES))


def _mlstm_parts(pm3, ps3, lane_params):
    b, s, _ = pm3.shape
    nblk = s // ML_BLK
    qw = ML_HEADS * ML_DK
    vw = ML_HEADS * ML_DV
    def specs(tmap):
        blk = lambda j: tmap(j)
        return [
            pl.BlockSpec((None, ML_BLK, qw), lambda i, j: (i, blk(j), _OFF["d_q"] // qw)),
            pl.BlockSpec((None, ML_BLK, qw), lambda i, j: (i, blk(j), _OFF["d_k"] // qw)),
            pl.BlockSpec((None, ML_BLK, vw), lambda i, j: (i, blk(j), _OFF["d_v"] // vw)),
            pl.BlockSpec((None, ML_BLK, LANES), lambda i, j: (i, blk(j), 0)),
        ]
    fwd = lambda j: j
    bwd = lambda j: nblk - 1 - j
    return dict(
        grid=(b, nblk), args=(pm3, pm3, pm3, ps3, pm3, pm3, pm3, ps3, lane_params),
        in_specs=specs(fwd) + specs(bwd) + [pl.BlockSpec((SUBLANES, LANES), lambda i, j: (0, 0))],
        out_specs=[pl.BlockSpec((None, ML_BLK, vw), lambda i, j: (i, j, 0)),
                   pl.BlockSpec((None, ML_BLK, vw), lambda i, j: (i, nblk - 1 - j, 0))],
        out_shape=[jax.ShapeDtypeStruct((b, s, vw), F32)] * 2,
        scratch_shapes=[pltpu.VMEM((2 * ML_HEADS, ML_DK, ML_AUG), F32),
                        pltpu.VMEM((2 * ML_HEADS, SUBLANES, LANES), F32)])


def _merge_kernel(x_ref, af_ref, ab_ref, df_ref, db_ref, yb_ref, yc_ref, az_ref, dz_ref, do_ref, gl_ref,
                  ag_ref, dg_ref, wb_ref, wo_ref, o_ref):
    d = x_ref.shape[-1]

    def head_rms(x, g):
        outs = []
        for h in range(x.shape[-1] // LANES):
            xh = x[:, h * LANES:(h + 1) * LANES]
            ms = jnp.mean(xh * xh, axis=-1, keepdims=True)
            outs.append(xh * lax.rsqrt(ms + EPS) * g)
        return jnp.concatenate(outs, axis=-1)

    ya = head_rms(af_ref[...] + ab_ref[...], ag_ref[...]) * _silu(az_ref[...].astype(F32))
    yd = _sigmoid(do_ref[...].astype(F32)) * head_rms(df_ref[...] + db_ref[...], dg_ref[...])
    yd = yd * _silu(dz_ref[...].astype(F32))
    twice = None
    for i, y in enumerate((ya.astype(BF16), yb_ref[...], yc_ref[...], yd.astype(BF16))):
        proj = jnp.dot(y, wb_ref[i], preferred_element_type=F32)
        term = proj + jnp.tanh(0.5 * gl_ref[:, i * d:(i + 1) * d].astype(F32)) * proj
        twice = term if twice is None else twice + term
    merged = (0.5 * twice).astype(BF16)
    o_ref[...] = x_ref[...] + jnp.dot(merged, wo_ref[...], preferred_element_type=F32)


def _merge(x2d, af, ab, df, db, yb, yc, pm2, ag, dg, wb, wo, layer, tm=512):
    m, d = x2d.shape
    gw = N_BRANCH * d
    w = BRANCH_W
    tok = pl.BlockSpec((tm, w), lambda i: (i, 0))
    col = lambda name: pl.BlockSpec((tm, w), lambda i: (i, _OFF[name] // w))
    vec = pl.BlockSpec((1, LANES), lambda i: (0, 0))
    return pl.pallas_call(
        _merge_kernel,
        grid=(m // tm,),
        in_specs=[
            pl.BlockSpec((tm, d), lambda i: (i, 0)),
            tok, tok, tok, tok, tok, tok,
            col("a_z"), col("d_z"), col("d_o"),
            pl.BlockSpec((tm, gw), lambda i: (i, _OFF["gate"] // gw)),
            vec, vec,
            pl.BlockSpec((None, N_BRANCH, w, d), lambda i: (layer, 0, 0, 0)),
            pl.BlockSpec((None, d, d), lambda i: (layer, 0, 0)),
        ],
        out_specs=pl.BlockSpec((tm, d), lambda i: (i, 0)),
        out_shape=jax.ShapeDtypeStruct((m, d), F32),
        compiler_params=_cparams(("parallel",)),
        name="merge",
    )(x2d, af, ab, df, db, yb, yc, pm2, pm2, pm2, pm2, ag, dg, wb, wo)


def _rope_lane_tables(s):
    t = jnp.arange(s)
    row = (t // GRID_W).astype(F32)
    col = (t % GRID_W).astype(F32)
    m = GA_DH // 4
    inv = ROPE_THETA ** (-jnp.arange(m, dtype=F32) / m)
    ar = row[:, None] * inv
    ac = col[:, None] * inv
    cos_t = jnp.concatenate([jnp.cos(ar), jnp.cos(ar), jnp.cos(ac), jnp.cos(ac)], axis=-1)
    sin_t = jnp.concatenate([-jnp.sin(ar), jnp.sin(ar), -jnp.sin(ac), jnp.sin(ac)], axis=-1)
    return cos_t.astype(F32), sin_t.astype(F32)


def _main_columns(w):
    return jnp.concatenate([w[..., o:o + wd] for _, o, wd in _MAIN_SEGS], axis=-1)


def _lane_tiles(rows):
    padded = []
    for off, vals in rows:
        vals = vals.reshape(vals.shape[0], 1, -1).astype(F32)
        padded.append(jnp.pad(vals, ((0, 0), (0, 0), (off, LANES - off - vals.shape[-1]))))
    tiles = jnp.concatenate(padded, axis=1)
    return jnp.pad(tiles, ((0, 0), (0, SUBLANES - len(rows)), (0, 0)))


def kernel(x, norm_g, w_in, conv_a, dn_a_log, dn_dt_bias, dn_norm_g, na_q_norm, na_k_norm, na_rpb,
           ga_q_norm, ga_k_norm, ml_i_bias, ml_f_bias, ml_norm_g, w_branch, w_out):
    b, s, d = x.shape
    depth = w_in.shape[0]
    hw = BRANCH_W
    cos_t, sin_t = _rope_lane_tables(s)
    w_main = _main_columns(w_in).astype(BF16)
    w_small = jnp.pad(jnp.concatenate([w_in[:, :, o:o + 8] for o in _SMALL_SRC], axis=2),
                      ((0, 0), (0, 0), (0, LANES - 8 * len(_SMALL_SRC)))).astype(BF16)
    conv8 = jnp.pad(conv_a.astype(F32), ((0, 0), (0, SUBLANES - DN_CONV), (0, 0)))
    dn_lp = _lane_tiles([(_L_AA, dn_a_log), (_L_AA, dn_dt_bias)])
    ml_lp = _lane_tiles([(_L_DI, ml_i_bias), (_L_DF, ml_f_bias)])
    na_bias = _na_bias_table(na_rpb, s // GRID_W)
    na_qg = jnp.tile(na_q_norm, (1, 2)).reshape(depth, 1, 2 * NA_DH)
    na_kg = jnp.tile(na_k_norm, (1, 2)).reshape(depth, 1, 2 * NA_DH)
    ga_qg = ga_q_norm.reshape(depth, 1, GA_DH)
    ga_kg = ga_k_norm.reshape(depth, 1, GA_DH)
    wb_bf, wo_bf = w_branch.astype(BF16), w_out.astype(BF16)

    x2 = x.reshape(b * s, d)
    for l in range(depth):
        pm2, ps2 = _inproj(x2, norm_g[l].reshape(1, d), w_main, w_small, l)
        pm3 = pm2.reshape(b, s, N_MAIN)
        ps3 = ps2.reshape(b, s, LANES)
        o_af, o_ab, h_df, h_db = _scan_and_mlstm(_dn_pre(pm3, ps3, conv8[l], dn_lp[l]), pm3, ps3, ml_lp[l])
        yb = _natten(pm3, na_bias, na_qg[l], na_kg[l], l)
        yc = _gqa(pm3, cos_t, sin_t, ga_qg[l], ga_kg[l])
        x2 = _merge(x2, o_af.reshape(b * s, hw), o_ab.reshape(b * s, hw), h_df.reshape(b * s, hw),
                    h_db.reshape(b * s, hw), yb.reshape(b * s, hw), yc.reshape(b * s, hw), pm2,
                    dn_norm_g[l].reshape(1, LANES), ml_norm_g[l].reshape(1, LANES), wb_bf, wo_bf, l)
    return x2.reshape(b, s, d)
```

```python
import math

import jax
import jax.numpy as jnp
from jax import lax
from jax.experimental import pallas as pl
from jax.experimental.pallas import tpu as pltpu

F32 = jnp.float32
BF16 = jnp.bfloat16

GRID_W = 64
N_BRANCH = 4
BRANCH_W = 512
EPS = 1e-6
DN_HEADS, DN_DK, DN_DV, DN_CONV, DN_CHUNK = 4, 128, 128, 5, 64
NA_HEADS, NA_DH, NA_ROWS, NA_COLS = 8, 64, 8, 16
GA_HEADS, GA_KV_HEADS, GA_DH = 4, 2, 128
ROPE_THETA = 10000.0
ML_HEADS, ML_DK, ML_DV, ML_CHUNK = 4, 64, 128, 128

LANES = 128
SUBLANES = 8
VMEM_LIMIT_BYTES = 56 * 1024 * 1024

_O_A_QKV, _O_A_A, _O_A_B, _O_A_Z = 0, 1536, 1544, 1552
_O_B_QKV, _O_B_Z = 2064, 3600
_O_C_Q, _O_C_K, _O_C_V, _O_C_Z = 4112, 4624, 4880, 5136
_O_D_Q, _O_D_K, _O_D_V, _O_D_I, _O_D_F, _O_D_O, _O_D_Z = 5648, 5904, 6160, 6672, 6680, 6688, 7200
_O_GATE = 7712
_MAIN_SEGS = (
    ("a_qkv", _O_A_QKV, 1536), ("b_qkv", _O_B_QKV, 1536), ("a_z", _O_A_Z, 512), ("b_z", _O_B_Z, 512),
    ("gate", _O_GATE, 4096), ("c_q", _O_C_Q, 512), ("c_k", _O_C_K, 256), ("c_v", _O_C_V, 256),
    ("c_z", _O_C_Z, 512), ("d_q", _O_D_Q, 256), ("d_k", _O_D_K, 256), ("d_v", _O_D_V, 512),
    ("d_o", _O_D_O, 512), ("d_z", _O_D_Z, 512),
)
_OFF = {}
_o = 0
for _name, _src, _w in _MAIN_SEGS:
    _OFF[_name] = _o
    _o += _w
N_MAIN = _o
_SMALL_SRC = (_O_A_A, _O_A_B, _O_D_I, _O_D_F)
_L_AA, _L_AB, _L_DI, _L_DF = 0, 8, 16, 24

P_DTYPE = BF16
BLK = 128
NEG = -1e30


def _cparams(sem):
    return pltpu.CompilerParams(dimension_semantics=sem, vmem_limit_bytes=VMEM_LIMIT_BYTES)


def _sigmoid(x):
    return 0.5 * jnp.tanh(0.5 * x) + 0.5


def _silu(x):
    return x * _sigmoid(x)


def _softplus(x):
    return jnp.maximum(x, 0.0) + jnp.log(1.0 + jnp.exp(-jnp.abs(x)))


def _mm(a, b):
    return jnp.dot(a.astype(BF16), b.astype(BF16), preferred_element_type=F32)


def _mm_nt(a, b):
    return lax.dot_general(a.astype(BF16), b.astype(BF16), (((1,), (1,)), ((), ())),
                           preferred_element_type=F32)


def _mm_tn(a, b):
    return lax.dot_general(a.astype(BF16), b.astype(BF16), (((0,), (0,)), ((), ())),
                           preferred_element_type=F32)


def _mask_sum(mask, x):
    m = jnp.where(mask, 1.0, 0.0).astype(BF16)
    x1 = x.astype(BF16)
    r1 = x - x1.astype(F32)
    x2 = r1.astype(BF16)
    x3 = (r1 - x2.astype(F32)).astype(BF16)
    dot = lambda v: jnp.dot(m, v, preferred_element_type=F32)
    return dot(x1) + (dot(x2) + dot(x3))


def _chunk_masks(reverse, chunk):
    i = lax.broadcasted_iota(jnp.int32, (BLK, BLK), 0)
    j = lax.broadcasted_iota(jnp.int32, (BLK, BLK), 1)
    shift = int(math.log2(chunk))
    same = (i >> shift) == (j >> shift)
    if reverse:
        incl = same & (j >= i)
        strict = same & (j > i)
    else:
        incl = same & (j <= i)
        strict = same & (j < i)
    return same, incl, strict, (i == j)


def _chunk_totals(cum, chunk, reverse):
    pieces = []
    for c0 in range(0, BLK, chunk):
        r = c0 if reverse else c0 + chunk - 1
        pieces.append(jnp.broadcast_to(cum[r:r + 1, :], (chunk, cum.shape[1])))
    return jnp.concatenate(pieces, axis=0)


def _rows(start, size):
    if isinstance(start, int):
        return slice(start, start + size)
    return pl.ds(pl.multiple_of(start, size), size)


def _as_f32(mask):
    return jnp.where(mask, 1.0, 0.0).astype(F32)


def _inproj_kernel(x_ref, g_ref, w_ref, ws_ref, pm_ref, ps_ref, h_scr):
    @pl.when(pl.program_id(1) == 0)
    def _():
        x = x_ref[...]
        ms = jnp.mean(x * x, axis=-1, keepdims=True)
        h = (x * lax.rsqrt(ms + EPS) * g_ref[...]).astype(BF16)
        h_scr[...] = h
        ps_ref[...] = jnp.dot(h, ws_ref[...], preferred_element_type=F32)

    pm_ref[...] = jnp.dot(h_scr[...], w_ref[...], preferred_element_type=F32).astype(pm_ref.dtype)


def _inproj(x2d, g, w_main, w_small, layer, tm=1024, tn=N_MAIN // 4):
    m, d = x2d.shape
    tm = min(tm, m)
    return pl.pallas_call(
        _inproj_kernel,
        grid=(m // tm, N_MAIN // tn),
        in_specs=[
            pl.BlockSpec((tm, d), lambda i, j: (i, 0)),
            pl.BlockSpec((1, d), lambda i, j: (0, 0)),
            pl.BlockSpec((None, d, tn), lambda i, j: (layer, 0, j)),
            pl.BlockSpec((None, d, LANES), lambda i, j: (layer, 0, 0)),
        ],
        out_specs=[
            pl.BlockSpec((tm, tn), lambda i, j: (i, j)),
            pl.BlockSpec((tm, LANES), lambda i, j: (i, 0)),
        ],
        out_shape=[jax.ShapeDtypeStruct((m, N_MAIN), P_DTYPE), jax.ShapeDtypeStruct((m, LANES), F32)],
        scratch_shapes=[pltpu.VMEM((tm, d), BF16)],
        compiler_params=_cparams(("parallel", "arbitrary")),
        name="inproj",
    )(x2d, g, w_main, w_small)


GQA_TQ = 512


def _gqa_kernel(q_ref, k_ref, v_ref, z_ref, cos_ref, sin_ref, qg_ref, kg_ref, y_ref,
                q_scr, k_scr, v_scr, s0_scr, s1_scr, p0_scr, p1_scr, l0_scr, l1_scr):
    def norm_rope(x, g, cos, sin):
        ms = jnp.mean(x * x, axis=-1, keepdims=True)
        xn = x * lax.rsqrt(ms + EPS) * g
        lane = lax.broadcasted_iota(jnp.int32, xn.shape, 1)
        partner = jnp.where((lane & 63) < 32, pltpu.roll(xn, LANES - 32, 1), pltpu.roll(xn, 32, 1))
        return xn * cos + partner * sin

    s_len = k_ref.shape[0]
    group = GA_HEADS // GA_KV_HEADS
    assert group == 2
    scale = GA_DH ** -0.5
    n_blk = s_len // GQA_TQ
    s_bufs, p_bufs, l_bufs = (s0_scr, s1_scr), (p0_scr, p1_scr), (l0_scr, l1_scr)
    for kv in range(k_ref.shape[1] // GA_DH):
        _gqa_pipeline(kv, group, s_len, n_blk, scale, norm_rope, q_ref, k_ref, v_ref, z_ref, cos_ref, sin_ref,
                      qg_ref, kg_ref, y_ref, q_scr, k_scr, v_scr, s_bufs, p_bufs, l_bufs)


def _gqa_pipeline(kv, group, s_len, n_blk, scale, norm_rope, q_ref, k_ref, v_ref, z_ref, cos_ref, sin_ref,
                  qg_ref, kg_ref, y_ref, q_scr, k_scr, v_scr, s_bufs, p_bufs, l_bufs):
    kcols = slice(kv * GA_DH, (kv + 1) * GA_DH)
    k_scr[kv] = norm_rope(k_ref[:, kcols].astype(F32), kg_ref[...], cos_ref[...], sin_ref[...]).astype(BF16)
    v_scr[kv] = v_ref[:, kcols].astype(BF16)

    def stacked(head, blk):
        return _rows(head * s_len + blk * GQA_TQ, GQA_TQ)

    def head_cols(head):
        return slice((kv * group + head) * GA_DH, (kv * group + head + 1) * GA_DH)

    def prep(head, blk):
        rows = _rows(blk * GQA_TQ, GQA_TQ)
        q = norm_rope(q_ref[rows, head_cols(head)].astype(F32), qg_ref[...], cos_ref[rows, :], sin_ref[rows, :])
        q_scr[stacked(head, blk), :] = q.astype(BF16)

    def logits(head, blk):
        s_bufs[head][...] = _mm_nt(q_scr[stacked(head, blk), :], k_scr[kv]) * scale

    def softmax(slot):
        s = s_bufs[slot][...]
        p = jnp.exp(s - jnp.max(s, axis=-1, keepdims=True))
        l_bufs[slot][...] = jnp.broadcast_to(jnp.sum(p, axis=-1, keepdims=True), (GQA_TQ, GA_DH))
        p_bufs[slot][...] = p.astype(BF16)

    def weighted(head, blk):
        rows = _rows(blk * GQA_TQ, GQA_TQ)
        cols = head_cols(head)
        o = _mm(p_bufs[head][...], v_scr[kv]) / l_bufs[head][...]
        y_ref[rows, cols] = (o * _silu(z_ref[rows, cols].astype(F32))).astype(y_ref.dtype)

    prep(0, 0)
    prep(1, 0)
    prep(0, 1)
    logits(0, 0)
    logits(1, 0)
    softmax(0)

    def body(j, carry):
        logits(0, j + 1)
        softmax(1)
        weighted(0, j)
        prep(1, j + 1)
        logits(1, j + 1)
        softmax(0)
        weighted(1, j)
        prep(0, jnp.minimum(j + 2, n_blk - 1))
        return carry

    lax.fori_loop(0, n_blk - 1, body, 0)
    softmax(1)
    weighted(0, n_blk - 1)
    weighted(1, n_blk - 1)


def _gqa(pm3, cos_t, sin_t, qg, kg):
    b, s, _ = pm3.shape
    qw = GA_HEADS * GA_DH
    kw = GA_KV_HEADS * GA_DH
    group = GA_HEADS // GA_KV_HEADS
    return pl.pallas_call(
        _gqa_kernel,
        grid=(b,),
        in_specs=[
            pl.BlockSpec((None, s, qw), lambda i: (i, 0, _OFF["c_q"] // qw)),
            pl.BlockSpec((None, s, kw), lambda i: (i, 0, _OFF["c_k"] // kw)),
            pl.BlockSpec((None, s, kw), lambda i: (i, 0, _OFF["c_v"] // kw)),
            pl.BlockSpec((None, s, qw), lambda i: (i, 0, _OFF["c_z"] // qw)),
            pl.BlockSpec((s, GA_DH), lambda i: (0, 0)),
            pl.BlockSpec((s, GA_DH), lambda i: (0, 0)),
            pl.BlockSpec((1, GA_DH), lambda i: (0, 0)),
            pl.BlockSpec((1, GA_DH), lambda i: (0, 0)),
        ],
        out_specs=pl.BlockSpec((None, s, qw), lambda i: (i, 0, 0)),
        out_shape=jax.ShapeDtypeStruct((b, s, BRANCH_W), BF16),
        scratch_shapes=[pltpu.VMEM((group * s, GA_DH), BF16), pltpu.VMEM((GA_KV_HEADS, s, GA_DH), BF16),
                        pltpu.VMEM((GA_KV_HEADS, s, GA_DH), BF16),
                        pltpu.VMEM((GQA_TQ, s), F32), pltpu.VMEM((GQA_TQ, s), F32),
                        pltpu.VMEM((GQA_TQ, s), BF16), pltpu.VMEM((GQA_TQ, s), BF16),
                        pltpu.VMEM((GQA_TQ, GA_DH), F32), pltpu.VMEM((GQA_TQ, GA_DH), F32)],
        compiler_params=_cparams(("parallel",)),
        name="gqa",
    )(pm3, pm3, pm3, pm3, cos_t, sin_t, qg, kg)


NA_ROW_UNROLL = 16


def _na_kernel(q_ref, k_ref, v_ref, z_ref, bias_ref, qg_ref, kg_ref, y_ref, q_scr, k_scr):
    s_len = q_ref.shape[0]
    rows = s_len // GRID_W
    kr = min(NA_ROWS, rows)
    hi = lax.broadcasted_iota(jnp.int32, (2 * NA_DH, 2 * NA_DH), 0) >= NA_DH
    hj = lax.broadcasted_iota(jnp.int32, (2 * NA_DH, 2 * NA_DH), 1) >= NA_DH
    same_head = jnp.where(hi == hj, 1.0, 0.0).astype(BF16)

    def rms_pair(x, g):
        x2 = x * x
        x2_hi = x2.astype(BF16)
        x2_lo = (x2 - x2_hi.astype(F32)).astype(BF16)
        ssq = (jnp.dot(x2_hi, same_head, preferred_element_type=F32)
               + jnp.dot(x2_lo, same_head, preferred_element_type=F32))
        return x * lax.rsqrt(ssq * (1.0 / NA_DH) + EPS) * g

    scale = NA_DH ** -0.5
    assert math.log2(scale).is_integer()
    nkeys = kr * GRID_W
    first = lax.broadcasted_iota(jnp.int32, (s_len, 2 * NA_DH), 1) < NA_DH
    first_q = lax.broadcasted_iota(jnp.int32, (GRID_W, 2 * NA_DH), 1) < NA_DH
    for pair in range(q_ref.shape[1] // (2 * NA_DH)):
        pc = slice(pair * 2 * NA_DH, (pair + 1) * 2 * NA_DH)
        qn = rms_pair(q_ref[:, pc].astype(F32), qg_ref[...]) * scale
        q_scr[0] = jnp.where(first, qn, 0.0).astype(BF16)
        q_scr[1] = jnp.where(first, 0.0, qn).astype(BF16)
        k_scr[...] = rms_pair(k_ref[:, pc].astype(F32), kg_ref[...]).astype(BF16)

        def body(it, carry, pair=pair, pc=pc):
            units = []
            for u in range(NA_ROW_UNROLL):
                r = it * NA_ROW_UNROLL + u
                r0 = jnp.clip(r - kr // 2, 0, rows - kr)
                var = r0 - r + (NA_ROWS - 1)
                qrows = pl.ds(pl.multiple_of(r * GRID_W, GRID_W), GRID_W)
                krows = pl.ds(pl.multiple_of(r0 * GRID_W, GRID_W), nkeys)
                units.append((var, qrows, krows))
            logits = [_mm_nt(jnp.concatenate([q_scr[0, qrows, :], q_scr[1, qrows, :]], axis=0), k_scr[krows, :])
                      + bias_ref[pair, var] for var, qrows, krows in units]
            probs = [jnp.exp(s - jnp.max(s, axis=-1, keepdims=True)) for s in logits]
            sums = [jnp.sum(p, axis=-1, keepdims=True) for p in probs]
            outs = [_mm(p, v_ref[krows, pc]) / l for p, l, (_, _, krows) in zip(probs, sums, units)]
            for o, (_, qrows, _) in zip(outs, units):
                o_pair = jnp.where(first_q, o[:GRID_W], o[GRID_W:])
                y_ref[qrows, pc] = (o_pair * _silu(z_ref[qrows, pc].astype(F32))).astype(y_ref.dtype)
            return carry

        lax.fori_loop(0, rows // NA_ROW_UNROLL, body, 0)


def _na_bias_table(rpb, rows):
    kr = min(NA_ROWS, rows)
    c = jnp.arange(GRID_W)
    c0 = jnp.clip(c - NA_COLS // 2, 0, GRID_W - NA_COLS)
    in_win = (c[None, :] >= c0[:, None]) & (c[None, :] < c0[:, None] + NA_COLS)
    col_off = jnp.clip(c[None, :] - c[:, None], -(NA_COLS - 1), NA_COLS - 1) + NA_COLS - 1
    t = jnp.where(in_win, rpb[..., col_off], NEG)
    lead, n_heads, n_off = rpb.shape[:-3], rpb.shape[-3], rpb.shape[-2]
    t = t.reshape(*lead, n_heads // 2, 2, n_off, GRID_W, GRID_W)
    t = jnp.swapaxes(t, -4, -3).reshape(*lead, n_heads // 2, n_off, 2 * GRID_W, GRID_W)
    tv = jnp.concatenate([t[..., j:j + NA_ROWS, :, :] for j in range(kr)], axis=-1)
    return tv.astype(F32)


def _natten(pm3, bias, qg, kg, layer):
    b, s, _ = pm3.shape
    pw = 2 * NA_DH
    hw = NA_HEADS * NA_DH
    return pl.pallas_call(
        _na_kernel,
        grid=(b,),
        in_specs=[
            pl.BlockSpec((None, s, hw), lambda i: (i, 0, _OFF["b_qkv"] // hw)),
            pl.BlockSpec((None, s, hw), lambda i: (i, 0, _OFF["b_qkv"] // hw + 1)),
            pl.BlockSpec((None, s, hw), lambda i: (i, 0, _OFF["b_qkv"] // hw + 2)),
            pl.BlockSpec((None, s, hw), lambda i: (i, 0, _OFF["b_z"] // hw)),
            pl.BlockSpec((None,) + bias.shape[1:], lambda i: (layer, 0, 0, 0, 0)),
            pl.BlockSpec((1, pw), lambda i: (0, 0)),
            pl.BlockSpec((1, pw), lambda i: (0, 0)),
        ],
        out_specs=pl.BlockSpec((None, s, hw), lambda i: (i, 0, 0)),
        out_shape=jax.ShapeDtypeStruct((b, s, BRANCH_W), BF16),
        scratch_shapes=[pltpu.VMEM((2, s, pw), BF16), pltpu.VMEM((s, pw), BF16)],
        compiler_params=_cparams(("parallel",)),
        name="natten",
    )(pm3, pm3, pm3, pm3, bias, qg, kg)


INV_BASE = 8
PRE_BLK = 4 * BLK


def _inverse_level_masks():
    i = lax.broadcasted_iota(jnp.int32, (BLK, BLK), 0)
    j = lax.broadcasted_iota(jnp.int32, (BLK, BLK), 1)
    same = lambda size: (i >> int(math.log2(size))) == (j >> int(math.log2(size)))
    base = same(INV_BASE)
    joins = []
    size = INV_BASE
    while size < DN_CHUNK:
        joins.append(same(2 * size) & jnp.logical_not(same(size)))
        size *= 2
    return base, joins


def _tri_inverses(l_mats, eyes, level_masks):
    base, joins = level_masks
    ps = [jnp.where(base, -l, 0.0) for l in l_mats]
    ts = [eye + p for eye, p in zip(eyes, ps)]
    for _ in range(int(math.log2(INV_BASE)) - 1):
        ps = [_mm(p, p) for p in ps]
        ts = [t + _mm(t, p) for t, p in zip(ts, ps)]
    for join in joins:
        mids = [_mm(jnp.where(join, l, 0.0), t) for l, t in zip(l_mats, ts)]
        ts = [t - _mm(t, mid) for t, mid in zip(ts, mids)]
    return ts


DN_HW = DN_HEADS * LANES
PK_W, PK_QD, PK_KD, PK_QK = (i * DN_HW for i in range(4))


def _dn_pre_kernel(qkv_ref, sm_ref, cw_ref, lp_ref, uf_ref, ub_ref, pkf_ref, pkb_ref, gtf_ref, gtb_ref):
    n = pl.program_id(1)
    s_len = qkv_ref.shape[0]
    assert qkv_ref.dtype == BF16
    halo = 2 * SUBLANES
    n_sub = sm_ref.shape[0] // BLK
    chunks_per_blk = BLK // DN_CHUNK
    out_row = lax.broadcasted_iota(jnp.int32, (BLK, BLK + 2 * halo), 0)
    in_row = lax.broadcasted_iota(jnp.int32, (BLK, BLK + 2 * halo), 1)
    shifts = {j: jnp.where(in_row == out_row + (halo + j - DN_CONV // 2), 1.0, 0.0).astype(BF16)
              for j in range(DN_CONV) if j != DN_CONV // 2}
    dir_masks = [_chunk_masks(d == 1, DN_CHUNK) for d in range(2)]
    level_masks = _inverse_level_masks()
    outs = ((uf_ref, pkf_ref, gtf_ref), (ub_ref, pkb_ref, gtb_ref))

    def l2n(x):
        return x * lax.rsqrt(jnp.sum(x * x, axis=-1, keepdims=True) + EPS)

    blocks = []
    for sub in range(n_sub):
        t0 = pl.multiple_of((n * n_sub + sub) * BLK, BLK)
        pstart = pl.multiple_of(jnp.maximum(t0 - halo, 0), halo)
        nstart = pl.multiple_of(jnp.minimum(t0 + BLK, s_len - halo), halo)
        prev = qkv_ref[pl.ds(pstart, halo), :]
        prev = jnp.where(t0 > 0, prev, jnp.zeros_like(prev))
        cur = qkv_ref[pl.ds(t0, BLK), :]
        nxt = qkv_ref[pl.ds(nstart, halo), :]
        nxt = jnp.where(t0 + BLK < s_len, nxt, jnp.zeros_like(nxt))
        xw = jnp.concatenate([prev, cur, nxt], axis=0)
        conv = cur.astype(F32) * cw_ref[DN_CONV // 2:DN_CONV // 2 + 1, :]
        for j, shift in shifts.items():
            conv = conv + jnp.dot(shift, xw, preferred_element_type=F32) * cw_ref[j:j + 1, :]
        conv = _silu(conv)

        rows = slice(sub * BLK, (sub + 1) * BLK)
        sm = sm_ref[rows, :]
        g_all = -jnp.exp(lp_ref[0:1, :]) * _softplus(sm + lp_ref[1:2, :])
        beta_all = _sigmoid(sm)
        per_dir = []
        for d in range(2):
            same, incl, strict, eye_b = dir_masks[d]
            gc = _mask_sum(incl, g_all)
            tot = _mask_sum(same, g_all)
            per_dir.append((incl, strict, _as_f32(eye_b), gc, gc.T, tot))
        blocks.append((sub, rows, conv, beta_all, per_dir))

    heads = []
    for sub, rows, conv, beta_all, per_dir in blocks:
        for h in range(DN_HEADS):
            q = l2n(conv[:, h * DN_DK:(h + 1) * DN_DK]) * (DN_DK ** -0.5)
            k = l2n(conv[:, (DN_HEADS + h) * DN_DK:(DN_HEADS + h + 1) * DN_DK])
            v = conv[:, 2 * DN_HEADS * DN_DK + h * DN_DV:2 * DN_HEADS * DN_DK + (h + 1) * DN_DV]
            heads.append((sub, rows, beta_all, per_dir, h, q, k, v))
    kks = [_mm_nt(hd[6], hd[6]) for hd in heads]
    qks = [_mm_nt(hd[5], hd[6]) for hd in heads]

    l_mats, rhss, eyes, slots = [], [], [], []
    for (sub, rows, beta_all, per_dir, h, q, k, v), kk, qk in zip(heads, kks, qks):
        hc = slice(h * LANES, (h + 1) * LANES)
        for d in range(2):
            incl, strict, eye_f, gc, gct, tot = per_dir[d]
            u_ref, pk_ref, gt_ref = outs[d]
            pk = lambda off: slice(off + h * LANES, off + (h + 1) * LANES)
            c = _L_AA + d * DN_HEADS + h
            gcol = gc[:, c:c + 1]
            grow = gct[c:c + 1, :]
            tcol = tot[:, c:c + 1]
            beta = beta_all[:, _L_AB + d * DN_HEADS + h:_L_AB + d * DN_HEADS + h + 1]
            decay = jnp.where(incl, jnp.exp(jnp.where(incl, gcol - grow, 0.0)), 0.0)
            l_mats.append(jnp.where(strict, beta * kk * decay, 0.0))
            egc = jnp.exp(gcol)
            rhss.append(jnp.concatenate([v * beta, k * (beta * egc)], axis=-1).astype(BF16))
            eyes.append(eye_f)
            slots.append((rows, h, d))
            pk_ref[rows, pk(PK_QD)] = (q * egc).astype(pk_ref.dtype)
            pk_ref[rows, pk(PK_KD)] = (k * jnp.exp(tcol - gcol)).astype(pk_ref.dtype)
            pk_ref[rows, pk(PK_QK)] = (qk * decay).astype(pk_ref.dtype)
            gtot = jnp.exp(tcol)
            for ci in range(chunks_per_blk):
                g0 = (sub * chunks_per_blk + ci) * SUBLANES
                gt_ref[g0:g0 + SUBLANES, hc] = jnp.broadcast_to(
                    gtot[ci * DN_CHUNK:ci * DN_CHUNK + SUBLANES, :], (SUBLANES, LANES))

    t_invs = _tri_inverses(l_mats, eyes, level_masks)
    sols = [_mm(t, rhs) for t, rhs in zip(t_invs, rhss)]
    for (rows, h, d), sol in zip(slots, sols):
        u_ref, pk_ref = outs[d][0], outs[d][1]
        u_ref[rows, h * LANES:(h + 1) * LANES] = sol[:, :DN_DV]
        pk_ref[rows, PK_W + h * LANES:PK_W + (h + 1) * LANES] = sol[:, DN_DV:].astype(pk_ref.dtype)


def _dn_pre(pm3, ps3, conv_w8, lane_params):
    b, s, _ = pm3.shape
    nblk = s // PRE_BLK
    gt_rows = PRE_BLK // DN_CHUNK * SUBLANES
    wq = DN_HEADS * (2 * DN_DK + DN_DV)
    hw = DN_HEADS * LANES
    tok = lambda i, j: (i, j, 0)
    big = lambda dt: jax.ShapeDtypeStruct((b, s, hw), dt)
    gts = jax.ShapeDtypeStruct((b, nblk * gt_rows, hw), F32)
    bs_tok = pl.BlockSpec((None, PRE_BLK, hw), tok)
    bs_gt = pl.BlockSpec((None, gt_rows, hw), tok)
    bs_pk = pl.BlockSpec((None, PRE_BLK, 4 * hw), tok)
    packed = jax.ShapeDtypeStruct((b, s, 4 * hw), BF16)
    return pl.pallas_call(
        _dn_pre_kernel,
        grid=(b, nblk),
        in_specs=[
            pl.BlockSpec((None, s, wq), lambda i, j: (i, 0, _OFF["a_qkv"] // wq)),
            pl.BlockSpec((None, PRE_BLK, LANES), tok),
            pl.BlockSpec((SUBLANES, wq), lambda i, j: (0, 0)),
            pl.BlockSpec((SUBLANES, LANES), lambda i, j: (0, 0)),
        ],
        out_specs=[bs_tok, bs_tok, bs_pk, bs_pk, bs_gt, bs_gt],
        out_shape=[big(F32), big(F32), packed, packed, gts, gts],
        compiler_params=_cparams(("parallel", "arbitrary")),
        name="dn_pre",
    )(pm3, ps3, conv_w8, lane_params)


SCAN_BLK = 4 * BLK


def _zero_at_sequence_start(*scratch):
    @pl.when(pl.program_id(1) == 0)
    def _():
        for ref in scratch:
            ref[...] = jnp.zeros_like(ref)


def _dn_scan_body(uf_ref, pkf_ref, gtf_ref, ub_ref, pkb_ref, gtb_ref, of_ref, ob_ref, st_scr):
    per_blk = BLK // DN_CHUNK
    nchunk = SCAN_BLK // DN_CHUNK
    zeros_c = jnp.zeros((DN_CHUNK, DN_DV), F32)
    streams = ((uf_ref, pkf_ref, gtf_ref, of_ref, range(nchunk)),
               (ub_ref, pkb_ref, gtb_ref, ob_ref, range(nchunk - 1, -1, -1)))
    chains = [(d, h) + streams[d] for d in range(2) for h in range(DN_HEADS)]
    states = [st_scr[d * DN_HEADS + h] for d, h, *_ in chains]
    for step in range(nchunk):
        rs, v_pads = [], []
        for (d, h, u_ref, pk_ref, gt_ref, o_ref, order), state in zip(chains, states):
            rows = slice(order[step] * DN_CHUNK, (order[step] + 1) * DN_CHUNK)
            w = pk_ref[rows, PK_W + h * LANES:PK_W + (h + 1) * LANES]
            qd = pk_ref[rows, PK_QD + h * LANES:PK_QD + (h + 1) * LANES]
            rs.append(_mm(jnp.concatenate([w, qd], axis=0), state))
        for (d, h, u_ref, pk_ref, gt_ref, o_ref, order), r in zip(chains, rs):
            ci = order[step]
            rows = slice(ci * DN_CHUNK, (ci + 1) * DN_CHUNK)
            parts = [zeros_c] * per_blk
            parts[ci % per_blk] = u_ref[rows, h * LANES:(h + 1) * LANES] - r[:DN_CHUNK]
            v_pads.append(jnp.concatenate(parts, axis=0))
        new_states = []
        for (d, h, u_ref, pk_ref, gt_ref, o_ref, order), r, v_pad, state in zip(chains, rs, v_pads, states):
            hc = slice(h * LANES, (h + 1) * LANES)
            ci = order[step]
            rows = slice(ci * DN_CHUNK, (ci + 1) * DN_CHUNK)
            blk_rows = slice((ci // per_blk) * BLK, (ci // per_blk + 1) * BLK)
            o_ref[rows, hc] = r[DN_CHUNK:] + _mm(pk_ref[rows, PK_QK + h * LANES:PK_QK + (h + 1) * LANES], v_pad)
            gt = gt_ref[ci * SUBLANES:ci * SUBLANES + 1, hc]
            kd = pk_ref[blk_rows, PK_KD + h * LANES:PK_KD + (h + 1) * LANES]
            new_states.append(state * gt + _mm_tn(kd, v_pad))
        states = new_states
    for (d, h, *_), state in zip(chains, states):
        st_scr[d * DN_HEADS + h] = state


def _dn_scan_parts(pre):
    uf, ub, pkf, pkb, gtf, gtb = pre
    b, s, hw = uf.shape
    nblk = s // SCAN_BLK
    gt_rows = SCAN_BLK // DN_CHUNK * SUBLANES
    fwd = lambda i, j: (i, j, 0)
    bwd = lambda i, j: (i, nblk - 1 - j, 0)
    def specs(imap):
        return [pl.BlockSpec((None, SCAN_BLK, hw), imap), pl.BlockSpec((None, SCAN_BLK, 4 * hw), imap),
                pl.BlockSpec((None, gt_rows, hw), imap)]
    return dict(
        grid=(b, nblk), args=(uf, pkf, gtf, ub, pkb, gtb), in_specs=specs(fwd) + specs(bwd),
        out_specs=[pl.BlockSpec((None, SCAN_BLK, hw), fwd), pl.BlockSpec((None, SCAN_BLK, hw), bwd)],
        out_shape=[jax.ShapeDtypeStruct((b, s, hw), F32)] * 2,
        scratch_shapes=[pltpu.VMEM((2 * DN_HEADS, DN_DK, DN_DV), F32)])


def _call_parts(kernel_fn, name, parts):
    return pl.pallas_call(
        kernel_fn, grid=parts["grid"], in_specs=parts["in_specs"], out_specs=parts["out_specs"],
        out_shape=parts["out_shape"], scratch_shapes=parts["scratch_shapes"],
        compiler_params=_cparams(("parallel", "arbitrary")), name=name)(*parts["args"])


def _scan_and_mlstm(pre, pm3, ps3, ml_lane_params):
    dn, ml = _dn_scan_parts(pre), _mlstm_parts(pm3, ps3, ml_lane_params)
    assert dn["grid"] == ml["grid"]
    n_in = (len(dn["args"]), len(ml["args"]))
    n_out = (len(dn["out_shape"]), len(ml["out_shape"]))

    def both(*refs):
        ins, rest = refs[:sum(n_in)], refs[sum(n_in):]
        outs, scr = rest[:sum(n_out)], rest[sum(n_out):]
        n_dn_scr = len(dn["scratch_shapes"])
        _zero_at_sequence_start(*scr)
        _dn_scan_body(*ins[:n_in[0]], *outs[:n_out[0]], *scr[:n_dn_scr])
        _ml_body(*ins[n_in[0]:], *outs[n_out[0]:], *scr[n_dn_scr:])

    fused = {k: dn[k] + ml[k] for k in ("args", "in_specs", "out_specs", "out_shape", "scratch_shapes")}
    fused["grid"] = dn["grid"]
    return _call_parts(both, "scan_mlstm", fused)


ML_AUG = 2 * LANES
ML_GROUP = 4
ML_BLK = 4 * BLK


def _ml_body(qf_ref, kf_ref, vf_ref, smf_ref, qb_ref, kb_ref, vb_ref, smb_ref, lp_ref,
             hf_ref, hb_ref, c_scr, m_scr):
    nchunk = BLK // ML_CHUNK
    ones_col = jnp.ones((BLK, LANES), BF16)
    zeros_aug = jnp.zeros((ML_CHUNK, ML_AUG), BF16)
    streams = ((qf_ref, kf_ref, vf_ref, smf_ref, hf_ref, range(nchunk)),
               (qb_ref, kb_ref, vb_ref, smb_ref, hb_ref, range(nchunk - 1, -1, -1)))
    lanes = lambda col: jnp.broadcast_to(col, (col.shape[0], LANES))
    n_sub = smf_ref.shape[0] // BLK
    units = [(d, p if d == 0 else n_sub - 1 - p) for p in range(n_sub) for d in range(2)]
    for d, sub in units:
        _ml_block(d, sub, streams[d], lp_ref, c_scr, m_scr, lanes, ones_col, zeros_aug, nchunk)


def _ml_block(d, sub, stream, lp_ref, c_scr, m_scr, lanes, ones_col, zeros_aug, nchunk):
    q_ref, k_ref, v_ref, sm_ref, h_ref, order = stream
    blk_rows = slice(sub * BLK, (sub + 1) * BLK)
    same, incl, _, _ = _chunk_masks(d == 1, ML_CHUNK)
    sm = sm_ref[blk_rows, :]
    ig_all = sm + lp_ref[0:1, :]
    x = sm + lp_ref[1:2, :]
    lf_all = jnp.minimum(x, 0.0) - jnp.log(1.0 + jnp.exp(-jnp.abs(x)))
    lf_all = pltpu.roll(lf_all, LANES - (_L_DF - _L_DI), 1)
    bc_all = _mask_sum(incl, lf_all)
    tot_all = _chunk_totals(bc_all, ML_CHUNK, reverse=(d == 1))
    a_all = ig_all - bc_all
    mwa_all = jnp.concatenate(
        [jnp.broadcast_to(jnp.max(a_all[ci * ML_CHUNK:(ci + 1) * ML_CHUNK], axis=0, keepdims=True),
                          (ML_CHUNK, LANES)) for ci in range(nchunk)], axis=0)
    a_t, w_all, mw_all = a_all.T, jnp.exp(a_all - mwa_all), tot_all + mwa_all

    for chains in [[(d, h) for h in range(g, g + ML_GROUP)] for g in range(0, ML_HEADS, ML_GROUP)]:
        ins = []
        for _, h in chains:
            q = q_ref[blk_rows, h * ML_DK:(h + 1) * ML_DK].astype(BF16)
            k = k_ref[blk_rows, h * ML_DK:(h + 1) * ML_DK].astype(F32) * (ML_DK ** -0.5)
            v_aug = jnp.concatenate([v_ref[blk_rows, h * ML_DV:(h + 1) * ML_DV].astype(BF16), ones_col], axis=-1)
            ins.append((q, k, v_aug))
        qks = [_mm_nt(q, k) for q, k, _ in ins]

        mids = []
        for (d, h), (q, k, v_aug), qk in zip(chains, ins, qks):
            c = _L_DI + d * ML_HEADS + h
            b_l = lanes(bc_all[:, c:c + 1])
            dlog = jnp.where(incl, b_l + a_t[c:c + 1, :], NEG)
            m_intra = lanes(jnp.max(dlog, axis=-1, keepdims=True))
            s_intra = qk * jnp.exp(dlog - m_intra)
            wk = (k * lanes(w_all[:, c:c + 1])[:, :ML_DK]).astype(BF16)
            mids.append((b_l, m_intra, s_intra.astype(BF16), wk))
        p_intras = [_mm(s_b, v_aug) for (_, _, s_b, _), (_, _, v_aug) in zip(mids, ins)]
        kvs = []
        for (_, _, _, wk), (_, _, v_aug) in zip(mids, ins):
            per_chunk = []
            for ci in range(nchunk):
                parts = [zeros_aug] * nchunk
                parts[ci] = v_aug[ci * ML_CHUNK:(ci + 1) * ML_CHUNK]
                per_chunk.append(_mm_tn(wk, jnp.concatenate(parts, axis=0)))
            kvs.append(per_chunk)

        c_sts = [c_scr[d * ML_HEADS + h] for d, h in chains]
        m_sts = [m_scr[d * ML_HEADS + h][0:1, :] for d, h in chains]
        for step in range(nchunk):
            qcs = []
            ci = order[step]
            for (q, _, _), c_st in zip(ins, c_sts):
                qcs.append(_mm(q[ci * ML_CHUNK:(ci + 1) * ML_CHUNK], c_st))
            for idx, (_, h) in enumerate(chains):
                b_l, m_intra, _, _ = mids[idx]
                c = _L_DI + d * ML_HEADS + h
                rows = slice(ci * ML_CHUNK, (ci + 1) * ML_CHUNK)
                out_rows = slice(sub * BLK + ci * ML_CHUNK, sub * BLK + (ci + 1) * ML_CHUNK)
                r8 = slice(ci * ML_CHUNK, ci * ML_CHUNK + SUBLANES)
                m_st, c_st, qc = m_sts[idx], c_sts[idx], qcs[idx]
                m_inter = b_l[rows] + m_st
                m_i = jnp.maximum(m_intra[rows], m_inter)
                f_i = jnp.exp(m_intra[rows] - m_i)
                inter = jnp.exp(m_inter - m_i)
                both = (jnp.concatenate([inter, inter], axis=-1) * qc
                        + jnp.concatenate([f_i, f_i], axis=-1) * p_intras[idx][rows])
                numer, denom = both[:, :ML_DV], both[:, ML_DV:]
                h_ref[out_rows, h * ML_DV:(h + 1) * ML_DV] = numer / jnp.maximum(jnp.abs(denom), jnp.exp(-m_i))
                tot_c = lanes(tot_all[r8, c:c + 1])[0:1]
                mw_c = lanes(mw_all[r8, c:c + 1])[0:1]
                m_new = jnp.maximum(tot_c + m_st, mw_c)
                dec = jnp.exp(tot_c + m_st - m_new)
                gain = jnp.exp(mw_c - m_new)
                c_sts[idx] = (jnp.concatenate([dec, dec], axis=-1) * c_st
                              + jnp.concatenate([gain, gain], axis=-1) * kvs[idx][ci])
                m_sts[idx] = m_new
        for idx, (d, h) in enumerate(chains):
            c_scr[d * ML_HEADS + h] = c_sts[idx]
            m_scr[d * ML_HEADS + h] = jnp.broadcast_to(m_sts[idx], (SUBLANES, LANES))


def _mlstm_parts(pm3, ps3, lane_params):
    b, s, _ = pm3.shape
    nblk = s // ML_BLK
    qw = ML_HEADS * ML_DK
    vw = ML_HEADS * ML_DV
    def specs(tmap):
        blk = lambda j: tmap(j)
        return [
            pl.BlockSpec((None, ML_BLK, qw), lambda i, j: (i, blk(j), _OFF["d_q"] // qw)),
            pl.BlockSpec((None, ML_BLK, qw), lambda i, j: (i, blk(j), _OFF["d_k"] // qw)),
            pl.BlockSpec((None, ML_BLK, vw), lambda i, j: (i, blk(j), _OFF["d_v"] // vw)),
            pl.BlockSpec((None, ML_BLK, LANES), lambda i, j: (i, blk(j), 0)),
        ]
    fwd = lambda j: j
    bwd = lambda j: nblk - 1 - j
    return dict(
        grid=(b, nblk), args=(pm3, pm3, pm3, ps3, pm3, pm3, pm3, ps3, lane_params),
        in_specs=specs(fwd) + specs(bwd) + [pl.BlockSpec((SUBLANES, LANES), lambda i, j: (0, 0))],
        out_specs=[pl.BlockSpec((None, ML_BLK, vw), lambda i, j: (i, j, 0)),
                   pl.BlockSpec((None, ML_BLK, vw), lambda i, j: (i, nblk - 1 - j, 0))],
        out_shape=[jax.ShapeDtypeStruct((b, s, vw), F32)] * 2,
        scratch_shapes=[pltpu.VMEM((2 * ML_HEADS, ML_DK, ML_AUG), F32),
                        pltpu.VMEM((2 * ML_HEADS, SUBLANES, LANES), F32)])


def _merge_kernel(x_ref, af_ref, ab_ref, df_ref, db_ref, yb_ref, yc_ref, az_ref, dz_ref, do_ref, gl_ref,
                  ag_ref, dg_ref, wb_ref, wo_ref, o_ref):
    d = x_ref.shape[-1]

    def head_rms(x, g):
        outs = []
        for h in range(x.shape[-1] // LANES):
            xh = x[:, h * LANES:(h + 1) * LANES]
            ms = jnp.mean(xh * xh, axis=-1, keepdims=True)
            outs.append(xh * lax.rsqrt(ms + EPS) * g)
        return jnp.concatenate(outs, axis=-1)

    ya = head_rms(af_ref[...] + ab_ref[...], ag_ref[...]) * _silu(az_ref[...].astype(F32))
    yd = _sigmoid(do_ref[...].astype(F32)) * head_rms(df_ref[...] + db_ref[...], dg_ref[...])
    yd = yd * _silu(dz_ref[...].astype(F32))
    twice = None
    for i, y in enumerate((ya.astype(BF16), yb_ref[...], yc_ref[...], yd.astype(BF16))):
        proj = jnp.dot(y, wb_ref[i], preferred_element_type=F32)
        term = proj + jnp.tanh(0.5 * gl_ref[:, i * d:(i + 1) * d].astype(F32)) * proj
        twice = term if twice is None else twice + term
    merged = (0.5 * twice).astype(BF16)
    o_ref[...] = x_ref[...] + jnp.dot(merged, wo_ref[...], preferred_element_type=F32)


def _merge(x2d, af, ab, df, db, yb, yc, pm2, ag, dg, wb, wo, layer, tm=512):
    m, d = x2d.shape
    gw = N_BRANCH * d
    w = BRANCH_W
    tok = pl.BlockSpec((tm, w), lambda i: (i, 0))
    col = lambda name: pl.BlockSpec((tm, w), lambda i: (i, _OFF[name] // w))
    vec = pl.BlockSpec((1, LANES), lambda i: (0, 0))
    return pl.pallas_call(
        _merge_kernel,
        grid=(m // tm,),
        in_specs=[
            pl.BlockSpec((tm, d), lambda i: (i, 0)),
            tok, tok, tok, tok, tok, tok,
            col("a_z"), col("d_z"), col("d_o"),
            pl.BlockSpec((tm, gw), lambda i: (i, _OFF["gate"] // gw)),
            vec, vec,
            pl.BlockSpec((None, N_BRANCH, w, d), lambda i: (layer, 0, 0, 0)),
            pl.BlockSpec((None, d, d), lambda i: (layer, 0, 0)),
        ],
        out_specs=pl.BlockSpec((tm, d), lambda i: (i, 0)),
        out_shape=jax.ShapeDtypeStruct((m, d), F32),
        compiler_params=_cparams(("parallel",)),
        name="merge",
    )(x2d, af, ab, df, db, yb, yc, pm2, pm2, pm2, pm2, ag, dg, wb, wo)


def _rope_lane_tables(s):
    t = jnp.arange(s)
    row = (t // GRID_W).astype(F32)
    col = (t % GRID_W).astype(F32)
    m = GA_DH // 4
    inv = ROPE_THETA ** (-jnp.arange(m, dtype=F32) / m)
    ar = row[:, None] * inv
    ac = col[:, None] * inv
    cos_t = jnp.concatenate([jnp.cos(ar), jnp.cos(ar), jnp.cos(ac), jnp.cos(ac)], axis=-1)
    sin_t = jnp.concatenate([-jnp.sin(ar), jnp.sin(ar), -jnp.sin(ac), jnp.sin(ac)], axis=-1)
    return cos_t.astype(F32), sin_t.astype(F32)


def _main_columns(w):
    return jnp.concatenate([w[..., o:o + wd] for _, o, wd in _MAIN_SEGS], axis=-1)


def _lane_tiles(rows):
    padded = []
    for off, vals in rows:
        vals = vals.reshape(vals.shape[0], 1, -1).astype(F32)
        padded.append(jnp.pad(vals, ((0, 0), (0, 0), (off, LANES - off - vals.shape[-1]))))
    tiles = jnp.concatenate(padded, axis=1)
    return jnp.pad(tiles, ((0, 0), (0, SUBLANES - len(rows)), (0, 0)))


def kernel(x, norm_g, w_in, conv_a, dn_a_log, dn_dt_bias, dn_norm_g, na_q_norm, na_k_norm, na_rpb,
           ga_q_norm, ga_k_norm, ml_i_bias, ml_f_bias, ml_norm_g, w_branch, w_out):
    b, s, d = x.shape
    depth = w_in.shape[0]
    hw = BRANCH_W
    cos_t, sin_t = _rope_lane_tables(s)
    w_main = _main_columns(w_in).astype(BF16)
    w_small = jnp.pad(jnp.concatenate([w_in[:, :, o:o + 8] for o in _SMALL_SRC], axis=2),
                      ((0, 0), (0, 0), (0, LANES - 8 * len(_SMALL_SRC)))).astype(BF16)
    conv8 = jnp.pad(conv_a.astype(F32), ((0, 0), (0, SUBLANES - DN_CONV), (0, 0)))
    dn_lp = _lane_tiles([(_L_AA, dn_a_log), (_L_AA, dn_dt_bias)])
    ml_lp = _lane_tiles([(_L_DI, ml_i_bias), (_L_DF, ml_f_bias)])
    na_bias = _na_bias_table(na_rpb, s // GRID_W)
    na_qg = jnp.tile(na_q_norm, (1, 2)).reshape(depth, 1, 2 * NA_DH)
    na_kg = jnp.tile(na_k_norm, (1, 2)).reshape(depth, 1, 2 * NA_DH)
    ga_qg = ga_q_norm.reshape(depth, 1, GA_DH)
    ga_kg = ga_k_norm.reshape(depth, 1, GA_DH)
    wb_bf, wo_bf = w_branch.astype(BF16), w_out.astype(BF16)

    x2 = x.reshape(b * s, d)
    for l in range(depth):
        pm2, ps2 = _inproj(x2, norm_g[l].reshape(1, d), w_main, w_small, l)
        pm3 = pm2.reshape(b, s, N_MAIN)
        ps3 = ps2.reshape(b, s, LANES)
        o_af, o_ab, h_df, h_db = _scan_and_mlstm(_dn_pre(pm3, ps3, conv8[l], dn_lp[l]), pm3, ps3, ml_lp[l])
        yb = _natten(pm3, na_bias, na_qg[l], na_kg[l], l)
        yc = _gqa(pm3, cos_t, sin_t, ga_qg[l], ga_kg[l])
        x2 = _merge(x2, o_af.reshape(b * s, hw), o_ab.reshape(b * s, hw), h_df.reshape(b * s, hw),
                    h_db.reshape(b * s, hw), yb.reshape(b * s, hw), yc.reshape(b * s, hw), pm2,
                    dn_norm_g[l].reshape(1, LANES), ml_norm_g[l].reshape(1, LANES), wb_bf, wo_bf, l)
    return x2.reshape(b, s, d)
```

```python
import math

import jax
import jax.numpy as jnp
from jax import lax
from jax.experimental import pallas as pl
from jax.experimental.pallas import tpu as pltpu

F32 = jnp.float32
BF16 = jnp.bfloat16

GRID_W = 64
N_BRANCH = 4
BRANCH_W = 512
EPS = 1e-6
DN_HEADS, DN_DK, DN_DV, DN_CONV, DN_CHUNK = 4, 128, 128, 5, 64
NA_HEADS, NA_DH, NA_ROWS, NA_COLS = 8, 64, 8, 16
GA_HEADS, GA_KV_HEADS, GA_DH = 4, 2, 128
ROPE_THETA = 10000.0
ML_HEADS, ML_DK, ML_DV, ML_CHUNK = 4, 64, 128, 128

LANES = 128
SUBLANES = 8
VMEM_LIMIT_BYTES = 56 * 1024 * 1024

_O_A_QKV, _O_A_A, _O_A_B, _O_A_Z = 0, 1536, 1544, 1552
_O_B_QKV, _O_B_Z = 2064, 3600
_O_C_Q, _O_C_K, _O_C_V, _O_C_Z = 4112, 4624, 4880, 5136
_O_D_Q, _O_D_K, _O_D_V, _O_D_I, _O_D_F, _O_D_O, _O_D_Z = 5648, 5904, 6160, 6672, 6680, 6688, 7200
_O_GATE = 7712
_MAIN_SEGS = (
    ("a_qkv", _O_A_QKV, 1536), ("b_qkv", _O_B_QKV, 1536), ("a_z", _O_A_Z, 512), ("b_z", _O_B_Z, 512),
    ("gate", _O_GATE, 4096), ("c_q", _O_C_Q, 512), ("c_k", _O_C_K, 256), ("c_v", _O_C_V, 256),
    ("c_z", _O_C_Z, 512), ("d_q", _O_D_Q, 256), ("d_k", _O_D_K, 256), ("d_v", _O_D_V, 512),
    ("d_o", _O_D_O, 512), ("d_z", _O_D_Z, 512),
)
_OFF = {}
_o = 0
for _name, _src, _w in _MAIN_SEGS:
    _OFF[_name] = _o
    _o += _w
N_MAIN = _o
_SMALL_SRC = (_O_A_A, _O_A_B, _O_D_I, _O_D_F)
_L_AA, _L_AB, _L_DI, _L_DF = 0, 8, 16, 24

P_DTYPE = BF16
BLK = 128
NEG = -1e30


def _cparams(sem):
    return pltpu.CompilerParams(dimension_semantics=sem, vmem_limit_bytes=VMEM_LIMIT_BYTES)


def _sigmoid(x):
    return 0.5 * jnp.tanh(0.5 * x) + 0.5


def _silu(x):
    return x * _sigmoid(x)


def _softplus(x):
    return jnp.maximum(x, 0.0) + jnp.log(1.0 + jnp.exp(-jnp.abs(x)))


def _mm(a, b):
    return jnp.dot(a.astype(BF16), b.astype(BF16), preferred_element_type=F32)


def _mm_nt(a, b):
    return lax.dot_general(a.astype(BF16), b.astype(BF16), (((1,), (1,)), ((), ())),
                           preferred_element_type=F32)


def _mm_tn(a, b):
    return lax.dot_general(a.astype(BF16), b.astype(BF16), (((0,), (0,)), ((), ())),
                           preferred_element_type=F32)


def _mask_sum(mask, x):
    m = jnp.where(mask, 1.0, 0.0).astype(BF16)
    x1 = x.astype(BF16)
    r1 = x - x1.astype(F32)
    x2 = r1.astype(BF16)
    x3 = (r1 - x2.astype(F32)).astype(BF16)
    dot = lambda v: jnp.dot(m, v, preferred_element_type=F32)
    return dot(x1) + (dot(x2) + dot(x3))


def _chunk_masks(reverse, chunk):
    i = lax.broadcasted_iota(jnp.int32, (BLK, BLK), 0)
    j = lax.broadcasted_iota(jnp.int32, (BLK, BLK), 1)
    shift = int(math.log2(chunk))
    same = (i >> shift) == (j >> shift)
    if reverse:
        incl = same & (j >= i)
        strict = same & (j > i)
    else:
        incl = same & (j <= i)
        strict = same & (j < i)
    return same, incl, strict, (i == j)


def _chunk_totals(cum, chunk, reverse):
    pieces = []
    for c0 in range(0, BLK, chunk):
        r = c0 if reverse else c0 + chunk - 1
        pieces.append(jnp.broadcast_to(cum[r:r + 1, :], (chunk, cum.shape[1])))
    return jnp.concatenate(pieces, axis=0)


def _rows(start, size):
    if isinstance(start, int):
        return slice(start, start + size)
    return pl.ds(pl.multiple_of(start, size), size)


def _as_f32(mask):
    return jnp.where(mask, 1.0, 0.0).astype(F32)


def _inproj_kernel(x_ref, g_ref, w_ref, ws_ref, pm_ref, ps_ref, h_scr):
    @pl.when(pl.program_id(1) == 0)
    def _():
        x = x_ref[...]
        ms = jnp.mean(x * x, axis=-1, keepdims=True)
        h = (x * lax.rsqrt(ms + EPS) * g_ref[...]).astype(BF16)
        h_scr[...] = h
        ps_ref[...] = jnp.dot(h, ws_ref[...], preferred_element_type=F32)

    pm_ref[...] = jnp.dot(h_scr[...], w_ref[...], preferred_element_type=F32).astype(pm_ref.dtype)


def _inproj(x2d, g, w_main, w_small, layer, tm=1024, tn=N_MAIN // 4):
    m, d = x2d.shape
    tm = min(tm, m)
    return pl.pallas_call(
        _inproj_kernel,
        grid=(m // tm, N_MAIN // tn),
        in_specs=[
            pl.BlockSpec((tm, d), lambda i, j: (i, 0)),
            pl.BlockSpec((1, d), lambda i, j: (0, 0)),
            pl.BlockSpec((None, d, tn), lambda i, j: (layer, 0, j)),
            pl.BlockSpec((None, d, LANES), lambda i, j: (layer, 0, 0)),
        ],
        out_specs=[
            pl.BlockSpec((tm, tn), lambda i, j: (i, j)),
            pl.BlockSpec((tm, LANES), lambda i, j: (i, 0)),
        ],
        out_shape=[jax.ShapeDtypeStruct((m, N_MAIN), P_DTYPE), jax.ShapeDtypeStruct((m, LANES), F32)],
        scratch_shapes=[pltpu.VMEM((tm, d), BF16)],
        compiler_params=_cparams(("parallel", "arbitrary")),
        name="inproj",
    )(x2d, g, w_main, w_small)


GQA_TQ = 512


def _gqa_kernel(q_ref, k_ref, v_ref, z_ref, cos_ref, sin_ref, qg_ref, kg_ref, y_ref,
                q_scr, k_scr, v_scr, s0_scr, s1_scr, p0_scr, p1_scr, l0_scr, l1_scr):
    def norm_rope(x, g, cos, sin):
        ms = jnp.mean(x * x, axis=-1, keepdims=True)
        xn = x * lax.rsqrt(ms + EPS) * g
        lane = lax.broadcasted_iota(jnp.int32, xn.shape, 1)
        partner = jnp.where((lane & 63) < 32, pltpu.roll(xn, LANES - 32, 1), pltpu.roll(xn, 32, 1))
        return xn * cos + partner * sin

    s_len = k_ref.shape[0]
    group = GA_HEADS // GA_KV_HEADS
    assert group == 2
    scale = GA_DH ** -0.5
    n_blk = s_len // GQA_TQ
    s_bufs, p_bufs, l_bufs = (s0_scr, s1_scr), (p0_scr, p1_scr), (l0_scr, l1_scr)
    for kv in range(k_ref.shape[1] // GA_DH):
        _gqa_pipeline(kv, group, s_len, n_blk, scale, norm_rope, q_ref, k_ref, v_ref, z_ref, cos_ref, sin_ref,
                      qg_ref, kg_ref, y_ref, q_scr, k_scr, v_scr, s_bufs, p_bufs, l_bufs)


def _gqa_pipeline(kv, group, s_len, n_blk, scale, norm_rope, q_ref, k_ref, v_ref, z_ref, cos_ref, sin_ref,
                  qg_ref, kg_ref, y_ref, q_scr, k_scr, v_scr, s_bufs, p_bufs, l_bufs):
    kcols = slice(kv * GA_DH, (kv + 1) * GA_DH)
    k_scr[kv] = norm_rope(k_ref[:, kcols].astype(F32), kg_ref[...], cos_ref[...], sin_ref[...]).astype(BF16)
    v_scr[kv] = v_ref[:, kcols].astype(BF16)

    def stacked(head, blk):
        return _rows(head * s_len + blk * GQA_TQ, GQA_TQ)

    def head_cols(head):
        return slice((kv * group + head) * GA_DH, (kv * group + head + 1) * GA_DH)

    def prep(head, blk):
        rows = _rows(blk * GQA_TQ, GQA_TQ)
        q = norm_rope(q_ref[rows, head_cols(head)].astype(F32), qg_ref[...], cos_ref[rows, :], sin_ref[rows, :])
        q_scr[stacked(head, blk), :] = q.astype(BF16)

    def logits(head, blk):
        s_bufs[head][...] = _mm_nt(q_scr[stacked(head, blk), :], k_scr[kv]) * scale

    def softmax(slot):
        s = s_bufs[slot][...]
        p = jnp.exp(s - jnp.max(s, axis=-1, keepdims=True))
        l_bufs[slot][...] = jnp.broadcast_to(jnp.sum(p, axis=-1, keepdims=True), (GQA_TQ, GA_DH))
        p_bufs[slot][...] = p.astype(BF16)

    def weighted(head, blk):
        rows = _rows(blk * GQA_TQ, GQA_TQ)
        cols = head_cols(head)
        o = _mm(p_bufs[head][...], v_scr[kv]) / l_bufs[head][...]
        y_ref[rows, cols] = (o * _silu(z_ref[rows, cols].astype(F32))).astype(y_ref.dtype)

    prep(0, 0)
    prep(1, 0)
    prep(0, 1)
    logits(0, 0)
    logits(1, 0)
    softmax(0)

    def body(j, carry):
        logits(0, j + 1)
        softmax(1)
        weighted(0, j)
        prep(1, j + 1)
        logits(1, j + 1)
        softmax(0)
        weighted(1, j)
        prep(0, jnp.minimum(j + 2, n_blk - 1))
        return carry

    lax.fori_loop(0, n_blk - 1, body, 0)
    softmax(1)
    weighted(0, n_blk - 1)
    weighted(1, n_blk - 1)


def _gqa(pm3, cos_t, sin_t, qg, kg):
    b, s, _ = pm3.shape
    qw = GA_HEADS * GA_DH
    kw = GA_KV_HEADS * GA_DH
    group = GA_HEADS // GA_KV_HEADS
    return pl.pallas_call(
        _gqa_kernel,
        grid=(b,),
        in_specs=[
            pl.BlockSpec((None, s, qw), lambda i: (i, 0, _OFF["c_q"] // qw)),
            pl.BlockSpec((None, s, kw), lambda i: (i, 0, _OFF["c_k"] // kw)),
            pl.BlockSpec((None, s, kw), lambda i: (i, 0, _OFF["c_v"] // kw)),
            pl.BlockSpec((None, s, qw), lambda i: (i, 0, _OFF["c_z"] // qw)),
            pl.BlockSpec((s, GA_DH), lambda i: (0, 0)),
            pl.BlockSpec((s, GA_DH), lambda i: (0, 0)),
            pl.BlockSpec((1, GA_DH), lambda i: (0, 0)),
            pl.BlockSpec((1, GA_DH), lambda i: (0, 0)),
        ],
        out_specs=pl.BlockSpec((None, s, qw), lambda i: (i, 0, 0)),
        out_shape=jax.ShapeDtypeStruct((b, s, BRANCH_W), BF16),
        scratch_shapes=[pltpu.VMEM((group * s, GA_DH), BF16), pltpu.VMEM((GA_KV_HEADS, s, GA_DH), BF16),
                        pltpu.VMEM((GA_KV_HEADS, s, GA_DH), BF16),
                        pltpu.VMEM((GQA_TQ, s), F32), pltpu.VMEM((GQA_TQ, s), F32),
                        pltpu.VMEM((GQA_TQ, s), BF16), pltpu.VMEM((GQA_TQ, s), BF16),
                        pltpu.VMEM((GQA_TQ, GA_DH), F32), pltpu.VMEM((GQA_TQ, GA_DH), F32)],
        compiler_params=_cparams(("parallel",)),
        name="gqa",
    )(pm3, pm3, pm3, pm3, cos_t, sin_t, qg, kg)


NA_ROW_UNROLL = 16


def _na_kernel(q_ref, k_ref, v_ref, z_ref, bias_ref, qg_ref, kg_ref, y_ref, q_scr, k_scr):
    s_len = q_ref.shape[0]
    rows = s_len // GRID_W
    kr = min(NA_ROWS, rows)
    hi = lax.broadcasted_iota(jnp.int32, (2 * NA_DH, 2 * NA_DH), 0) >= NA_DH
    hj = lax.broadcasted_iota(jnp.int32, (2 * NA_DH, 2 * NA_DH), 1) >= NA_DH
    same_head = jnp.where(hi == hj, 1.0, 0.0).astype(BF16)

    def rms_pair(x, g):
        x2 = x * x
        x2_hi = x2.astype(BF16)
        x2_lo = (x2 - x2_hi.astype(F32)).astype(BF16)
        ssq = (jnp.dot(x2_hi, same_head, preferred_element_type=F32)
               + jnp.dot(x2_lo, same_head, preferred_element_type=F32))
        return x * lax.rsqrt(ssq * (1.0 / NA_DH) + EPS) * g

    scale = NA_DH ** -0.5
    assert math.log2(scale).is_integer()
    nkeys = kr * GRID_W
    first = lax.broadcasted_iota(jnp.int32, (s_len, 2 * NA_DH), 1) < NA_DH
    first_q = lax.broadcasted_iota(jnp.int32, (GRID_W, 2 * NA_DH), 1) < NA_DH
    for pair in range(q_ref.shape[1] // (2 * NA_DH)):
        pc = slice(pair * 2 * NA_DH, (pair + 1) * 2 * NA_DH)
        qn = rms_pair(q_ref[:, pc].astype(F32), qg_ref[...]) * scale
        q_scr[0] = jnp.where(first, qn, 0.0).astype(BF16)
        q_scr[1] = jnp.where(first, 0.0, qn).astype(BF16)
        k_scr[...] = rms_pair(k_ref[:, pc].astype(F32), kg_ref[...]).astype(BF16)

        def body(it, carry, pair=pair, pc=pc):
            units = []
            for u in range(NA_ROW_UNROLL):
                r = it * NA_ROW_UNROLL + u
                r0 = jnp.clip(r - kr // 2, 0, rows - kr)
                var = r0 - r + (NA_ROWS - 1)
                qrows = pl.ds(pl.multiple_of(r * GRID_W, GRID_W), GRID_W)
                krows = pl.ds(pl.multiple_of(r0 * GRID_W, GRID_W), nkeys)
                units.append((var, qrows, krows))
            logits = [_mm_nt(jnp.concatenate([q_scr[0, qrows, :], q_scr[1, qrows, :]], axis=0), k_scr[krows, :])
                      + bias_ref[pair, var] for var, qrows, krows in units]
            probs = [jnp.exp(s - jnp.max(s, axis=-1, keepdims=True)) for s in logits]
            sums = [jnp.sum(p, axis=-1, keepdims=True) for p in probs]
            outs = [_mm(p, v_ref[krows, pc]) / l for p, l, (_, _, krows) in zip(probs, sums, units)]
            for o, (_, qrows, _) in zip(outs, units):
                o_pair = jnp.where(first_q, o[:GRID_W], o[GRID_W:])
                y_ref[qrows, pc] = (o_pair * _silu(z_ref[qrows, pc].astype(F32))).astype(y_ref.dtype)
            return carry

        lax.fori_loop(0, rows // NA_ROW_UNROLL, body, 0)


def _na_bias_table(rpb, rows):
    kr = min(NA_ROWS, rows)
    c = jnp.arange(GRID_W)
    c0 = jnp.clip(c - NA_COLS // 2, 0, GRID_W - NA_COLS)
    in_win = (c[None, :] >= c0[:, None]) & (c[None, :] < c0[:, None] + NA_COLS)
    d = jnp.arange(2 * GRID_W - 1) - (GRID_W - 1)
    e = rpb[..., jnp.clip(d, -(NA_COLS - 1), NA_COLS - 1) + NA_COLS - 1]
    toeplitz = jnp.stack([e[..., GRID_W - 1 - q:2 * GRID_W - 1 - q] for q in range(GRID_W)], axis=-2)
    t = jnp.where(in_win, toeplitz, NEG)
    lead, n_heads, n_off = rpb.shape[:-3], rpb.shape[-3], rpb.shape[-2]
    t = t.reshape(*lead, n_heads // 2, 2, n_off, GRID_W, GRID_W)
    t = jnp.swapaxes(t, -4, -3).reshape(*lead, n_heads // 2, n_off, 2 * GRID_W, GRID_W)
    tv = jnp.concatenate([t[..., j:j + NA_ROWS, :, :] for j in range(kr)], axis=-1)
    return tv.astype(F32)


def _natten(pm3, bias, qg, kg, layer):
    b, s, _ = pm3.shape
    pw = 2 * NA_DH
    hw = NA_HEADS * NA_DH
    return pl.pallas_call(
        _na_kernel,
        grid=(b,),
        in_specs=[
            pl.BlockSpec((None, s, hw), lambda i: (i, 0, _OFF["b_qkv"] // hw)),
            pl.BlockSpec((None, s, hw), lambda i: (i, 0, _OFF["b_qkv"] // hw + 1)),
            pl.BlockSpec((None, s, hw), lambda i: (i, 0, _OFF["b_qkv"] // hw + 2)),
            pl.BlockSpec((None, s, hw), lambda i: (i, 0, _OFF["b_z"] // hw)),
            pl.BlockSpec((None,) + bias.shape[1:], lambda i: (layer, 0, 0, 0, 0)),
            pl.BlockSpec((1, pw), lambda i: (0, 0)),
            pl.BlockSpec((1, pw), lambda i: (0, 0)),
        ],
        out_specs=pl.BlockSpec((None, s, hw), lambda i: (i, 0, 0)),
        out_shape=jax.ShapeDtypeStruct((b, s, BRANCH_W), BF16),
        scratch_shapes=[pltpu.VMEM((2, s, pw), BF16), pltpu.VMEM((s, pw), BF16)],
        compiler_params=_cparams(("parallel",)),
        name="natten",
    )(pm3, pm3, pm3, pm3, bias, qg, kg)


INV_BASE = 8
PRE_BLK = 4 * BLK


def _inverse_level_masks():
    i = lax.broadcasted_iota(jnp.int32, (BLK, BLK), 0)
    j = lax.broadcasted_iota(jnp.int32, (BLK, BLK), 1)
    same = lambda size: (i >> int(math.log2(size))) == (j >> int(math.log2(size)))
    base = same(INV_BASE)
    joins = []
    size = INV_BASE
    while size < DN_CHUNK:
        joins.append(same(2 * size) & jnp.logical_not(same(size)))
        size *= 2
    return base, joins


def _tri_inverses(l_mats, eyes, level_masks):
    base, joins = level_masks
    ps = [jnp.where(base, -l, 0.0) for l in l_mats]
    ts = [eye + p for eye, p in zip(eyes, ps)]
    for _ in range(int(math.log2(INV_BASE)) - 1):
        ps = [_mm(p, p) for p in ps]
        ts = [t + _mm(t, p) for t, p in zip(ts, ps)]
    for join in joins:
        mids = [_mm(jnp.where(join, l, 0.0), t) for l, t in zip(l_mats, ts)]
        ts = [t - _mm(t, mid) for t, mid in zip(ts, mids)]
    return ts


DN_HW = DN_HEADS * LANES
PK_W, PK_QD, PK_KD, PK_QK = (i * DN_HW for i in range(4))


def _dn_pre_kernel(qkv_ref, sm_ref, cw_ref, lp_ref, uf_ref, ub_ref, pkf_ref, pkb_ref, gtf_ref, gtb_ref):
    n = pl.program_id(1)
    s_len = qkv_ref.shape[0]
    assert qkv_ref.dtype == BF16
    halo = 2 * SUBLANES
    n_sub = sm_ref.shape[0] // BLK
    chunks_per_blk = BLK // DN_CHUNK
    out_row = lax.broadcasted_iota(jnp.int32, (BLK, BLK + 2 * halo), 0)
    in_row = lax.broadcasted_iota(jnp.int32, (BLK, BLK + 2 * halo), 1)
    shifts = {j: jnp.where(in_row == out_row + (halo + j - DN_CONV // 2), 1.0, 0.0).astype(BF16)
              for j in range(DN_CONV) if j != DN_CONV // 2}
    dir_masks = [_chunk_masks(d == 1, DN_CHUNK) for d in range(2)]
    level_masks = _inverse_level_masks()
    outs = ((uf_ref, pkf_ref, gtf_ref), (ub_ref, pkb_ref, gtb_ref))

    def l2n(x):
        return x * lax.rsqrt(jnp.sum(x * x, axis=-1, keepdims=True) + EPS)

    blocks = []
    for sub in range(n_sub):
        t0 = pl.multiple_of((n * n_sub + sub) * BLK, BLK)
        pstart = pl.multiple_of(jnp.maximum(t0 - halo, 0), halo)
        nstart = pl.multiple_of(jnp.minimum(t0 + BLK, s_len - halo), halo)
        prev = qkv_ref[pl.ds(pstart, halo), :]
        prev = jnp.where(t0 > 0, prev, jnp.zeros_like(prev))
        cur = qkv_ref[pl.ds(t0, BLK), :]
        nxt = qkv_ref[pl.ds(nstart, halo), :]
        nxt = jnp.where(t0 + BLK < s_len, nxt, jnp.zeros_like(nxt))
        xw = jnp.concatenate([prev, cur, nxt], axis=0)
        conv = cur.astype(F32) * cw_ref[DN_CONV // 2:DN_CONV // 2 + 1, :]
        for j, shift in shifts.items():
            conv = conv + jnp.dot(shift, xw, preferred_element_type=F32) * cw_ref[j:j + 1, :]
        conv = _silu(conv)

        rows = slice(sub * BLK, (sub + 1) * BLK)
        sm = sm_ref[rows, :]
        g_all = -jnp.exp(lp_ref[0:1, :]) * _softplus(sm + lp_ref[1:2, :])
        beta_all = _sigmoid(sm)
        per_dir = []
        for d in range(2):
            same, incl, strict, eye_b = dir_masks[d]
            gc = _mask_sum(incl, g_all)
            tot = _mask_sum(same, g_all)
            per_dir.append((incl, strict, _as_f32(eye_b), gc, gc.T, tot))
        blocks.append((sub, rows, conv, beta_all, per_dir))

    heads = []
    for sub, rows, conv, beta_all, per_dir in blocks:
        for h in range(DN_HEADS):
            q = l2n(conv[:, h * DN_DK:(h + 1) * DN_DK]) * (DN_DK ** -0.5)
            k = l2n(conv[:, (DN_HEADS + h) * DN_DK:(DN_HEADS + h + 1) * DN_DK])
            v = conv[:, 2 * DN_HEADS * DN_DK + h * DN_DV:2 * DN_HEADS * DN_DK + (h + 1) * DN_DV]
            heads.append((sub, rows, beta_all, per_dir, h, q, k, v))
    kks = [_mm_nt(hd[6], hd[6]) for hd in heads]
    qks = [_mm_nt(hd[5], hd[6]) for hd in heads]

    l_mats, rhss, eyes, slots = [], [], [], []
    for (sub, rows, beta_all, per_dir, h, q, k, v), kk, qk in zip(heads, kks, qks):
        hc = slice(h * LANES, (h + 1) * LANES)
        for d in range(2):
            incl, strict, eye_f, gc, gct, tot = per_dir[d]
            u_ref, pk_ref, gt_ref = outs[d]
            pk = lambda off: slice(off + h * LANES, off + (h + 1) * LANES)
            c = _L_AA + d * DN_HEADS + h
            gcol = gc[:, c:c + 1]
            grow = gct[c:c + 1, :]
            tcol = tot[:, c:c + 1]
            beta = beta_all[:, _L_AB + d * DN_HEADS + h:_L_AB + d * DN_HEADS + h + 1]
            decay = jnp.where(incl, jnp.exp(jnp.where(incl, gcol - grow, 0.0)), 0.0)
            l_mats.append(jnp.where(strict, beta * kk * decay, 0.0))
            egc = jnp.exp(gcol)
            rhss.append(jnp.concatenate([v * beta, k * (beta * egc)], axis=-1).astype(BF16))
            eyes.append(eye_f)
            slots.append((rows, h, d))
            pk_ref[rows, pk(PK_QD)] = (q * egc).astype(pk_ref.dtype)
            pk_ref[rows, pk(PK_KD)] = (k * jnp.exp(tcol - gcol)).astype(pk_ref.dtype)
            pk_ref[rows, pk(PK_QK)] = (qk * decay).astype(pk_ref.dtype)
            gtot = jnp.exp(tcol)
            for ci in range(chunks_per_blk):
                g0 = (sub * chunks_per_blk + ci) * SUBLANES
                gt_ref[g0:g0 + SUBLANES, hc] = jnp.broadcast_to(
                    gtot[ci * DN_CHUNK:ci * DN_CHUNK + SUBLANES, :], (SUBLANES, LANES))

    t_invs = _tri_inverses(l_mats, eyes, level_masks)
    sols = [_mm(t, rhs) for t, rhs in zip(t_invs, rhss)]
    for (rows, h, d), sol in zip(slots, sols):
        u_ref, pk_ref = outs[d][0], outs[d][1]
        u_ref[rows, h * LANES:(h + 1) * LANES] = sol[:, :DN_DV]
        pk_ref[rows, PK_W + h * LANES:PK_W + (h + 1) * LANES] = sol[:, DN_DV:].astype(pk_ref.dtype)


def _dn_pre(pm3, ps3, conv_w8, lane_params):
    b, s, _ = pm3.shape
    nblk = s // PRE_BLK
    gt_rows = PRE_BLK // DN_CHUNK * SUBLANES
    wq = DN_HEADS * (2 * DN_DK + DN_DV)
    hw = DN_HEADS * LANES
    tok = lambda i, j: (i, j, 0)
    big = lambda dt: jax.ShapeDtypeStruct((b, s, hw), dt)
    gts = jax.ShapeDtypeStruct((b, nblk * gt_rows, hw), F32)
    bs_tok = pl.BlockSpec((None, PRE_BLK, hw), tok)
    bs_gt = pl.BlockSpec((None, gt_rows, hw), tok)
    bs_pk = pl.BlockSpec((None, PRE_BLK, 4 * hw), tok)
    packed = jax.ShapeDtypeStruct((b, s, 4 * hw), BF16)
    return pl.pallas_call(
        _dn_pre_kernel,
        grid=(b, nblk),
        in_specs=[
            pl.BlockSpec((None, s, wq), lambda i, j: (i, 0, _OFF["a_qkv"] // wq)),
            pl.BlockSpec((None, PRE_BLK, LANES), tok),
            pl.BlockSpec((SUBLANES, wq), lambda i, j: (0, 0)),
            pl.BlockSpec((SUBLANES, LANES), lambda i, j: (0, 0)),
        ],
        out_specs=[bs_tok, bs_tok, bs_pk, bs_pk, bs_gt, bs_gt],
        out_shape=[big(F32), big(F32), packed, packed, gts, gts],
        compiler_params=_cparams(("parallel", "arbitrary")),
        name="dn_pre",
    )(pm3, ps3, conv_w8, lane_params)


SCAN_BLK = 4 * BLK


def _zero_at_sequence_start(*scratch):
    @pl.when(pl.program_id(1) == 0)
    def _():
        for ref in scratch:
            ref[...] = jnp.zeros_like(ref)


def _dn_scan_body(uf_ref, pkf_ref, gtf_ref, ub_ref, pkb_ref, gtb_ref, of_ref, ob_ref, st_scr):
    per_blk = BLK // DN_CHUNK
    nchunk = SCAN_BLK // DN_CHUNK
    zeros_c = jnp.zeros((DN_CHUNK, DN_DV), F32)
    streams = ((uf_ref, pkf_ref, gtf_ref, of_ref, range(nchunk)),
               (ub_ref, pkb_ref, gtb_ref, ob_ref, range(nchunk - 1, -1, -1)))
    chains = [(d, h) + streams[d] for d in range(2) for h in range(DN_HEADS)]
    states = [st_scr[d * DN_HEADS + h] for d, h, *_ in chains]
    for step in range(nchunk):
        rs, v_pads = [], []
        for (d, h, u_ref, pk_ref, gt_ref, o_ref, order), state in zip(chains, states):
            rows = slice(order[step] * DN_CHUNK, (order[step] + 1) * DN_CHUNK)
            w = pk_ref[rows, PK_W + h * LANES:PK_W + (h + 1) * LANES]
            qd = pk_ref[rows, PK_QD + h * LANES:PK_QD + (h + 1) * LANES]
            rs.append(_mm(jnp.concatenate([w, qd], axis=0), state))
        for (d, h, u_ref, pk_ref, gt_ref, o_ref, order), r in zip(chains, rs):
            ci = order[step]
            rows = slice(ci * DN_CHUNK, (ci + 1) * DN_CHUNK)
            parts = [zeros_c] * per_blk
            parts[ci % per_blk] = u_ref[rows, h * LANES:(h + 1) * LANES] - r[:DN_CHUNK]
            v_pads.append(jnp.concatenate(parts, axis=0))
        new_states = []
        for (d, h, u_ref, pk_ref, gt_ref, o_ref, order), r, v_pad, state in zip(chains, rs, v_pads, states):
            hc = slice(h * LANES, (h + 1) * LANES)
            ci = order[step]
            rows = slice(ci * DN_CHUNK, (ci + 1) * DN_CHUNK)
            blk_rows = slice((ci // per_blk) * BLK, (ci // per_blk + 1) * BLK)
            o_ref[rows, hc] = r[DN_CHUNK:] + _mm(pk_ref[rows, PK_QK + h * LANES:PK_QK + (h + 1) * LANES], v_pad)
            gt = gt_ref[ci * SUBLANES:ci * SUBLANES + 1, hc]
            kd = pk_ref[blk_rows, PK_KD + h * LANES:PK_KD + (h + 1) * LANES]
            new_states.append(state * gt + _mm_tn(kd, v_pad))
        states = new_states
    for (d, h, *_), state in zip(chains, states):
        st_scr[d * DN_HEADS + h] = state


def _dn_scan_parts(pre):
    uf, ub, pkf, pkb, gtf, gtb = pre
    b, s, hw = uf.shape
    nblk = s // SCAN_BLK
    gt_rows = SCAN_BLK // DN_CHUNK * SUBLANES
    fwd = lambda i, j: (i, j, 0)
    bwd = lambda i, j: (i, nblk - 1 - j, 0)
    def specs(imap):
        return [pl.BlockSpec((None, SCAN_BLK, hw), imap), pl.BlockSpec((None, SCAN_BLK, 4 * hw), imap),
                pl.BlockSpec((None, gt_rows, hw), imap)]
    return dict(
        grid=(b, nblk), args=(uf, pkf, gtf, ub, pkb, gtb), in_specs=specs(fwd) + specs(bwd),
        out_specs=[pl.BlockSpec((None, SCAN_BLK, hw), fwd), pl.BlockSpec((None, SCAN_BLK, hw), bwd)],
        out_shape=[jax.ShapeDtypeStruct((b, s, hw), F32)] * 2,
        scratch_shapes=[pltpu.VMEM((2 * DN_HEADS, DN_DK, DN_DV), F32)])


def _call_parts(kernel_fn, name, parts):
    return pl.pallas_call(
        kernel_fn, grid=parts["grid"], in_specs=parts["in_specs"], out_specs=parts["out_specs"],
        out_shape=parts["out_shape"], scratch_shapes=parts["scratch_shapes"],
        compiler_params=_cparams(("parallel", "arbitrary")), name=name)(*parts["args"])


def _scan_and_mlstm(pre, pm3, ps3, ml_lane_params):
    dn, ml = _dn_scan_parts(pre), _mlstm_parts(pm3, ps3, ml_lane_params)
    assert dn["grid"] == ml["grid"]
    n_in = (len(dn["args"]), len(ml["args"]))
    n_out = (len(dn["out_shape"]), len(ml["out_shape"]))

    def both(*refs):
        ins, rest = refs[:sum(n_in)], refs[sum(n_in):]
        outs, scr = rest[:sum(n_out)], rest[sum(n_out):]
        n_dn_scr = len(dn["scratch_shapes"])
        _zero_at_sequence_start(*scr)
        _dn_scan_body(*ins[:n_in[0]], *outs[:n_out[0]], *scr[:n_dn_scr])
        _ml_body(*ins[n_in[0]:], *outs[n_out[0]:], *scr[n_dn_scr:])

    fused = {k: dn[k] + ml[k] for k in ("args", "in_specs", "out_specs", "out_shape", "scratch_shapes")}
    fused["grid"] = dn["grid"]
    return _call_parts(both, "scan_mlstm", fused)


ML_AUG = 2 * LANES
ML_GROUP = 4
ML_BLK = 4 * BLK


def _ml_body(qf_ref, kf_ref, vf_ref, smf_ref, qb_ref, kb_ref, vb_ref, smb_ref, lp_ref,
             hf_ref, hb_ref, c_scr, m_scr):
    nchunk = BLK // ML_CHUNK
    ones_col = jnp.ones((BLK, LANES), BF16)
    zeros_aug = jnp.zeros((ML_CHUNK, ML_AUG), BF16)
    streams = ((qf_ref, kf_ref, vf_ref, smf_ref, hf_ref, range(nchunk)),
               (qb_ref, kb_ref, vb_ref, smb_ref, hb_ref, range(nchunk - 1, -1, -1)))
    lanes = lambda col: jnp.broadcast_to(col, (col.shape[0], LANES))
    n_sub = smf_ref.shape[0] // BLK
    units = [(d, p if d == 0 else n_sub - 1 - p) for p in range(n_sub) for d in range(2)]
    for d, sub in units:
        _ml_block(d, sub, streams[d], lp_ref, c_scr, m_scr, lanes, ones_col, zeros_aug, nchunk)


def _ml_block(d, sub, stream, lp_ref, c_scr, m_scr, lanes, ones_col, zeros_aug, nchunk):
    q_ref, k_ref, v_ref, sm_ref, h_ref, order = stream
    blk_rows = slice(sub * BLK, (sub + 1) * BLK)
    same, incl, _, _ = _chunk_masks(d == 1, ML_CHUNK)
    sm = sm_ref[blk_rows, :]
    ig_all = sm + lp_ref[0:1, :]
    x = sm + lp_ref[1:2, :]
    lf_all = jnp.minimum(x, 0.0) - jnp.log(1.0 + jnp.exp(-jnp.abs(x)))
    lf_all = pltpu.roll(lf_all, LANES - (_L_DF - _L_DI), 1)
    bc_all = _mask_sum(incl, lf_all)
    tot_all = _chunk_totals(bc_all, ML_CHUNK, reverse=(d == 1))
    a_all = ig_all - bc_all
    mwa_all = jnp.concatenate(
        [jnp.broadcast_to(jnp.max(a_all[ci * ML_CHUNK:(ci + 1) * ML_CHUNK], axis=0, keepdims=True),
                          (ML_CHUNK, LANES)) for ci in range(nchunk)], axis=0)
    a_t, w_all, mw_all = a_all.T, jnp.exp(a_all - mwa_all), tot_all + mwa_all

    for chains in [[(d, h) for h in range(g, g + ML_GROUP)] for g in range(0, ML_HEADS, ML_GROUP)]:
        ins = []
        for _, h in chains:
            q = q_ref[blk_rows, h * ML_DK:(h + 1) * ML_DK].astype(BF16)
            k = k_ref[blk_rows, h * ML_DK:(h + 1) * ML_DK].astype(F32) * (ML_DK ** -0.5)
            v_aug = jnp.concatenate([v_ref[blk_rows, h * ML_DV:(h + 1) * ML_DV].astype(BF16), ones_col], axis=-1)
            ins.append((q, k, v_aug))
        qks = [_mm_nt(q, k) for q, k, _ in ins]

        mids = []
        for (d, h), (q, k, v_aug), qk in zip(chains, ins, qks):
            c = _L_DI + d * ML_HEADS + h
            b_l = lanes(bc_all[:, c:c + 1])
            dlog = jnp.where(incl, b_l + a_t[c:c + 1, :], NEG)
            m_intra = lanes(jnp.max(dlog, axis=-1, keepdims=True))
            s_intra = qk * jnp.exp(dlog - m_intra)
            wk = (k * lanes(w_all[:, c:c + 1])[:, :ML_DK]).astype(BF16)
            mids.append((b_l, m_intra, s_intra.astype(BF16), wk))
        p_intras = [_mm(s_b, v_aug) for (_, _, s_b, _), (_, _, v_aug) in zip(mids, ins)]
        kvs = []
        for (_, _, _, wk), (_, _, v_aug) in zip(mids, ins):
            per_chunk = []
            for ci in range(nchunk):
                parts = [zeros_aug] * nchunk
                parts[ci] = v_aug[ci * ML_CHUNK:(ci + 1) * ML_CHUNK]
                per_chunk.append(_mm_tn(wk, jnp.concatenate(parts, axis=0)))
            kvs.append(per_chunk)

        c_sts = [c_scr[d * ML_HEADS + h] for d, h in chains]
        m_sts = [m_scr[d * ML_HEADS + h][0:1, :] for d, h in chains]
        for step in range(nchunk):
            qcs = []
            ci = order[step]
            for (q, _, _), c_st in zip(ins, c_sts):
                qcs.append(_mm(q[ci * ML_CHUNK:(ci + 1) * ML_CHUNK], c_st))
            for idx, (_, h) in enumerate(chains):
                b_l, m_intra, _, _ = mids[idx]
                c = _L_DI + d * ML_HEADS + h
                rows = slice(ci * ML_CHUNK, (ci + 1) * ML_CHUNK)
                out_rows = slice(sub * BLK + ci * ML_CHUNK, sub * BLK + (ci + 1) * ML_CHUNK)
                r8 = slice(ci * ML_CHUNK, ci * ML_CHUNK + SUBLANES)
                m_st, c_st, qc = m_sts[idx], c_sts[idx], qcs[idx]
                m_inter = b_l[rows] + m_st
                m_i = jnp.maximum(m_intra[rows], m_inter)
                f_i = jnp.exp(m_intra[rows] - m_i)
                inter = jnp.exp(m_inter - m_i)
                both = (jnp.concatenate([inter, inter], axis=-1) * qc
                        + jnp.concatenate([f_i, f_i], axis=-1) * p_intras[idx][rows])
                numer, denom = both[:, :ML_DV], both[:, ML_DV:]
                h_ref[out_rows, h * ML_DV:(h + 1) * ML_DV] = numer / jnp.maximum(jnp.abs(denom), jnp.exp(-m_i))
                tot_c = lanes(tot_all[r8, c:c + 1])[0:1]
                mw_c = lanes(mw_all[r8, c:c + 1])[0:1]
                m_new = jnp.maximum(tot_c + m_st, mw_c)
                dec = jnp.exp(tot_c + m_st - m_new)
                gain = jnp.exp(mw_c - m_new)
                c_sts[idx] = (jnp.concatenate([dec, dec], axis=-1) * c_st
                              + jnp.concatenate([gain, gain], axis=-1) * kvs[idx][ci])
                m_sts[idx] = m_new
        for idx, (d, h) in enumerate(chains):
            c_scr[d * ML_HEADS + h] = c_sts[idx]
            m_scr[d * ML_HEADS + h] = jnp.broadcast_to(m_sts[idx], (SUBLANES, LANES))


def _mlstm_parts(pm3, ps3, lane_params):
    b, s, _ = pm3.shape
    nblk = s // ML_BLK
    qw = ML_HEADS * ML_DK
    vw = ML_HEADS * ML_DV
    def specs(tmap):
        blk = lambda j: tmap(j)
        return [
            pl.BlockSpec((None, ML_BLK, qw), lambda i, j: (i, blk(j), _OFF["d_q"] // qw)),
            pl.BlockSpec((None, ML_BLK, qw), lambda i, j: (i, blk(j), _OFF["d_k"] // qw)),
            pl.BlockSpec((None, ML_BLK, vw), lambda i, j: (i, blk(j), _OFF["d_v"] // vw)),
            pl.BlockSpec((None, ML_BLK, LANES), lambda i, j: (i, blk(j), 0)),
        ]
    fwd = lambda j: j
    bwd = lambda j: nblk - 1 - j
    return dict(
        grid=(b, nblk), args=(pm3, pm3, pm3, ps3, pm3, pm3, pm3, ps3, lane_params),
        in_specs=specs(fwd) + specs(bwd) + [pl.BlockSpec((SUBLANES, LANES), lambda i, j: (0, 0))],
        out_specs=[pl.BlockSpec((None, ML_BLK, vw), lambda i, j: (i, j, 0)),
                   pl.BlockSpec((None, ML_BLK, vw), lambda i, j: (i, nblk - 1 - j, 0))],
        out_shape=[jax.ShapeDtypeStruct((b, s, vw), F32)] * 2,
        scratch_shapes=[pltpu.VMEM((2 * ML_HEADS, ML_DK, ML_AUG), F32),
                        pltpu.VMEM((2 * ML_HEADS, SUBLANES, LANES), F32)])


def _merge_kernel(x_ref, af_ref, ab_ref, df_ref, db_ref, yb_ref, yc_ref, az_ref, dz_ref, do_ref, gl_ref,
                  ag_ref, dg_ref, wb_ref, wo_ref, o_ref):
    d = x_ref.shape[-1]

    def head_rms(x, g):
        outs = []
        for h in range(x.shape[-1] // LANES):
            xh = x[:, h * LANES:(h + 1) * LANES]
            ms = jnp.mean(xh * xh, axis=-1, keepdims=True)
            outs.append(xh * lax.rsqrt(ms + EPS) * g)
        return jnp.concatenate(outs, axis=-1)

    ya = head_rms(af_ref[...] + ab_ref[...], ag_ref[...]) * _silu(az_ref[...].astype(F32))
    yd = _sigmoid(do_ref[...].astype(F32)) * head_rms(df_ref[...] + db_ref[...], dg_ref[...])
    yd = yd * _silu(dz_ref[...].astype(F32))
    twice = None
    for i, y in enumerate((ya.astype(BF16), yb_ref[...], yc_ref[...], yd.astype(BF16))):
        proj = jnp.dot(y, wb_ref[i], preferred_element_type=F32)
        term = proj + jnp.tanh(0.5 * gl_ref[:, i * d:(i + 1) * d].astype(F32)) * proj
        twice = term if twice is None else twice + term
    merged = (0.5 * twice).astype(BF16)
    o_ref[...] = x_ref[...] + jnp.dot(merged, wo_ref[...], preferred_element_type=F32)


def _merge(x2d, af, ab, df, db, yb, yc, pm2, ag, dg, wb, wo, layer, tm=512):
    m, d = x2d.shape
    gw = N_BRANCH * d
    w = BRANCH_W
    tok = pl.BlockSpec((tm, w), lambda i: (i, 0))
    col = lambda name: pl.BlockSpec((tm, w), lambda i: (i, _OFF[name] // w))
    vec = pl.BlockSpec((1, LANES), lambda i: (0, 0))
    return pl.pallas_call(
        _merge_kernel,
        grid=(m // tm,),
        in_specs=[
            pl.BlockSpec((tm, d), lambda i: (i, 0)),
            tok, tok, tok, tok, tok, tok,
            col("a_z"), col("d_z"), col("d_o"),
            pl.BlockSpec((tm, gw), lambda i: (i, _OFF["gate"] // gw)),
            vec, vec,
            pl.BlockSpec((None, N_BRANCH, w, d), lambda i: (layer, 0, 0, 0)),
            pl.BlockSpec((None, d, d), lambda i: (layer, 0, 0)),
        ],
        out_specs=pl.BlockSpec((tm, d), lambda i: (i, 0)),
        out_shape=jax.ShapeDtypeStruct((m, d), F32),
        compiler_params=_cparams(("parallel",)),
        name="merge",
    )(x2d, af, ab, df, db, yb, yc, pm2, pm2, pm2, pm2, ag, dg, wb, wo)


def _rope_lane_tables(s):
    t = jnp.arange(s)
    row = (t // GRID_W).astype(F32)
    col = (t % GRID_W).astype(F32)
    m = GA_DH // 4
    inv = ROPE_THETA ** (-jnp.arange(m, dtype=F32) / m)
    ar = row[:, None] * inv
    ac = col[:, None] * inv
    cos_t = jnp.concatenate([jnp.cos(ar), jnp.cos(ar), jnp.cos(ac), jnp.cos(ac)], axis=-1)
    sin_t = jnp.concatenate([-jnp.sin(ar), jnp.sin(ar), -jnp.sin(ac), jnp.sin(ac)], axis=-1)
    return cos_t.astype(F32), sin_t.astype(F32)


def _main_columns(w):
    return jnp.concatenate([w[..., o:o + wd] for _, o, wd in _MAIN_SEGS], axis=-1)


def _lane_tiles(rows):
    padded = []
    for off, vals in rows:
        vals = vals.reshape(vals.shape[0], 1, -1).astype(F32)
        padded.append(jnp.pad(vals, ((0, 0), (0, 0), (off, LANES - off - vals.shape[-1]))))
    tiles = jnp.concatenate(padded, axis=1)
    return jnp.pad(tiles, ((0, 0), (0, SUBLANES - len(rows)), (0, 0)))


def kernel(x, norm_g, w_in, conv_a, dn_a_log, dn_dt_bias, dn_norm_g, na_q_norm, na_k_norm, na_rpb,
           ga_q_norm, ga_k_norm, ml_i_bias, ml_f_bias, ml_norm_g, w_branch, w_out):
    b, s, d = x.shape
    depth = w_in.shape[0]
    hw = BRANCH_W
    cos_t, sin_t = _rope_lane_tables(s)
    w_main = _main_columns(w_in).astype(BF16)
    w_small = jnp.pad(jnp.concatenate([w_in[:, :, o:o + 8] for o in _SMALL_SRC], axis=2),
                      ((0, 0), (0, 0), (0, LANES - 8 * len(_SMALL_SRC)))).astype(BF16)
    conv8 = jnp.pad(conv_a.astype(F32), ((0, 0), (0, SUBLANES - DN_CONV), (0, 0)))
    dn_lp = _lane_tiles([(_L_AA, dn_a_log), (_L_AA, dn_dt_bias)])
    ml_lp = _lane_tiles([(_L_DI, ml_i_bias), (_L_DF, ml_f_bias)])
    na_bias = _na_bias_table(na_rpb, s // GRID_W)
    na_qg = jnp.tile(na_q_norm, (1, 2)).reshape(depth, 1, 2 * NA_DH)
    na_kg = jnp.tile(na_k_norm, (1, 2)).reshape(depth, 1, 2 * NA_DH)
    ga_qg = ga_q_norm.reshape(depth, 1, GA_DH)
    ga_kg = ga_k_norm.reshape(depth, 1, GA_DH)
    wb_bf, wo_bf = w_branch.astype(BF16), w_out.astype(BF16)

    x2 = x.reshape(b * s, d)
    for l in range(depth):
        pm2, ps2 = _inproj(x2, norm_g[l].reshape(1, d), w_main, w_small, l)
        pm3 = pm2.reshape(b, s, N_MAIN)
        ps3 = ps2.reshape(b, s, LANES)
        o_af, o_ab, h_df, h_db = _scan_and_mlstm(_dn_pre(pm3, ps3, conv8[l], dn_lp[l]), pm3, ps3, ml_lp[l])
        yb = _natten(pm3, na_bias, na_qg[l], na_kg[l], l)
        yc = _gqa(pm3, cos_t, sin_t, ga_qg[l], ga_kg[l])
        x2 = _merge(x2, o_af.reshape(b * s, hw), o_ab.reshape(b * s, hw), h_df.reshape(b * s, hw),
                    h_db.reshape(b * s, hw), yb.reshape(b * s, hw), yc.reshape(b * s, hw), pm2,
                    dn_norm_g[l].reshape(1, LANES), ml_norm_g[l].reshape(1, LANES), wb_bf, wo_bf, l)
    return x2.reshape(b, s, d)
```

```python
import math

import jax
import jax.numpy as jnp
from jax import lax
from jax.experimental import pallas as pl
from jax.experimental.pallas import tpu as pltpu

F32 = jnp.float32
BF16 = jnp.bfloat16

GRID_W = 64
N_BRANCH = 4
BRANCH_W = 512
EPS = 1e-6
DN_HEADS, DN_DK, DN_DV, DN_CONV, DN_CHUNK = 4, 128, 128, 5, 64
NA_HEADS, NA_DH, NA_ROWS, NA_COLS = 8, 64, 8, 16
GA_HEADS, GA_KV_HEADS, GA_DH = 4, 2, 128
ROPE_THETA = 10000.0
ML_HEADS, ML_DK, ML_DV, ML_CHUNK = 4, 64, 128, 128

LANES = 128
SUBLANES = 8
VMEM_LIMIT_BYTES = 56 * 1024 * 1024

_O_A_QKV, _O_A_A, _O_A_B, _O_A_Z = 0, 1536, 1544, 1552
_O_B_QKV, _O_B_Z = 2064, 3600
_O_C_Q, _O_C_K, _O_C_V, _O_C_Z = 4112, 4624, 4880, 5136
_O_D_Q, _O_D_K, _O_D_V, _O_D_I, _O_D_F, _O_D_O, _O_D_Z = 5648, 5904, 6160, 6672, 6680, 6688, 7200
_O_GATE = 7712
_MAIN_SEGS = (
    ("a_qkv", _O_A_QKV, 1536), ("b_qkv", _O_B_QKV, 1536), ("a_z", _O_A_Z, 512), ("b_z", _O_B_Z, 512),
    ("gate", _O_GATE, 4096), ("c_q", _O_C_Q, 512), ("c_k", _O_C_K, 256), ("c_v", _O_C_V, 256),
    ("c_z", _O_C_Z, 512), ("d_q", _O_D_Q, 256), ("d_k", _O_D_K, 256), ("d_v", _O_D_V, 512),
    ("d_o", _O_D_O, 512), ("d_z", _O_D_Z, 512),
)
_OFF = {}
_o = 0
for _name, _src, _w in _MAIN_SEGS:
    _OFF[_name] = _o
    _o += _w
N_MAIN = _o
_SMALL_SRC = (_O_A_A, _O_A_B, _O_D_I, _O_D_F)
_L_AA, _L_AB, _L_DI, _L_DF = 0, 8, 16, 24

P_DTYPE = BF16
BLK = 128
NEG = -1e30


def _cparams(sem):
    return pltpu.CompilerParams(dimension_semantics=sem, vmem_limit_bytes=VMEM_LIMIT_BYTES)


def _sigmoid(x):
    return 0.5 * jnp.tanh(0.5 * x) + 0.5


def _silu(x):
    return x * _sigmoid(x)


def _softplus(x):
    return jnp.maximum(x, 0.0) + jnp.log(1.0 + jnp.exp(-jnp.abs(x)))


def _mm(a, b):
    return jnp.dot(a.astype(BF16), b.astype(BF16), preferred_element_type=F32)


def _mm_nt(a, b):
    return lax.dot_general(a.astype(BF16), b.astype(BF16), (((1,), (1,)), ((), ())),
                           preferred_element_type=F32)


def _mm_tn(a, b):
    return lax.dot_general(a.astype(BF16), b.astype(BF16), (((0,), (0,)), ((), ())),
                           preferred_element_type=F32)


def _mask_sum(mask, x):
    m = jnp.where(mask, 1.0, 0.0).astype(BF16)
    x1 = x.astype(BF16)
    r1 = x - x1.astype(F32)
    x2 = r1.astype(BF16)
    x3 = (r1 - x2.astype(F32)).astype(BF16)
    dot = lambda v: jnp.dot(m, v, preferred_element_type=F32)
    return dot(x1) + (dot(x2) + dot(x3))


def _chunk_masks(reverse, chunk):
    i = lax.broadcasted_iota(jnp.int32, (BLK, BLK), 0)
    j = lax.broadcasted_iota(jnp.int32, (BLK, BLK), 1)
    shift = int(math.log2(chunk))
    same = (i >> shift) == (j >> shift)
    if reverse:
        incl = same & (j >= i)
        strict = same & (j > i)
    else:
        incl = same & (j <= i)
        strict = same & (j < i)
    return same, incl, strict, (i == j)


def _chunk_totals(cum, chunk, reverse):
    pieces = []
    for c0 in range(0, BLK, chunk):
        r = c0 if reverse else c0 + chunk - 1
        pieces.append(jnp.broadcast_to(cum[r:r + 1, :], (chunk, cum.shape[1])))
    return jnp.concatenate(pieces, axis=0)


def _rows(start, size):
    if isinstance(start, int):
        return slice(start, start + size)
    return pl.ds(pl.multiple_of(start, size), size)


def _as_f32(mask):
    return jnp.where(mask, 1.0, 0.0).astype(F32)


def _inproj_kernel(x_ref, g_ref, w_ref, ws_ref, pm_ref, ps_ref, h_scr):
    @pl.when(pl.program_id(1) == 0)
    def _():
        x = x_ref[...]
        ms = jnp.mean(x * x, axis=-1, keepdims=True)
        h = (x * lax.rsqrt(ms + EPS) * g_ref[...]).astype(BF16)
        h_scr[...] = h
        ps_ref[...] = jnp.dot(h, ws_ref[...], preferred_element_type=F32)

    pm_ref[...] = jnp.dot(h_scr[...], w_ref[...], preferred_element_type=F32).astype(pm_ref.dtype)


def _inproj(x2d, g, w_main, w_small, layer, tm=1024, tn=N_MAIN // 4):
    m, d = x2d.shape
    tm = min(tm, m)
    return pl.pallas_call(
        _inproj_kernel,
        grid=(m // tm, N_MAIN // tn),
        in_specs=[
            pl.BlockSpec((tm, d), lambda i, j: (i, 0)),
            pl.BlockSpec((1, d), lambda i, j: (0, 0)),
            pl.BlockSpec((None, d, tn), lambda i, j: (layer, 0, j)),
            pl.BlockSpec((None, d, LANES), lambda i, j: (layer, 0, 0)),
        ],
        out_specs=[
            pl.BlockSpec((tm, tn), lambda i, j: (i, j)),
            pl.BlockSpec((tm, LANES), lambda i, j: (i, 0)),
        ],
        out_shape=[jax.ShapeDtypeStruct((m, N_MAIN), P_DTYPE), jax.ShapeDtypeStruct((m, LANES), F32)],
        scratch_shapes=[pltpu.VMEM((tm, d), BF16)],
        compiler_params=_cparams(("parallel", "arbitrary")),
        name="inproj",
    )(x2d, g, w_main, w_small)


GQA_TQ = 512


def _gqa_kernel(q_ref, k_ref, v_ref, z_ref, cos_ref, sin_ref, qg_ref, kg_ref, y_ref,
                q_scr, k_scr, v_scr, s0_scr, s1_scr, p0_scr, p1_scr, l0_scr, l1_scr):
    def norm_rope(x, g, cos, sin):
        ms = jnp.mean(x * x, axis=-1, keepdims=True)
        xn = x * lax.rsqrt(ms + EPS) * g
        lane = lax.broadcasted_iota(jnp.int32, xn.shape, 1)
        partner = jnp.where((lane & 63) < 32, pltpu.roll(xn, LANES - 32, 1), pltpu.roll(xn, 32, 1))
        return xn * cos + partner * sin

    s_len = k_ref.shape[0]
    group = GA_HEADS // GA_KV_HEADS
    assert group == 2
    scale = GA_DH ** -0.5
    n_blk = s_len // GQA_TQ
    s_bufs, p_bufs, l_bufs = (s0_scr, s1_scr), (p0_scr, p1_scr), (l0_scr, l1_scr)
    for kv in range(k_ref.shape[1] // GA_DH):
        _gqa_pipeline(kv, group, s_len, n_blk, scale, norm_rope, q_ref, k_ref, v_ref, z_ref, cos_ref, sin_ref,
                      qg_ref, kg_ref, y_ref, q_scr, k_scr, v_scr, s_bufs, p_bufs, l_bufs)


def _gqa_pipeline(kv, group, s_len, n_blk, scale, norm_rope, q_ref, k_ref, v_ref, z_ref, cos_ref, sin_ref,
                  qg_ref, kg_ref, y_ref, q_scr, k_scr, v_scr, s_bufs, p_bufs, l_bufs):
    kcols = slice(kv * GA_DH, (kv + 1) * GA_DH)
    k_scr[kv] = norm_rope(k_ref[:, kcols].astype(F32), kg_ref[...], cos_ref[...], sin_ref[...]).astype(BF16)
    v_scr[kv] = v_ref[:, kcols].astype(BF16)

    def stacked(head, blk):
        return _rows(head * s_len + blk * GQA_TQ, GQA_TQ)

    def head_cols(head):
        return slice((kv * group + head) * GA_DH, (kv * group + head + 1) * GA_DH)

    def prep(head, blk):
        rows = _rows(blk * GQA_TQ, GQA_TQ)
        q = norm_rope(q_ref[rows, head_cols(head)].astype(F32), qg_ref[...], cos_ref[rows, :], sin_ref[rows, :])
        q_scr[stacked(head, blk), :] = q.astype(BF16)

    def logits(head, blk):
        s_bufs[head][...] = _mm_nt(q_scr[stacked(head, blk), :], k_scr[kv]) * scale

    def softmax(slot):
        s = s_bufs[slot][...]
        p = jnp.exp(s - jnp.max(s, axis=-1, keepdims=True))
        l_bufs[slot][...] = jnp.broadcast_to(jnp.sum(p, axis=-1, keepdims=True), (GQA_TQ, GA_DH))
        p_bufs[slot][...] = p.astype(BF16)

    def weighted(head, blk):
        rows = _rows(blk * GQA_TQ, GQA_TQ)
        cols = head_cols(head)
        o = _mm(p_bufs[head][...], v_scr[kv]) / l_bufs[head][...]
        y_ref[rows, cols] = (o * _silu(z_ref[rows, cols].astype(F32))).astype(y_ref.dtype)

    prep(0, 0)
    prep(1, 0)
    prep(0, 1)
    logits(0, 0)
    logits(1, 0)
    softmax(0)

    def body(j, carry):
        logits(0, j + 1)
        softmax(1)
        weighted(0, j)
        prep(1, j + 1)
        logits(1, j + 1)
        softmax(0)
        weighted(1, j)
        prep(0, jnp.minimum(j + 2, n_blk - 1))
        return carry

    lax.fori_loop(0, n_blk - 1, body, 0)
    softmax(1)
    weighted(0, n_blk - 1)
    weighted(1, n_blk - 1)


def _gqa(pm3, cos_t, sin_t, qg, kg):
    b, s, _ = pm3.shape
    qw = GA_HEADS * GA_DH
    kw = GA_KV_HEADS * GA_DH
    group = GA_HEADS // GA_KV_HEADS
    return pl.pallas_call(
        _gqa_kernel,
        grid=(b,),
        in_specs=[
            pl.BlockSpec((None, s, qw), lambda i: (i, 0, _OFF["c_q"] // qw)),
            pl.BlockSpec((None, s, kw), lambda i: (i, 0, _OFF["c_k"] // kw)),
            pl.BlockSpec((None, s, kw), lambda i: (i, 0, _OFF["c_v"] // kw)),
            pl.BlockSpec((None, s, qw), lambda i: (i, 0, _OFF["c_z"] // qw)),
            pl.BlockSpec((s, GA_DH), lambda i: (0, 0)),
            pl.BlockSpec((s, GA_DH), lambda i: (0, 0)),
            pl.BlockSpec((1, GA_DH), lambda i: (0, 0)),
            pl.BlockSpec((1, GA_DH), lambda i: (0, 0)),
        ],
        out_specs=pl.BlockSpec((None, s, qw), lambda i: (i, 0, 0)),
        out_shape=jax.ShapeDtypeStruct((b, s, BRANCH_W), BF16),
        scratch_shapes=[pltpu.VMEM((group * s, GA_DH), BF16), pltpu.VMEM((GA_KV_HEADS, s, GA_DH), BF16),
                        pltpu.VMEM((GA_KV_HEADS, s, GA_DH), BF16),
                        pltpu.VMEM((GQA_TQ, s), F32), pltpu.VMEM((GQA_TQ, s), F32),
                        pltpu.VMEM((GQA_TQ, s), BF16), pltpu.VMEM((GQA_TQ, s), BF16),
                        pltpu.VMEM((GQA_TQ, GA_DH), F32), pltpu.VMEM((GQA_TQ, GA_DH), F32)],
        compiler_params=_cparams(("parallel",)),
        name="gqa",
    )(pm3, pm3, pm3, pm3, cos_t, sin_t, qg, kg)


NA_ROW_UNROLL = 16


def _na_kernel(q_ref, k_ref, v_ref, z_ref, bias_ref, qg_ref, kg_ref, y_ref, q_scr, k_scr):
    s_len = q_ref.shape[0]
    rows = s_len // GRID_W
    kr = min(NA_ROWS, rows)
    hi = lax.broadcasted_iota(jnp.int32, (2 * NA_DH, 2 * NA_DH), 0) >= NA_DH
    hj = lax.broadcasted_iota(jnp.int32, (2 * NA_DH, 2 * NA_DH), 1) >= NA_DH
    same_head = jnp.where(hi == hj, 1.0, 0.0).astype(BF16)

    def rms_pair(x, g):
        x2 = x * x
        x2_hi = x2.astype(BF16)
        x2_lo = (x2 - x2_hi.astype(F32)).astype(BF16)
        ssq = (jnp.dot(x2_hi, same_head, preferred_element_type=F32)
               + jnp.dot(x2_lo, same_head, preferred_element_type=F32))
        return x * lax.rsqrt(ssq * (1.0 / NA_DH) + EPS) * g

    scale = NA_DH ** -0.5
    assert math.log2(scale).is_integer()
    nkeys = kr * GRID_W
    first = lax.broadcasted_iota(jnp.int32, (s_len, 2 * NA_DH), 1) < NA_DH
    first_q = lax.broadcasted_iota(jnp.int32, (GRID_W, 2 * NA_DH), 1) < NA_DH
    for pair in range(q_ref.shape[1] // (2 * NA_DH)):
        pc = slice(pair * 2 * NA_DH, (pair + 1) * 2 * NA_DH)
        qn = rms_pair(q_ref[:, pc].astype(F32), qg_ref[...]) * scale
        q_scr[0] = jnp.where(first, qn, 0.0).astype(BF16)
        q_scr[1] = jnp.where(first, 0.0, qn).astype(BF16)
        k_scr[...] = rms_pair(k_ref[:, pc].astype(F32), kg_ref[...]).astype(BF16)

        def body(it, carry, pair=pair, pc=pc):
            units = []
            for u in range(NA_ROW_UNROLL):
                r = it * NA_ROW_UNROLL + u
                r0 = jnp.clip(r - kr // 2, 0, rows - kr)
                var = r0 - r + (NA_ROWS - 1)
                qrows = pl.ds(pl.multiple_of(r * GRID_W, GRID_W), GRID_W)
                krows = pl.ds(pl.multiple_of(r0 * GRID_W, GRID_W), nkeys)
                units.append((var, qrows, krows))
            logits = [_mm_nt(jnp.concatenate([q_scr[0, qrows, :], q_scr[1, qrows, :]], axis=0), k_scr[krows, :])
                      + bias_ref[pair, var] for var, qrows, krows in units]
            probs = [jnp.exp(s - jnp.max(s, axis=-1, keepdims=True)) for s in logits]
            sums = [jnp.sum(p, axis=-1, keepdims=True) for p in probs]
            outs = [_mm(p, v_ref[krows, pc]) / l for p, l, (_, _, krows) in zip(probs, sums, units)]
            for o, (_, qrows, _) in zip(outs, units):
                o_pair = jnp.where(first_q, o[:GRID_W], o[GRID_W:])
                y_ref[qrows, pc] = (o_pair * _silu(z_ref[qrows, pc].astype(F32))).astype(y_ref.dtype)
            return carry

        lax.fori_loop(0, rows // NA_ROW_UNROLL, body, 0)


def _na_bias_table(rpb, rows):
    kr = min(NA_ROWS, rows)
    c = jnp.arange(GRID_W)
    c0 = jnp.clip(c - NA_COLS // 2, 0, GRID_W - NA_COLS)
    in_win = (c[None, :] >= c0[:, None]) & (c[None, :] < c0[:, None] + NA_COLS)
    col_off = jnp.clip(c[None, :] - c[:, None], -(NA_COLS - 1), NA_COLS - 1) + NA_COLS - 1
    t = jnp.where(in_win, rpb[..., col_off], NEG)
    lead, n_heads, n_off = rpb.shape[:-3], rpb.shape[-3], rpb.shape[-2]
    t = t.reshape(*lead, n_heads // 2, 2, n_off, GRID_W, GRID_W)
    t = jnp.swapaxes(t, -4, -3).reshape(*lead, n_heads // 2, n_off, 2 * GRID_W, GRID_W)
    tv = jnp.concatenate([t[..., j:j + NA_ROWS, :, :] for j in range(kr)], axis=-1)
    return tv.astype(F32)


def _natten(pm3, bias, qg, kg, layer):
    b, s, _ = pm3.shape
    pw = 2 * NA_DH
    hw = NA_HEADS * NA_DH
    return pl.pallas_call(
        _na_kernel,
        grid=(b,),
        in_specs=[
            pl.BlockSpec((None, s, hw), lambda i: (i, 0, _OFF["b_qkv"] // hw)),
            pl.BlockSpec((None, s, hw), lambda i: (i, 0, _OFF["b_qkv"] // hw + 1)),
            pl.BlockSpec((None, s, hw), lambda i: (i, 0, _OFF["b_qkv"] // hw + 2)),
            pl.BlockSpec((None, s, hw), lambda i: (i, 0, _OFF["b_z"] // hw)),
            pl.BlockSpec((None,) + bias.shape[1:], lambda i: (layer, 0, 0, 0, 0)),
            pl.BlockSpec((1, pw), lambda i: (0, 0)),
            pl.BlockSpec((1, pw), lambda i: (0, 0)),
        ],
        out_specs=pl.BlockSpec((None, s, hw), lambda i: (i, 0, 0)),
        out_shape=jax.ShapeDtypeStruct((b, s, BRANCH_W), BF16),
        scratch_shapes=[pltpu.VMEM((2, s, pw), BF16), pltpu.VMEM((s, pw), BF16)],
        compiler_params=_cparams(("parallel",)),
        name="natten",
    )(pm3, pm3, pm3, pm3, bias, qg, kg)


INV_BASE = 8
PRE_BLK = 4 * BLK


def _inverse_level_masks():
    i = lax.broadcasted_iota(jnp.int32, (BLK, BLK), 0)
    j = lax.broadcasted_iota(jnp.int32, (BLK, BLK), 1)
    same = lambda size: (i >> int(math.log2(size))) == (j >> int(math.log2(size)))
    base = same(INV_BASE)
    joins = []
    size = INV_BASE
    while size < DN_CHUNK:
        joins.append(same(2 * size) & jnp.logical_not(same(size)))
        size *= 2
    return base, joins


def _tri_inverses(l_mats, eyes, level_masks):
    base, joins = level_masks
    ps = [jnp.where(base, -l, 0.0) for l in l_mats]
    ts = [eye + p for eye, p in zip(eyes, ps)]
    for _ in range(int(math.log2(INV_BASE)) - 1):
        ps = [_mm(p, p) for p in ps]
        ts = [t + _mm(t, p) for t, p in zip(ts, ps)]
    for join in joins:
        mids = [_mm(jnp.where(join, l, 0.0), t) for l, t in zip(l_mats, ts)]
        ts = [t - _mm(t, mid) for t, mid in zip(ts, mids)]
    return ts


DN_HW = DN_HEADS * LANES
PK_W, PK_QD, PK_KD, PK_QK = (i * DN_HW for i in range(4))


def _dn_pre_kernel(qkv_ref, sm_ref, cw_ref, lp_ref, uf_ref, ub_ref, pkf_ref, pkb_ref, gtf_ref, gtb_ref):
    n = pl.program_id(1)
    s_len = qkv_ref.shape[0]
    assert qkv_ref.dtype == BF16
    halo = 2 * SUBLANES
    n_sub = sm_ref.shape[0] // BLK
    chunks_per_blk = BLK // DN_CHUNK
    out_row = lax.broadcasted_iota(jnp.int32, (BLK, BLK + 2 * halo), 0)
    in_row = lax.broadcasted_iota(jnp.int32, (BLK, BLK + 2 * halo), 1)
    shifts = {j: jnp.where(in_row == out_row + (halo + j - DN_CONV // 2), 1.0, 0.0).astype(BF16)
              for j in range(DN_CONV) if j != DN_CONV // 2}
    dir_masks = [_chunk_masks(d == 1, DN_CHUNK) for d in range(2)]
    level_masks = _inverse_level_masks()
    outs = ((uf_ref, pkf_ref, gtf_ref), (ub_ref, pkb_ref, gtb_ref))

    def l2n(x):
        return x * lax.rsqrt(jnp.sum(x * x, axis=-1, keepdims=True) + EPS)

    blocks = []
    for sub in range(n_sub):
        t0 = pl.multiple_of((n * n_sub + sub) * BLK, BLK)
        pstart = pl.multiple_of(jnp.maximum(t0 - halo, 0), halo)
        nstart = pl.multiple_of(jnp.minimum(t0 + BLK, s_len - halo), halo)
        prev = qkv_ref[pl.ds(pstart, halo), :]
        prev = jnp.where(t0 > 0, prev, jnp.zeros_like(prev))
        cur = qkv_ref[pl.ds(t0, BLK), :]
        nxt = qkv_ref[pl.ds(nstart, halo), :]
        nxt = jnp.where(t0 + BLK < s_len, nxt, jnp.zeros_like(nxt))
        xw = jnp.concatenate([prev, cur, nxt], axis=0)
        conv = cur.astype(F32) * cw_ref[DN_CONV // 2:DN_CONV // 2 + 1, :]
        for j, shift in shifts.items():
            conv = conv + jnp.dot(shift, xw, preferred_element_type=F32) * cw_ref[j:j + 1, :]
        conv = _silu(conv)

        rows = slice(sub * BLK, (sub + 1) * BLK)
        sm = sm_ref[rows, :]
        g_all = -jnp.exp(lp_ref[0:1, :]) * _softplus(sm + lp_ref[1:2, :])
        beta_all = _sigmoid(sm)
        per_dir = []
        for d in range(2):
            same, incl, strict, eye_b = dir_masks[d]
            gc = _mask_sum(incl, g_all)
            tot = _mask_sum(same, g_all)
            per_dir.append((incl, strict, _as_f32(eye_b), gc, gc.T, tot))
        blocks.append((sub, rows, conv, beta_all, per_dir))

    heads = []
    for sub, rows, conv, beta_all, per_dir in blocks:
        for h in range(DN_HEADS):
            q = l2n(conv[:, h * DN_DK:(h + 1) * DN_DK]) * (DN_DK ** -0.5)
            k = l2n(conv[:, (DN_HEADS + h) * DN_DK:(DN_HEADS + h + 1) * DN_DK])
            v = conv[:, 2 * DN_HEADS * DN_DK + h * DN_DV:2 * DN_HEADS * DN_DK + (h + 1) * DN_DV]
            heads.append((sub, rows, beta_all, per_dir, h, q, k, v))
    kks = [_mm_nt(hd[6], hd[6]) for hd in heads]
    qks = [_mm_nt(hd[5], hd[6]) for hd in heads]

    l_mats, rhss, eyes, slots = [], [], [], []
    for (sub, rows, beta_all, per_dir, h, q, k, v), kk, qk in zip(heads, kks, qks):
        hc = slice(h * LANES, (h + 1) * LANES)
        for d in range(2):
            incl, strict, eye_f, gc, gct, tot = per_dir[d]
            u_ref, pk_ref, gt_ref = outs[d]
            pk = lambda off: slice(off + h * LANES, off + (h + 1) * LANES)
            c = _L_AA + d * DN_HEADS + h
            gcol = gc[:, c:c + 1]
            grow = gct[c:c + 1, :]
            tcol = tot[:, c:c + 1]
            beta = beta_all[:, _L_AB + d * DN_HEADS + h:_L_AB + d * DN_HEADS + h + 1]
            decay = jnp.where(incl, jnp.exp(jnp.where(incl, gcol - grow, 0.0)), 0.0)
            l_mats.append(jnp.where(strict, beta * kk * decay, 0.0))
            egc = jnp.exp(gcol)
            rhss.append(jnp.concatenate([v * beta, k * (beta * egc)], axis=-1).astype(BF16))
            eyes.append(eye_f)
            slots.append((rows, h, d))
            pk_ref[rows, pk(PK_QD)] = (q * egc).astype(pk_ref.dtype)
            pk_ref[rows, pk(PK_KD)] = (k * jnp.exp(tcol - gcol)).astype(pk_ref.dtype)
            pk_ref[rows, pk(PK_QK)] = (qk * decay).astype(pk_ref.dtype)
            gtot = jnp.exp(tcol)
            for ci in range(chunks_per_blk):
                g0 = (sub * chunks_per_blk + ci) * SUBLANES
                gt_ref[g0:g0 + SUBLANES, hc] = jnp.broadcast_to(
                    gtot[ci * DN_CHUNK:ci * DN_CHUNK + SUBLANES, :], (SUBLANES, LANES))

    t_invs = _tri_inverses(l_mats, eyes, level_masks)
    sols = [_mm(t, rhs) for t, rhs in zip(t_invs, rhss)]
    for (rows, h, d), sol in zip(slots, sols):
        u_ref, pk_ref = outs[d][0], outs[d][1]
        u_ref[rows, h * LANES:(h + 1) * LANES] = sol[:, :DN_DV]
        pk_ref[rows, PK_W + h * LANES:PK_W + (h + 1) * LANES] = sol[:, DN_DV:].astype(pk_ref.dtype)


def _dn_pre(pm3, ps3, conv_w8, lane_params):
    b, s, _ = pm3.shape
    nblk = s // PRE_BLK
    gt_rows = PRE_BLK // DN_CHUNK * SUBLANES
    wq = DN_HEADS * (2 * DN_DK + DN_DV)
    hw = DN_HEADS * LANES
    tok = lambda i, j: (i, j, 0)
    big = lambda dt: jax.ShapeDtypeStruct((b, s, hw), dt)
    gts = jax.ShapeDtypeStruct((b, nblk * gt_rows, hw), F32)
    bs_tok = pl.BlockSpec((None, PRE_BLK, hw), tok)
    bs_gt = pl.BlockSpec((None, gt_rows, hw), tok)
    bs_pk = pl.BlockSpec((None, PRE_BLK, 4 * hw), tok)
    packed = jax.ShapeDtypeStruct((b, s, 4 * hw), BF16)
    return pl.pallas_call(
        _dn_pre_kernel,
        grid=(b, nblk),
        in_specs=[
            pl.BlockSpec((None, s, wq), lambda i, j: (i, 0, _OFF["a_qkv"] // wq)),
            pl.BlockSpec((None, PRE_BLK, LANES), tok),
            pl.BlockSpec((SUBLANES, wq), lambda i, j: (0, 0)),
            pl.BlockSpec((SUBLANES, LANES), lambda i, j: (0, 0)),
        ],
        out_specs=[bs_tok, bs_tok, bs_pk, bs_pk, bs_gt, bs_gt],
        out_shape=[big(F32), big(F32), packed, packed, gts, gts],
        compiler_params=_cparams(("parallel", "arbitrary")),
        name="dn_pre",
    )(pm3, ps3, conv_w8, lane_params)


SCAN_BLK = 4 * BLK


def _zero_at_sequence_start(*scratch):
    @pl.when(pl.program_id(1) == 0)
    def _():
        for ref in scratch:
            ref[...] = jnp.zeros_like(ref)


def _dn_scan_body(uf_ref, pkf_ref, gtf_ref, ub_ref, pkb_ref, gtb_ref, of_ref, ob_ref, st_scr):
    per_blk = BLK // DN_CHUNK
    nchunk = SCAN_BLK // DN_CHUNK
    zeros_c = jnp.zeros((DN_CHUNK, DN_DV), F32)
    streams = ((uf_ref, pkf_ref, gtf_ref, of_ref, range(nchunk)),
               (ub_ref, pkb_ref, gtb_ref, ob_ref, range(nchunk - 1, -1, -1)))
    chains = [(d, h) + streams[d] for d in range(2) for h in range(DN_HEADS)]
    states = [st_scr[d * DN_HEADS + h] for d, h, *_ in chains]
    for step in range(nchunk):
        rs, v_pads = [], []
        for (d, h, u_ref, pk_ref, gt_ref, o_ref, order), state in zip(chains, states):
            rows = slice(order[step] * DN_CHUNK, (order[step] + 1) * DN_CHUNK)
            w = pk_ref[rows, PK_W + h * LANES:PK_W + (h + 1) * LANES]
            qd = pk_ref[rows, PK_QD + h * LANES:PK_QD + (h + 1) * LANES]
            rs.append(_mm(jnp.concatenate([w, qd], axis=0), state))
        for (d, h, u_ref, pk_ref, gt_ref, o_ref, order), r in zip(chains, rs):
            ci = order[step]
            rows = slice(ci * DN_CHUNK, (ci + 1) * DN_CHUNK)
            parts = [zeros_c] * per_blk
            parts[ci % per_blk] = u_ref[rows, h * LANES:(h + 1) * LANES] - r[:DN_CHUNK]
            v_pads.append(jnp.concatenate(parts, axis=0))
        new_states = []
        for (d, h, u_ref, pk_ref, gt_ref, o_ref, order), r, v_pad, state in zip(chains, rs, v_pads, states):
            hc = slice(h * LANES, (h + 1) * LANES)
            ci = order[step]
            rows = slice(ci * DN_CHUNK, (ci + 1) * DN_CHUNK)
            blk_rows = slice((ci // per_blk) * BLK, (ci // per_blk + 1) * BLK)
            o_ref[rows, hc] = r[DN_CHUNK:] + _mm(pk_ref[rows, PK_QK + h * LANES:PK_QK + (h + 1) * LANES], v_pad)
            gt = gt_ref[ci * SUBLANES:ci * SUBLANES + 1, hc]
            kd = pk_ref[blk_rows, PK_KD + h * LANES:PK_KD + (h + 1) * LANES]
            new_states.append(state * gt + _mm_tn(kd, v_pad))
        states = new_states
    for (d, h, *_), state in zip(chains, states):
        st_scr[d * DN_HEADS + h] = state


def _dn_scan_parts(pre):
    uf, ub, pkf, pkb, gtf, gtb = pre
    b, s, hw = uf.shape
    nblk = s // SCAN_BLK
    gt_rows = SCAN_BLK // DN_CHUNK * SUBLANES
    fwd = lambda i, j: (i, j, 0)
    bwd = lambda i, j: (i, nblk - 1 - j, 0)
    def specs(imap):
        return [pl.BlockSpec((None, SCAN_BLK, hw), imap), pl.BlockSpec((None, SCAN_BLK, 4 * hw), imap),
                pl.BlockSpec((None, gt_rows, hw), imap)]
    return dict(
        grid=(b, nblk), args=(uf, pkf, gtf, ub, pkb, gtb), in_specs=specs(fwd) + specs(bwd),
        out_specs=[pl.BlockSpec((None, SCAN_BLK, hw), fwd), pl.BlockSpec((None, SCAN_BLK, hw), bwd)],
        out_shape=[jax.ShapeDtypeStruct((b, s, hw), F32)] * 2,
        scratch_shapes=[pltpu.VMEM((2 * DN_HEADS, DN_DK, DN_DV), F32)])


def _call_parts(kernel_fn, name, parts):
    return pl.pallas_call(
        kernel_fn, grid=parts["grid"], in_specs=parts["in_specs"], out_specs=parts["out_specs"],
        out_shape=parts["out_shape"], scratch_shapes=parts["scratch_shapes"],
        compiler_params=_cparams(("parallel", "arbitrary")), name=name)(*parts["args"])


def _scan_and_mlstm(pre, pm3, ps3, ml_lane_params):
    dn, ml = _dn_scan_parts(pre), _mlstm_parts(pm3, ps3, ml_lane_params)
    assert dn["grid"] == ml["grid"]
    n_in = (len(dn["args"]), len(ml["args"]))
    n_out = (len(dn["out_shape"]), len(ml["out_shape"]))

    def both(*refs):
        ins, rest = refs[:sum(n_in)], refs[sum(n_in):]
        outs, scr = rest[:sum(n_out)], rest[sum(n_out):]
        n_dn_scr = len(dn["scratch_shapes"])
        _zero_at_sequence_start(*scr)
        _dn_scan_body(*ins[:n_in[0]], *outs[:n_out[0]], *scr[:n_dn_scr])
        _ml_body(*ins[n_in[0]:], *outs[n_out[0]:], *scr[n_dn_scr:])

    fused = {k: dn[k] + ml[k] for k in ("args", "in_specs", "out_specs", "out_shape", "scratch_shapes")}
    fused["grid"] = dn["grid"]
    return _call_parts(both, "scan_mlstm", fused)


ML_AUG = 2 * LANES
ML_GROUP = 4
ML_BLK = 4 * BLK


def _ml_body(qf_ref, kf_ref, vf_ref, smf_ref, qb_ref, kb_ref, vb_ref, smb_ref, lp_ref,
             hf_ref, hb_ref, c_scr, m_scr):
    nchunk = BLK // ML_CHUNK
    ones_col = jnp.ones((BLK, LANES), BF16)
    zeros_aug = jnp.zeros((ML_CHUNK, ML_AUG), BF16)
    streams = ((qf_ref, kf_ref, vf_ref, smf_ref, hf_ref, range(nchunk)),
               (qb_ref, kb_ref, vb_ref, smb_ref, hb_ref, range(nchunk - 1, -1, -1)))
    lanes = lambda col: jnp.broadcast_to(col, (col.shape[0], LANES))
    n_sub = smf_ref.shape[0] // BLK
    units = [(d, p if d == 0 else n_sub - 1 - p) for p in range(n_sub) for d in range(2)]
    for d, sub in units:
        _ml_block(d, sub, streams[d], lp_ref, c_scr, m_scr, lanes, ones_col, zeros_aug, nchunk)


def _ml_block(d, sub, stream, lp_ref, c_scr, m_scr, lanes, ones_col, zeros_aug, nchunk):
    q_ref, k_ref, v_ref, sm_ref, h_ref, order = stream
    blk_rows = slice(sub * BLK, (sub + 1) * BLK)
    same, incl, _, _ = _chunk_masks(d == 1, ML_CHUNK)
    sm = sm_ref[blk_rows, :]
    ig_all = sm + lp_ref[0:1, :]
    x = sm + lp_ref[1:2, :]
    lf_all = jnp.minimum(x, 0.0) - jnp.log(1.0 + jnp.exp(-jnp.abs(x)))
    lf_all = pltpu.roll(lf_all, LANES - (_L_DF - _L_DI), 1)
    bc_all = _mask_sum(incl, lf_all)
    tot_all = _chunk_totals(bc_all, ML_CHUNK, reverse=(d == 1))
    a_all = ig_all - bc_all
    mwa_all = jnp.concatenate(
        [jnp.broadcast_to(jnp.max(a_all[ci * ML_CHUNK:(ci + 1) * ML_CHUNK], axis=0, keepdims=True),
                          (ML_CHUNK, LANES)) for ci in range(nchunk)], axis=0)
    a_t, w_all, mw_all = a_all.T, jnp.exp(a_all - mwa_all), tot_all + mwa_all

    for chains in [[(d, h) for h in range(g, g + ML_GROUP)] for g in range(0, ML_HEADS, ML_GROUP)]:
        ins = []
        for _, h in chains:
            q = q_ref[blk_rows, h * ML_DK:(h + 1) * ML_DK].astype(BF16)
            k = k_ref[blk_rows, h * ML_DK:(h + 1) * ML_DK].astype(F32) * (ML_DK ** -0.5)
            v_aug = jnp.concatenate([v_ref[blk_rows, h * ML_DV:(h + 1) * ML_DV].astype(BF16), ones_col], axis=-1)
            ins.append((q, k, v_aug))
        qks = [_mm_nt(q, k) for q, k, _ in ins]

        mids = []
        for (d, h), (q, k, v_aug), qk in zip(chains, ins, qks):
            c = _L_DI + d * ML_HEADS + h
            b_l = lanes(bc_all[:, c:c + 1])
            dlog = jnp.where(incl, b_l + a_t[c:c + 1, :], NEG)
            m_intra = lanes(jnp.max(dlog, axis=-1, keepdims=True))
            s_intra = qk * jnp.exp(dlog - m_intra)
            wk = (k * lanes(w_all[:, c:c + 1])[:, :ML_DK]).astype(BF16)
            mids.append((b_l, m_intra, s_intra.astype(BF16), wk))
        p_intras = [_mm(s_b, v_aug) for (_, _, s_b, _), (_, _, v_aug) in zip(mids, ins)]
        kvs = []
        for (_, _, _, wk), (_, _, v_aug) in zip(mids, ins):
            per_chunk = []
            for ci in range(nchunk):
                parts = [zeros_aug] * nchunk
                parts[ci] = v_aug[ci * ML_CHUNK:(ci + 1) * ML_CHUNK]
                per_chunk.append(_mm_tn(wk, jnp.concatenate(parts, axis=0)))
            kvs.append(per_chunk)

        c_sts = [c_scr[d * ML_HEADS + h] for d, h in chains]
        m_sts = [m_scr[d * ML_HEADS + h][0:1, :] for d, h in chains]
        for step in range(nchunk):
            qcs = []
            ci = order[step]
            for (q, _, _), c_st in zip(ins, c_sts):
                qcs.append(_mm(q[ci * ML_CHUNK:(ci + 1) * ML_CHUNK], c_st))
            for idx, (_, h) in enumerate(chains):
                b_l, m_intra, _, _ = mids[idx]
                c = _L_DI + d * ML_HEADS + h
                rows = slice(ci * ML_CHUNK, (ci + 1) * ML_CHUNK)
                out_rows = slice(sub * BLK + ci * ML_CHUNK, sub * BLK + (ci + 1) * ML_CHUNK)
                r8 = slice(ci * ML_CHUNK, ci * ML_CHUNK + SUBLANES)
                m_st, c_st, qc = m_sts[idx], c_sts[idx], qcs[idx]
                m_inter = b_l[rows] + m_st
                m_i = jnp.maximum(m_intra[rows], m_inter)
                f_i = jnp.exp(m_intra[rows] - m_i)
                inter = jnp.exp(m_inter - m_i)
                both = (jnp.concatenate([inter, inter], axis=-1) * qc
                        + jnp.concatenate([f_i, f_i], axis=-1) * p_intras[idx][rows])
                numer, denom = both[:, :ML_DV], both[:, ML_DV:]
                h_ref[out_rows, h * ML_DV:(h + 1) * ML_DV] = numer / jnp.maximum(jnp.abs(denom), jnp.exp(-m_i))
                tot_c = lanes(tot_all[r8, c:c + 1])[0:1]
                mw_c = lanes(mw_all[r8, c:c + 1])[0:1]
                m_new = jnp.maximum(tot_c + m_st, mw_c)
                dec = jnp.exp(tot_c + m_st - m_new)
                gain = jnp.exp(mw_c - m_new)
                c_sts[idx] = (jnp.concatenate([dec, dec], axis=-1) * c_st
                              + jnp.concatenate([gain, gain], axis=-1) * kvs[idx][ci])
                m_sts[idx] = m_new
        for idx, (d, h) in enumerate(chains):
            c_scr[d * ML_HEADS + h] = c_sts[idx]
            m_scr[d * ML_HEADS + h] = jnp.broadcast_to(m_sts[idx], (SUBLANES, LANES))


def _mlstm_parts(pm3, ps3, lane_params):
    b, s, _ = pm3.shape
    nblk = s // ML_BLK
    qw = ML_HEADS * ML_DK
    vw = ML_HEADS * ML_DV
    def specs(tmap):
        blk = lambda j: tmap(j)
        return [
            pl.BlockSpec((None, ML_BLK, qw), lambda i, j: (i, blk(j), _OFF["d_q"] // qw)),
            pl.BlockSpec((None, ML_BLK, qw), lambda i, j: (i, blk(j), _OFF["d_k"] // qw)),
            pl.BlockSpec((None, ML_BLK, vw), lambda i, j: (i, blk(j), _OFF["d_v"] // vw)),
            pl.BlockSpec((None, ML_BLK, LANES), lambda i, j: (i, blk(j), 0)),
        ]
    fwd = lambda j: j
    bwd = lambda j: nblk - 1 - j
    return dict(
        grid=(b, nblk), args=(pm3, pm3, pm3, ps3, pm3, pm3, pm3, ps3, lane_params),
        in_specs=specs(fwd) + specs(bwd) + [pl.BlockSpec((SUBLANES, LANES), lambda i, j: (0, 0))],
        out_specs=[pl.BlockSpec((None, ML_BLK, vw), lambda i, j: (i, j, 0)),
                   pl.BlockSpec((None, ML_BLK, vw), lambda i, j: (i, nblk - 1 - j, 0))],
        out_shape=[jax.ShapeDtypeStruct((b, s, vw), F32)] * 2,
        scratch_shapes=[pltpu.VMEM((2 * ML_HEADS, ML_DK, ML_AUG), F32),
                        pltpu.VMEM((2 * ML_HEADS, SUBLANES, LANES), F32)])


def _merge_kernel(x_ref, af_ref, ab_ref, df_ref, db_ref, yb_ref, yc_ref, az_ref, dz_ref, do_ref, gl_ref,
                  ag_ref, dg_ref, wb_ref, wo_ref, o_ref):
    d = x_ref.shape[-1]

    def head_rms(x, g):
        outs = []
        for h in range(x.shape[-1] // LANES):
            xh = x[:, h * LANES:(h + 1) * LANES]
            ms = jnp.mean(xh * xh, axis=-1, keepdims=True)
            outs.append(xh * lax.rsqrt(ms + EPS) * g)
        return jnp.concatenate(outs, axis=-1)

    ya = head_rms(af_ref[...] + ab_ref[...], ag_ref[...]) * _silu(az_ref[...].astype(F32))
    yd = _sigmoid(do_ref[...].astype(F32)) * head_rms(df_ref[...] + db_ref[...], dg_ref[...])
    yd = yd * _silu(dz_ref[...].astype(F32))
    twice = None
    for i, y in enumerate((ya.astype(BF16), yb_ref[...], yc_ref[...], yd.astype(BF16))):
        proj = jnp.dot(y, wb_ref[i], preferred_element_type=F32)
        term = proj + jnp.tanh(0.5 * gl_ref[:, i * d:(i + 1) * d].astype(F32)) * proj
        twice = term if twice is None else twice + term
    merged = (0.5 * twice).astype(BF16)
    o_ref[...] = x_ref[...] + jnp.dot(merged, wo_ref[...], preferred_element_type=F32)


def _merge(x2d, af, ab, df, db, yb, yc, pm2, ag, dg, wb, wo, layer, tm=512):
    m, d = x2d.shape
    gw = N_BRANCH * d
    w = BRANCH_W
    tok = pl.BlockSpec((tm, w), lambda i: (i, 0))
    col = lambda name: pl.BlockSpec((tm, w), lambda i: (i, _OFF[name] // w))
    vec = pl.BlockSpec((1, LANES), lambda i: (0, 0))
    return pl.pallas_call(
        _merge_kernel,
        grid=(m // tm,),
        in_specs=[
            pl.BlockSpec((tm, d), lambda i: (i, 0)),
            tok, tok, tok, tok, tok, tok,
            col("a_z"), col("d_z"), col("d_o"),
            pl.BlockSpec((tm, gw), lambda i: (i, _OFF["gate"] // gw)),
            vec, vec,
            pl.BlockSpec((None, N_BRANCH, w, d), lambda i: (layer, 0, 0, 0)),
            pl.BlockSpec((None, d, d), lambda i: (layer, 0, 0)),
        ],
        out_specs=pl.BlockSpec((tm, d), lambda i: (i, 0)),
        out_shape=jax.ShapeDtypeStruct((m, d), F32),
        compiler_params=_cparams(("parallel",)),
        name="merge",
    )(x2d, af, ab, df, db, yb, yc, pm2, pm2, pm2, pm2, ag, dg, wb, wo)


def _rope_lane_tables(s):
    t = jnp.arange(s)
    row = (t // GRID_W).astype(F32)
    col = (t % GRID_W).astype(F32)
    m = GA_DH // 4
    inv = ROPE_THETA ** (-jnp.arange(m, dtype=F32) / m)
    ar = row[:, None] * inv
    ac = col[:, None] * inv
    cos_t = jnp.concatenate([jnp.cos(ar), jnp.cos(ar), jnp.cos(ac), jnp.cos(ac)], axis=-1)
    sin_t = jnp.concatenate([-jnp.sin(ar), jnp.sin(ar), -jnp.sin(ac), jnp.sin(ac)], axis=-1)
    return cos_t.astype(F32), sin_t.astype(F32)


def _main_columns(w):
    return jnp.concatenate([w[..., o:o + wd] for _, o, wd in _MAIN_SEGS], axis=-1)


def _lane_tiles(rows):
    padded = []
    for off, vals in rows:
        vals = vals.reshape(vals.shape[0], 1, -1).astype(F32)
        padded.append(jnp.pad(vals, ((0, 0), (0, 0), (off, LANES - off - vals.shape[-1]))))
    tiles = jnp.concatenate(padded, axis=1)
    return jnp.pad(tiles, ((0, 0), (0, SUBLANES - len(rows)), (0, 0)))


def kernel(x, norm_g, w_in, conv_a, dn_a_log, dn_dt_bias, dn_norm_g, na_q_norm, na_k_norm, na_rpb,
           ga_q_norm, ga_k_norm, ml_i_bias, ml_f_bias, ml_norm_g, w_branch, w_out):
    b, s, d = x.shape
    depth = w_in.shape[0]
    hw = BRANCH_W
    cos_t, sin_t = _rope_lane_tables(s)
    w_main = _main_columns(w_in.astype(BF16))
    w_small = jnp.pad(jnp.concatenate([w_in[:, :, o:o + 8] for o in _SMALL_SRC], axis=2),
                      ((0, 0), (0, 0), (0, LANES - 8 * len(_SMALL_SRC)))).astype(BF16)
    conv8 = jnp.pad(conv_a.astype(F32), ((0, 0), (0, SUBLANES - DN_CONV), (0, 0)))
    dn_lp = _lane_tiles([(_L_AA, dn_a_log), (_L_AA, dn_dt_bias)])
    ml_lp = _lane_tiles([(_L_DI, ml_i_bias), (_L_DF, ml_f_bias)])
    na_bias = _na_bias_table(na_rpb, s // GRID_W)
    na_qg = jnp.tile(na_q_norm, (1, 2)).reshape(depth, 1, 2 * NA_DH)
    na_kg = jnp.tile(na_k_norm, (1, 2)).reshape(depth, 1, 2 * NA_DH)
    ga_qg = ga_q_norm.reshape(depth, 1, GA_DH)
    ga_kg = ga_k_norm.reshape(depth, 1, GA_DH)
    wb_bf, wo_bf = w_branch.astype(BF16), w_out.astype(BF16)

    x2 = x.reshape(b * s, d)
    for l in range(depth):
        pm2, ps2 = _inproj(x2, norm_g[l].reshape(1, d), w_main, w_small, l)
        pm3 = pm2.reshape(b, s, N_MAIN)
        ps3 = ps2.reshape(b, s, LANES)
        o_af, o_ab, h_df, h_db = _scan_and_mlstm(_dn_pre(pm3, ps3, conv8[l], dn_lp[l]), pm3, ps3, ml_lp[l])
        yb = _natten(pm3, na_bias, na_qg[l], na_kg[l], l)
        yc = _gqa(pm3, cos_t, sin_t, ga_qg[l], ga_kg[l])
        x2 = _merge(x2, o_af.reshape(b * s, hw), o_ab.reshape(b * s, hw), h_df.reshape(b * s, hw),
                    h_db.reshape(b * s, hw), yb.reshape(b * s, hw), yc.reshape(b * s, hw), pm2,
                    dn_norm_g[l].reshape(1, LANES), ml_norm_g[l].reshape(1, LANES), wb_bf, wo_bf, l)
    return x2.reshape(b, s, d)
```

```python
import math

import jax
import jax.numpy as jnp
from jax import lax
from jax.experimental import pallas as pl
from jax.experimental.pallas import tpu as pltpu

F32 = jnp.float32
BF16 = jnp.bfloat16

GRID_W = 64
N_BRANCH = 4
BRANCH_W = 512
EPS = 1e-6
DN_HEADS, DN_DK, DN_DV, DN_CONV, DN_CHUNK = 4, 128, 128, 5, 64
NA_HEADS, NA_DH, NA_ROWS, NA_COLS = 8, 64, 8, 16
GA_HEADS, GA_KV_HEADS, GA_DH = 4, 2, 128
ROPE_THETA = 10000.0
ML_HEADS, ML_DK, ML_DV, ML_CHUNK = 4, 64, 128, 128

LANES = 128
SUBLANES = 8
VMEM_LIMIT_BYTES = 56 * 1024 * 1024

_O_A_QKV, _O_A_A, _O_A_B, _O_A_Z = 0, 1536, 1544, 1552
_O_B_QKV, _O_B_Z = 2064, 3600
_O_C_Q, _O_C_K, _O_C_V, _O_C_Z = 4112, 4624, 4880, 5136
_O_D_Q, _O_D_K, _O_D_V, _O_D_I, _O_D_F, _O_D_O, _O_D_Z = 5648, 5904, 6160, 6672, 6680, 6688, 7200
_O_GATE = 7712
_MAIN_SEGS = (
    ("a_qkv", _O_A_QKV, 1536), ("b_qkv", _O_B_QKV, 1536), ("a_z", _O_A_Z, 512), ("b_z", _O_B_Z, 512),
    ("gate", _O_GATE, 4096), ("c_q", _O_C_Q, 512), ("c_k", _O_C_K, 256), ("c_v", _O_C_V, 256),
    ("c_z", _O_C_Z, 512), ("d_q", _O_D_Q, 256), ("d_k", _O_D_K, 256), ("d_v", _O_D_V, 512),
    ("d_o", _O_D_O, 512), ("d_z", _O_D_Z, 512),
)
_OFF = {}
_o = 0
for _name, _src, _w in _MAIN_SEGS:
    _OFF[_name] = _o
    _o += _w
N_MAIN = _o
_SMALL_SRC = (_O_A_A, _O_A_B, _O_D_I, _O_D_F)
_L_AA, _L_AB, _L_DI, _L_DF = 0, 8, 16, 24

P_DTYPE = BF16
BLK = 128
NEG = -1e30


def _cparams(sem):
    return pltpu.CompilerParams(dimension_semantics=sem, vmem_limit_bytes=VMEM_LIMIT_BYTES)


def _sigmoid(x):
    return 0.5 * jnp.tanh(0.5 * x) + 0.5


def _silu(x):
    return x * _sigmoid(x)


def _softplus(x):
    return jnp.maximum(x, 0.0) + jnp.log(1.0 + jnp.exp(-jnp.abs(x)))


def _mm(a, b):
    return jnp.dot(a.astype(BF16), b.astype(BF16), preferred_element_type=F32)


def _mm_nt(a, b):
    return lax.dot_general(a.astype(BF16), b.astype(BF16), (((1,), (1,)), ((), ())),
                           preferred_element_type=F32)


def _mm_tn(a, b):
    return lax.dot_general(a.astype(BF16), b.astype(BF16), (((0,), (0,)), ((), ())),
                           preferred_element_type=F32)


def _mask_sum(mask, x):
    m = jnp.where(mask, 1.0, 0.0).astype(BF16)
    x1 = x.astype(BF16)
    r1 = x - x1.astype(F32)
    x2 = r1.astype(BF16)
    x3 = (r1 - x2.astype(F32)).astype(BF16)
    dot = lambda v: jnp.dot(m, v, preferred_element_type=F32)
    return dot(x1) + (dot(x2) + dot(x3))


def _chunk_masks(reverse, chunk):
    i = lax.broadcasted_iota(jnp.int32, (BLK, BLK), 0)
    j = lax.broadcasted_iota(jnp.int32, (BLK, BLK), 1)
    shift = int(math.log2(chunk))
    same = (i >> shift) == (j >> shift)
    if reverse:
        incl = same & (j >= i)
        strict = same & (j > i)
    else:
        incl = same & (j <= i)
        strict = same & (j < i)
    return same, incl, strict, (i == j)


def _chunk_totals(cum, chunk, reverse):
    pieces = []
    for c0 in range(0, BLK, chunk):
        r = c0 if reverse else c0 + chunk - 1
        pieces.append(jnp.broadcast_to(cum[r:r + 1, :], (chunk, cum.shape[1])))
    return jnp.concatenate(pieces, axis=0)


def _rows(start, size):
    if isinstance(start, int):
        return slice(start, start + size)
    return pl.ds(pl.multiple_of(start, size), size)


def _as_f32(mask):
    return jnp.where(mask, 1.0, 0.0).astype(F32)


def _inproj_kernel(x_ref, g_ref, w_ref, ws_ref, pm_ref, ps_ref, h_scr):
    @pl.when(pl.program_id(1) == 0)
    def _():
        x = x_ref[...]
        ms = jnp.mean(x * x, axis=-1, keepdims=True)
        h = (x * lax.rsqrt(ms + EPS) * g_ref[...]).astype(BF16)
        h_scr[...] = h
        ps_ref[...] = jnp.dot(h, ws_ref[...], preferred_element_type=F32)

    pm_ref[...] = jnp.dot(h_scr[...], w_ref[...], preferred_element_type=F32).astype(pm_ref.dtype)


def _inproj(x2d, g, w_main, w_small, layer, tm=1024, tn=N_MAIN // 4):
    m, d = x2d.shape
    tm = min(tm, m)
    return pl.pallas_call(
        _inproj_kernel,
        grid=(m // tm, N_MAIN // tn),
        in_specs=[
            pl.BlockSpec((tm, d), lambda i, j: (i, 0)),
            pl.BlockSpec((1, d), lambda i, j: (0, 0)),
            pl.BlockSpec((None, d, tn), lambda i, j: (layer, 0, j)),
            pl.BlockSpec((None, d, LANES), lambda i, j: (layer, 0, 0)),
        ],
        out_specs=[
            pl.BlockSpec((tm, tn), lambda i, j: (i, j)),
            pl.BlockSpec((tm, LANES), lambda i, j: (i, 0)),
        ],
        out_shape=[jax.ShapeDtypeStruct((m, N_MAIN), P_DTYPE), jax.ShapeDtypeStruct((m, LANES), F32)],
        scratch_shapes=[pltpu.VMEM((tm, d), BF16)],
        compiler_params=_cparams(("parallel", "arbitrary")),
        name="inproj",
    )(x2d, g, w_main, w_small)


GQA_TQ = 512


def _gqa_kernel(q_ref, k_ref, v_ref, z_ref, cos_ref, sin_ref, qg_ref, kg_ref, y_ref,
                q_scr, k_scr, v_scr, s0_scr, s1_scr, p0_scr, p1_scr, l0_scr, l1_scr):
    def norm_rope(x, g, cos, sin):
        ms = jnp.mean(x * x, axis=-1, keepdims=True)
        xn = x * lax.rsqrt(ms + EPS) * g
        lane = lax.broadcasted_iota(jnp.int32, xn.shape, 1)
        partner = jnp.where((lane & 63) < 32, pltpu.roll(xn, LANES - 32, 1), pltpu.roll(xn, 32, 1))
        return xn * cos + partner * sin

    s_len = k_ref.shape[0]
    group = GA_HEADS // GA_KV_HEADS
    assert group == 2
    scale = GA_DH ** -0.5
    n_blk = s_len // GQA_TQ
    s_bufs, p_bufs, l_bufs = (s0_scr, s1_scr), (p0_scr, p1_scr), (l0_scr, l1_scr)
    for kv in range(k_ref.shape[1] // GA_DH):
        _gqa_pipeline(kv, group, s_len, n_blk, scale, norm_rope, q_ref, k_ref, v_ref, z_ref, cos_ref, sin_ref,
                      qg_ref, kg_ref, y_ref, q_scr, k_scr, v_scr, s_bufs, p_bufs, l_bufs)


def _gqa_pipeline(kv, group, s_len, n_blk, scale, norm_rope, q_ref, k_ref, v_ref, z_ref, cos_ref, sin_ref,
                  qg_ref, kg_ref, y_ref, q_scr, k_scr, v_scr, s_bufs, p_bufs, l_bufs):
    kcols = slice(kv * GA_DH, (kv + 1) * GA_DH)
    k_scr[kv] = norm_rope(k_ref[:, kcols].astype(F32), kg_ref[...], cos_ref[...], sin_ref[...]).astype(BF16)
    v_scr[kv] = v_ref[:, kcols].astype(BF16)

    def stacked(head, blk):
        return _rows(head * s_len + blk * GQA_TQ, GQA_TQ)

    def head_cols(head):
        return slice((kv * group + head) * GA_DH, (kv * group + head + 1) * GA_DH)

    def prep(head, blk):
        rows = _rows(blk * GQA_TQ, GQA_TQ)
        q = norm_rope(q_ref[rows, head_cols(head)].astype(F32), qg_ref[...], cos_ref[rows, :], sin_ref[rows, :])
        q_scr[stacked(head, blk), :] = q.astype(BF16)

    def logits(head, blk):
        s_bufs[head][...] = _mm_nt(q_scr[stacked(head, blk), :], k_scr[kv]) * scale

    def softmax(slot):
        s = s_bufs[slot][...]
        p = jnp.exp(s - jnp.max(s, axis=-1, keepdims=True))
        l_bufs[slot][...] = jnp.broadcast_to(jnp.sum(p, axis=-1, keepdims=True), (GQA_TQ, GA_DH))
        p_bufs[slot][...] = p.astype(BF16)

    def weighted(head, blk):
        rows = _rows(blk * GQA_TQ, GQA_TQ)
        cols = head_cols(head)
        o = _mm(p_bufs[head][...], v_scr[kv]) / l_bufs[head][...]
        y_ref[rows, cols] = (o * _silu(z_ref[rows, cols].astype(F32))).astype(y_ref.dtype)

    prep(0, 0)
    prep(1, 0)
    prep(0, 1)
    logits(0, 0)
    logits(1, 0)
    softmax(0)

    def body(j, carry):
        logits(0, j + 1)
        softmax(1)
        weighted(0, j)
        prep(1, j + 1)
        logits(1, j + 1)
        softmax(0)
        weighted(1, j)
        prep(0, jnp.minimum(j + 2, n_blk - 1))
        return carry

    lax.fori_loop(0, n_blk - 1, body, 0)
    softmax(1)
    weighted(0, n_blk - 1)
    weighted(1, n_blk - 1)


def _gqa(pm3, cos_t, sin_t, qg, kg):
    b, s, _ = pm3.shape
    qw = GA_HEADS * GA_DH
    kw = GA_KV_HEADS * GA_DH
    group = GA_HEADS // GA_KV_HEADS
    return pl.pallas_call(
        _gqa_kernel,
        grid=(b,),
        in_specs=[
            pl.BlockSpec((None, s, qw), lambda i: (i, 0, _OFF["c_q"] // qw)),
            pl.BlockSpec((None, s, kw), lambda i: (i, 0, _OFF["c_k"] // kw)),
            pl.BlockSpec((None, s, kw), lambda i: (i, 0, _OFF["c_v"] // kw)),
            pl.BlockSpec((None, s, qw), lambda i: (i, 0, _OFF["c_z"] // qw)),
            pl.BlockSpec((s, GA_DH), lambda i: (0, 0)),
            pl.BlockSpec((s, GA_DH), lambda i: (0, 0)),
            pl.BlockSpec((1, GA_DH), lambda i: (0, 0)),
            pl.BlockSpec((1, GA_DH), lambda i: (0, 0)),
        ],
        out_specs=pl.BlockSpec((None, s, qw), lambda i: (i, 0, 0)),
        out_shape=jax.ShapeDtypeStruct((b, s, BRANCH_W), BF16),
        scratch_shapes=[pltpu.VMEM((group * s, GA_DH), BF16), pltpu.VMEM((GA_KV_HEADS, s, GA_DH), BF16),
                        pltpu.VMEM((GA_KV_HEADS, s, GA_DH), BF16),
                        pltpu.VMEM((GQA_TQ, s), F32), pltpu.VMEM((GQA_TQ, s), F32),
                        pltpu.VMEM((GQA_TQ, s), BF16), pltpu.VMEM((GQA_TQ, s), BF16),
                        pltpu.VMEM((GQA_TQ, GA_DH), F32), pltpu.VMEM((GQA_TQ, GA_DH), F32)],
        compiler_params=_cparams(("parallel",)),
        name="gqa",
    )(pm3, pm3, pm3, pm3, cos_t, sin_t, qg, kg)


NA_ROW_UNROLL = 32


def _na_kernel(q_ref, k_ref, v_ref, z_ref, bias_ref, qg_ref, kg_ref, y_ref, q_scr, k_scr):
    s_len = q_ref.shape[0]
    rows = s_len // GRID_W
    kr = min(NA_ROWS, rows)
    hi = lax.broadcasted_iota(jnp.int32, (2 * NA_DH, 2 * NA_DH), 0) >= NA_DH
    hj = lax.broadcasted_iota(jnp.int32, (2 * NA_DH, 2 * NA_DH), 1) >= NA_DH
    same_head = jnp.where(hi == hj, 1.0, 0.0).astype(BF16)

    def rms_pair(x, g):
        x2 = x * x
        x2_hi = x2.astype(BF16)
        x2_lo = (x2 - x2_hi.astype(F32)).astype(BF16)
        ssq = (jnp.dot(x2_hi, same_head, preferred_element_type=F32)
               + jnp.dot(x2_lo, same_head, preferred_element_type=F32))
        return x * lax.rsqrt(ssq * (1.0 / NA_DH) + EPS) * g

    scale = NA_DH ** -0.5
    assert math.log2(scale).is_integer()
    nkeys = kr * GRID_W
    first = lax.broadcasted_iota(jnp.int32, (s_len, 2 * NA_DH), 1) < NA_DH
    first_q = lax.broadcasted_iota(jnp.int32, (GRID_W, 2 * NA_DH), 1) < NA_DH
    for pair in range(q_ref.shape[1] // (2 * NA_DH)):
        pc = slice(pair * 2 * NA_DH, (pair + 1) * 2 * NA_DH)
        qn = rms_pair(q_ref[:, pc].astype(F32), qg_ref[...]) * scale
        q_scr[0] = jnp.where(first, qn, 0.0).astype(BF16)
        q_scr[1] = jnp.where(first, 0.0, qn).astype(BF16)
        k_scr[...] = rms_pair(k_ref[:, pc].astype(F32), kg_ref[...]).astype(BF16)

        def body(it, carry, pair=pair, pc=pc):
            units = []
            for u in range(NA_ROW_UNROLL):
                r = it * NA_ROW_UNROLL + u
                r0 = jnp.clip(r - kr // 2, 0, rows - kr)
                var = r0 - r + (NA_ROWS - 1)
                qrows = pl.ds(pl.multiple_of(r * GRID_W, GRID_W), GRID_W)
                krows = pl.ds(pl.multiple_of(r0 * GRID_W, GRID_W), nkeys)
                units.append((var, qrows, krows))
            logits = [_mm_nt(jnp.concatenate([q_scr[0, qrows, :], q_scr[1, qrows, :]], axis=0), k_scr[krows, :])
                      + bias_ref[pair, var] for var, qrows, krows in units]
            probs = [jnp.exp(s - jnp.max(s, axis=-1, keepdims=True)) for s in logits]
            sums = [jnp.sum(p, axis=-1, keepdims=True) for p in probs]
            outs = [_mm(p, v_ref[krows, pc]) / l for p, l, (_, _, krows) in zip(probs, sums, units)]
            for o, (_, qrows, _) in zip(outs, units):
                o_pair = jnp.where(first_q, o[:GRID_W], o[GRID_W:])
                y_ref[qrows, pc] = (o_pair * _silu(z_ref[qrows, pc].astype(F32))).astype(y_ref.dtype)
            return carry

        lax.fori_loop(0, rows // NA_ROW_UNROLL, body, 0)


def _na_bias_table(rpb, rows):
    kr = min(NA_ROWS, rows)
    c = jnp.arange(GRID_W)
    c0 = jnp.clip(c - NA_COLS // 2, 0, GRID_W - NA_COLS)
    in_win = (c[None, :] >= c0[:, None]) & (c[None, :] < c0[:, None] + NA_COLS)
    col_off = jnp.clip(c[None, :] - c[:, None], -(NA_COLS - 1), NA_COLS - 1) + NA_COLS - 1
    t = jnp.where(in_win, rpb[..., col_off], NEG)
    lead, n_heads, n_off = rpb.shape[:-3], rpb.shape[-3], rpb.shape[-2]
    t = t.reshape(*lead, n_heads // 2, 2, n_off, GRID_W, GRID_W)
    t = jnp.swapaxes(t, -4, -3).reshape(*lead, n_heads // 2, n_off, 2 * GRID_W, GRID_W)
    tv = jnp.concatenate([t[..., j:j + NA_ROWS, :, :] for j in range(kr)], axis=-1)
    return tv.astype(F32)


def _natten(pm3, bias, qg, kg, layer):
    b, s, _ = pm3.shape
    pw = 2 * NA_DH
    hw = NA_HEADS * NA_DH
    return pl.pallas_call(
        _na_kernel,
        grid=(b,),
        in_specs=[
            pl.BlockSpec((None, s, hw), lambda i: (i, 0, _OFF["b_qkv"] // hw)),
            pl.BlockSpec((None, s, hw), lambda i: (i, 0, _OFF["b_qkv"] // hw + 1)),
            pl.BlockSpec((None, s, hw), lambda i: (i, 0, _OFF["b_qkv"] // hw + 2)),
            pl.BlockSpec((None, s, hw), lambda i: (i, 0, _OFF["b_z"] // hw)),
            pl.BlockSpec((None,) + bias.shape[1:], lambda i: (layer, 0, 0, 0, 0)),
            pl.BlockSpec((1, pw), lambda i: (0, 0)),
            pl.BlockSpec((1, pw), lambda i: (0, 0)),
        ],
        out_specs=pl.BlockSpec((None, s, hw), lambda i: (i, 0, 0)),
        out_shape=jax.ShapeDtypeStruct((b, s, BRANCH_W), BF16),
        scratch_shapes=[pltpu.VMEM((2, s, pw), BF16), pltpu.VMEM((s, pw), BF16)],
        compiler_params=_cparams(("parallel",)),
        name="natten",
    )(pm3, pm3, pm3, pm3, bias, qg, kg)


INV_BASE = 8
PRE_BLK = 4 * BLK


def _inverse_level_masks():
    i = lax.broadcasted_iota(jnp.int32, (BLK, BLK), 0)
    j = lax.broadcasted_iota(jnp.int32, (BLK, BLK), 1)
    same = lambda size: (i >> int(math.log2(size))) == (j >> int(math.log2(size)))
    base = same(INV_BASE)
    joins = []
    size = INV_BASE
    while size < DN_CHUNK:
        joins.append(same(2 * size) & jnp.logical_not(same(size)))
        size *= 2
    return base, joins


def _tri_inverses(l_mats, eyes, level_masks):
    base, joins = level_masks
    ps = [jnp.where(base, -l, 0.0) for l in l_mats]
    ts = [eye + p for eye, p in zip(eyes, ps)]
    for _ in range(int(math.log2(INV_BASE)) - 1):
        ps = [_mm(p, p) for p in ps]
        ts = [t + _mm(t, p) for t, p in zip(ts, ps)]
    for join in joins:
        mids = [_mm(jnp.where(join, l, 0.0), t) for l, t in zip(l_mats, ts)]
        ts = [t - _mm(t, mid) for t, mid in zip(ts, mids)]
    return ts


DN_HW = DN_HEADS * LANES
PK_W, PK_QD, PK_KD, PK_QK = (i * DN_HW for i in range(4))


def _dn_pre_kernel(qkv_ref, sm_ref, cw_ref, lp_ref, uf_ref, ub_ref, pkf_ref, pkb_ref, gtf_ref, gtb_ref):
    n = pl.program_id(1)
    s_len = qkv_ref.shape[0]
    assert qkv_ref.dtype == BF16
    halo = 2 * SUBLANES
    n_sub = sm_ref.shape[0] // BLK
    chunks_per_blk = BLK // DN_CHUNK
    out_row = lax.broadcasted_iota(jnp.int32, (BLK, BLK + 2 * halo), 0)
    in_row = lax.broadcasted_iota(jnp.int32, (BLK, BLK + 2 * halo), 1)
    shifts = {j: jnp.where(in_row == out_row + (halo + j - DN_CONV // 2), 1.0, 0.0).astype(BF16)
              for j in range(DN_CONV) if j != DN_CONV // 2}
    dir_masks = [_chunk_masks(d == 1, DN_CHUNK) for d in range(2)]
    level_masks = _inverse_level_masks()
    outs = ((uf_ref, pkf_ref, gtf_ref), (ub_ref, pkb_ref, gtb_ref))

    def l2n(x):
        return x * lax.rsqrt(jnp.sum(x * x, axis=-1, keepdims=True) + EPS)

    blocks = []
    for sub in range(n_sub):
        t0 = pl.multiple_of((n * n_sub + sub) * BLK, BLK)
        pstart = pl.multiple_of(jnp.maximum(t0 - halo, 0), halo)
        nstart = pl.multiple_of(jnp.minimum(t0 + BLK, s_len - halo), halo)
        prev = qkv_ref[pl.ds(pstart, halo), :]
        prev = jnp.where(t0 > 0, prev, jnp.zeros_like(prev))
        cur = qkv_ref[pl.ds(t0, BLK), :]
        nxt = qkv_ref[pl.ds(nstart, halo), :]
        nxt = jnp.where(t0 + BLK < s_len, nxt, jnp.zeros_like(nxt))
        xw = jnp.concatenate([prev, cur, nxt], axis=0)
        conv = cur.astype(F32) * cw_ref[DN_CONV // 2:DN_CONV // 2 + 1, :]
        for j, shift in shifts.items():
            conv = conv + jnp.dot(shift, xw, preferred_element_type=F32) * cw_ref[j:j + 1, :]
        conv = _silu(conv)

        rows = slice(sub * BLK, (sub + 1) * BLK)
        sm = sm_ref[rows, :]
        g_all = -jnp.exp(lp_ref[0:1, :]) * _softplus(sm + lp_ref[1:2, :])
        beta_all = _sigmoid(sm)
        per_dir = []
        for d in range(2):
            same, incl, strict, eye_b = dir_masks[d]
            gc = _mask_sum(incl, g_all)
            tot = _mask_sum(same, g_all)
            per_dir.append((incl, strict, _as_f32(eye_b), gc, gc.T, tot))
        blocks.append((sub, rows, conv, beta_all, per_dir))

    heads = []
    for sub, rows, conv, beta_all, per_dir in blocks:
        for h in range(DN_HEADS):
            q = l2n(conv[:, h * DN_DK:(h + 1) * DN_DK]) * (DN_DK ** -0.5)
            k = l2n(conv[:, (DN_HEADS + h) * DN_DK:(DN_HEADS + h + 1) * DN_DK])
            v = conv[:, 2 * DN_HEADS * DN_DK + h * DN_DV:2 * DN_HEADS * DN_DK + (h + 1) * DN_DV]
            heads.append((sub, rows, beta_all, per_dir, h, q, k, v))
    kks = [_mm_nt(hd[6], hd[6]) for hd in heads]
    qks = [_mm_nt(hd[5], hd[6]) for hd in heads]

    l_mats, rhss, eyes, slots = [], [], [], []
    for (sub, rows, beta_all, per_dir, h, q, k, v), kk, qk in zip(heads, kks, qks):
        hc = slice(h * LANES, (h + 1) * LANES)
        for d in range(2):
            incl, strict, eye_f, gc, gct, tot = per_dir[d]
            u_ref, pk_ref, gt_ref = outs[d]
            pk = lambda off: slice(off + h * LANES, off + (h + 1) * LANES)
            c = _L_AA + d * DN_HEADS + h
            gcol = gc[:, c:c + 1]
            grow = gct[c:c + 1, :]
            tcol = tot[:, c:c + 1]
            beta = beta_all[:, _L_AB + d * DN_HEADS + h:_L_AB + d * DN_HEADS + h + 1]
            decay = jnp.where(incl, jnp.exp(jnp.where(incl, gcol - grow, 0.0)), 0.0)
            l_mats.append(jnp.where(strict, beta * kk * decay, 0.0))
            egc = jnp.exp(gcol)
            rhss.append(jnp.concatenate([v * beta, k * (beta * egc)], axis=-1).astype(BF16))
            eyes.append(eye_f)
            slots.append((rows, h, d))
            pk_ref[rows, pk(PK_QD)] = (q * egc).astype(pk_ref.dtype)
            pk_ref[rows, pk(PK_KD)] = (k * jnp.exp(tcol - gcol)).astype(pk_ref.dtype)
            pk_ref[rows, pk(PK_QK)] = (qk * decay).astype(pk_ref.dtype)
            gtot = jnp.exp(tcol)
            for ci in range(chunks_per_blk):
                g0 = (sub * chunks_per_blk + ci) * SUBLANES
                gt_ref[g0:g0 + SUBLANES, hc] = jnp.broadcast_to(
                    gtot[ci * DN_CHUNK:ci * DN_CHUNK + SUBLANES, :], (SUBLANES, LANES))

    t_invs = _tri_inverses(l_mats, eyes, level_masks)
    sols = [_mm(t, rhs) for t, rhs in zip(t_invs, rhss)]
    for (rows, h, d), sol in zip(slots, sols):
        u_ref, pk_ref = outs[d][0], outs[d][1]
        u_ref[rows, h * LANES:(h + 1) * LANES] = sol[:, :DN_DV]
        pk_ref[rows, PK_W + h * LANES:PK_W + (h + 1) * LANES] = sol[:, DN_DV:].astype(pk_ref.dtype)


def _dn_pre(pm3, ps3, conv_w8, lane_params):
    b, s, _ = pm3.shape
    nblk = s // PRE_BLK
    gt_rows = PRE_BLK // DN_CHUNK * SUBLANES
    wq = DN_HEADS * (2 * DN_DK + DN_DV)
    hw = DN_HEADS * LANES
    tok = lambda i, j: (i, j, 0)
    big = lambda dt: jax.ShapeDtypeStruct((b, s, hw), dt)
    gts = jax.ShapeDtypeStruct((b, nblk * gt_rows, hw), F32)
    bs_tok = pl.BlockSpec((None, PRE_BLK, hw), tok)
    bs_gt = pl.BlockSpec((None, gt_rows, hw), tok)
    bs_pk = pl.BlockSpec((None, PRE_BLK, 4 * hw), tok)
    packed = jax.ShapeDtypeStruct((b, s, 4 * hw), BF16)
    return pl.pallas_call(
        _dn_pre_kernel,
        grid=(b, nblk),
        in_specs=[
            pl.BlockSpec((None, s, wq), lambda i, j: (i, 0, _OFF["a_qkv"] // wq)),
            pl.BlockSpec((None, PRE_BLK, LANES), tok),
            pl.BlockSpec((SUBLANES, wq), lambda i, j: (0, 0)),
            pl.BlockSpec((SUBLANES, LANES), lambda i, j: (0, 0)),
        ],
        out_specs=[bs_tok, bs_tok, bs_pk, bs_pk, bs_gt, bs_gt],
        out_shape=[big(F32), big(F32), packed, packed, gts, gts],
        compiler_params=_cparams(("parallel", "arbitrary")),
        name="dn_pre",
    )(pm3, ps3, conv_w8, lane_params)


SCAN_BLK = 4 * BLK


def _zero_at_sequence_start(*scratch):
    @pl.when(pl.program_id(1) == 0)
    def _():
        for ref in scratch:
            ref[...] = jnp.zeros_like(ref)


def _dn_scan_body(uf_ref, pkf_ref, gtf_ref, ub_ref, pkb_ref, gtb_ref, of_ref, ob_ref, st_scr):
    per_blk = BLK // DN_CHUNK
    nchunk = SCAN_BLK // DN_CHUNK
    zeros_c = jnp.zeros((DN_CHUNK, DN_DV), F32)
    streams = ((uf_ref, pkf_ref, gtf_ref, of_ref, range(nchunk)),
               (ub_ref, pkb_ref, gtb_ref, ob_ref, range(nchunk - 1, -1, -1)))
    chains = [(d, h) + streams[d] for d in range(2) for h in range(DN_HEADS)]
    states = [st_scr[d * DN_HEADS + h] for d, h, *_ in chains]
    for step in range(nchunk):
        rs, v_pads = [], []
        for (d, h, u_ref, pk_ref, gt_ref, o_ref, order), state in zip(chains, states):
            rows = slice(order[step] * DN_CHUNK, (order[step] + 1) * DN_CHUNK)
            w = pk_ref[rows, PK_W + h * LANES:PK_W + (h + 1) * LANES]
            qd = pk_ref[rows, PK_QD + h * LANES:PK_QD + (h + 1) * LANES]
            rs.append(_mm(jnp.concatenate([w, qd], axis=0), state))
        for (d, h, u_ref, pk_ref, gt_ref, o_ref, order), r in zip(chains, rs):
            ci = order[step]
            rows = slice(ci * DN_CHUNK, (ci + 1) * DN_CHUNK)
            parts = [zeros_c] * per_blk
            parts[ci % per_blk] = u_ref[rows, h * LANES:(h + 1) * LANES] - r[:DN_CHUNK]
            v_pads.append(jnp.concatenate(parts, axis=0))
        new_states = []
        for (d, h, u_ref, pk_ref, gt_ref, o_ref, order), r, v_pad, state in zip(chains, rs, v_pads, states):
            hc = slice(h * LANES, (h + 1) * LANES)
            ci = order[step]
            rows = slice(ci * DN_CHUNK, (ci + 1) * DN_CHUNK)
            blk_rows = slice((ci // per_blk) * BLK, (ci // per_blk + 1) * BLK)
            o_ref[rows, hc] = r[DN_CHUNK:] + _mm(pk_ref[rows, PK_QK + h * LANES:PK_QK + (h + 1) * LANES], v_pad)
            gt = gt_ref[ci * SUBLANES:ci * SUBLANES + 1, hc]
            kd = pk_ref[blk_rows, PK_KD + h * LANES:PK_KD + (h + 1) * LANES]
            new_states.append(state * gt + _mm_tn(kd, v_pad))
        states = new_states
    for (d, h, *_), state in zip(chains, states):
        st_scr[d * DN_HEADS + h] = state


def _dn_scan_parts(pre):
    uf, ub, pkf, pkb, gtf, gtb = pre
    b, s, hw = uf.shape
    nblk = s // SCAN_BLK
    gt_rows = SCAN_BLK // DN_CHUNK * SUBLANES
    fwd = lambda i, j: (i, j, 0)
    bwd = lambda i, j: (i, nblk - 1 - j, 0)
    def specs(imap):
        return [pl.BlockSpec((None, SCAN_BLK, hw), imap), pl.BlockSpec((None, SCAN_BLK, 4 * hw), imap),
                pl.BlockSpec((None, gt_rows, hw), imap)]
    return dict(
        grid=(b, nblk), args=(uf, pkf, gtf, ub, pkb, gtb), in_specs=specs(fwd) + specs(bwd),
        out_specs=[pl.BlockSpec((None, SCAN_BLK, hw), fwd), pl.BlockSpec((None, SCAN_BLK, hw), bwd)],
        out_shape=[jax.ShapeDtypeStruct((b, s, hw), F32)] * 2,
        scratch_shapes=[pltpu.VMEM((2 * DN_HEADS, DN_DK, DN_DV), F32)])


def _call_parts(kernel_fn, name, parts):
    return pl.pallas_call(
        kernel_fn, grid=parts["grid"], in_specs=parts["in_specs"], out_specs=parts["out_specs"],
        out_shape=parts["out_shape"], scratch_shapes=parts["scratch_shapes"],
        compiler_params=_cparams(("parallel", "arbitrary")), name=name)(*parts["args"])


def _scan_and_mlstm(pre, pm3, ps3, ml_lane_params):
    dn, ml = _dn_scan_parts(pre), _mlstm_parts(pm3, ps3, ml_lane_params)
    assert dn["grid"] == ml["grid"]
    n_in = (len(dn["args"]), len(ml["args"]))
    n_out = (len(dn["out_shape"]), len(ml["out_shape"]))

    def both(*refs):
        ins, rest = refs[:sum(n_in)], refs[sum(n_in):]
        outs, scr = rest[:sum(n_out)], rest[sum(n_out):]
        n_dn_scr = len(dn["scratch_shapes"])
        _zero_at_sequence_start(*scr)
        _dn_scan_body(*ins[:n_in[0]], *outs[:n_out[0]], *scr[:n_dn_scr])
        _ml_body(*ins[n_in[0]:], *outs[n_out[0]:], *scr[n_dn_scr:])

    fused = {k: dn[k] + ml[k] for k in ("args", "in_specs", "out_specs", "out_shape", "scratch_shapes")}
    fused["grid"] = dn["grid"]
    return _call_parts(both, "scan_mlstm", fused)


ML_AUG = 2 * LANES
ML_GROUP = 4
ML_BLK = 4 * BLK


def _ml_body(qf_ref, kf_ref, vf_ref, smf_ref, qb_ref, kb_ref, vb_ref, smb_ref, lp_ref,
             hf_ref, hb_ref, c_scr, m_scr):
    nchunk = BLK // ML_CHUNK
    ones_col = jnp.ones((BLK, LANES), BF16)
    zeros_aug = jnp.zeros((ML_CHUNK, ML_AUG), BF16)
    streams = ((qf_ref, kf_ref, vf_ref, smf_ref, hf_ref, range(nchunk)),
               (qb_ref, kb_ref, vb_ref, smb_ref, hb_ref, range(nchunk - 1, -1, -1)))
    lanes = lambda col: jnp.broadcast_to(col, (col.shape[0], LANES))
    n_sub = smf_ref.shape[0] // BLK
    units = [(d, p if d == 0 else n_sub - 1 - p) for p in range(n_sub) for d in range(2)]
    for d, sub in units:
        _ml_block(d, sub, streams[d], lp_ref, c_scr, m_scr, lanes, ones_col, zeros_aug, nchunk)


def _ml_block(d, sub, stream, lp_ref, c_scr, m_scr, lanes, ones_col, zeros_aug, nchunk):
    q_ref, k_ref, v_ref, sm_ref, h_ref, order = stream
    blk_rows = slice(sub * BLK, (sub + 1) * BLK)
    same, incl, _, _ = _chunk_masks(d == 1, ML_CHUNK)
    sm = sm_ref[blk_rows, :]
    ig_all = sm + lp_ref[0:1, :]
    x = sm + lp_ref[1:2, :]
    lf_all = jnp.minimum(x, 0.0) - jnp.log(1.0 + jnp.exp(-jnp.abs(x)))
    lf_all = pltpu.roll(lf_all, LANES - (_L_DF - _L_DI), 1)
    bc_all = _mask_sum(incl, lf_all)
    tot_all = _chunk_totals(bc_all, ML_CHUNK, reverse=(d == 1))
    a_all = ig_all - bc_all
    mwa_all = jnp.concatenate(
        [jnp.broadcast_to(jnp.max(a_all[ci * ML_CHUNK:(ci + 1) * ML_CHUNK], axis=0, keepdims=True),
                          (ML_CHUNK, LANES)) for ci in range(nchunk)], axis=0)
    a_t, w_all, mw_all = a_all.T, jnp.exp(a_all - mwa_all), tot_all + mwa_all

    for chains in [[(d, h) for h in range(g, g + ML_GROUP)] for g in range(0, ML_HEADS, ML_GROUP)]:
        ins = []
        for _, h in chains:
            q = q_ref[blk_rows, h * ML_DK:(h + 1) * ML_DK].astype(BF16)
            k = k_ref[blk_rows, h * ML_DK:(h + 1) * ML_DK].astype(F32) * (ML_DK ** -0.5)
            v_aug = jnp.concatenate([v_ref[blk_rows, h * ML_DV:(h + 1) * ML_DV].astype(BF16), ones_col], axis=-1)
            ins.append((q, k, v_aug))
        qks = [_mm_nt(q, k) for q, k, _ in ins]

        mids = []
        for (d, h), (q, k, v_aug), qk in zip(chains, ins, qks):
            c = _L_DI + d * ML_HEADS + h
            b_l = lanes(bc_all[:, c:c + 1])
            dlog = jnp.where(incl, b_l + a_t[c:c + 1, :], NEG)
            m_intra = lanes(jnp.max(dlog, axis=-1, keepdims=True))
            s_intra = qk * jnp.exp(dlog - m_intra)
            wk = (k * lanes(w_all[:, c:c + 1])[:, :ML_DK]).astype(BF16)
            mids.append((b_l, m_intra, s_intra.astype(BF16), wk))
        p_intras = [_mm(s_b, v_aug) for (_, _, s_b, _), (_, _, v_aug) in zip(mids, ins)]
        kvs = []
        for (_, _, _, wk), (_, _, v_aug) in zip(mids, ins):
            per_chunk = []
            for ci in range(nchunk):
                parts = [zeros_aug] * nchunk
                parts[ci] = v_aug[ci * ML_CHUNK:(ci + 1) * ML_CHUNK]
                per_chunk.append(_mm_tn(wk, jnp.concatenate(parts, axis=0)))
            kvs.append(per_chunk)

        c_sts = [c_scr[d * ML_HEADS + h] for d, h in chains]
        m_sts = [m_scr[d * ML_HEADS + h][0:1, :] for d, h in chains]
        for step in range(nchunk):
            qcs = []
            ci = order[step]
            for (q, _, _), c_st in zip(ins, c_sts):
                qcs.append(_mm(q[ci * ML_CHUNK:(ci + 1) * ML_CHUNK], c_st))
            for idx, (_, h) in enumerate(chains):
                b_l, m_intra, _, _ = mids[idx]
                c = _L_DI + d * ML_HEADS + h
                rows = slice(ci * ML_CHUNK, (ci + 1) * ML_CHUNK)
                out_rows = slice(sub * BLK + ci * ML_CHUNK, sub * BLK + (ci + 1) * ML_CHUNK)
                r8 = slice(ci * ML_CHUNK, ci * ML_CHUNK + SUBLANES)
                m_st, c_st, qc = m_sts[idx], c_sts[idx], qcs[idx]
                m_inter = b_l[rows] + m_st
                m_i = jnp.maximum(m_intra[rows], m_inter)
                f_i = jnp.exp(m_intra[rows] - m_i)
                inter = jnp.exp(m_inter - m_i)
                both = (jnp.concatenate([inter, inter], axis=-1) * qc
                        + jnp.concatenate([f_i, f_i], axis=-1) * p_intras[idx][rows])
                numer, denom = both[:, :ML_DV], both[:, ML_DV:]
                h_ref[out_rows, h * ML_DV:(h + 1) * ML_DV] = numer / jnp.maximum(jnp.abs(denom), jnp.exp(-m_i))
                tot_c = lanes(tot_all[r8, c:c + 1])[0:1]
                mw_c = lanes(mw_all[r8, c:c + 1])[0:1]
                m_new = jnp.maximum(tot_c + m_st, mw_c)
                dec = jnp.exp(tot_c + m_st - m_new)
                gain = jnp.exp(mw_c - m_new)
                c_sts[idx] = (jnp.concatenate([dec, dec], axis=-1) * c_st
                              + jnp.concatenate([gain, gain], axis=-1) * kvs[idx][ci])
                m_sts[idx] = m_new
        for idx, (d, h) in enumerate(chains):
            c_scr[d * ML_HEADS + h] = c_sts[idx]
            m_scr[d * ML_HEADS + h] = jnp.broadcast_to(m_sts[idx], (SUBLANES, LANES))


def _mlstm_parts(pm3, ps3, lane_params):
    b, s, _ = pm3.shape
    nblk = s // ML_BLK
    qw = ML_HEADS * ML_DK
    vw = ML_HEADS * ML_DV
    def specs(tmap):
        blk = lambda j: tmap(j)
        return [
            pl.BlockSpec((None, ML_BLK, qw), lambda i, j: (i, blk(j), _OFF["d_q"] // qw)),
            pl.BlockSpec((None, ML_BLK, qw), lambda i, j: (i, blk(j), _OFF["d_k"] // qw)),
            pl.BlockSpec((None, ML_BLK, vw), lambda i, j: (i, blk(j), _OFF["d_v"] // vw)),
            pl.BlockSpec((None, ML_BLK, LANES), lambda i, j: (i, blk(j), 0)),
        ]
    fwd = lambda j: j
    bwd = lambda j: nblk - 1 - j
    return dict(
        grid=(b, nblk), args=(pm3, pm3, pm3, ps3, pm3, pm3, pm3, ps3, lane_params),
        in_specs=specs(fwd) + specs(bwd) + [pl.BlockSpec((SUBLANES, LANES), lambda i, j: (0, 0))],
        out_specs=[pl.BlockSpec((None, ML_BLK, vw), lambda i, j: (i, j, 0)),
                   pl.BlockSpec((None, ML_BLK, vw), lambda i, j: (i, nblk - 1 - j, 0))],
        out_shape=[jax.ShapeDtypeStruct((b, s, vw), F32)] * 2,
        scratch_shapes=[pltpu.VMEM((2 * ML_HEADS, ML_DK, ML_AUG), F32),
                        pltpu.VMEM((2 * ML_HEADS, SUBLANES, LANES), F32)])


def _merge_kernel(x_ref, af_ref, ab_ref, df_ref, db_ref, yb_ref, yc_ref, az_ref, dz_ref, do_ref, gl_ref,
                  ag_ref, dg_ref, wb_ref, wo_ref, o_ref):
    d = x_ref.shape[-1]

    def head_rms(x, g):
        outs = []
        for h in range(x.shape[-1] // LANES):
            xh = x[:, h * LANES:(h + 1) * LANES]
            ms = jnp.mean(xh * xh, axis=-1, keepdims=True)
            outs.append(xh * lax.rsqrt(ms + EPS) * g)
        return jnp.concatenate(outs, axis=-1)

    ya = head_rms(af_ref[...] + ab_ref[...], ag_ref[...]) * _silu(az_ref[...].astype(F32))
    yd = _sigmoid(do_ref[...].astype(F32)) * head_rms(df_ref[...] + db_ref[...], dg_ref[...])
    yd = yd * _silu(dz_ref[...].astype(F32))
    twice = None
    for i, y in enumerate((ya.astype(BF16), yb_ref[...], yc_ref[...], yd.astype(BF16))):
        proj = jnp.dot(y, wb_ref[i], preferred_element_type=F32)
        term = proj + jnp.tanh(0.5 * gl_ref[:, i * d:(i + 1) * d].astype(F32)) * proj
        twice = term if twice is None else twice + term
    merged = (0.5 * twice).astype(BF16)
    o_ref[...] = x_ref[...] + jnp.dot(merged, wo_ref[...], preferred_element_type=F32)


def _merge(x2d, af, ab, df, db, yb, yc, pm2, ag, dg, wb, wo, layer, tm=512):
    m, d = x2d.shape
    gw = N_BRANCH * d
    w = BRANCH_W
    tok = pl.BlockSpec((tm, w), lambda i: (i, 0))
    col = lambda name: pl.BlockSpec((tm, w), lambda i: (i, _OFF[name] // w))
    vec = pl.BlockSpec((1, LANES), lambda i: (0, 0))
    return pl.pallas_call(
        _merge_kernel,
        grid=(m // tm,),
        in_specs=[
            pl.BlockSpec((tm, d), lambda i: (i, 0)),
            tok, tok, tok, tok, tok, tok,
            col("a_z"), col("d_z"), col("d_o"),
            pl.BlockSpec((tm, gw), lambda i: (i, _OFF["gate"] // gw)),
            vec, vec,
            pl.BlockSpec((None, N_BRANCH, w, d), lambda i: (layer, 0, 0, 0)),
            pl.BlockSpec((None, d, d), lambda i: (layer, 0, 0)),
        ],
        out_specs=pl.BlockSpec((tm, d), lambda i: (i, 0)),
        out_shape=jax.ShapeDtypeStruct((m, d), F32),
        compiler_params=_cparams(("parallel",)),
        name="merge",
    )(x2d, af, ab, df, db, yb, yc, pm2, pm2, pm2, pm2, ag, dg, wb, wo)


def _rope_lane_tables(s):
    t = jnp.arange(s)
    row = (t // GRID_W).astype(F32)
    col = (t % GRID_W).astype(F32)
    m = GA_DH // 4
    inv = ROPE_THETA ** (-jnp.arange(m, dtype=F32) / m)
    ar = row[:, None] * inv
    ac = col[:, None] * inv
    cos_t = jnp.concatenate([jnp.cos(ar), jnp.cos(ar), jnp.cos(ac), jnp.cos(ac)], axis=-1)
    sin_t = jnp.concatenate([-jnp.sin(ar), jnp.sin(ar), -jnp.sin(ac), jnp.sin(ac)], axis=-1)
    return cos_t.astype(F32), sin_t.astype(F32)


def _main_columns(w):
    return jnp.concatenate([w[..., o:o + wd] for _, o, wd in _MAIN_SEGS], axis=-1)


def _lane_tiles(rows):
    padded = []
    for off, vals in rows:
        vals = vals.reshape(vals.shape[0], 1, -1).astype(F32)
        padded.append(jnp.pad(vals, ((0, 0), (0, 0), (off, LANES - off - vals.shape[-1]))))
    tiles = jnp.concatenate(padded, axis=1)
    return jnp.pad(tiles, ((0, 0), (0, SUBLANES - len(rows)), (0, 0)))


def kernel(x, norm_g, w_in, conv_a, dn_a_log, dn_dt_bias, dn_norm_g, na_q_norm, na_k_norm, na_rpb,
           ga_q_norm, ga_k_norm, ml_i_bias, ml_f_bias, ml_norm_g, w_branch, w_out):
    b, s, d = x.shape
    depth = w_in.shape[0]
    hw = BRANCH_W
    cos_t, sin_t = _rope_lane_tables(s)
    w_main = _main_columns(w_in).astype(BF16)
    w_small = jnp.pad(jnp.concatenate([w_in[:, :, o:o + 8] for o in _SMALL_SRC], axis=2),
                      ((0, 0), (0, 0), (0, LANES - 8 * len(_SMALL_SRC)))).astype(BF16)
    conv8 = jnp.pad(conv_a.astype(F32), ((0, 0), (0, SUBLANES - DN_CONV), (0, 0)))
    dn_lp = _lane_tiles([(_L_AA, dn_a_log), (_L_AA, dn_dt_bias)])
    ml_lp = _lane_tiles([(_L_DI, ml_i_bias), (_L_DF, ml_f_bias)])
    na_bias = _na_bias_table(na_rpb, s // GRID_W)
    na_qg = jnp.tile(na_q_norm, (1, 2)).reshape(depth, 1, 2 * NA_DH)
    na_kg = jnp.tile(na_k_norm, (1, 2)).reshape(depth, 1, 2 * NA_DH)
    ga_qg = ga_q_norm.reshape(depth, 1, GA_DH)
    ga_kg = ga_k_norm.reshape(depth, 1, GA_DH)
    wb_bf, wo_bf = w_branch.astype(BF16), w_out.astype(BF16)

    x2 = x.reshape(b * s, d)
    for l in range(depth):
        pm2, ps2 = _inproj(x2, norm_g[l].reshape(1, d), w_main, w_small, l)
        pm3 = pm2.reshape(b, s, N_MAIN)
        ps3 = ps2.reshape(b, s, LANES)
        o_af, o_ab, h_df, h_db = _scan_and_mlstm(_dn_pre(pm3, ps3, conv8[l], dn_lp[l]), pm3, ps3, ml_lp[l])
        yb = _natten(pm3, na_bias, na_qg[l], na_kg[l], l)
        yc = _gqa(pm3, cos_t, sin_t, ga_qg[l], ga_kg[l])
        x2 = _merge(x2, o_af.reshape(b * s, hw), o_ab.reshape(b * s, hw), h_df.reshape(b * s, hw),
                    h_db.reshape(b * s, hw), yb.reshape(b * s, hw), yc.reshape(b * s, hw), pm2,
                    dn_norm_g[l].reshape(1, LANES), ml_norm_g[l].reshape(1, LANES), wb_bf, wo_bf, l)
    return x2.reshape(b, s, d)
```
